```python
import math
import jax, jax.numpy as jnp
from jax import lax
import numpy as np

D_MODEL = 1024
BATCH = 8
SEQ = 8192
DEPTH = 1

CHUNK = 64
Q_BLOCK = 128
EPS = 1e-6
D_FF = 2816
N_MOD = 9

GDN_HEADS = 4
GDN_DK = 128
GDN_DV = 128
CONV_K = 4

MLA_HEADS = 4
MLA_NOPE = 128
MLA_ROPE = 64
MLA_V = 128
MLA_Q_LORA = 384
MLA_KV_LORA = 256
ROPE_BASE = 10000.0

GDN_WIDTH = GDN_HEADS * GDN_DV
MLA_WIDTH = MLA_HEADS * MLA_V
MIX_WIDTH = GDN_WIDTH + MLA_WIDTH

IN_SPLITS = (GDN_HEADS * GDN_DK,
             GDN_HEADS * GDN_DK,
             GDN_WIDTH,
             GDN_WIDTH,
             GDN_HEADS,
             GDN_HEADS,
             MLA_Q_LORA,
             MLA_KV_LORA,
             MLA_ROPE)
N_IN = sum(IN_SPLITS)
IN_OFFSETS = tuple(int(o) for o in np.cumsum(IN_SPLITS)[:-1])

kernel_name = "hybrid_gdn_mla_macaron_adaln_block"


def _rms(x, w=None):
    xf = x.astype(jnp.float32)
    y = xf * lax.rsqrt(jnp.mean(xf * xf, axis=-1, keepdims=True) + EPS)
    if w is not None:
        y = y * w.astype(jnp.float32)
    return y.astype(x.dtype)


def _l2n(x):
    return x * lax.rsqrt(jnp.sum(x * x, axis=-1, keepdims=True) + EPS)


def _modulate(x, shift, scale):
    return _rms(x) * (1.0 + scale[:, None, :]) + shift[:, None, :]


def _swiglu(h, w_in, w_out):
    gate, up = jnp.split(h @ w_in, 2, axis=-1)
    return (jax.nn.silu(gate) * up) @ w_out


def _rope(x, cos, sin):
    x1, x2 = jnp.split(x, 2, axis=-1)
    return jnp.concatenate([x1 * cos - x2 * sin, x2 * cos + x1 * sin], axis=-1)


def _causal_conv(x, w):
    return lax.conv_general_dilated(
        x, w[:, None, :].astype(x.dtype), window_strides=(1,),
        padding=[(CONV_K - 1, 0)], dimension_numbers=("NWC", "WIO", "NWC"),
        feature_group_count=x.shape[-1])


def _gated_delta_rule(q, k, v, g, beta):
    B, S, H, _ = q.shape
    nc = S // CHUNK

    def to_chunks(t):
        return t.reshape(B, nc, CHUNK, H, t.shape[-1]).transpose(0, 3, 1, 2, 4)

    q, k, v = to_chunks(q), to_chunks(k), to_chunks(v)
    g = g.reshape(B, nc, CHUNK, H).transpose(0, 3, 1, 2)
    beta = beta.reshape(B, nc, CHUNK, H).transpose(0, 3, 1, 2)

    G = jnp.cumsum(g, axis=-1)
    idx = jnp.arange(CHUNK)
    incl = idx[:, None] >= idx[None, :]
    strict = idx[:, None] > idx[None, :]
    decay = jnp.exp(jnp.where(incl, G[..., :, None] - G[..., None, :], -jnp.inf))

    kk = jnp.einsum('bhncd,bhnsd->bhncs', k, k)
    A = jnp.where(strict, beta[..., :, None] * kk * decay, 0.0)
    M = A + jnp.eye(CHUNK, dtype=A.dtype)
    rhs = jnp.concatenate([v * beta[..., None], k * (beta * jnp.exp(G))[..., None]], axis=-1)
    W = lax.linalg.triangular_solve(M, rhs, left_side=True, lower=True, unit_diagonal=True)
    u, wk = W[..., :GDN_DV], W[..., GDN_DV:]

    qk = jnp.einsum('bhncd,bhnsd->bhncs', q, k) * decay
    q_dec = q * jnp.exp(G)[..., None]
    k_dec = k * jnp.exp(G[..., -1:] - G)[..., None]
    g_last = jnp.exp(G[..., -1])

    xs = tuple(jnp.moveaxis(t, 2, 0) for t in (u, wk, q_dec, k_dec, qk, g_last))

    def step(state, inp):
        u_c, wk_c, qd_c, kd_c, qk_c, gl_c = inp
        v_new = u_c - jnp.einsum('bhck,bhkv->bhcv', wk_c, state)
        o_c = jnp.einsum('bhck,bhkv->bhcv', qd_c, state) + jnp.einsum('bhcs,bhsv->bhcv', qk_c, v_new)
        state = state * gl_c[..., None, None] + jnp.einsum('bhck,bhcv->bhkv', kd_c, v_new)
        return state, o_c

    s0 = jnp.zeros((B, H, GDN_DK, GDN_DV), jnp.float32)
    _, o = lax.scan(step, s0, xs)
    return o.transpose(1, 0, 3, 2, 4).reshape(B, S, H, GDN_DV)


def _hybrid_mixer(h, cos, sin, w_in, gdn_conv_w, gdn_a_log, gdn_dt_bias, gdn_norm_w,
                  mla_q_norm_w, mla_w_uq, mla_kv_norm_w, mla_w_ukv,
                  qkn_q_nope, qkn_q_rope, qkn_k_nope, qkn_k_rope, mla_out_norm_w, w_out):
    B, S, _ = h.shape
    nb = S // Q_BLOCK
    proj = h @ w_in
    gq, gk, gv, gz, ga, gb, cq, ckv, kr = jnp.split(proj, IN_OFFSETS, axis=-1)

    qkv = jax.nn.silu(_causal_conv(jnp.concatenate([gq, gk, gv], axis=-1), gdn_conv_w))
    q_a = qkv[..., :GDN_HEADS * GDN_DK].reshape(B, S, GDN_HEADS, GDN_DK).astype(jnp.float32)
    k_a = qkv[..., GDN_HEADS * GDN_DK:2 * GDN_HEADS * GDN_DK].reshape(B, S, GDN_HEADS, GDN_DK).astype(jnp.float32)
    v_a = qkv[..., 2 * GDN_HEADS * GDN_DK:].reshape(B, S, GDN_HEADS, GDN_DV).astype(jnp.float32)
    q_a = _l2n(q_a) * (GDN_DK ** -0.5)
    k_a = _l2n(k_a)
    beta = jax.nn.sigmoid(gb.astype(jnp.float32))
    g = -jnp.exp(gdn_a_log.astype(jnp.float32)) * jax.nn.softplus(
        ga.astype(jnp.float32) + gdn_dt_bias.astype(jnp.float32))
    o_a = _gated_delta_rule(q_a, k_a, v_a, g, beta).astype(h.dtype)
    o_a = _rms(o_a, gdn_norm_w) * jax.nn.silu(gz.reshape(B, S, GDN_HEADS, GDN_DV))

    qf = (_rms(cq, mla_q_norm_w) @ mla_w_uq).reshape(B, S, MLA_HEADS, MLA_NOPE + MLA_ROPE)
    kvf = (_rms(ckv, mla_kv_norm_w) @ mla_w_ukv).reshape(B, S, MLA_HEADS, MLA_NOPE + MLA_V)
    q_nope, q_rope = qf[..., :MLA_NOPE], qf[..., MLA_NOPE:]
    k_nope, v_b = kvf[..., :MLA_NOPE], kvf[..., MLA_NOPE:]
    scale = (MLA_NOPE + MLA_ROPE) ** -0.5
    q_nope = _rms(q_nope, qkn_q_nope) * scale
    q_rope = _rope(_rms(q_rope, qkn_q_rope), cos[:, :, None], sin[:, :, None]) * scale
    k_nope = _rms(k_nope, qkn_k_nope)
    k_rope = _rope(_rms(kr, qkn_k_rope), cos, sin)

    k_chunk = jnp.arange(S) // CHUNK
    q_chunk_b = k_chunk.reshape(nb, Q_BLOCK)
    qn_b = q_nope.reshape(B, nb, Q_BLOCK, MLA_HEADS, MLA_NOPE).transpose(1, 0, 2, 3, 4)
    qr_b = q_rope.reshape(B, nb, Q_BLOCK, MLA_HEADS, MLA_ROPE).transpose(1, 0, 2, 3, 4)

    def attend(blk):
        qn, qr, qc = blk
        s = (jnp.einsum('bqhd,bkhd->bhqk', qn, k_nope)
             + jnp.einsum('bqhd,bkd->bhqk', qr, k_rope)).astype(jnp.float32)
        s = jnp.where(qc[:, None] >= k_chunk[None, :], s, -jnp.inf)
        p = jax.nn.softmax(s, axis=-1).astype(v_b.dtype)
        return jnp.einsum('bhqk,bkhd->bqhd', p, v_b)

    o_b = lax.map(attend, (qn_b, qr_b, q_chunk_b))
    o_b = o_b.transpose(1, 0, 2, 3, 4).reshape(B, S, MLA_HEADS, MLA_V)
    o_b = _rms(o_b, mla_out_norm_w)

    mixed = jnp.concatenate([o_a.reshape(B, S, GDN_WIDTH), o_b.reshape(B, S, MLA_WIDTH)], axis=-1)
    return mixed @ w_out


def _fwd_setup_inputs(seed: int = 0) -> dict:
    key = jax.random.key(seed)
    ks = jax.random.split(key, 32)
    f32 = jnp.float32
    L = DEPTH

    def nrm(k, shape, fan_in, mult=1.0):
        return jax.random.normal(k, shape, f32) * (mult * fan_in ** -0.5)

    def gain(k, shape):
        return 1.0 + 0.02 * jax.random.normal(k, shape, f32)

    x = jax.random.normal(ks[0], (BATCH, SEQ, D_MODEL), f32)
    c = jax.random.normal(ks[1], (BATCH, D_MODEL), f32)
    offset = jax.random.randint(ks[2], (BATCH, 1), 0, 4096, dtype=jnp.int32)
    positions = (offset + jnp.arange(SEQ, dtype=jnp.int32)[None, :]).astype(jnp.int32)

    dt = jnp.exp(jax.random.uniform(ks[10], (L, GDN_HEADS), f32, math.log(1e-3), math.log(1e-1)))
    return {
        "x": x,
        "c": c,
        "positions": positions,
        "w_ada": nrm(ks[3], (L, D_MODEL, N_MOD * D_MODEL), D_MODEL, 0.5),
        "b_ada": 0.02 * jax.random.normal(ks[4], (L, N_MOD * D_MODEL), f32),
        "ffn1_w_in": nrm(ks[5], (L, D_MODEL, 2 * D_FF), D_MODEL),
        "ffn1_w_out": nrm(ks[6], (L, D_FF, D_MODEL), D_FF),
        "w_in": nrm(ks[7], (L, D_MODEL, N_IN), D_MODEL),
        "gdn_conv_w": nrm(ks[8], (L, CONV_K, 3 * GDN_WIDTH), CONV_K),
        "gdn_a_log": jnp.log(jax.random.uniform(ks[9], (L, GDN_HEADS), f32, 1.0, 16.0)),
        "gdn_dt_bias": dt + jnp.log(-jnp.expm1(-dt)),
        "gdn_norm_w": gain(ks[11], (L, GDN_DV)),
        "mla_q_norm_w": gain(ks[12], (L, MLA_Q_LORA)),
        "mla_w_uq": nrm(ks[13], (L, MLA_Q_LORA, MLA_HEADS * (MLA_NOPE + MLA_ROPE)), MLA_Q_LORA),
        "mla_kv_norm_w": gain(ks[14], (L, MLA_KV_LORA)),
        "mla_w_ukv": nrm(ks[15], (L, MLA_KV_LORA, MLA_HEADS * (MLA_NOPE + MLA_V)), MLA_KV_LORA),
        "qkn_q_nope": gain(ks[16], (L, MLA_NOPE)),
        "qkn_q_rope": gain(ks[17], (L, MLA_ROPE)),
        "qkn_k_nope": gain(ks[18], (L, MLA_NOPE)),
        "qkn_k_rope": gain(ks[19], (L, MLA_ROPE)),
        "mla_out_norm_w": gain(ks[20], (L, MLA_V)),
        "w_out": nrm(ks[21], (L, MIX_WIDTH, D_MODEL), MIX_WIDTH),
        "ffn2_w_in": nrm(ks[22], (L, D_MODEL, 2 * D_FF), D_MODEL),
        "ffn2_w_out": nrm(ks[23], (L, D_FF, D_MODEL), D_FF),
    }


def _fwd_reference(x, c, positions, w_ada, b_ada, ffn1_w_in, ffn1_w_out, w_in, gdn_conv_w,
              gdn_a_log, gdn_dt_bias, gdn_norm_w, mla_q_norm_w, mla_w_uq, mla_kv_norm_w,
              mla_w_ukv, qkn_q_nope, qkn_q_rope, qkn_k_nope, qkn_k_rope, mla_out_norm_w,
              w_out, ffn2_w_in, ffn2_w_out):
    half = MLA_ROPE // 2
    inv_freq = ROPE_BASE ** (-jnp.arange(half, dtype=jnp.float32) / half)
    ang = positions.astype(jnp.float32)[..., None] * inv_freq
    cos = jnp.cos(ang).astype(x.dtype)
    sin = jnp.sin(ang).astype(x.dtype)
    sc = jax.nn.silu(c)

    for l in range(DEPTH):
        mod = sc @ w_ada[l] + b_ada[l]
        sh1, s1, g1, sh2, s2, g2, sh3, s3, g3 = jnp.split(mod, N_MOD, axis=-1)
        h = _modulate(x, sh1, s1)
        x = x + 0.5 * g1[:, None, :] * _swiglu(h, ffn1_w_in[l], ffn1_w_out[l])
        h = _modulate(x, sh2, s2)
        y = _hybrid_mixer(h, cos, sin, w_in[l], gdn_conv_w[l], gdn_a_log[l], gdn_dt_bias[l],
                          gdn_norm_w[l], mla_q_norm_w[l], mla_w_uq[l], mla_kv_norm_w[l],
                          mla_w_ukv[l], qkn_q_nope[l], qkn_q_rope[l], qkn_k_nope[l],
                          qkn_k_rope[l], mla_out_norm_w[l], w_out[l])
        x = x + g2[:, None, :] * y
        h = _modulate(x, sh3, s3)
        x = x + 0.5 * g3[:, None, :] * _swiglu(h, ffn2_w_in[l], ffn2_w_out[l])
    return x


import jax as _jax
import jax.numpy as _jnp

TWIN_FORMAT = 'train_step'
FWD_PARAMS = ['x', 'c', 'positions', 'w_ada', 'b_ada', 'ffn1_w_in', 'ffn1_w_out', 'w_in', 'gdn_conv_w', 'gdn_a_log', 'gdn_dt_bias', 'gdn_norm_w', 'mla_q_norm_w', 'mla_w_uq', 'mla_kv_norm_w', 'mla_w_ukv', 'qkn_q_nope', 'qkn_q_rope', 'qkn_k_nope', 'qkn_k_rope', 'mla_out_norm_w', 'w_out', 'ffn2_w_in', 'ffn2_w_out']
TWIN_WEIGHTS = ['w_ada', 'b_ada', 'ffn1_w_in', 'ffn1_w_out', 'w_in', 'gdn_conv_w', 'gdn_a_log', 'gdn_dt_bias', 'gdn_norm_w', 'mla_q_norm_w', 'mla_w_uq', 'mla_kv_norm_w', 'mla_w_ukv', 'qkn_q_nope', 'qkn_q_rope', 'qkn_k_nope', 'qkn_k_rope', 'mla_out_norm_w', 'w_out', 'ffn2_w_in', 'ffn2_w_out']
TWIN_DIFF_INPUT = 'x'
TWIN_INPUTS = ['x', 'c', 'positions', 'w_ada', 'b_ada', 'ffn1_w_in', 'ffn1_w_out', 'w_in', 'gdn_conv_w', 'gdn_a_log', 'gdn_dt_bias', 'gdn_norm_w', 'mla_q_norm_w', 'mla_w_uq', 'mla_kv_norm_w', 'mla_w_ukv', 'qkn_q_nope', 'qkn_q_rope', 'qkn_k_nope', 'qkn_k_rope', 'mla_out_norm_w', 'w_out', 'ffn2_w_in', 'ffn2_w_out', 'loss_target', 'm_w_ada', 'm_b_ada', 'm_ffn1_w_in', 'm_ffn1_w_out', 'm_w_in', 'm_gdn_conv_w', 'm_gdn_a_log', 'm_gdn_dt_bias', 'm_gdn_norm_w', 'm_mla_q_norm_w', 'm_mla_w_uq', 'm_mla_kv_norm_w', 'm_mla_w_ukv', 'm_qkn_q_nope', 'm_qkn_q_rope', 'm_qkn_k_nope', 'm_qkn_k_rope', 'm_mla_out_norm_w', 'm_w_out', 'm_ffn2_w_in', 'm_ffn2_w_out', 'v_w_ada', 'v_b_ada', 'v_ffn1_w_in', 'v_ffn1_w_out', 'v_w_in', 'v_gdn_conv_w', 'v_gdn_a_log', 'v_gdn_dt_bias', 'v_gdn_norm_w', 'v_mla_q_norm_w', 'v_mla_w_uq', 'v_mla_kv_norm_w', 'v_mla_w_ukv', 'v_qkn_q_nope', 'v_qkn_q_rope', 'v_qkn_k_nope', 'v_qkn_k_rope', 'v_mla_out_norm_w', 'v_w_out', 'v_ffn2_w_in', 'v_ffn2_w_out']
TWIN_OUTPUTS = ['loss', 'grad_x', 'grad_w_ada', 'grad_b_ada', 'grad_ffn1_w_in', 'grad_ffn1_w_out', 'grad_w_in', 'grad_gdn_conv_w', 'grad_gdn_a_log', 'grad_gdn_dt_bias', 'grad_gdn_norm_w', 'grad_mla_q_norm_w', 'grad_mla_w_uq', 'grad_mla_kv_norm_w', 'grad_mla_w_ukv', 'grad_qkn_q_nope', 'grad_qkn_q_rope', 'grad_qkn_k_nope', 'grad_qkn_k_rope', 'grad_mla_out_norm_w', 'grad_w_out', 'grad_ffn2_w_in', 'grad_ffn2_w_out', 'delta_w_ada', 'delta_b_ada', 'delta_ffn1_w_in', 'delta_ffn1_w_out', 'delta_w_in', 'delta_gdn_conv_w', 'delta_gdn_a_log', 'delta_gdn_dt_bias', 'delta_gdn_norm_w', 'delta_mla_q_norm_w', 'delta_mla_w_uq', 'delta_mla_kv_norm_w', 'delta_mla_w_ukv', 'delta_qkn_q_nope', 'delta_qkn_q_rope', 'delta_qkn_k_nope', 'delta_qkn_k_rope', 'delta_mla_out_norm_w', 'delta_w_out', 'delta_ffn2_w_in', 'delta_ffn2_w_out', 'new_m_w_ada', 'new_m_b_ada', 'new_m_ffn1_w_in', 'new_m_ffn1_w_out', 'new_m_w_in', 'new_m_gdn_conv_w', 'new_m_gdn_a_log', 'new_m_gdn_dt_bias', 'new_m_gdn_norm_w', 'new_m_mla_q_norm_w', 'new_m_mla_w_uq', 'new_m_mla_kv_norm_w', 'new_m_mla_w_ukv', 'new_m_qkn_q_nope', 'new_m_qkn_q_rope', 'new_m_qkn_k_nope', 'new_m_qkn_k_rope', 'new_m_mla_out_norm_w', 'new_m_w_out', 'new_m_ffn2_w_in', 'new_m_ffn2_w_out', 'new_v_w_ada', 'new_v_b_ada', 'new_v_ffn1_w_in', 'new_v_ffn1_w_out', 'new_v_w_in', 'new_v_gdn_conv_w', 'new_v_gdn_a_log', 'new_v_gdn_dt_bias', 'new_v_gdn_norm_w', 'new_v_mla_q_norm_w', 'new_v_mla_w_uq', 'new_v_mla_kv_norm_w', 'new_v_mla_w_ukv', 'new_v_qkn_q_nope', 'new_v_qkn_q_rope', 'new_v_qkn_k_nope', 'new_v_qkn_k_rope', 'new_v_mla_out_norm_w', 'new_v_w_out', 'new_v_ffn2_w_in', 'new_v_ffn2_w_out']
TWIN_LEAF_KINDS = {'loss': 'loss', 'grad_x': 'grad_x', 'grad_w_ada': 'grad_w', 'grad_b_ada': 'grad_w', 'grad_ffn1_w_in': 'grad_w', 'grad_ffn1_w_out': 'grad_w', 'grad_w_in': 'grad_w', 'grad_gdn_conv_w': 'grad_w', 'grad_gdn_a_log': 'grad_w', 'grad_gdn_dt_bias': 'grad_w', 'grad_gdn_norm_w': 'grad_w', 'grad_mla_q_norm_w': 'grad_w', 'grad_mla_w_uq': 'grad_w', 'grad_mla_kv_norm_w': 'grad_w', 'grad_mla_w_ukv': 'grad_w', 'grad_qkn_q_nope': 'grad_w', 'grad_qkn_q_rope': 'grad_w', 'grad_qkn_k_nope': 'grad_w', 'grad_qkn_k_rope': 'grad_w', 'grad_mla_out_norm_w': 'grad_w', 'grad_w_out': 'grad_w', 'grad_ffn2_w_in': 'grad_w', 'grad_ffn2_w_out': 'grad_w', 'delta_w_ada': 'delta_w', 'delta_b_ada': 'delta_w', 'delta_ffn1_w_in': 'delta_w', 'delta_ffn1_w_out': 'delta_w', 'delta_w_in': 'delta_w', 'delta_gdn_conv_w': 'delta_w', 'delta_gdn_a_log': 'delta_w', 'delta_gdn_dt_bias': 'delta_w', 'delta_gdn_norm_w': 'delta_w', 'delta_mla_q_norm_w': 'delta_w', 'delta_mla_w_uq': 'delta_w', 'delta_mla_kv_norm_w': 'delta_w', 'delta_mla_w_ukv': 'delta_w', 'delta_qkn_q_nope': 'delta_w', 'delta_qkn_q_rope': 'delta_w', 'delta_qkn_k_nope': 'delta_w', 'delta_qkn_k_rope': 'delta_w', 'delta_mla_out_norm_w': 'delta_w', 'delta_w_out': 'delta_w', 'delta_ffn2_w_in': 'delta_w', 'delta_ffn2_w_out': 'delta_w', 'new_m_w_ada': 'new_m', 'new_m_b_ada': 'new_m', 'new_m_ffn1_w_in': 'new_m', 'new_m_ffn1_w_out': 'new_m', 'new_m_w_in': 'new_m', 'new_m_gdn_conv_w': 'new_m', 'new_m_gdn_a_log': 'new_m', 'new_m_gdn_dt_bias': 'new_m', 'new_m_gdn_norm_w': 'new_m', 'new_m_mla_q_norm_w': 'new_m', 'new_m_mla_w_uq': 'new_m', 'new_m_mla_kv_norm_w': 'new_m', 'new_m_mla_w_ukv': 'new_m', 'new_m_qkn_q_nope': 'new_m', 'new_m_qkn_q_rope': 'new_m', 'new_m_qkn_k_nope': 'new_m', 'new_m_qkn_k_rope': 'new_m', 'new_m_mla_out_norm_w': 'new_m', 'new_m_w_out': 'new_m', 'new_m_ffn2_w_in': 'new_m', 'new_m_ffn2_w_out': 'new_m', 'new_v_w_ada': 'new_v', 'new_v_b_ada': 'new_v', 'new_v_ffn1_w_in': 'new_v', 'new_v_ffn1_w_out': 'new_v', 'new_v_w_in': 'new_v', 'new_v_gdn_conv_w': 'new_v', 'new_v_gdn_a_log': 'new_v', 'new_v_gdn_dt_bias': 'new_v', 'new_v_gdn_norm_w': 'new_v', 'new_v_mla_q_norm_w': 'new_v', 'new_v_mla_w_uq': 'new_v', 'new_v_mla_kv_norm_w': 'new_v', 'new_v_mla_w_ukv': 'new_v', 'new_v_qkn_q_nope': 'new_v', 'new_v_qkn_q_rope': 'new_v', 'new_v_qkn_k_nope': 'new_v', 'new_v_qkn_k_rope': 'new_v', 'new_v_mla_out_norm_w': 'new_v', 'new_v_w_out': 'new_v', 'new_v_ffn2_w_in': 'new_v', 'new_v_ffn2_w_out': 'new_v'}


def _forward(args):
    return _fwd_reference(*[args[k] for k in FWD_PARAMS])


def _output_shape():
    def fwd():
        inp = _fwd_setup_inputs(0)
        return _fwd_reference(*[inp[k] for k in FWD_PARAMS])
    out = _jax.eval_shape(fwd)
    return out.shape, out.dtype

N_MICROBATCH = 1
ADAM_LR = 0.001
ADAM_B1 = 0.9
ADAM_B2 = 0.999
ADAM_EPS = 1e-08
ADAM_WD = 0.01
ADAM_STEP = 10
PER_EXAMPLE_BATCH_AXIS = {'x': 0, 'c': 0, 'positions': 0, 'loss_target': 0}
SHARED_INPUTS = []
_WEIGHT_DTYPES = {'w_ada': _jnp.float32, 'b_ada': _jnp.float32, 'ffn1_w_in': _jnp.float32, 'ffn1_w_out': _jnp.float32, 'w_in': _jnp.float32, 'gdn_conv_w': _jnp.float32, 'gdn_a_log': _jnp.float32, 'gdn_dt_bias': _jnp.float32, 'gdn_norm_w': _jnp.float32, 'mla_q_norm_w': _jnp.float32, 'mla_w_uq': _jnp.float32, 'mla_kv_norm_w': _jnp.float32, 'mla_w_ukv': _jnp.float32, 'qkn_q_nope': _jnp.float32, 'qkn_q_rope': _jnp.float32, 'qkn_k_nope': _jnp.float32, 'qkn_k_rope': _jnp.float32, 'mla_out_norm_w': _jnp.float32, 'w_out': _jnp.float32, 'ffn2_w_in': _jnp.float32, 'ffn2_w_out': _jnp.float32}
MOMENT_SCALE = {'w_ada': 1.780658e+00, 'b_ada': 3.440478e+00, 'ffn1_w_in': 4.303840e-02, 'ffn1_w_out': 7.919969e-02, 'w_in': 9.728215e-01, 'gdn_conv_w': 3.137143e-01, 'gdn_a_log': 4.288828e+00, 'gdn_dt_bias': 3.996746e+00, 'gdn_norm_w': 1.113422e+01, 'mla_q_norm_w': 8.558506e-02, 'mla_w_uq': 6.669244e-02, 'mla_kv_norm_w': 3.319581e+00, 'mla_w_ukv': 1.709840e+00, 'qkn_q_nope': 2.104665e-01, 'qkn_q_rope': 8.094456e-02, 'qkn_k_nope': 2.123059e-01, 'qkn_k_rope': 8.592879e-02, 'mla_out_norm_w': 2.265377e+01, 'w_out': 1.773946e+00, 'ffn2_w_in': 4.028387e-02, 'ffn2_w_out': 6.674234e-02}


def _to_microbatches(a, axis):
    t = _jnp.moveaxis(a, axis, 0)
    t = t.reshape((N_MICROBATCH, t.shape[0] // N_MICROBATCH) + t.shape[1:])
    return _jnp.moveaxis(t, 1, axis + 1)


def setup_inputs(seed: int = 0) -> dict:
    inp = _fwd_setup_inputs(seed)
    key = _jax.random.fold_in(_jax.random.key(seed), 7919)
    shape, _ = _output_shape()
    out = dict(inp)
    out["loss_target"] = _jax.random.normal(_jax.random.fold_in(key, 0), shape, _jnp.float32)
    for i, name in enumerate(TWIN_WEIGHTS):
        w = inp[name].astype(_jnp.float32)
        if MOMENT_SCALE is None:
            s = _jnp.sqrt(_jnp.mean(_jnp.square(w)) + 1e-30)
        else:
            s = MOMENT_SCALE[name]
        km, kv = _jax.random.split(_jax.random.fold_in(key, i + 1))
        out[name] = w
        out["m_" + name] = s * _jax.random.normal(km, w.shape, _jnp.float32)
        out["v_" + name] = (s * s) * _jax.random.uniform(kv, w.shape, _jnp.float32, 0.5, 1.5)
    if N_MICROBATCH > 1:
        for name, axis in PER_EXAMPLE_BATCH_AXIS.items():
            out[name] = _to_microbatches(out[name], axis)
    return {'x': out['x'], 'c': out['c'], 'positions': out['positions'], 'w_ada': out['w_ada'], 'b_ada': out['b_ada'], 'ffn1_w_in': out['ffn1_w_in'], 'ffn1_w_out': out['ffn1_w_out'], 'w_in': out['w_in'], 'gdn_conv_w': out['gdn_conv_w'], 'gdn_a_log': out['gdn_a_log'], 'gdn_dt_bias': out['gdn_dt_bias'], 'gdn_norm_w': out['gdn_norm_w'], 'mla_q_norm_w': out['mla_q_norm_w'], 'mla_w_uq': out['mla_w_uq'], 'mla_kv_norm_w': out['mla_kv_norm_w'], 'mla_w_ukv': out['mla_w_ukv'], 'qkn_q_nope': out['qkn_q_nope'], 'qkn_q_rope': out['qkn_q_rope'], 'qkn_k_nope': out['qkn_k_nope'], 'qkn_k_rope': out['qkn_k_rope'], 'mla_out_norm_w': out['mla_out_norm_w'], 'w_out': out['w_out'], 'ffn2_w_in': out['ffn2_w_in'], 'ffn2_w_out': out['ffn2_w_out'], 'loss_target': out['loss_target'], 'm_w_ada': out['m_w_ada'], 'm_b_ada': out['m_b_ada'], 'm_ffn1_w_in': out['m_ffn1_w_in'], 'm_ffn1_w_out': out['m_ffn1_w_out'], 'm_w_in': out['m_w_in'], 'm_gdn_conv_w': out['m_gdn_conv_w'], 'm_gdn_a_log': out['m_gdn_a_log'], 'm_gdn_dt_bias': out['m_gdn_dt_bias'], 'm_gdn_norm_w': out['m_gdn_norm_w'], 'm_mla_q_norm_w': out['m_mla_q_norm_w'], 'm_mla_w_uq': out['m_mla_w_uq'], 'm_mla_kv_norm_w': out['m_mla_kv_norm_w'], 'm_mla_w_ukv': out['m_mla_w_ukv'], 'm_qkn_q_nope': out['m_qkn_q_nope'], 'm_qkn_q_rope': out['m_qkn_q_rope'], 'm_qkn_k_nope': out['m_qkn_k_nope'], 'm_qkn_k_rope': out['m_qkn_k_rope'], 'm_mla_out_norm_w': out['m_mla_out_norm_w'], 'm_w_out': out['m_w_out'], 'm_ffn2_w_in': out['m_ffn2_w_in'], 'm_ffn2_w_out': out['m_ffn2_w_out'], 'v_w_ada': out['v_w_ada'], 'v_b_ada': out['v_b_ada'], 'v_ffn1_w_in': out['v_ffn1_w_in'], 'v_ffn1_w_out': out['v_ffn1_w_out'], 'v_w_in': out['v_w_in'], 'v_gdn_conv_w': out['v_gdn_conv_w'], 'v_gdn_a_log': out['v_gdn_a_log'], 'v_gdn_dt_bias': out['v_gdn_dt_bias'], 'v_gdn_norm_w': out['v_gdn_norm_w'], 'v_mla_q_norm_w': out['v_mla_q_norm_w'], 'v_mla_w_uq': out['v_mla_w_uq'], 'v_mla_kv_norm_w': out['v_mla_kv_norm_w'], 'v_mla_w_ukv': out['v_mla_w_ukv'], 'v_qkn_q_nope': out['v_qkn_q_nope'], 'v_qkn_q_rope': out['v_qkn_q_rope'], 'v_qkn_k_nope': out['v_qkn_k_nope'], 'v_qkn_k_rope': out['v_qkn_k_rope'], 'v_mla_out_norm_w': out['v_mla_out_norm_w'], 'v_w_out': out['v_w_out'], 'v_ffn2_w_in': out['v_ffn2_w_in'], 'v_ffn2_w_out': out['v_ffn2_w_out']}


def _loss(weights, diff, rest, loss_target):
    with _jax.named_scope("forward"):
        args = {**rest, TWIN_DIFF_INPUT: diff, **{k: w.astype(_WEIGHT_DTYPES[k]) for k, w in weights.items()}}
        y = _forward(args)
    with _jax.named_scope("loss_head"):
        err = _jnp.square(y.astype(_jnp.float32) - loss_target)
        return 0.5 * _jnp.sum(_jnp.mean(err, axis=-1)) if err.ndim else 0.5 * err


def _adamw(w, g, m, v):
    m = ADAM_B1 * m + (1.0 - ADAM_B1) * g
    v = ADAM_B2 * v + (1.0 - ADAM_B2) * _jnp.square(g)
    m_hat = m / (1.0 - ADAM_B1 ** ADAM_STEP)
    v_hat = v / (1.0 - ADAM_B2 ** ADAM_STEP)
    delta = -ADAM_LR * (m_hat / (_jnp.sqrt(v_hat) + ADAM_EPS) + ADAM_WD * w)
    return delta, m, v


def reference(x, c, positions, w_ada, b_ada, ffn1_w_in, ffn1_w_out, w_in, gdn_conv_w, gdn_a_log, gdn_dt_bias, gdn_norm_w, mla_q_norm_w, mla_w_uq, mla_kv_norm_w, mla_w_ukv, qkn_q_nope, qkn_q_rope, qkn_k_nope, qkn_k_rope, mla_out_norm_w, w_out, ffn2_w_in, ffn2_w_out, loss_target, m_w_ada, m_b_ada, m_ffn1_w_in, m_ffn1_w_out, m_w_in, m_gdn_conv_w, m_gdn_a_log, m_gdn_dt_bias, m_gdn_norm_w, m_mla_q_norm_w, m_mla_w_uq, m_mla_kv_norm_w, m_mla_w_ukv, m_qkn_q_nope, m_qkn_q_rope, m_qkn_k_nope, m_qkn_k_rope, m_mla_out_norm_w, m_w_out, m_ffn2_w_in, m_ffn2_w_out, v_w_ada, v_b_ada, v_ffn1_w_in, v_ffn1_w_out, v_w_in, v_gdn_conv_w, v_gdn_a_log, v_gdn_dt_bias, v_gdn_norm_w, v_mla_q_norm_w, v_mla_w_uq, v_mla_kv_norm_w, v_mla_w_ukv, v_qkn_q_nope, v_qkn_q_rope, v_qkn_k_nope, v_qkn_k_rope, v_mla_out_norm_w, v_w_out, v_ffn2_w_in, v_ffn2_w_out):
    given = dict(x=x, c=c, positions=positions, w_ada=w_ada, b_ada=b_ada, ffn1_w_in=ffn1_w_in, ffn1_w_out=ffn1_w_out, w_in=w_in, gdn_conv_w=gdn_conv_w, gdn_a_log=gdn_a_log, gdn_dt_bias=gdn_dt_bias, gdn_norm_w=gdn_norm_w, mla_q_norm_w=mla_q_norm_w, mla_w_uq=mla_w_uq, mla_kv_norm_w=mla_kv_norm_w, mla_w_ukv=mla_w_ukv, qkn_q_nope=qkn_q_nope, qkn_q_rope=qkn_q_rope, qkn_k_nope=qkn_k_nope, qkn_k_rope=qkn_k_rope, mla_out_norm_w=mla_out_norm_w, w_out=w_out, ffn2_w_in=ffn2_w_in, ffn2_w_out=ffn2_w_out, loss_target=loss_target, m_w_ada=m_w_ada, m_b_ada=m_b_ada, m_ffn1_w_in=m_ffn1_w_in, m_ffn1_w_out=m_ffn1_w_out, m_w_in=m_w_in, m_gdn_conv_w=m_gdn_conv_w, m_gdn_a_log=m_gdn_a_log, m_gdn_dt_bias=m_gdn_dt_bias, m_gdn_norm_w=m_gdn_norm_w, m_mla_q_norm_w=m_mla_q_norm_w, m_mla_w_uq=m_mla_w_uq, m_mla_kv_norm_w=m_mla_kv_norm_w, m_mla_w_ukv=m_mla_w_ukv, m_qkn_q_nope=m_qkn_q_nope, m_qkn_q_rope=m_qkn_q_rope, m_qkn_k_nope=m_qkn_k_nope, m_qkn_k_rope=m_qkn_k_rope, m_mla_out_norm_w=m_mla_out_norm_w, m_w_out=m_w_out, m_ffn2_w_in=m_ffn2_w_in, m_ffn2_w_out=m_ffn2_w_out, v_w_ada=v_w_ada, v_b_ada=v_b_ada, v_ffn1_w_in=v_ffn1_w_in, v_ffn1_w_out=v_ffn1_w_out, v_w_in=v_w_in, v_gdn_conv_w=v_gdn_conv_w, v_gdn_a_log=v_gdn_a_log, v_gdn_dt_bias=v_gdn_dt_bias, v_gdn_norm_w=v_gdn_norm_w, v_mla_q_norm_w=v_mla_q_norm_w, v_mla_w_uq=v_mla_w_uq, v_mla_kv_norm_w=v_mla_kv_norm_w, v_mla_w_ukv=v_mla_w_ukv, v_qkn_q_nope=v_qkn_q_nope, v_qkn_q_rope=v_qkn_q_rope, v_qkn_k_nope=v_qkn_k_nope, v_qkn_k_rope=v_qkn_k_rope, v_mla_out_norm_w=v_mla_out_norm_w, v_w_out=v_w_out, v_ffn2_w_in=v_ffn2_w_in, v_ffn2_w_out=v_ffn2_w_out)
    weights = {n: given[n] for n in TWIN_WEIGHTS}
    shared = {n: given[n] for n in SHARED_INPUTS}
    per_example = {n: given[n] for n in ['x', 'c', 'positions']}
    grad_fn = _jax.value_and_grad(_loss, argnums=(0, 1))

    def one_microbatch(ex, loss_target):
        ex = dict(ex)
        diff = ex.pop(TWIN_DIFF_INPUT)
        return grad_fn(weights, diff, {**shared, **ex}, loss_target)

    if N_MICROBATCH == 1:
        loss, (grad_w, grad_x) = one_microbatch(per_example, given["loss_target"])
    else:
        def body(carry, xs):
            loss_sum, grad_sum = carry
            l_k, (gw_k, gx_k) = one_microbatch(xs[0], xs[1])
            with _jax.named_scope("update"):
                return (loss_sum + l_k, _jax.tree.map(_jnp.add, grad_sum, gw_k)), gx_k

        init = (_jnp.zeros((), _jnp.float32), _jax.tree.map(_jnp.zeros_like, weights))
        (loss, grad_w), grad_x = _jax.lax.scan(body, init, (per_example, given["loss_target"]))
    with _jax.named_scope("update"):
        delta_w, new_m, new_v = {}, {}, {}
        for n in TWIN_WEIGHTS:
            delta_w[n], new_m[n], new_v[n] = _adamw(weights[n], grad_w[n], given["m_" + n], given["v_" + n])
    return (loss, grad_x, *[grad_w[n] for n in TWIN_WEIGHTS], *[delta_w[n] for n in TWIN_WEIGHTS],
            *[new_m[n] for n in TWIN_WEIGHTS], *[new_v[n] for n in TWIN_WEIGHTS])
```

```python
import functools

import jax
import jax.numpy as jnp
from jax import lax
from jax.experimental import pallas as pl
from jax.experimental.pallas import tpu as pltpu

F32 = jnp.float32
BF16 = jnp.bfloat16
HI = lax.Precision.HIGHEST
MESH = pl.DeviceIdType.MESH

EPS = 1e-6
CHUNK = 64
D_FF = 2816
GDN_HEADS = 4
HEAD = 128
MLA_HEADS = 4
MLA_ROPE = 64
MLA_Q_LORA = 384
MLA_KV_LORA = 256
QK_PAD = 256
ATT_SCALE = (HEAD + MLA_ROPE) ** -0.5
N_PROJ = 3072

ADAM_LR, ADAM_B1, ADAM_B2, ADAM_EPS, ADAM_WD, ADAM_STEP = 0.001, 0.9, 0.999, 1e-08, 0.01, 10

LANES = 128
SUBLANES = 8
VMEM_LIMIT = 56 * 2 ** 20


def _params(sem=None):
    return pltpu.CompilerParams(dimension_semantics=sem, vmem_limit_bytes=VMEM_LIMIT)


def _pick(n, cap, align):
    best = None
    d = align
    while d <= min(n, cap):
        if n % d == 0:
            best = d
        d += align
    return best if best is not None else n


def _iota(shape, dim):
    return lax.broadcasted_iota(jnp.int32, shape, dim)


def _rowcall(fn, rows, params, out_rows, out_accs, *, tile, name):
    rows = [r if isinstance(r, tuple) else (r, r.shape[1], 0) for r in rows]
    s = rows[0][0].shape[0]
    t = min(tile, s)
    n = s // t
    n_in = len(rows) + len(params)
    n_row_out = len(out_rows)

    in_specs = [pl.BlockSpec((t, w), functools.partial(lambda i, b: (i, b), b=b)) for (_, w, b) in rows]
    in_specs += [pl.BlockSpec(p.shape, lambda i: (0, 0)) for p in params]
    out_shape = [jax.ShapeDtypeStruct((s, w), dt) for (w, dt) in out_rows]
    out_shape += [jax.ShapeDtypeStruct(shape, F32) for shape in out_accs]
    out_specs = [pl.BlockSpec((t, w), lambda i: (i, 0)) for (w, _) in out_rows]
    out_specs += [pl.BlockSpec(shape, lambda i: (0, 0)) for shape in out_accs]

    def body(*refs):
        ins = refs[:n_in]
        outs = refs[n_in:]
        i = pl.program_id(0)
        vals = [r[...] for r in ins]
        row_outs, acc_outs = fn(vals[:len(rows)], vals[len(rows):])
        for r, v in zip(outs[:n_row_out], row_outs):
            r[...] = v.astype(r.dtype)
        if out_accs:
            @pl.when(i == 0)
            def _():
                for r in outs[n_row_out:]:
                    r[...] = jnp.zeros(r.shape, F32)
            for r, v in zip(outs[n_row_out:], acc_outs):
                r[...] += v

    res = pl.pallas_call(
        body, name=name, grid=(n,), in_specs=in_specs, out_specs=out_specs, out_shape=out_shape,
        compiler_params=_params(("arbitrary",) if out_accs else ("parallel",)),
    )(*[r[0] for r in rows], *params)
    return list(res)


def _mm(a, b, mode, out_dtype, name, hi=False):
    if mode == "nn":
        (m, k), (_, n) = a.shape, b.shape
        dims = (((1,), (0,)), ((), ()))
    elif mode == "nt":
        (m, k), (n, _) = a.shape, b.shape
        dims = (((1,), (1,)), ((), ()))
    else:
        (k, m), (_, n) = a.shape, b.shape
        dims = (((0,), (0,)), ((), ()))
    tm = _pick(m, 512, LANES if mode == "tn" else 16)
    tn = _pick(n, 512, LANES)
    tk = _pick(k, 512 if mode == "tn" else 1024, LANES)
    nk = k // tk
    if mode == "nn":
        a_spec = pl.BlockSpec((tm, tk), lambda i, j, kk: (i, kk))
        b_spec = pl.BlockSpec((tk, tn), lambda i, j, kk: (kk, j))
    elif mode == "nt":
        a_spec = pl.BlockSpec((tm, tk), lambda i, j, kk: (i, kk))
        b_spec = pl.BlockSpec((tn, tk), lambda i, j, kk: (j, kk))
    else:
        a_spec = pl.BlockSpec((tk, tm), lambda i, j, kk: (kk, i))
        b_spec = pl.BlockSpec((tk, tn), lambda i, j, kk: (kk, j))

    def body(a_ref, b_ref, o_ref, acc_ref):
        kk = pl.program_id(2)

        @pl.when(kk == 0)
        def _():
            acc_ref[...] = jnp.zeros(acc_ref.shape, F32)

        av, bv = a_ref[...], b_ref[...]
        if hi:
            acc_ref[...] += lax.dot_general(av, bv, dims, precision=HI, preferred_element_type=F32)
        else:
            acc_ref[...] += lax.dot_general(av.astype(BF16), bv.astype(BF16), dims,
                                            preferred_element_type=F32)

        @pl.when(kk == nk - 1)
        def _():
            o_ref[...] = acc_ref[...].astype(o_ref.dtype)

    return pl.pallas_call(
        body, name=name, grid=(m // tm, n // tn, nk),
        in_specs=[a_spec, b_spec],
        out_specs=pl.BlockSpec((tm, tn), lambda i, j, kk: (i, j)),
        out_shape=jax.ShapeDtypeStruct((m, n), out_dtype),
        scratch_shapes=[pltpu.VMEM((tm, tn), F32)],
        compiler_params=_params(("parallel", "parallel", "arbitrary")),
    )(a, b)


def _rms(x, w=None, n=None):
    n = x.shape[-1] if n is None else n
    y = x * lax.rsqrt(jnp.sum(x * x, axis=-1, keepdims=True) * (1.0 / n) + EPS)
    return y if w is None else y * w


def _silu(x):
    return x * jax.nn.sigmoid(x)


def _softplus(x):
    return jnp.maximum(x, 0.0) + jnp.log1p(jnp.exp(-jnp.abs(x)))


def _split(x, widths):
    out, o = [], 0
    for w in widths:
        out.append(x[:, o:o + w])
        o += w
    return out


def _modulate(x, s, sh):
    return _rms(x) * (1.0 + s) + sh


def _rope_rot(x):
    r, c = _iota((LANES, LANES), 0), _iota((LANES, LANES), 1)
    half = MLA_ROPE // 2
    perm = (((r < half) & (c == r + half)) | ((r >= half) & (r < MLA_ROPE) & (c == r - half))).astype(F32)
    return jnp.dot(x, perm, precision=HI, preferred_element_type=F32)


def _rope(x, cos2, sin2):
    return x * cos2 + _rope_rot(x) * sin2


def _gdn_prep_core(qkv_parts, gab, a_log, dt_bias):
    act = [_silu(p) for p in qkv_parts]
    qs = [p * lax.rsqrt(jnp.sum(p * p, -1, keepdims=True) + EPS) * (HEAD ** -0.5) for p in act[:4]]
    ks = [p * lax.rsqrt(jnp.sum(p * p, -1, keepdims=True) + EPS) for p in act[4:8]]
    lane = _iota(gab.shape, 1)
    g = -jnp.exp(a_log) * _softplus(gab + dt_bias)
    beta = jax.nn.sigmoid(gab)
    gb = jnp.where(lane < GDN_HEADS, g, jnp.where(lane < 2 * GDN_HEADS, beta, 0.0))
    return (jnp.concatenate(qs, 1), jnp.concatenate(ks, 1), jnp.concatenate(act[8:], 1), gb)


def _mla_prep_core(cq, ckv, kr, cos2, sin2, wq, wkv, wkr):
    cqn = _rms(cq, wq)
    ckvn = _rms(ckv, wkv)
    k_rope = _rope(_rms(kr, wkr, MLA_ROPE), cos2, sin2)
    return cqn, ckvn, k_rope


def _qk_prep_core(qn_parts, qr_parts, kn_parts, v_parts, k_rope, cos2, sin2, wqn, wqr, wkn):
    qs, ks = [], []
    for h in range(MLA_HEADS):
        qn = _rms(qn_parts[h], wqn) * ATT_SCALE
        qr = _rope(_rms(qr_parts[h], wqr, MLA_ROPE), cos2, sin2) * ATT_SCALE
        qs += [qn, qr]
        ks += [_rms(kn_parts[h], wkn), k_rope]
    return jnp.concatenate(qs, 1), jnp.concatenate(ks, 1), jnp.concatenate(v_parts, 1)


def _mix_post_core(o_parts, gz_parts, ob_parts, wn, won):
    oa = [_rms(o, wn) * _silu(z) for o, z in zip(o_parts, gz_parts)]
    ob = [_rms(o, won) for o in ob_parts]
    return jnp.concatenate(oa + ob, 1)


CONV_K = 4
HALO = SUBLANES


def _conv_fwd(proj, w8, name):
    s = proj.shape[0]
    c = w8.shape[1]
    t = min(256, s)
    n = s // t
    hb = t // HALO

    def body(x_ref, prev_ref, w_ref, o_ref, buf):
        i = pl.program_id(0)
        buf[pl.ds(0, HALO), :] = jnp.where(i > 0, prev_ref[...], 0.0)
        buf[pl.ds(HALO, t), :] = x_ref[...]
        acc = jnp.zeros((t, c), F32)
        for k in range(CONV_K):
            acc = acc + w_ref[k:k + 1, :] * buf[pl.ds(HALO - (CONV_K - 1) + k, t), :]
        o_ref[...] = acc

    return pl.pallas_call(
        body, name=name, grid=(n,),
        in_specs=[pl.BlockSpec((t, c), lambda i: (i, 0)),
                  pl.BlockSpec((HALO, c), lambda i: (jnp.maximum(i * hb - 1, 0), 0)),
                  pl.BlockSpec(w8.shape, lambda i: (0, 0))],
        out_specs=pl.BlockSpec((t, c), lambda i: (i, 0)),
        out_shape=jax.ShapeDtypeStruct((s, c), F32),
        scratch_shapes=[pltpu.VMEM((t + HALO, c), F32)],
        compiler_params=_params(("parallel",)),
    )(proj, proj, w8)


def _conv_bwd(proj, dy, w8, name):
    s = proj.shape[0]
    c = w8.shape[1]
    t = min(256, s)
    n = s // t
    hb = t // HALO

    def body(x_ref, prev_ref, dy_ref, next_ref, w_ref, dx_ref, dw_ref, bufx, bufd):
        i = pl.program_id(0)
        bufx[pl.ds(0, HALO), :] = jnp.where(i > 0, prev_ref[...], 0.0)
        bufx[pl.ds(HALO, t), :] = x_ref[...]
        bufd[pl.ds(0, t), :] = dy_ref[...]
        bufd[pl.ds(t, HALO), :] = jnp.where(i < n - 1, next_ref[...], 0.0)

        @pl.when(i == 0)
        def _():
            dw_ref[...] = jnp.zeros(dw_ref.shape, F32)

        dyv = dy_ref[...]
        acc = jnp.zeros((t, c), F32)
        for k in range(CONV_K):
            acc = acc + w_ref[k:k + 1, :] * bufd[pl.ds(CONV_K - 1 - k, t), :]
            dw_ref[k:k + 1, :] += jnp.sum(dyv * bufx[pl.ds(HALO - (CONV_K - 1) + k, t), :], axis=0, keepdims=True)
        dx_ref[...] = acc

    return pl.pallas_call(
        body, name=name, grid=(n,),
        in_specs=[pl.BlockSpec((t, c), lambda i: (i, 0)),
                  pl.BlockSpec((HALO, c), lambda i: (jnp.maximum(i * hb - 1, 0), 0)),
                  pl.BlockSpec((t, c), lambda i: (i, 0)),
                  pl.BlockSpec((HALO, c), lambda i: (jnp.minimum((i + 1) * hb, s // HALO - 1), 0)),
                  pl.BlockSpec(w8.shape, lambda i: (0, 0))],
        out_specs=[pl.BlockSpec((t, c), lambda i: (i, 0)), pl.BlockSpec(w8.shape, lambda i: (0, 0))],
        out_shape=[jax.ShapeDtypeStruct((s, c), F32), jax.ShapeDtypeStruct(w8.shape, F32)],
        scratch_shapes=[pltpu.VMEM((t + HALO, c), F32), pltpu.VMEM((t + HALO, c), F32)],
        compiler_params=_params(("arbitrary",)),
    )(proj, proj, dy, dy, w8)


def _dot(a, b):
    return jnp.dot(a, b, precision=HI, preferred_element_type=F32)


def _dot_nt(a, b):
    return lax.dot_general(a, b, (((1,), (1,)), ((), ())), precision=HI, preferred_element_type=F32)


def _dot_tn(a, b):
    return lax.dot_general(a, b, (((0,), (0,)), ((), ())), precision=HI, preferred_element_type=F32)


def _unit_lower_solve(a, rhs):
    c = a.shape[0]
    ri, ci = _iota((c, c), 0), _iota((c, c), 1)
    same = (ri // 16) == (ci // 16)
    eye = (ri == ci).astype(F32)
    x = jnp.where(same, -a, 0.0)
    a_off = jnp.where(same, 0.0, a)
    t = eye + x
    p = x
    for _ in range(3):
        p = _dot(p, p)
        t = t + _dot(t, p)
    w = _dot(t, rhs)
    for _ in range(3):
        w = _dot(t, rhs - _dot(a_off, w))
    return w


def _gdn_chunk(state, q, k, v, gcol, bcol):
    c = q.shape[0]
    ri, ci = _iota((c, c), 0), _iota((c, c), 1)
    incl = ri >= ci
    strict = ri > ci
    tril = incl.astype(F32)
    eye = ri == ci
    ones = jnp.ones((c, c), F32)
    g_cc = _dot(tril, jnp.broadcast_to(gcol, (c, c)))
    g_row = _dot(ones, jnp.where(eye, g_cc, 0.0))
    g_cl = _dot(tril, jnp.broadcast_to(gcol, (c, HEAD)))
    g_last = jnp.sum(jnp.broadcast_to(gcol, (c, HEAD)), axis=0, keepdims=True)
    decay = jnp.where(incl, jnp.exp(jnp.where(incl, g_cc - g_row, 0.0)), 0.0)
    kk = _dot_nt(k, k)
    a = jnp.where(strict, bcol * kk * decay, 0.0)
    e_g = jnp.exp(g_cl)
    u = _unit_lower_solve(a, v * bcol)
    wk = _unit_lower_solve(a, k * (bcol * e_g))
    qk = _dot_nt(q, k) * decay
    q_dec = q * e_g
    k_dec = k * jnp.exp(g_last - g_cl)
    v_new = u - _dot(wk, state)
    o = _dot(q_dec, state) + _dot(qk, v_new)
    state_new = state * jnp.exp(g_last) + _dot_tn(k_dec, v_new)
    return state_new, o


def _gdn_chunk_all(states, q_parts, k_parts, v_parts, gb):
    lane = _iota(gb.shape, 1)
    new_states, outs = [], []
    for h in range(GDN_HEADS):
        gcol = jnp.sum(jnp.where(lane == h, gb, 0.0), axis=1, keepdims=True)
        bcol = jnp.sum(jnp.where(lane == GDN_HEADS + h, gb, 0.0), axis=1, keepdims=True)
        s_new, o = _gdn_chunk(states[h], q_parts[h], k_parts[h], v_parts[h], gcol, bcol)
        new_states.append(s_new)
        outs.append(o)
    return new_states, outs


def _gdn_fwd(q, k, v, gb, name):
    s = q.shape[0]
    nc = s // CHUNK
    w = GDN_HEADS * HEAD
    hw = [HEAD] * GDN_HEADS

    def body(q_ref, k_ref, v_ref, gb_ref, o_ref, st_ref, state):
        i = pl.program_id(0)

        @pl.when(i == 0)
        def _():
            state[...] = jnp.zeros(state.shape, F32)

        st_ref[...] = state[...]
        states = [state[h] for h in range(GDN_HEADS)]
        new_states, outs = _gdn_chunk_all(states, _split(q_ref[...], hw), _split(k_ref[...], hw),
                                          _split(v_ref[...], hw), gb_ref[...])
        for h in range(GDN_HEADS):
            state[h] = new_states[h]
        o_ref[...] = jnp.concatenate(outs, 1)

    row = pl.BlockSpec((CHUNK, w), lambda i: (i, 0))
    return pl.pallas_call(
        body, name=name, grid=(nc,),
        in_specs=[row, row, row, pl.BlockSpec((CHUNK, LANES), lambda i: (i, 0))],
        out_specs=[row, pl.BlockSpec((None, GDN_HEADS, HEAD, HEAD), lambda i: (i, 0, 0, 0))],
        out_shape=[jax.ShapeDtypeStruct((s, w), F32),
                   jax.ShapeDtypeStruct((nc, GDN_HEADS, HEAD, HEAD), F32)],
        scratch_shapes=[pltpu.VMEM((GDN_HEADS, HEAD, HEAD), F32)],
        compiler_params=_params(("arbitrary",)),
    )(q, k, v, gb)


def _gdn_bwd(q, k, v, gb, st, do, name):
    s = q.shape[0]
    nc = s // CHUNK
    w = GDN_HEADS * HEAD
    hw = [HEAD] * GDN_HEADS

    def body(q_ref, k_ref, v_ref, gb_ref, st_ref, do_ref, dq_ref, dk_ref, dv_ref, dgb_ref, dstate):
        i = pl.program_id(0)

        @pl.when(i == 0)
        def _():
            dstate[...] = jnp.zeros(dstate.shape, F32)

        states = [st_ref[h] for h in range(GDN_HEADS)]
        _, vjp = jax.vjp(_gdn_chunk_all, states, _split(q_ref[...], hw), _split(k_ref[...], hw),
                         _split(v_ref[...], hw), gb_ref[...])
        ds_in = [dstate[h] for h in range(GDN_HEADS)]
        d_states, d_q, d_k, d_v, d_gb = vjp((ds_in, _split(do_ref[...], hw)))
        for h in range(GDN_HEADS):
            dstate[h] = d_states[h]
        dq_ref[...] = jnp.concatenate(d_q, 1)
        dk_ref[...] = jnp.concatenate(d_k, 1)
        dv_ref[...] = jnp.concatenate(d_v, 1)
        dgb_ref[...] = d_gb

    rev = lambda i: (nc - 1 - i, 0)
    row = pl.BlockSpec((CHUNK, w), rev)
    gbs = pl.BlockSpec((CHUNK, LANES), rev)
    return pl.pallas_call(
        body, name=name, grid=(nc,),
        in_specs=[row, row, row, gbs,
                  pl.BlockSpec((None, GDN_HEADS, HEAD, HEAD), lambda i: (nc - 1 - i, 0, 0, 0)), row],
        out_specs=[row, row, row, gbs],
        out_shape=[jax.ShapeDtypeStruct((s, w), F32)] * 3 + [jax.ShapeDtypeStruct((s, LANES), F32)],
        scratch_shapes=[pltpu.VMEM((GDN_HEADS, HEAD, HEAD), F32)],
        compiler_params=_params(("arbitrary",)),
    )(q, k, v, gb, st, do)


def _chunk_mask(i, j, t):
    r = i * t + _iota((t, t), 0)
    c = j * t + _iota((t, t), 1)
    return (r // CHUNK) >= (c // CHUNK)


def _attn_fwd(q, k, v, name):
    s = q.shape[0]
    t = min(512, s)
    n = s // t

    def body(q_ref, k_ref, v_ref, o_ref, lse_ref, m_sc, l_sc, acc_sc):
        i, j = pl.program_id(1), pl.program_id(2)

        @pl.when(j == 0)
        def _():
            m_sc[...] = jnp.full(m_sc.shape, -jnp.inf, F32)
            l_sc[...] = jnp.zeros(l_sc.shape, F32)
            acc_sc[...] = jnp.zeros(acc_sc.shape, F32)

        @pl.when(j <= i)
        def _():
            sc = lax.dot_general(q_ref[...], k_ref[...], (((1,), (1,)), ((), ())), preferred_element_type=F32)
            sc = jnp.where(_chunk_mask(i, j, t), sc, -jnp.inf)
            m_prev = m_sc[:, :1]
            m_new = jnp.maximum(m_prev, jnp.max(sc, axis=1, keepdims=True))
            alpha = jnp.exp(m_prev - m_new)
            p = jnp.exp(sc - m_new)
            l_sc[...] = jnp.broadcast_to(alpha * l_sc[:, :1] + jnp.sum(p, axis=1, keepdims=True), l_sc.shape)
            acc_sc[...] = alpha * acc_sc[...] + jnp.dot(p.astype(BF16), v_ref[...], preferred_element_type=F32)
            m_sc[...] = jnp.broadcast_to(m_new, m_sc.shape)

        @pl.when(j == n - 1)
        def _():
            o_ref[...] = acc_sc[...] / l_sc[:, :1]
            lse_ref[...] = m_sc[...] + jnp.log(l_sc[...])

    return pl.pallas_call(
        body, name=name, grid=(MLA_HEADS, n, n),
        in_specs=[pl.BlockSpec((t, QK_PAD), lambda h, i, j: (i, h)),
                  pl.BlockSpec((t, QK_PAD), lambda h, i, j: (jnp.minimum(j, i), h)),
                  pl.BlockSpec((t, HEAD), lambda h, i, j: (jnp.minimum(j, i), h))],
        out_specs=[pl.BlockSpec((t, HEAD), lambda h, i, j: (i, h)),
                   pl.BlockSpec((None, t, LANES), lambda h, i, j: (h, i, 0))],
        out_shape=[jax.ShapeDtypeStruct((s, MLA_HEADS * HEAD), F32),
                   jax.ShapeDtypeStruct((MLA_HEADS, s, LANES), F32)],
        scratch_shapes=[pltpu.VMEM((t, LANES), F32), pltpu.VMEM((t, LANES), F32), pltpu.VMEM((t, HEAD), F32)],
        compiler_params=_params(("parallel", "parallel", "arbitrary")),
    )(q, k, v)


def _attn_probs(q_ref, k_ref, v_ref, o_ref, do_ref, lse_ref, i, j, t):
    sc = lax.dot_general(q_ref[...], k_ref[...], (((1,), (1,)), ((), ())), preferred_element_type=F32)
    p = jnp.where(_chunk_mask(i, j, t), jnp.exp(sc - lse_ref[:, :1]), 0.0)
    do = do_ref[...]
    dp = lax.dot_general(do.astype(BF16), v_ref[...], (((1,), (1,)), ((), ())), preferred_element_type=F32)
    dd = jnp.sum(do * o_ref[...], axis=1, keepdims=True)
    return p, p * (dp - dd)


def _attn_bwd_dq(q, k, v, o, do, lse, name):
    s = q.shape[0]
    t = min(512, s)
    n = s // t

    def body(q_ref, k_ref, v_ref, o_ref, do_ref, lse_ref, dq_ref, acc):
        i, j = pl.program_id(1), pl.program_id(2)

        @pl.when(j == 0)
        def _():
            acc[...] = jnp.zeros(acc.shape, F32)

        @pl.when(j <= i)
        def _():
            _, ds = _attn_probs(q_ref, k_ref, v_ref, o_ref, do_ref, lse_ref, i, j, t)
            acc[...] += jnp.dot(ds.astype(BF16), k_ref[...], preferred_element_type=F32)

        @pl.when(j == n - 1)
        def _():
            dq_ref[...] = acc[...]

    qrow = lambda h, i, j: (i, h)
    krow = lambda h, i, j: (jnp.minimum(j, i), h)
    return pl.pallas_call(
        body, name=name, grid=(MLA_HEADS, n, n),
        in_specs=[pl.BlockSpec((t, QK_PAD), qrow), pl.BlockSpec((t, QK_PAD), krow), pl.BlockSpec((t, HEAD), krow),
                  pl.BlockSpec((t, HEAD), qrow), pl.BlockSpec((t, HEAD), qrow),
                  pl.BlockSpec((None, t, LANES), lambda h, i, j: (h, i, 0))],
        out_specs=pl.BlockSpec((t, QK_PAD), qrow),
        out_shape=jax.ShapeDtypeStruct((s, MLA_HEADS * QK_PAD), F32),
        scratch_shapes=[pltpu.VMEM((t, QK_PAD), F32)],
        compiler_params=_params(("parallel", "parallel", "arbitrary")),
    )(q, k, v, o, do, lse)


def _attn_bwd_dkv(q, k, v, o, do, lse, name):
    s = q.shape[0]
    t = min(512, s)
    n = s // t

    def body(q_ref, k_ref, v_ref, o_ref, do_ref, lse_ref, dk_ref, dv_ref, dk_acc, dv_acc):
        j, i = pl.program_id(1), pl.program_id(2)

        @pl.when(i == 0)
        def _():
            dk_acc[...] = jnp.zeros(dk_acc.shape, F32)
            dv_acc[...] = jnp.zeros(dv_acc.shape, F32)

        @pl.when(i >= j)
        def _():
            p, ds = _attn_probs(q_ref, k_ref, v_ref, o_ref, do_ref, lse_ref, i, j, t)
            tn = (((0,), (0,)), ((), ()))
            dv_acc[...] += lax.dot_general(p.astype(BF16), do_ref[...].astype(BF16), tn, preferred_element_type=F32)
            dk_acc[...] += lax.dot_general(ds.astype(BF16), q_ref[...], tn, preferred_element_type=F32)

        @pl.when(i == n - 1)
        def _():
            dk_ref[...] = dk_acc[...]
            dv_ref[...] = dv_acc[...]

    qrow = lambda h, j, i: (jnp.maximum(i, j), h)
    krow = lambda h, j, i: (j, h)
    return pl.pallas_call(
        body, name=name, grid=(MLA_HEADS, n, n),
        in_specs=[pl.BlockSpec((t, QK_PAD), qrow), pl.BlockSpec((t, QK_PAD), krow), pl.BlockSpec((t, HEAD), krow),
                  pl.BlockSpec((t, HEAD), qrow), pl.BlockSpec((t, HEAD), qrow),
                  pl.BlockSpec((None, t, LANES), lambda h, j, i: (h, jnp.maximum(i, j), 0))],
        out_specs=[pl.BlockSpec((t, QK_PAD), krow), pl.BlockSpec((t, HEAD), krow)],
        out_shape=[jax.ShapeDtypeStruct((s, MLA_HEADS * QK_PAD), F32),
                   jax.ShapeDtypeStruct((s, MLA_HEADS * HEAD), F32)],
        scratch_shapes=[pltpu.VMEM((t, QK_PAD), F32), pltpu.VMEM((t, HEAD), F32)],
        compiler_params=_params(("parallel", "parallel", "arbitrary")),
    )(q, k, v, o, do, lse)


def _place():
    return lax.axis_index("x"), lax.axis_index("y"), lax.axis_index("c")


def _allgather8(x, name):
    r, c = x.shape

    def body(x_ref, out_ref, send_sems, recv_sems, local_sem):
        mx, my, mc = _place()
        me = 4 * mx + 2 * my + mc
        mine = pltpu.make_async_copy(x_ref, out_ref.at[me], local_sem)
        mine.start()
        copies = []
        for d in range(1, 8):
            px = 1 - mx if d & 4 else mx
            py = 1 - my if d & 2 else my
            pc = 1 - mc if d & 1 else mc
            cp = pltpu.make_async_remote_copy(
                src_ref=x_ref, dst_ref=out_ref.at[me], send_sem=send_sems.at[d - 1], recv_sem=recv_sems.at[d - 1],
                device_id=(px, py, pc), device_id_type=MESH)
            cp.start()
            copies.append(cp)
        for cp in copies:
            cp.wait()
        mine.wait()

    return pl.pallas_call(
        body, name=name,
        out_shape=jax.ShapeDtypeStruct((8, r, c), x.dtype),
        in_specs=[pl.BlockSpec(memory_space=pltpu.VMEM)],
        out_specs=pl.BlockSpec(memory_space=pltpu.VMEM),
        scratch_shapes=[pltpu.SemaphoreType.DMA((7,)), pltpu.SemaphoreType.DMA((7,)), pltpu.SemaphoreType.DMA],
        compiler_params=pltpu.CompilerParams(vmem_limit_bytes=VMEM_LIMIT),
    )(x)


def _allgather_chips(x, name):
    r, c = x.shape
    rh = r // 2

    def body(x_ref, out_ref, send_sems, recv_sems, local_sem):
        mx, my, mc = _place()
        j = 2 * mx + my
        chips = [(1 - mx, my), (mx, 1 - my), (1 - mx, 1 - my)]

        def half(jj, hc):
            return out_ref.at[jj, pl.ds(hc * rh, rh), :]

        mine = pltpu.make_async_copy(x_ref, out_ref.at[j], local_sem)
        mine.start()
        first = []
        for kk, (px, py) in enumerate(chips):
            cp = pltpu.make_async_remote_copy(
                src_ref=x_ref.at[pl.ds(mc * rh, rh), :], dst_ref=half(j, mc),
                send_sem=send_sems.at[kk], recv_sem=recv_sems.at[kk], device_id=(px, py, mc), device_id_type=MESH)
            cp.start()
            first.append(cp)
        passed = []
        for kk, (px, py) in enumerate(chips):
            jj = 2 * px + py
            pltpu.make_async_remote_copy(
                src_ref=x_ref.at[pl.ds(mc * rh, rh), :], dst_ref=half(jj, mc),
                send_sem=send_sems.at[kk], recv_sem=recv_sems.at[kk], device_id=(px, py, mc),
                device_id_type=MESH).wait_recv()
            cp = pltpu.make_async_remote_copy(
                src_ref=half(jj, mc), dst_ref=half(jj, mc), send_sem=send_sems.at[3 + kk],
                recv_sem=recv_sems.at[3 + kk], device_id=(mx, my, 1 - mc), device_id_type=MESH)
            cp.start()
            passed.append(cp)
        for kk, (px, py) in enumerate(chips):
            jj = 2 * px + py
            pltpu.make_async_remote_copy(
                src_ref=half(jj, 1 - mc), dst_ref=half(jj, 1 - mc), send_sem=send_sems.at[3 + kk],
                recv_sem=recv_sems.at[3 + kk], device_id=(mx, my, 1 - mc), device_id_type=MESH).wait_recv()
        for cp in first + passed:
            cp.wait_send()
        mine.wait()

    return pl.pallas_call(
        body, name=name,
        out_shape=jax.ShapeDtypeStruct((4, r, c), x.dtype),
        in_specs=[pl.BlockSpec(memory_space=pltpu.VMEM)],
        out_specs=pl.BlockSpec(memory_space=pltpu.VMEM),
        scratch_shapes=[pltpu.SemaphoreType.DMA((6,)), pltpu.SemaphoreType.DMA((6,)), pltpu.SemaphoreType.DMA],
        compiler_params=pltpu.CompilerParams(vmem_limit_bytes=VMEM_LIMIT),
    )(x)


RS_ROWS = 32


def _reduce_scatter_chips(g, name):
    _, r, c = g.shape
    rh = r // 2
    steps = rh // RS_ROWS

    def body(g_ref, out_ref, sib_ref, part_ref, got_ref, send_sems, recv_sems):
        mx, my, mc = _place()
        j = 2 * mx + my
        sibling = (mx, my, 1 - mc)
        chips = [(1 - mx, my), (mx, 1 - my), (1 - mx, 1 - my)]

        to_sib = pltpu.make_async_remote_copy(
            src_ref=g_ref.at[:, pl.ds((1 - mc) * rh, rh), :], dst_ref=sib_ref,
            send_sem=send_sems.at[0], recv_sem=recv_sems.at[0], device_id=sibling, device_id_type=MESH)
        to_sib.start()
        to_sib.wait()

        def add_sibling(step, carry):
            rows = pl.ds(pl.multiple_of(step * RS_ROWS, RS_ROWS), RS_ROWS)
            mine = g_ref[:, pl.ds(pl.multiple_of(mc * rh + step * RS_ROWS, RS_ROWS), RS_ROWS), :]
            part_ref[:, rows, :] = mine.astype(F32) + sib_ref[:, rows, :].astype(F32)
            return carry

        lax.fori_loop(0, steps, add_sibling, 0)

        def to_bf16(step, carry):
            rows = pl.ds(pl.multiple_of(step * RS_ROWS, RS_ROWS), RS_ROWS)
            sib_ref[:, rows, :] = part_ref[:, rows, :].astype(BF16)
            return carry

        lax.fori_loop(0, steps, to_bf16, 0)

        sends = []
        for kk, (px, py) in enumerate(chips):
            cp = pltpu.make_async_remote_copy(
                src_ref=sib_ref.at[2 * px + py], dst_ref=got_ref.at[kk],
                send_sem=send_sems.at[1 + kk], recv_sem=recv_sems.at[1 + kk],
                device_id=(px, py, mc), device_id_type=MESH)
            cp.start()
            sends.append(cp)
        for cp in sends:
            cp.wait()

        def total(step, carry):
            rows = pl.ds(pl.multiple_of(step * RS_ROWS, RS_ROWS), RS_ROWS)
            acc = part_ref[j, rows, :]
            for kk in range(3):
                acc = acc + got_ref[kk, rows, :].astype(F32)
            out_ref[pl.ds(pl.multiple_of(mc * rh + step * RS_ROWS, RS_ROWS), RS_ROWS), :] = acc
            return carry

        lax.fori_loop(0, steps, total, 0)

        done = pltpu.make_async_remote_copy(
            src_ref=out_ref.at[pl.ds(mc * rh, rh), :], dst_ref=out_ref.at[pl.ds(mc * rh, rh), :],
            send_sem=send_sems.at[4], recv_sem=recv_sems.at[4], device_id=sibling, device_id_type=MESH)
        done.start()
        done.wait_send()
        pltpu.make_async_remote_copy(
            src_ref=out_ref.at[pl.ds((1 - mc) * rh, rh), :], dst_ref=out_ref.at[pl.ds((1 - mc) * rh, rh), :],
            send_sem=send_sems.at[4], recv_sem=recv_sems.at[4], device_id=sibling, device_id_type=MESH).wait_recv()

    return pl.pallas_call(
        body, name=name,
        out_shape=jax.ShapeDtypeStruct((r, c), F32),
        in_specs=[pl.BlockSpec(memory_space=pltpu.VMEM)],
        out_specs=pl.BlockSpec(memory_space=pltpu.VMEM),
        scratch_shapes=[pltpu.VMEM((4, rh, c), BF16), pltpu.VMEM((4, rh, c), F32), pltpu.VMEM((3, rh, c), BF16),
                        pltpu.SemaphoreType.DMA((5,)), pltpu.SemaphoreType.DMA((5,))],
        compiler_params=pltpu.CompilerParams(vmem_limit_bytes=VMEM_LIMIT),
    )(g)


def _sum8(x, name):
    _, r, c = x.shape

    def body(x_ref, o_ref):
        acc = x_ref[0]
        for d in range(1, 8):
            acc = acc + x_ref[d]
        o_ref[...] = acc

    return pl.pallas_call(
        body, name=name, out_shape=jax.ShapeDtypeStruct((r, c), F32),
        in_specs=[pl.BlockSpec(memory_space=pltpu.VMEM)], out_specs=pl.BlockSpec(memory_space=pltpu.VMEM),
    )(x)


def _adamw(w, g, m, v, name):
    r, c = w.shape
    t = _pick(r, 256, SUBLANES)
    spec = pl.BlockSpec((t, c), lambda i: (i, 0))

    def body(w_ref, g_ref, m_ref, v_ref, d_ref, nm_ref, nv_ref):
        gv = g_ref[...]
        m_new = ADAM_B1 * m_ref[...] + (1.0 - ADAM_B1) * gv
        v_new = ADAM_B2 * v_ref[...] + (1.0 - ADAM_B2) * (gv * gv)
        m_hat = m_new / (1.0 - ADAM_B1 ** ADAM_STEP)
        v_hat = v_new / (1.0 - ADAM_B2 ** ADAM_STEP)
        d_ref[...] = -ADAM_LR * (m_hat / (jnp.sqrt(v_hat) + ADAM_EPS) + ADAM_WD * w_ref[...])
        nm_ref[...] = m_new
        nv_ref[...] = v_new

    return pl.pallas_call(
        body, name=name, grid=(r // t,), in_specs=[spec] * 4, out_specs=[spec] * 3,
        out_shape=[jax.ShapeDtypeStruct((r, c), F32)] * 3, compiler_params=_params(("parallel",)),
    )(w, g, m, v)


def _pack_rows(parts):
    rows, offs, o = [], [], 0
    for p in parts:
        f = p.reshape(-1)
        n = -(-f.shape[0] // LANES)
        rows.append(jnp.pad(f, (0, n * LANES - f.shape[0])).reshape(n, LANES))
        offs.append((o, n))
        o += n
    pad = (-o) % SUBLANES
    if pad:
        rows.append(jnp.zeros((pad, LANES), F32))
    return jnp.concatenate(rows, 0), offs


def _unpack_rows(packed, offs, shapes):
    out = []
    for (o, n), shp in zip(offs, shapes):
        size = 1
        for d in shp:
            size *= d
        out.append(packed[o:o + n].reshape(-1)[:size].reshape(shp))
    return out


def _ffn_fwd(x, s, sh, g, w_in, w_out, tag):
    (h,) = _rowcall(lambda r, p: ([_modulate(r[0], p[0], p[1])], []), [x], [s, sh], [(x.shape[1], BF16)], [],
                    tile=512, name=tag + "_mod")
    gu = _mm(h, w_in, "nn", BF16, tag + "_in")
    (act,) = _rowcall(lambda r, p: ([_silu(r[0].astype(F32)) * r[1].astype(F32)], []),
                      [(gu, D_FF, 0), (gu, D_FF, 1)], [], [(D_FF, BF16)], [], tile=256, name=tag + "_act")
    f = _mm(act, w_out, "nn", F32, tag + "_out")
    (y,) = _rowcall(lambda r, p: ([r[0] + 0.5 * p[0] * r[1]], []), [x, f], [g], [(x.shape[1], F32)], [],
                    tile=512, name=tag + "_res")
    return y, (x, h, gu, act, f)


def _ffn_bwd(dy, saved, s, sh, g, w_in, w_out, tag):
    x, h, gu, act, f = saved
    d = x.shape[1]
    df, dg = _rowcall(lambda r, p: ([0.5 * p[0] * r[0]], [0.5 * jnp.sum(r[0] * r[1], 0, keepdims=True)]),
                      [dy, f], [g], [(d, BF16)], [(1, d)], tile=512, name=tag + "_bres")
    da = _mm(df, w_out, "nt", BF16, tag + "_bout")
    dw_out = _mm(act, df, "tn", BF16, tag + "_bwout")

    def act_bwd(r, p):
        gate, up, dav = r[0].astype(F32), r[1].astype(F32), r[2].astype(F32)
        _, vjp = jax.vjp(lambda a, b: _silu(a) * b, gate, up)
        dgate, dup = vjp(dav)
        return [jnp.concatenate([dgate, dup], 1)], []

    (dgu,) = _rowcall(act_bwd, [(gu, D_FF, 0), (gu, D_FF, 1), da], [], [(2 * D_FF, BF16)], [], tile=256,
                      name=tag + "_bact")
    dh = _mm(dgu, w_in, "nt", F32, tag + "_bin")
    dw_in = _mm(h, dgu, "tn", BF16, tag + "_bwin")

    def mod_bwd(r, p):
        _, vjp = jax.vjp(_modulate, r[0], p[0], p[1])
        dx, ds, dsh = vjp(r[1])
        return [r[2] + dx], [ds, dsh]

    dx, ds, dsh = _rowcall(mod_bwd, [x, dh, dy], [s, sh], [(d, F32)], [(1, d), (1, d)], tile=512, name=tag + "_bmod")
    return dx, (dsh, ds, dg), dw_in, dw_out


HW4 = [HEAD] * 4


def _mixer_fwd(x, s, sh, g, wts, rope):
    w_in_p, conv8, a_log, dt_bias, wn, wq, w_uq_p, wkv, w_ukv, wqn, wqr, wkn, wkr, won, w_out = wts
    cos2, sin2 = rope
    d = x.shape[1]
    (h,) = _rowcall(lambda r, p: ([_modulate(r[0], p[0], p[1])], []), [x], [s, sh], [(d, BF16)], [],
                    tile=512, name="mix_mod")
    proj = _mm(h, w_in_p, "nn", F32, "mix_in")
    qkv_c = _conv_fwd(proj, conv8, "mix_conv")
    gab = (proj, LANES, 23)

    q, k, v, gb = _rowcall(
        lambda r, p: (list(_gdn_prep_core(_split(r[0], [HEAD] * 12), r[1], p[0], p[1])), []),
        [qkv_c, gab], [a_log, dt_bias], [(512, F32)] * 3 + [(LANES, F32)], [], tile=256, name="mix_gdn_prep")
    o_gdn, states = _gdn_fwd(q, k, v, gb, "mix_gdn")

    cq, ckv, kr = (proj, 512, 4), (proj, 256, 10), (proj, LANES, 22)
    cqn, ckvn, k_rope = _rowcall(
        lambda r, p: (list(_mla_prep_core(r[0][:, :MLA_Q_LORA], r[1], r[2], r[3], r[4], p[0], p[1], p[2])), []),
        [cq, ckv, kr, cos2, sin2], [wq, wkv, wkr], [(MLA_Q_LORA, BF16), (MLA_KV_LORA, BF16), (LANES, F32)], [],
        tile=512, name="mix_mla_prep")
    qf = _mm(cqn, w_uq_p, "nn", F32, "mix_uq")
    kvf = _mm(ckvn, w_ukv, "nn", F32, "mix_ukv")

    def qk_prep(r, p):
        qparts = _split(r[0], [HEAD] * 8)
        kvparts = _split(r[1], [HEAD] * 8)
        return list(_qk_prep_core(qparts[:4], qparts[4:], kvparts[0::2], kvparts[1::2], r[2], r[3], r[4],
                                  p[0], p[1], p[2])), []

    qa, ka, va = _rowcall(qk_prep, [qf, kvf, k_rope, cos2, sin2], [wqn, wqr, wkn],
                          [(4 * QK_PAD, BF16), (4 * QK_PAD, BF16), (4 * HEAD, BF16)], [], tile=256,
                          name="mix_qk_prep")
    o_b, lse = _attn_fwd(qa, ka, va, "mix_attn")

    gz = (proj, 512, 3)
    (mixed,) = _rowcall(
        lambda r, p: ([_mix_post_core(_split(r[0], HW4), _split(r[1], HW4), _split(r[2], HW4), p[0], p[1])], []),
        [o_gdn, gz, o_b], [wn, won], [(2 * 512, BF16)], [], tile=512, name="mix_post")
    y = _mm(mixed, w_out, "nn", F32, "mix_out")
    (x_out,) = _rowcall(lambda r, p: ([r[0] + p[0] * r[1]], []), [x, y], [g], [(d, F32)], [], tile=512,
                        name="mix_res")
    saved = (x, h, proj, qkv_c, q, k, v, gb, states, o_gdn, cqn, ckvn, k_rope, qf, kvf, qa, ka, va, o_b, lse,
             mixed, y)
    return x_out, saved


def _mixer_bwd(dy, saved, s, sh, g, wts, rope):
    w_in_p, conv8, a_log, dt_bias, wn, wq, w_uq_p, wkv, w_ukv, wqn, wqr, wkn, wkr, won, w_out = wts
    cos2, sin2 = rope
    (x, h, proj, qkv_c, q, k, v, gb, states, o_gdn, cqn, ckvn, k_rope, qf, kvf, qa, ka, va, o_b, lse,
     mixed, y) = saved
    d = x.shape[1]
    dyb, dg = _rowcall(lambda r, p: ([p[0] * r[0]], [jnp.sum(r[0] * r[1], 0, keepdims=True)]),
                       [dy, y], [g], [(d, BF16)], [(1, d)], tile=512, name="mix_bres")
    dmixed = _mm(dyb, w_out, "nt", F32, "mix_bout")
    dw_out = _mm(mixed, dyb, "tn", BF16, "mix_bwout")

    gz = (proj, 512, 3)

    def post_bwd(r, p):
        _, vjp = jax.vjp(_mix_post_core, _split(r[0], HW4), _split(r[1], HW4), _split(r[2], HW4), p[0], p[1])
        do, dz, dob, dwn, dwon = vjp(r[3])
        return [jnp.concatenate(do, 1), jnp.concatenate(dz, 1), jnp.concatenate(dob, 1)], [dwn, dwon]

    do_gdn, dgz, do_b, dwn, dwon = _rowcall(post_bwd, [o_gdn, gz, o_b, dmixed], [wn, won], [(512, F32)] * 3,
                                            [(1, HEAD), (1, HEAD)], tile=256, name="mix_bpost")

    dqa = _attn_bwd_dq(qa, ka, va, o_b, do_b, lse, "mix_battn_dq")
    dka, dva = _attn_bwd_dkv(qa, ka, va, o_b, do_b, lse, "mix_battn_dkv")

    def qk_bwd(r, p):
        qparts = _split(r[0], [HEAD] * 8)
        kvparts = _split(r[1], [HEAD] * 8)
        _, vjp = jax.vjp(_qk_prep_core, qparts[:4], qparts[4:], kvparts[0::2], kvparts[1::2], r[2], r[3], r[4],
                         p[0], p[1], p[2])
        dqn, dqr, dkn, dvp, dkrope, _, _, dwqn, dwqr, dwkn = vjp((r[5], r[6], r[7]))
        dkv = []
        for a, b in zip(dkn, dvp):
            dkv += [a, b]
        return [jnp.concatenate(list(dqn) + list(dqr), 1), jnp.concatenate(dkv, 1), dkrope], [dwqn, dwqr, dwkn]

    dqf, dkvf, dk_rope, dwqn, dwqr, dwkn = _rowcall(
        qk_bwd, [qf, kvf, k_rope, cos2, sin2, dqa, dka, dva], [wqn, wqr, wkn],
        [(8 * HEAD, BF16), (8 * HEAD, BF16), (LANES, F32)], [(1, HEAD)] * 3, tile=256, name="mix_bqk_prep")
    dcqn = _mm(dqf, w_uq_p, "nt", F32, "mix_buq")
    dw_uq_p = _mm(cqn, dqf, "tn", F32, "mix_bwuq")
    dckvn = _mm(dkvf, w_ukv, "nt", F32, "mix_bukv")
    dw_ukv = _mm(ckvn, dkvf, "tn", F32, "mix_bwukv")

    cq, ckv, kr = (proj, 512, 4), (proj, 256, 10), (proj, LANES, 22)

    def mla_bwd(r, p):
        _, vjp = jax.vjp(_mla_prep_core, r[0][:, :MLA_Q_LORA], r[1], r[2], r[3], r[4], p[0], p[1], p[2])
        dcq, dckv, dkr, _, _, dwq, dwkv, dwkr = vjp((r[5], r[6], r[7]))
        pad = jnp.zeros((dcq.shape[0], 512 - MLA_Q_LORA), F32)
        return [jnp.concatenate([dcq, pad], 1), dckv, dkr], [dwq, dwkv, dwkr]

    dcq, dckv, dkr, dwq, dwkv, dwkr = _rowcall(
        mla_bwd, [cq, ckv, kr, cos2, sin2, dcqn, dckvn, dk_rope], [wq, wkv, wkr],
        [(512, F32), (MLA_KV_LORA, F32), (LANES, F32)], [(1, MLA_Q_LORA), (1, MLA_KV_LORA), (1, LANES)],
        tile=512, name="mix_bmla_prep")

    dq, dk, dv, dgb = _gdn_bwd(q, k, v, gb, states, do_gdn, "mix_bgdn")
    gab = (proj, LANES, 23)

    def gdn_prep_bwd(r, p):
        _, vjp = jax.vjp(_gdn_prep_core, _split(r[0], [HEAD] * 12), r[1], p[0], p[1])
        dparts, dgab, da_log, ddt = vjp((r[2], r[3], r[4], r[5]))
        return [jnp.concatenate(dparts, 1), dgab], [da_log, ddt]

    dqkv_c, dgab, da_log, ddt = _rowcall(gdn_prep_bwd, [qkv_c, gab, dq, dk, dv, dgb], [a_log, dt_bias],
                                         [(1536, F32), (LANES, F32)], [(1, LANES), (1, LANES)], tile=256,
                                         name="mix_bgdn_prep")
    dqkv_pre, dconv8 = _conv_bwd(proj, dqkv_c, conv8, "mix_bconv")

    dproj = jnp.concatenate([dqkv_pre.astype(BF16), dgz.astype(BF16), dcq.astype(BF16), dckv.astype(BF16),
                             dkr.astype(BF16), dgab.astype(BF16)], axis=1)
    dh = _mm(dproj, w_in_p, "nt", F32, "mix_bin")
    dw_in_p = _mm(h, dproj, "tn", F32, "mix_bwin")

    def mod_bwd(r, p):
        _, vjp = jax.vjp(_modulate, r[0], p[0], p[1])
        dx, ds, dsh = vjp(r[1])
        return [r[2] + dx], [ds, dsh]

    dx, ds, dsh = _rowcall(mod_bwd, [x, dh, dy], [s, sh], [(d, F32)], [(1, d), (1, d)], tile=512, name="mix_bmod")
    small = dict(conv=dconv8, a_log=da_log, dt=ddt, wn=dwn, wq=dwq, wkv=dwkv, wqn=dwqn, wqr=dwqr, wkn=dwkn,
                 wkr=dwkr, won=dwon)
    return dx, (dsh, ds, dg), dw_in_p, dw_uq_p, dw_ukv, dw_out, small


def _pad_cols(a, n):
    return jnp.pad(a, ((0, 0),) * (a.ndim - 1) + ((0, n - a.shape[-1]),))


def _pack_w_in(w):
    z = lambda n: jnp.zeros((w.shape[0], n), w.dtype)
    return jnp.concatenate([w[:, 0:2048], w[:, 2056:2440], z(128), w[:, 2440:2696], w[:, 2696:2760], z(64),
                            w[:, 2048:2056], z(120)], axis=1)


def _unpack_w_in(wp):
    return jnp.concatenate([wp[:, 0:2048], wp[:, 2944:2952], wp[:, 2048:2432], wp[:, 2560:2816], wp[:, 2816:2880]],
                           axis=1)


def _pack_w_uq(w):
    z = jnp.zeros((w.shape[0], LANES - MLA_ROPE), w.dtype)
    nope = [w[:, h * 192:h * 192 + HEAD] for h in range(MLA_HEADS)]
    rope = []
    for h in range(MLA_HEADS):
        rope += [w[:, h * 192 + HEAD:(h + 1) * 192], z]
    return jnp.concatenate(nope + rope, axis=1)


def _unpack_w_uq(wp):
    cols = []
    for h in range(MLA_HEADS):
        cols += [wp[:, h * HEAD:(h + 1) * HEAD], wp[:, 512 + h * LANES:512 + h * LANES + MLA_ROPE]]
    return jnp.concatenate(cols, axis=1)


def _cols_to_chips(a):
    r, c = a.shape
    return a.reshape(r, 4, c // 4).transpose(1, 0, 2)


def _chips_to_cols(a):
    _, r, n = a.shape
    return a.transpose(1, 0, 2).reshape(r, 4 * n)


def _pad128(v, n=LANES):
    return _pad_cols(v.reshape(1, -1), n)


def kernel(x, c, positions, w_ada, b_ada, ffn1_w_in, ffn1_w_out, w_in, gdn_conv_w, gdn_a_log, gdn_dt_bias, gdn_norm_w, mla_q_norm_w, mla_w_uq, mla_kv_norm_w, mla_w_ukv, qkn_q_nope, qkn_q_rope, qkn_k_nope, qkn_k_rope, mla_out_norm_w, w_out, ffn2_w_in, ffn2_w_out, loss_target, m_w_ada, m_b_ada, m_ffn1_w_in, m_ffn1_w_out, m_w_in, m_gdn_conv_w, m_gdn_a_log, m_gdn_dt_bias, m_gdn_norm_w, m_mla_q_norm_w, m_mla_w_uq, m_mla_kv_norm_w, m_mla_w_ukv, m_qkn_q_nope, m_qkn_q_rope, m_qkn_k_nope, m_qkn_k_rope, m_mla_out_norm_w, m_w_out, m_ffn2_w_in, m_ffn2_w_out, v_w_ada, v_b_ada, v_ffn1_w_in, v_ffn1_w_out, v_w_in, v_gdn_conv_w, v_gdn_a_log, v_gdn_dt_bias, v_gdn_norm_w, v_mla_q_norm_w, v_mla_w_uq, v_mla_kv_norm_w, v_mla_w_ukv, v_qkn_q_nope, v_qkn_q_rope, v_qkn_k_nope, v_qkn_k_rope, v_mla_out_norm_w, v_w_out, v_ffn2_w_in, v_ffn2_w_out):
    weights = dict(w_ada=w_ada, b_ada=b_ada, ffn1_w_in=ffn1_w_in, ffn1_w_out=ffn1_w_out, w_in=w_in,
                   gdn_conv_w=gdn_conv_w, gdn_a_log=gdn_a_log, gdn_dt_bias=gdn_dt_bias, gdn_norm_w=gdn_norm_w,
                   mla_q_norm_w=mla_q_norm_w, mla_w_uq=mla_w_uq, mla_kv_norm_w=mla_kv_norm_w, mla_w_ukv=mla_w_ukv,
                   qkn_q_nope=qkn_q_nope, qkn_q_rope=qkn_q_rope, qkn_k_nope=qkn_k_nope, qkn_k_rope=qkn_k_rope,
                   mla_out_norm_w=mla_out_norm_w, w_out=w_out, ffn2_w_in=ffn2_w_in, ffn2_w_out=ffn2_w_out)
    moms_m = dict(w_ada=m_w_ada, b_ada=m_b_ada, ffn1_w_in=m_ffn1_w_in, ffn1_w_out=m_ffn1_w_out, w_in=m_w_in,
                  gdn_conv_w=m_gdn_conv_w, gdn_a_log=m_gdn_a_log, gdn_dt_bias=m_gdn_dt_bias,
                  gdn_norm_w=m_gdn_norm_w, mla_q_norm_w=m_mla_q_norm_w, mla_w_uq=m_mla_w_uq,
                  mla_kv_norm_w=m_mla_kv_norm_w, mla_w_ukv=m_mla_w_ukv, qkn_q_nope=m_qkn_q_nope,
                  qkn_q_rope=m_qkn_q_rope, qkn_k_nope=m_qkn_k_nope, qkn_k_rope=m_qkn_k_rope,
                  mla_out_norm_w=m_mla_out_norm_w, w_out=m_w_out, ffn2_w_in=m_ffn2_w_in, ffn2_w_out=m_ffn2_w_out)
    moms_v = dict(w_ada=v_w_ada, b_ada=v_b_ada, ffn1_w_in=v_ffn1_w_in, ffn1_w_out=v_ffn1_w_out, w_in=v_w_in,
                  gdn_conv_w=v_gdn_conv_w, gdn_a_log=v_gdn_a_log, gdn_dt_bias=v_gdn_dt_bias,
                  gdn_norm_w=v_gdn_norm_w, mla_q_norm_w=v_mla_q_norm_w, mla_w_uq=v_mla_w_uq,
                  mla_kv_norm_w=v_mla_kv_norm_w, mla_w_ukv=v_mla_w_ukv, qkn_q_nope=v_qkn_q_nope,
                  qkn_q_rope=v_qkn_q_rope, qkn_k_nope=v_qkn_k_nope, qkn_k_rope=v_qkn_k_rope,
                  mla_out_norm_w=v_mla_out_norm_w, w_out=v_w_out, ffn2_w_in=v_ffn2_w_in, ffn2_w_out=v_ffn2_w_out)
    names = list(weights)

    seq, d = x.shape[1], x.shape[2]
    x2d = x.reshape(seq, d)
    tgt = loss_target.reshape(seq, d)
    mx, my, mc = _place()
    chip = 2 * mx + my
    me = 2 * chip + mc
    n_mod = b_ada.shape[1] // d
    shard = w_ada.shape[2]

    half = MLA_ROPE // 2
    inv_freq = 10000.0 ** (-jnp.arange(half, dtype=F32) / half)
    ang = positions.astype(F32).reshape(seq, 1) * inv_freq
    cosv, sinv = jnp.cos(ang), jnp.sin(ang)
    cos2 = _pad_cols(jnp.concatenate([cosv, cosv], 1), LANES)
    sin2 = _pad_cols(jnp.concatenate([-sinv, sinv], 1), LANES)
    rope = (cos2, sin2)

    c_all = _allgather8(jnp.pad(c, ((0, SUBLANES - 1), (0, 0))), "gather_c")[:, 0, :]
    (sc_all,) = _rowcall(lambda r, p: ([_silu(r[0])], []), [c_all], [], [(d, F32)], [], tile=8, name="ada_silu")
    mod_part = _mm(sc_all, w_ada[0], "nn", F32, "ada_mm", hi=True)
    mod_all = _allgather8(mod_part, "gather_mod")
    mod_rows = lax.dynamic_index_in_dim(mod_all, me, axis=1, keepdims=False)
    mod_raw = jnp.concatenate([mod_rows[2 * jj] for jj in range(4)], axis=0).reshape(1, 4 * shard)
    (mod,) = _rowcall(lambda r, p: ([r[0] + r[1]], []),
                      [jnp.pad(mod_raw, ((0, 7), (0, 0))), jnp.pad(b_ada, ((0, 7), (0, 0)))], [],
                      [(4 * shard, F32)], [], tile=8, name="ada_bias")
    mods = [mod[0:1, i * d:(i + 1) * d] for i in range(n_mod)]
    sh1, s1, g1, sh2, s2, g2, sh3, s3, g3 = mods

    def gather_cols(w, name, pad_to=None):
        w2 = w[0].astype(BF16)
        n = w2.shape[1]
        if pad_to:
            w2 = _pad_cols(w2, pad_to)
        return _chips_to_cols(_allgather_chips(w2, name)[:, :, :n])

    def gather_rows(w, name):
        w2 = w[0].astype(BF16)
        return _allgather_chips(w2, name).reshape(4 * w2.shape[0], w2.shape[1])

    f1_in = gather_cols(ffn1_w_in, "gather_f1_in")
    f1_out = gather_rows(ffn1_w_out, "gather_f1_out")
    w_in_full = gather_cols(w_in, "gather_w_in", 768)
    w_uq_full = gather_cols(mla_w_uq, "gather_w_uq", 256)
    w_ukv_full = gather_cols(mla_w_ukv, "gather_w_ukv")
    w_out_full = gather_rows(w_out, "gather_w_out")
    f2_in = gather_cols(ffn2_w_in, "gather_f2_in")
    f2_out = gather_rows(ffn2_w_out, "gather_f2_out")
    conv_all = _allgather8(jnp.pad(gdn_conv_w[0], ((0, SUBLANES - CONV_K), (0, 0))), "gather_conv")
    conv8 = jnp.concatenate([conv_all[2 * jj] for jj in range(4)], axis=1)

    wts = (_pack_w_in(w_in_full), conv8, _pad128(gdn_a_log), _pad128(gdn_dt_bias), gdn_norm_w,
           mla_q_norm_w, _pack_w_uq(w_uq_full), mla_kv_norm_w, w_ukv_full, qkn_q_nope, _pad128(qkn_q_rope),
           qkn_k_nope, _pad128(qkn_k_rope), mla_out_norm_w, w_out_full)

    x1, sv1 = _ffn_fwd(x2d, s1, sh1, g1, f1_in, f1_out, "ffn1")
    xm, svm = _mixer_fwd(x1, s2, sh2, g2, wts, rope)
    x3, sv3 = _ffn_fwd(xm, s3, sh3, g3, f2_in, f2_out, "ffn2")

    def loss_fn(r, p):
        err = r[0] - r[1]
        part = 0.5 * jnp.sum(jnp.sum(err * err, axis=1, keepdims=True) * (1.0 / d), axis=0, keepdims=True)
        return [err * (1.0 / d)], [jnp.broadcast_to(part, (1, LANES))]

    dy, loss_part = _rowcall(loss_fn, [x3, tgt], [], [(d, F32)], [(1, LANES)], tile=512, name="loss")
    loss = lax.psum(loss_part[0, 0], ("x", "y", "c"))

    dxm, dmod3, dw_f2_in, dw_f2_out = _ffn_bwd(dy, sv3, s3, sh3, g3, f2_in, f2_out, "ffn2")
    dx1, dmod2, dw_in_p, dw_uq_p, dw_ukv, dw_out_m, small = _mixer_bwd(dxm, svm, s2, sh2, g2, wts, rope)
    dx0, dmod1, dw_f1_in, dw_f1_out = _ffn_bwd(dx1, sv1, s1, sh1, g1, f1_in, f1_out, "ffn1")
    grad_x = dx0.reshape(x.shape)

    dmod = jnp.concatenate(list(dmod1) + list(dmod2) + list(dmod3), axis=1)
    small_parts = [dmod, small["conv"][:CONV_K], small["a_log"], small["dt"], small["wn"], small["wq"],
                   small["wkv"], small["wqn"], small["wqr"], small["wkn"], small["wkr"], small["won"]]
    packed, offs = _pack_rows(small_parts)
    gathered = _allgather8(packed, "gather_small")
    total = _sum8(gathered, "sum_small")
    (g_b_ada, g_conv, g_a_log, g_dt, g_wn, g_wq, g_wkv, g_wqn, g_wqr, g_wkn, g_wkr, g_won) = _unpack_rows(
        total, offs, [p.shape for p in small_parts])
    dmod_all = _unpack_rows(gathered.reshape(-1, LANES),
                            [(dd * packed.shape[0] + offs[0][0], offs[0][1]) for dd in range(8)],
                            [dmod.shape] * 8)
    dmod_all = jnp.concatenate(dmod_all, axis=0)
    dmod_mine = lax.dynamic_slice_in_dim(dmod_all, chip * shard, shard, axis=1)

    def ada_grad(r, p):
        acc = jnp.zeros((r[0].shape[0], shard), F32)
        for b in range(8):
            acc = acc + r[0][:, b:b + 1] * p[0][b:b + 1, :]
        return [acc], []

    (g_w_ada,) = _rowcall(ada_grad, [_pad_cols(sc_all.T, LANES)], [dmod_mine], [(shard, F32)], [], tile=256,
                          name="ada_grad")

    grads = dict(
        w_ada=g_w_ada[None], b_ada=g_b_ada,
        gdn_conv_w=lax.dynamic_slice_in_dim(g_conv, chip * gdn_conv_w.shape[2], gdn_conv_w.shape[2], axis=1)[None],
        gdn_a_log=g_a_log[:, :GDN_HEADS], gdn_dt_bias=g_dt[:, :GDN_HEADS], gdn_norm_w=g_wn, mla_q_norm_w=g_wq,
        mla_kv_norm_w=g_wkv, qkn_q_nope=g_wqn, qkn_q_rope=g_wqr[:, :MLA_ROPE], qkn_k_nope=g_wkn,
        qkn_k_rope=g_wkr[:, :MLA_ROPE], mla_out_norm_w=g_won)

    def rs_cols(dw, name, pad_to=None):
        g4 = _cols_to_chips(dw).astype(BF16)
        n = g4.shape[2]
        if pad_to:
            g4 = _pad_cols(g4, pad_to)
        return _reduce_scatter_chips(g4, name)[:, :n][None]

    def rs_rows(dw, name):
        r, cc = dw.shape
        return _reduce_scatter_chips(dw.astype(BF16).reshape(4, r // 4, cc), name)[None]

    grads["ffn2_w_in"] = rs_cols(dw_f2_in, "rs_f2_in")
    grads["ffn2_w_out"] = rs_rows(dw_f2_out, "rs_f2_out")
    grads["w_in"] = rs_cols(_unpack_w_in(dw_in_p), "rs_w_in", 768)
    grads["mla_w_uq"] = rs_cols(_unpack_w_uq(dw_uq_p), "rs_w_uq", 256)
    grads["mla_w_ukv"] = rs_cols(dw_ukv, "rs_w_ukv")
    grads["w_out"] = rs_rows(dw_out_m, "rs_w_out")
    grads["ffn1_w_in"] = rs_cols(dw_f1_in, "rs_f1_in")
    grads["ffn1_w_out"] = rs_rows(dw_f1_out, "rs_f1_out")

    big = ["w_ada", "ffn1_w_in", "ffn1_w_out", "w_in", "mla_w_uq", "mla_w_ukv", "w_out", "ffn2_w_in", "ffn2_w_out"]
    delta, new_m, new_v = {}, {}, {}
    for nme in big:
        shp = weights[nme].shape
        dl, nm, nv = _adamw(weights[nme][0], grads[nme][0], moms_m[nme][0], moms_v[nme][0], "adamw_" + nme)
        delta[nme], new_m[nme], new_v[nme] = dl.reshape(shp), nm.reshape(shp), nv.reshape(shp)
    tiny = [nme for nme in names if nme not in big]
    shapes = [weights[nme].shape for nme in tiny]
    pw, poffs = _pack_rows([weights[nme] for nme in tiny])
    pg, _ = _pack_rows([grads[nme] for nme in tiny])
    pm, _ = _pack_rows([moms_m[nme] for nme in tiny])
    pv, _ = _pack_rows([moms_v[nme] for nme in tiny])
    pd, pnm, pnv = _adamw(pw, pg, pm, pv, "adamw_small")
    for nme, dl, nm, nv in zip(tiny, _unpack_rows(pd, poffs, shapes), _unpack_rows(pnm, poffs, shapes),
                               _unpack_rows(pnv, poffs, shapes)):
        delta[nme], new_m[nme], new_v[nme] = dl, nm, nv

    return (loss, grad_x, *[grads[nme].reshape(weights[nme].shape) for nme in names],
            *[delta[nme] for nme in names], *[new_m[nme] for nme in names], *[new_v[nme] for nme in names])
```

```python
import functools

import jax
import jax.numpy as jnp
from jax import lax
from jax.experimental import pallas as pl
from jax.experimental.pallas import tpu as pltpu

F32 = jnp.float32
BF16 = jnp.bfloat16
HI = lax.Precision.HIGHEST
MESH = pl.DeviceIdType.MESH

EPS = 1e-6
CHUNK = 64
D_FF = 2816
GDN_HEADS = 4
HEAD = 128
MLA_HEADS = 4
MLA_ROPE = 64
MLA_Q_LORA = 384
MLA_KV_LORA = 256
QK_PAD = 256
ATT_SCALE = (HEAD + MLA_ROPE) ** -0.5
N_PROJ = 3072

ADAM_LR, ADAM_B1, ADAM_B2, ADAM_EPS, ADAM_WD, ADAM_STEP = 0.001, 0.9, 0.999, 1e-08, 0.01, 10

LANES = 128
SUBLANES = 8
VMEM_LIMIT = 56 * 2 ** 20


def _params(sem=None):
    return pltpu.CompilerParams(dimension_semantics=sem, vmem_limit_bytes=VMEM_LIMIT)


def _pick(n, cap, align):
    best = None
    d = align
    while d <= min(n, cap):
        if n % d == 0:
            best = d
        d += align
    return best if best is not None else n


def _iota(shape, dim):
    return lax.broadcasted_iota(jnp.int32, shape, dim)


def _rowcall(fn, rows, params, out_rows, out_accs, *, tile, name):
    rows = [r if isinstance(r, tuple) else (r, r.shape[1], 0) for r in rows]
    s = rows[0][0].shape[0]
    t = min(tile, s)
    n = s // t
    n_in = len(rows) + len(params)
    n_row_out = len(out_rows)

    in_specs = [pl.BlockSpec((t, w), functools.partial(lambda i, b: (i, b), b=b)) for (_, w, b) in rows]
    in_specs += [pl.BlockSpec(p.shape, lambda i: (0, 0)) for p in params]
    out_shape = [jax.ShapeDtypeStruct((s, w), dt) for (w, dt) in out_rows]
    out_shape += [jax.ShapeDtypeStruct(shape, F32) for shape in out_accs]
    out_specs = [pl.BlockSpec((t, w), lambda i: (i, 0)) for (w, _) in out_rows]
    out_specs += [pl.BlockSpec(shape, lambda i: (0, 0)) for shape in out_accs]

    def body(*refs):
        ins = refs[:n_in]
        outs = refs[n_in:]
        i = pl.program_id(0)
        vals = [r[...] for r in ins]
        row_outs, acc_outs = fn(vals[:len(rows)], vals[len(rows):])
        for r, v in zip(outs[:n_row_out], row_outs):
            r[...] = v.astype(r.dtype)
        if out_accs:
            @pl.when(i == 0)
            def _():
                for r in outs[n_row_out:]:
                    r[...] = jnp.zeros(r.shape, F32)
            for r, v in zip(outs[n_row_out:], acc_outs):
                r[...] += v

    res = pl.pallas_call(
        body, name=name, grid=(n,), in_specs=in_specs, out_specs=out_specs, out_shape=out_shape,
        compiler_params=_params(("arbitrary",) if out_accs else ("parallel",)),
    )(*[r[0] for r in rows], *params)
    return list(res)


MM_TILE_MN = 1536


def _mm(a, b, mode, out_dtype, name, hi=False):
    if mode == "nn":
        (m, k), (_, n) = a.shape, b.shape
        dims = (((1,), (0,)), ((), ()))
    elif mode == "nt":
        (m, k), (n, _) = a.shape, b.shape
        dims = (((1,), (1,)), ((), ()))
    else:
        (k, m), (_, n) = a.shape, b.shape
        dims = (((0,), (0,)), ((), ()))
    tm = _pick(m, MM_TILE_MN if mode == "tn" else 1024, LANES if mode == "tn" else 16)
    tn = _pick(n, MM_TILE_MN, LANES)
    tk = _pick(k, 1024 if mode == "tn" else MM_TILE_MN, LANES)
    nk = k // tk
    if mode == "nn":
        a_spec = pl.BlockSpec((tm, tk), lambda i, j, kk: (i, kk))
        b_spec = pl.BlockSpec((tk, tn), lambda i, j, kk: (kk, j))
    elif mode == "nt":
        a_spec = pl.BlockSpec((tm, tk), lambda i, j, kk: (i, kk))
        b_spec = pl.BlockSpec((tn, tk), lambda i, j, kk: (j, kk))
    else:
        a_spec = pl.BlockSpec((tk, tm), lambda i, j, kk: (kk, i))
        b_spec = pl.BlockSpec((tk, tn), lambda i, j, kk: (kk, j))

    def body(a_ref, b_ref, o_ref, acc_ref):
        kk = pl.program_id(2)

        @pl.when(kk == 0)
        def _():
            acc_ref[...] = jnp.zeros(acc_ref.shape, F32)

        av, bv = a_ref[...], b_ref[...]
        if hi:
            acc_ref[...] += lax.dot_general(av, bv, dims, precision=HI, preferred_element_type=F32)
        else:
            acc_ref[...] += lax.dot_general(av.astype(BF16), bv.astype(BF16), dims,
                                            preferred_element_type=F32)

        @pl.when(kk == nk - 1)
        def _():
            o_ref[...] = acc_ref[...].astype(o_ref.dtype)

    return pl.pallas_call(
        body, name=name, grid=(m // tm, n // tn, nk),
        in_specs=[a_spec, b_spec],
        out_specs=pl.BlockSpec((tm, tn), lambda i, j, kk: (i, j)),
        out_shape=jax.ShapeDtypeStruct((m, n), out_dtype),
        scratch_shapes=[pltpu.VMEM((tm, tn), F32)],
        compiler_params=_params(("parallel", "parallel", "arbitrary")),
    )(a, b)


def _rms(x, w=None, n=None):
    n = x.shape[-1] if n is None else n
    y = x * lax.rsqrt(jnp.sum(x * x, axis=-1, keepdims=True) * (1.0 / n) + EPS)
    return y if w is None else y * w


def _silu(x):
    return x * jax.nn.sigmoid(x)


def _softplus(x):
    return jnp.maximum(x, 0.0) + jnp.log1p(jnp.exp(-jnp.abs(x)))


def _split(x, widths):
    out, o = [], 0
    for w in widths:
        out.append(x[:, o:o + w])
        o += w
    return out


def _modulate(x, s, sh):
    return _rms(x) * (1.0 + s) + sh


def _rope_rot(x):
    r, c = _iota((LANES, LANES), 0), _iota((LANES, LANES), 1)
    half = MLA_ROPE // 2
    perm = (((r < half) & (c == r + half)) | ((r >= half) & (r < MLA_ROPE) & (c == r - half))).astype(F32)
    return jnp.dot(x, perm, precision=HI, preferred_element_type=F32)


def _rope(x, cos2, sin2):
    return x * cos2 + _rope_rot(x) * sin2


def _gdn_prep_core(qkv_parts, gab, a_log, dt_bias):
    act = [_silu(p) for p in qkv_parts]
    qs = [p * lax.rsqrt(jnp.sum(p * p, -1, keepdims=True) + EPS) * (HEAD ** -0.5) for p in act[:4]]
    ks = [p * lax.rsqrt(jnp.sum(p * p, -1, keepdims=True) + EPS) for p in act[4:8]]
    lane = _iota(gab.shape, 1)
    g = -jnp.exp(a_log) * _softplus(gab + dt_bias)
    beta = jax.nn.sigmoid(gab)
    gb = jnp.where(lane < GDN_HEADS, g, jnp.where(lane < 2 * GDN_HEADS, beta, 0.0))
    return (jnp.concatenate(qs, 1), jnp.concatenate(ks, 1), jnp.concatenate(act[8:], 1), gb)


def _mla_prep_core(cq, ckv, kr, cos2, sin2, wq, wkv, wkr):
    cqn = _rms(cq, wq)
    ckvn = _rms(ckv, wkv)
    k_rope = _rope(_rms(kr, wkr, MLA_ROPE), cos2, sin2)
    return cqn, ckvn, k_rope


def _qk_prep_core(qn_parts, qr_parts, kn_parts, v_parts, k_rope, cos2, sin2, wqn, wqr, wkn):
    qs, ks = [], []
    for h in range(MLA_HEADS):
        qn = _rms(qn_parts[h], wqn) * ATT_SCALE
        qr = _rope(_rms(qr_parts[h], wqr, MLA_ROPE), cos2, sin2) * ATT_SCALE
        qs += [qn, qr]
        ks += [_rms(kn_parts[h], wkn), k_rope]
    return jnp.concatenate(qs, 1), jnp.concatenate(ks, 1), jnp.concatenate(v_parts, 1)


def _mix_post_core(o_parts, gz_parts, ob_parts, wn, won):
    oa = [_rms(o, wn) * _silu(z) for o, z in zip(o_parts, gz_parts)]
    ob = [_rms(o, won) for o in ob_parts]
    return jnp.concatenate(oa + ob, 1)


CONV_K = 4
HALO = SUBLANES


def _conv_fwd(proj, w8, name):
    s = proj.shape[0]
    c = w8.shape[1]
    t = min(256, s)
    n = s // t
    hb = t // HALO

    def body(x_ref, prev_ref, w_ref, o_ref, buf):
        i = pl.program_id(0)
        buf[pl.ds(0, HALO), :] = jnp.where(i > 0, prev_ref[...], 0.0)
        buf[pl.ds(HALO, t), :] = x_ref[...]
        acc = jnp.zeros((t, c), F32)
        for k in range(CONV_K):
            acc = acc + w_ref[k:k + 1, :] * buf[pl.ds(HALO - (CONV_K - 1) + k, t), :]
        o_ref[...] = acc

    return pl.pallas_call(
        body, name=name, grid=(n,),
        in_specs=[pl.BlockSpec((t, c), lambda i: (i, 0)),
                  pl.BlockSpec((HALO, c), lambda i: (jnp.maximum(i * hb - 1, 0), 0)),
                  pl.BlockSpec(w8.shape, lambda i: (0, 0))],
        out_specs=pl.BlockSpec((t, c), lambda i: (i, 0)),
        out_shape=jax.ShapeDtypeStruct((s, c), F32),
        scratch_shapes=[pltpu.VMEM((t + HALO, c), F32)],
        compiler_params=_params(("parallel",)),
    )(proj, proj, w8)


def _conv_bwd(proj, dy, w8, name):
    s = proj.shape[0]
    c = w8.shape[1]
    t = min(256, s)
    n = s // t
    hb = t // HALO

    def body(x_ref, prev_ref, dy_ref, next_ref, w_ref, dx_ref, dw_ref, bufx, bufd):
        i = pl.program_id(0)
        bufx[pl.ds(0, HALO), :] = jnp.where(i > 0, prev_ref[...], 0.0)
        bufx[pl.ds(HALO, t), :] = x_ref[...]
        bufd[pl.ds(0, t), :] = dy_ref[...]
        bufd[pl.ds(t, HALO), :] = jnp.where(i < n - 1, next_ref[...], 0.0)

        @pl.when(i == 0)
        def _():
            dw_ref[...] = jnp.zeros(dw_ref.shape, F32)

        dyv = dy_ref[...]
        acc = jnp.zeros((t, c), F32)
        for k in range(CONV_K):
            acc = acc + w_ref[k:k + 1, :] * bufd[pl.ds(CONV_K - 1 - k, t), :]
            dw_ref[k:k + 1, :] += jnp.sum(dyv * bufx[pl.ds(HALO - (CONV_K - 1) + k, t), :], axis=0, keepdims=True)
        dx_ref[...] = acc

    return pl.pallas_call(
        body, name=name, grid=(n,),
        in_specs=[pl.BlockSpec((t, c), lambda i: (i, 0)),
                  pl.BlockSpec((HALO, c), lambda i: (jnp.maximum(i * hb - 1, 0), 0)),
                  pl.BlockSpec((t, c), lambda i: (i, 0)),
                  pl.BlockSpec((HALO, c), lambda i: (jnp.minimum((i + 1) * hb, s // HALO - 1), 0)),
                  pl.BlockSpec(w8.shape, lambda i: (0, 0))],
        out_specs=[pl.BlockSpec((t, c), lambda i: (i, 0)), pl.BlockSpec(w8.shape, lambda i: (0, 0))],
        out_shape=[jax.ShapeDtypeStruct((s, c), F32), jax.ShapeDtypeStruct(w8.shape, F32)],
        scratch_shapes=[pltpu.VMEM((t + HALO, c), F32), pltpu.VMEM((t + HALO, c), F32)],
        compiler_params=_params(("arbitrary",)),
    )(proj, proj, dy, dy, w8)


def _dot(a, b):
    return jnp.dot(a, b, precision=HI, preferred_element_type=F32)


def _dot_nt(a, b):
    return lax.dot_general(a, b, (((1,), (1,)), ((), ())), precision=HI, preferred_element_type=F32)


def _dot_tn(a, b):
    return lax.dot_general(a, b, (((0,), (0,)), ((), ())), precision=HI, preferred_element_type=F32)


def _bdot(a, b):
    return lax.dot_general(a, b, (((2,), (1,)), ((0,), (0,))), precision=HI, preferred_element_type=F32)


def _bdot_nt(a, b):
    return lax.dot_general(a, b, (((2,), (2,)), ((0,), (0,))), precision=HI, preferred_element_type=F32)


def _bdot_tn(a, b):
    return lax.dot_general(a, b, (((1,), (1,)), ((0,), (0,))), precision=HI, preferred_element_type=F32)


def _unit_lower_inverse(a):
    c = a.shape[-1]
    ri, ci = _iota(a.shape, 1), _iota(a.shape, 2)
    inner = (ri // 2) == (ci // 2)
    t = (ri == ci).astype(F32) - jnp.where(inner, a, 0.0)
    blk = 4
    while blk <= c:
        outer = (ri // blk) == (ci // blk)
        low = jnp.where(outer & jnp.logical_not(inner), a, 0.0)
        t = t - _bdot(_bdot(t, low), t)
        inner = outer
        blk *= 2
    return t


def _stack(xs):
    return jnp.concatenate([x[None] for x in xs], axis=0)


def _gdn_local(q, k, v, gbs):
    b, c, _ = q.shape
    gcols, bcols = [], []
    for gb in gbs:
        lane = _iota(gb.shape, 1)
        for h in range(GDN_HEADS):
            gcols.append(jnp.sum(jnp.where(lane == h, gb, 0.0), axis=1, keepdims=True))
            bcols.append(jnp.sum(jnp.where(lane == GDN_HEADS + h, gb, 0.0), axis=1, keepdims=True))
    gcol, bcol = _stack(gcols), _stack(bcols)
    ri, ci = _iota((b, c, c), 1), _iota((b, c, c), 2)
    incl = ri >= ci
    tril = incl.astype(F32)
    g_cc = _bdot(tril, jnp.broadcast_to(gcol, (b, c, c)))
    g_row = _bdot(jnp.ones((b, c, c), F32), jnp.where(ri == ci, g_cc, 0.0))
    g_cl = _bdot(tril, jnp.broadcast_to(gcol, (b, c, HEAD)))
    g_last = jnp.sum(jnp.broadcast_to(gcol, (b, c, HEAD)), axis=1, keepdims=True)
    decay = jnp.where(incl, jnp.exp(jnp.where(incl, g_cc - g_row, 0.0)), 0.0)
    kk = _bdot_nt(k, k)
    minv = _unit_lower_inverse(jnp.where(ri > ci, bcol * kk * decay, 0.0))
    e_g = jnp.exp(g_cl)
    u = _bdot(minv, v * bcol)
    wk = _bdot(minv, k * (bcol * e_g))
    qk = _bdot_nt(q, k) * decay
    return u, wk, q * e_g, k * jnp.exp(g_last - g_cl), qk, jnp.exp(g_last)


def _gdn_scan(states, u, wk, qd, kd, qk, gl_tile):
    lane, row = _iota(gl_tile.shape, 1), _iota(gl_tile.shape, 0)
    gl = _stack([
        jnp.sum(jnp.sum(jnp.where((lane == h) & (row == 0), gl_tile, 0.0), axis=1, keepdims=True),
                axis=0, keepdims=True) for h in range(GDN_HEADS)])
    v_new = u - _bdot(wk, states)
    o = _bdot(qd, states) + _bdot(qk, v_new)
    return states * gl + _bdot_tn(kd, v_new), o


def _heads(x):
    return jnp.stack(_split(x, HW4))


GDN_W = GDN_HEADS * HEAD
HW4 = [HEAD] * GDN_HEADS
LOCAL_CHUNKS = 2
_CHUNK_ROWS = [pl.ds(cc * CHUNK, CHUNK) for cc in range(LOCAL_CHUNKS)]


def _chunk_heads(ref):
    return jnp.concatenate([_heads(ref[rows, :]) for rows in _CHUNK_ROWS], 0)


def _gdn_local_fwd(q, k, v, gb, name):
    s = q.shape[0]
    t = LOCAL_CHUNKS * CHUNK

    def body(q_ref, k_ref, v_ref, gb_ref, u_ref, wk_ref, qd_ref, kd_ref, qk_ref, gl_ref):
        u, wk, qd, kd, qk, gl = _gdn_local(_chunk_heads(q_ref), _chunk_heads(k_ref), _chunk_heads(v_ref),
                                           [gb_ref[rows, :] for rows in _CHUNK_ROWS])
        lane = _iota((CHUNK, LANES), 1)
        for cc, rows in enumerate(_CHUNK_ROWS):
            gl_tile = jnp.zeros((CHUNK, LANES), F32)
            for h in range(GDN_HEADS):
                b, cols = cc * GDN_HEADS + h, pl.ds(h * HEAD, HEAD)
                u_ref[rows, cols] = u[b]
                wk_ref[rows, cols] = wk[b]
                qd_ref[rows, cols] = qd[b]
                kd_ref[rows, cols] = kd[b]
                qk_ref[h, rows, :] = qk[b]
                gl_tile = gl_tile + jnp.where(lane == h, gl[b], 0.0)
            gl_ref[rows, :] = gl_tile

    row = pl.BlockSpec((t, GDN_W), lambda i: (i, 0))
    lane = pl.BlockSpec((t, LANES), lambda i: (i, 0))
    qks = pl.BlockSpec((GDN_HEADS, t, CHUNK), lambda i: (0, i, 0))
    return pl.pallas_call(
        body, name=name, grid=(s // t,),
        in_specs=[row, row, row, lane],
        out_specs=[row, row, row, row, qks, lane],
        out_shape=[jax.ShapeDtypeStruct((s, GDN_W), F32)] * 4
        + [jax.ShapeDtypeStruct((GDN_HEADS, s, CHUNK), F32), jax.ShapeDtypeStruct((s, LANES), F32)],
        compiler_params=_params(("parallel",)),
    )(q, k, v, gb)


def _gdn_local_bwd(q, k, v, gb, du, dwk, dqd, dkd, dqk, dgl, name):
    s = q.shape[0]
    t = LOCAL_CHUNKS * CHUNK

    def body(q_ref, k_ref, v_ref, gb_ref, du_ref, dwk_ref, dqd_ref, dkd_ref, dqk_ref, dgl_ref,
             dq_ref, dk_ref, dv_ref, dgb_ref):
        _, vjp = jax.vjp(_gdn_local, _chunk_heads(q_ref), _chunk_heads(k_ref), _chunk_heads(v_ref),
                         [gb_ref[rows, :] for rows in _CHUNK_ROWS])
        lane = _iota((CHUNK, LANES), 1)
        dqk = jnp.stack([dqk_ref[h, rows, :] for rows in _CHUNK_ROWS for h in range(GDN_HEADS)])
        dgl = jnp.stack([jnp.sum(jnp.where(lane == h, dgl_ref[rows, :], 0.0), axis=0, keepdims=True)
                         for rows in _CHUNK_ROWS for h in range(GDN_HEADS)])
        d_q, d_k, d_v, d_gbs = vjp((_chunk_heads(du_ref), _chunk_heads(dwk_ref), _chunk_heads(dqd_ref),
                                    _chunk_heads(dkd_ref), dqk, dgl))
        for cc, rows in enumerate(_CHUNK_ROWS):
            for h in range(GDN_HEADS):
                b, cols = cc * GDN_HEADS + h, pl.ds(h * HEAD, HEAD)
                dq_ref[rows, cols] = d_q[b]
                dk_ref[rows, cols] = d_k[b]
                dv_ref[rows, cols] = d_v[b]
            dgb_ref[rows, :] = d_gbs[cc]

    row = pl.BlockSpec((t, GDN_W), lambda i: (i, 0))
    lane = pl.BlockSpec((t, LANES), lambda i: (i, 0))
    qks = pl.BlockSpec((GDN_HEADS, t, CHUNK), lambda i: (0, i, 0))
    return pl.pallas_call(
        body, name=name, grid=(s // t,),
        in_specs=[row, row, row, lane, row, row, row, row, qks, lane],
        out_specs=[row, row, row, lane],
        out_shape=[jax.ShapeDtypeStruct((s, GDN_W), F32)] * 3 + [jax.ShapeDtypeStruct((s, LANES), F32)],
        compiler_params=_params(("parallel",)),
    )(q, k, v, gb, du, dwk, dqd, dkd, dqk, dgl)


def _gdn_scan_fwd(u, wk, qd, kd, qk, gl, name):
    s = u.shape[0]
    nc = s // CHUNK

    def body(u_ref, wk_ref, qd_ref, kd_ref, qk_ref, gl_ref, o_ref, st_ref, state):
        i = pl.program_id(0)

        @pl.when(i == 0)
        def _():
            state[...] = jnp.zeros(state.shape, F32)

        st_ref[...] = state[...]
        new_states, o = _gdn_scan(state[...], _heads(u_ref[...]), _heads(wk_ref[...]), _heads(qd_ref[...]),
                                  _heads(kd_ref[...]), qk_ref[...], gl_ref[...])
        state[...] = new_states
        o_ref[...] = jnp.concatenate([o[h] for h in range(GDN_HEADS)], 1)

    row = pl.BlockSpec((CHUNK, GDN_W), lambda i: (i, 0))
    return pl.pallas_call(
        body, name=name, grid=(nc,),
        in_specs=[row, row, row, row, pl.BlockSpec((GDN_HEADS, CHUNK, CHUNK), lambda i: (0, i, 0)),
                  pl.BlockSpec((CHUNK, LANES), lambda i: (i, 0))],
        out_specs=[row, pl.BlockSpec((None, GDN_HEADS, HEAD, HEAD), lambda i: (i, 0, 0, 0))],
        out_shape=[jax.ShapeDtypeStruct((s, GDN_W), F32),
                   jax.ShapeDtypeStruct((nc, GDN_HEADS, HEAD, HEAD), F32)],
        scratch_shapes=[pltpu.VMEM((GDN_HEADS, HEAD, HEAD), F32)],
        compiler_params=_params(("arbitrary",)),
    )(u, wk, qd, kd, qk, gl)


def _gdn_scan_bwd(u, wk, qd, kd, qk, gl, st, do, name):
    s = u.shape[0]
    nc = s // CHUNK

    def body(u_ref, wk_ref, qd_ref, kd_ref, qk_ref, gl_ref, st_ref, do_ref,
             du_ref, dwk_ref, dqd_ref, dkd_ref, dqk_ref, dgl_ref, dstate):
        i = pl.program_id(0)

        @pl.when(i == 0)
        def _():
            dstate[...] = jnp.zeros(dstate.shape, F32)

        _, vjp = jax.vjp(_gdn_scan, st_ref[...], _heads(u_ref[...]), _heads(wk_ref[...]), _heads(qd_ref[...]),
                         _heads(kd_ref[...]), qk_ref[...], gl_ref[...])
        d_states, d_u, d_wk, d_qd, d_kd, d_qk, d_gl = vjp((dstate[...], _heads(do_ref[...])))
        dstate[...] = d_states
        dqk_ref[...] = d_qk
        unheads = lambda x: jnp.concatenate([x[h] for h in range(GDN_HEADS)], 1)
        du_ref[...] = unheads(d_u)
        dwk_ref[...] = unheads(d_wk)
        dqd_ref[...] = unheads(d_qd)
        dkd_ref[...] = unheads(d_kd)
        dgl_ref[...] = d_gl

    rev = lambda i: (nc - 1 - i, 0)
    row = pl.BlockSpec((CHUNK, GDN_W), rev)
    lane = pl.BlockSpec((CHUNK, LANES), rev)
    qks = pl.BlockSpec((GDN_HEADS, CHUNK, CHUNK), lambda i: (0, nc - 1 - i, 0))
    return pl.pallas_call(
        body, name=name, grid=(nc,),
        in_specs=[row, row, row, row, qks, lane,
                  pl.BlockSpec((None, GDN_HEADS, HEAD, HEAD), lambda i: (nc - 1 - i, 0, 0, 0)), row],
        out_specs=[row, row, row, row, qks, lane],
        out_shape=[jax.ShapeDtypeStruct((s, GDN_W), F32)] * 4
        + [jax.ShapeDtypeStruct((GDN_HEADS, s, CHUNK), F32), jax.ShapeDtypeStruct((s, LANES), F32)],
        scratch_shapes=[pltpu.VMEM((GDN_HEADS, HEAD, HEAD), F32)],
        compiler_params=_params(("arbitrary",)),
    )(u, wk, qd, kd, qk, gl, st, do)


def _chunk_mask(i, j, t):
    r = i * t + _iota((t, t), 0)
    c = j * t + _iota((t, t), 1)
    return (r // CHUNK) >= (c // CHUNK)


def _attn_fwd(q, k, v, name):
    s = q.shape[0]
    t = min(512, s)
    n = s // t

    def body(q_ref, k_ref, v_ref, o_ref, lse_ref, m_sc, l_sc, acc_sc):
        i, j = pl.program_id(1), pl.program_id(2)

        @pl.when(j == 0)
        def _():
            m_sc[...] = jnp.full(m_sc.shape, -jnp.inf, F32)
            l_sc[...] = jnp.zeros(l_sc.shape, F32)
            acc_sc[...] = jnp.zeros(acc_sc.shape, F32)

        @pl.when(j <= i)
        def _():
            sc = lax.dot_general(q_ref[...], k_ref[...], (((1,), (1,)), ((), ())), preferred_element_type=F32)
            sc = jnp.where(_chunk_mask(i, j, t), sc, -jnp.inf)
            m_prev = m_sc[:, :1]
            m_new = jnp.maximum(m_prev, jnp.max(sc, axis=1, keepdims=True))
            alpha = jnp.exp(m_prev - m_new)
            p = jnp.exp(sc - m_new)
            l_sc[...] = jnp.broadcast_to(alpha * l_sc[:, :1] + jnp.sum(p, axis=1, keepdims=True), l_sc.shape)
            acc_sc[...] = alpha * acc_sc[...] + jnp.dot(p.astype(BF16), v_ref[...], preferred_element_type=F32)
            m_sc[...] = jnp.broadcast_to(m_new, m_sc.shape)

        @pl.when(j == n - 1)
        def _():
            o_ref[...] = acc_sc[...] / l_sc[:, :1]
            lse_ref[...] = m_sc[...] + jnp.log(l_sc[...])

    return pl.pallas_call(
        body, name=name, grid=(MLA_HEADS, n, n),
        in_specs=[pl.BlockSpec((t, QK_PAD), lambda h, i, j: (i, h)),
                  pl.BlockSpec((t, QK_PAD), lambda h, i, j: (jnp.minimum(j, i), h)),
                  pl.BlockSpec((t, HEAD), lambda h, i, j: (jnp.minimum(j, i), h))],
        out_specs=[pl.BlockSpec((t, HEAD), lambda h, i, j: (i, h)),
                   pl.BlockSpec((None, t, LANES), lambda h, i, j: (h, i, 0))],
        out_shape=[jax.ShapeDtypeStruct((s, MLA_HEADS * HEAD), F32),
                   jax.ShapeDtypeStruct((MLA_HEADS, s, LANES), F32)],
        scratch_shapes=[pltpu.VMEM((t, LANES), F32), pltpu.VMEM((t, LANES), F32), pltpu.VMEM((t, HEAD), F32)],
        compiler_params=_params(("parallel", "parallel", "arbitrary")),
    )(q, k, v)


def _attn_probs(q_ref, k_ref, v_ref, o_ref, do_ref, lse_ref, i, j, t):
    sc = lax.dot_general(q_ref[...], k_ref[...], (((1,), (1,)), ((), ())), preferred_element_type=F32)
    p = jnp.where(_chunk_mask(i, j, t), jnp.exp(sc - lse_ref[:, :1]), 0.0)
    do = do_ref[...]
    dp = lax.dot_general(do.astype(BF16), v_ref[...], (((1,), (1,)), ((), ())), preferred_element_type=F32)
    dd = jnp.sum(do * o_ref[...], axis=1, keepdims=True)
    return p, p * (dp - dd)


def _attn_bwd_dq(q, k, v, o, do, lse, name):
    s = q.shape[0]
    t = min(512, s)
    n = s // t

    def body(q_ref, k_ref, v_ref, o_ref, do_ref, lse_ref, dq_ref, acc):
        i, j = pl.program_id(1), pl.program_id(2)

        @pl.when(j == 0)
        def _():
            acc[...] = jnp.zeros(acc.shape, F32)

        @pl.when(j <= i)
        def _():
            _, ds = _attn_probs(q_ref, k_ref, v_ref, o_ref, do_ref, lse_ref, i, j, t)
            acc[...] += jnp.dot(ds.astype(BF16), k_ref[...], preferred_element_type=F32)

        @pl.when(j == n - 1)
        def _():
            dq_ref[...] = acc[...]

    qrow = lambda h, i, j: (i, h)
    krow = lambda h, i, j: (jnp.minimum(j, i), h)
    return pl.pallas_call(
        body, name=name, grid=(MLA_HEADS, n, n),
        in_specs=[pl.BlockSpec((t, QK_PAD), qrow), pl.BlockSpec((t, QK_PAD), krow), pl.BlockSpec((t, HEAD), krow),
                  pl.BlockSpec((t, HEAD), qrow), pl.BlockSpec((t, HEAD), qrow),
                  pl.BlockSpec((None, t, LANES), lambda h, i, j: (h, i, 0))],
        out_specs=pl.BlockSpec((t, QK_PAD), qrow),
        out_shape=jax.ShapeDtypeStruct((s, MLA_HEADS * QK_PAD), F32),
        scratch_shapes=[pltpu.VMEM((t, QK_PAD), F32)],
        compiler_params=_params(("parallel", "parallel", "arbitrary")),
    )(q, k, v, o, do, lse)


def _attn_bwd_dkv(q, k, v, o, do, lse, name):
    s = q.shape[0]
    t = min(512, s)
    n = s // t

    def body(q_ref, k_ref, v_ref, o_ref, do_ref, lse_ref, dk_ref, dv_ref, dk_acc, dv_acc):
        j, i = pl.program_id(1), pl.program_id(2)

        @pl.when(i == 0)
        def _():
            dk_acc[...] = jnp.zeros(dk_acc.shape, F32)
            dv_acc[...] = jnp.zeros(dv_acc.shape, F32)

        @pl.when(i >= j)
        def _():
            p, ds = _attn_probs(q_ref, k_ref, v_ref, o_ref, do_ref, lse_ref, i, j, t)
            tn = (((0,), (0,)), ((), ()))
            dv_acc[...] += lax.dot_general(p.astype(BF16), do_ref[...].astype(BF16), tn, preferred_element_type=F32)
            dk_acc[...] += lax.dot_general(ds.astype(BF16), q_ref[...], tn, preferred_element_type=F32)

        @pl.when(i == n - 1)
        def _():
            dk_ref[...] = dk_acc[...]
            dv_ref[...] = dv_acc[...]

    qrow = lambda h, j, i: (jnp.maximum(i, j), h)
    krow = lambda h, j, i: (j, h)
    return pl.pallas_call(
        body, name=name, grid=(MLA_HEADS, n, n),
        in_specs=[pl.BlockSpec((t, QK_PAD), qrow), pl.BlockSpec((t, QK_PAD), krow), pl.BlockSpec((t, HEAD), krow),
                  pl.BlockSpec((t, HEAD), qrow), pl.BlockSpec((t, HEAD), qrow),
                  pl.BlockSpec((None, t, LANES), lambda h, j, i: (h, jnp.maximum(i, j), 0))],
        out_specs=[pl.BlockSpec((t, QK_PAD), krow), pl.BlockSpec((t, HEAD), krow)],
        out_shape=[jax.ShapeDtypeStruct((s, MLA_HEADS * QK_PAD), F32),
                   jax.ShapeDtypeStruct((s, MLA_HEADS * HEAD), F32)],
        scratch_shapes=[pltpu.VMEM((t, QK_PAD), F32), pltpu.VMEM((t, HEAD), F32)],
        compiler_params=_params(("parallel", "parallel", "arbitrary")),
    )(q, k, v, o, do, lse)


def _place():
    return lax.axis_index("x"), lax.axis_index("y"), lax.axis_index("c")


def _allgather8(x, name):
    r, c = x.shape

    def body(x_ref, out_ref, send_sems, recv_sems, local_sem):
        mx, my, mc = _place()
        me = 4 * mx + 2 * my + mc
        mine = pltpu.make_async_copy(x_ref, out_ref.at[me], local_sem)
        mine.start()
        copies = []
        for d in range(1, 8):
            px = 1 - mx if d & 4 else mx
            py = 1 - my if d & 2 else my
            pc = 1 - mc if d & 1 else mc
            cp = pltpu.make_async_remote_copy(
                src_ref=x_ref, dst_ref=out_ref.at[me], send_sem=send_sems.at[d - 1], recv_sem=recv_sems.at[d - 1],
                device_id=(px, py, pc), device_id_type=MESH)
            cp.start()
            copies.append(cp)
        for cp in copies:
            cp.wait()
        mine.wait()

    return pl.pallas_call(
        body, name=name,
        out_shape=jax.ShapeDtypeStruct((8, r, c), x.dtype),
        in_specs=[pl.BlockSpec(memory_space=pltpu.VMEM)],
        out_specs=pl.BlockSpec(memory_space=pltpu.VMEM),
        scratch_shapes=[pltpu.SemaphoreType.DMA((7,)), pltpu.SemaphoreType.DMA((7,)), pltpu.SemaphoreType.DMA],
        compiler_params=pltpu.CompilerParams(vmem_limit_bytes=VMEM_LIMIT),
    )(x)


def _allgather_chips(x, name):
    r, c = x.shape
    rh = r // 2

    def body(x_ref, out_ref, send_sems, recv_sems, local_sem):
        mx, my, mc = _place()
        j = 2 * mx + my
        chips = [(1 - mx, my), (mx, 1 - my), (1 - mx, 1 - my)]

        def half(jj, hc):
            return out_ref.at[jj, pl.ds(hc * rh, rh), :]

        mine = pltpu.make_async_copy(x_ref, out_ref.at[j], local_sem)
        mine.start()
        first = []
        for kk, (px, py) in enumerate(chips):
            cp = pltpu.make_async_remote_copy(
                src_ref=x_ref.at[pl.ds(mc * rh, rh), :], dst_ref=half(j, mc),
                send_sem=send_sems.at[kk], recv_sem=recv_sems.at[kk], device_id=(px, py, mc), device_id_type=MESH)
            cp.start()
            first.append(cp)
        passed = []
        for kk, (px, py) in enumerate(chips):
            jj = 2 * px + py
            pltpu.make_async_remote_copy(
                src_ref=x_ref.at[pl.ds(mc * rh, rh), :], dst_ref=half(jj, mc),
                send_sem=send_sems.at[kk], recv_sem=recv_sems.at[kk], device_id=(px, py, mc),
                device_id_type=MESH).wait_recv()
            cp = pltpu.make_async_remote_copy(
                src_ref=half(jj, mc), dst_ref=half(jj, mc), send_sem=send_sems.at[3 + kk],
                recv_sem=recv_sems.at[3 + kk], device_id=(mx, my, 1 - mc), device_id_type=MESH)
            cp.start()
            passed.append(cp)
        for kk, (px, py) in enumerate(chips):
            jj = 2 * px + py
            pltpu.make_async_remote_copy(
                src_ref=half(jj, 1 - mc), dst_ref=half(jj, 1 - mc), send_sem=send_sems.at[3 + kk],
                recv_sem=recv_sems.at[3 + kk], device_id=(mx, my, 1 - mc), device_id_type=MESH).wait_recv()
        for cp in first + passed:
            cp.wait_send()
        mine.wait()

    return pl.pallas_call(
        body, name=name,
        out_shape=jax.ShapeDtypeStruct((4, r, c), x.dtype),
        in_specs=[pl.BlockSpec(memory_space=pltpu.VMEM)],
        out_specs=pl.BlockSpec(memory_space=pltpu.VMEM),
        scratch_shapes=[pltpu.SemaphoreType.DMA((6,)), pltpu.SemaphoreType.DMA((6,)), pltpu.SemaphoreType.DMA],
        compiler_params=pltpu.CompilerParams(vmem_limit_bytes=VMEM_LIMIT),
    )(x)


RS_ROWS = 32


def _reduce_scatter_chips(g, name):
    _, r, c = g.shape
    rh = r // 2
    steps = rh // RS_ROWS

    def body(g_ref, out_ref, sib_ref, part_ref, got_ref, send_sems, recv_sems):
        mx, my, mc = _place()
        j = 2 * mx + my
        sibling = (mx, my, 1 - mc)
        chips = [(1 - mx, my), (mx, 1 - my), (1 - mx, 1 - my)]

        to_sib = pltpu.make_async_remote_copy(
            src_ref=g_ref.at[:, pl.ds((1 - mc) * rh, rh), :], dst_ref=sib_ref,
            send_sem=send_sems.at[0], recv_sem=recv_sems.at[0], device_id=sibling, device_id_type=MESH)
        to_sib.start()
        to_sib.wait()

        def add_sibling(step, carry):
            rows = pl.ds(pl.multiple_of(step * RS_ROWS, RS_ROWS), RS_ROWS)
            mine = g_ref[:, pl.ds(pl.multiple_of(mc * rh + step * RS_ROWS, RS_ROWS), RS_ROWS), :]
            part_ref[:, rows, :] = mine.astype(F32) + sib_ref[:, rows, :].astype(F32)
            return carry

        lax.fori_loop(0, steps, add_sibling, 0)

        def to_bf16(step, carry):
            rows = pl.ds(pl.multiple_of(step * RS_ROWS, RS_ROWS), RS_ROWS)
            sib_ref[:, rows, :] = part_ref[:, rows, :].astype(BF16)
            return carry

        lax.fori_loop(0, steps, to_bf16, 0)

        sends = []
        for kk, (px, py) in enumerate(chips):
            cp = pltpu.make_async_remote_copy(
                src_ref=sib_ref.at[2 * px + py], dst_ref=got_ref.at[kk],
                send_sem=send_sems.at[1 + kk], recv_sem=recv_sems.at[1 + kk],
                device_id=(px, py, mc), device_id_type=MESH)
            cp.start()
            sends.append(cp)
        for cp in sends:
            cp.wait()

        def total(step, carry):
            rows = pl.ds(pl.multiple_of(step * RS_ROWS, RS_ROWS), RS_ROWS)
            acc = part_ref[j, rows, :]
            for kk in range(3):
                acc = acc + got_ref[kk, rows, :].astype(F32)
            out_ref[pl.ds(pl.multiple_of(mc * rh + step * RS_ROWS, RS_ROWS), RS_ROWS), :] = acc
            return carry

        lax.fori_loop(0, steps, total, 0)

        done = pltpu.make_async_remote_copy(
            src_ref=out_ref.at[pl.ds(mc * rh, rh), :], dst_ref=out_ref.at[pl.ds(mc * rh, rh), :],
            send_sem=send_sems.at[4], recv_sem=recv_sems.at[4], device_id=sibling, device_id_type=MESH)
        done.start()
        done.wait_send()
        pltpu.make_async_remote_copy(
            src_ref=out_ref.at[pl.ds((1 - mc) * rh, rh), :], dst_ref=out_ref.at[pl.ds((1 - mc) * rh, rh), :],
            send_sem=send_sems.at[4], recv_sem=recv_sems.at[4], device_id=sibling, device_id_type=MESH).wait_recv()

    return pl.pallas_call(
        body, name=name,
        out_shape=jax.ShapeDtypeStruct((r, c), F32),
        in_specs=[pl.BlockSpec(memory_space=pltpu.VMEM)],
        out_specs=pl.BlockSpec(memory_space=pltpu.VMEM),
        scratch_shapes=[pltpu.VMEM((4, rh, c), BF16), pltpu.VMEM((4, rh, c), F32), pltpu.VMEM((3, rh, c), BF16),
                        pltpu.SemaphoreType.DMA((5,)), pltpu.SemaphoreType.DMA((5,))],
        compiler_params=pltpu.CompilerParams(vmem_limit_bytes=VMEM_LIMIT),
    )(g)


def _sum8(x, name):
    _, r, c = x.shape

    def body(x_ref, o_ref):
        acc = x_ref[0]
        for d in range(1, 8):
            acc = acc + x_ref[d]
        o_ref[...] = acc

    return pl.pallas_call(
        body, name=name, out_shape=jax.ShapeDtypeStruct((r, c), F32),
        in_specs=[pl.BlockSpec(memory_space=pltpu.VMEM)], out_specs=pl.BlockSpec(memory_space=pltpu.VMEM),
    )(x)


def _adamw(w, g, m, v, name):
    r, c = w.shape
    t = _pick(r, 256, SUBLANES)
    spec = pl.BlockSpec((t, c), lambda i: (i, 0))

    def body(w_ref, g_ref, m_ref, v_ref, d_ref, nm_ref, nv_ref):
        gv = g_ref[...]
        m_new = ADAM_B1 * m_ref[...] + (1.0 - ADAM_B1) * gv
        v_new = ADAM_B2 * v_ref[...] + (1.0 - ADAM_B2) * (gv * gv)
        m_hat = m_new / (1.0 - ADAM_B1 ** ADAM_STEP)
        v_hat = v_new / (1.0 - ADAM_B2 ** ADAM_STEP)
        d_ref[...] = -ADAM_LR * (m_hat / (jnp.sqrt(v_hat) + ADAM_EPS) + ADAM_WD * w_ref[...])
        nm_ref[...] = m_new
        nv_ref[...] = v_new

    return pl.pallas_call(
        body, name=name, grid=(r // t,), in_specs=[spec] * 4, out_specs=[spec] * 3,
        out_shape=[jax.ShapeDtypeStruct((r, c), F32)] * 3, compiler_params=_params(("parallel",)),
    )(w, g, m, v)


def _pack_rows(parts):
    rows, offs, o = [], [], 0
    for p in parts:
        f = p.reshape(-1)
        n = -(-f.shape[0] // (LANES * SUBLANES)) * SUBLANES
        rows.append(jnp.pad(f, (0, n * LANES - f.shape[0])).reshape(n, LANES))
        offs.append((o, n))
        o += n
    return jnp.concatenate(rows, 0), offs


def _unpack_rows(packed, offs, shapes):
    out = []
    for (o, n), shp in zip(offs, shapes):
        size = 1
        for d in shp:
            size *= d
        out.append(packed[o:o + n].reshape(-1)[:size].reshape(shp))
    return out


def _ffn_fwd(x, s, sh, g, w_in, w_out, tag):
    (h,) = _rowcall(lambda r, p: ([_modulate(r[0], p[0], p[1])], []), [x], [s, sh], [(x.shape[1], BF16)], [],
                    tile=512, name=tag + "_mod")
    gu = _mm(h, w_in, "nn", BF16, tag + "_in")
    (act,) = _rowcall(lambda r, p: ([_silu(r[0].astype(F32)) * r[1].astype(F32)], []),
                      [(gu, D_FF, 0), (gu, D_FF, 1)], [], [(D_FF, BF16)], [], tile=256, name=tag + "_act")
    f = _mm(act, w_out, "nn", F32, tag + "_out")
    (y,) = _rowcall(lambda r, p: ([r[0] + 0.5 * p[0] * r[1]], []), [x, f], [g], [(x.shape[1], F32)], [],
                    tile=512, name=tag + "_res")
    return y, (x, h, gu, act, f)


def _ffn_bwd(dy, saved, s, sh, g, w_in, w_out, tag):
    x, h, gu, act, f = saved
    d = x.shape[1]
    df, dg = _rowcall(lambda r, p: ([0.5 * p[0] * r[0]], [0.5 * jnp.sum(r[0] * r[1], 0, keepdims=True)]),
                      [dy, f], [g], [(d, BF16)], [(1, d)], tile=512, name=tag + "_bres")
    da = _mm(df, w_out, "nt", BF16, tag + "_bout")
    dw_out = _mm(act, df, "tn", BF16, tag + "_bwout")

    def act_bwd(r, p):
        gate, up, dav = r[0].astype(F32), r[1].astype(F32), r[2].astype(F32)
        _, vjp = jax.vjp(lambda a, b: _silu(a) * b, gate, up)
        dgate, dup = vjp(dav)
        return [jnp.concatenate([dgate, dup], 1)], []

    (dgu,) = _rowcall(act_bwd, [(gu, D_FF, 0), (gu, D_FF, 1), da], [], [(2 * D_FF, BF16)], [], tile=256,
                      name=tag + "_bact")
    dh = _mm(dgu, w_in, "nt", F32, tag + "_bin")
    dw_in = _mm(h, dgu, "tn", BF16, tag + "_bwin")

    def mod_bwd(r, p):
        _, vjp = jax.vjp(_modulate, r[0], p[0], p[1])
        dx, ds, dsh = vjp(r[1])
        return [r[2] + dx], [ds, dsh]

    dx, ds, dsh = _rowcall(mod_bwd, [x, dh, dy], [s, sh], [(d, F32)], [(1, d), (1, d)], tile=512, name=tag + "_bmod")
    return dx, (dsh, ds, dg), dw_in, dw_out


def _mixer_fwd(x, s, sh, g, wts, rope):
    w_in_p, conv8, a_log, dt_bias, wn, wq, w_uq_p, wkv, w_ukv, wqn, wqr, wkn, wkr, won, w_out = wts
    cos2, sin2 = rope
    d = x.shape[1]
    (h,) = _rowcall(lambda r, p: ([_modulate(r[0], p[0], p[1])], []), [x], [s, sh], [(d, BF16)], [],
                    tile=512, name="mix_mod")
    proj = _mm(h, w_in_p, "nn", F32, "mix_in")
    qkv_c = _conv_fwd(proj, conv8, "mix_conv")
    gab = (proj, LANES, 23)

    q, k, v, gb = _rowcall(
        lambda r, p: (list(_gdn_prep_core(_split(r[0], [HEAD] * 12), r[1], p[0], p[1])), []),
        [qkv_c, gab], [a_log, dt_bias], [(512, F32)] * 3 + [(LANES, F32)], [], tile=256, name="mix_gdn_prep")
    gdn_local = _gdn_local_fwd(q, k, v, gb, "mix_gdn_local")
    o_gdn, gdn_states = _gdn_scan_fwd(*gdn_local, "mix_gdn_scan")
    states = (gdn_local, gdn_states)

    cq, ckv, kr = (proj, 512, 4), (proj, 256, 10), (proj, LANES, 22)
    cqn, ckvn, k_rope = _rowcall(
        lambda r, p: (list(_mla_prep_core(r[0][:, :MLA_Q_LORA], r[1], r[2], r[3], r[4], p[0], p[1], p[2])), []),
        [cq, ckv, kr, cos2, sin2], [wq, wkv, wkr], [(MLA_Q_LORA, BF16), (MLA_KV_LORA, BF16), (LANES, F32)], [],
        tile=512, name="mix_mla_prep")
    qf = _mm(cqn, w_uq_p, "nn", F32, "mix_uq")
    kvf = _mm(ckvn, w_ukv, "nn", F32, "mix_ukv")

    def qk_prep(r, p):
        qparts = _split(r[0], [HEAD] * 8)
        kvparts = _split(r[1], [HEAD] * 8)
        return list(_qk_prep_core(qparts[:4], qparts[4:], kvparts[0::2], kvparts[1::2], r[2], r[3], r[4],
                                  p[0], p[1], p[2])), []

    qa, ka, va = _rowcall(qk_prep, [qf, kvf, k_rope, cos2, sin2], [wqn, wqr, wkn],
                          [(4 * QK_PAD, BF16), (4 * QK_PAD, BF16), (4 * HEAD, BF16)], [], tile=256,
                          name="mix_qk_prep")
    o_b, lse = _attn_fwd(qa, ka, va, "mix_attn")

    gz = (proj, 512, 3)
    (mixed,) = _rowcall(
        lambda r, p: ([_mix_post_core(_split(r[0], HW4), _split(r[1], HW4), _split(r[2], HW4), p[0], p[1])], []),
        [o_gdn, gz, o_b], [wn, won], [(2 * 512, BF16)], [], tile=512, name="mix_post")
    y = _mm(mixed, w_out, "nn", F32, "mix_out")
    (x_out,) = _rowcall(lambda r, p: ([r[0] + p[0] * r[1]], []), [x, y], [g], [(d, F32)], [], tile=512,
                        name="mix_res")
    saved = (x, h, proj, qkv_c, q, k, v, gb, states, o_gdn, cqn, ckvn, k_rope, qf, kvf, qa, ka, va, o_b, lse,
             mixed, y)
    return x_out, saved


def _mixer_bwd(dy, saved, s, sh, g, wts, rope):
    w_in_p, conv8, a_log, dt_bias, wn, wq, w_uq_p, wkv, w_ukv, wqn, wqr, wkn, wkr, won, w_out = wts
    cos2, sin2 = rope
    (x, h, proj, qkv_c, q, k, v, gb, states, o_gdn, cqn, ckvn, k_rope, qf, kvf, qa, ka, va, o_b, lse,
     mixed, y) = saved
    d = x.shape[1]
    dyb, dg = _rowcall(lambda r, p: ([p[0] * r[0]], [jnp.sum(r[0] * r[1], 0, keepdims=True)]),
                       [dy, y], [g], [(d, BF16)], [(1, d)], tile=512, name="mix_bres")
    dmixed = _mm(dyb, w_out, "nt", F32, "mix_bout")
    dw_out = _mm(mixed, dyb, "tn", BF16, "mix_bwout")

    gz = (proj, 512, 3)

    def post_bwd(r, p):
        _, vjp = jax.vjp(_mix_post_core, _split(r[0], HW4), _split(r[1], HW4), _split(r[2], HW4), p[0], p[1])
        do, dz, dob, dwn, dwon = vjp(r[3])
        return [jnp.concatenate(do, 1), jnp.concatenate(dz, 1), jnp.concatenate(dob, 1)], [dwn, dwon]

    do_gdn, dgz, do_b, dwn, dwon = _rowcall(post_bwd, [o_gdn, gz, o_b, dmixed], [wn, won], [(512, F32)] * 3,
                                            [(1, HEAD), (1, HEAD)], tile=256, name="mix_bpost")

    dqa = _attn_bwd_dq(qa, ka, va, o_b, do_b, lse, "mix_battn_dq")
    dka, dva = _attn_bwd_dkv(qa, ka, va, o_b, do_b, lse, "mix_battn_dkv")

    def qk_bwd(r, p):
        qparts = _split(r[0], [HEAD] * 8)
        kvparts = _split(r[1], [HEAD] * 8)
        _, vjp = jax.vjp(_qk_prep_core, qparts[:4], qparts[4:], kvparts[0::2], kvparts[1::2], r[2], r[3], r[4],
                         p[0], p[1], p[2])
        dqn, dqr, dkn, dvp, dkrope, _, _, dwqn, dwqr, dwkn = vjp((r[5], r[6], r[7]))
        dkv = []
        for a, b in zip(dkn, dvp):
            dkv += [a, b]
        return [jnp.concatenate(list(dqn) + list(dqr), 1), jnp.concatenate(dkv, 1), dkrope], [dwqn, dwqr, dwkn]

    dqf, dkvf, dk_rope, dwqn, dwqr, dwkn = _rowcall(
        qk_bwd, [qf, kvf, k_rope, cos2, sin2, dqa, dka, dva], [wqn, wqr, wkn],
        [(8 * HEAD, BF16), (8 * HEAD, BF16), (LANES, F32)], [(1, HEAD)] * 3, tile=256, name="mix_bqk_prep")
    dcqn = _mm(dqf, w_uq_p, "nt", F32, "mix_buq")
    dw_uq_p = _mm(cqn, dqf, "tn", F32, "mix_bwuq")
    dckvn = _mm(dkvf, w_ukv, "nt", F32, "mix_bukv")
    dw_ukv = _mm(ckvn, dkvf, "tn", F32, "mix_bwukv")

    cq, ckv, kr = (proj, 512, 4), (proj, 256, 10), (proj, LANES, 22)

    def mla_bwd(r, p):
        _, vjp = jax.vjp(_mla_prep_core, r[0][:, :MLA_Q_LORA], r[1], r[2], r[3], r[4], p[0], p[1], p[2])
        dcq, dckv, dkr, _, _, dwq, dwkv, dwkr = vjp((r[5], r[6], r[7]))
        pad = jnp.zeros((dcq.shape[0], 512 - MLA_Q_LORA), F32)
        return [jnp.concatenate([dcq, pad], 1), dckv, dkr], [dwq, dwkv, dwkr]

    dcq, dckv, dkr, dwq, dwkv, dwkr = _rowcall(
        mla_bwd, [cq, ckv, kr, cos2, sin2, dcqn, dckvn, dk_rope], [wq, wkv, wkr],
        [(512, F32), (MLA_KV_LORA, F32), (LANES, F32)], [(1, MLA_Q_LORA), (1, MLA_KV_LORA), (1, LANES)],
        tile=512, name="mix_bmla_prep")

    gdn_local, gdn_states = states
    d_local = _gdn_scan_bwd(*gdn_local, gdn_states, do_gdn, "mix_bgdn_scan")
    dq, dk, dv, dgb = _gdn_local_bwd(q, k, v, gb, *d_local, "mix_bgdn_local")
    gab = (proj, LANES, 23)

    def gdn_prep_bwd(r, p):
        _, vjp = jax.vjp(_gdn_prep_core, _split(r[0], [HEAD] * 12), r[1], p[0], p[1])
        dparts, dgab, da_log, ddt = vjp((r[2], r[3], r[4], r[5]))
        return [jnp.concatenate(dparts, 1), dgab], [da_log, ddt]

    dqkv_c, dgab, da_log, ddt = _rowcall(gdn_prep_bwd, [qkv_c, gab, dq, dk, dv, dgb], [a_log, dt_bias],
                                         [(1536, F32), (LANES, F32)], [(1, LANES), (1, LANES)], tile=256,
                                         name="mix_bgdn_prep")
    dqkv_pre, dconv8 = _conv_bwd(proj, dqkv_c, conv8, "mix_bconv")

    dproj = jnp.concatenate([dqkv_pre.astype(BF16), dgz.astype(BF16), dcq.astype(BF16), dckv.astype(BF16),
                             dkr.astype(BF16), dgab.astype(BF16)], axis=1)
    dh = _mm(dproj, w_in_p, "nt", F32, "mix_bin")
    dw_in_p = _mm(h, dproj, "tn", F32, "mix_bwin")

    def mod_bwd(r, p):
        _, vjp = jax.vjp(_modulate, r[0], p[0], p[1])
        dx, ds, dsh = vjp(r[1])
        return [r[2] + dx], [ds, dsh]

    dx, ds, dsh = _rowcall(mod_bwd, [x, dh, dy], [s, sh], [(d, F32)], [(1, d), (1, d)], tile=512, name="mix_bmod")
    small = dict(conv=dconv8, a_log=da_log, dt=ddt, wn=dwn, wq=dwq, wkv=dwkv, wqn=dwqn, wqr=dwqr, wkn=dwkn,
                 wkr=dwkr, won=dwon)
    return dx, (dsh, ds, dg), dw_in_p, dw_uq_p, dw_ukv, dw_out, small


def _pad_cols(a, n):
    return jnp.pad(a, ((0, 0),) * (a.ndim - 1) + ((0, n - a.shape[-1]),))


def _pack_w_in(w):
    z = lambda n: jnp.zeros((w.shape[0], n), w.dtype)
    return jnp.concatenate([w[:, 0:2048], w[:, 2056:2440], z(128), w[:, 2440:2696], w[:, 2696:2760], z(64),
                            w[:, 2048:2056], z(120)], axis=1)


def _unpack_w_in(wp):
    return jnp.concatenate([wp[:, 0:2048], wp[:, 2944:2952], wp[:, 2048:2432], wp[:, 2560:2816], wp[:, 2816:2880]],
                           axis=1)


def _pack_w_uq(w):
    z = jnp.zeros((w.shape[0], LANES - MLA_ROPE), w.dtype)
    nope = [w[:, h * 192:h * 192 + HEAD] for h in range(MLA_HEADS)]
    rope = []
    for h in range(MLA_HEADS):
        rope += [w[:, h * 192 + HEAD:(h + 1) * 192], z]
    return jnp.concatenate(nope + rope, axis=1)


def _unpack_w_uq(wp):
    cols = []
    for h in range(MLA_HEADS):
        cols += [wp[:, h * HEAD:(h + 1) * HEAD], wp[:, 512 + h * LANES:512 + h * LANES + MLA_ROPE]]
    return jnp.concatenate(cols, axis=1)


def _cols_to_chips(a):
    r, c = a.shape
    return a.reshape(r, 4, c // 4).transpose(1, 0, 2)


def _chips_to_cols(a):
    _, r, n = a.shape
    return a.transpose(1, 0, 2).reshape(r, 4 * n)


def _pad128(v, n=LANES):
    return _pad_cols(v.reshape(1, -1), n)


def kernel(x, c, positions, w_ada, b_ada, ffn1_w_in, ffn1_w_out, w_in, gdn_conv_w, gdn_a_log, gdn_dt_bias, gdn_norm_w, mla_q_norm_w, mla_w_uq, mla_kv_norm_w, mla_w_ukv, qkn_q_nope, qkn_q_rope, qkn_k_nope, qkn_k_rope, mla_out_norm_w, w_out, ffn2_w_in, ffn2_w_out, loss_target, m_w_ada, m_b_ada, m_ffn1_w_in, m_ffn1_w_out, m_w_in, m_gdn_conv_w, m_gdn_a_log, m_gdn_dt_bias, m_gdn_norm_w, m_mla_q_norm_w, m_mla_w_uq, m_mla_kv_norm_w, m_mla_w_ukv, m_qkn_q_nope, m_qkn_q_rope, m_qkn_k_nope, m_qkn_k_rope, m_mla_out_norm_w, m_w_out, m_ffn2_w_in, m_ffn2_w_out, v_w_ada, v_b_ada, v_ffn1_w_in, v_ffn1_w_out, v_w_in, v_gdn_conv_w, v_gdn_a_log, v_gdn_dt_bias, v_gdn_norm_w, v_mla_q_norm_w, v_mla_w_uq, v_mla_kv_norm_w, v_mla_w_ukv, v_qkn_q_nope, v_qkn_q_rope, v_qkn_k_nope, v_qkn_k_rope, v_mla_out_norm_w, v_w_out, v_ffn2_w_in, v_ffn2_w_out):
    weights = dict(w_ada=w_ada, b_ada=b_ada, ffn1_w_in=ffn1_w_in, ffn1_w_out=ffn1_w_out, w_in=w_in,
                   gdn_conv_w=gdn_conv_w, gdn_a_log=gdn_a_log, gdn_dt_bias=gdn_dt_bias, gdn_norm_w=gdn_norm_w,
                   mla_q_norm_w=mla_q_norm_w, mla_w_uq=mla_w_uq, mla_kv_norm_w=mla_kv_norm_w, mla_w_ukv=mla_w_ukv,
                   qkn_q_nope=qkn_q_nope, qkn_q_rope=qkn_q_rope, qkn_k_nope=qkn_k_nope, qkn_k_rope=qkn_k_rope,
                   mla_out_norm_w=mla_out_norm_w, w_out=w_out, ffn2_w_in=ffn2_w_in, ffn2_w_out=ffn2_w_out)
    moms_m = dict(w_ada=m_w_ada, b_ada=m_b_ada, ffn1_w_in=m_ffn1_w_in, ffn1_w_out=m_ffn1_w_out, w_in=m_w_in,
                  gdn_conv_w=m_gdn_conv_w, gdn_a_log=m_gdn_a_log, gdn_dt_bias=m_gdn_dt_bias,
                  gdn_norm_w=m_gdn_norm_w, mla_q_norm_w=m_mla_q_norm_w, mla_w_uq=m_mla_w_uq,
                  mla_kv_norm_w=m_mla_kv_norm_w, mla_w_ukv=m_mla_w_ukv, qkn_q_nope=m_qkn_q_nope,
                  qkn_q_rope=m_qkn_q_rope, qkn_k_nope=m_qkn_k_nope, qkn_k_rope=m_qkn_k_rope,
                  mla_out_norm_w=m_mla_out_norm_w, w_out=m_w_out, ffn2_w_in=m_ffn2_w_in, ffn2_w_out=m_ffn2_w_out)
    moms_v = dict(w_ada=v_w_ada, b_ada=v_b_ada, ffn1_w_in=v_ffn1_w_in, ffn1_w_out=v_ffn1_w_out, w_in=v_w_in,
                  gdn_conv_w=v_gdn_conv_w, gdn_a_log=v_gdn_a_log, gdn_dt_bias=v_gdn_dt_bias,
                  gdn_norm_w=v_gdn_norm_w, mla_q_norm_w=v_mla_q_norm_w, mla_w_uq=v_mla_w_uq,
                  mla_kv_norm_w=v_mla_kv_norm_w, mla_w_ukv=v_mla_w_ukv, qkn_q_nope=v_qkn_q_nope,
                  qkn_q_rope=v_qkn_q_rope, qkn_k_nope=v_qkn_k_nope, qkn_k_rope=v_qkn_k_rope,
                  mla_out_norm_w=v_mla_out_norm_w, w_out=v_w_out, ffn2_w_in=v_ffn2_w_in, ffn2_w_out=v_ffn2_w_out)
    names = list(weights)

    seq, d = x.shape[1], x.shape[2]
    x2d = x.reshape(seq, d)
    tgt = loss_target.reshape(seq, d)
    mx, my, mc = _place()
    chip = 2 * mx + my
    me = 2 * chip + mc
    n_mod = b_ada.shape[1] // d
    shard = w_ada.shape[2]

    half = MLA_ROPE // 2
    inv_freq = 10000.0 ** (-jnp.arange(half, dtype=F32) / half)
    ang = positions.astype(F32).reshape(seq, 1) * inv_freq
    cosv, sinv = jnp.cos(ang), jnp.sin(ang)
    cos2 = _pad_cols(jnp.concatenate([cosv, cosv], 1), LANES)
    sin2 = _pad_cols(jnp.concatenate([-sinv, sinv], 1), LANES)
    rope = (cos2, sin2)

    c_all = _allgather8(jnp.pad(c, ((0, SUBLANES - 1), (0, 0))), "gather_c")[:, 0, :]
    (sc_all,) = _rowcall(lambda r, p: ([_silu(r[0])], []), [c_all], [], [(d, F32)], [], tile=8, name="ada_silu")
    mod_part = _mm(sc_all, w_ada[0], "nn", F32, "ada_mm", hi=True)
    mod_all = _allgather8(mod_part, "gather_mod")
    mod_rows = lax.dynamic_index_in_dim(mod_all, me, axis=1, keepdims=False)
    mod_raw = jnp.concatenate([mod_rows[2 * jj] for jj in range(4)], axis=0).reshape(1, 4 * shard)
    (mod,) = _rowcall(lambda r, p: ([r[0] + r[1]], []),
                      [jnp.pad(mod_raw, ((0, 7), (0, 0))), jnp.pad(b_ada, ((0, 7), (0, 0)))], [],
                      [(4 * shard, F32)], [], tile=8, name="ada_bias")
    mods = [mod[0:1, i * d:(i + 1) * d] for i in range(n_mod)]
    sh1, s1, g1, sh2, s2, g2, sh3, s3, g3 = mods

    def gather_cols(w, name, pad_to=None):
        w2 = w[0].astype(BF16)
        n = w2.shape[1]
        if pad_to:
            w2 = _pad_cols(w2, pad_to)
        return _chips_to_cols(_allgather_chips(w2, name)[:, :, :n])

    def gather_rows(w, name):
        w2 = w[0].astype(BF16)
        return _allgather_chips(w2, name).reshape(4 * w2.shape[0], w2.shape[1])

    f1_in = gather_cols(ffn1_w_in, "gather_f1_in")
    f1_out = gather_rows(ffn1_w_out, "gather_f1_out")
    w_in_full = gather_cols(w_in, "gather_w_in", 768)
    w_uq_full = gather_cols(mla_w_uq, "gather_w_uq", 256)
    w_ukv_full = gather_cols(mla_w_ukv, "gather_w_ukv")
    w_out_full = gather_rows(w_out, "gather_w_out")
    f2_in = gather_cols(ffn2_w_in, "gather_f2_in")
    f2_out = gather_rows(ffn2_w_out, "gather_f2_out")
    conv_all = _allgather8(jnp.pad(gdn_conv_w[0], ((0, SUBLANES - CONV_K), (0, 0))), "gather_conv")
    conv8 = jnp.concatenate([conv_all[2 * jj] for jj in range(4)], axis=1)

    wts = (_pack_w_in(w_in_full), conv8, _pad128(gdn_a_log), _pad128(gdn_dt_bias), gdn_norm_w,
           mla_q_norm_w, _pack_w_uq(w_uq_full), mla_kv_norm_w, w_ukv_full, qkn_q_nope, _pad128(qkn_q_rope),
           qkn_k_nope, _pad128(qkn_k_rope), mla_out_norm_w, w_out_full)

    x1, sv1 = _ffn_fwd(x2d, s1, sh1, g1, f1_in, f1_out, "ffn1")
    xm, svm = _mixer_fwd(x1, s2, sh2, g2, wts, rope)
    x3, sv3 = _ffn_fwd(xm, s3, sh3, g3, f2_in, f2_out, "ffn2")

    def loss_fn(r, p):
        err = r[0] - r[1]
        part = 0.5 * jnp.sum(jnp.sum(err * err, axis=1, keepdims=True) * (1.0 / d), axis=0, keepdims=True)
        return [err * (1.0 / d)], [jnp.broadcast_to(part, (1, LANES))]

    dy, loss_part = _rowcall(loss_fn, [x3, tgt], [], [(d, F32)], [(1, LANES)], tile=512, name="loss")
    loss = lax.psum(loss_part[0, 0], ("x", "y", "c"))

    dxm, dmod3, dw_f2_in, dw_f2_out = _ffn_bwd(dy, sv3, s3, sh3, g3, f2_in, f2_out, "ffn2")
    dx1, dmod2, dw_in_p, dw_uq_p, dw_ukv, dw_out_m, small = _mixer_bwd(dxm, svm, s2, sh2, g2, wts, rope)
    dx0, dmod1, dw_f1_in, dw_f1_out = _ffn_bwd(dx1, sv1, s1, sh1, g1, f1_in, f1_out, "ffn1")
    grad_x = dx0.reshape(x.shape)

    dmod = jnp.concatenate(list(dmod1) + list(dmod2) + list(dmod3), axis=1)
    small_parts = [dmod, small["conv"][:CONV_K], small["a_log"], small["dt"], small["wn"], small["wq"],
                   small["wkv"], small["wqn"], small["wqr"], small["wkn"], small["wkr"], small["won"]]
    packed, offs = _pack_rows(small_parts)
    gathered = _allgather8(packed, "gather_small")
    total = _sum8(gathered, "sum_small")
    (g_b_ada, g_conv, g_a_log, g_dt, g_wn, g_wq, g_wkv, g_wqn, g_wqr, g_wkn, g_wkr, g_won) = _unpack_rows(
        total, offs, [p.shape for p in small_parts])
    dmod_all = _unpack_rows(gathered.reshape(-1, LANES),
                            [(dd * packed.shape[0] + offs[0][0], offs[0][1]) for dd in range(8)],
                            [dmod.shape] * 8)
    dmod_all = jnp.concatenate(dmod_all, axis=0)
    dmod_mine = lax.dynamic_slice_in_dim(dmod_all, chip * shard, shard, axis=1)

    def ada_grad(r, p):
        acc = jnp.zeros((r[0].shape[0], shard), F32)
        for b in range(8):
            acc = acc + r[0][:, b:b + 1] * p[0][b:b + 1, :]
        return [acc], []

    (g_w_ada,) = _rowcall(ada_grad, [_pad_cols(sc_all.T, LANES)], [dmod_mine], [(shard, F32)], [], tile=256,
                          name="ada_grad")

    grads = dict(
        w_ada=g_w_ada[None], b_ada=g_b_ada,
        gdn_conv_w=lax.dynamic_slice_in_dim(g_conv, chip * gdn_conv_w.shape[2], gdn_conv_w.shape[2], axis=1)[None],
        gdn_a_log=g_a_log[:, :GDN_HEADS], gdn_dt_bias=g_dt[:, :GDN_HEADS], gdn_norm_w=g_wn, mla_q_norm_w=g_wq,
        mla_kv_norm_w=g_wkv, qkn_q_nope=g_wqn, qkn_q_rope=g_wqr[:, :MLA_ROPE], qkn_k_nope=g_wkn,
        qkn_k_rope=g_wkr[:, :MLA_ROPE], mla_out_norm_w=g_won)

    def rs_cols(dw, name, pad_to=None):
        g4 = _cols_to_chips(dw).astype(BF16)
        n = g4.shape[2]
        if pad_to:
            g4 = _pad_cols(g4, pad_to)
        return _reduce_scatter_chips(g4, name)[:, :n][None]

    def rs_rows(dw, name):
        r, cc = dw.shape
        return _reduce_scatter_chips(dw.astype(BF16).reshape(4, r // 4, cc), name)[None]

    grads["ffn2_w_in"] = rs_cols(dw_f2_in, "rs_f2_in")
    grads["ffn2_w_out"] = rs_rows(dw_f2_out, "rs_f2_out")
    grads["w_in"] = rs_cols(_unpack_w_in(dw_in_p), "rs_w_in", 768)
    grads["mla_w_uq"] = rs_cols(_unpack_w_uq(dw_uq_p), "rs_w_uq", 256)
    grads["mla_w_ukv"] = rs_cols(dw_ukv, "rs_w_ukv")
    grads["w_out"] = rs_rows(dw_out_m, "rs_w_out")
    grads["ffn1_w_in"] = rs_cols(dw_f1_in, "rs_f1_in")
    grads["ffn1_w_out"] = rs_rows(dw_f1_out, "rs_f1_out")

    big = ["w_ada", "ffn1_w_in", "ffn1_w_out", "w_in", "mla_w_uq", "mla_w_ukv", "w_out", "ffn2_w_in", "ffn2_w_out"]
    delta, new_m, new_v = {}, {}, {}
    for nme in big:
        shp = weights[nme].shape
        dl, nm, nv = _adamw(weights[nme][0], grads[nme][0], moms_m[nme][0], moms_v[nme][0], "adamw_" + nme)
        delta[nme], new_m[nme], new_v[nme] = dl.reshape(shp), nm.reshape(shp), nv.reshape(shp)
    tiny = [nme for nme in names if nme not in big]
    shapes = [weights[nme].shape for nme in tiny]
    pw, poffs = _pack_rows([weights[nme] for nme in tiny])
    pg, _ = _pack_rows([grads[nme] for nme in tiny])
    pm, _ = _pack_rows([moms_m[nme] for nme in tiny])
    pv, _ = _pack_rows([moms_v[nme] for nme in tiny])
    pd, pnm, pnv = _adamw(pw, pg, pm, pv, "adamw_small")
    for nme, dl, nm, nv in zip(tiny, _unpack_rows(pd, poffs, shapes), _unpack_rows(pnm, poffs, shapes),
                               _unpack_rows(pnv, poffs, shapes)):
        delta[nme], new_m[nme], new_v[nme] = dl, nm, nv

    return (loss, grad_x, *[grads[nme].reshape(weights[nme].shape) for nme in names],
            *[delta[nme] for nme in names], *[new_m[nme] for nme in names], *[new_v[nme] for nme in names])
```

```python
import functools

import jax
import jax.numpy as jnp
from jax import lax
from jax.experimental import pallas as pl
from jax.experimental.pallas import tpu as pltpu

F32 = jnp.float32
BF16 = jnp.bfloat16
HI = lax.Precision.HIGHEST
MESH = pl.DeviceIdType.MESH

EPS = 1e-6
CHUNK = 64
D_FF = 2816
GDN_HEADS = 4
HEAD = 128
MLA_HEADS = 4
MLA_ROPE = 64
MLA_Q_LORA = 384
MLA_KV_LORA = 256
QK_PAD = 256
ATT_SCALE = (HEAD + MLA_ROPE) ** -0.5
N_PROJ = 3072

ADAM_LR, ADAM_B1, ADAM_B2, ADAM_EPS, ADAM_WD, ADAM_STEP = 0.001, 0.9, 0.999, 1e-08, 0.01, 10

LANES = 128
SUBLANES = 8
VMEM_LIMIT = 56 * 2 ** 20


def _params(sem=None):
    return pltpu.CompilerParams(dimension_semantics=sem, vmem_limit_bytes=VMEM_LIMIT)


def _pick(n, cap, align):
    best = None
    d = align
    while d <= min(n, cap):
        if n % d == 0:
            best = d
        d += align
    return best if best is not None else n


def _iota(shape, dim):
    return lax.broadcasted_iota(jnp.int32, shape, dim)


def _rowcall(fn, rows, params, out_rows, out_accs, *, tile, name):
    rows = [r if isinstance(r, tuple) else (r, r.shape[1], 0) for r in rows]
    s = rows[0][0].shape[-2]
    t = min(tile, s)
    n = s // t
    n_in = len(rows) + len(params)
    n_row_out = len(out_rows)

    in_specs = []
    for r in rows:
        if len(r) == 3:
            in_specs.append(pl.BlockSpec((t, r[1]), functools.partial(lambda i, b: (i, b), b=r[2])))
        else:
            in_specs.append(pl.BlockSpec((None, t, r[1]), functools.partial(lambda i, b, h: (h, i, b), b=r[2], h=r[3])))
    in_specs += [pl.BlockSpec(p.shape, lambda i: (0, 0)) for p in params]
    out_shape, out_specs = [], []
    for o in out_rows:
        if len(o) == 2:
            out_shape.append(jax.ShapeDtypeStruct((s, o[0]), o[1]))
            out_specs.append(pl.BlockSpec((t, o[0]), lambda i: (i, 0)))
        else:
            out_shape.append(jax.ShapeDtypeStruct((o[2], s, o[0]), o[1]))
            out_specs.append(pl.BlockSpec((o[2], t, o[0]), lambda i: (0, i, 0)))
    out_shape += [jax.ShapeDtypeStruct(shape, F32) for shape in out_accs]
    out_specs += [pl.BlockSpec(shape, lambda i: (0, 0)) for shape in out_accs]

    def body(*refs):
        ins = refs[:n_in]
        outs = refs[n_in:]
        i = pl.program_id(0)
        vals = [r[...] for r in ins]
        row_outs, acc_outs = fn(vals[:len(rows)], vals[len(rows):])
        for r, v in zip(outs[:n_row_out], row_outs):
            if isinstance(v, (list, tuple)):
                for hh, piece in enumerate(v):
                    r[hh] = piece.astype(r.dtype)
            else:
                r[...] = v.astype(r.dtype)
        if out_accs:
            @pl.when(i == 0)
            def _():
                for r in outs[n_row_out:]:
                    r[...] = jnp.zeros(r.shape, F32)
            for r, v in zip(outs[n_row_out:], acc_outs):
                r[...] += v

    res = pl.pallas_call(
        body, name=name, grid=(n,), in_specs=in_specs, out_specs=out_specs, out_shape=out_shape,
        compiler_params=_params(("arbitrary",) if out_accs else ("parallel",)),
    )(*[r[0] for r in rows], *params)
    return list(res)


MM_TILE_MN = 1536


def _mm(a, b, mode, out_dtype, name, hi=False):
    if mode == "nn":
        (m, k), (_, n) = a.shape, b.shape
        dims = (((1,), (0,)), ((), ()))
    elif mode == "nt":
        (m, k), (n, _) = a.shape, b.shape
        dims = (((1,), (1,)), ((), ()))
    else:
        (k, m), (_, n) = a.shape, b.shape
        dims = (((0,), (0,)), ((), ()))
    tm = _pick(m, MM_TILE_MN if mode == "tn" else 1024, LANES if mode == "tn" else 16)
    tn = _pick(n, MM_TILE_MN, LANES)
    tk = _pick(k, 1024 if mode == "tn" else MM_TILE_MN, LANES)
    nk = k // tk
    if mode == "nn":
        a_spec = pl.BlockSpec((tm, tk), lambda i, j, kk: (i, kk))
        b_spec = pl.BlockSpec((tk, tn), lambda i, j, kk: (kk, j))
    elif mode == "nt":
        a_spec = pl.BlockSpec((tm, tk), lambda i, j, kk: (i, kk))
        b_spec = pl.BlockSpec((tn, tk), lambda i, j, kk: (j, kk))
    else:
        a_spec = pl.BlockSpec((tk, tm), lambda i, j, kk: (kk, i))
        b_spec = pl.BlockSpec((tk, tn), lambda i, j, kk: (kk, j))

    def body(a_ref, b_ref, o_ref, acc_ref):
        kk = pl.program_id(2)

        @pl.when(kk == 0)
        def _():
            acc_ref[...] = jnp.zeros(acc_ref.shape, F32)

        av, bv = a_ref[...], b_ref[...]
        if hi:
            acc_ref[...] += lax.dot_general(av, bv, dims, precision=HI, preferred_element_type=F32)
        else:
            acc_ref[...] += lax.dot_general(av.astype(BF16), bv.astype(BF16), dims,
                                            preferred_element_type=F32)

        @pl.when(kk == nk - 1)
        def _():
            o_ref[...] = acc_ref[...].astype(o_ref.dtype)

    return pl.pallas_call(
        body, name=name, grid=(m // tm, n // tn, nk),
        in_specs=[a_spec, b_spec],
        out_specs=pl.BlockSpec((tm, tn), lambda i, j, kk: (i, j)),
        out_shape=jax.ShapeDtypeStruct((m, n), out_dtype),
        scratch_shapes=[pltpu.VMEM((tm, tn), F32)],
        compiler_params=_params(("parallel", "parallel", "arbitrary")),
    )(a, b)


def _rms(x, w=None, n=None):
    n = x.shape[-1] if n is None else n
    y = x * lax.rsqrt(jnp.sum(x * x, axis=-1, keepdims=True) * (1.0 / n) + EPS)
    return y if w is None else y * w


def _silu(x):
    return x * jax.nn.sigmoid(x)


def _softplus(x):
    return jnp.maximum(x, 0.0) + jnp.log1p(jnp.exp(-jnp.abs(x)))


def _split(x, widths):
    out, o = [], 0
    for w in widths:
        out.append(x[:, o:o + w])
        o += w
    return out


def _modulate(x, s, sh):
    return _rms(x) * (1.0 + s) + sh


def _rope_rot(x):
    r, c = _iota((LANES, LANES), 0), _iota((LANES, LANES), 1)
    half = MLA_ROPE // 2
    perm = (((r < half) & (c == r + half)) | ((r >= half) & (r < MLA_ROPE) & (c == r - half))).astype(F32)
    return jnp.dot(x, perm, precision=HI, preferred_element_type=F32)


def _rope(x, cos2, sin2):
    return x * cos2 + _rope_rot(x) * sin2


def _gdn_prep_core(qkv_parts, gab, a_log, dt_bias):
    act = [_silu(p) for p in qkv_parts]
    qs = [p * lax.rsqrt(jnp.sum(p * p, -1, keepdims=True) + EPS) * (HEAD ** -0.5) for p in act[:4]]
    ks = [p * lax.rsqrt(jnp.sum(p * p, -1, keepdims=True) + EPS) for p in act[4:8]]
    lane = _iota(gab.shape, 1)
    g = -jnp.exp(a_log) * _softplus(gab + dt_bias)
    beta = jax.nn.sigmoid(gab)
    gb = jnp.where(lane < GDN_HEADS, g, jnp.where(lane < 2 * GDN_HEADS, beta, 0.0))
    return (jnp.concatenate(qs, 1), jnp.concatenate(ks, 1), jnp.concatenate(act[8:], 1), gb)


def _mla_prep_core(cq, ckv, kr, cos2, sin2, wq, wkv, wkr):
    cqn = _rms(cq, wq)
    ckvn = _rms(ckv, wkv)
    k_rope = _rope(_rms(kr, wkr, MLA_ROPE), cos2, sin2)
    return cqn, ckvn, k_rope


def _qk_prep_core(qn_parts, qr_parts, kn_parts, v_parts, k_rope, cos2, sin2, wqn, wqr, wkn):
    qs, ks = [], []
    for h in range(MLA_HEADS):
        qn = _rms(qn_parts[h], wqn) * ATT_SCALE
        qr = _rope(_rms(qr_parts[h], wqr, MLA_ROPE), cos2, sin2) * ATT_SCALE
        qs.append(jnp.concatenate([qn, qr], 1))
        ks.append(jnp.concatenate([_rms(kn_parts[h], wkn), k_rope], 1))
    return qs, ks, list(v_parts)


def _mix_post_core(o_parts, gz_parts, ob_parts, wn, won):
    oa = [_rms(o, wn) * _silu(z) for o, z in zip(o_parts, gz_parts)]
    ob = [_rms(o, won) for o in ob_parts]
    return jnp.concatenate(oa + ob, 1)


CONV_K = 4
HALO = SUBLANES


def _conv_fwd(proj, w8, name):
    s = proj.shape[0]
    c = w8.shape[1]
    t = min(256, s)
    n = s // t
    hb = t // HALO

    def body(x_ref, prev_ref, w_ref, o_ref, buf):
        i = pl.program_id(0)
        buf[pl.ds(0, HALO), :] = jnp.where(i > 0, prev_ref[...], 0.0)
        buf[pl.ds(HALO, t), :] = x_ref[...]
        acc = jnp.zeros((t, c), F32)
        for k in range(CONV_K):
            acc = acc + w_ref[k:k + 1, :] * buf[pl.ds(HALO - (CONV_K - 1) + k, t), :]
        o_ref[...] = acc

    return pl.pallas_call(
        body, name=name, grid=(n,),
        in_specs=[pl.BlockSpec((t, c), lambda i: (i, 0)),
                  pl.BlockSpec((HALO, c), lambda i: (jnp.maximum(i * hb - 1, 0), 0)),
                  pl.BlockSpec(w8.shape, lambda i: (0, 0))],
        out_specs=pl.BlockSpec((t, c), lambda i: (i, 0)),
        out_shape=jax.ShapeDtypeStruct((s, c), F32),
        scratch_shapes=[pltpu.VMEM((t + HALO, c), F32)],
        compiler_params=_params(("parallel",)),
    )(proj, proj, w8)


def _conv_bwd(proj, dy, w8, name):
    s = proj.shape[0]
    c = w8.shape[1]
    t = min(256, s)
    n = s // t
    hb = t // HALO

    def body(x_ref, prev_ref, dy_ref, next_ref, w_ref, dx_ref, dw_ref, bufx, bufd):
        i = pl.program_id(0)
        bufx[pl.ds(0, HALO), :] = jnp.where(i > 0, prev_ref[...], 0.0)
        bufx[pl.ds(HALO, t), :] = x_ref[...]
        bufd[pl.ds(0, t), :] = dy_ref[...]
        bufd[pl.ds(t, HALO), :] = jnp.where(i < n - 1, next_ref[...], 0.0)

        @pl.when(i == 0)
        def _():
            dw_ref[...] = jnp.zeros(dw_ref.shape, F32)

        dyv = dy_ref[...]
        acc = jnp.zeros((t, c), F32)
        for k in range(CONV_K):
            acc = acc + w_ref[k:k + 1, :] * bufd[pl.ds(CONV_K - 1 - k, t), :]
            dw_ref[k:k + 1, :] += jnp.sum(dyv * bufx[pl.ds(HALO - (CONV_K - 1) + k, t), :], axis=0, keepdims=True)
        dx_ref[...] = acc

    return pl.pallas_call(
        body, name=name, grid=(n,),
        in_specs=[pl.BlockSpec((t, c), lambda i: (i, 0)),
                  pl.BlockSpec((HALO, c), lambda i: (jnp.maximum(i * hb - 1, 0), 0)),
                  pl.BlockSpec((t, c), lambda i: (i, 0)),
                  pl.BlockSpec((HALO, c), lambda i: (jnp.minimum((i + 1) * hb, s // HALO - 1), 0)),
                  pl.BlockSpec(w8.shape, lambda i: (0, 0))],
        out_specs=[pl.BlockSpec((t, c), lambda i: (i, 0)), pl.BlockSpec(w8.shape, lambda i: (0, 0))],
        out_shape=[jax.ShapeDtypeStruct((s, c), F32), jax.ShapeDtypeStruct(w8.shape, F32)],
        scratch_shapes=[pltpu.VMEM((t + HALO, c), F32), pltpu.VMEM((t + HALO, c), F32)],
        compiler_params=_params(("arbitrary",)),
    )(proj, proj, dy, dy, w8)


def _dot(a, b):
    return jnp.dot(a, b, precision=HI, preferred_element_type=F32)


def _dot_nt(a, b):
    return lax.dot_general(a, b, (((1,), (1,)), ((), ())), precision=HI, preferred_element_type=F32)


def _dot_tn(a, b):
    return lax.dot_general(a, b, (((0,), (0,)), ((), ())), precision=HI, preferred_element_type=F32)


_B_NN = (((2,), (1,)), ((0,), (0,)))
_B_NT = (((2,), (2,)), ((0,), (0,)))
_B_TN = (((1,), (1,)), ((0,), (0,)))


def _bdot_hi(a, b):
    return lax.dot_general(a, b, _B_NN, precision=HI, preferred_element_type=F32)


def _dot3(a, b, dims):
    return lax.dot_general(a, b, dims, precision=HI, preferred_element_type=F32)


class _Dots:
    def __init__(self, diff):
        nn = lambda a, b: _dot3(a, b, _B_NN)
        nt = lambda a, b: _dot3(a, b, _B_NT)
        tn = lambda a, b: _dot3(a, b, _B_TN)
        if diff:
            def with_rule(f, bwd):
                g = jax.custom_vjp(f)
                g.defvjp(lambda a, b: (f(a, b), (a, b)), bwd)
                return g
            self.nn = with_rule(nn, lambda r, ct: (nt(ct, r[1]), tn(r[0], ct)))
            self.nt = with_rule(nt, lambda r, ct: (nn(ct, r[1]), tn(ct, r[0])))
            self.tn = with_rule(tn, lambda r, ct: (nt(r[1], ct), nn(r[0], ct)))
        else:
            self.nn, self.nt, self.tn = nn, nt, tn


def _unit_lower_inverse(a, dots):
    c = a.shape[-1]
    ri, ci = _iota(a.shape, 1), _iota(a.shape, 2)
    inner = (ri // 2) == (ci // 2)
    t = (ri == ci).astype(F32) - jnp.where(inner, a, 0.0)
    blk = 4
    while blk <= c:
        outer = (ri // blk) == (ci // blk)
        low = jnp.where(outer & jnp.logical_not(inner), a, 0.0)
        t = t - dots.nn(dots.nn(t, low), t)
        inner = outer
        blk *= 2
    return t


def _stack(xs):
    return jnp.concatenate([x[None] for x in xs], axis=0)


def _gdn_local(dots, q, k, v, gbs):
    b, c, _ = q.shape
    gcols, bcols = [], []
    for gb in gbs:
        lane = _iota(gb.shape, 1)
        for h in range(GDN_HEADS):
            gcols.append(jnp.sum(jnp.where(lane == h, gb, 0.0), axis=1, keepdims=True))
            bcols.append(jnp.sum(jnp.where(lane == GDN_HEADS + h, gb, 0.0), axis=1, keepdims=True))
    gcol, bcol = _stack(gcols), _stack(bcols)
    ri, ci = _iota((b, c, c), 1), _iota((b, c, c), 2)
    incl = ri >= ci
    tril = incl.astype(F32)
    g_cc = _bdot_hi(tril, jnp.broadcast_to(gcol, (b, c, c)))
    g_row = _bdot_hi(jnp.ones((b, c, c), F32), jnp.where(ri == ci, g_cc, 0.0))
    g_cl = _bdot_hi(tril, jnp.broadcast_to(gcol, (b, c, HEAD)))
    g_last = jnp.sum(jnp.broadcast_to(gcol, (b, c, HEAD)), axis=1, keepdims=True)
    decay = jnp.where(incl, jnp.exp(jnp.where(incl, g_cc - g_row, 0.0)), 0.0)
    kk = dots.nt(k, k)
    minv = _unit_lower_inverse(jnp.where(ri > ci, bcol * kk * decay, 0.0), dots)
    e_g = jnp.exp(g_cl)
    u = dots.nn(minv, v * bcol)
    wk = dots.nn(minv, k * (bcol * e_g))
    qk = dots.nt(q, k) * decay
    return u, wk, q * e_g, k * jnp.exp(g_last - g_cl), qk, jnp.exp(g_last)


def _gdn_scan(dots, states, u, wk, qd, kd, qk, gl_tile):
    lane, row = _iota(gl_tile.shape, 1), _iota(gl_tile.shape, 0)
    gl = _stack([
        jnp.sum(jnp.sum(jnp.where((lane == h) & (row == 0), gl_tile, 0.0), axis=1, keepdims=True),
                axis=0, keepdims=True) for h in range(GDN_HEADS)])
    v_new = u - dots.nn(wk, states)
    o = dots.nn(qd, states) + dots.nn(qk, v_new)
    return states * gl + dots.tn(kd, v_new), o


def _heads(x):
    return jnp.stack(_split(x, HW4))


GDN_W = GDN_HEADS * HEAD
HW4 = [HEAD] * GDN_HEADS
LOCAL_CHUNKS = 2
_CHUNK_ROWS = [pl.ds(cc * CHUNK, CHUNK) for cc in range(LOCAL_CHUNKS)]


def _chunk_heads(ref):
    return jnp.concatenate([_heads(ref[rows, :]) for rows in _CHUNK_ROWS], 0)


def _gdn_local_fwd(q, k, v, gb, name):
    s = q.shape[0]
    t = LOCAL_CHUNKS * CHUNK

    def body(q_ref, k_ref, v_ref, gb_ref, u_ref, wk_ref, qd_ref, kd_ref, qk_ref, gl_ref):
        u, wk, qd, kd, qk, gl = _gdn_local(_Dots(False), _chunk_heads(q_ref), _chunk_heads(k_ref),
                                           _chunk_heads(v_ref), [gb_ref[rows, :] for rows in _CHUNK_ROWS])
        lane = _iota((CHUNK, LANES), 1)
        for cc, rows in enumerate(_CHUNK_ROWS):
            gl_tile = jnp.zeros((CHUNK, LANES), F32)
            for h in range(GDN_HEADS):
                b, cols = cc * GDN_HEADS + h, pl.ds(h * HEAD, HEAD)
                u_ref[rows, cols] = u[b]
                wk_ref[rows, cols] = wk[b]
                qd_ref[rows, cols] = qd[b]
                kd_ref[rows, cols] = kd[b]
                qk_ref[h, rows, :] = qk[b]
                gl_tile = gl_tile + jnp.where(lane == h, gl[b], 0.0)
            gl_ref[rows, :] = gl_tile

    row = pl.BlockSpec((t, GDN_W), lambda i: (i, 0))
    lane = pl.BlockSpec((t, LANES), lambda i: (i, 0))
    qks = pl.BlockSpec((GDN_HEADS, t, CHUNK), lambda i: (0, i, 0))
    return pl.pallas_call(
        body, name=name, grid=(s // t,),
        in_specs=[row, row, row, lane],
        out_specs=[row, row, row, row, qks, lane],
        out_shape=[jax.ShapeDtypeStruct((s, GDN_W), F32)] * 4
        + [jax.ShapeDtypeStruct((GDN_HEADS, s, CHUNK), F32), jax.ShapeDtypeStruct((s, LANES), F32)],
        compiler_params=_params(("parallel",)),
    )(q, k, v, gb)


def _gdn_local_bwd(q, k, v, gb, du, dwk, dqd, dkd, dqk, dgl, name):
    s = q.shape[0]
    t = LOCAL_CHUNKS * CHUNK

    def body(q_ref, k_ref, v_ref, gb_ref, du_ref, dwk_ref, dqd_ref, dkd_ref, dqk_ref, dgl_ref,
             dq_ref, dk_ref, dv_ref, dgb_ref):
        _, vjp = jax.vjp(functools.partial(_gdn_local, _Dots(False)), _chunk_heads(q_ref), _chunk_heads(k_ref),
                         _chunk_heads(v_ref), [gb_ref[rows, :] for rows in _CHUNK_ROWS])
        lane = _iota((CHUNK, LANES), 1)
        dqk = jnp.stack([dqk_ref[h, rows, :] for rows in _CHUNK_ROWS for h in range(GDN_HEADS)])
        dgl = jnp.stack([jnp.sum(jnp.where(lane == h, dgl_ref[rows, :], 0.0), axis=0, keepdims=True)
                         for rows in _CHUNK_ROWS for h in range(GDN_HEADS)])
        d_q, d_k, d_v, d_gbs = vjp((_chunk_heads(du_ref), _chunk_heads(dwk_ref), _chunk_heads(dqd_ref),
                                    _chunk_heads(dkd_ref), dqk, dgl))
        for cc, rows in enumerate(_CHUNK_ROWS):
            for h in range(GDN_HEADS):
                b, cols = cc * GDN_HEADS + h, pl.ds(h * HEAD, HEAD)
                dq_ref[rows, cols] = d_q[b]
                dk_ref[rows, cols] = d_k[b]
                dv_ref[rows, cols] = d_v[b]
            dgb_ref[rows, :] = d_gbs[cc]

    row = pl.BlockSpec((t, GDN_W), lambda i: (i, 0))
    lane = pl.BlockSpec((t, LANES), lambda i: (i, 0))
    qks = pl.BlockSpec((GDN_HEADS, t, CHUNK), lambda i: (0, i, 0))
    return pl.pallas_call(
        body, name=name, grid=(s // t,),
        in_specs=[row, row, row, lane, row, row, row, row, qks, lane],
        out_specs=[row, row, row, lane],
        out_shape=[jax.ShapeDtypeStruct((s, GDN_W), F32)] * 3 + [jax.ShapeDtypeStruct((s, LANES), F32)],
        compiler_params=_params(("parallel",)),
    )(q, k, v, gb, du, dwk, dqd, dkd, dqk, dgl)


def _gdn_scan_fwd(u, wk, qd, kd, qk, gl, name):
    s = u.shape[0]
    nc = s // CHUNK

    def body(u_ref, wk_ref, qd_ref, kd_ref, qk_ref, gl_ref, o_ref, st_ref, state):
        i = pl.program_id(0)

        @pl.when(i == 0)
        def _():
            state[...] = jnp.zeros(state.shape, F32)

        st_ref[...] = state[...]
        new_states, o = _gdn_scan(_Dots(False), state[...], _heads(u_ref[...]), _heads(wk_ref[...]),
                                  _heads(qd_ref[...]), _heads(kd_ref[...]), qk_ref[...], gl_ref[...])
        state[...] = new_states
        o_ref[...] = jnp.concatenate([o[h] for h in range(GDN_HEADS)], 1)

    row = pl.BlockSpec((CHUNK, GDN_W), lambda i: (i, 0))
    return pl.pallas_call(
        body, name=name, grid=(nc,),
        in_specs=[row, row, row, row, pl.BlockSpec((GDN_HEADS, CHUNK, CHUNK), lambda i: (0, i, 0)),
                  pl.BlockSpec((CHUNK, LANES), lambda i: (i, 0))],
        out_specs=[row, pl.BlockSpec((None, GDN_HEADS, HEAD, HEAD), lambda i: (i, 0, 0, 0))],
        out_shape=[jax.ShapeDtypeStruct((s, GDN_W), F32),
                   jax.ShapeDtypeStruct((nc, GDN_HEADS, HEAD, HEAD), F32)],
        scratch_shapes=[pltpu.VMEM((GDN_HEADS, HEAD, HEAD), F32)],
        compiler_params=_params(("arbitrary",)),
    )(u, wk, qd, kd, qk, gl)


def _gdn_scan_bwd(u, wk, qd, kd, qk, gl, st, do, name):
    s = u.shape[0]
    nc = s // CHUNK

    def body(u_ref, wk_ref, qd_ref, kd_ref, qk_ref, gl_ref, st_ref, do_ref,
             du_ref, dwk_ref, dqd_ref, dkd_ref, dqk_ref, dgl_ref, dstate):
        i = pl.program_id(0)

        @pl.when(i == 0)
        def _():
            dstate[...] = jnp.zeros(dstate.shape, F32)

        _, vjp = jax.vjp(functools.partial(_gdn_scan, _Dots(False)), st_ref[...], _heads(u_ref[...]),
                         _heads(wk_ref[...]), _heads(qd_ref[...]), _heads(kd_ref[...]), qk_ref[...], gl_ref[...])
        d_states, d_u, d_wk, d_qd, d_kd, d_qk, d_gl = vjp((dstate[...], _heads(do_ref[...])))
        dstate[...] = d_states
        dqk_ref[...] = d_qk
        unheads = lambda x: jnp.concatenate([x[h] for h in range(GDN_HEADS)], 1)
        du_ref[...] = unheads(d_u)
        dwk_ref[...] = unheads(d_wk)
        dqd_ref[...] = unheads(d_qd)
        dkd_ref[...] = unheads(d_kd)
        dgl_ref[...] = d_gl

    rev = lambda i: (nc - 1 - i, 0)
    row = pl.BlockSpec((CHUNK, GDN_W), rev)
    lane = pl.BlockSpec((CHUNK, LANES), rev)
    qks = pl.BlockSpec((GDN_HEADS, CHUNK, CHUNK), lambda i: (0, nc - 1 - i, 0))
    return pl.pallas_call(
        body, name=name, grid=(nc,),
        in_specs=[row, row, row, row, qks, lane,
                  pl.BlockSpec((None, GDN_HEADS, HEAD, HEAD), lambda i: (nc - 1 - i, 0, 0, 0)), row],
        out_specs=[row, row, row, row, qks, lane],
        out_shape=[jax.ShapeDtypeStruct((s, GDN_W), F32)] * 4
        + [jax.ShapeDtypeStruct((GDN_HEADS, s, CHUNK), F32), jax.ShapeDtypeStruct((s, LANES), F32)],
        scratch_shapes=[pltpu.VMEM((GDN_HEADS, HEAD, HEAD), F32)],
        compiler_params=_params(("arbitrary",)),
    )(u, wk, qd, kd, qk, gl, st, do)


def _chunk_mask(i, j, t):
    r = i * t + _iota((t, t), 0)
    c = j * t + _iota((t, t), 1)
    return (r // CHUNK) >= (c // CHUNK)


ATT_TILE = 512
_QK_T = (((2,), (2,)), ((0,), (0,)))
_PV = (((2,), (1,)), ((0,), (0,)))


def _attn_fwd(q, k, v, name):
    nh, s = MLA_HEADS, q.shape[0]
    t = min(ATT_TILE, s)
    n = s // t
    nt = (((1,), (1,)), ((), ()))

    def body(q_ref, k_ref, v_ref, o_ref, lse_ref, m_sc, l_sc, acc_sc):
        i, j = pl.program_id(1), pl.program_id(2)

        @pl.when(j == 0)
        def _():
            m_sc[...] = jnp.full(m_sc.shape, -jnp.inf, F32)
            l_sc[...] = jnp.zeros(l_sc.shape, F32)
            acc_sc[...] = jnp.zeros(acc_sc.shape, F32)

        def step(masked):
            sc = lax.dot_general(q_ref[...], k_ref[...], nt, preferred_element_type=F32)
            if masked:
                sc = jnp.where(_chunk_mask(i, j, t), sc, -jnp.inf)
            m_prev = m_sc[:, :1]
            m_new = jnp.maximum(m_prev, jnp.max(sc, axis=1, keepdims=True))
            alpha = jnp.exp(m_prev - m_new)
            p = jnp.exp(sc - m_new)
            l_sc[...] = jnp.broadcast_to(alpha * l_sc[:, :1] + jnp.sum(p, axis=1, keepdims=True), l_sc.shape)
            acc_sc[...] = alpha * acc_sc[...] + jnp.dot(p.astype(BF16), v_ref[...], preferred_element_type=F32)
            m_sc[...] = jnp.broadcast_to(m_new, m_sc.shape)

        pl.when(j < i)(lambda: step(False))
        pl.when(j == i)(lambda: step(True))

        @pl.when(j == n - 1)
        def _():
            o_ref[...] = acc_sc[...] / l_sc[:, :1]
            lse_ref[...] = m_sc[...] + jnp.log(l_sc[...])

    qrow = lambda h, i, j: (i, h)
    krow = lambda h, i, j: (jnp.minimum(j, i), h)
    return pl.pallas_call(
        body, name=name, grid=(nh, n, n),
        in_specs=[pl.BlockSpec((t, QK_PAD), qrow), pl.BlockSpec((t, QK_PAD), krow), pl.BlockSpec((t, HEAD), krow)],
        out_specs=[pl.BlockSpec((t, HEAD), qrow), pl.BlockSpec((None, t, LANES), lambda h, i, j: (h, i, 0))],
        out_shape=[jax.ShapeDtypeStruct((s, nh * HEAD), F32), jax.ShapeDtypeStruct((nh, s, LANES), F32)],
        scratch_shapes=[pltpu.VMEM((t, LANES), F32), pltpu.VMEM((t, LANES), F32), pltpu.VMEM((t, HEAD), F32)],
        compiler_params=_params(("parallel", "parallel", "arbitrary")),
    )(q, k, v)


def _attn_bwd(q, k, v, o, do, lse, name):
    nh, s = MLA_HEADS, q.shape[0]
    t = min(ATT_TILE, s)
    n = s // t
    tn = (((0,), (0,)), ((), ()))
    nt = (((1,), (1,)), ((), ()))

    def body(q_ref, k_ref, v_ref, o_ref, do_ref, lse_ref, dq_ref, dk_ref, dv_ref, dk_acc, dv_acc, dq_acc):
        j, i = pl.program_id(1), pl.program_id(2)

        @pl.when(i + j == 0)
        def _():
            dq_acc[...] = jnp.zeros(dq_acc.shape, F32)

        @pl.when(i == 0)
        def _():
            dk_acc[...] = jnp.zeros(dk_acc.shape, F32)
            dv_acc[...] = jnp.zeros(dv_acc.shape, F32)

        def step(masked):
            qv, kv, do = q_ref[...], k_ref[...], do_ref[...]
            sc = lax.dot_general(qv, kv, nt, preferred_element_type=F32)
            p = jnp.exp(sc - lse_ref[:, :1])
            if masked:
                p = jnp.where(_chunk_mask(i, j, t), p, 0.0)
            dob = do.astype(BF16)
            dp = lax.dot_general(dob, v_ref[...], nt, preferred_element_type=F32)
            ds = (p * (dp - jnp.sum(do * o_ref[...], axis=1, keepdims=True))).astype(BF16)
            dv_acc[...] += lax.dot_general(p.astype(BF16), dob, tn, preferred_element_type=F32)
            dk_acc[...] += lax.dot_general(ds, qv, tn, preferred_element_type=F32)
            rows = pl.ds(pl.multiple_of(i * t, t), t)
            dq_acc[rows, :] += jnp.dot(ds, kv, preferred_element_type=F32)

        pl.when(i >= j)(lambda: step(True))

        @pl.when(i == n - 1)
        def _():
            dk_ref[...] = dk_acc[...]
            dv_ref[...] = dv_acc[...]

        @pl.when(i + j == 2 * (n - 1))
        def _():
            dq_ref[...] = dq_acc[...]

    qrow = lambda h, j, i: (jnp.maximum(i, j), h)
    krow = lambda h, j, i: (j, h)
    return pl.pallas_call(
        body, name=name, grid=(nh, n, n),
        in_specs=[pl.BlockSpec((t, QK_PAD), qrow), pl.BlockSpec((t, QK_PAD), krow), pl.BlockSpec((t, HEAD), krow),
                  pl.BlockSpec((t, HEAD), qrow), pl.BlockSpec((t, HEAD), qrow),
                  pl.BlockSpec((None, t, LANES), lambda h, j, i: (h, jnp.maximum(i, j), 0))],
        out_specs=[pl.BlockSpec((s, QK_PAD), lambda h, j, i: (0, h)),
                   pl.BlockSpec((t, QK_PAD), krow), pl.BlockSpec((t, HEAD), krow)],
        out_shape=[jax.ShapeDtypeStruct((s, nh * QK_PAD), F32), jax.ShapeDtypeStruct((s, nh * QK_PAD), F32),
                   jax.ShapeDtypeStruct((s, nh * HEAD), F32)],
        scratch_shapes=[pltpu.VMEM((t, QK_PAD), F32), pltpu.VMEM((t, HEAD), F32), pltpu.VMEM((s, QK_PAD), F32)],
        compiler_params=_params(("arbitrary", "arbitrary", "arbitrary")),
    )(q, k, v, o, do, lse)


def _place():
    return lax.axis_index("x"), lax.axis_index("y"), lax.axis_index("c")


def _allgather8(x, name):
    r, c = x.shape

    def body(x_ref, out_ref, send_sems, recv_sems, local_sem):
        mx, my, mc = _place()
        me = 4 * mx + 2 * my + mc
        mine = pltpu.make_async_copy(x_ref, out_ref.at[me], local_sem)
        mine.start()
        copies = []
        for d in range(1, 8):
            px = 1 - mx if d & 4 else mx
            py = 1 - my if d & 2 else my
            pc = 1 - mc if d & 1 else mc
            cp = pltpu.make_async_remote_copy(
                src_ref=x_ref, dst_ref=out_ref.at[me], send_sem=send_sems.at[d - 1], recv_sem=recv_sems.at[d - 1],
                device_id=(px, py, pc), device_id_type=MESH)
            cp.start()
            copies.append(cp)
        for cp in copies:
            cp.wait()
        mine.wait()

    return pl.pallas_call(
        body, name=name,
        out_shape=jax.ShapeDtypeStruct((8, r, c), x.dtype),
        in_specs=[pl.BlockSpec(memory_space=pltpu.VMEM)],
        out_specs=pl.BlockSpec(memory_space=pltpu.VMEM),
        scratch_shapes=[pltpu.SemaphoreType.DMA((7,)), pltpu.SemaphoreType.DMA((7,)), pltpu.SemaphoreType.DMA],
        compiler_params=pltpu.CompilerParams(vmem_limit_bytes=VMEM_LIMIT),
    )(x)


def _allgather_chips(x, name):
    r, c = x.shape
    rh = r // 2

    def body(x_ref, out_ref, send_sems, recv_sems, local_sem):
        mx, my, mc = _place()
        j = 2 * mx + my
        chips = [(1 - mx, my), (mx, 1 - my), (1 - mx, 1 - my)]

        def half(jj, hc):
            return out_ref.at[jj, pl.ds(hc * rh, rh), :]

        mine = pltpu.make_async_copy(x_ref, out_ref.at[j], local_sem)
        mine.start()
        first = []
        for kk, (px, py) in enumerate(chips):
            cp = pltpu.make_async_remote_copy(
                src_ref=x_ref.at[pl.ds(mc * rh, rh), :], dst_ref=half(j, mc),
                send_sem=send_sems.at[kk], recv_sem=recv_sems.at[kk], device_id=(px, py, mc), device_id_type=MESH)
            cp.start()
            first.append(cp)
        passed = []
        for kk, (px, py) in enumerate(chips):
            jj = 2 * px + py
            pltpu.make_async_remote_copy(
                src_ref=x_ref.at[pl.ds(mc * rh, rh), :], dst_ref=half(jj, mc),
                send_sem=send_sems.at[kk], recv_sem=recv_sems.at[kk], device_id=(px, py, mc),
                device_id_type=MESH).wait_recv()
            cp = pltpu.make_async_remote_copy(
                src_ref=half(jj, mc), dst_ref=half(jj, mc), send_sem=send_sems.at[3 + kk],
                recv_sem=recv_sems.at[3 + kk], device_id=(mx, my, 1 - mc), device_id_type=MESH)
            cp.start()
            passed.append(cp)
        for kk, (px, py) in enumerate(chips):
            jj = 2 * px + py
            pltpu.make_async_remote_copy(
                src_ref=half(jj, 1 - mc), dst_ref=half(jj, 1 - mc), send_sem=send_sems.at[3 + kk],
                recv_sem=recv_sems.at[3 + kk], device_id=(mx, my, 1 - mc), device_id_type=MESH).wait_recv()
        for cp in first + passed:
            cp.wait_send()
        mine.wait()

    return pl.pallas_call(
        body, name=name,
        out_shape=jax.ShapeDtypeStruct((4, r, c), x.dtype),
        in_specs=[pl.BlockSpec(memory_space=pltpu.VMEM)],
        out_specs=pl.BlockSpec(memory_space=pltpu.VMEM),
        scratch_shapes=[pltpu.SemaphoreType.DMA((6,)), pltpu.SemaphoreType.DMA((6,)), pltpu.SemaphoreType.DMA],
        compiler_params=pltpu.CompilerParams(vmem_limit_bytes=VMEM_LIMIT),
    )(x)


RS_ROWS = 32


def _reduce_scatter_chips(g, name):
    _, r, c = g.shape
    rh = r // 2
    steps = rh // RS_ROWS

    def body(g_ref, out_ref, sib_ref, part_ref, got_ref, send_sems, recv_sems):
        mx, my, mc = _place()
        j = 2 * mx + my
        sibling = (mx, my, 1 - mc)
        chips = [(1 - mx, my), (mx, 1 - my), (1 - mx, 1 - my)]

        to_sib = pltpu.make_async_remote_copy(
            src_ref=g_ref.at[:, pl.ds((1 - mc) * rh, rh), :], dst_ref=sib_ref,
            send_sem=send_sems.at[0], recv_sem=recv_sems.at[0], device_id=sibling, device_id_type=MESH)
        to_sib.start()
        to_sib.wait()

        def add_sibling(step, carry):
            rows = pl.ds(pl.multiple_of(step * RS_ROWS, RS_ROWS), RS_ROWS)
            mine = g_ref[:, pl.ds(pl.multiple_of(mc * rh + step * RS_ROWS, RS_ROWS), RS_ROWS), :]
            part_ref[:, rows, :] = mine.astype(F32) + sib_ref[:, rows, :].astype(F32)
            return carry

        lax.fori_loop(0, steps, add_sibling, 0)

        def to_bf16(step, carry):
            rows = pl.ds(pl.multiple_of(step * RS_ROWS, RS_ROWS), RS_ROWS)
            sib_ref[:, rows, :] = part_ref[:, rows, :].astype(BF16)
            return carry

        lax.fori_loop(0, steps, to_bf16, 0)

        sends = []
        for kk, (px, py) in enumerate(chips):
            cp = pltpu.make_async_remote_copy(
                src_ref=sib_ref.at[2 * px + py], dst_ref=got_ref.at[kk],
                send_sem=send_sems.at[1 + kk], recv_sem=recv_sems.at[1 + kk],
                device_id=(px, py, mc), device_id_type=MESH)
            cp.start()
            sends.append(cp)
        for cp in sends:
            cp.wait()

        def total(step, carry):
            rows = pl.ds(pl.multiple_of(step * RS_ROWS, RS_ROWS), RS_ROWS)
            acc = part_ref[j, rows, :]
            for kk in range(3):
                acc = acc + got_ref[kk, rows, :].astype(F32)
            out_ref[pl.ds(pl.multiple_of(mc * rh + step * RS_ROWS, RS_ROWS), RS_ROWS), :] = acc
            return carry

        lax.fori_loop(0, steps, total, 0)

        done = pltpu.make_async_remote_copy(
            src_ref=out_ref.at[pl.ds(mc * rh, rh), :], dst_ref=out_ref.at[pl.ds(mc * rh, rh), :],
            send_sem=send_sems.at[4], recv_sem=recv_sems.at[4], device_id=sibling, device_id_type=MESH)
        done.start()
        done.wait_send()
        pltpu.make_async_remote_copy(
            src_ref=out_ref.at[pl.ds((1 - mc) * rh, rh), :], dst_ref=out_ref.at[pl.ds((1 - mc) * rh, rh), :],
            send_sem=send_sems.at[4], recv_sem=recv_sems.at[4], device_id=sibling, device_id_type=MESH).wait_recv()

    return pl.pallas_call(
        body, name=name,
        out_shape=jax.ShapeDtypeStruct((r, c), F32),
        in_specs=[pl.BlockSpec(memory_space=pltpu.VMEM)],
        out_specs=pl.BlockSpec(memory_space=pltpu.VMEM),
        scratch_shapes=[pltpu.VMEM((4, rh, c), BF16), pltpu.VMEM((4, rh, c), F32), pltpu.VMEM((3, rh, c), BF16),
                        pltpu.SemaphoreType.DMA((5,)), pltpu.SemaphoreType.DMA((5,))],
        compiler_params=pltpu.CompilerParams(vmem_limit_bytes=VMEM_LIMIT),
    )(g)


def _sum8(x, name):
    _, r, c = x.shape

    def body(x_ref, o_ref):
        acc = x_ref[0]
        for d in range(1, 8):
            acc = acc + x_ref[d]
        o_ref[...] = acc

    return pl.pallas_call(
        body, name=name, out_shape=jax.ShapeDtypeStruct((r, c), F32),
        in_specs=[pl.BlockSpec(memory_space=pltpu.VMEM)], out_specs=pl.BlockSpec(memory_space=pltpu.VMEM),
    )(x)


def _adamw(w, g, m, v, name):
    r, c = w.shape
    t = _pick(r, 256, SUBLANES)
    spec = pl.BlockSpec((t, c), lambda i: (i, 0))

    def body(w_ref, g_ref, m_ref, v_ref, d_ref, nm_ref, nv_ref):
        gv = g_ref[...]
        m_new = ADAM_B1 * m_ref[...] + (1.0 - ADAM_B1) * gv
        v_new = ADAM_B2 * v_ref[...] + (1.0 - ADAM_B2) * (gv * gv)
        m_hat = m_new / (1.0 - ADAM_B1 ** ADAM_STEP)
        v_hat = v_new / (1.0 - ADAM_B2 ** ADAM_STEP)
        d_ref[...] = -ADAM_LR * (m_hat / (jnp.sqrt(v_hat) + ADAM_EPS) + ADAM_WD * w_ref[...])
        nm_ref[...] = m_new
        nv_ref[...] = v_new

    return pl.pallas_call(
        body, name=name, grid=(r // t,), in_specs=[spec] * 4, out_specs=[spec] * 3,
        out_shape=[jax.ShapeDtypeStruct((r, c), F32)] * 3, compiler_params=_params(("parallel",)),
    )(w, g, m, v)


def _pack_rows(parts):
    rows, offs, o = [], [], 0
    for p in parts:
        f = p.reshape(-1)
        n = -(-f.shape[0] // (LANES * SUBLANES)) * SUBLANES
        rows.append(jnp.pad(f, (0, n * LANES - f.shape[0])).reshape(n, LANES))
        offs.append((o, n))
        o += n
    return jnp.concatenate(rows, 0), offs


def _unpack_rows(packed, offs, shapes):
    out = []
    for (o, n), shp in zip(offs, shapes):
        size = 1
        for d in shp:
            size *= d
        out.append(packed[o:o + n].reshape(-1)[:size].reshape(shp))
    return out


def _ffn_fwd(x, s, sh, g, w_in, w_out, tag):
    (h,) = _rowcall(lambda r, p: ([_modulate(r[0], p[0], p[1])], []), [x], [s, sh], [(x.shape[1], BF16)], [],
                    tile=512, name=tag + "_mod")
    gu = _mm(h, w_in, "nn", BF16, tag + "_in")
    (act,) = _rowcall(lambda r, p: ([_silu(r[0].astype(F32)) * r[1].astype(F32)], []),
                      [(gu, D_FF, 0), (gu, D_FF, 1)], [], [(D_FF, BF16)], [], tile=256, name=tag + "_act")
    f = _mm(act, w_out, "nn", F32, tag + "_out")
    (y,) = _rowcall(lambda r, p: ([r[0] + 0.5 * p[0] * r[1]], []), [x, f], [g], [(x.shape[1], F32)], [],
                    tile=512, name=tag + "_res")
    return y, (x, h, gu, act, f)


def _ffn_bwd(dy, saved, s, sh, g, w_in, w_out, tag):
    x, h, gu, act, f = saved
    d = x.shape[1]
    df, dg = _rowcall(lambda r, p: ([0.5 * p[0] * r[0]], [0.5 * jnp.sum(r[0] * r[1], 0, keepdims=True)]),
                      [dy, f], [g], [(d, BF16)], [(1, d)], tile=512, name=tag + "_bres")
    da = _mm(df, w_out, "nt", BF16, tag + "_bout")
    dw_out = _mm(act, df, "tn", BF16, tag + "_bwout")

    def act_bwd(r, p):
        gate, up, dav = r[0].astype(F32), r[1].astype(F32), r[2].astype(F32)
        _, vjp = jax.vjp(lambda a, b: _silu(a) * b, gate, up)
        dgate, dup = vjp(dav)
        return [jnp.concatenate([dgate, dup], 1)], []

    (dgu,) = _rowcall(act_bwd, [(gu, D_FF, 0), (gu, D_FF, 1), da], [], [(2 * D_FF, BF16)], [], tile=256,
                      name=tag + "_bact")
    dh = _mm(dgu, w_in, "nt", F32, tag + "_bin")
    dw_in = _mm(h, dgu, "tn", BF16, tag + "_bwin")

    def mod_bwd(r, p):
        _, vjp = jax.vjp(_modulate, r[0], p[0], p[1])
        dx, ds, dsh = vjp(r[1])
        return [r[2] + dx], [ds, dsh]

    dx, ds, dsh = _rowcall(mod_bwd, [x, dh, dy], [s, sh], [(d, F32)], [(1, d), (1, d)], tile=512, name=tag + "_bmod")
    return dx, (dsh, ds, dg), dw_in, dw_out


def _mixer_fwd(x, s, sh, g, wts, rope):
    w_in_p, conv8, a_log, dt_bias, wn, wq, w_uq_p, wkv, w_ukv, wqn, wqr, wkn, wkr, won, w_out = wts
    cos2, sin2 = rope
    d = x.shape[1]
    (h,) = _rowcall(lambda r, p: ([_modulate(r[0], p[0], p[1])], []), [x], [s, sh], [(d, BF16)], [],
                    tile=512, name="mix_mod")
    proj = _mm(h, w_in_p, "nn", F32, "mix_in")
    qkv_c = _conv_fwd(proj, conv8, "mix_conv")
    gab = (proj, LANES, 23)

    q, k, v, gb = _rowcall(
        lambda r, p: (list(_gdn_prep_core(_split(r[0], [HEAD] * 12), r[1], p[0], p[1])), []),
        [qkv_c, gab], [a_log, dt_bias], [(512, F32)] * 3 + [(LANES, F32)], [], tile=256, name="mix_gdn_prep")
    gdn_local = _gdn_local_fwd(q, k, v, gb, "mix_gdn_local")
    o_gdn, gdn_states = _gdn_scan_fwd(*gdn_local, "mix_gdn_scan")
    states = (gdn_local, gdn_states)

    cq, ckv, kr = (proj, 512, 4), (proj, 256, 10), (proj, LANES, 22)
    cqn, ckvn, k_rope = _rowcall(
        lambda r, p: (list(_mla_prep_core(r[0][:, :MLA_Q_LORA], r[1], r[2], r[3], r[4], p[0], p[1], p[2])), []),
        [cq, ckv, kr, cos2, sin2], [wq, wkv, wkr], [(MLA_Q_LORA, BF16), (MLA_KV_LORA, BF16), (LANES, F32)], [],
        tile=512, name="mix_mla_prep")
    qf = _mm(cqn, w_uq_p, "nn", F32, "mix_uq")
    kvf = _mm(ckvn, w_ukv, "nn", F32, "mix_ukv")

    def qk_prep(r, p):
        qparts = _split(r[0], [HEAD] * 8)
        kvparts = _split(r[1], [HEAD] * 8)
        qs, ks, vs = _qk_prep_core(qparts[:4], qparts[4:], kvparts[0::2], kvparts[1::2], r[2], r[3], r[4],
                                   p[0], p[1], p[2])
        return [jnp.concatenate(qs, 1), jnp.concatenate(ks, 1), jnp.concatenate(vs, 1)], []

    qa, ka, va = _rowcall(qk_prep, [qf, kvf, k_rope, cos2, sin2], [wqn, wqr, wkn],
                          [(4 * QK_PAD, BF16), (4 * QK_PAD, BF16), (4 * HEAD, BF16)], [], tile=256,
                          name="mix_qk_prep")
    o_b, lse = _attn_fwd(qa, ka, va, "mix_attn")

    gz = (proj, 512, 3)
    (mixed,) = _rowcall(
        lambda r, p: ([_mix_post_core(_split(r[0], HW4), _split(r[1], HW4), _split(r[2], HW4), p[0], p[1])], []),
        [o_gdn, gz, o_b], [wn, won], [(2 * 512, BF16)], [], tile=512, name="mix_post")
    y = _mm(mixed, w_out, "nn", F32, "mix_out")
    (x_out,) = _rowcall(lambda r, p: ([r[0] + p[0] * r[1]], []), [x, y], [g], [(d, F32)], [], tile=512,
                        name="mix_res")
    saved = (x, h, proj, qkv_c, q, k, v, gb, states, o_gdn, cqn, ckvn, k_rope, qf, kvf, qa, ka, va, o_b, lse,
             mixed, y)
    return x_out, saved


def _mixer_bwd(dy, saved, s, sh, g, wts, rope):
    w_in_p, conv8, a_log, dt_bias, wn, wq, w_uq_p, wkv, w_ukv, wqn, wqr, wkn, wkr, won, w_out = wts
    cos2, sin2 = rope
    (x, h, proj, qkv_c, q, k, v, gb, states, o_gdn, cqn, ckvn, k_rope, qf, kvf, qa, ka, va, o_b, lse,
     mixed, y) = saved
    d = x.shape[1]
    dyb, dg = _rowcall(lambda r, p: ([p[0] * r[0]], [jnp.sum(r[0] * r[1], 0, keepdims=True)]),
                       [dy, y], [g], [(d, BF16)], [(1, d)], tile=512, name="mix_bres")
    dmixed = _mm(dyb, w_out, "nt", F32, "mix_bout")
    dw_out = _mm(mixed, dyb, "tn", BF16, "mix_bwout")

    gz = (proj, 512, 3)

    def post_bwd(r, p):
        _, vjp = jax.vjp(_mix_post_core, _split(r[0], HW4), _split(r[1], HW4), _split(r[2], HW4), p[0], p[1])
        do, dz, dob, dwn, dwon = vjp(r[3])
        return [jnp.concatenate(do, 1), jnp.concatenate(dz, 1), jnp.concatenate(dob, 1)], [dwn, dwon]

    do_gdn, dgz, do_b, dwn, dwon = _rowcall(post_bwd, [o_gdn, gz, o_b, dmixed], [wn, won], [(512, F32)] * 3,
                                            [(1, HEAD), (1, HEAD)], tile=256, name="mix_bpost")

    dqa, dka, dva = _attn_bwd(qa, ka, va, o_b, do_b, lse, "mix_battn")

    def qk_bwd(r, p):
        qparts = _split(r[0], [HEAD] * 8)
        kvparts = _split(r[1], [HEAD] * 8)
        _, vjp = jax.vjp(_qk_prep_core, qparts[:4], qparts[4:], kvparts[0::2], kvparts[1::2], r[2], r[3], r[4],
                         p[0], p[1], p[2])
        cot = (_split(r[5], [QK_PAD] * 4), _split(r[6], [QK_PAD] * 4), _split(r[7], HW4))
        dqn, dqr, dkn, dvp, dkrope, _, _, dwqn, dwqr, dwkn = vjp(cot)
        dkv = []
        for a, b in zip(dkn, dvp):
            dkv += [a, b]
        return [jnp.concatenate(list(dqn) + list(dqr), 1), jnp.concatenate(dkv, 1), dkrope], [dwqn, dwqr, dwkn]

    dqf, dkvf, dk_rope, dwqn, dwqr, dwkn = _rowcall(
        qk_bwd, [qf, kvf, k_rope, cos2, sin2, dqa, dka, dva], [wqn, wqr, wkn],
        [(8 * HEAD, BF16), (8 * HEAD, BF16), (LANES, F32)], [(1, HEAD)] * 3, tile=256, name="mix_bqk_prep")
    dcqn = _mm(dqf, w_uq_p, "nt", F32, "mix_buq")
    dw_uq_p = _mm(cqn, dqf, "tn", F32, "mix_bwuq")
    dckvn = _mm(dkvf, w_ukv, "nt", F32, "mix_bukv")
    dw_ukv = _mm(ckvn, dkvf, "tn", F32, "mix_bwukv")

    cq, ckv, kr = (proj, 512, 4), (proj, 256, 10), (proj, LANES, 22)

    def mla_bwd(r, p):
        _, vjp = jax.vjp(_mla_prep_core, r[0][:, :MLA_Q_LORA], r[1], r[2], r[3], r[4], p[0], p[1], p[2])
        dcq, dckv, dkr, _, _, dwq, dwkv, dwkr = vjp((r[5], r[6], r[7]))
        pad = jnp.zeros((dcq.shape[0], 512 - MLA_Q_LORA), F32)
        return [jnp.concatenate([dcq, pad], 1), dckv, dkr], [dwq, dwkv, dwkr]

    dcq, dckv, dkr, dwq, dwkv, dwkr = _rowcall(
        mla_bwd, [cq, ckv, kr, cos2, sin2, dcqn, dckvn, dk_rope], [wq, wkv, wkr],
        [(512, F32), (MLA_KV_LORA, F32), (LANES, F32)], [(1, MLA_Q_LORA), (1, MLA_KV_LORA), (1, LANES)],
        tile=512, name="mix_bmla_prep")

    gdn_local, gdn_states = states
    d_local = _gdn_scan_bwd(*gdn_local, gdn_states, do_gdn, "mix_bgdn_scan")
    dq, dk, dv, dgb = _gdn_local_bwd(q, k, v, gb, *d_local, "mix_bgdn_local")
    gab = (proj, LANES, 23)

    def gdn_prep_bwd(r, p):
        _, vjp = jax.vjp(_gdn_prep_core, _split(r[0], [HEAD] * 12), r[1], p[0], p[1])
        dparts, dgab, da_log, ddt = vjp((r[2], r[3], r[4], r[5]))
        return [jnp.concatenate(dparts, 1), dgab], [da_log, ddt]

    dqkv_c, dgab, da_log, ddt = _rowcall(gdn_prep_bwd, [qkv_c, gab, dq, dk, dv, dgb], [a_log, dt_bias],
                                         [(1536, F32), (LANES, F32)], [(1, LANES), (1, LANES)], tile=256,
                                         name="mix_bgdn_prep")
    dqkv_pre, dconv8 = _conv_bwd(proj, dqkv_c, conv8, "mix_bconv")

    dproj = jnp.concatenate([dqkv_pre.astype(BF16), dgz.astype(BF16), dcq.astype(BF16), dckv.astype(BF16),
                             dkr.astype(BF16), dgab.astype(BF16)], axis=1)
    dh = _mm(dproj, w_in_p, "nt", F32, "mix_bin")
    dw_in_p = _mm(h, dproj, "tn", F32, "mix_bwin")

    def mod_bwd(r, p):
        _, vjp = jax.vjp(_modulate, r[0], p[0], p[1])
        dx, ds, dsh = vjp(r[1])
        return [r[2] + dx], [ds, dsh]

    dx, ds, dsh = _rowcall(mod_bwd, [x, dh, dy], [s, sh], [(d, F32)], [(1, d), (1, d)], tile=512, name="mix_bmod")
    small = dict(conv=dconv8, a_log=da_log, dt=ddt, wn=dwn, wq=dwq, wkv=dwkv, wqn=dwqn, wqr=dwqr, wkn=dwkn,
                 wkr=dwkr, won=dwon)
    return dx, (dsh, ds, dg), dw_in_p, dw_uq_p, dw_ukv, dw_out, small


def _pad_cols(a, n):
    return jnp.pad(a, ((0, 0),) * (a.ndim - 1) + ((0, n - a.shape[-1]),))


def _pack_w_in(w):
    z = lambda n: jnp.zeros((w.shape[0], n), w.dtype)
    return jnp.concatenate([w[:, 0:2048], w[:, 2056:2440], z(128), w[:, 2440:2696], w[:, 2696:2760], z(64),
                            w[:, 2048:2056], z(120)], axis=1)


def _unpack_w_in(wp):
    return jnp.concatenate([wp[:, 0:2048], wp[:, 2944:2952], wp[:, 2048:2432], wp[:, 2560:2816], wp[:, 2816:2880]],
                           axis=1)


def _pack_w_uq(w):
    z = jnp.zeros((w.shape[0], LANES - MLA_ROPE), w.dtype)
    nope = [w[:, h * 192:h * 192 + HEAD] for h in range(MLA_HEADS)]
    rope = []
    for h in range(MLA_HEADS):
        rope += [w[:, h * 192 + HEAD:(h + 1) * 192], z]
    return jnp.concatenate(nope + rope, axis=1)


def _unpack_w_uq(wp):
    cols = []
    for h in range(MLA_HEADS):
        cols += [wp[:, h * HEAD:(h + 1) * HEAD], wp[:, 512 + h * LANES:512 + h * LANES + MLA_ROPE]]
    return jnp.concatenate(cols, axis=1)


def _cols_to_chips(a):
    r, c = a.shape
    return a.reshape(r, 4, c // 4).transpose(1, 0, 2)


def _chips_to_cols(a):
    _, r, n = a.shape
    return a.transpose(1, 0, 2).reshape(r, 4 * n)


def _pad128(v, n=LANES):
    return _pad_cols(v.reshape(1, -1), n)


def kernel(x, c, positions, w_ada, b_ada, ffn1_w_in, ffn1_w_out, w_in, gdn_conv_w, gdn_a_log, gdn_dt_bias, gdn_norm_w, mla_q_norm_w, mla_w_uq, mla_kv_norm_w, mla_w_ukv, qkn_q_nope, qkn_q_rope, qkn_k_nope, qkn_k_rope, mla_out_norm_w, w_out, ffn2_w_in, ffn2_w_out, loss_target, m_w_ada, m_b_ada, m_ffn1_w_in, m_ffn1_w_out, m_w_in, m_gdn_conv_w, m_gdn_a_log, m_gdn_dt_bias, m_gdn_norm_w, m_mla_q_norm_w, m_mla_w_uq, m_mla_kv_norm_w, m_mla_w_ukv, m_qkn_q_nope, m_qkn_q_rope, m_qkn_k_nope, m_qkn_k_rope, m_mla_out_norm_w, m_w_out, m_ffn2_w_in, m_ffn2_w_out, v_w_ada, v_b_ada, v_ffn1_w_in, v_ffn1_w_out, v_w_in, v_gdn_conv_w, v_gdn_a_log, v_gdn_dt_bias, v_gdn_norm_w, v_mla_q_norm_w, v_mla_w_uq, v_mla_kv_norm_w, v_mla_w_ukv, v_qkn_q_nope, v_qkn_q_rope, v_qkn_k_nope, v_qkn_k_rope, v_mla_out_norm_w, v_w_out, v_ffn2_w_in, v_ffn2_w_out):
    weights = dict(w_ada=w_ada, b_ada=b_ada, ffn1_w_in=ffn1_w_in, ffn1_w_out=ffn1_w_out, w_in=w_in,
                   gdn_conv_w=gdn_conv_w, gdn_a_log=gdn_a_log, gdn_dt_bias=gdn_dt_bias, gdn_norm_w=gdn_norm_w,
                   mla_q_norm_w=mla_q_norm_w, mla_w_uq=mla_w_uq, mla_kv_norm_w=mla_kv_norm_w, mla_w_ukv=mla_w_ukv,
                   qkn_q_nope=qkn_q_nope, qkn_q_rope=qkn_q_rope, qkn_k_nope=qkn_k_nope, qkn_k_rope=qkn_k_rope,
                   mla_out_norm_w=mla_out_norm_w, w_out=w_out, ffn2_w_in=ffn2_w_in, ffn2_w_out=ffn2_w_out)
    moms_m = dict(w_ada=m_w_ada, b_ada=m_b_ada, ffn1_w_in=m_ffn1_w_in, ffn1_w_out=m_ffn1_w_out, w_in=m_w_in,
                  gdn_conv_w=m_gdn_conv_w, gdn_a_log=m_gdn_a_log, gdn_dt_bias=m_gdn_dt_bias,
                  gdn_norm_w=m_gdn_norm_w, mla_q_norm_w=m_mla_q_norm_w, mla_w_uq=m_mla_w_uq,
                  mla_kv_norm_w=m_mla_kv_norm_w, mla_w_ukv=m_mla_w_ukv, qkn_q_nope=m_qkn_q_nope,
                  qkn_q_rope=m_qkn_q_rope, qkn_k_nope=m_qkn_k_nope, qkn_k_rope=m_qkn_k_rope,
                  mla_out_norm_w=m_mla_out_norm_w, w_out=m_w_out, ffn2_w_in=m_ffn2_w_in, ffn2_w_out=m_ffn2_w_out)
    moms_v = dict(w_ada=v_w_ada, b_ada=v_b_ada, ffn1_w_in=v_ffn1_w_in, ffn1_w_out=v_ffn1_w_out, w_in=v_w_in,
                  gdn_conv_w=v_gdn_conv_w, gdn_a_log=v_gdn_a_log, gdn_dt_bias=v_gdn_dt_bias,
                  gdn_norm_w=v_gdn_norm_w, mla_q_norm_w=v_mla_q_norm_w, mla_w_uq=v_mla_w_uq,
                  mla_kv_norm_w=v_mla_kv_norm_w, mla_w_ukv=v_mla_w_ukv, qkn_q_nope=v_qkn_q_nope,
                  qkn_q_rope=v_qkn_q_rope, qkn_k_nope=v_qkn_k_nope, qkn_k_rope=v_qkn_k_rope,
                  mla_out_norm_w=v_mla_out_norm_w, w_out=v_w_out, ffn2_w_in=v_ffn2_w_in, ffn2_w_out=v_ffn2_w_out)
    names = list(weights)

    seq, d = x.shape[1], x.shape[2]
    x2d = x.reshape(seq, d)
    tgt = loss_target.reshape(seq, d)
    mx, my, mc = _place()
    chip = 2 * mx + my
    me = 2 * chip + mc
    n_mod = b_ada.shape[1] // d
    shard = w_ada.shape[2]

    half = MLA_ROPE // 2
    inv_freq = 10000.0 ** (-jnp.arange(half, dtype=F32) / half)
    ang = positions.astype(F32).reshape(seq, 1) * inv_freq
    cosv, sinv = jnp.cos(ang), jnp.sin(ang)
    cos2 = _pad_cols(jnp.concatenate([cosv, cosv], 1), LANES)
    sin2 = _pad_cols(jnp.concatenate([-sinv, sinv], 1), LANES)
    rope = (cos2, sin2)

    c_all = _allgather8(jnp.pad(c, ((0, SUBLANES - 1), (0, 0))), "gather_c")[:, 0, :]
    (sc_all,) = _rowcall(lambda r, p: ([_silu(r[0])], []), [c_all], [], [(d, F32)], [], tile=8, name="ada_silu")
    mod_part = _mm(sc_all, w_ada[0], "nn", F32, "ada_mm", hi=True)
    mod_all = _allgather8(mod_part, "gather_mod")
    mod_rows = lax.dynamic_index_in_dim(mod_all, me, axis=1, keepdims=False)
    mod_raw = jnp.concatenate([mod_rows[2 * jj] for jj in range(4)], axis=0).reshape(1, 4 * shard)
    (mod,) = _rowcall(lambda r, p: ([r[0] + r[1]], []),
                      [jnp.pad(mod_raw, ((0, 7), (0, 0))), jnp.pad(b_ada, ((0, 7), (0, 0)))], [],
                      [(4 * shard, F32)], [], tile=8, name="ada_bias")
    mods = [mod[0:1, i * d:(i + 1) * d] for i in range(n_mod)]
    sh1, s1, g1, sh2, s2, g2, sh3, s3, g3 = mods

    def gather_cols(w, name, pad_to=None):
        w2 = w[0].astype(BF16)
        n = w2.shape[1]
        if pad_to:
            w2 = _pad_cols(w2, pad_to)
        return _chips_to_cols(_allgather_chips(w2, name)[:, :, :n])

    def gather_rows(w, name):
        w2 = w[0].astype(BF16)
        return _allgather_chips(w2, name).reshape(4 * w2.shape[0], w2.shape[1])

    f1_in = gather_cols(ffn1_w_in, "gather_f1_in")
    f1_out = gather_rows(ffn1_w_out, "gather_f1_out")
    w_in_full = gather_cols(w_in, "gather_w_in", 768)
    w_uq_full = gather_cols(mla_w_uq, "gather_w_uq", 256)
    w_ukv_full = gather_cols(mla_w_ukv, "gather_w_ukv")
    w_out_full = gather_rows(w_out, "gather_w_out")
    f2_in = gather_cols(ffn2_w_in, "gather_f2_in")
    f2_out = gather_rows(ffn2_w_out, "gather_f2_out")
    conv_all = _allgather8(jnp.pad(gdn_conv_w[0], ((0, SUBLANES - CONV_K), (0, 0))), "gather_conv")
    conv8 = jnp.concatenate([conv_all[2 * jj] for jj in range(4)], axis=1)

    wts = (_pack_w_in(w_in_full), conv8, _pad128(gdn_a_log), _pad128(gdn_dt_bias), gdn_norm_w,
           mla_q_norm_w, _pack_w_uq(w_uq_full), mla_kv_norm_w, w_ukv_full, qkn_q_nope, _pad128(qkn_q_rope),
           qkn_k_nope, _pad128(qkn_k_rope), mla_out_norm_w, w_out_full)

    x1, sv1 = _ffn_fwd(x2d, s1, sh1, g1, f1_in, f1_out, "ffn1")
    xm, svm = _mixer_fwd(x1, s2, sh2, g2, wts, rope)
    x3, sv3 = _ffn_fwd(xm, s3, sh3, g3, f2_in, f2_out, "ffn2")

    def loss_fn(r, p):
        err = r[0] - r[1]
        part = 0.5 * jnp.sum(jnp.sum(err * err, axis=1, keepdims=True) * (1.0 / d), axis=0, keepdims=True)
        return [err * (1.0 / d)], [jnp.broadcast_to(part, (1, LANES))]

    dy, loss_part = _rowcall(loss_fn, [x3, tgt], [], [(d, F32)], [(1, LANES)], tile=512, name="loss")
    loss = lax.psum(loss_part[0, 0], ("x", "y", "c"))

    dxm, dmod3, dw_f2_in, dw_f2_out = _ffn_bwd(dy, sv3, s3, sh3, g3, f2_in, f2_out, "ffn2")
    dx1, dmod2, dw_in_p, dw_uq_p, dw_ukv, dw_out_m, small = _mixer_bwd(dxm, svm, s2, sh2, g2, wts, rope)
    dx0, dmod1, dw_f1_in, dw_f1_out = _ffn_bwd(dx1, sv1, s1, sh1, g1, f1_in, f1_out, "ffn1")
    grad_x = dx0.reshape(x.shape)

    dmod = jnp.concatenate(list(dmod1) + list(dmod2) + list(dmod3), axis=1)
    small_parts = [dmod, small["conv"][:CONV_K], small["a_log"], small["dt"], small["wn"], small["wq"],
                   small["wkv"], small["wqn"], small["wqr"], small["wkn"], small["wkr"], small["won"]]
    packed, offs = _pack_rows(small_parts)
    gathered = _allgather8(packed, "gather_small")
    total = _sum8(gathered, "sum_small")
    (g_b_ada, g_conv, g_a_log, g_dt, g_wn, g_wq, g_wkv, g_wqn, g_wqr, g_wkn, g_wkr, g_won) = _unpack_rows(
        total, offs, [p.shape for p in small_parts])
    dmod_all = _unpack_rows(gathered.reshape(-1, LANES),
                            [(dd * packed.shape[0] + offs[0][0], offs[0][1]) for dd in range(8)],
                            [dmod.shape] * 8)
    dmod_all = jnp.concatenate(dmod_all, axis=0)
    dmod_mine = lax.dynamic_slice_in_dim(dmod_all, chip * shard, shard, axis=1)

    def ada_grad(r, p):
        acc = jnp.zeros((r[0].shape[0], shard), F32)
        for b in range(8):
            acc = acc + r[0][:, b:b + 1] * p[0][b:b + 1, :]
        return [acc], []

    (g_w_ada,) = _rowcall(ada_grad, [_pad_cols(sc_all.T, LANES)], [dmod_mine], [(shard, F32)], [], tile=256,
                          name="ada_grad")

    grads = dict(
        w_ada=g_w_ada[None], b_ada=g_b_ada,
        gdn_conv_w=lax.dynamic_slice_in_dim(g_conv, chip * gdn_conv_w.shape[2], gdn_conv_w.shape[2], axis=1)[None],
        gdn_a_log=g_a_log[:, :GDN_HEADS], gdn_dt_bias=g_dt[:, :GDN_HEADS], gdn_norm_w=g_wn, mla_q_norm_w=g_wq,
        mla_kv_norm_w=g_wkv, qkn_q_nope=g_wqn, qkn_q_rope=g_wqr[:, :MLA_ROPE], qkn_k_nope=g_wkn,
        qkn_k_rope=g_wkr[:, :MLA_ROPE], mla_out_norm_w=g_won)

    def rs_cols(dw, name, pad_to=None):
        g4 = _cols_to_chips(dw).astype(BF16)
        n = g4.shape[2]
        if pad_to:
            g4 = _pad_cols(g4, pad_to)
        return _reduce_scatter_chips(g4, name)[:, :n][None]

    def rs_rows(dw, name):
        r, cc = dw.shape
        return _reduce_scatter_chips(dw.astype(BF16).reshape(4, r // 4, cc), name)[None]

    grads["ffn2_w_in"] = rs_cols(dw_f2_in, "rs_f2_in")
    grads["ffn2_w_out"] = rs_rows(dw_f2_out, "rs_f2_out")
    grads["w_in"] = rs_cols(_unpack_w_in(dw_in_p), "rs_w_in", 768)
    grads["mla_w_uq"] = rs_cols(_unpack_w_uq(dw_uq_p), "rs_w_uq", 256)
    grads["mla_w_ukv"] = rs_cols(dw_ukv, "rs_w_ukv")
    grads["w_out"] = rs_rows(dw_out_m, "rs_w_out")
    grads["ffn1_w_in"] = rs_cols(dw_f1_in, "rs_f1_in")
    grads["ffn1_w_out"] = rs_rows(dw_f1_out, "rs_f1_out")

    big = ["w_ada", "ffn1_w_in", "ffn1_w_out", "w_in", "mla_w_uq", "mla_w_ukv", "w_out", "ffn2_w_in", "ffn2_w_out"]
    delta, new_m, new_v = {}, {}, {}
    for nme in big:
        shp = weights[nme].shape
        dl, nm, nv = _adamw(weights[nme][0], grads[nme][0], moms_m[nme][0], moms_v[nme][0], "adamw_" + nme)
        delta[nme], new_m[nme], new_v[nme] = dl.reshape(shp), nm.reshape(shp), nv.reshape(shp)
    tiny = [nme for nme in names if nme not in big]
    shapes = [weights[nme].shape for nme in tiny]
    pw, poffs = _pack_rows([weights[nme] for nme in tiny])
    pg, _ = _pack_rows([grads[nme] for nme in tiny])
    pm, _ = _pack_rows([moms_m[nme] for nme in tiny])
    pv, _ = _pack_rows([moms_v[nme] for nme in tiny])
    pd, pnm, pnv = _adamw(pw, pg, pm, pv, "adamw_small")
    for nme, dl, nm, nv in zip(tiny, _unpack_rows(pd, poffs, shapes), _unpack_rows(pnm, poffs, shapes),
                               _unpack_rows(pnv, poffs, shapes)):
        delta[nme], new_m[nme], new_v[nme] = dl, nm, nv

    return (loss, grad_x, *[grads[nme].reshape(weights[nme].shape) for nme in names],
            *[delta[nme] for nme in names], *[new_m[nme] for nme in names], *[new_v[nme] for nme in names])
```

```python
import functools

import jax
import jax.numpy as jnp
from jax import lax
from jax.experimental import pallas as pl
from jax.experimental.pallas import tpu as pltpu

F32 = jnp.float32
BF16 = jnp.bfloat16
HI = lax.Precision.HIGHEST
MESH = pl.DeviceIdType.MESH

EPS = 1e-6
CHUNK = 64
D_FF = 2816
GDN_HEADS = 4
HEAD = 128
MLA_HEADS = 4
MLA_ROPE = 64
MLA_Q_LORA = 384
MLA_KV_LORA = 256
QK_PAD = 256
ATT_SCALE = (HEAD + MLA_ROPE) ** -0.5
N_PROJ = 3072

ADAM_LR, ADAM_B1, ADAM_B2, ADAM_EPS, ADAM_WD, ADAM_STEP = 0.001, 0.9, 0.999, 1e-08, 0.01, 10

LANES = 128
SUBLANES = 8
VMEM_LIMIT = 56 * 2 ** 20


def _params(sem=None):
    return pltpu.CompilerParams(dimension_semantics=sem, vmem_limit_bytes=VMEM_LIMIT)


def _pick(n, cap, align):
    best = None
    d = align
    while d <= min(n, cap):
        if n % d == 0:
            best = d
        d += align
    return best if best is not None else n


def _iota(shape, dim):
    return lax.broadcasted_iota(jnp.int32, shape, dim)


def _rowcall(fn, rows, params, out_rows, out_accs, *, tile, name):
    rows = [r if isinstance(r, tuple) else (r, r.shape[1], 0) for r in rows]
    s = rows[0][0].shape[-2]
    t = min(tile, s)
    n = s // t
    n_in = len(rows) + len(params)
    n_row_out = len(out_rows)

    in_specs = []
    for r in rows:
        if len(r) == 3:
            in_specs.append(pl.BlockSpec((t, r[1]), functools.partial(lambda i, b: (i, b), b=r[2])))
        else:
            in_specs.append(pl.BlockSpec((None, t, r[1]), functools.partial(lambda i, b, h: (h, i, b), b=r[2], h=r[3])))
    in_specs += [pl.BlockSpec(p.shape, lambda i: (0, 0)) for p in params]
    out_shape, out_specs = [], []
    for o in out_rows:
        if len(o) == 2:
            out_shape.append(jax.ShapeDtypeStruct((s, o[0]), o[1]))
            out_specs.append(pl.BlockSpec((t, o[0]), lambda i: (i, 0)))
        else:
            out_shape.append(jax.ShapeDtypeStruct((o[2], s, o[0]), o[1]))
            out_specs.append(pl.BlockSpec((o[2], t, o[0]), lambda i: (0, i, 0)))
    out_shape += [jax.ShapeDtypeStruct(shape, F32) for shape in out_accs]
    out_specs += [pl.BlockSpec(shape, lambda i: (0, 0)) for shape in out_accs]

    def body(*refs):
        ins = refs[:n_in]
        outs = refs[n_in:]
        i = pl.program_id(0)
        vals = [r[...] for r in ins]
        row_outs, acc_outs = fn(vals[:len(rows)], vals[len(rows):])
        for r, v in zip(outs[:n_row_out], row_outs):
            if isinstance(v, (list, tuple)):
                for hh, piece in enumerate(v):
                    r[hh] = piece.astype(r.dtype)
            else:
                r[...] = v.astype(r.dtype)
        if out_accs:
            @pl.when(i == 0)
            def _():
                for r in outs[n_row_out:]:
                    r[...] = jnp.zeros(r.shape, F32)
            for r, v in zip(outs[n_row_out:], acc_outs):
                r[...] += v

    res = pl.pallas_call(
        body, name=name, grid=(n,), in_specs=in_specs, out_specs=out_specs, out_shape=out_shape,
        compiler_params=_params(("arbitrary",) if out_accs else ("parallel",)),
    )(*[r[0] for r in rows], *params)
    return list(res)


MM_TILE_MN = 1536


def _mm(a, b, mode, out_dtype, name, hi=False):
    if mode == "nn":
        (m, k), (_, n) = a.shape, b.shape
        dims = (((1,), (0,)), ((), ()))
    elif mode == "nt":
        (m, k), (n, _) = a.shape, b.shape
        dims = (((1,), (1,)), ((), ()))
    else:
        (k, m), (_, n) = a.shape, b.shape
        dims = (((0,), (0,)), ((), ()))
    tm = _pick(m, MM_TILE_MN if mode == "tn" else 1024, LANES if mode == "tn" else 16)
    tn = _pick(n, MM_TILE_MN, LANES)
    tk = _pick(k, 1024 if mode == "tn" else MM_TILE_MN, LANES)
    nk = k // tk
    if mode == "nn":
        a_spec = pl.BlockSpec((tm, tk), lambda i, j, kk: (i, kk))
        b_spec = pl.BlockSpec((tk, tn), lambda i, j, kk: (kk, j))
    elif mode == "nt":
        a_spec = pl.BlockSpec((tm, tk), lambda i, j, kk: (i, kk))
        b_spec = pl.BlockSpec((tn, tk), lambda i, j, kk: (j, kk))
    else:
        a_spec = pl.BlockSpec((tk, tm), lambda i, j, kk: (kk, i))
        b_spec = pl.BlockSpec((tk, tn), lambda i, j, kk: (kk, j))

    def body(a_ref, b_ref, o_ref, acc_ref):
        kk = pl.program_id(2)

        @pl.when(kk == 0)
        def _():
            acc_ref[...] = jnp.zeros(acc_ref.shape, F32)

        av, bv = a_ref[...], b_ref[...]
        if hi:
            acc_ref[...] += lax.dot_general(av, bv, dims, precision=HI, preferred_element_type=F32)
        else:
            acc_ref[...] += lax.dot_general(av.astype(BF16), bv.astype(BF16), dims,
                                            preferred_element_type=F32)

        @pl.when(kk == nk - 1)
        def _():
            o_ref[...] = acc_ref[...].astype(o_ref.dtype)

    return pl.pallas_call(
        body, name=name, grid=(m // tm, n // tn, nk),
        in_specs=[a_spec, b_spec],
        out_specs=pl.BlockSpec((tm, tn), lambda i, j, kk: (i, j)),
        out_shape=jax.ShapeDtypeStruct((m, n), out_dtype),
        scratch_shapes=[pltpu.VMEM((tm, tn), F32)],
        compiler_params=_params(("parallel", "parallel", "arbitrary")),
    )(a, b)


def _rms(x, w=None, n=None):
    n = x.shape[-1] if n is None else n
    y = x * lax.rsqrt(jnp.sum(x * x, axis=-1, keepdims=True) * (1.0 / n) + EPS)
    return y if w is None else y * w


def _silu(x):
    return x * jax.nn.sigmoid(x)


def _softplus(x):
    return jnp.maximum(x, 0.0) + jnp.log1p(jnp.exp(-jnp.abs(x)))


def _split(x, widths):
    out, o = [], 0
    for w in widths:
        out.append(x[:, o:o + w])
        o += w
    return out


def _modulate(x, s, sh):
    return _rms(x) * (1.0 + s) + sh


def _rope_rot(x):
    r, c = _iota((LANES, LANES), 0), _iota((LANES, LANES), 1)
    half = MLA_ROPE // 2
    perm = (((r < half) & (c == r + half)) | ((r >= half) & (r < MLA_ROPE) & (c == r - half))).astype(F32)
    return jnp.dot(x, perm, precision=HI, preferred_element_type=F32)


def _rope(x, cos2, sin2):
    return x * cos2 + _rope_rot(x) * sin2


def _gdn_prep_core(qkv_parts, gab, a_log, dt_bias):
    act = [_silu(p) for p in qkv_parts]
    qs = [p * lax.rsqrt(jnp.sum(p * p, -1, keepdims=True) + EPS) * (HEAD ** -0.5) for p in act[:4]]
    ks = [p * lax.rsqrt(jnp.sum(p * p, -1, keepdims=True) + EPS) for p in act[4:8]]
    lane = _iota(gab.shape, 1)
    g = -jnp.exp(a_log) * _softplus(gab + dt_bias)
    beta = jax.nn.sigmoid(gab)
    gb = jnp.where(lane < GDN_HEADS, g, jnp.where(lane < 2 * GDN_HEADS, beta, 0.0))
    return (jnp.concatenate(qs, 1), jnp.concatenate(ks, 1), jnp.concatenate(act[8:], 1), gb)


def _mla_prep_core(cq, ckv, kr, cos2, sin2, wq, wkv, wkr):
    cqn = _rms(cq, wq)
    ckvn = _rms(ckv, wkv)
    k_rope = _rope(_rms(kr, wkr, MLA_ROPE), cos2, sin2)
    return cqn, ckvn, k_rope


def _qk_prep_core(qn_parts, qr_parts, kn_parts, v_parts, k_rope, cos2, sin2, wqn, wqr, wkn):
    qs, ks = [], []
    for h in range(MLA_HEADS):
        qn = _rms(qn_parts[h], wqn) * ATT_SCALE
        qr = _rope(_rms(qr_parts[h], wqr, MLA_ROPE), cos2, sin2) * ATT_SCALE
        qs.append(jnp.concatenate([qn, qr], 1))
        ks.append(jnp.concatenate([_rms(kn_parts[h], wkn), k_rope], 1))
    return qs, ks, list(v_parts)


def _mix_post_core(o_parts, gz_parts, ob_parts, wn, won):
    oa = [_rms(o, wn) * _silu(z) for o, z in zip(o_parts, gz_parts)]
    ob = [_rms(o, won) for o in ob_parts]
    return jnp.concatenate(oa + ob, 1)


CONV_K = 4
HALO = SUBLANES


def _conv_fwd(proj, w8, name):
    s = proj.shape[0]
    c = w8.shape[1]
    t = min(256, s)
    n = s // t
    hb = t // HALO

    def body(x_ref, prev_ref, w_ref, o_ref, buf):
        i = pl.program_id(0)
        buf[pl.ds(0, HALO), :] = jnp.where(i > 0, prev_ref[...], 0.0)
        buf[pl.ds(HALO, t), :] = x_ref[...]
        acc = jnp.zeros((t, c), F32)
        for k in range(CONV_K):
            acc = acc + w_ref[k:k + 1, :] * buf[pl.ds(HALO - (CONV_K - 1) + k, t), :]
        o_ref[...] = acc

    return pl.pallas_call(
        body, name=name, grid=(n,),
        in_specs=[pl.BlockSpec((t, c), lambda i: (i, 0)),
                  pl.BlockSpec((HALO, c), lambda i: (jnp.maximum(i * hb - 1, 0), 0)),
                  pl.BlockSpec(w8.shape, lambda i: (0, 0))],
        out_specs=pl.BlockSpec((t, c), lambda i: (i, 0)),
        out_shape=jax.ShapeDtypeStruct((s, c), F32),
        scratch_shapes=[pltpu.VMEM((t + HALO, c), F32)],
        compiler_params=_params(("parallel",)),
    )(proj, proj, w8)


def _conv_bwd(proj, dy, w8, name):
    s = proj.shape[0]
    c = w8.shape[1]
    t = min(256, s)
    n = s // t
    hb = t // HALO

    def body(x_ref, prev_ref, dy_ref, next_ref, w_ref, dx_ref, dw_ref, bufx, bufd):
        i = pl.program_id(0)
        bufx[pl.ds(0, HALO), :] = jnp.where(i > 0, prev_ref[...], 0.0)
        bufx[pl.ds(HALO, t), :] = x_ref[...]
        bufd[pl.ds(0, t), :] = dy_ref[...]
        bufd[pl.ds(t, HALO), :] = jnp.where(i < n - 1, next_ref[...], 0.0)

        @pl.when(i == 0)
        def _():
            dw_ref[...] = jnp.zeros(dw_ref.shape, F32)

        dyv = dy_ref[...]
        acc = jnp.zeros((t, c), F32)
        for k in range(CONV_K):
            acc = acc + w_ref[k:k + 1, :] * bufd[pl.ds(CONV_K - 1 - k, t), :]
            dw_ref[k:k + 1, :] += jnp.sum(dyv * bufx[pl.ds(HALO - (CONV_K - 1) + k, t), :], axis=0, keepdims=True)
        dx_ref[...] = acc

    return pl.pallas_call(
        body, name=name, grid=(n,),
        in_specs=[pl.BlockSpec((t, c), lambda i: (i, 0)),
                  pl.BlockSpec((HALO, c), lambda i: (jnp.maximum(i * hb - 1, 0), 0)),
                  pl.BlockSpec((t, c), lambda i: (i, 0)),
                  pl.BlockSpec((HALO, c), lambda i: (jnp.minimum((i + 1) * hb, s // HALO - 1), 0)),
                  pl.BlockSpec(w8.shape, lambda i: (0, 0))],
        out_specs=[pl.BlockSpec((t, c), lambda i: (i, 0)), pl.BlockSpec(w8.shape, lambda i: (0, 0))],
        out_shape=[jax.ShapeDtypeStruct((s, c), F32), jax.ShapeDtypeStruct(w8.shape, F32)],
        scratch_shapes=[pltpu.VMEM((t + HALO, c), F32), pltpu.VMEM((t + HALO, c), F32)],
        compiler_params=_params(("arbitrary",)),
    )(proj, proj, dy, dy, w8)


def _dot(a, b):
    return jnp.dot(a, b, precision=HI, preferred_element_type=F32)


def _dot_nt(a, b):
    return lax.dot_general(a, b, (((1,), (1,)), ((), ())), precision=HI, preferred_element_type=F32)


def _dot_tn(a, b):
    return lax.dot_general(a, b, (((0,), (0,)), ((), ())), precision=HI, preferred_element_type=F32)


_B_NN = (((2,), (1,)), ((0,), (0,)))
_B_NT = (((2,), (2,)), ((0,), (0,)))
_B_TN = (((1,), (1,)), ((0,), (0,)))


def _bdot_hi(a, b):
    return lax.dot_general(a, b, _B_NN, precision=HI, preferred_element_type=F32)


def _dot3(a, b, dims):
    return lax.dot_general(a, b, dims, precision=lax.Precision.HIGH, preferred_element_type=F32)


class _Dots:
    def __init__(self, diff):
        nn = lambda a, b: _dot3(a, b, _B_NN)
        nt = lambda a, b: _dot3(a, b, _B_NT)
        tn = lambda a, b: _dot3(a, b, _B_TN)
        if diff:
            def with_rule(f, bwd):
                g = jax.custom_vjp(f)
                g.defvjp(lambda a, b: (f(a, b), (a, b)), bwd)
                return g
            self.nn = with_rule(nn, lambda r, ct: (nt(ct, r[1]), tn(r[0], ct)))
            self.nt = with_rule(nt, lambda r, ct: (nn(ct, r[1]), tn(ct, r[0])))
            self.tn = with_rule(tn, lambda r, ct: (nt(r[1], ct), nn(r[0], ct)))
        else:
            self.nn, self.nt, self.tn = nn, nt, tn


def _unit_lower_inverse(a, dots):
    c = a.shape[-1]
    ri, ci = _iota(a.shape, 1), _iota(a.shape, 2)
    inner = (ri // 2) == (ci // 2)
    t = (ri == ci).astype(F32) - jnp.where(inner, a, 0.0)
    blk = 4
    while blk <= c:
        outer = (ri // blk) == (ci // blk)
        low = jnp.where(outer & jnp.logical_not(inner), a, 0.0)
        t = t - dots.nn(dots.nn(t, low), t)
        inner = outer
        blk *= 2
    return t


def _stack(xs):
    return jnp.concatenate([x[None] for x in xs], axis=0)


def _gdn_local(dots, q, k, v, gbs):
    b, c, _ = q.shape
    gcols, bcols = [], []
    for gb in gbs:
        lane = _iota(gb.shape, 1)
        for h in range(GDN_HEADS):
            gcols.append(jnp.sum(jnp.where(lane == h, gb, 0.0), axis=1, keepdims=True))
            bcols.append(jnp.sum(jnp.where(lane == GDN_HEADS + h, gb, 0.0), axis=1, keepdims=True))
    gcol, bcol = _stack(gcols), _stack(bcols)
    ri, ci = _iota((b, c, c), 1), _iota((b, c, c), 2)
    incl = ri >= ci
    tril = incl.astype(F32)
    g_cc = _bdot_hi(tril, jnp.broadcast_to(gcol, (b, c, c)))
    g_row = _bdot_hi(jnp.ones((b, c, c), F32), jnp.where(ri == ci, g_cc, 0.0))
    g_cl = _bdot_hi(tril, jnp.broadcast_to(gcol, (b, c, HEAD)))
    g_last = jnp.sum(jnp.broadcast_to(gcol, (b, c, HEAD)), axis=1, keepdims=True)
    decay = jnp.where(incl, jnp.exp(jnp.where(incl, g_cc - g_row, 0.0)), 0.0)
    kk = dots.nt(k, k)
    minv = _unit_lower_inverse(jnp.where(ri > ci, bcol * kk * decay, 0.0), dots)
    e_g = jnp.exp(g_cl)
    u = dots.nn(minv, v * bcol)
    wk = dots.nn(minv, k * (bcol * e_g))
    qk = dots.nt(q, k) * decay
    return u, wk, q * e_g, k * jnp.exp(g_last - g_cl), qk, jnp.exp(g_last)


def _gdn_scan(dots, states, u, wk, qd, kd, qk, gl_tile):
    lane, row = _iota(gl_tile.shape, 1), _iota(gl_tile.shape, 0)
    gl = _stack([
        jnp.sum(jnp.sum(jnp.where((lane == h) & (row == 0), gl_tile, 0.0), axis=1, keepdims=True),
                axis=0, keepdims=True) for h in range(GDN_HEADS)])
    v_new = u - dots.nn(wk, states)
    o = dots.nn(qd, states) + dots.nn(qk, v_new)
    return states * gl + dots.tn(kd, v_new), o


def _heads(x):
    return jnp.stack(_split(x, HW4))


GDN_W = GDN_HEADS * HEAD
HW4 = [HEAD] * GDN_HEADS
LOCAL_CHUNKS = 2
_CHUNK_ROWS = [pl.ds(cc * CHUNK, CHUNK) for cc in range(LOCAL_CHUNKS)]


def _chunk_heads(ref):
    return jnp.concatenate([_heads(ref[rows, :]) for rows in _CHUNK_ROWS], 0)


def _gdn_local_fwd(q, k, v, gb, name):
    s = q.shape[0]
    t = LOCAL_CHUNKS * CHUNK

    def body(q_ref, k_ref, v_ref, gb_ref, u_ref, wk_ref, qd_ref, kd_ref, qk_ref, gl_ref):
        u, wk, qd, kd, qk, gl = _gdn_local(_Dots(False), _chunk_heads(q_ref), _chunk_heads(k_ref),
                                           _chunk_heads(v_ref), [gb_ref[rows, :] for rows in _CHUNK_ROWS])
        lane = _iota((CHUNK, LANES), 1)
        for cc, rows in enumerate(_CHUNK_ROWS):
            gl_tile = jnp.zeros((CHUNK, LANES), F32)
            for h in range(GDN_HEADS):
                b, cols = cc * GDN_HEADS + h, pl.ds(h * HEAD, HEAD)
                u_ref[rows, cols] = u[b]
                wk_ref[rows, cols] = wk[b]
                qd_ref[rows, cols] = qd[b]
                kd_ref[rows, cols] = kd[b]
                qk_ref[h, rows, :] = qk[b]
                gl_tile = gl_tile + jnp.where(lane == h, gl[b], 0.0)
            gl_ref[rows, :] = gl_tile

    row = pl.BlockSpec((t, GDN_W), lambda i: (i, 0))
    lane = pl.BlockSpec((t, LANES), lambda i: (i, 0))
    qks = pl.BlockSpec((GDN_HEADS, t, CHUNK), lambda i: (0, i, 0))
    return pl.pallas_call(
        body, name=name, grid=(s // t,),
        in_specs=[row, row, row, lane],
        out_specs=[row, row, row, row, qks, lane],
        out_shape=[jax.ShapeDtypeStruct((s, GDN_W), F32)] * 4
        + [jax.ShapeDtypeStruct((GDN_HEADS, s, CHUNK), F32), jax.ShapeDtypeStruct((s, LANES), F32)],
        compiler_params=_params(("parallel",)),
    )(q, k, v, gb)


def _gdn_local_bwd(q, k, v, gb, du, dwk, dqd, dkd, dqk, dgl, name):
    s = q.shape[0]
    t = LOCAL_CHUNKS * CHUNK

    def body(q_ref, k_ref, v_ref, gb_ref, du_ref, dwk_ref, dqd_ref, dkd_ref, dqk_ref, dgl_ref,
             dq_ref, dk_ref, dv_ref, dgb_ref):
        _, vjp = jax.vjp(functools.partial(_gdn_local, _Dots(False)), _chunk_heads(q_ref), _chunk_heads(k_ref),
                         _chunk_heads(v_ref), [gb_ref[rows, :] for rows in _CHUNK_ROWS])
        lane = _iota((CHUNK, LANES), 1)
        dqk = jnp.stack([dqk_ref[h, rows, :] for rows in _CHUNK_ROWS for h in range(GDN_HEADS)])
        dgl = jnp.stack([jnp.sum(jnp.where(lane == h, dgl_ref[rows, :], 0.0), axis=0, keepdims=True)
                         for rows in _CHUNK_ROWS for h in range(GDN_HEADS)])
        d_q, d_k, d_v, d_gbs = vjp((_chunk_heads(du_ref), _chunk_heads(dwk_ref), _chunk_heads(dqd_ref),
                                    _chunk_heads(dkd_ref), dqk, dgl))
        for cc, rows in enumerate(_CHUNK_ROWS):
            for h in range(GDN_HEADS):
                b, cols = cc * GDN_HEADS + h, pl.ds(h * HEAD, HEAD)
                dq_ref[rows, cols] = d_q[b]
                dk_ref[rows, cols] = d_k[b]
                dv_ref[rows, cols] = d_v[b]
            dgb_ref[rows, :] = d_gbs[cc]

    row = pl.BlockSpec((t, GDN_W), lambda i: (i, 0))
    lane = pl.BlockSpec((t, LANES), lambda i: (i, 0))
    qks = pl.BlockSpec((GDN_HEADS, t, CHUNK), lambda i: (0, i, 0))
    return pl.pallas_call(
        body, name=name, grid=(s // t,),
        in_specs=[row, row, row, lane, row, row, row, row, qks, lane],
        out_specs=[row, row, row, lane],
        out_shape=[jax.ShapeDtypeStruct((s, GDN_W), F32)] * 3 + [jax.ShapeDtypeStruct((s, LANES), F32)],
        compiler_params=_params(("parallel",)),
    )(q, k, v, gb, du, dwk, dqd, dkd, dqk, dgl)


def _gdn_scan_fwd(u, wk, qd, kd, qk, gl, name):
    s = u.shape[0]
    nc = s // CHUNK

    def body(u_ref, wk_ref, qd_ref, kd_ref, qk_ref, gl_ref, o_ref, st_ref, state):
        i = pl.program_id(0)

        @pl.when(i == 0)
        def _():
            state[...] = jnp.zeros(state.shape, F32)

        st_ref[...] = state[...]
        new_states, o = _gdn_scan(_Dots(False), state[...], _heads(u_ref[...]), _heads(wk_ref[...]),
                                  _heads(qd_ref[...]), _heads(kd_ref[...]), qk_ref[...], gl_ref[...])
        state[...] = new_states
        o_ref[...] = jnp.concatenate([o[h] for h in range(GDN_HEADS)], 1)

    row = pl.BlockSpec((CHUNK, GDN_W), lambda i: (i, 0))
    return pl.pallas_call(
        body, name=name, grid=(nc,),
        in_specs=[row, row, row, row, pl.BlockSpec((GDN_HEADS, CHUNK, CHUNK), lambda i: (0, i, 0)),
                  pl.BlockSpec((CHUNK, LANES), lambda i: (i, 0))],
        out_specs=[row, pl.BlockSpec((None, GDN_HEADS, HEAD, HEAD), lambda i: (i, 0, 0, 0))],
        out_shape=[jax.ShapeDtypeStruct((s, GDN_W), F32),
                   jax.ShapeDtypeStruct((nc, GDN_HEADS, HEAD, HEAD), F32)],
        scratch_shapes=[pltpu.VMEM((GDN_HEADS, HEAD, HEAD), F32)],
        compiler_params=_params(("arbitrary",)),
    )(u, wk, qd, kd, qk, gl)


def _gdn_scan_bwd(u, wk, qd, kd, qk, gl, st, do, name):
    s = u.shape[0]
    nc = s // CHUNK

    def body(u_ref, wk_ref, qd_ref, kd_ref, qk_ref, gl_ref, st_ref, do_ref,
             du_ref, dwk_ref, dqd_ref, dkd_ref, dqk_ref, dgl_ref, dstate):
        i = pl.program_id(0)

        @pl.when(i == 0)
        def _():
            dstate[...] = jnp.zeros(dstate.shape, F32)

        _, vjp = jax.vjp(functools.partial(_gdn_scan, _Dots(False)), st_ref[...], _heads(u_ref[...]),
                         _heads(wk_ref[...]), _heads(qd_ref[...]), _heads(kd_ref[...]), qk_ref[...], gl_ref[...])
        d_states, d_u, d_wk, d_qd, d_kd, d_qk, d_gl = vjp((dstate[...], _heads(do_ref[...])))
        dstate[...] = d_states
        dqk_ref[...] = d_qk
        unheads = lambda x: jnp.concatenate([x[h] for h in range(GDN_HEADS)], 1)
        du_ref[...] = unheads(d_u)
        dwk_ref[...] = unheads(d_wk)
        dqd_ref[...] = unheads(d_qd)
        dkd_ref[...] = unheads(d_kd)
        dgl_ref[...] = d_gl

    rev = lambda i: (nc - 1 - i, 0)
    row = pl.BlockSpec((CHUNK, GDN_W), rev)
    lane = pl.BlockSpec((CHUNK, LANES), rev)
    qks = pl.BlockSpec((GDN_HEADS, CHUNK, CHUNK), lambda i: (0, nc - 1 - i, 0))
    return pl.pallas_call(
        body, name=name, grid=(nc,),
        in_specs=[row, row, row, row, qks, lane,
                  pl.BlockSpec((None, GDN_HEADS, HEAD, HEAD), lambda i: (nc - 1 - i, 0, 0, 0)), row],
        out_specs=[row, row, row, row, qks, lane],
        out_shape=[jax.ShapeDtypeStruct((s, GDN_W), F32)] * 4
        + [jax.ShapeDtypeStruct((GDN_HEADS, s, CHUNK), F32), jax.ShapeDtypeStruct((s, LANES), F32)],
        scratch_shapes=[pltpu.VMEM((GDN_HEADS, HEAD, HEAD), F32)],
        compiler_params=_params(("arbitrary",)),
    )(u, wk, qd, kd, qk, gl, st, do)


def _chunk_mask(i, j, t):
    r = i * t + _iota((t, t), 0)
    c = j * t + _iota((t, t), 1)
    return (r // CHUNK) >= (c // CHUNK)


ATT_TILE = 512
ATT_Q_TILES = 2


def _attn_fwd(q, k, v, name):
    nh, s = MLA_HEADS, q.shape[0]
    tk = min(ATT_TILE, s)
    tq = min(ATT_Q_TILES * tk, s)
    qk = tq // tk
    nq, n = s // tq, s // tk
    nt = (((1,), (1,)), ((), ()))

    def body(q_ref, k_ref, v_ref, o_ref, lse_ref, m_sc, l_sc, acc_sc):
        i, j = pl.program_id(1), pl.program_id(2)

        @pl.when(j == 0)
        def _():
            m_sc[...] = jnp.full(m_sc.shape, -jnp.inf, F32)
            l_sc[...] = jnp.zeros(l_sc.shape, F32)
            acc_sc[...] = jnp.zeros(acc_sc.shape, F32)

        def step(masked):
            sc = lax.dot_general(q_ref[...], k_ref[...], nt, preferred_element_type=F32)
            if masked:
                r = i * tq + _iota((tq, tk), 0)
                c = j * tk + _iota((tq, tk), 1)
                sc = jnp.where((r // CHUNK) >= (c // CHUNK), sc, -jnp.inf)
            m_prev = m_sc[:, :1]
            m_new = jnp.maximum(m_prev, jnp.max(sc, axis=1, keepdims=True))
            alpha = jnp.exp(m_prev - m_new)
            p = jnp.exp(sc - m_new)
            l_sc[...] = jnp.broadcast_to(alpha * l_sc[:, :1] + jnp.sum(p, axis=1, keepdims=True), l_sc.shape)
            acc_sc[...] = alpha * acc_sc[...] + jnp.dot(p.astype(BF16), v_ref[...], preferred_element_type=F32)
            m_sc[...] = jnp.broadcast_to(m_new, m_sc.shape)

        pl.when(j < i * qk)(lambda: step(False))
        pl.when(j // qk == i)(lambda: step(True))

        @pl.when(j == n - 1)
        def _():
            o_ref[...] = acc_sc[...] / l_sc[:, :1]
            lse_ref[...] = m_sc[...] + jnp.log(l_sc[...])

    qrow = lambda h, i, j: (i, h)
    krow = lambda h, i, j: (jnp.minimum(j, (i + 1) * qk - 1), h)
    return pl.pallas_call(
        body, name=name, grid=(nh, nq, n),
        in_specs=[pl.BlockSpec((tq, QK_PAD), qrow), pl.BlockSpec((tk, QK_PAD), krow),
                  pl.BlockSpec((tk, HEAD), krow)],
        out_specs=[pl.BlockSpec((tq, HEAD), qrow), pl.BlockSpec((None, tq, LANES), lambda h, i, j: (h, i, 0))],
        out_shape=[jax.ShapeDtypeStruct((s, nh * HEAD), F32), jax.ShapeDtypeStruct((nh, s, LANES), F32)],
        scratch_shapes=[pltpu.VMEM((tq, LANES), F32), pltpu.VMEM((tq, LANES), F32), pltpu.VMEM((tq, HEAD), F32)],
        compiler_params=_params(("parallel", "parallel", "arbitrary")),
    )(q, k, v)


def _attn_bwd(q, k, v, o, do, lse, name):
    nh, s = MLA_HEADS, q.shape[0]
    t = min(ATT_TILE, s)
    n = s // t
    tn = (((0,), (0,)), ((), ()))
    nt = (((1,), (1,)), ((), ()))

    def body(q_ref, k_ref, v_ref, o_ref, do_ref, lse_ref, dq_ref, dk_ref, dv_ref, dk_acc, dv_acc, dq_acc):
        j, i = pl.program_id(1), pl.program_id(2)

        @pl.when(i + j == 0)
        def _():
            dq_acc[...] = jnp.zeros(dq_acc.shape, F32)

        @pl.when(i == 0)
        def _():
            dk_acc[...] = jnp.zeros(dk_acc.shape, F32)
            dv_acc[...] = jnp.zeros(dv_acc.shape, F32)

        def step(masked):
            qv, kv, do = q_ref[...], k_ref[...], do_ref[...]
            sc = lax.dot_general(qv, kv, nt, preferred_element_type=F32)
            p = jnp.exp(sc - lse_ref[:, :1])
            if masked:
                p = jnp.where(_chunk_mask(i, j, t), p, 0.0)
            dob = do.astype(BF16)
            dp = lax.dot_general(dob, v_ref[...], nt, preferred_element_type=F32)
            ds = (p * (dp - jnp.sum(do * o_ref[...], axis=1, keepdims=True))).astype(BF16)
            dv_acc[...] += lax.dot_general(p.astype(BF16), dob, tn, preferred_element_type=F32)
            dk_acc[...] += lax.dot_general(ds, qv, tn, preferred_element_type=F32)
            rows = pl.ds(pl.multiple_of(i * t, t), t)
            dq_acc[rows, :] += jnp.dot(ds, kv, preferred_element_type=F32)

        pl.when(i > j)(lambda: step(False))
        pl.when(i == j)(lambda: step(True))

        @pl.when(i == n - 1)
        def _():
            dk_ref[...] = dk_acc[...]
            dv_ref[...] = dv_acc[...]

        @pl.when(i + j == 2 * (n - 1))
        def _():
            dq_ref[...] = dq_acc[...]

    qrow = lambda h, j, i: (jnp.maximum(i, j), h)
    krow = lambda h, j, i: (j, h)
    return pl.pallas_call(
        body, name=name, grid=(nh, n, n),
        in_specs=[pl.BlockSpec((t, QK_PAD), qrow), pl.BlockSpec((t, QK_PAD), krow), pl.BlockSpec((t, HEAD), krow),
                  pl.BlockSpec((t, HEAD), qrow), pl.BlockSpec((t, HEAD), qrow),
                  pl.BlockSpec((None, t, LANES), lambda h, j, i: (h, jnp.maximum(i, j), 0))],
        out_specs=[pl.BlockSpec((s, QK_PAD), lambda h, j, i: (0, h)),
                   pl.BlockSpec((t, QK_PAD), krow), pl.BlockSpec((t, HEAD), krow)],
        out_shape=[jax.ShapeDtypeStruct((s, nh * QK_PAD), F32), jax.ShapeDtypeStruct((s, nh * QK_PAD), F32),
                   jax.ShapeDtypeStruct((s, nh * HEAD), F32)],
        scratch_shapes=[pltpu.VMEM((t, QK_PAD), F32), pltpu.VMEM((t, HEAD), F32), pltpu.VMEM((s, QK_PAD), F32)],
        compiler_params=_params(("arbitrary", "arbitrary", "arbitrary")),
    )(q, k, v, o, do, lse)


def _place():
    return lax.axis_index("x"), lax.axis_index("y"), lax.axis_index("c")


def _allgather8(x, name):
    r, c = x.shape

    def body(x_ref, out_ref, send_sems, recv_sems, local_sem):
        mx, my, mc = _place()
        me = 4 * mx + 2 * my + mc
        mine = pltpu.make_async_copy(x_ref, out_ref.at[me], local_sem)
        mine.start()
        copies = []
        for d in range(1, 8):
            px = 1 - mx if d & 4 else mx
            py = 1 - my if d & 2 else my
            pc = 1 - mc if d & 1 else mc
            cp = pltpu.make_async_remote_copy(
                src_ref=x_ref, dst_ref=out_ref.at[me], send_sem=send_sems.at[d - 1], recv_sem=recv_sems.at[d - 1],
                device_id=(px, py, pc), device_id_type=MESH)
            cp.start()
            copies.append(cp)
        for cp in copies:
            cp.wait()
        mine.wait()

    return pl.pallas_call(
        body, name=name,
        out_shape=jax.ShapeDtypeStruct((8, r, c), x.dtype),
        in_specs=[pl.BlockSpec(memory_space=pltpu.VMEM)],
        out_specs=pl.BlockSpec(memory_space=pltpu.VMEM),
        scratch_shapes=[pltpu.SemaphoreType.DMA((7,)), pltpu.SemaphoreType.DMA((7,)), pltpu.SemaphoreType.DMA],
        compiler_params=pltpu.CompilerParams(vmem_limit_bytes=VMEM_LIMIT),
    )(x)


def _allgather_chips(x, name):
    r, c = x.shape
    rh = r // 2

    def body(x_ref, out_ref, send_sems, recv_sems, local_sem):
        mx, my, mc = _place()
        j = 2 * mx + my
        chips = [(1 - mx, my), (mx, 1 - my), (1 - mx, 1 - my)]

        def half(jj, hc):
            return out_ref.at[jj, pl.ds(hc * rh, rh), :]

        mine = pltpu.make_async_copy(x_ref, out_ref.at[j], local_sem)
        mine.start()
        first = []
        for kk, (px, py) in enumerate(chips):
            cp = pltpu.make_async_remote_copy(
                src_ref=x_ref.at[pl.ds(mc * rh, rh), :], dst_ref=half(j, mc),
                send_sem=send_sems.at[kk], recv_sem=recv_sems.at[kk], device_id=(px, py, mc), device_id_type=MESH)
            cp.start()
            first.append(cp)
        passed = []
        for kk, (px, py) in enumerate(chips):
            jj = 2 * px + py
            pltpu.make_async_remote_copy(
                src_ref=x_ref.at[pl.ds(mc * rh, rh), :], dst_ref=half(jj, mc),
                send_sem=send_sems.at[kk], recv_sem=recv_sems.at[kk], device_id=(px, py, mc),
                device_id_type=MESH).wait_recv()
            cp = pltpu.make_async_remote_copy(
                src_ref=half(jj, mc), dst_ref=half(jj, mc), send_sem=send_sems.at[3 + kk],
                recv_sem=recv_sems.at[3 + kk], device_id=(mx, my, 1 - mc), device_id_type=MESH)
            cp.start()
            passed.append(cp)
        for kk, (px, py) in enumerate(chips):
            jj = 2 * px + py
            pltpu.make_async_remote_copy(
                src_ref=half(jj, 1 - mc), dst_ref=half(jj, 1 - mc), send_sem=send_sems.at[3 + kk],
                recv_sem=recv_sems.at[3 + kk], device_id=(mx, my, 1 - mc), device_id_type=MESH).wait_recv()
        for cp in first + passed:
            cp.wait_send()
        mine.wait()

    return pl.pallas_call(
        body, name=name,
        out_shape=jax.ShapeDtypeStruct((4, r, c), x.dtype),
        in_specs=[pl.BlockSpec(memory_space=pltpu.VMEM)],
        out_specs=pl.BlockSpec(memory_space=pltpu.VMEM),
        scratch_shapes=[pltpu.SemaphoreType.DMA((6,)), pltpu.SemaphoreType.DMA((6,)), pltpu.SemaphoreType.DMA],
        compiler_params=pltpu.CompilerParams(vmem_limit_bytes=VMEM_LIMIT),
    )(x)


RS_ROWS = 32


def _reduce_scatter_chips(g, name):
    _, r, c = g.shape
    rh = r // 2
    steps = rh // RS_ROWS

    def body(g_ref, out_ref, sib_ref, part_ref, got_ref, send_sems, recv_sems):
        mx, my, mc = _place()
        j = 2 * mx + my
        sibling = (mx, my, 1 - mc)
        chips = [(1 - mx, my), (mx, 1 - my), (1 - mx, 1 - my)]

        to_sib = pltpu.make_async_remote_copy(
            src_ref=g_ref.at[:, pl.ds((1 - mc) * rh, rh), :], dst_ref=sib_ref,
            send_sem=send_sems.at[0], recv_sem=recv_sems.at[0], device_id=sibling, device_id_type=MESH)
        to_sib.start()
        to_sib.wait()

        def add_sibling(step, carry):
            rows = pl.ds(pl.multiple_of(step * RS_ROWS, RS_ROWS), RS_ROWS)
            mine = g_ref[:, pl.ds(pl.multiple_of(mc * rh + step * RS_ROWS, RS_ROWS), RS_ROWS), :]
            part_ref[:, rows, :] = mine.astype(F32) + sib_ref[:, rows, :].astype(F32)
            return carry

        lax.fori_loop(0, steps, add_sibling, 0)

        def to_bf16(step, carry):
            rows = pl.ds(pl.multiple_of(step * RS_ROWS, RS_ROWS), RS_ROWS)
            sib_ref[:, rows, :] = part_ref[:, rows, :].astype(BF16)
            return carry

        lax.fori_loop(0, steps, to_bf16, 0)

        sends = []
        for kk, (px, py) in enumerate(chips):
            cp = pltpu.make_async_remote_copy(
                src_ref=sib_ref.at[2 * px + py], dst_ref=got_ref.at[kk],
                send_sem=send_sems.at[1 + kk], recv_sem=recv_sems.at[1 + kk],
                device_id=(px, py, mc), device_id_type=MESH)
            cp.start()
            sends.append(cp)
        for cp in sends:
            cp.wait()

        def total(step, carry):
            rows = pl.ds(pl.multiple_of(step * RS_ROWS, RS_ROWS), RS_ROWS)
            acc = part_ref[j, rows, :]
            for kk in range(3):
                acc = acc + got_ref[kk, rows, :].astype(F32)
            out_ref[pl.ds(pl.multiple_of(mc * rh + step * RS_ROWS, RS_ROWS), RS_ROWS), :] = acc
            return carry

        lax.fori_loop(0, steps, total, 0)

        done = pltpu.make_async_remote_copy(
            src_ref=out_ref.at[pl.ds(mc * rh, rh), :], dst_ref=out_ref.at[pl.ds(mc * rh, rh), :],
            send_sem=send_sems.at[4], recv_sem=recv_sems.at[4], device_id=sibling, device_id_type=MESH)
        done.start()
        done.wait_send()
        pltpu.make_async_remote_copy(
            src_ref=out_ref.at[pl.ds((1 - mc) * rh, rh), :], dst_ref=out_ref.at[pl.ds((1 - mc) * rh, rh), :],
            send_sem=send_sems.at[4], recv_sem=recv_sems.at[4], device_id=sibling, device_id_type=MESH).wait_recv()

    return pl.pallas_call(
        body, name=name,
        out_shape=jax.ShapeDtypeStruct((r, c), F32),
        in_specs=[pl.BlockSpec(memory_space=pltpu.VMEM)],
        out_specs=pl.BlockSpec(memory_space=pltpu.VMEM),
        scratch_shapes=[pltpu.VMEM((4, rh, c), BF16), pltpu.VMEM((4, rh, c), F32), pltpu.VMEM((3, rh, c), BF16),
                        pltpu.SemaphoreType.DMA((5,)), pltpu.SemaphoreType.DMA((5,))],
        compiler_params=pltpu.CompilerParams(vmem_limit_bytes=VMEM_LIMIT),
    )(g)


def _sum8(x, name):
    _, r, c = x.shape

    def body(x_ref, o_ref):
        acc = x_ref[0]
        for d in range(1, 8):
            acc = acc + x_ref[d]
        o_ref[...] = acc

    return pl.pallas_call(
        body, name=name, out_shape=jax.ShapeDtypeStruct((r, c), F32),
        in_specs=[pl.BlockSpec(memory_space=pltpu.VMEM)], out_specs=pl.BlockSpec(memory_space=pltpu.VMEM),
    )(x)


def _adamw(w, g, m, v, name):
    r, c = w.shape
    t = _pick(r, 256, SUBLANES)
    spec = pl.BlockSpec((t, c), lambda i: (i, 0))

    def body(w_ref, g_ref, m_ref, v_ref, d_ref, nm_ref, nv_ref):
        gv = g_ref[...]
        m_new = ADAM_B1 * m_ref[...] + (1.0 - ADAM_B1) * gv
        v_new = ADAM_B2 * v_ref[...] + (1.0 - ADAM_B2) * (gv * gv)
        m_hat = m_new / (1.0 - ADAM_B1 ** ADAM_STEP)
        v_hat = v_new / (1.0 - ADAM_B2 ** ADAM_STEP)
        d_ref[...] = -ADAM_LR * (m_hat / (jnp.sqrt(v_hat) + ADAM_EPS) + ADAM_WD * w_ref[...])
        nm_ref[...] = m_new
        nv_ref[...] = v_new

    return pl.pallas_call(
        body, name=name, grid=(r // t,), in_specs=[spec] * 4, out_specs=[spec] * 3,
        out_shape=[jax.ShapeDtypeStruct((r, c), F32)] * 3, compiler_params=_params(("parallel",)),
    )(w, g, m, v)


def _pack_rows(parts):
    rows, offs, o = [], [], 0
    for p in parts:
        f = p.reshape(-1)
        n = -(-f.shape[0] // (LANES * SUBLANES)) * SUBLANES
        rows.append(jnp.pad(f, (0, n * LANES - f.shape[0])).reshape(n, LANES))
        offs.append((o, n))
        o += n
    return jnp.concatenate(rows, 0), offs


def _unpack_rows(packed, offs, shapes):
    out = []
    for (o, n), shp in zip(offs, shapes):
        size = 1
        for d in shp:
            size *= d
        out.append(packed[o:o + n].reshape(-1)[:size].reshape(shp))
    return out


def _ffn_fwd(x, s, sh, g, w_in, w_out, tag):
    (h,) = _rowcall(lambda r, p: ([_modulate(r[0], p[0], p[1])], []), [x], [s, sh], [(x.shape[1], BF16)], [],
                    tile=512, name=tag + "_mod")
    gu = _mm(h, w_in, "nn", BF16, tag + "_in")
    (act,) = _rowcall(lambda r, p: ([_silu(r[0].astype(F32)) * r[1].astype(F32)], []),
                      [(gu, D_FF, 0), (gu, D_FF, 1)], [], [(D_FF, BF16)], [], tile=256, name=tag + "_act")
    f = _mm(act, w_out, "nn", F32, tag + "_out")
    (y,) = _rowcall(lambda r, p: ([r[0] + 0.5 * p[0] * r[1]], []), [x, f], [g], [(x.shape[1], F32)], [],
                    tile=512, name=tag + "_res")
    return y, (x, h, gu, act, f)


def _ffn_bwd(dy, saved, s, sh, g, w_in, w_out, tag):
    x, h, gu, act, f = saved
    d = x.shape[1]
    df, dg = _rowcall(lambda r, p: ([0.5 * p[0] * r[0]], [0.5 * jnp.sum(r[0] * r[1], 0, keepdims=True)]),
                      [dy, f], [g], [(d, BF16)], [(1, d)], tile=512, name=tag + "_bres")
    da = _mm(df, w_out, "nt", BF16, tag + "_bout")
    dw_out = _mm(act, df, "tn", BF16, tag + "_bwout")

    def act_bwd(r, p):
        gate, up, dav = r[0].astype(F32), r[1].astype(F32), r[2].astype(F32)
        _, vjp = jax.vjp(lambda a, b: _silu(a) * b, gate, up)
        dgate, dup = vjp(dav)
        return [jnp.concatenate([dgate, dup], 1)], []

    (dgu,) = _rowcall(act_bwd, [(gu, D_FF, 0), (gu, D_FF, 1), da], [], [(2 * D_FF, BF16)], [], tile=256,
                      name=tag + "_bact")
    dh = _mm(dgu, w_in, "nt", F32, tag + "_bin")
    dw_in = _mm(h, dgu, "tn", BF16, tag + "_bwin")

    def mod_bwd(r, p):
        _, vjp = jax.vjp(_modulate, r[0], p[0], p[1])
        dx, ds, dsh = vjp(r[1])
        return [r[2] + dx], [ds, dsh]

    dx, ds, dsh = _rowcall(mod_bwd, [x, dh, dy], [s, sh], [(d, F32)], [(1, d), (1, d)], tile=512, name=tag + "_bmod")
    return dx, (dsh, ds, dg), dw_in, dw_out


def _mixer_fwd(x, s, sh, g, wts, rope):
    w_in_p, conv8, a_log, dt_bias, wn, wq, w_uq_p, wkv, w_ukv, wqn, wqr, wkn, wkr, won, w_out = wts
    cos2, sin2 = rope
    d = x.shape[1]
    (h,) = _rowcall(lambda r, p: ([_modulate(r[0], p[0], p[1])], []), [x], [s, sh], [(d, BF16)], [],
                    tile=512, name="mix_mod")
    proj = _mm(h, w_in_p, "nn", F32, "mix_in")
    qkv_c = _conv_fwd(proj, conv8, "mix_conv")
    gab = (proj, LANES, 23)

    q, k, v, gb = _rowcall(
        lambda r, p: (list(_gdn_prep_core(_split(r[0], [HEAD] * 12), r[1], p[0], p[1])), []),
        [qkv_c, gab], [a_log, dt_bias], [(512, F32)] * 3 + [(LANES, F32)], [], tile=256, name="mix_gdn_prep")
    gdn_local = _gdn_local_fwd(q, k, v, gb, "mix_gdn_local")
    o_gdn, gdn_states = _gdn_scan_fwd(*gdn_local, "mix_gdn_scan")
    states = (gdn_local, gdn_states)

    cq, ckv, kr = (proj, 512, 4), (proj, 256, 10), (proj, LANES, 22)
    cqn, ckvn, k_rope = _rowcall(
        lambda r, p: (list(_mla_prep_core(r[0][:, :MLA_Q_LORA], r[1], r[2], r[3], r[4], p[0], p[1], p[2])), []),
        [cq, ckv, kr, cos2, sin2], [wq, wkv, wkr], [(MLA_Q_LORA, BF16), (MLA_KV_LORA, BF16), (LANES, F32)], [],
        tile=512, name="mix_mla_prep")
    qf = _mm(cqn, w_uq_p, "nn", F32, "mix_uq")
    kvf = _mm(ckvn, w_ukv, "nn", F32, "mix_ukv")

    def qk_prep(r, p):
        qparts = _split(r[0], [HEAD] * 8)
        kvparts = _split(r[1], [HEAD] * 8)
        qs, ks, vs = _qk_prep_core(qparts[:4], qparts[4:], kvparts[0::2], kvparts[1::2], r[2], r[3], r[4],
                                   p[0], p[1], p[2])
        return [jnp.concatenate(qs, 1), jnp.concatenate(ks, 1), jnp.concatenate(vs, 1)], []

    qa, ka, va = _rowcall(qk_prep, [qf, kvf, k_rope, cos2, sin2], [wqn, wqr, wkn],
                          [(4 * QK_PAD, BF16), (4 * QK_PAD, BF16), (4 * HEAD, BF16)], [], tile=256,
                          name="mix_qk_prep")
    o_b, lse = _attn_fwd(qa, ka, va, "mix_attn")

    gz = (proj, 512, 3)
    (mixed,) = _rowcall(
        lambda r, p: ([_mix_post_core(_split(r[0], HW4), _split(r[1], HW4), _split(r[2], HW4), p[0], p[1])], []),
        [o_gdn, gz, o_b], [wn, won], [(2 * 512, BF16)], [], tile=512, name="mix_post")
    y = _mm(mixed, w_out, "nn", F32, "mix_out")
    (x_out,) = _rowcall(lambda r, p: ([r[0] + p[0] * r[1]], []), [x, y], [g], [(d, F32)], [], tile=512,
                        name="mix_res")
    saved = (x, h, proj, qkv_c, q, k, v, gb, states, o_gdn, cqn, ckvn, k_rope, qf, kvf, qa, ka, va, o_b, lse,
             mixed, y)
    return x_out, saved


def _mixer_bwd(dy, saved, s, sh, g, wts, rope):
    w_in_p, conv8, a_log, dt_bias, wn, wq, w_uq_p, wkv, w_ukv, wqn, wqr, wkn, wkr, won, w_out = wts
    cos2, sin2 = rope
    (x, h, proj, qkv_c, q, k, v, gb, states, o_gdn, cqn, ckvn, k_rope, qf, kvf, qa, ka, va, o_b, lse,
     mixed, y) = saved
    d = x.shape[1]
    dyb, dg = _rowcall(lambda r, p: ([p[0] * r[0]], [jnp.sum(r[0] * r[1], 0, keepdims=True)]),
                       [dy, y], [g], [(d, BF16)], [(1, d)], tile=512, name="mix_bres")
    dmixed = _mm(dyb, w_out, "nt", F32, "mix_bout")
    dw_out = _mm(mixed, dyb, "tn", BF16, "mix_bwout")

    gz = (proj, 512, 3)

    def post_bwd(r, p):
        _, vjp = jax.vjp(_mix_post_core, _split(r[0], HW4), _split(r[1], HW4), _split(r[2], HW4), p[0], p[1])
        do, dz, dob, dwn, dwon = vjp(r[3])
        return [jnp.concatenate(do, 1), jnp.concatenate(dz, 1), jnp.concatenate(dob, 1)], [dwn, dwon]

    do_gdn, dgz, do_b, dwn, dwon = _rowcall(post_bwd, [o_gdn, gz, o_b, dmixed], [wn, won], [(512, F32)] * 3,
                                            [(1, HEAD), (1, HEAD)], tile=256, name="mix_bpost")

    dqa, dka, dva = _attn_bwd(qa, ka, va, o_b, do_b, lse, "mix_battn")

    def qk_bwd(r, p):
        qparts = _split(r[0], [HEAD] * 8)
        kvparts = _split(r[1], [HEAD] * 8)
        _, vjp = jax.vjp(_qk_prep_core, qparts[:4], qparts[4:], kvparts[0::2], kvparts[1::2], r[2], r[3], r[4],
                         p[0], p[1], p[2])
        cot = (_split(r[5], [QK_PAD] * 4), _split(r[6], [QK_PAD] * 4), _split(r[7], HW4))
        dqn, dqr, dkn, dvp, dkrope, _, _, dwqn, dwqr, dwkn = vjp(cot)
        dkv = []
        for a, b in zip(dkn, dvp):
            dkv += [a, b]
        return [jnp.concatenate(list(dqn) + list(dqr), 1), jnp.concatenate(dkv, 1), dkrope], [dwqn, dwqr, dwkn]

    dqf, dkvf, dk_rope, dwqn, dwqr, dwkn = _rowcall(
        qk_bwd, [qf, kvf, k_rope, cos2, sin2, dqa, dka, dva], [wqn, wqr, wkn],
        [(8 * HEAD, BF16), (8 * HEAD, BF16), (LANES, F32)], [(1, HEAD)] * 3, tile=256, name="mix_bqk_prep")
    dcqn = _mm(dqf, w_uq_p, "nt", F32, "mix_buq")
    dw_uq_p = _mm(cqn, dqf, "tn", F32, "mix_bwuq")
    dckvn = _mm(dkvf, w_ukv, "nt", F32, "mix_bukv")
    dw_ukv = _mm(ckvn, dkvf, "tn", F32, "mix_bwukv")

    cq, ckv, kr = (proj, 512, 4), (proj, 256, 10), (proj, LANES, 22)

    def mla_bwd(r, p):
        _, vjp = jax.vjp(_mla_prep_core, r[0][:, :MLA_Q_LORA], r[1], r[2], r[3], r[4], p[0], p[1], p[2])
        dcq, dckv, dkr, _, _, dwq, dwkv, dwkr = vjp((r[5], r[6], r[7]))
        pad = jnp.zeros((dcq.shape[0], 512 - MLA_Q_LORA), F32)
        return [jnp.concatenate([dcq, pad], 1), dckv, dkr], [dwq, dwkv, dwkr]

    dcq, dckv, dkr, dwq, dwkv, dwkr = _rowcall(
        mla_bwd, [cq, ckv, kr, cos2, sin2, dcqn, dckvn, dk_rope], [wq, wkv, wkr],
        [(512, F32), (MLA_KV_LORA, F32), (LANES, F32)], [(1, MLA_Q_LORA), (1, MLA_KV_LORA), (1, LANES)],
        tile=512, name="mix_bmla_prep")

    gdn_local, gdn_states = states
    d_local = _gdn_scan_bwd(*gdn_local, gdn_states, do_gdn, "mix_bgdn_scan")
    dq, dk, dv, dgb = _gdn_local_bwd(q, k, v, gb, *d_local, "mix_bgdn_local")
    gab = (proj, LANES, 23)

    def gdn_prep_bwd(r, p):
        _, vjp = jax.vjp(_gdn_prep_core, _split(r[0], [HEAD] * 12), r[1], p[0], p[1])
        dparts, dgab, da_log, ddt = vjp((r[2], r[3], r[4], r[5]))
        return [jnp.concatenate(dparts, 1), dgab], [da_log, ddt]

    dqkv_c, dgab, da_log, ddt = _rowcall(gdn_prep_bwd, [qkv_c, gab, dq, dk, dv, dgb], [a_log, dt_bias],
                                         [(1536, F32), (LANES, F32)], [(1, LANES), (1, LANES)], tile=256,
                                         name="mix_bgdn_prep")
    dqkv_pre, dconv8 = _conv_bwd(proj, dqkv_c, conv8, "mix_bconv")

    dproj = jnp.concatenate([dqkv_pre.astype(BF16), dgz.astype(BF16), dcq.astype(BF16), dckv.astype(BF16),
                             dkr.astype(BF16), dgab.astype(BF16)], axis=1)
    dh = _mm(dproj, w_in_p, "nt", F32, "mix_bin")
    dw_in_p = _mm(h, dproj, "tn", F32, "mix_bwin")

    def mod_bwd(r, p):
        _, vjp = jax.vjp(_modulate, r[0], p[0], p[1])
        dx, ds, dsh = vjp(r[1])
        return [r[2] + dx], [ds, dsh]

    dx, ds, dsh = _rowcall(mod_bwd, [x, dh, dy], [s, sh], [(d, F32)], [(1, d), (1, d)], tile=512, name="mix_bmod")
    small = dict(conv=dconv8, a_log=da_log, dt=ddt, wn=dwn, wq=dwq, wkv=dwkv, wqn=dwqn, wqr=dwqr, wkn=dwkn,
                 wkr=dwkr, won=dwon)
    return dx, (dsh, ds, dg), dw_in_p, dw_uq_p, dw_ukv, dw_out, small


def _pad_cols(a, n):
    return jnp.pad(a, ((0, 0),) * (a.ndim - 1) + ((0, n - a.shape[-1]),))


def _pack_w_in(w):
    z = lambda n: jnp.zeros((w.shape[0], n), w.dtype)
    return jnp.concatenate([w[:, 0:2048], w[:, 2056:2440], z(128), w[:, 2440:2696], w[:, 2696:2760], z(64),
                            w[:, 2048:2056], z(120)], axis=1)


def _unpack_w_in(wp):
    return jnp.concatenate([wp[:, 0:2048], wp[:, 2944:2952], wp[:, 2048:2432], wp[:, 2560:2816], wp[:, 2816:2880]],
                           axis=1)


def _pack_w_uq(w):
    z = jnp.zeros((w.shape[0], LANES - MLA_ROPE), w.dtype)
    nope = [w[:, h * 192:h * 192 + HEAD] for h in range(MLA_HEADS)]
    rope = []
    for h in range(MLA_HEADS):
        rope += [w[:, h * 192 + HEAD:(h + 1) * 192], z]
    return jnp.concatenate(nope + rope, axis=1)


def _unpack_w_uq(wp):
    cols = []
    for h in range(MLA_HEADS):
        cols += [wp[:, h * HEAD:(h + 1) * HEAD], wp[:, 512 + h * LANES:512 + h * LANES + MLA_ROPE]]
    return jnp.concatenate(cols, axis=1)


def _cols_to_chips(a):
    r, c = a.shape
    return a.reshape(r, 4, c // 4).transpose(1, 0, 2)


def _chips_to_cols(a):
    _, r, n = a.shape
    return a.transpose(1, 0, 2).reshape(r, 4 * n)


def _pad128(v, n=LANES):
    return _pad_cols(v.reshape(1, -1), n)


def kernel(x, c, positions, w_ada, b_ada, ffn1_w_in, ffn1_w_out, w_in, gdn_conv_w, gdn_a_log, gdn_dt_bias, gdn_norm_w, mla_q_norm_w, mla_w_uq, mla_kv_norm_w, mla_w_ukv, qkn_q_nope, qkn_q_rope, qkn_k_nope, qkn_k_rope, mla_out_norm_w, w_out, ffn2_w_in, ffn2_w_out, loss_target, m_w_ada, m_b_ada, m_ffn1_w_in, m_ffn1_w_out, m_w_in, m_gdn_conv_w, m_gdn_a_log, m_gdn_dt_bias, m_gdn_norm_w, m_mla_q_norm_w, m_mla_w_uq, m_mla_kv_norm_w, m_mla_w_ukv, m_qkn_q_nope, m_qkn_q_rope, m_qkn_k_nope, m_qkn_k_rope, m_mla_out_norm_w, m_w_out, m_ffn2_w_in, m_ffn2_w_out, v_w_ada, v_b_ada, v_ffn1_w_in, v_ffn1_w_out, v_w_in, v_gdn_conv_w, v_gdn_a_log, v_gdn_dt_bias, v_gdn_norm_w, v_mla_q_norm_w, v_mla_w_uq, v_mla_kv_norm_w, v_mla_w_ukv, v_qkn_q_nope, v_qkn_q_rope, v_qkn_k_nope, v_qkn_k_rope, v_mla_out_norm_w, v_w_out, v_ffn2_w_in, v_ffn2_w_out):
    weights = dict(w_ada=w_ada, b_ada=b_ada, ffn1_w_in=ffn1_w_in, ffn1_w_out=ffn1_w_out, w_in=w_in,
                   gdn_conv_w=gdn_conv_w, gdn_a_log=gdn_a_log, gdn_dt_bias=gdn_dt_bias, gdn_norm_w=gdn_norm_w,
                   mla_q_norm_w=mla_q_norm_w, mla_w_uq=mla_w_uq, mla_kv_norm_w=mla_kv_norm_w, mla_w_ukv=mla_w_ukv,
                   qkn_q_nope=qkn_q_nope, qkn_q_rope=qkn_q_rope, qkn_k_nope=qkn_k_nope, qkn_k_rope=qkn_k_rope,
                   mla_out_norm_w=mla_out_norm_w, w_out=w_out, ffn2_w_in=ffn2_w_in, ffn2_w_out=ffn2_w_out)
    moms_m = dict(w_ada=m_w_ada, b_ada=m_b_ada, ffn1_w_in=m_ffn1_w_in, ffn1_w_out=m_ffn1_w_out, w_in=m_w_in,
                  gdn_conv_w=m_gdn_conv_w, gdn_a_log=m_gdn_a_log, gdn_dt_bias=m_gdn_dt_bias,
                  gdn_norm_w=m_gdn_norm_w, mla_q_norm_w=m_mla_q_norm_w, mla_w_uq=m_mla_w_uq,
                  mla_kv_norm_w=m_mla_kv_norm_w, mla_w_ukv=m_mla_w_ukv, qkn_q_nope=m_qkn_q_nope,
                  qkn_q_rope=m_qkn_q_rope, qkn_k_nope=m_qkn_k_nope, qkn_k_rope=m_qkn_k_rope,
                  mla_out_norm_w=m_mla_out_norm_w, w_out=m_w_out, ffn2_w_in=m_ffn2_w_in, ffn2_w_out=m_ffn2_w_out)
    moms_v = dict(w_ada=v_w_ada, b_ada=v_b_ada, ffn1_w_in=v_ffn1_w_in, ffn1_w_out=v_ffn1_w_out, w_in=v_w_in,
                  gdn_conv_w=v_gdn_conv_w, gdn_a_log=v_gdn_a_log, gdn_dt_bias=v_gdn_dt_bias,
                  gdn_norm_w=v_gdn_norm_w, mla_q_norm_w=v_mla_q_norm_w, mla_w_uq=v_mla_w_uq,
                  mla_kv_norm_w=v_mla_kv_norm_w, mla_w_ukv=v_mla_w_ukv, qkn_q_nope=v_qkn_q_nope,
                  qkn_q_rope=v_qkn_q_rope, qkn_k_nope=v_qkn_k_nope, qkn_k_rope=v_qkn_k_rope,
                  mla_out_norm_w=v_mla_out_norm_w, w_out=v_w_out, ffn2_w_in=v_ffn2_w_in, ffn2_w_out=v_ffn2_w_out)
    names = list(weights)

    seq, d = x.shape[1], x.shape[2]
    x2d = x.reshape(seq, d)
    tgt = loss_target.reshape(seq, d)
    mx, my, mc = _place()
    chip = 2 * mx + my
    me = 2 * chip + mc
    n_mod = b_ada.shape[1] // d
    shard = w_ada.shape[2]

    half = MLA_ROPE // 2
    inv_freq = 10000.0 ** (-jnp.arange(half, dtype=F32) / half)
    ang = positions.astype(F32).reshape(seq, 1) * inv_freq
    cosv, sinv = jnp.cos(ang), jnp.sin(ang)
    cos2 = _pad_cols(jnp.concatenate([cosv, cosv], 1), LANES)
    sin2 = _pad_cols(jnp.concatenate([-sinv, sinv], 1), LANES)
    rope = (cos2, sin2)

    c_all = _allgather8(jnp.pad(c, ((0, SUBLANES - 1), (0, 0))), "gather_c")[:, 0, :]
    (sc_all,) = _rowcall(lambda r, p: ([_silu(r[0])], []), [c_all], [], [(d, F32)], [], tile=8, name="ada_silu")
    mod_part = _mm(sc_all, w_ada[0], "nn", F32, "ada_mm", hi=True)
    mod_all = _allgather8(mod_part, "gather_mod")
    mod_rows = lax.dynamic_index_in_dim(mod_all, me, axis=1, keepdims=False)
    mod_raw = jnp.concatenate([mod_rows[2 * jj] for jj in range(4)], axis=0).reshape(1, 4 * shard)
    (mod,) = _rowcall(lambda r, p: ([r[0] + r[1]], []),
                      [jnp.pad(mod_raw, ((0, 7), (0, 0))), jnp.pad(b_ada, ((0, 7), (0, 0)))], [],
                      [(4 * shard, F32)], [], tile=8, name="ada_bias")
    mods = [mod[0:1, i * d:(i + 1) * d] for i in range(n_mod)]
    sh1, s1, g1, sh2, s2, g2, sh3, s3, g3 = mods

    def gather_cols(w, name, pad_to=None):
        w2 = w[0].astype(BF16)
        n = w2.shape[1]
        if pad_to:
            w2 = _pad_cols(w2, pad_to)
        return _chips_to_cols(_allgather_chips(w2, name)[:, :, :n])

    def gather_rows(w, name):
        w2 = w[0].astype(BF16)
        return _allgather_chips(w2, name).reshape(4 * w2.shape[0], w2.shape[1])

    f1_in = gather_cols(ffn1_w_in, "gather_f1_in")
    f1_out = gather_rows(ffn1_w_out, "gather_f1_out")
    w_in_full = gather_cols(w_in, "gather_w_in", 768)
    w_uq_full = gather_cols(mla_w_uq, "gather_w_uq", 256)
    w_ukv_full = gather_cols(mla_w_ukv, "gather_w_ukv")
    w_out_full = gather_rows(w_out, "gather_w_out")
    f2_in = gather_cols(ffn2_w_in, "gather_f2_in")
    f2_out = gather_rows(ffn2_w_out, "gather_f2_out")
    conv_all = _allgather8(jnp.pad(gdn_conv_w[0], ((0, SUBLANES - CONV_K), (0, 0))), "gather_conv")
    conv8 = jnp.concatenate([conv_all[2 * jj] for jj in range(4)], axis=1)

    wts = (_pack_w_in(w_in_full), conv8, _pad128(gdn_a_log), _pad128(gdn_dt_bias), gdn_norm_w,
           mla_q_norm_w, _pack_w_uq(w_uq_full), mla_kv_norm_w, w_ukv_full, qkn_q_nope, _pad128(qkn_q_rope),
           qkn_k_nope, _pad128(qkn_k_rope), mla_out_norm_w, w_out_full)

    x1, sv1 = _ffn_fwd(x2d, s1, sh1, g1, f1_in, f1_out, "ffn1")
    xm, svm = _mixer_fwd(x1, s2, sh2, g2, wts, rope)
    x3, sv3 = _ffn_fwd(xm, s3, sh3, g3, f2_in, f2_out, "ffn2")

    def loss_fn(r, p):
        err = r[0] - r[1]
        part = 0.5 * jnp.sum(jnp.sum(err * err, axis=1, keepdims=True) * (1.0 / d), axis=0, keepdims=True)
        return [err * (1.0 / d)], [jnp.broadcast_to(part, (1, LANES))]

    dy, loss_part = _rowcall(loss_fn, [x3, tgt], [], [(d, F32)], [(1, LANES)], tile=512, name="loss")
    loss = lax.psum(loss_part[0, 0], ("x", "y", "c"))

    dxm, dmod3, dw_f2_in, dw_f2_out = _ffn_bwd(dy, sv3, s3, sh3, g3, f2_in, f2_out, "ffn2")
    dx1, dmod2, dw_in_p, dw_uq_p, dw_ukv, dw_out_m, small = _mixer_bwd(dxm, svm, s2, sh2, g2, wts, rope)
    dx0, dmod1, dw_f1_in, dw_f1_out = _ffn_bwd(dx1, sv1, s1, sh1, g1, f1_in, f1_out, "ffn1")
    grad_x = dx0.reshape(x.shape)

    dmod = jnp.concatenate(list(dmod1) + list(dmod2) + list(dmod3), axis=1)
    small_parts = [dmod, small["conv"][:CONV_K], small["a_log"], small["dt"], small["wn"], small["wq"],
                   small["wkv"], small["wqn"], small["wqr"], small["wkn"], small["wkr"], small["won"]]
    packed, offs = _pack_rows(small_parts)
    gathered = _allgather8(packed, "gather_small")
    total = _sum8(gathered, "sum_small")
    (g_b_ada, g_conv, g_a_log, g_dt, g_wn, g_wq, g_wkv, g_wqn, g_wqr, g_wkn, g_wkr, g_won) = _unpack_rows(
        total, offs, [p.shape for p in small_parts])
    dmod_all = _unpack_rows(gathered.reshape(-1, LANES),
                            [(dd * packed.shape[0] + offs[0][0], offs[0][1]) for dd in range(8)],
                            [dmod.shape] * 8)
    dmod_all = jnp.concatenate(dmod_all, axis=0)
    dmod_mine = lax.dynamic_slice_in_dim(dmod_all, chip * shard, shard, axis=1)

    def ada_grad(r, p):
        acc = jnp.zeros((r[0].shape[0], shard), F32)
        for b in range(8):
            acc = acc + r[0][:, b:b + 1] * p[0][b:b + 1, :]
        return [acc], []

    (g_w_ada,) = _rowcall(ada_grad, [_pad_cols(sc_all.T, LANES)], [dmod_mine], [(shard, F32)], [], tile=256,
                          name="ada_grad")

    grads = dict(
        w_ada=g_w_ada[None], b_ada=g_b_ada,
        gdn_conv_w=lax.dynamic_slice_in_dim(g_conv, chip * gdn_conv_w.shape[2], gdn_conv_w.shape[2], axis=1)[None],
        gdn_a_log=g_a_log[:, :GDN_HEADS], gdn_dt_bias=g_dt[:, :GDN_HEADS], gdn_norm_w=g_wn, mla_q_norm_w=g_wq,
        mla_kv_norm_w=g_wkv, qkn_q_nope=g_wqn, qkn_q_rope=g_wqr[:, :MLA_ROPE], qkn_k_nope=g_wkn,
        qkn_k_rope=g_wkr[:, :MLA_ROPE], mla_out_norm_w=g_won)

    def rs_cols(dw, name, pad_to=None):
        g4 = _cols_to_chips(dw).astype(BF16)
        n = g4.shape[2]
        if pad_to:
            g4 = _pad_cols(g4, pad_to)
        return _reduce_scatter_chips(g4, name)[:, :n][None]

    def rs_rows(dw, name):
        r, cc = dw.shape
        return _reduce_scatter_chips(dw.astype(BF16).reshape(4, r // 4, cc), name)[None]

    grads["ffn2_w_in"] = rs_cols(dw_f2_in, "rs_f2_in")
    grads["ffn2_w_out"] = rs_rows(dw_f2_out, "rs_f2_out")
    grads["w_in"] = rs_cols(_unpack_w_in(dw_in_p), "rs_w_in", 768)
    grads["mla_w_uq"] = rs_cols(_unpack_w_uq(dw_uq_p), "rs_w_uq", 256)
    grads["mla_w_ukv"] = rs_cols(dw_ukv, "rs_w_ukv")
    grads["w_out"] = rs_rows(dw_out_m, "rs_w_out")
    grads["ffn1_w_in"] = rs_cols(dw_f1_in, "rs_f1_in")
    grads["ffn1_w_out"] = rs_rows(dw_f1_out, "rs_f1_out")

    big = ["w_ada", "ffn1_w_in", "ffn1_w_out", "w_in", "mla_w_uq", "mla_w_ukv", "w_out", "ffn2_w_in", "ffn2_w_out"]
    delta, new_m, new_v = {}, {}, {}
    for nme in big:
        shp = weights[nme].shape
        dl, nm, nv = _adamw(weights[nme][0], grads[nme][0], moms_m[nme][0], moms_v[nme][0], "adamw_" + nme)
        delta[nme], new_m[nme], new_v[nme] = dl.reshape(shp), nm.reshape(shp), nv.reshape(shp)
    tiny = [nme for nme in names if nme not in big]
    shapes = [weights[nme].shape for nme in tiny]
    pw, poffs = _pack_rows([weights[nme] for nme in tiny])
    pg, _ = _pack_rows([grads[nme] for nme in tiny])
    pm, _ = _pack_rows([moms_m[nme] for nme in tiny])
    pv, _ = _pack_rows([moms_v[nme] for nme in tiny])
    pd, pnm, pnv = _adamw(pw, pg, pm, pv, "adamw_small")
    for nme, dl, nm, nv in zip(tiny, _unpack_rows(pd, poffs, shapes), _unpack_rows(pnm, poffs, shapes),
                               _unpack_rows(pnv, poffs, shapes)):
        delta[nme], new_m[nme], new_v[nme] = dl, nm, nv

    return (loss, grad_x, *[grads[nme].reshape(weights[nme].shape) for nme in names],
            *[delta[nme] for nme in names], *[new_m[nme] for nme in names], *[new_v[nme] for nme in names])
```

```python
import functools

import jax
import jax.numpy as jnp
from jax import lax
from jax.experimental import pallas as pl
from jax.experimental.pallas import tpu as pltpu

F32 = jnp.float32
BF16 = jnp.bfloat16
HI = lax.Precision.HIGHEST
MESH = pl.DeviceIdType.MESH

EPS = 1e-6
CHUNK = 64
D_FF = 2816
GDN_HEADS = 4
HEAD = 128
MLA_HEADS = 4
MLA_ROPE = 64
MLA_Q_LORA = 384
MLA_KV_LORA = 256
QK_PAD = 256
ATT_SCALE = (HEAD + MLA_ROPE) ** -0.5
N_PROJ = 3072

ADAM_LR, ADAM_B1, ADAM_B2, ADAM_EPS, ADAM_WD, ADAM_STEP = 0.001, 0.9, 0.999, 1e-08, 0.01, 10

LANES = 128
SUBLANES = 8
VMEM_LIMIT = 56 * 2 ** 20


def _params(sem=None):
    return pltpu.CompilerParams(dimension_semantics=sem, vmem_limit_bytes=VMEM_LIMIT)


def _pick(n, cap, align):
    best = None
    d = align
    while d <= min(n, cap):
        if n % d == 0:
            best = d
        d += align
    return best if best is not None else n


def _iota(shape, dim):
    return lax.broadcasted_iota(jnp.int32, shape, dim)


def _rowcall(fn, rows, params, out_rows, out_accs, *, tile, name):
    rows = [r if isinstance(r, tuple) else (r, r.shape[1], 0) for r in rows]
    s = rows[0][0].shape[-2]
    t = min(tile, s)
    n = s // t
    n_in = len(rows) + len(params)
    n_row_out = len(out_rows)

    in_specs = []
    for r in rows:
        if len(r) == 3:
            in_specs.append(pl.BlockSpec((t, r[1]), functools.partial(lambda i, b: (i, b), b=r[2])))
        else:
            in_specs.append(pl.BlockSpec((None, t, r[1]), functools.partial(lambda i, b, h: (h, i, b), b=r[2], h=r[3])))
    in_specs += [pl.BlockSpec(p.shape, lambda i: (0, 0)) for p in params]
    out_shape, out_specs = [], []
    for o in out_rows:
        if len(o) == 2:
            out_shape.append(jax.ShapeDtypeStruct((s, o[0]), o[1]))
            out_specs.append(pl.BlockSpec((t, o[0]), lambda i: (i, 0)))
        else:
            out_shape.append(jax.ShapeDtypeStruct((o[2], s, o[0]), o[1]))
            out_specs.append(pl.BlockSpec((o[2], t, o[0]), lambda i: (0, i, 0)))
    out_shape += [jax.ShapeDtypeStruct(shape, F32) for shape in out_accs]
    out_specs += [pl.BlockSpec(shape, lambda i: (0, 0)) for shape in out_accs]

    def body(*refs):
        ins = refs[:n_in]
        outs = refs[n_in:]
        i = pl.program_id(0)
        vals = [r[...] for r in ins]
        row_outs, acc_outs = fn(vals[:len(rows)], vals[len(rows):])
        for r, v in zip(outs[:n_row_out], row_outs):
            if isinstance(v, (list, tuple)):
                for hh, piece in enumerate(v):
                    r[hh] = piece.astype(r.dtype)
            else:
                r[...] = v.astype(r.dtype)
        if out_accs:
            @pl.when(i == 0)
            def _():
                for r in outs[n_row_out:]:
                    r[...] = jnp.zeros(r.shape, F32)
            for r, v in zip(outs[n_row_out:], acc_outs):
                r[...] += v

    res = pl.pallas_call(
        body, name=name, grid=(n,), in_specs=in_specs, out_specs=out_specs, out_shape=out_shape,
        compiler_params=_params(("arbitrary",) if out_accs else ("parallel",)),
    )(*[r[0] for r in rows], *params)
    return list(res)


MM_TILE_MN = 1536


def _mm(a, b, mode, out_dtype, name, hi=False):
    if mode == "nn":
        (m, k), (_, n) = a.shape, b.shape
        dims = (((1,), (0,)), ((), ()))
    elif mode == "nt":
        (m, k), (n, _) = a.shape, b.shape
        dims = (((1,), (1,)), ((), ()))
    else:
        (k, m), (_, n) = a.shape, b.shape
        dims = (((0,), (0,)), ((), ()))
    tm = _pick(m, MM_TILE_MN if mode == "tn" else 1024, LANES if mode == "tn" else 16)
    tn = _pick(n, MM_TILE_MN, LANES)
    tk = _pick(k, 1024 if mode == "tn" else MM_TILE_MN, LANES)
    nk = k // tk
    if mode == "nn":
        a_spec = pl.BlockSpec((tm, tk), lambda i, j, kk: (i, kk))
        b_spec = pl.BlockSpec((tk, tn), lambda i, j, kk: (kk, j))
    elif mode == "nt":
        a_spec = pl.BlockSpec((tm, tk), lambda i, j, kk: (i, kk))
        b_spec = pl.BlockSpec((tn, tk), lambda i, j, kk: (j, kk))
    else:
        a_spec = pl.BlockSpec((tk, tm), lambda i, j, kk: (kk, i))
        b_spec = pl.BlockSpec((tk, tn), lambda i, j, kk: (kk, j))

    def body(a_ref, b_ref, o_ref, acc_ref):
        kk = pl.program_id(2)

        @pl.when(kk == 0)
        def _():
            acc_ref[...] = jnp.zeros(acc_ref.shape, F32)

        av, bv = a_ref[...], b_ref[...]
        if hi:
            acc_ref[...] += lax.dot_general(av, bv, dims, precision=HI, preferred_element_type=F32)
        else:
            acc_ref[...] += lax.dot_general(av.astype(BF16), bv.astype(BF16), dims,
                                            preferred_element_type=F32)

        @pl.when(kk == nk - 1)
        def _():
            o_ref[...] = acc_ref[...].astype(o_ref.dtype)

    return pl.pallas_call(
        body, name=name, grid=(m // tm, n // tn, nk),
        in_specs=[a_spec, b_spec],
        out_specs=pl.BlockSpec((tm, tn), lambda i, j, kk: (i, j)),
        out_shape=jax.ShapeDtypeStruct((m, n), out_dtype),
        scratch_shapes=[pltpu.VMEM((tm, tn), F32)],
        compiler_params=_params(("parallel", "parallel", "arbitrary")),
    )(a, b)


def _rms(x, w=None, n=None):
    n = x.shape[-1] if n is None else n
    y = x * lax.rsqrt(jnp.sum(x * x, axis=-1, keepdims=True) * (1.0 / n) + EPS)
    return y if w is None else y * w


def _silu(x):
    return x * jax.nn.sigmoid(x)


def _softplus(x):
    return jnp.maximum(x, 0.0) + jnp.log1p(jnp.exp(-jnp.abs(x)))


def _split(x, widths):
    out, o = [], 0
    for w in widths:
        out.append(x[:, o:o + w])
        o += w
    return out


def _modulate(x, s, sh):
    return _rms(x) * (1.0 + s) + sh


def _rope_rot(x):
    r, c = _iota((LANES, LANES), 0), _iota((LANES, LANES), 1)
    half = MLA_ROPE // 2
    perm = (((r < half) & (c == r + half)) | ((r >= half) & (r < MLA_ROPE) & (c == r - half))).astype(F32)
    return jnp.dot(x, perm, precision=HI, preferred_element_type=F32)


def _rope(x, cos2, sin2):
    return x * cos2 + _rope_rot(x) * sin2


def _gdn_prep_core(qkv_parts, gab, a_log, dt_bias):
    act = [_silu(p) for p in qkv_parts]
    qs = [p * lax.rsqrt(jnp.sum(p * p, -1, keepdims=True) + EPS) * (HEAD ** -0.5) for p in act[:4]]
    ks = [p * lax.rsqrt(jnp.sum(p * p, -1, keepdims=True) + EPS) for p in act[4:8]]
    lane = _iota(gab.shape, 1)
    g = -jnp.exp(a_log) * _softplus(gab + dt_bias)
    beta = jax.nn.sigmoid(gab)
    gb = jnp.where(lane < GDN_HEADS, g, jnp.where(lane < 2 * GDN_HEADS, beta, 0.0))
    return (jnp.concatenate(qs, 1), jnp.concatenate(ks, 1), jnp.concatenate(act[8:], 1), gb)


def _mla_prep_core(cq, ckv, kr, cos2, sin2, wq, wkv, wkr):
    cqn = _rms(cq, wq)
    ckvn = _rms(ckv, wkv)
    k_rope = _rope(_rms(kr, wkr, MLA_ROPE), cos2, sin2)
    return cqn, ckvn, k_rope


def _qk_prep_core(qn_parts, qr_parts, kn_parts, v_parts, k_rope, cos2, sin2, wqn, wqr, wkn):
    qs, ks = [], []
    for h in range(MLA_HEADS):
        qn = _rms(qn_parts[h], wqn) * ATT_SCALE
        qr = _rope(_rms(qr_parts[h], wqr, MLA_ROPE), cos2, sin2) * ATT_SCALE
        qs.append(jnp.concatenate([qn, qr], 1))
        ks.append(jnp.concatenate([_rms(kn_parts[h], wkn), k_rope], 1))
    return qs, ks, list(v_parts)


def _mix_post_core(o_parts, gz_parts, ob_parts, wn, won):
    oa = [_rms(o, wn) * _silu(z) for o, z in zip(o_parts, gz_parts)]
    ob = [_rms(o, won) for o in ob_parts]
    return jnp.concatenate(oa + ob, 1)


CONV_K = 4
HALO = SUBLANES


def _conv_fwd(proj, w8, name):
    s = proj.shape[0]
    c = w8.shape[1]
    t = min(256, s)
    n = s // t
    hb = t // HALO

    def body(x_ref, prev_ref, w_ref, o_ref, buf):
        i = pl.program_id(0)
        buf[pl.ds(0, HALO), :] = jnp.where(i > 0, prev_ref[...], 0.0)
        buf[pl.ds(HALO, t), :] = x_ref[...]
        acc = jnp.zeros((t, c), F32)
        for k in range(CONV_K):
            acc = acc + w_ref[k:k + 1, :] * buf[pl.ds(HALO - (CONV_K - 1) + k, t), :]
        o_ref[...] = acc

    return pl.pallas_call(
        body, name=name, grid=(n,),
        in_specs=[pl.BlockSpec((t, c), lambda i: (i, 0)),
                  pl.BlockSpec((HALO, c), lambda i: (jnp.maximum(i * hb - 1, 0), 0)),
                  pl.BlockSpec(w8.shape, lambda i: (0, 0))],
        out_specs=pl.BlockSpec((t, c), lambda i: (i, 0)),
        out_shape=jax.ShapeDtypeStruct((s, c), F32),
        scratch_shapes=[pltpu.VMEM((t + HALO, c), F32)],
        compiler_params=_params(("parallel",)),
    )(proj, proj, w8)


def _conv_bwd(proj, dy, w8, name):
    s = proj.shape[0]
    c = w8.shape[1]
    t = min(256, s)
    n = s // t
    hb = t // HALO

    def body(x_ref, prev_ref, dy_ref, next_ref, w_ref, dx_ref, dw_ref, bufx, bufd):
        i = pl.program_id(0)
        bufx[pl.ds(0, HALO), :] = jnp.where(i > 0, prev_ref[...], 0.0)
        bufx[pl.ds(HALO, t), :] = x_ref[...]
        bufd[pl.ds(0, t), :] = dy_ref[...]
        bufd[pl.ds(t, HALO), :] = jnp.where(i < n - 1, next_ref[...], 0.0)

        @pl.when(i == 0)
        def _():
            dw_ref[...] = jnp.zeros(dw_ref.shape, F32)

        dyv = dy_ref[...]
        acc = jnp.zeros((t, c), F32)
        for k in range(CONV_K):
            acc = acc + w_ref[k:k + 1, :] * bufd[pl.ds(CONV_K - 1 - k, t), :]
            dw_ref[k:k + 1, :] += jnp.sum(dyv * bufx[pl.ds(HALO - (CONV_K - 1) + k, t), :], axis=0, keepdims=True)
        dx_ref[...] = acc

    return pl.pallas_call(
        body, name=name, grid=(n,),
        in_specs=[pl.BlockSpec((t, c), lambda i: (i, 0)),
                  pl.BlockSpec((HALO, c), lambda i: (jnp.maximum(i * hb - 1, 0), 0)),
                  pl.BlockSpec((t, c), lambda i: (i, 0)),
                  pl.BlockSpec((HALO, c), lambda i: (jnp.minimum((i + 1) * hb, s // HALO - 1), 0)),
                  pl.BlockSpec(w8.shape, lambda i: (0, 0))],
        out_specs=[pl.BlockSpec((t, c), lambda i: (i, 0)), pl.BlockSpec(w8.shape, lambda i: (0, 0))],
        out_shape=[jax.ShapeDtypeStruct((s, c), F32), jax.ShapeDtypeStruct(w8.shape, F32)],
        scratch_shapes=[pltpu.VMEM((t + HALO, c), F32), pltpu.VMEM((t + HALO, c), F32)],
        compiler_params=_params(("arbitrary",)),
    )(proj, proj, dy, dy, w8)


def _dot(a, b):
    return jnp.dot(a, b, precision=HI, preferred_element_type=F32)


def _dot_nt(a, b):
    return lax.dot_general(a, b, (((1,), (1,)), ((), ())), precision=HI, preferred_element_type=F32)


def _dot_tn(a, b):
    return lax.dot_general(a, b, (((0,), (0,)), ((), ())), precision=HI, preferred_element_type=F32)


_B_NN = (((2,), (1,)), ((0,), (0,)))
_B_NT = (((2,), (2,)), ((0,), (0,)))
_B_TN = (((1,), (1,)), ((0,), (0,)))


def _bdot_hi(a, b):
    return lax.dot_general(a, b, _B_NN, precision=HI, preferred_element_type=F32)


def _dot3(a, b, dims):
    return lax.dot_general(a, b, dims, precision=lax.Precision.HIGH, preferred_element_type=F32)


class _Dots:
    def __init__(self, diff):
        nn = lambda a, b: _dot3(a, b, _B_NN)
        nt = lambda a, b: _dot3(a, b, _B_NT)
        tn = lambda a, b: _dot3(a, b, _B_TN)
        if diff:
            def with_rule(f, bwd):
                g = jax.custom_vjp(f)
                g.defvjp(lambda a, b: (f(a, b), (a, b)), bwd)
                return g
            self.nn = with_rule(nn, lambda r, ct: (nt(ct, r[1]), tn(r[0], ct)))
            self.nt = with_rule(nt, lambda r, ct: (nn(ct, r[1]), tn(ct, r[0])))
            self.tn = with_rule(tn, lambda r, ct: (nt(r[1], ct), nn(r[0], ct)))
        else:
            self.nn, self.nt, self.tn = nn, nt, tn


def _unit_lower_inverse(a, dots):
    c = a.shape[-1]
    ri, ci = _iota(a.shape, 1), _iota(a.shape, 2)
    inner = (ri // 2) == (ci // 2)
    t = (ri == ci).astype(F32) - jnp.where(inner, a, 0.0)
    blk = 4
    while blk <= c:
        outer = (ri // blk) == (ci // blk)
        low = jnp.where(outer & jnp.logical_not(inner), a, 0.0)
        t = t - dots.nn(dots.nn(t, low), t)
        inner = outer
        blk *= 2
    return t


def _stack(xs):
    return jnp.concatenate([x[None] for x in xs], axis=0)


def _gdn_local(dots, q, k, v, gbs):
    b, c, _ = q.shape
    gcols, bcols = [], []
    for gb in gbs:
        lane = _iota(gb.shape, 1)
        for h in range(GDN_HEADS):
            gcols.append(jnp.sum(jnp.where(lane == h, gb, 0.0), axis=1, keepdims=True))
            bcols.append(jnp.sum(jnp.where(lane == GDN_HEADS + h, gb, 0.0), axis=1, keepdims=True))
    gcol, bcol = _stack(gcols), _stack(bcols)
    ri, ci = _iota((b, c, c), 1), _iota((b, c, c), 2)
    incl = ri >= ci
    tril = incl.astype(F32)
    g_cc = _bdot_hi(tril, jnp.broadcast_to(gcol, (b, c, c)))
    g_row = _bdot_hi(jnp.ones((b, c, c), F32), jnp.where(ri == ci, g_cc, 0.0))
    g_cl = _bdot_hi(tril, jnp.broadcast_to(gcol, (b, c, HEAD)))
    g_last = jnp.sum(jnp.broadcast_to(gcol, (b, c, HEAD)), axis=1, keepdims=True)
    decay = jnp.where(incl, jnp.exp(jnp.where(incl, g_cc - g_row, 0.0)), 0.0)
    kk = dots.nt(k, k)
    minv = _unit_lower_inverse(jnp.where(ri > ci, bcol * kk * decay, 0.0), dots)
    e_g = jnp.exp(g_cl)
    u = dots.nn(minv, v * bcol)
    wk = dots.nn(minv, k * (bcol * e_g))
    qk = dots.nt(q, k) * decay
    return u, wk, q * e_g, k * jnp.exp(g_last - g_cl), qk, jnp.exp(g_last)


def _gdn_scan(dots, states, u, wk, qd, kd, qk, gl_tile):
    lane, row = _iota(gl_tile.shape, 1), _iota(gl_tile.shape, 0)
    gl = _stack([
        jnp.sum(jnp.sum(jnp.where((lane == h) & (row == 0), gl_tile, 0.0), axis=1, keepdims=True),
                axis=0, keepdims=True) for h in range(GDN_HEADS)])
    v_new = u - dots.nn(wk, states)
    o = dots.nn(qd, states) + dots.nn(qk, v_new)
    return states * gl + dots.tn(kd, v_new), o


def _heads(x):
    return jnp.stack(_split(x, HW4))


GDN_W = GDN_HEADS * HEAD
HW4 = [HEAD] * GDN_HEADS
LOCAL_CHUNKS = 4
_CHUNK_ROWS = [pl.ds(cc * CHUNK, CHUNK) for cc in range(LOCAL_CHUNKS)]


def _chunk_heads(ref):
    return jnp.concatenate([_heads(ref[rows, :]) for rows in _CHUNK_ROWS], 0)


def _gdn_local_fwd(q, k, v, gb, name):
    s = q.shape[0]
    t = LOCAL_CHUNKS * CHUNK

    def body(q_ref, k_ref, v_ref, gb_ref, u_ref, wk_ref, qd_ref, kd_ref, qk_ref, gl_ref):
        u, wk, qd, kd, qk, gl = _gdn_local(_Dots(False), _chunk_heads(q_ref), _chunk_heads(k_ref),
                                           _chunk_heads(v_ref), [gb_ref[rows, :] for rows in _CHUNK_ROWS])
        lane = _iota((CHUNK, LANES), 1)
        for cc, rows in enumerate(_CHUNK_ROWS):
            gl_tile = jnp.zeros((CHUNK, LANES), F32)
            for h in range(GDN_HEADS):
                b, cols = cc * GDN_HEADS + h, pl.ds(h * HEAD, HEAD)
                u_ref[rows, cols] = u[b]
                wk_ref[rows, cols] = wk[b]
                qd_ref[rows, cols] = qd[b]
                kd_ref[rows, cols] = kd[b]
                qk_ref[h, rows, :] = qk[b]
                gl_tile = gl_tile + jnp.where(lane == h, gl[b], 0.0)
            gl_ref[rows, :] = gl_tile

    row = pl.BlockSpec((t, GDN_W), lambda i: (i, 0))
    lane = pl.BlockSpec((t, LANES), lambda i: (i, 0))
    qks = pl.BlockSpec((GDN_HEADS, t, CHUNK), lambda i: (0, i, 0))
    return pl.pallas_call(
        body, name=name, grid=(s // t,),
        in_specs=[row, row, row, lane],
        out_specs=[row, row, row, row, qks, lane],
        out_shape=[jax.ShapeDtypeStruct((s, GDN_W), F32)] * 4
        + [jax.ShapeDtypeStruct((GDN_HEADS, s, CHUNK), F32), jax.ShapeDtypeStruct((s, LANES), F32)],
        compiler_params=_params(("parallel",)),
    )(q, k, v, gb)


def _gdn_local_bwd(q, k, v, gb, du, dwk, dqd, dkd, dqk, dgl, name):
    s = q.shape[0]
    t = LOCAL_CHUNKS * CHUNK

    def body(q_ref, k_ref, v_ref, gb_ref, du_ref, dwk_ref, dqd_ref, dkd_ref, dqk_ref, dgl_ref,
             dq_ref, dk_ref, dv_ref, dgb_ref):
        _, vjp = jax.vjp(functools.partial(_gdn_local, _Dots(False)), _chunk_heads(q_ref), _chunk_heads(k_ref),
                         _chunk_heads(v_ref), [gb_ref[rows, :] for rows in _CHUNK_ROWS])
        lane = _iota((CHUNK, LANES), 1)
        dqk = jnp.stack([dqk_ref[h, rows, :] for rows in _CHUNK_ROWS for h in range(GDN_HEADS)])
        dgl = jnp.stack([jnp.sum(jnp.where(lane == h, dgl_ref[rows, :], 0.0), axis=0, keepdims=True)
                         for rows in _CHUNK_ROWS for h in range(GDN_HEADS)])
        d_q, d_k, d_v, d_gbs = vjp((_chunk_heads(du_ref), _chunk_heads(dwk_ref), _chunk_heads(dqd_ref),
                                    _chunk_heads(dkd_ref), dqk, dgl))
        for cc, rows in enumerate(_CHUNK_ROWS):
            for h in range(GDN_HEADS):
                b, cols = cc * GDN_HEADS + h, pl.ds(h * HEAD, HEAD)
                dq_ref[rows, cols] = d_q[b]
                dk_ref[rows, cols] = d_k[b]
                dv_ref[rows, cols] = d_v[b]
            dgb_ref[rows, :] = d_gbs[cc]

    row = pl.BlockSpec((t, GDN_W), lambda i: (i, 0))
    lane = pl.BlockSpec((t, LANES), lambda i: (i, 0))
    qks = pl.BlockSpec((GDN_HEADS, t, CHUNK), lambda i: (0, i, 0))
    return pl.pallas_call(
        body, name=name, grid=(s // t,),
        in_specs=[row, row, row, lane, row, row, row, row, qks, lane],
        out_specs=[row, row, row, lane],
        out_shape=[jax.ShapeDtypeStruct((s, GDN_W), F32)] * 3 + [jax.ShapeDtypeStruct((s, LANES), F32)],
        compiler_params=_params(("parallel",)),
    )(q, k, v, gb, du, dwk, dqd, dkd, dqk, dgl)


def _gdn_scan_fwd(u, wk, qd, kd, qk, gl, name):
    s = u.shape[0]
    nc = s // CHUNK

    def body(u_ref, wk_ref, qd_ref, kd_ref, qk_ref, gl_ref, o_ref, st_ref, state):
        i = pl.program_id(0)

        @pl.when(i == 0)
        def _():
            state[...] = jnp.zeros(state.shape, F32)

        st_ref[...] = state[...]
        new_states, o = _gdn_scan(_Dots(False), state[...], _heads(u_ref[...]), _heads(wk_ref[...]),
                                  _heads(qd_ref[...]), _heads(kd_ref[...]), qk_ref[...], gl_ref[...])
        state[...] = new_states
        o_ref[...] = jnp.concatenate([o[h] for h in range(GDN_HEADS)], 1)

    row = pl.BlockSpec((CHUNK, GDN_W), lambda i: (i, 0))
    return pl.pallas_call(
        body, name=name, grid=(nc,),
        in_specs=[row, row, row, row, pl.BlockSpec((GDN_HEADS, CHUNK, CHUNK), lambda i: (0, i, 0)),
                  pl.BlockSpec((CHUNK, LANES), lambda i: (i, 0))],
        out_specs=[row, pl.BlockSpec((None, GDN_HEADS, HEAD, HEAD), lambda i: (i, 0, 0, 0))],
        out_shape=[jax.ShapeDtypeStruct((s, GDN_W), F32),
                   jax.ShapeDtypeStruct((nc, GDN_HEADS, HEAD, HEAD), F32)],
        scratch_shapes=[pltpu.VMEM((GDN_HEADS, HEAD, HEAD), F32)],
        compiler_params=_params(("arbitrary",)),
    )(u, wk, qd, kd, qk, gl)


def _gdn_scan_bwd(u, wk, qd, kd, qk, gl, st, do, name):
    s = u.shape[0]
    nc = s // CHUNK

    def body(u_ref, wk_ref, qd_ref, kd_ref, qk_ref, gl_ref, st_ref, do_ref,
             du_ref, dwk_ref, dqd_ref, dkd_ref, dqk_ref, dgl_ref, dstate):
        i = pl.program_id(0)

        @pl.when(i == 0)
        def _():
            dstate[...] = jnp.zeros(dstate.shape, F32)

        _, vjp = jax.vjp(functools.partial(_gdn_scan, _Dots(False)), st_ref[...], _heads(u_ref[...]),
                         _heads(wk_ref[...]), _heads(qd_ref[...]), _heads(kd_ref[...]), qk_ref[...], gl_ref[...])
        d_states, d_u, d_wk, d_qd, d_kd, d_qk, d_gl = vjp((dstate[...], _heads(do_ref[...])))
        dstate[...] = d_states
        dqk_ref[...] = d_qk
        unheads = lambda x: jnp.concatenate([x[h] for h in range(GDN_HEADS)], 1)
        du_ref[...] = unheads(d_u)
        dwk_ref[...] = unheads(d_wk)
        dqd_ref[...] = unheads(d_qd)
        dkd_ref[...] = unheads(d_kd)
        dgl_ref[...] = d_gl

    rev = lambda i: (nc - 1 - i, 0)
    row = pl.BlockSpec((CHUNK, GDN_W), rev)
    lane = pl.BlockSpec((CHUNK, LANES), rev)
    qks = pl.BlockSpec((GDN_HEADS, CHUNK, CHUNK), lambda i: (0, nc - 1 - i, 0))
    return pl.pallas_call(
        body, name=name, grid=(nc,),
        in_specs=[row, row, row, row, qks, lane,
                  pl.BlockSpec((None, GDN_HEADS, HEAD, HEAD), lambda i: (nc - 1 - i, 0, 0, 0)), row],
        out_specs=[row, row, row, row, qks, lane],
        out_shape=[jax.ShapeDtypeStruct((s, GDN_W), F32)] * 4
        + [jax.ShapeDtypeStruct((GDN_HEADS, s, CHUNK), F32), jax.ShapeDtypeStruct((s, LANES), F32)],
        scratch_shapes=[pltpu.VMEM((GDN_HEADS, HEAD, HEAD), F32)],
        compiler_params=_params(("arbitrary",)),
    )(u, wk, qd, kd, qk, gl, st, do)


def _chunk_mask(i, j, t):
    r = i * t + _iota((t, t), 0)
    c = j * t + _iota((t, t), 1)
    return (r // CHUNK) >= (c // CHUNK)


ATT_TILE = 1024
ATT_Q_TILES = 1
ATT_BWD_TILE = 1024


def _attn_fwd(q, k, v, name):
    nh, s = MLA_HEADS, q.shape[0]
    tk = min(ATT_TILE, s)
    tq = min(ATT_Q_TILES * tk, s)
    qk = tq // tk
    nq, n = s // tq, s // tk
    nt = (((1,), (1,)), ((), ()))

    def body(q_ref, k_ref, v_ref, o_ref, lse_ref, m_sc, l_sc, acc_sc):
        i, j = pl.program_id(1), pl.program_id(2)

        @pl.when(j == 0)
        def _():
            m_sc[...] = jnp.full(m_sc.shape, -jnp.inf, F32)
            l_sc[...] = jnp.zeros(l_sc.shape, F32)
            acc_sc[...] = jnp.zeros(acc_sc.shape, F32)

        def step(masked):
            sc = lax.dot_general(q_ref[...], k_ref[...], nt, preferred_element_type=F32)
            if masked:
                r = i * tq + _iota((tq, tk), 0)
                c = j * tk + _iota((tq, tk), 1)
                sc = jnp.where((r // CHUNK) >= (c // CHUNK), sc, -jnp.inf)
            m_prev = m_sc[:, :1]
            m_new = jnp.maximum(m_prev, jnp.max(sc, axis=1, keepdims=True))
            alpha = jnp.exp(m_prev - m_new)
            p = jnp.exp(sc - m_new)
            l_sc[...] = jnp.broadcast_to(alpha * l_sc[:, :1] + jnp.sum(p, axis=1, keepdims=True), l_sc.shape)
            acc_sc[...] = alpha * acc_sc[...] + jnp.dot(p.astype(BF16), v_ref[...], preferred_element_type=F32)
            m_sc[...] = jnp.broadcast_to(m_new, m_sc.shape)

        pl.when(j < i * qk)(lambda: step(False))
        pl.when(j // qk == i)(lambda: step(True))

        @pl.when(j == n - 1)
        def _():
            o_ref[...] = acc_sc[...] / l_sc[:, :1]
            lse_ref[...] = m_sc[...] + jnp.log(l_sc[...])

    qrow = lambda h, i, j: (i, h)
    krow = lambda h, i, j: (jnp.minimum(j, (i + 1) * qk - 1), h)
    return pl.pallas_call(
        body, name=name, grid=(nh, nq, n),
        in_specs=[pl.BlockSpec((tq, QK_PAD), qrow), pl.BlockSpec((tk, QK_PAD), krow),
                  pl.BlockSpec((tk, HEAD), krow)],
        out_specs=[pl.BlockSpec((tq, HEAD), qrow), pl.BlockSpec((None, tq, LANES), lambda h, i, j: (h, i, 0))],
        out_shape=[jax.ShapeDtypeStruct((s, nh * HEAD), F32), jax.ShapeDtypeStruct((nh, s, LANES), F32)],
        scratch_shapes=[pltpu.VMEM((tq, LANES), F32), pltpu.VMEM((tq, LANES), F32), pltpu.VMEM((tq, HEAD), F32)],
        compiler_params=_params(("parallel", "parallel", "arbitrary")),
    )(q, k, v)


def _attn_bwd(q, k, v, o, do, lse, name):
    nh, s = MLA_HEADS, q.shape[0]
    t = min(ATT_BWD_TILE, s)
    n = s // t
    tn = (((0,), (0,)), ((), ()))
    nt = (((1,), (1,)), ((), ()))

    def body(q_ref, k_ref, v_ref, o_ref, do_ref, lse_ref, dq_ref, dk_ref, dv_ref, dk_acc, dv_acc, dq_acc):
        j, i = pl.program_id(1), pl.program_id(2)

        @pl.when(i + j == 0)
        def _():
            dq_acc[...] = jnp.zeros(dq_acc.shape, F32)

        @pl.when(i == 0)
        def _():
            dk_acc[...] = jnp.zeros(dk_acc.shape, F32)
            dv_acc[...] = jnp.zeros(dv_acc.shape, F32)

        def step(masked):
            qv, kv, do = q_ref[...], k_ref[...], do_ref[...]
            sc = lax.dot_general(qv, kv, nt, preferred_element_type=F32)
            p = jnp.exp(sc - lse_ref[:, :1])
            if masked:
                p = jnp.where(_chunk_mask(i, j, t), p, 0.0)
            dob = do.astype(BF16)
            dp = lax.dot_general(dob, v_ref[...], nt, preferred_element_type=F32)
            ds = (p * (dp - jnp.sum(do * o_ref[...], axis=1, keepdims=True))).astype(BF16)
            dv_acc[...] += lax.dot_general(p.astype(BF16), dob, tn, preferred_element_type=F32)
            dk_acc[...] += lax.dot_general(ds, qv, tn, preferred_element_type=F32)
            rows = pl.ds(pl.multiple_of(i * t, t), t)
            dq_acc[rows, :] += jnp.dot(ds, kv, preferred_element_type=F32)

        pl.when(i > j)(lambda: step(False))
        pl.when(i == j)(lambda: step(True))

        @pl.when(i == n - 1)
        def _():
            dk_ref[...] = dk_acc[...]
            dv_ref[...] = dv_acc[...]

        @pl.when(i + j == 2 * (n - 1))
        def _():
            dq_ref[...] = dq_acc[...]

    qrow = lambda h, j, i: (jnp.maximum(i, j), h)
    krow = lambda h, j, i: (j, h)
    return pl.pallas_call(
        body, name=name, grid=(nh, n, n),
        in_specs=[pl.BlockSpec((t, QK_PAD), qrow), pl.BlockSpec((t, QK_PAD), krow), pl.BlockSpec((t, HEAD), krow),
                  pl.BlockSpec((t, HEAD), qrow), pl.BlockSpec((t, HEAD), qrow),
                  pl.BlockSpec((None, t, LANES), lambda h, j, i: (h, jnp.maximum(i, j), 0))],
        out_specs=[pl.BlockSpec((s, QK_PAD), lambda h, j, i: (0, h)),
                   pl.BlockSpec((t, QK_PAD), krow), pl.BlockSpec((t, HEAD), krow)],
        out_shape=[jax.ShapeDtypeStruct((s, nh * QK_PAD), F32), jax.ShapeDtypeStruct((s, nh * QK_PAD), F32),
                   jax.ShapeDtypeStruct((s, nh * HEAD), F32)],
        scratch_shapes=[pltpu.VMEM((t, QK_PAD), F32), pltpu.VMEM((t, HEAD), F32), pltpu.VMEM((s, QK_PAD), F32)],
        compiler_params=_params(("arbitrary", "arbitrary", "arbitrary")),
    )(q, k, v, o, do, lse)


def _place():
    return lax.axis_index("x"), lax.axis_index("y"), lax.axis_index("c")


def _allgather8(x, name):
    r, c = x.shape

    def body(x_ref, out_ref, send_sems, recv_sems, local_sem):
        mx, my, mc = _place()
        me = 4 * mx + 2 * my + mc
        mine = pltpu.make_async_copy(x_ref, out_ref.at[me], local_sem)
        mine.start()
        copies = []
        for d in range(1, 8):
            px = 1 - mx if d & 4 else mx
            py = 1 - my if d & 2 else my
            pc = 1 - mc if d & 1 else mc
            cp = pltpu.make_async_remote_copy(
                src_ref=x_ref, dst_ref=out_ref.at[me], send_sem=send_sems.at[d - 1], recv_sem=recv_sems.at[d - 1],
                device_id=(px, py, pc), device_id_type=MESH)
            cp.start()
            copies.append(cp)
        for cp in copies:
            cp.wait()
        mine.wait()

    return pl.pallas_call(
        body, name=name,
        out_shape=jax.ShapeDtypeStruct((8, r, c), x.dtype),
        in_specs=[pl.BlockSpec(memory_space=pltpu.VMEM)],
        out_specs=pl.BlockSpec(memory_space=pltpu.VMEM),
        scratch_shapes=[pltpu.SemaphoreType.DMA((7,)), pltpu.SemaphoreType.DMA((7,)), pltpu.SemaphoreType.DMA],
        compiler_params=pltpu.CompilerParams(vmem_limit_bytes=VMEM_LIMIT),
    )(x)


def _allgather_chips(x, name):
    r, c = x.shape
    rh = r // 2

    def body(x_ref, out_ref, send_sems, recv_sems, local_sem):
        mx, my, mc = _place()
        j = 2 * mx + my
        chips = [(1 - mx, my), (mx, 1 - my), (1 - mx, 1 - my)]

        def half(jj, hc):
            return out_ref.at[jj, pl.ds(hc * rh, rh), :]

        mine = pltpu.make_async_copy(x_ref, out_ref.at[j], local_sem)
        mine.start()
        first = []
        for kk, (px, py) in enumerate(chips):
            cp = pltpu.make_async_remote_copy(
                src_ref=x_ref.at[pl.ds(mc * rh, rh), :], dst_ref=half(j, mc),
                send_sem=send_sems.at[kk], recv_sem=recv_sems.at[kk], device_id=(px, py, mc), device_id_type=MESH)
            cp.start()
            first.append(cp)
        passed = []
        for kk, (px, py) in enumerate(chips):
            jj = 2 * px + py
            pltpu.make_async_remote_copy(
                src_ref=x_ref.at[pl.ds(mc * rh, rh), :], dst_ref=half(jj, mc),
                send_sem=send_sems.at[kk], recv_sem=recv_sems.at[kk], device_id=(px, py, mc),
                device_id_type=MESH).wait_recv()
            cp = pltpu.make_async_remote_copy(
                src_ref=half(jj, mc), dst_ref=half(jj, mc), send_sem=send_sems.at[3 + kk],
                recv_sem=recv_sems.at[3 + kk], device_id=(mx, my, 1 - mc), device_id_type=MESH)
            cp.start()
            passed.append(cp)
        for kk, (px, py) in enumerate(chips):
            jj = 2 * px + py
            pltpu.make_async_remote_copy(
                src_ref=half(jj, 1 - mc), dst_ref=half(jj, 1 - mc), send_sem=send_sems.at[3 + kk],
                recv_sem=recv_sems.at[3 + kk], device_id=(mx, my, 1 - mc), device_id_type=MESH).wait_recv()
        for cp in first + passed:
            cp.wait_send()
        mine.wait()

    return pl.pallas_call(
        body, name=name,
        out_shape=jax.ShapeDtypeStruct((4, r, c), x.dtype),
        in_specs=[pl.BlockSpec(memory_space=pltpu.VMEM)],
        out_specs=pl.BlockSpec(memory_space=pltpu.VMEM),
        scratch_shapes=[pltpu.SemaphoreType.DMA((6,)), pltpu.SemaphoreType.DMA((6,)), pltpu.SemaphoreType.DMA],
        compiler_params=pltpu.CompilerParams(vmem_limit_bytes=VMEM_LIMIT),
    )(x)


RS_ROWS = 32


def _reduce_scatter_chips(g, name):
    _, r, c = g.shape
    rh = r // 2
    steps = rh // RS_ROWS

    def body(g_ref, out_ref, sib_ref, part_ref, got_ref, send_sems, recv_sems):
        mx, my, mc = _place()
        j = 2 * mx + my
        sibling = (mx, my, 1 - mc)
        chips = [(1 - mx, my), (mx, 1 - my), (1 - mx, 1 - my)]

        to_sib = pltpu.make_async_remote_copy(
            src_ref=g_ref.at[:, pl.ds((1 - mc) * rh, rh), :], dst_ref=sib_ref,
            send_sem=send_sems.at[0], recv_sem=recv_sems.at[0], device_id=sibling, device_id_type=MESH)
        to_sib.start()
        to_sib.wait()

        def add_sibling(step, carry):
            rows = pl.ds(pl.multiple_of(step * RS_ROWS, RS_ROWS), RS_ROWS)
            mine = g_ref[:, pl.ds(pl.multiple_of(mc * rh + step * RS_ROWS, RS_ROWS), RS_ROWS), :]
            part_ref[:, rows, :] = mine.astype(F32) + sib_ref[:, rows, :].astype(F32)
            return carry

        lax.fori_loop(0, steps, add_sibling, 0)

        def to_bf16(step, carry):
            rows = pl.ds(pl.multiple_of(step * RS_ROWS, RS_ROWS), RS_ROWS)
            sib_ref[:, rows, :] = part_ref[:, rows, :].astype(BF16)
            return carry

        lax.fori_loop(0, steps, to_bf16, 0)

        sends = []
        for kk, (px, py) in enumerate(chips):
            cp = pltpu.make_async_remote_copy(
                src_ref=sib_ref.at[2 * px + py], dst_ref=got_ref.at[kk],
                send_sem=send_sems.at[1 + kk], recv_sem=recv_sems.at[1 + kk],
                device_id=(px, py, mc), device_id_type=MESH)
            cp.start()
            sends.append(cp)
        for cp in sends:
            cp.wait()

        def total(step, carry):
            rows = pl.ds(pl.multiple_of(step * RS_ROWS, RS_ROWS), RS_ROWS)
            acc = part_ref[j, rows, :]
            for kk in range(3):
                acc = acc + got_ref[kk, rows, :].astype(F32)
            out_ref[pl.ds(pl.multiple_of(mc * rh + step * RS_ROWS, RS_ROWS), RS_ROWS), :] = acc
            return carry

        lax.fori_loop(0, steps, total, 0)

        done = pltpu.make_async_remote_copy(
            src_ref=out_ref.at[pl.ds(mc * rh, rh), :], dst_ref=out_ref.at[pl.ds(mc * rh, rh), :],
            send_sem=send_sems.at[4], recv_sem=recv_sems.at[4], device_id=sibling, device_id_type=MESH)
        done.start()
        done.wait_send()
        pltpu.make_async_remote_copy(
            src_ref=out_ref.at[pl.ds((1 - mc) * rh, rh), :], dst_ref=out_ref.at[pl.ds((1 - mc) * rh, rh), :],
            send_sem=send_sems.at[4], recv_sem=recv_sems.at[4], device_id=sibling, device_id_type=MESH).wait_recv()

    return pl.pallas_call(
        body, name=name,
        out_shape=jax.ShapeDtypeStruct((r, c), F32),
        in_specs=[pl.BlockSpec(memory_space=pltpu.VMEM)],
        out_specs=pl.BlockSpec(memory_space=pltpu.VMEM),
        scratch_shapes=[pltpu.VMEM((4, rh, c), BF16), pltpu.VMEM((4, rh, c), F32), pltpu.VMEM((3, rh, c), BF16),
                        pltpu.SemaphoreType.DMA((5,)), pltpu.SemaphoreType.DMA((5,))],
        compiler_params=pltpu.CompilerParams(vmem_limit_bytes=VMEM_LIMIT),
    )(g)


def _sum8(x, name):
    _, r, c = x.shape

    def body(x_ref, o_ref):
        acc = x_ref[0]
        for d in range(1, 8):
            acc = acc + x_ref[d]
        o_ref[...] = acc

    return pl.pallas_call(
        body, name=name, out_shape=jax.ShapeDtypeStruct((r, c), F32),
        in_specs=[pl.BlockSpec(memory_space=pltpu.VMEM)], out_specs=pl.BlockSpec(memory_space=pltpu.VMEM),
    )(x)


def _adamw(w, g, m, v, name):
    r, c = w.shape
    t = _pick(r, 256, SUBLANES)
    spec = pl.BlockSpec((t, c), lambda i: (i, 0))

    def body(w_ref, g_ref, m_ref, v_ref, d_ref, nm_ref, nv_ref):
        gv = g_ref[...]
        m_new = ADAM_B1 * m_ref[...] + (1.0 - ADAM_B1) * gv
        v_new = ADAM_B2 * v_ref[...] + (1.0 - ADAM_B2) * (gv * gv)
        m_hat = m_new / (1.0 - ADAM_B1 ** ADAM_STEP)
        v_hat = v_new / (1.0 - ADAM_B2 ** ADAM_STEP)
        d_ref[...] = -ADAM_LR * (m_hat / (jnp.sqrt(v_hat) + ADAM_EPS) + ADAM_WD * w_ref[...])
        nm_ref[...] = m_new
        nv_ref[...] = v_new

    return pl.pallas_call(
        body, name=name, grid=(r // t,), in_specs=[spec] * 4, out_specs=[spec] * 3,
        out_shape=[jax.ShapeDtypeStruct((r, c), F32)] * 3, compiler_params=_params(("parallel",)),
    )(w, g, m, v)


def _pack_rows(parts):
    rows, offs, o = [], [], 0
    for p in parts:
        f = p.reshape(-1)
        n = -(-f.shape[0] // (LANES * SUBLANES)) * SUBLANES
        rows.append(jnp.pad(f, (0, n * LANES - f.shape[0])).reshape(n, LANES))
        offs.append((o, n))
        o += n
    return jnp.concatenate(rows, 0), offs


def _unpack_rows(packed, offs, shapes):
    out = []
    for (o, n), shp in zip(offs, shapes):
        size = 1
        for d in shp:
            size *= d
        out.append(packed[o:o + n].reshape(-1)[:size].reshape(shp))
    return out


def _ffn_fwd(x, s, sh, g, w_in, w_out, tag):
    (h,) = _rowcall(lambda r, p: ([_modulate(r[0], p[0], p[1])], []), [x], [s, sh], [(x.shape[1], BF16)], [],
                    tile=512, name=tag + "_mod")
    gu = _mm(h, w_in, "nn", BF16, tag + "_in")
    (act,) = _rowcall(lambda r, p: ([_silu(r[0].astype(F32)) * r[1].astype(F32)], []),
                      [(gu, D_FF, 0), (gu, D_FF, 1)], [], [(D_FF, BF16)], [], tile=256, name=tag + "_act")
    f = _mm(act, w_out, "nn", F32, tag + "_out")
    (y,) = _rowcall(lambda r, p: ([r[0] + 0.5 * p[0] * r[1]], []), [x, f], [g], [(x.shape[1], F32)], [],
                    tile=512, name=tag + "_res")
    return y, (x, h, gu, act, f)


def _ffn_bwd(dy, saved, s, sh, g, w_in, w_out, tag):
    x, h, gu, act, f = saved
    d = x.shape[1]
    df, dg = _rowcall(lambda r, p: ([0.5 * p[0] * r[0]], [0.5 * jnp.sum(r[0] * r[1], 0, keepdims=True)]),
                      [dy, f], [g], [(d, BF16)], [(1, d)], tile=512, name=tag + "_bres")
    da = _mm(df, w_out, "nt", BF16, tag + "_bout")
    dw_out = _mm(act, df, "tn", BF16, tag + "_bwout")

    def act_bwd(r, p):
        gate, up, dav = r[0].astype(F32), r[1].astype(F32), r[2].astype(F32)
        _, vjp = jax.vjp(lambda a, b: _silu(a) * b, gate, up)
        dgate, dup = vjp(dav)
        return [jnp.concatenate([dgate, dup], 1)], []

    (dgu,) = _rowcall(act_bwd, [(gu, D_FF, 0), (gu, D_FF, 1), da], [], [(2 * D_FF, BF16)], [], tile=256,
                      name=tag + "_bact")
    dh = _mm(dgu, w_in, "nt", F32, tag + "_bin")
    dw_in = _mm(h, dgu, "tn", BF16, tag + "_bwin")

    def mod_bwd(r, p):
        _, vjp = jax.vjp(_modulate, r[0], p[0], p[1])
        dx, ds, dsh = vjp(r[1])
        return [r[2] + dx], [ds, dsh]

    dx, ds, dsh = _rowcall(mod_bwd, [x, dh, dy], [s, sh], [(d, F32)], [(1, d), (1, d)], tile=512, name=tag + "_bmod")
    return dx, (dsh, ds, dg), dw_in, dw_out


def _mixer_fwd(x, s, sh, g, wts, rope):
    w_in_p, conv8, a_log, dt_bias, wn, wq, w_uq_p, wkv, w_ukv, wqn, wqr, wkn, wkr, won, w_out = wts
    cos2, sin2 = rope
    d = x.shape[1]
    (h,) = _rowcall(lambda r, p: ([_modulate(r[0], p[0], p[1])], []), [x], [s, sh], [(d, BF16)], [],
                    tile=512, name="mix_mod")
    proj = _mm(h, w_in_p, "nn", F32, "mix_in")
    qkv_c = _conv_fwd(proj, conv8, "mix_conv")
    gab = (proj, LANES, 23)

    q, k, v, gb = _rowcall(
        lambda r, p: (list(_gdn_prep_core(_split(r[0], [HEAD] * 12), r[1], p[0], p[1])), []),
        [qkv_c, gab], [a_log, dt_bias], [(512, F32)] * 3 + [(LANES, F32)], [], tile=256, name="mix_gdn_prep")
    gdn_local = _gdn_local_fwd(q, k, v, gb, "mix_gdn_local")
    o_gdn, gdn_states = _gdn_scan_fwd(*gdn_local, "mix_gdn_scan")
    states = (gdn_local, gdn_states)

    cq, ckv, kr = (proj, 512, 4), (proj, 256, 10), (proj, LANES, 22)
    cqn, ckvn, k_rope = _rowcall(
        lambda r, p: (list(_mla_prep_core(r[0][:, :MLA_Q_LORA], r[1], r[2], r[3], r[4], p[0], p[1], p[2])), []),
        [cq, ckv, kr, cos2, sin2], [wq, wkv, wkr], [(MLA_Q_LORA, BF16), (MLA_KV_LORA, BF16), (LANES, F32)], [],
        tile=512, name="mix_mla_prep")
    qf = _mm(cqn, w_uq_p, "nn", F32, "mix_uq")
    kvf = _mm(ckvn, w_ukv, "nn", F32, "mix_ukv")

    def qk_prep(r, p):
        qparts = _split(r[0], [HEAD] * 8)
        kvparts = _split(r[1], [HEAD] * 8)
        qs, ks, vs = _qk_prep_core(qparts[:4], qparts[4:], kvparts[0::2], kvparts[1::2], r[2], r[3], r[4],
                                   p[0], p[1], p[2])
        return [jnp.concatenate(qs, 1), jnp.concatenate(ks, 1), jnp.concatenate(vs, 1)], []

    qa, ka, va = _rowcall(qk_prep, [qf, kvf, k_rope, cos2, sin2], [wqn, wqr, wkn],
                          [(4 * QK_PAD, BF16), (4 * QK_PAD, BF16), (4 * HEAD, BF16)], [], tile=256,
                          name="mix_qk_prep")
    o_b, lse = _attn_fwd(qa, ka, va, "mix_attn")

    gz = (proj, 512, 3)
    (mixed,) = _rowcall(
        lambda r, p: ([_mix_post_core(_split(r[0], HW4), _split(r[1], HW4), _split(r[2], HW4), p[0], p[1])], []),
        [o_gdn, gz, o_b], [wn, won], [(2 * 512, BF16)], [], tile=512, name="mix_post")
    y = _mm(mixed, w_out, "nn", F32, "mix_out")
    (x_out,) = _rowcall(lambda r, p: ([r[0] + p[0] * r[1]], []), [x, y], [g], [(d, F32)], [], tile=512,
                        name="mix_res")
    saved = (x, h, proj, qkv_c, q, k, v, gb, states, o_gdn, cqn, ckvn, k_rope, qf, kvf, qa, ka, va, o_b, lse,
             mixed, y)
    return x_out, saved


def _mixer_bwd(dy, saved, s, sh, g, wts, rope):
    w_in_p, conv8, a_log, dt_bias, wn, wq, w_uq_p, wkv, w_ukv, wqn, wqr, wkn, wkr, won, w_out = wts
    cos2, sin2 = rope
    (x, h, proj, qkv_c, q, k, v, gb, states, o_gdn, cqn, ckvn, k_rope, qf, kvf, qa, ka, va, o_b, lse,
     mixed, y) = saved
    d = x.shape[1]
    dyb, dg = _rowcall(lambda r, p: ([p[0] * r[0]], [jnp.sum(r[0] * r[1], 0, keepdims=True)]),
                       [dy, y], [g], [(d, BF16)], [(1, d)], tile=512, name="mix_bres")
    dmixed = _mm(dyb, w_out, "nt", F32, "mix_bout")
    dw_out = _mm(mixed, dyb, "tn", BF16, "mix_bwout")

    gz = (proj, 512, 3)

    def post_bwd(r, p):
        _, vjp = jax.vjp(_mix_post_core, _split(r[0], HW4), _split(r[1], HW4), _split(r[2], HW4), p[0], p[1])
        do, dz, dob, dwn, dwon = vjp(r[3])
        return [jnp.concatenate(do, 1), jnp.concatenate(dz, 1), jnp.concatenate(dob, 1)], [dwn, dwon]

    do_gdn, dgz, do_b, dwn, dwon = _rowcall(post_bwd, [o_gdn, gz, o_b, dmixed], [wn, won], [(512, F32)] * 3,
                                            [(1, HEAD), (1, HEAD)], tile=256, name="mix_bpost")

    dqa, dka, dva = _attn_bwd(qa, ka, va, o_b, do_b, lse, "mix_battn")

    def qk_bwd(r, p):
        qparts = _split(r[0], [HEAD] * 8)
        kvparts = _split(r[1], [HEAD] * 8)
        _, vjp = jax.vjp(_qk_prep_core, qparts[:4], qparts[4:], kvparts[0::2], kvparts[1::2], r[2], r[3], r[4],
                         p[0], p[1], p[2])
        cot = (_split(r[5], [QK_PAD] * 4), _split(r[6], [QK_PAD] * 4), _split(r[7], HW4))
        dqn, dqr, dkn, dvp, dkrope, _, _, dwqn, dwqr, dwkn = vjp(cot)
        dkv = []
        for a, b in zip(dkn, dvp):
            dkv += [a, b]
        return [jnp.concatenate(list(dqn) + list(dqr), 1), jnp.concatenate(dkv, 1), dkrope], [dwqn, dwqr, dwkn]

    dqf, dkvf, dk_rope, dwqn, dwqr, dwkn = _rowcall(
        qk_bwd, [qf, kvf, k_rope, cos2, sin2, dqa, dka, dva], [wqn, wqr, wkn],
        [(8 * HEAD, BF16), (8 * HEAD, BF16), (LANES, F32)], [(1, HEAD)] * 3, tile=256, name="mix_bqk_prep")
    dcqn = _mm(dqf, w_uq_p, "nt", F32, "mix_buq")
    dw_uq_p = _mm(cqn, dqf, "tn", F32, "mix_bwuq")
    dckvn = _mm(dkvf, w_ukv, "nt", F32, "mix_bukv")
    dw_ukv = _mm(ckvn, dkvf, "tn", F32, "mix_bwukv")

    cq, ckv, kr = (proj, 512, 4), (proj, 256, 10), (proj, LANES, 22)

    def mla_bwd(r, p):
        _, vjp = jax.vjp(_mla_prep_core, r[0][:, :MLA_Q_LORA], r[1], r[2], r[3], r[4], p[0], p[1], p[2])
        dcq, dckv, dkr, _, _, dwq, dwkv, dwkr = vjp((r[5], r[6], r[7]))
        pad = jnp.zeros((dcq.shape[0], 512 - MLA_Q_LORA), F32)
        return [jnp.concatenate([dcq, pad], 1), dckv, dkr], [dwq, dwkv, dwkr]

    dcq, dckv, dkr, dwq, dwkv, dwkr = _rowcall(
        mla_bwd, [cq, ckv, kr, cos2, sin2, dcqn, dckvn, dk_rope], [wq, wkv, wkr],
        [(512, F32), (MLA_KV_LORA, F32), (LANES, F32)], [(1, MLA_Q_LORA), (1, MLA_KV_LORA), (1, LANES)],
        tile=512, name="mix_bmla_prep")

    gdn_local, gdn_states = states
    d_local = _gdn_scan_bwd(*gdn_local, gdn_states, do_gdn, "mix_bgdn_scan")
    dq, dk, dv, dgb = _gdn_local_bwd(q, k, v, gb, *d_local, "mix_bgdn_local")
    gab = (proj, LANES, 23)

    def gdn_prep_bwd(r, p):
        _, vjp = jax.vjp(_gdn_prep_core, _split(r[0], [HEAD] * 12), r[1], p[0], p[1])
        dparts, dgab, da_log, ddt = vjp((r[2], r[3], r[4], r[5]))
        return [jnp.concatenate(dparts, 1), dgab], [da_log, ddt]

    dqkv_c, dgab, da_log, ddt = _rowcall(gdn_prep_bwd, [qkv_c, gab, dq, dk, dv, dgb], [a_log, dt_bias],
                                         [(1536, F32), (LANES, F32)], [(1, LANES), (1, LANES)], tile=256,
                                         name="mix_bgdn_prep")
    dqkv_pre, dconv8 = _conv_bwd(proj, dqkv_c, conv8, "mix_bconv")

    dproj = jnp.concatenate([dqkv_pre.astype(BF16), dgz.astype(BF16), dcq.astype(BF16), dckv.astype(BF16),
                             dkr.astype(BF16), dgab.astype(BF16)], axis=1)
    dh = _mm(dproj, w_in_p, "nt", F32, "mix_bin")
    dw_in_p = _mm(h, dproj, "tn", F32, "mix_bwin")

    def mod_bwd(r, p):
        _, vjp = jax.vjp(_modulate, r[0], p[0], p[1])
        dx, ds, dsh = vjp(r[1])
        return [r[2] + dx], [ds, dsh]

    dx, ds, dsh = _rowcall(mod_bwd, [x, dh, dy], [s, sh], [(d, F32)], [(1, d), (1, d)], tile=512, name="mix_bmod")
    small = dict(conv=dconv8, a_log=da_log, dt=ddt, wn=dwn, wq=dwq, wkv=dwkv, wqn=dwqn, wqr=dwqr, wkn=dwkn,
                 wkr=dwkr, won=dwon)
    return dx, (dsh, ds, dg), dw_in_p, dw_uq_p, dw_ukv, dw_out, small


def _pad_cols(a, n):
    return jnp.pad(a, ((0, 0),) * (a.ndim - 1) + ((0, n - a.shape[-1]),))


def _pack_w_in(w):
    z = lambda n: jnp.zeros((w.shape[0], n), w.dtype)
    return jnp.concatenate([w[:, 0:2048], w[:, 2056:2440], z(128), w[:, 2440:2696], w[:, 2696:2760], z(64),
                            w[:, 2048:2056], z(120)], axis=1)


def _unpack_w_in(wp):
    return jnp.concatenate([wp[:, 0:2048], wp[:, 2944:2952], wp[:, 2048:2432], wp[:, 2560:2816], wp[:, 2816:2880]],
                           axis=1)


def _pack_w_uq(w):
    z = jnp.zeros((w.shape[0], LANES - MLA_ROPE), w.dtype)
    nope = [w[:, h * 192:h * 192 + HEAD] for h in range(MLA_HEADS)]
    rope = []
    for h in range(MLA_HEADS):
        rope += [w[:, h * 192 + HEAD:(h + 1) * 192], z]
    return jnp.concatenate(nope + rope, axis=1)


def _unpack_w_uq(wp):
    cols = []
    for h in range(MLA_HEADS):
        cols += [wp[:, h * HEAD:(h + 1) * HEAD], wp[:, 512 + h * LANES:512 + h * LANES + MLA_ROPE]]
    return jnp.concatenate(cols, axis=1)


def _cols_to_chips(a):
    r, c = a.shape
    return a.reshape(r, 4, c // 4).transpose(1, 0, 2)


def _chips_to_cols(a):
    _, r, n = a.shape
    return a.transpose(1, 0, 2).reshape(r, 4 * n)


def _pad128(v, n=LANES):
    return _pad_cols(v.reshape(1, -1), n)


def kernel(x, c, positions, w_ada, b_ada, ffn1_w_in, ffn1_w_out, w_in, gdn_conv_w, gdn_a_log, gdn_dt_bias, gdn_norm_w, mla_q_norm_w, mla_w_uq, mla_kv_norm_w, mla_w_ukv, qkn_q_nope, qkn_q_rope, qkn_k_nope, qkn_k_rope, mla_out_norm_w, w_out, ffn2_w_in, ffn2_w_out, loss_target, m_w_ada, m_b_ada, m_ffn1_w_in, m_ffn1_w_out, m_w_in, m_gdn_conv_w, m_gdn_a_log, m_gdn_dt_bias, m_gdn_norm_w, m_mla_q_norm_w, m_mla_w_uq, m_mla_kv_norm_w, m_mla_w_ukv, m_qkn_q_nope, m_qkn_q_rope, m_qkn_k_nope, m_qkn_k_rope, m_mla_out_norm_w, m_w_out, m_ffn2_w_in, m_ffn2_w_out, v_w_ada, v_b_ada, v_ffn1_w_in, v_ffn1_w_out, v_w_in, v_gdn_conv_w, v_gdn_a_log, v_gdn_dt_bias, v_gdn_norm_w, v_mla_q_norm_w, v_mla_w_uq, v_mla_kv_norm_w, v_mla_w_ukv, v_qkn_q_nope, v_qkn_q_rope, v_qkn_k_nope, v_qkn_k_rope, v_mla_out_norm_w, v_w_out, v_ffn2_w_in, v_ffn2_w_out):
    weights = dict(w_ada=w_ada, b_ada=b_ada, ffn1_w_in=ffn1_w_in, ffn1_w_out=ffn1_w_out, w_in=w_in,
                   gdn_conv_w=gdn_conv_w, gdn_a_log=gdn_a_log, gdn_dt_bias=gdn_dt_bias, gdn_norm_w=gdn_norm_w,
                   mla_q_norm_w=mla_q_norm_w, mla_w_uq=mla_w_uq, mla_kv_norm_w=mla_kv_norm_w, mla_w_ukv=mla_w_ukv,
                   qkn_q_nope=qkn_q_nope, qkn_q_rope=qkn_q_rope, qkn_k_nope=qkn_k_nope, qkn_k_rope=qkn_k_rope,
                   mla_out_norm_w=mla_out_norm_w, w_out=w_out, ffn2_w_in=ffn2_w_in, ffn2_w_out=ffn2_w_out)
    moms_m = dict(w_ada=m_w_ada, b_ada=m_b_ada, ffn1_w_in=m_ffn1_w_in, ffn1_w_out=m_ffn1_w_out, w_in=m_w_in,
                  gdn_conv_w=m_gdn_conv_w, gdn_a_log=m_gdn_a_log, gdn_dt_bias=m_gdn_dt_bias,
                  gdn_norm_w=m_gdn_norm_w, mla_q_norm_w=m_mla_q_norm_w, mla_w_uq=m_mla_w_uq,
                  mla_kv_norm_w=m_mla_kv_norm_w, mla_w_ukv=m_mla_w_ukv, qkn_q_nope=m_qkn_q_nope,
                  qkn_q_rope=m_qkn_q_rope, qkn_k_nope=m_qkn_k_nope, qkn_k_rope=m_qkn_k_rope,
                  mla_out_norm_w=m_mla_out_norm_w, w_out=m_w_out, ffn2_w_in=m_ffn2_w_in, ffn2_w_out=m_ffn2_w_out)
    moms_v = dict(w_ada=v_w_ada, b_ada=v_b_ada, ffn1_w_in=v_ffn1_w_in, ffn1_w_out=v_ffn1_w_out, w_in=v_w_in,
                  gdn_conv_w=v_gdn_conv_w, gdn_a_log=v_gdn_a_log, gdn_dt_bias=v_gdn_dt_bias,
                  gdn_norm_w=v_gdn_norm_w, mla_q_norm_w=v_mla_q_norm_w, mla_w_uq=v_mla_w_uq,
                  mla_kv_norm_w=v_mla_kv_norm_w, mla_w_ukv=v_mla_w_ukv, qkn_q_nope=v_qkn_q_nope,
                  qkn_q_rope=v_qkn_q_rope, qkn_k_nope=v_qkn_k_nope, qkn_k_rope=v_qkn_k_rope,
                  mla_out_norm_w=v_mla_out_norm_w, w_out=v_w_out, ffn2_w_in=v_ffn2_w_in, ffn2_w_out=v_ffn2_w_out)
    names = list(weights)

    seq, d = x.shape[1], x.shape[2]
    x2d = x.reshape(seq, d)
    tgt = loss_target.reshape(seq, d)
    mx, my, mc = _place()
    chip = 2 * mx + my
    me = 2 * chip + mc
    n_mod = b_ada.shape[1] // d
    shard = w_ada.shape[2]

    half = MLA_ROPE // 2
    inv_freq = 10000.0 ** (-jnp.arange(half, dtype=F32) / half)
    ang = positions.astype(F32).reshape(seq, 1) * inv_freq
    cosv, sinv = jnp.cos(ang), jnp.sin(ang)
    cos2 = _pad_cols(jnp.concatenate([cosv, cosv], 1), LANES)
    sin2 = _pad_cols(jnp.concatenate([-sinv, sinv], 1), LANES)
    rope = (cos2, sin2)

    c_all = _allgather8(jnp.pad(c, ((0, SUBLANES - 1), (0, 0))), "gather_c")[:, 0, :]
    (sc_all,) = _rowcall(lambda r, p: ([_silu(r[0])], []), [c_all], [], [(d, F32)], [], tile=8, name="ada_silu")
    mod_part = _mm(sc_all, w_ada[0], "nn", F32, "ada_mm", hi=True)
    mod_all = _allgather8(mod_part, "gather_mod")
    mod_rows = lax.dynamic_index_in_dim(mod_all, me, axis=1, keepdims=False)
    mod_raw = jnp.concatenate([mod_rows[2 * jj] for jj in range(4)], axis=0).reshape(1, 4 * shard)
    (mod,) = _rowcall(lambda r, p: ([r[0] + r[1]], []),
                      [jnp.pad(mod_raw, ((0, 7), (0, 0))), jnp.pad(b_ada, ((0, 7), (0, 0)))], [],
                      [(4 * shard, F32)], [], tile=8, name="ada_bias")
    mods = [mod[0:1, i * d:(i + 1) * d] for i in range(n_mod)]
    sh1, s1, g1, sh2, s2, g2, sh3, s3, g3 = mods

    def gather_cols(w, name, pad_to=None):
        w2 = w[0].astype(BF16)
        n = w2.shape[1]
        if pad_to:
            w2 = _pad_cols(w2, pad_to)
        return _chips_to_cols(_allgather_chips(w2, name)[:, :, :n])

    def gather_rows(w, name):
        w2 = w[0].astype(BF16)
        return _allgather_chips(w2, name).reshape(4 * w2.shape[0], w2.shape[1])

    f1_in = gather_cols(ffn1_w_in, "gather_f1_in")
    f1_out = gather_rows(ffn1_w_out, "gather_f1_out")
    w_in_full = gather_cols(w_in, "gather_w_in", 768)
    w_uq_full = gather_cols(mla_w_uq, "gather_w_uq", 256)
    w_ukv_full = gather_cols(mla_w_ukv, "gather_w_ukv")
    w_out_full = gather_rows(w_out, "gather_w_out")
    f2_in = gather_cols(ffn2_w_in, "gather_f2_in")
    f2_out = gather_rows(ffn2_w_out, "gather_f2_out")
    conv_all = _allgather8(jnp.pad(gdn_conv_w[0], ((0, SUBLANES - CONV_K), (0, 0))), "gather_conv")
    conv8 = jnp.concatenate([conv_all[2 * jj] for jj in range(4)], axis=1)

    wts = (_pack_w_in(w_in_full), conv8, _pad128(gdn_a_log), _pad128(gdn_dt_bias), gdn_norm_w,
           mla_q_norm_w, _pack_w_uq(w_uq_full), mla_kv_norm_w, w_ukv_full, qkn_q_nope, _pad128(qkn_q_rope),
           qkn_k_nope, _pad128(qkn_k_rope), mla_out_norm_w, w_out_full)

    x1, sv1 = _ffn_fwd(x2d, s1, sh1, g1, f1_in, f1_out, "ffn1")
    xm, svm = _mixer_fwd(x1, s2, sh2, g2, wts, rope)
    x3, sv3 = _ffn_fwd(xm, s3, sh3, g3, f2_in, f2_out, "ffn2")

    def loss_fn(r, p):
        err = r[0] - r[1]
        part = 0.5 * jnp.sum(jnp.sum(err * err, axis=1, keepdims=True) * (1.0 / d), axis=0, keepdims=True)
        return [err * (1.0 / d)], [jnp.broadcast_to(part, (1, LANES))]

    dy, loss_part = _rowcall(loss_fn, [x3, tgt], [], [(d, F32)], [(1, LANES)], tile=512, name="loss")
    loss = lax.psum(loss_part[0, 0], ("x", "y", "c"))

    dxm, dmod3, dw_f2_in, dw_f2_out = _ffn_bwd(dy, sv3, s3, sh3, g3, f2_in, f2_out, "ffn2")
    dx1, dmod2, dw_in_p, dw_uq_p, dw_ukv, dw_out_m, small = _mixer_bwd(dxm, svm, s2, sh2, g2, wts, rope)
    dx0, dmod1, dw_f1_in, dw_f1_out = _ffn_bwd(dx1, sv1, s1, sh1, g1, f1_in, f1_out, "ffn1")
    grad_x = dx0.reshape(x.shape)

    dmod = jnp.concatenate(list(dmod1) + list(dmod2) + list(dmod3), axis=1)
    small_parts = [dmod, small["conv"][:CONV_K], small["a_log"], small["dt"], small["wn"], small["wq"],
                   small["wkv"], small["wqn"], small["wqr"], small["wkn"], small["wkr"], small["won"]]
    packed, offs = _pack_rows(small_parts)
    gathered = _allgather8(packed, "gather_small")
    total = _sum8(gathered, "sum_small")
    (g_b_ada, g_conv, g_a_log, g_dt, g_wn, g_wq, g_wkv, g_wqn, g_wqr, g_wkn, g_wkr, g_won) = _unpack_rows(
        total, offs, [p.shape for p in small_parts])
    dmod_all = _unpack_rows(gathered.reshape(-1, LANES),
                            [(dd * packed.shape[0] + offs[0][0], offs[0][1]) for dd in range(8)],
                            [dmod.shape] * 8)
    dmod_all = jnp.concatenate(dmod_all, axis=0)
    dmod_mine = lax.dynamic_slice_in_dim(dmod_all, chip * shard, shard, axis=1)

    def ada_grad(r, p):
        acc = jnp.zeros((r[0].shape[0], shard), F32)
        for b in range(8):
            acc = acc + r[0][:, b:b + 1] * p[0][b:b + 1, :]
        return [acc], []

    (g_w_ada,) = _rowcall(ada_grad, [_pad_cols(sc_all.T, LANES)], [dmod_mine], [(shard, F32)], [], tile=256,
                          name="ada_grad")

    grads = dict(
        w_ada=g_w_ada[None], b_ada=g_b_ada,
        gdn_conv_w=lax.dynamic_slice_in_dim(g_conv, chip * gdn_conv_w.shape[2], gdn_conv_w.shape[2], axis=1)[None],
        gdn_a_log=g_a_log[:, :GDN_HEADS], gdn_dt_bias=g_dt[:, :GDN_HEADS], gdn_norm_w=g_wn, mla_q_norm_w=g_wq,
        mla_kv_norm_w=g_wkv, qkn_q_nope=g_wqn, qkn_q_rope=g_wqr[:, :MLA_ROPE], qkn_k_nope=g_wkn,
        qkn_k_rope=g_wkr[:, :MLA_ROPE], mla_out_norm_w=g_won)

    def rs_cols(dw, name, pad_to=None):
        g4 = _cols_to_chips(dw).astype(BF16)
        n = g4.shape[2]
        if pad_to:
            g4 = _pad_cols(g4, pad_to)
        return _reduce_scatter_chips(g4, name)[:, :n][None]

    def rs_rows(dw, name):
        r, cc = dw.shape
        return _reduce_scatter_chips(dw.astype(BF16).reshape(4, r // 4, cc), name)[None]

    grads["ffn2_w_in"] = rs_cols(dw_f2_in, "rs_f2_in")
    grads["ffn2_w_out"] = rs_rows(dw_f2_out, "rs_f2_out")
    grads["w_in"] = rs_cols(_unpack_w_in(dw_in_p), "rs_w_in", 768)
    grads["mla_w_uq"] = rs_cols(_unpack_w_uq(dw_uq_p), "rs_w_uq", 256)
    grads["mla_w_ukv"] = rs_cols(dw_ukv, "rs_w_ukv")
    grads["w_out"] = rs_rows(dw_out_m, "rs_w_out")
    grads["ffn1_w_in"] = rs_cols(dw_f1_in, "rs_f1_in")
    grads["ffn1_w_out"] = rs_rows(dw_f1_out, "rs_f1_out")

    big = ["w_ada", "ffn1_w_in", "ffn1_w_out", "w_in", "mla_w_uq", "mla_w_ukv", "w_out", "ffn2_w_in", "ffn2_w_out"]
    delta, new_m, new_v = {}, {}, {}
    for nme in big:
        shp = weights[nme].shape
        dl, nm, nv = _adamw(weights[nme][0], grads[nme][0], moms_m[nme][0], moms_v[nme][0], "adamw_" + nme)
        delta[nme], new_m[nme], new_v[nme] = dl.reshape(shp), nm.reshape(shp), nv.reshape(shp)
    tiny = [nme for nme in names if nme not in big]
    shapes = [weights[nme].shape for nme in tiny]
    pw, poffs = _pack_rows([weights[nme] for nme in tiny])
    pg, _ = _pack_rows([grads[nme] for nme in tiny])
    pm, _ = _pack_rows([moms_m[nme] for nme in tiny])
    pv, _ = _pack_rows([moms_v[nme] for nme in tiny])
    pd, pnm, pnv = _adamw(pw, pg, pm, pv, "adamw_small")
    for nme, dl, nm, nv in zip(tiny, _unpack_rows(pd, poffs, shapes), _unpack_rows(pnm, poffs, shapes),
                               _unpack_rows(pnv, poffs, shapes)):
        delta[nme], new_m[nme], new_v[nme] = dl, nm, nv

    return (loss, grad_x, *[grads[nme].reshape(weights[nme].shape) for nme in names],
            *[delta[nme] for nme in names], *[new_m[nme] for nme in names], *[new_v[nme] for nme in names])
```

```python
import functools

import jax
import jax.numpy as jnp
from jax import lax
from jax.experimental import pallas as pl
from jax.experimental.pallas import tpu as pltpu

F32 = jnp.float32
BF16 = jnp.bfloat16
HI = lax.Precision.HIGHEST
MESH = pl.DeviceIdType.MESH

EPS = 1e-6
CHUNK = 64
D_FF = 2816
GDN_HEADS = 4
HEAD = 128
MLA_HEADS = 4
MLA_ROPE = 64
MLA_Q_LORA = 384
MLA_KV_LORA = 256
QK_PAD = 256
ATT_SCALE = (HEAD + MLA_ROPE) ** -0.5
N_PROJ = 3072

ADAM_LR, ADAM_B1, ADAM_B2, ADAM_EPS, ADAM_WD, ADAM_STEP = 0.001, 0.9, 0.999, 1e-08, 0.01, 10

LANES = 128
SUBLANES = 8
VMEM_LIMIT = 56 * 2 ** 20


def _params(sem=None):
    return pltpu.CompilerParams(dimension_semantics=sem, vmem_limit_bytes=VMEM_LIMIT)


def _pick(n, cap, align):
    best = None
    d = align
    while d <= min(n, cap):
        if n % d == 0:
            best = d
        d += align
    return best if best is not None else n


def _iota(shape, dim):
    return lax.broadcasted_iota(jnp.int32, shape, dim)


def _rowcall(fn, rows, params, out_rows, out_accs, *, tile, name):
    rows = [r if isinstance(r, tuple) else (r, r.shape[1], 0) for r in rows]
    s = rows[0][0].shape[-2]
    t = min(tile, s)
    n = s // t
    n_in = len(rows) + len(params)
    n_row_out = len(out_rows)

    in_specs = []
    for r in rows:
        if len(r) == 3:
            in_specs.append(pl.BlockSpec((t, r[1]), functools.partial(lambda i, b: (i, b), b=r[2])))
        else:
            in_specs.append(pl.BlockSpec((None, t, r[1]), functools.partial(lambda i, b, h: (h, i, b), b=r[2], h=r[3])))
    in_specs += [pl.BlockSpec(p.shape, lambda i: (0, 0)) for p in params]
    out_shape, out_specs = [], []
    for o in out_rows:
        if len(o) == 2:
            out_shape.append(jax.ShapeDtypeStruct((s, o[0]), o[1]))
            out_specs.append(pl.BlockSpec((t, o[0]), lambda i: (i, 0)))
        else:
            out_shape.append(jax.ShapeDtypeStruct((o[2], s, o[0]), o[1]))
            out_specs.append(pl.BlockSpec((o[2], t, o[0]), lambda i: (0, i, 0)))
    out_shape += [jax.ShapeDtypeStruct(shape, F32) for shape in out_accs]
    out_specs += [pl.BlockSpec(shape, lambda i: (0, 0)) for shape in out_accs]

    def body(*refs):
        ins = refs[:n_in]
        outs = refs[n_in:]
        i = pl.program_id(0)
        vals = [r[...] for r in ins]
        row_outs, acc_outs = fn(vals[:len(rows)], vals[len(rows):])
        for r, v in zip(outs[:n_row_out], row_outs):
            if isinstance(v, (list, tuple)):
                for hh, piece in enumerate(v):
                    r[hh] = piece.astype(r.dtype)
            else:
                r[...] = v.astype(r.dtype)
        if out_accs:
            @pl.when(i == 0)
            def _():
                for r in outs[n_row_out:]:
                    r[...] = jnp.zeros(r.shape, F32)
            for r, v in zip(outs[n_row_out:], acc_outs):
                r[...] += v

    res = pl.pallas_call(
        body, name=name, grid=(n,), in_specs=in_specs, out_specs=out_specs, out_shape=out_shape,
        compiler_params=_params(("arbitrary",) if out_accs else ("parallel",)),
    )(*[r[0] for r in rows], *params)
    return list(res)


MM_TILE_MN = 1536


def _mm(a, b, mode, out_dtype, name, hi=False, gather=()):
    if mode == "nn":
        (m, k), (_, n) = a.shape, b.shape
        dims = (((1,), (0,)), ((), ()))
    elif mode == "nt":
        (m, k), (n, _) = a.shape, b.shape
        dims = (((1,), (1,)), ((), ()))
    else:
        (k, m), (_, n) = a.shape, b.shape
        dims = (((0,), (0,)), ((), ()))
    tm = _pick(m, MM_TILE_MN if mode == "tn" else 1024, LANES if mode == "tn" else 16)
    tn = _pick(n, MM_TILE_MN, LANES)
    tk = _pick(k, 1024 if mode == "tn" else MM_TILE_MN, LANES)
    nk = k // tk
    if mode == "nn":
        a_spec = pl.BlockSpec((tm, tk), lambda i, j, kk: (i, kk))
        b_spec = pl.BlockSpec((tk, tn), lambda i, j, kk: (kk, j))
    elif mode == "nt":
        a_spec = pl.BlockSpec((tm, tk), lambda i, j, kk: (i, kk))
        b_spec = pl.BlockSpec((tn, tk), lambda i, j, kk: (j, kk))
    else:
        a_spec = pl.BlockSpec((tk, tm), lambda i, j, kk: (kk, i))
        b_spec = pl.BlockSpec((tk, tn), lambda i, j, kk: (kk, j))

    ng = len(gather)
    grid = (m // tm, n // tn, nk)
    steps = grid[0] * grid[1] * grid[2]

    def body(*refs):
        a_ref, b_ref = refs[:2]
        x_refs = refs[2:2 + ng]
        o_ref = refs[2 + ng]
        got_refs = refs[3 + ng:3 + 2 * ng]
        acc_ref = refs[3 + 2 * ng]
        kk = pl.program_id(2)
        if ng:
            sems = refs[4 + 2 * ng:]
            step = (pl.program_id(0) * grid[1] + pl.program_id(1)) * nk + kk

            def phase(ph):
                for slot in range(ng):
                    _gather_phase(ph, x_refs[slot], got_refs[slot], *sems, slot)

            pl.when(step == 0)(lambda: phase(0))
            pl.when(step == steps // 2)(lambda: phase(1))

        @pl.when(kk == 0)
        def _():
            acc_ref[...] = jnp.zeros(acc_ref.shape, F32)

        av, bv = a_ref[...], b_ref[...]
        if hi:
            acc_ref[...] += lax.dot_general(av, bv, dims, precision=HI, preferred_element_type=F32)
        else:
            acc_ref[...] += lax.dot_general(av.astype(BF16), bv.astype(BF16), dims,
                                            preferred_element_type=F32)

        @pl.when(kk == nk - 1)
        def _():
            o_ref[...] = acc_ref[...].astype(o_ref.dtype)

        if ng:
            pl.when(step == steps - 1)(lambda: phase(2))

    hbm = pl.BlockSpec(memory_space=pl.ANY)
    res = pl.pallas_call(
        body, name=name, grid=grid,
        in_specs=[a_spec, b_spec] + [hbm] * ng,
        out_specs=[pl.BlockSpec((tm, tn), lambda i, j, kk: (i, j))] + [hbm] * ng,
        out_shape=[jax.ShapeDtypeStruct((m, n), out_dtype)]
        + [jax.ShapeDtypeStruct((4,) + x.shape, x.dtype) for x in gather],
        scratch_shapes=[pltpu.VMEM((tm, tn), F32)] + (_gather_sems(ng) if ng else []),
        compiler_params=_params(("arbitrary",) * 3 if ng else ("parallel", "parallel", "arbitrary")),
    )(a, b, *gather)
    return res if ng else res[0]


def _rms(x, w=None, n=None):
    n = x.shape[-1] if n is None else n
    y = x * lax.rsqrt(jnp.sum(x * x, axis=-1, keepdims=True) * (1.0 / n) + EPS)
    return y if w is None else y * w


def _silu(x):
    return x * jax.nn.sigmoid(x)


def _softplus(x):
    return jnp.maximum(x, 0.0) + jnp.log1p(jnp.exp(-jnp.abs(x)))


def _split(x, widths):
    out, o = [], 0
    for w in widths:
        out.append(x[:, o:o + w])
        o += w
    return out


def _modulate(x, s, sh):
    return _rms(x) * (1.0 + s) + sh


def _rope_rot(x):
    r, c = _iota((LANES, LANES), 0), _iota((LANES, LANES), 1)
    half = MLA_ROPE // 2
    perm = (((r < half) & (c == r + half)) | ((r >= half) & (r < MLA_ROPE) & (c == r - half))).astype(F32)
    return jnp.dot(x, perm, precision=HI, preferred_element_type=F32)


def _rope(x, cos2, sin2):
    return x * cos2 + _rope_rot(x) * sin2


def _gdn_prep_core(qkv_parts, gab, a_log, dt_bias):
    act = [_silu(p) for p in qkv_parts]
    qs = [p * lax.rsqrt(jnp.sum(p * p, -1, keepdims=True) + EPS) * (HEAD ** -0.5) for p in act[:4]]
    ks = [p * lax.rsqrt(jnp.sum(p * p, -1, keepdims=True) + EPS) for p in act[4:8]]
    lane = _iota(gab.shape, 1)
    g = -jnp.exp(a_log) * _softplus(gab + dt_bias)
    beta = jax.nn.sigmoid(gab)
    gb = jnp.where(lane < GDN_HEADS, g, jnp.where(lane < 2 * GDN_HEADS, beta, 0.0))
    return (jnp.concatenate(qs, 1), jnp.concatenate(ks, 1), jnp.concatenate(act[8:], 1), gb)


def _mla_prep_core(cq, ckv, kr, cos2, sin2, wq, wkv, wkr):
    cqn = _rms(cq, wq)
    ckvn = _rms(ckv, wkv)
    k_rope = _rope(_rms(kr, wkr, MLA_ROPE), cos2, sin2)
    return cqn, ckvn, k_rope


def _qk_prep_core(qn_parts, qr_parts, kn_parts, v_parts, k_rope, cos2, sin2, wqn, wqr, wkn):
    qs, ks = [], []
    for h in range(MLA_HEADS):
        qn = _rms(qn_parts[h], wqn) * ATT_SCALE
        qr = _rope(_rms(qr_parts[h], wqr, MLA_ROPE), cos2, sin2) * ATT_SCALE
        qs.append(jnp.concatenate([qn, qr], 1))
        ks.append(jnp.concatenate([_rms(kn_parts[h], wkn), k_rope], 1))
    return qs, ks, list(v_parts)


def _mix_post_core(o_parts, gz_parts, ob_parts, wn, won):
    oa = [_rms(o, wn) * _silu(z) for o, z in zip(o_parts, gz_parts)]
    ob = [_rms(o, won) for o in ob_parts]
    return jnp.concatenate(oa + ob, 1)


CONV_K = 4
HALO = SUBLANES


def _conv_fwd(proj, w8, name):
    s = proj.shape[0]
    c = w8.shape[1]
    t = min(256, s)
    n = s // t
    hb = t // HALO

    def body(x_ref, prev_ref, w_ref, o_ref, buf):
        i = pl.program_id(0)
        buf[pl.ds(0, HALO), :] = jnp.where(i > 0, prev_ref[...], 0.0)
        buf[pl.ds(HALO, t), :] = x_ref[...]
        acc = jnp.zeros((t, c), F32)
        for k in range(CONV_K):
            acc = acc + w_ref[k:k + 1, :] * buf[pl.ds(HALO - (CONV_K - 1) + k, t), :]
        o_ref[...] = acc

    return pl.pallas_call(
        body, name=name, grid=(n,),
        in_specs=[pl.BlockSpec((t, c), lambda i: (i, 0)),
                  pl.BlockSpec((HALO, c), lambda i: (jnp.maximum(i * hb - 1, 0), 0)),
                  pl.BlockSpec(w8.shape, lambda i: (0, 0))],
        out_specs=pl.BlockSpec((t, c), lambda i: (i, 0)),
        out_shape=jax.ShapeDtypeStruct((s, c), F32),
        scratch_shapes=[pltpu.VMEM((t + HALO, c), F32)],
        compiler_params=_params(("parallel",)),
    )(proj, proj, w8)


def _conv_bwd(proj, dy, w8, name):
    s = proj.shape[0]
    c = w8.shape[1]
    t = min(256, s)
    n = s // t
    hb = t // HALO

    def body(x_ref, prev_ref, dy_ref, next_ref, w_ref, dx_ref, dw_ref, bufx, bufd):
        i = pl.program_id(0)
        bufx[pl.ds(0, HALO), :] = jnp.where(i > 0, prev_ref[...], 0.0)
        bufx[pl.ds(HALO, t), :] = x_ref[...]
        bufd[pl.ds(0, t), :] = dy_ref[...]
        bufd[pl.ds(t, HALO), :] = jnp.where(i < n - 1, next_ref[...], 0.0)

        @pl.when(i == 0)
        def _():
            dw_ref[...] = jnp.zeros(dw_ref.shape, F32)

        dyv = dy_ref[...]
        acc = jnp.zeros((t, c), F32)
        for k in range(CONV_K):
            acc = acc + w_ref[k:k + 1, :] * bufd[pl.ds(CONV_K - 1 - k, t), :]
            dw_ref[k:k + 1, :] += jnp.sum(dyv * bufx[pl.ds(HALO - (CONV_K - 1) + k, t), :], axis=0, keepdims=True)
        dx_ref[...] = acc

    return pl.pallas_call(
        body, name=name, grid=(n,),
        in_specs=[pl.BlockSpec((t, c), lambda i: (i, 0)),
                  pl.BlockSpec((HALO, c), lambda i: (jnp.maximum(i * hb - 1, 0), 0)),
                  pl.BlockSpec((t, c), lambda i: (i, 0)),
                  pl.BlockSpec((HALO, c), lambda i: (jnp.minimum((i + 1) * hb, s // HALO - 1), 0)),
                  pl.BlockSpec(w8.shape, lambda i: (0, 0))],
        out_specs=[pl.BlockSpec((t, c), lambda i: (i, 0)), pl.BlockSpec(w8.shape, lambda i: (0, 0))],
        out_shape=[jax.ShapeDtypeStruct((s, c), F32), jax.ShapeDtypeStruct(w8.shape, F32)],
        scratch_shapes=[pltpu.VMEM((t + HALO, c), F32), pltpu.VMEM((t + HALO, c), F32)],
        compiler_params=_params(("arbitrary",)),
    )(proj, proj, dy, dy, w8)


def _dot(a, b):
    return jnp.dot(a, b, precision=HI, preferred_element_type=F32)


def _dot_nt(a, b):
    return lax.dot_general(a, b, (((1,), (1,)), ((), ())), precision=HI, preferred_element_type=F32)


def _dot_tn(a, b):
    return lax.dot_general(a, b, (((0,), (0,)), ((), ())), precision=HI, preferred_element_type=F32)


_B_NN = (((2,), (1,)), ((0,), (0,)))
_B_NT = (((2,), (2,)), ((0,), (0,)))
_B_TN = (((1,), (1,)), ((0,), (0,)))


def _bdot_hi(a, b):
    return lax.dot_general(a, b, _B_NN, precision=HI, preferred_element_type=F32)


def _dot3(a, b, dims):
    return lax.dot_general(a, b, dims, precision=lax.Precision.HIGH, preferred_element_type=F32)


class _Dots:
    def __init__(self, diff):
        nn = lambda a, b: _dot3(a, b, _B_NN)
        nt = lambda a, b: _dot3(a, b, _B_NT)
        tn = lambda a, b: _dot3(a, b, _B_TN)
        if diff:
            def with_rule(f, bwd):
                g = jax.custom_vjp(f)
                g.defvjp(lambda a, b: (f(a, b), (a, b)), bwd)
                return g
            self.nn = with_rule(nn, lambda r, ct: (nt(ct, r[1]), tn(r[0], ct)))
            self.nt = with_rule(nt, lambda r, ct: (nn(ct, r[1]), tn(ct, r[0])))
            self.tn = with_rule(tn, lambda r, ct: (nt(r[1], ct), nn(r[0], ct)))
        else:
            self.nn, self.nt, self.tn = nn, nt, tn


def _unit_lower_inverse(a, dots):
    c = a.shape[-1]
    ri, ci = _iota(a.shape, 1), _iota(a.shape, 2)
    inner = (ri // 2) == (ci // 2)
    t = (ri == ci).astype(F32) - jnp.where(inner, a, 0.0)
    blk = 4
    while blk <= c:
        outer = (ri // blk) == (ci // blk)
        low = jnp.where(outer & jnp.logical_not(inner), a, 0.0)
        t = t - dots.nn(dots.nn(t, low), t)
        inner = outer
        blk *= 2
    return t


def _stack(xs):
    return jnp.concatenate([x[None] for x in xs], axis=0)


def _gdn_local(dots, q, k, v, gbs):
    b, c, _ = q.shape
    gcols, bcols = [], []
    for gb in gbs:
        lane = _iota(gb.shape, 1)
        for h in range(GDN_HEADS):
            gcols.append(jnp.sum(jnp.where(lane == h, gb, 0.0), axis=1, keepdims=True))
            bcols.append(jnp.sum(jnp.where(lane == GDN_HEADS + h, gb, 0.0), axis=1, keepdims=True))
    gcol, bcol = _stack(gcols), _stack(bcols)
    ri, ci = _iota((b, c, c), 1), _iota((b, c, c), 2)
    incl = ri >= ci
    tril = incl.astype(F32)
    g_cc = _bdot_hi(tril, jnp.broadcast_to(gcol, (b, c, c)))
    g_row = _bdot_hi(jnp.ones((b, c, c), F32), jnp.where(ri == ci, g_cc, 0.0))
    g_cl = _bdot_hi(tril, jnp.broadcast_to(gcol, (b, c, HEAD)))
    g_last = jnp.sum(jnp.broadcast_to(gcol, (b, c, HEAD)), axis=1, keepdims=True)
    decay = jnp.where(incl, jnp.exp(jnp.where(incl, g_cc - g_row, 0.0)), 0.0)
    kk = dots.nt(k, k)
    minv = _unit_lower_inverse(jnp.where(ri > ci, bcol * kk * decay, 0.0), dots)
    e_g = jnp.exp(g_cl)
    u = dots.nn(minv, v * bcol)
    wk = dots.nn(minv, k * (bcol * e_g))
    qk = dots.nt(q, k) * decay
    return u, wk, q * e_g, k * jnp.exp(g_last - g_cl), qk, jnp.exp(g_last)


def _gdn_scan(dots, states, u, wk, qd, kd, qk, gl_tile):
    lane, row = _iota(gl_tile.shape, 1), _iota(gl_tile.shape, 0)
    gl = _stack([
        jnp.sum(jnp.sum(jnp.where((lane == h) & (row == 0), gl_tile, 0.0), axis=1, keepdims=True),
                axis=0, keepdims=True) for h in range(GDN_HEADS)])
    v_new = u - dots.nn(wk, states)
    o = dots.nn(qd, states) + dots.nn(qk, v_new)
    return states * gl + dots.tn(kd, v_new), o


def _heads(x):
    return jnp.stack(_split(x, HW4))


GDN_W = GDN_HEADS * HEAD
HW4 = [HEAD] * GDN_HEADS
LOCAL_CHUNKS = 4
_CHUNK_ROWS = [pl.ds(cc * CHUNK, CHUNK) for cc in range(LOCAL_CHUNKS)]


def _chunk_heads(ref):
    return jnp.concatenate([_heads(ref[rows, :]) for rows in _CHUNK_ROWS], 0)


def _gdn_local_fwd(q, k, v, gb, name):
    s = q.shape[0]
    t = LOCAL_CHUNKS * CHUNK

    def body(q_ref, k_ref, v_ref, gb_ref, u_ref, wk_ref, qd_ref, kd_ref, qk_ref, gl_ref):
        u, wk, qd, kd, qk, gl = _gdn_local(_Dots(False), _chunk_heads(q_ref), _chunk_heads(k_ref),
                                           _chunk_heads(v_ref), [gb_ref[rows, :] for rows in _CHUNK_ROWS])
        lane = _iota((CHUNK, LANES), 1)
        for cc, rows in enumerate(_CHUNK_ROWS):
            gl_tile = jnp.zeros((CHUNK, LANES), F32)
            for h in range(GDN_HEADS):
                b, cols = cc * GDN_HEADS + h, pl.ds(h * HEAD, HEAD)
                u_ref[rows, cols] = u[b]
                wk_ref[rows, cols] = wk[b]
                qd_ref[rows, cols] = qd[b]
                kd_ref[rows, cols] = kd[b]
                qk_ref[h, rows, :] = qk[b]
                gl_tile = gl_tile + jnp.where(lane == h, gl[b], 0.0)
            gl_ref[rows, :] = gl_tile

    row = pl.BlockSpec((t, GDN_W), lambda i: (i, 0))
    lane = pl.BlockSpec((t, LANES), lambda i: (i, 0))
    qks = pl.BlockSpec((GDN_HEADS, t, CHUNK), lambda i: (0, i, 0))
    return pl.pallas_call(
        body, name=name, grid=(s // t,),
        in_specs=[row, row, row, lane],
        out_specs=[row, row, row, row, qks, lane],
        out_shape=[jax.ShapeDtypeStruct((s, GDN_W), F32)] * 4
        + [jax.ShapeDtypeStruct((GDN_HEADS, s, CHUNK), F32), jax.ShapeDtypeStruct((s, LANES), F32)],
        compiler_params=_params(("parallel",)),
    )(q, k, v, gb)


def _gdn_local_bwd(q, k, v, gb, du, dwk, dqd, dkd, dqk, dgl, name):
    s = q.shape[0]
    t = LOCAL_CHUNKS * CHUNK

    def body(q_ref, k_ref, v_ref, gb_ref, du_ref, dwk_ref, dqd_ref, dkd_ref, dqk_ref, dgl_ref,
             dq_ref, dk_ref, dv_ref, dgb_ref):
        _, vjp = jax.vjp(functools.partial(_gdn_local, _Dots(False)), _chunk_heads(q_ref), _chunk_heads(k_ref),
                         _chunk_heads(v_ref), [gb_ref[rows, :] for rows in _CHUNK_ROWS])
        lane = _iota((CHUNK, LANES), 1)
        dqk = jnp.stack([dqk_ref[h, rows, :] for rows in _CHUNK_ROWS for h in range(GDN_HEADS)])
        dgl = jnp.stack([jnp.sum(jnp.where(lane == h, dgl_ref[rows, :], 0.0), axis=0, keepdims=True)
                         for rows in _CHUNK_ROWS for h in range(GDN_HEADS)])
        d_q, d_k, d_v, d_gbs = vjp((_chunk_heads(du_ref), _chunk_heads(dwk_ref), _chunk_heads(dqd_ref),
                                    _chunk_heads(dkd_ref), dqk, dgl))
        for cc, rows in enumerate(_CHUNK_ROWS):
            for h in range(GDN_HEADS):
                b, cols = cc * GDN_HEADS + h, pl.ds(h * HEAD, HEAD)
                dq_ref[rows, cols] = d_q[b]
                dk_ref[rows, cols] = d_k[b]
                dv_ref[rows, cols] = d_v[b]
            dgb_ref[rows, :] = d_gbs[cc]

    row = pl.BlockSpec((t, GDN_W), lambda i: (i, 0))
    lane = pl.BlockSpec((t, LANES), lambda i: (i, 0))
    qks = pl.BlockSpec((GDN_HEADS, t, CHUNK), lambda i: (0, i, 0))
    return pl.pallas_call(
        body, name=name, grid=(s // t,),
        in_specs=[row, row, row, lane, row, row, row, row, qks, lane],
        out_specs=[row, row, row, lane],
        out_shape=[jax.ShapeDtypeStruct((s, GDN_W), F32)] * 3 + [jax.ShapeDtypeStruct((s, LANES), F32)],
        compiler_params=_params(("parallel",)),
    )(q, k, v, gb, du, dwk, dqd, dkd, dqk, dgl)


def _gdn_scan_fwd(u, wk, qd, kd, qk, gl, name):
    s = u.shape[0]
    nc = s // CHUNK

    def body(u_ref, wk_ref, qd_ref, kd_ref, qk_ref, gl_ref, o_ref, st_ref, state):
        i = pl.program_id(0)

        @pl.when(i == 0)
        def _():
            state[...] = jnp.zeros(state.shape, F32)

        st_ref[...] = state[...]
        new_states, o = _gdn_scan(_Dots(False), state[...], _heads(u_ref[...]), _heads(wk_ref[...]),
                                  _heads(qd_ref[...]), _heads(kd_ref[...]), qk_ref[...], gl_ref[...])
        state[...] = new_states
        o_ref[...] = jnp.concatenate([o[h] for h in range(GDN_HEADS)], 1)

    row = pl.BlockSpec((CHUNK, GDN_W), lambda i: (i, 0))
    return pl.pallas_call(
        body, name=name, grid=(nc,),
        in_specs=[row, row, row, row, pl.BlockSpec((GDN_HEADS, CHUNK, CHUNK), lambda i: (0, i, 0)),
                  pl.BlockSpec((CHUNK, LANES), lambda i: (i, 0))],
        out_specs=[row, pl.BlockSpec((None, GDN_HEADS, HEAD, HEAD), lambda i: (i, 0, 0, 0))],
        out_shape=[jax.ShapeDtypeStruct((s, GDN_W), F32),
                   jax.ShapeDtypeStruct((nc, GDN_HEADS, HEAD, HEAD), F32)],
        scratch_shapes=[pltpu.VMEM((GDN_HEADS, HEAD, HEAD), F32)],
        compiler_params=_params(("arbitrary",)),
    )(u, wk, qd, kd, qk, gl)


def _gdn_scan_bwd(u, wk, qd, kd, qk, gl, st, do, name):
    s = u.shape[0]
    nc = s // CHUNK

    def body(u_ref, wk_ref, qd_ref, kd_ref, qk_ref, gl_ref, st_ref, do_ref,
             du_ref, dwk_ref, dqd_ref, dkd_ref, dqk_ref, dgl_ref, dstate):
        i = pl.program_id(0)

        @pl.when(i == 0)
        def _():
            dstate[...] = jnp.zeros(dstate.shape, F32)

        _, vjp = jax.vjp(functools.partial(_gdn_scan, _Dots(False)), st_ref[...], _heads(u_ref[...]),
                         _heads(wk_ref[...]), _heads(qd_ref[...]), _heads(kd_ref[...]), qk_ref[...], gl_ref[...])
        d_states, d_u, d_wk, d_qd, d_kd, d_qk, d_gl = vjp((dstate[...], _heads(do_ref[...])))
        dstate[...] = d_states
        dqk_ref[...] = d_qk
        unheads = lambda x: jnp.concatenate([x[h] for h in range(GDN_HEADS)], 1)
        du_ref[...] = unheads(d_u)
        dwk_ref[...] = unheads(d_wk)
        dqd_ref[...] = unheads(d_qd)
        dkd_ref[...] = unheads(d_kd)
        dgl_ref[...] = d_gl

    rev = lambda i: (nc - 1 - i, 0)
    row = pl.BlockSpec((CHUNK, GDN_W), rev)
    lane = pl.BlockSpec((CHUNK, LANES), rev)
    qks = pl.BlockSpec((GDN_HEADS, CHUNK, CHUNK), lambda i: (0, nc - 1 - i, 0))
    return pl.pallas_call(
        body, name=name, grid=(nc,),
        in_specs=[row, row, row, row, qks, lane,
                  pl.BlockSpec((None, GDN_HEADS, HEAD, HEAD), lambda i: (nc - 1 - i, 0, 0, 0)), row],
        out_specs=[row, row, row, row, qks, lane],
        out_shape=[jax.ShapeDtypeStruct((s, GDN_W), F32)] * 4
        + [jax.ShapeDtypeStruct((GDN_HEADS, s, CHUNK), F32), jax.ShapeDtypeStruct((s, LANES), F32)],
        scratch_shapes=[pltpu.VMEM((GDN_HEADS, HEAD, HEAD), F32)],
        compiler_params=_params(("arbitrary",)),
    )(u, wk, qd, kd, qk, gl, st, do)


def _chunk_mask(i, j, t):
    r = i * t + _iota((t, t), 0)
    c = j * t + _iota((t, t), 1)
    return (r // CHUNK) >= (c // CHUNK)


ATT_TILE = 1024
ATT_Q_TILES = 1
ATT_BWD_TILE = 1024


def _attn_fwd(q, k, v, name):
    nh, s = MLA_HEADS, q.shape[0]
    tk = min(ATT_TILE, s)
    tq = min(ATT_Q_TILES * tk, s)
    qk = tq // tk
    nq, n = s // tq, s // tk
    nt = (((1,), (1,)), ((), ()))

    def body(q_ref, k_ref, v_ref, o_ref, lse_ref, m_sc, l_sc, acc_sc):
        i, j = pl.program_id(1), pl.program_id(2)

        @pl.when(j == 0)
        def _():
            m_sc[...] = jnp.full(m_sc.shape, -jnp.inf, F32)
            l_sc[...] = jnp.zeros(l_sc.shape, F32)
            acc_sc[...] = jnp.zeros(acc_sc.shape, F32)

        def step(masked):
            sc = lax.dot_general(q_ref[...], k_ref[...], nt, preferred_element_type=F32)
            if masked:
                r = i * tq + _iota((tq, tk), 0)
                c = j * tk + _iota((tq, tk), 1)
                sc = jnp.where((r // CHUNK) >= (c // CHUNK), sc, -jnp.inf)
            m_prev = m_sc[:, :1]
            m_new = jnp.maximum(m_prev, jnp.max(sc, axis=1, keepdims=True))
            alpha = jnp.exp(m_prev - m_new)
            p = jnp.exp(sc - m_new)
            l_sc[...] = jnp.broadcast_to(alpha * l_sc[:, :1] + jnp.sum(p, axis=1, keepdims=True), l_sc.shape)
            acc_sc[...] = alpha * acc_sc[...] + jnp.dot(p.astype(BF16), v_ref[...], preferred_element_type=F32)
            m_sc[...] = jnp.broadcast_to(m_new, m_sc.shape)

        pl.when(j < i * qk)(lambda: step(False))
        pl.when(j // qk == i)(lambda: step(True))

        @pl.when(j == n - 1)
        def _():
            o_ref[...] = acc_sc[...] / l_sc[:, :1]
            lse_ref[...] = m_sc[...] + jnp.log(l_sc[...])

    qrow = lambda h, i, j: (i, h)
    krow = lambda h, i, j: (jnp.minimum(j, (i + 1) * qk - 1), h)
    return pl.pallas_call(
        body, name=name, grid=(nh, nq, n),
        in_specs=[pl.BlockSpec((tq, QK_PAD), qrow), pl.BlockSpec((tk, QK_PAD), krow),
                  pl.BlockSpec((tk, HEAD), krow)],
        out_specs=[pl.BlockSpec((tq, HEAD), qrow), pl.BlockSpec((None, tq, LANES), lambda h, i, j: (h, i, 0))],
        out_shape=[jax.ShapeDtypeStruct((s, nh * HEAD), F32), jax.ShapeDtypeStruct((nh, s, LANES), F32)],
        scratch_shapes=[pltpu.VMEM((tq, LANES), F32), pltpu.VMEM((tq, LANES), F32), pltpu.VMEM((tq, HEAD), F32)],
        compiler_params=_params(("parallel", "parallel", "arbitrary")),
    )(q, k, v)


def _attn_bwd(q, k, v, o, do, lse, name):
    nh, s = MLA_HEADS, q.shape[0]
    t = min(ATT_BWD_TILE, s)
    n = s // t
    tn = (((0,), (0,)), ((), ()))
    nt = (((1,), (1,)), ((), ()))

    def body(q_ref, k_ref, v_ref, o_ref, do_ref, lse_ref, dq_ref, dk_ref, dv_ref, dk_acc, dv_acc, dq_acc):
        j, i = pl.program_id(1), pl.program_id(2)

        @pl.when(i + j == 0)
        def _():
            dq_acc[...] = jnp.zeros(dq_acc.shape, F32)

        @pl.when(i == 0)
        def _():
            dk_acc[...] = jnp.zeros(dk_acc.shape, F32)
            dv_acc[...] = jnp.zeros(dv_acc.shape, F32)

        def step(masked):
            qv, kv, do = q_ref[...], k_ref[...], do_ref[...]
            sc = lax.dot_general(qv, kv, nt, preferred_element_type=F32)
            p = jnp.exp(sc - lse_ref[:, :1])
            if masked:
                p = jnp.where(_chunk_mask(i, j, t), p, 0.0)
            dob = do.astype(BF16)
            dp = lax.dot_general(dob, v_ref[...], nt, preferred_element_type=F32)
            ds = (p * (dp - jnp.sum(do * o_ref[...], axis=1, keepdims=True))).astype(BF16)
            dv_acc[...] += lax.dot_general(p.astype(BF16), dob, tn, preferred_element_type=F32)
            dk_acc[...] += lax.dot_general(ds, qv, tn, preferred_element_type=F32)
            rows = pl.ds(pl.multiple_of(i * t, t), t)
            dq_acc[rows, :] += jnp.dot(ds, kv, preferred_element_type=F32)

        pl.when(i > j)(lambda: step(False))
        pl.when(i == j)(lambda: step(True))

        @pl.when(i == n - 1)
        def _():
            dk_ref[...] = dk_acc[...]
            dv_ref[...] = dv_acc[...]

        @pl.when(i + j == 2 * (n - 1))
        def _():
            dq_ref[...] = dq_acc[...]

    qrow = lambda h, j, i: (jnp.maximum(i, j), h)
    krow = lambda h, j, i: (j, h)
    return pl.pallas_call(
        body, name=name, grid=(nh, n, n),
        in_specs=[pl.BlockSpec((t, QK_PAD), qrow), pl.BlockSpec((t, QK_PAD), krow), pl.BlockSpec((t, HEAD), krow),
                  pl.BlockSpec((t, HEAD), qrow), pl.BlockSpec((t, HEAD), qrow),
                  pl.BlockSpec((None, t, LANES), lambda h, j, i: (h, jnp.maximum(i, j), 0))],
        out_specs=[pl.BlockSpec((s, QK_PAD), lambda h, j, i: (0, h)),
                   pl.BlockSpec((t, QK_PAD), krow), pl.BlockSpec((t, HEAD), krow)],
        out_shape=[jax.ShapeDtypeStruct((s, nh * QK_PAD), F32), jax.ShapeDtypeStruct((s, nh * QK_PAD), F32),
                   jax.ShapeDtypeStruct((s, nh * HEAD), F32)],
        scratch_shapes=[pltpu.VMEM((t, QK_PAD), F32), pltpu.VMEM((t, HEAD), F32), pltpu.VMEM((s, QK_PAD), F32)],
        compiler_params=_params(("arbitrary", "arbitrary", "arbitrary")),
    )(q, k, v, o, do, lse)


def _place():
    return lax.axis_index("x"), lax.axis_index("y"), lax.axis_index("c")


def _allgather8(x, name):
    r, c = x.shape

    def body(x_ref, out_ref, send_sems, recv_sems, local_sem):
        mx, my, mc = _place()
        me = 4 * mx + 2 * my + mc
        mine = pltpu.make_async_copy(x_ref, out_ref.at[me], local_sem)
        mine.start()
        copies = []
        for d in range(1, 8):
            px = 1 - mx if d & 4 else mx
            py = 1 - my if d & 2 else my
            pc = 1 - mc if d & 1 else mc
            cp = pltpu.make_async_remote_copy(
                src_ref=x_ref, dst_ref=out_ref.at[me], send_sem=send_sems.at[d - 1], recv_sem=recv_sems.at[d - 1],
                device_id=(px, py, pc), device_id_type=MESH)
            cp.start()
            copies.append(cp)
        for cp in copies:
            cp.wait()
        mine.wait()

    return pl.pallas_call(
        body, name=name,
        out_shape=jax.ShapeDtypeStruct((8, r, c), x.dtype),
        in_specs=[pl.BlockSpec(memory_space=pltpu.VMEM)],
        out_specs=pl.BlockSpec(memory_space=pltpu.VMEM),
        scratch_shapes=[pltpu.SemaphoreType.DMA((7,)), pltpu.SemaphoreType.DMA((7,)), pltpu.SemaphoreType.DMA],
        compiler_params=pltpu.CompilerParams(vmem_limit_bytes=VMEM_LIMIT),
    )(x)


def _allgather_chips(x, name):
    r, c = x.shape

    def body(x_ref, out_ref, send_sems, recv_sems, local_sems):
        for phase in range(3):
            _gather_phase(phase, x_ref, out_ref, send_sems, recv_sems, local_sems, 0)

    return pl.pallas_call(
        body, name=name,
        out_shape=jax.ShapeDtypeStruct((4, r, c), x.dtype),
        in_specs=[pl.BlockSpec(memory_space=pltpu.VMEM)],
        out_specs=pl.BlockSpec(memory_space=pltpu.VMEM),
        scratch_shapes=_gather_sems(1),
        compiler_params=pltpu.CompilerParams(vmem_limit_bytes=VMEM_LIMIT),
    )(x)


GATHER_COPIES = 6


def _gather_sems(n):
    return [pltpu.SemaphoreType.DMA((GATHER_COPIES * n,)), pltpu.SemaphoreType.DMA((GATHER_COPIES * n,)),
            pltpu.SemaphoreType.DMA((n,))]


def _gather_phase(phase, x_ref, out_ref, send_sems, recv_sems, local_sems, slot):
    mx, my, mc = _place()
    j = 2 * mx + my
    rh = x_ref.shape[0] // 2
    base = GATHER_COPIES * slot
    chips = [(1 - mx, my), (mx, 1 - my), (1 - mx, 1 - my)]
    sibling = (mx, my, 1 - mc)

    def half(jj, hc):
        return out_ref.at[jj, pl.ds(hc * rh, rh), :]

    def over_ici(kk, block):
        px, py = chips[kk]
        return pltpu.make_async_remote_copy(
            src_ref=x_ref.at[pl.ds(mc * rh, rh), :], dst_ref=half(block, mc), send_sem=send_sems.at[base + kk],
            recv_sem=recv_sems.at[base + kk], device_id=(px, py, mc), device_id_type=MESH)

    def to_sibling(kk, hc):
        px, py = chips[kk]
        blk = half(2 * px + py, hc)
        return pltpu.make_async_remote_copy(
            src_ref=blk, dst_ref=blk, send_sem=send_sems.at[base + 3 + kk], recv_sem=recv_sems.at[base + 3 + kk],
            device_id=sibling, device_id_type=MESH)

    mine = pltpu.make_async_copy(x_ref, out_ref.at[j], local_sems.at[slot])
    if phase == 0:
        mine.start()
        for kk in range(3):
            over_ici(kk, j).start()
    elif phase == 1:
        for kk, (px, py) in enumerate(chips):
            over_ici(kk, 2 * px + py).wait_recv()
            to_sibling(kk, mc).start()
    else:
        for kk in range(3):
            to_sibling(kk, 1 - mc).wait_recv()
        for kk in range(3):
            over_ici(kk, j).wait_send()
            to_sibling(kk, mc).wait_send()
        mine.wait()


RS_ROWS = 32


def _reduce_scatter_chips(g, name):
    _, r, c = g.shape
    rh = r // 2
    steps = rh // RS_ROWS

    def body(g_ref, out_ref, sib_ref, part_ref, got_ref, send_sems, recv_sems):
        mx, my, mc = _place()
        j = 2 * mx + my
        sibling = (mx, my, 1 - mc)
        chips = [(1 - mx, my), (mx, 1 - my), (1 - mx, 1 - my)]

        to_sib = pltpu.make_async_remote_copy(
            src_ref=g_ref.at[:, pl.ds((1 - mc) * rh, rh), :], dst_ref=sib_ref,
            send_sem=send_sems.at[0], recv_sem=recv_sems.at[0], device_id=sibling, device_id_type=MESH)
        to_sib.start()
        to_sib.wait()

        def add_sibling(step, carry):
            rows = pl.ds(pl.multiple_of(step * RS_ROWS, RS_ROWS), RS_ROWS)
            mine = g_ref[:, pl.ds(pl.multiple_of(mc * rh + step * RS_ROWS, RS_ROWS), RS_ROWS), :]
            part_ref[:, rows, :] = mine.astype(F32) + sib_ref[:, rows, :].astype(F32)
            return carry

        lax.fori_loop(0, steps, add_sibling, 0)

        def to_bf16(step, carry):
            rows = pl.ds(pl.multiple_of(step * RS_ROWS, RS_ROWS), RS_ROWS)
            sib_ref[:, rows, :] = part_ref[:, rows, :].astype(BF16)
            return carry

        lax.fori_loop(0, steps, to_bf16, 0)

        sends = []
        for kk, (px, py) in enumerate(chips):
            cp = pltpu.make_async_remote_copy(
                src_ref=sib_ref.at[2 * px + py], dst_ref=got_ref.at[kk],
                send_sem=send_sems.at[1 + kk], recv_sem=recv_sems.at[1 + kk],
                device_id=(px, py, mc), device_id_type=MESH)
            cp.start()
            sends.append(cp)
        for cp in sends:
            cp.wait()

        def total(step, carry):
            rows = pl.ds(pl.multiple_of(step * RS_ROWS, RS_ROWS), RS_ROWS)
            acc = part_ref[j, rows, :]
            for kk in range(3):
                acc = acc + got_ref[kk, rows, :].astype(F32)
            out_ref[pl.ds(pl.multiple_of(mc * rh + step * RS_ROWS, RS_ROWS), RS_ROWS), :] = acc
            return carry

        lax.fori_loop(0, steps, total, 0)

        done = pltpu.make_async_remote_copy(
            src_ref=out_ref.at[pl.ds(mc * rh, rh), :], dst_ref=out_ref.at[pl.ds(mc * rh, rh), :],
            send_sem=send_sems.at[4], recv_sem=recv_sems.at[4], device_id=sibling, device_id_type=MESH)
        done.start()
        done.wait_send()
        pltpu.make_async_remote_copy(
            src_ref=out_ref.at[pl.ds((1 - mc) * rh, rh), :], dst_ref=out_ref.at[pl.ds((1 - mc) * rh, rh), :],
            send_sem=send_sems.at[4], recv_sem=recv_sems.at[4], device_id=sibling, device_id_type=MESH).wait_recv()

    return pl.pallas_call(
        body, name=name,
        out_shape=jax.ShapeDtypeStruct((r, c), F32),
        in_specs=[pl.BlockSpec(memory_space=pltpu.VMEM)],
        out_specs=pl.BlockSpec(memory_space=pltpu.VMEM),
        scratch_shapes=[pltpu.VMEM((4, rh, c), BF16), pltpu.VMEM((4, rh, c), F32), pltpu.VMEM((3, rh, c), BF16),
                        pltpu.SemaphoreType.DMA((5,)), pltpu.SemaphoreType.DMA((5,))],
        compiler_params=pltpu.CompilerParams(vmem_limit_bytes=VMEM_LIMIT),
    )(g)


def _sum8(x, name):
    _, r, c = x.shape

    def body(x_ref, o_ref):
        acc = x_ref[0]
        for d in range(1, 8):
            acc = acc + x_ref[d]
        o_ref[...] = acc

    return pl.pallas_call(
        body, name=name, out_shape=jax.ShapeDtypeStruct((r, c), F32),
        in_specs=[pl.BlockSpec(memory_space=pltpu.VMEM)], out_specs=pl.BlockSpec(memory_space=pltpu.VMEM),
    )(x)


def _adamw(w, g, m, v, name):
    r, c = w.shape
    t = _pick(r, 256, SUBLANES)
    spec = pl.BlockSpec((t, c), lambda i: (i, 0))

    def body(w_ref, g_ref, m_ref, v_ref, d_ref, nm_ref, nv_ref):
        gv = g_ref[...]
        m_new = ADAM_B1 * m_ref[...] + (1.0 - ADAM_B1) * gv
        v_new = ADAM_B2 * v_ref[...] + (1.0 - ADAM_B2) * (gv * gv)
        m_hat = m_new / (1.0 - ADAM_B1 ** ADAM_STEP)
        v_hat = v_new / (1.0 - ADAM_B2 ** ADAM_STEP)
        d_ref[...] = -ADAM_LR * (m_hat / (jnp.sqrt(v_hat) + ADAM_EPS) + ADAM_WD * w_ref[...])
        nm_ref[...] = m_new
        nv_ref[...] = v_new

    return pl.pallas_call(
        body, name=name, grid=(r // t,), in_specs=[spec] * 4, out_specs=[spec] * 3,
        out_shape=[jax.ShapeDtypeStruct((r, c), F32)] * 3, compiler_params=_params(("parallel",)),
    )(w, g, m, v)


def _pack_rows(parts):
    rows, offs, o = [], [], 0
    for p in parts:
        f = p.reshape(-1)
        n = -(-f.shape[0] // (LANES * SUBLANES)) * SUBLANES
        rows.append(jnp.pad(f, (0, n * LANES - f.shape[0])).reshape(n, LANES))
        offs.append((o, n))
        o += n
    return jnp.concatenate(rows, 0), offs


def _unpack_rows(packed, offs, shapes):
    out = []
    for (o, n), shp in zip(offs, shapes):
        size = 1
        for d in shp:
            size *= d
        out.append(packed[o:o + n].reshape(-1)[:size].reshape(shp))
    return out


def _ffn_fwd(x, s, sh, g, w_in, w_out, tag, gather=()):
    (h,) = _rowcall(lambda r, p: ([_modulate(r[0], p[0], p[1])], []), [x], [s, sh], [(x.shape[1], BF16)], [],
                    tile=512, name=tag + "_mod")
    res = _mm(h, w_in, "nn", BF16, tag + "_in", gather=gather)
    gu, got = (res[0], res[1:]) if gather else (res, [])
    (act,) = _rowcall(lambda r, p: ([_silu(r[0].astype(F32)) * r[1].astype(F32)], []),
                      [(gu, D_FF, 0), (gu, D_FF, 1)], [], [(D_FF, BF16)], [], tile=256, name=tag + "_act")
    f = _mm(act, w_out, "nn", F32, tag + "_out")
    (y,) = _rowcall(lambda r, p: ([r[0] + 0.5 * p[0] * r[1]], []), [x, f], [g], [(x.shape[1], F32)], [],
                    tile=512, name=tag + "_res")
    return y, (x, h, gu, act, f), got


def _ffn_bwd(dy, saved, s, sh, g, w_in, w_out, tag):
    x, h, gu, act, f = saved
    d = x.shape[1]
    df, dg = _rowcall(lambda r, p: ([0.5 * p[0] * r[0]], [0.5 * jnp.sum(r[0] * r[1], 0, keepdims=True)]),
                      [dy, f], [g], [(d, BF16)], [(1, d)], tile=512, name=tag + "_bres")
    da = _mm(df, w_out, "nt", BF16, tag + "_bout")
    dw_out = _mm(act, df, "tn", BF16, tag + "_bwout")

    def act_bwd(r, p):
        gate, up, dav = r[0].astype(F32), r[1].astype(F32), r[2].astype(F32)
        _, vjp = jax.vjp(lambda a, b: _silu(a) * b, gate, up)
        dgate, dup = vjp(dav)
        return [jnp.concatenate([dgate, dup], 1)], []

    (dgu,) = _rowcall(act_bwd, [(gu, D_FF, 0), (gu, D_FF, 1), da], [], [(2 * D_FF, BF16)], [], tile=256,
                      name=tag + "_bact")
    dh = _mm(dgu, w_in, "nt", F32, tag + "_bin")
    dw_in = _mm(h, dgu, "tn", BF16, tag + "_bwin")

    def mod_bwd(r, p):
        _, vjp = jax.vjp(_modulate, r[0], p[0], p[1])
        dx, ds, dsh = vjp(r[1])
        return [r[2] + dx], [ds, dsh]

    dx, ds, dsh = _rowcall(mod_bwd, [x, dh, dy], [s, sh], [(d, F32)], [(1, d), (1, d)], tile=512, name=tag + "_bmod")
    return dx, (dsh, ds, dg), dw_in, dw_out


def _mixer_fwd(x, s, sh, g, wts, rope):
    w_in_p, conv8, a_log, dt_bias, wn, wq, w_uq_p, wkv, w_ukv, wqn, wqr, wkn, wkr, won, w_out = wts
    cos2, sin2 = rope
    d = x.shape[1]
    (h,) = _rowcall(lambda r, p: ([_modulate(r[0], p[0], p[1])], []), [x], [s, sh], [(d, BF16)], [],
                    tile=512, name="mix_mod")
    proj = _mm(h, w_in_p, "nn", F32, "mix_in")
    qkv_c = _conv_fwd(proj, conv8, "mix_conv")
    gab = (proj, LANES, 23)

    q, k, v, gb = _rowcall(
        lambda r, p: (list(_gdn_prep_core(_split(r[0], [HEAD] * 12), r[1], p[0], p[1])), []),
        [qkv_c, gab], [a_log, dt_bias], [(512, F32)] * 3 + [(LANES, F32)], [], tile=256, name="mix_gdn_prep")
    gdn_local = _gdn_local_fwd(q, k, v, gb, "mix_gdn_local")
    o_gdn, gdn_states = _gdn_scan_fwd(*gdn_local, "mix_gdn_scan")
    states = (gdn_local, gdn_states)

    cq, ckv, kr = (proj, 512, 4), (proj, 256, 10), (proj, LANES, 22)
    cqn, ckvn, k_rope = _rowcall(
        lambda r, p: (list(_mla_prep_core(r[0][:, :MLA_Q_LORA], r[1], r[2], r[3], r[4], p[0], p[1], p[2])), []),
        [cq, ckv, kr, cos2, sin2], [wq, wkv, wkr], [(MLA_Q_LORA, BF16), (MLA_KV_LORA, BF16), (LANES, F32)], [],
        tile=512, name="mix_mla_prep")
    qf = _mm(cqn, w_uq_p, "nn", F32, "mix_uq")
    kvf = _mm(ckvn, w_ukv, "nn", F32, "mix_ukv")

    def qk_prep(r, p):
        qparts = _split(r[0], [HEAD] * 8)
        kvparts = _split(r[1], [HEAD] * 8)
        qs, ks, vs = _qk_prep_core(qparts[:4], qparts[4:], kvparts[0::2], kvparts[1::2], r[2], r[3], r[4],
                                   p[0], p[1], p[2])
        return [jnp.concatenate(qs, 1), jnp.concatenate(ks, 1), jnp.concatenate(vs, 1)], []

    qa, ka, va = _rowcall(qk_prep, [qf, kvf, k_rope, cos2, sin2], [wqn, wqr, wkn],
                          [(4 * QK_PAD, BF16), (4 * QK_PAD, BF16), (4 * HEAD, BF16)], [], tile=256,
                          name="mix_qk_prep")
    o_b, lse = _attn_fwd(qa, ka, va, "mix_attn")

    gz = (proj, 512, 3)
    (mixed,) = _rowcall(
        lambda r, p: ([_mix_post_core(_split(r[0], HW4), _split(r[1], HW4), _split(r[2], HW4), p[0], p[1])], []),
        [o_gdn, gz, o_b], [wn, won], [(2 * 512, BF16)], [], tile=512, name="mix_post")
    y = _mm(mixed, w_out, "nn", F32, "mix_out")
    (x_out,) = _rowcall(lambda r, p: ([r[0] + p[0] * r[1]], []), [x, y], [g], [(d, F32)], [], tile=512,
                        name="mix_res")
    saved = (x, h, proj, qkv_c, q, k, v, gb, states, o_gdn, cqn, ckvn, k_rope, qf, kvf, qa, ka, va, o_b, lse,
             mixed, y)
    return x_out, saved


def _mixer_bwd(dy, saved, s, sh, g, wts, rope):
    w_in_p, conv8, a_log, dt_bias, wn, wq, w_uq_p, wkv, w_ukv, wqn, wqr, wkn, wkr, won, w_out = wts
    cos2, sin2 = rope
    (x, h, proj, qkv_c, q, k, v, gb, states, o_gdn, cqn, ckvn, k_rope, qf, kvf, qa, ka, va, o_b, lse,
     mixed, y) = saved
    d = x.shape[1]
    dyb, dg = _rowcall(lambda r, p: ([p[0] * r[0]], [jnp.sum(r[0] * r[1], 0, keepdims=True)]),
                       [dy, y], [g], [(d, BF16)], [(1, d)], tile=512, name="mix_bres")
    dmixed = _mm(dyb, w_out, "nt", F32, "mix_bout")
    dw_out = _mm(mixed, dyb, "tn", BF16, "mix_bwout")

    gz = (proj, 512, 3)

    def post_bwd(r, p):
        _, vjp = jax.vjp(_mix_post_core, _split(r[0], HW4), _split(r[1], HW4), _split(r[2], HW4), p[0], p[1])
        do, dz, dob, dwn, dwon = vjp(r[3])
        return [jnp.concatenate(do, 1), jnp.concatenate(dz, 1), jnp.concatenate(dob, 1)], [dwn, dwon]

    do_gdn, dgz, do_b, dwn, dwon = _rowcall(post_bwd, [o_gdn, gz, o_b, dmixed], [wn, won], [(512, F32)] * 3,
                                            [(1, HEAD), (1, HEAD)], tile=256, name="mix_bpost")

    dqa, dka, dva = _attn_bwd(qa, ka, va, o_b, do_b, lse, "mix_battn")

    def qk_bwd(r, p):
        qparts = _split(r[0], [HEAD] * 8)
        kvparts = _split(r[1], [HEAD] * 8)
        _, vjp = jax.vjp(_qk_prep_core, qparts[:4], qparts[4:], kvparts[0::2], kvparts[1::2], r[2], r[3], r[4],
                         p[0], p[1], p[2])
        cot = (_split(r[5], [QK_PAD] * 4), _split(r[6], [QK_PAD] * 4), _split(r[7], HW4))
        dqn, dqr, dkn, dvp, dkrope, _, _, dwqn, dwqr, dwkn = vjp(cot)
        dkv = []
        for a, b in zip(dkn, dvp):
            dkv += [a, b]
        return [jnp.concatenate(list(dqn) + list(dqr), 1), jnp.concatenate(dkv, 1), dkrope], [dwqn, dwqr, dwkn]

    dqf, dkvf, dk_rope, dwqn, dwqr, dwkn = _rowcall(
        qk_bwd, [qf, kvf, k_rope, cos2, sin2, dqa, dka, dva], [wqn, wqr, wkn],
        [(8 * HEAD, BF16), (8 * HEAD, BF16), (LANES, F32)], [(1, HEAD)] * 3, tile=256, name="mix_bqk_prep")
    dcqn = _mm(dqf, w_uq_p, "nt", F32, "mix_buq")
    dw_uq_p = _mm(cqn, dqf, "tn", F32, "mix_bwuq")
    dckvn = _mm(dkvf, w_ukv, "nt", F32, "mix_bukv")
    dw_ukv = _mm(ckvn, dkvf, "tn", F32, "mix_bwukv")

    cq, ckv, kr = (proj, 512, 4), (proj, 256, 10), (proj, LANES, 22)

    def mla_bwd(r, p):
        _, vjp = jax.vjp(_mla_prep_core, r[0][:, :MLA_Q_LORA], r[1], r[2], r[3], r[4], p[0], p[1], p[2])
        dcq, dckv, dkr, _, _, dwq, dwkv, dwkr = vjp((r[5], r[6], r[7]))
        pad = jnp.zeros((dcq.shape[0], 512 - MLA_Q_LORA), F32)
        return [jnp.concatenate([dcq, pad], 1), dckv, dkr], [dwq, dwkv, dwkr]

    dcq, dckv, dkr, dwq, dwkv, dwkr = _rowcall(
        mla_bwd, [cq, ckv, kr, cos2, sin2, dcqn, dckvn, dk_rope], [wq, wkv, wkr],
        [(512, F32), (MLA_KV_LORA, F32), (LANES, F32)], [(1, MLA_Q_LORA), (1, MLA_KV_LORA), (1, LANES)],
        tile=512, name="mix_bmla_prep")

    gdn_local, gdn_states = states
    d_local = _gdn_scan_bwd(*gdn_local, gdn_states, do_gdn, "mix_bgdn_scan")
    dq, dk, dv, dgb = _gdn_local_bwd(q, k, v, gb, *d_local, "mix_bgdn_local")
    gab = (proj, LANES, 23)

    def gdn_prep_bwd(r, p):
        _, vjp = jax.vjp(_gdn_prep_core, _split(r[0], [HEAD] * 12), r[1], p[0], p[1])
        dparts, dgab, da_log, ddt = vjp((r[2], r[3], r[4], r[5]))
        return [jnp.concatenate(dparts, 1), dgab], [da_log, ddt]

    dqkv_c, dgab, da_log, ddt = _rowcall(gdn_prep_bwd, [qkv_c, gab, dq, dk, dv, dgb], [a_log, dt_bias],
                                         [(1536, F32), (LANES, F32)], [(1, LANES), (1, LANES)], tile=256,
                                         name="mix_bgdn_prep")
    dqkv_pre, dconv8 = _conv_bwd(proj, dqkv_c, conv8, "mix_bconv")

    dproj = jnp.concatenate([dqkv_pre.astype(BF16), dgz.astype(BF16), dcq.astype(BF16), dckv.astype(BF16),
                             dkr.astype(BF16), dgab.astype(BF16)], axis=1)
    dh = _mm(dproj, w_in_p, "nt", F32, "mix_bin")
    dw_in_p = _mm(h, dproj, "tn", F32, "mix_bwin")

    def mod_bwd(r, p):
        _, vjp = jax.vjp(_modulate, r[0], p[0], p[1])
        dx, ds, dsh = vjp(r[1])
        return [r[2] + dx], [ds, dsh]

    dx, ds, dsh = _rowcall(mod_bwd, [x, dh, dy], [s, sh], [(d, F32)], [(1, d), (1, d)], tile=512, name="mix_bmod")
    small = dict(conv=dconv8, a_log=da_log, dt=ddt, wn=dwn, wq=dwq, wkv=dwkv, wqn=dwqn, wqr=dwqr, wkn=dwkn,
                 wkr=dwkr, won=dwon)
    return dx, (dsh, ds, dg), dw_in_p, dw_uq_p, dw_ukv, dw_out, small


def _pad_cols(a, n):
    return jnp.pad(a, ((0, 0),) * (a.ndim - 1) + ((0, n - a.shape[-1]),))


def _pack_w_in(w):
    z = lambda n: jnp.zeros((w.shape[0], n), w.dtype)
    return jnp.concatenate([w[:, 0:2048], w[:, 2056:2440], z(128), w[:, 2440:2696], w[:, 2696:2760], z(64),
                            w[:, 2048:2056], z(120)], axis=1)


def _unpack_w_in(wp):
    return jnp.concatenate([wp[:, 0:2048], wp[:, 2944:2952], wp[:, 2048:2432], wp[:, 2560:2816], wp[:, 2816:2880]],
                           axis=1)


def _pack_w_uq(w):
    z = jnp.zeros((w.shape[0], LANES - MLA_ROPE), w.dtype)
    nope = [w[:, h * 192:h * 192 + HEAD] for h in range(MLA_HEADS)]
    rope = []
    for h in range(MLA_HEADS):
        rope += [w[:, h * 192 + HEAD:(h + 1) * 192], z]
    return jnp.concatenate(nope + rope, axis=1)


def _unpack_w_uq(wp):
    cols = []
    for h in range(MLA_HEADS):
        cols += [wp[:, h * HEAD:(h + 1) * HEAD], wp[:, 512 + h * LANES:512 + h * LANES + MLA_ROPE]]
    return jnp.concatenate(cols, axis=1)


def _cols_to_chips(a):
    r, c = a.shape
    return a.reshape(r, 4, c // 4).transpose(1, 0, 2)


def _chips_to_cols(a):
    _, r, n = a.shape
    return a.transpose(1, 0, 2).reshape(r, 4 * n)


def _pad128(v, n=LANES):
    return _pad_cols(v.reshape(1, -1), n)


def kernel(x, c, positions, w_ada, b_ada, ffn1_w_in, ffn1_w_out, w_in, gdn_conv_w, gdn_a_log, gdn_dt_bias, gdn_norm_w, mla_q_norm_w, mla_w_uq, mla_kv_norm_w, mla_w_ukv, qkn_q_nope, qkn_q_rope, qkn_k_nope, qkn_k_rope, mla_out_norm_w, w_out, ffn2_w_in, ffn2_w_out, loss_target, m_w_ada, m_b_ada, m_ffn1_w_in, m_ffn1_w_out, m_w_in, m_gdn_conv_w, m_gdn_a_log, m_gdn_dt_bias, m_gdn_norm_w, m_mla_q_norm_w, m_mla_w_uq, m_mla_kv_norm_w, m_mla_w_ukv, m_qkn_q_nope, m_qkn_q_rope, m_qkn_k_nope, m_qkn_k_rope, m_mla_out_norm_w, m_w_out, m_ffn2_w_in, m_ffn2_w_out, v_w_ada, v_b_ada, v_ffn1_w_in, v_ffn1_w_out, v_w_in, v_gdn_conv_w, v_gdn_a_log, v_gdn_dt_bias, v_gdn_norm_w, v_mla_q_norm_w, v_mla_w_uq, v_mla_kv_norm_w, v_mla_w_ukv, v_qkn_q_nope, v_qkn_q_rope, v_qkn_k_nope, v_qkn_k_rope, v_mla_out_norm_w, v_w_out, v_ffn2_w_in, v_ffn2_w_out):
    weights = dict(w_ada=w_ada, b_ada=b_ada, ffn1_w_in=ffn1_w_in, ffn1_w_out=ffn1_w_out, w_in=w_in,
                   gdn_conv_w=gdn_conv_w, gdn_a_log=gdn_a_log, gdn_dt_bias=gdn_dt_bias, gdn_norm_w=gdn_norm_w,
                   mla_q_norm_w=mla_q_norm_w, mla_w_uq=mla_w_uq, mla_kv_norm_w=mla_kv_norm_w, mla_w_ukv=mla_w_ukv,
                   qkn_q_nope=qkn_q_nope, qkn_q_rope=qkn_q_rope, qkn_k_nope=qkn_k_nope, qkn_k_rope=qkn_k_rope,
                   mla_out_norm_w=mla_out_norm_w, w_out=w_out, ffn2_w_in=ffn2_w_in, ffn2_w_out=ffn2_w_out)
    moms_m = dict(w_ada=m_w_ada, b_ada=m_b_ada, ffn1_w_in=m_ffn1_w_in, ffn1_w_out=m_ffn1_w_out, w_in=m_w_in,
                  gdn_conv_w=m_gdn_conv_w, gdn_a_log=m_gdn_a_log, gdn_dt_bias=m_gdn_dt_bias,
                  gdn_norm_w=m_gdn_norm_w, mla_q_norm_w=m_mla_q_norm_w, mla_w_uq=m_mla_w_uq,
                  mla_kv_norm_w=m_mla_kv_norm_w, mla_w_ukv=m_mla_w_ukv, qkn_q_nope=m_qkn_q_nope,
                  qkn_q_rope=m_qkn_q_rope, qkn_k_nope=m_qkn_k_nope, qkn_k_rope=m_qkn_k_rope,
                  mla_out_norm_w=m_mla_out_norm_w, w_out=m_w_out, ffn2_w_in=m_ffn2_w_in, ffn2_w_out=m_ffn2_w_out)
    moms_v = dict(w_ada=v_w_ada, b_ada=v_b_ada, ffn1_w_in=v_ffn1_w_in, ffn1_w_out=v_ffn1_w_out, w_in=v_w_in,
                  gdn_conv_w=v_gdn_conv_w, gdn_a_log=v_gdn_a_log, gdn_dt_bias=v_gdn_dt_bias,
                  gdn_norm_w=v_gdn_norm_w, mla_q_norm_w=v_mla_q_norm_w, mla_w_uq=v_mla_w_uq,
                  mla_kv_norm_w=v_mla_kv_norm_w, mla_w_ukv=v_mla_w_ukv, qkn_q_nope=v_qkn_q_nope,
                  qkn_q_rope=v_qkn_q_rope, qkn_k_nope=v_qkn_k_nope, qkn_k_rope=v_qkn_k_rope,
                  mla_out_norm_w=v_mla_out_norm_w, w_out=v_w_out, ffn2_w_in=v_ffn2_w_in, ffn2_w_out=v_ffn2_w_out)
    names = list(weights)

    seq, d = x.shape[1], x.shape[2]
    x2d = x.reshape(seq, d)
    tgt = loss_target.reshape(seq, d)
    mx, my, mc = _place()
    chip = 2 * mx + my
    me = 2 * chip + mc
    n_mod = b_ada.shape[1] // d
    shard = w_ada.shape[2]

    half = MLA_ROPE // 2
    inv_freq = 10000.0 ** (-jnp.arange(half, dtype=F32) / half)
    ang = positions.astype(F32).reshape(seq, 1) * inv_freq
    cosv, sinv = jnp.cos(ang), jnp.sin(ang)
    cos2 = _pad_cols(jnp.concatenate([cosv, cosv], 1), LANES)
    sin2 = _pad_cols(jnp.concatenate([-sinv, sinv], 1), LANES)
    rope = (cos2, sin2)

    c_all = _allgather8(jnp.pad(c, ((0, SUBLANES - 1), (0, 0))), "gather_c")[:, 0, :]
    (sc_all,) = _rowcall(lambda r, p: ([_silu(r[0])], []), [c_all], [], [(d, F32)], [], tile=8, name="ada_silu")
    mod_part = _mm(sc_all, w_ada[0], "nn", F32, "ada_mm", hi=True)
    mod_all = _allgather8(mod_part, "gather_mod")
    mod_rows = lax.dynamic_index_in_dim(mod_all, me, axis=1, keepdims=False)
    mod_raw = jnp.concatenate([mod_rows[2 * jj] for jj in range(4)], axis=0).reshape(1, 4 * shard)
    (mod,) = _rowcall(lambda r, p: ([r[0] + r[1]], []),
                      [jnp.pad(mod_raw, ((0, 7), (0, 0))), jnp.pad(b_ada, ((0, 7), (0, 0)))], [],
                      [(4 * shard, F32)], [], tile=8, name="ada_bias")
    mods = [mod[0:1, i * d:(i + 1) * d] for i in range(n_mod)]
    sh1, s1, g1, sh2, s2, g2, sh3, s3, g3 = mods

    def shard_bf16(w, pad_to=None):
        w2 = w[0].astype(BF16)
        return _pad_cols(w2, pad_to) if pad_to else w2

    def cols_of(got, w):
        return _chips_to_cols(got[:, :, :w.shape[2]])

    def rows_of(got):
        return got.reshape(4 * got.shape[1], got.shape[2])

    f1_in = cols_of(_allgather_chips(shard_bf16(ffn1_w_in), "gather_f1_in"), ffn1_w_in)
    f1_out = rows_of(_allgather_chips(shard_bf16(ffn1_w_out), "gather_f1_out"))
    later = [shard_bf16(w_in, 768), shard_bf16(mla_w_uq, 256), shard_bf16(mla_w_ukv), shard_bf16(w_out),
             shard_bf16(ffn2_w_in), shard_bf16(ffn2_w_out)]
    conv_all = _allgather8(jnp.pad(gdn_conv_w[0], ((0, SUBLANES - CONV_K), (0, 0))), "gather_conv")
    conv8 = jnp.concatenate([conv_all[2 * jj] for jj in range(4)], axis=1)

    x1, sv1, got = _ffn_fwd(x2d, s1, sh1, g1, f1_in, f1_out, "ffn1", gather=later)
    w_in_full, w_uq_full, w_ukv_full = cols_of(got[0], w_in), cols_of(got[1], mla_w_uq), cols_of(got[2], mla_w_ukv)
    w_out_full, f2_in, f2_out = rows_of(got[3]), cols_of(got[4], ffn2_w_in), rows_of(got[5])
    wts = (_pack_w_in(w_in_full), conv8, _pad128(gdn_a_log), _pad128(gdn_dt_bias), gdn_norm_w,
           mla_q_norm_w, _pack_w_uq(w_uq_full), mla_kv_norm_w, w_ukv_full, qkn_q_nope, _pad128(qkn_q_rope),
           qkn_k_nope, _pad128(qkn_k_rope), mla_out_norm_w, w_out_full)
    xm, svm = _mixer_fwd(x1, s2, sh2, g2, wts, rope)
    x3, sv3, _ = _ffn_fwd(xm, s3, sh3, g3, f2_in, f2_out, "ffn2")

    def loss_fn(r, p):
        err = r[0] - r[1]
        part = 0.5 * jnp.sum(jnp.sum(err * err, axis=1, keepdims=True) * (1.0 / d), axis=0, keepdims=True)
        return [err * (1.0 / d)], [jnp.broadcast_to(part, (1, LANES))]

    dy, loss_part = _rowcall(loss_fn, [x3, tgt], [], [(d, F32)], [(1, LANES)], tile=512, name="loss")
    loss = lax.psum(loss_part[0, 0], ("x", "y", "c"))

    dxm, dmod3, dw_f2_in, dw_f2_out = _ffn_bwd(dy, sv3, s3, sh3, g3, f2_in, f2_out, "ffn2")
    dx1, dmod2, dw_in_p, dw_uq_p, dw_ukv, dw_out_m, small = _mixer_bwd(dxm, svm, s2, sh2, g2, wts, rope)
    dx0, dmod1, dw_f1_in, dw_f1_out = _ffn_bwd(dx1, sv1, s1, sh1, g1, f1_in, f1_out, "ffn1")
    grad_x = dx0.reshape(x.shape)

    dmod = jnp.concatenate(list(dmod1) + list(dmod2) + list(dmod3), axis=1)
    small_parts = [dmod, small["conv"][:CONV_K], small["a_log"], small["dt"], small["wn"], small["wq"],
                   small["wkv"], small["wqn"], small["wqr"], small["wkn"], small["wkr"], small["won"]]
    packed, offs = _pack_rows(small_parts)
    gathered = _allgather8(packed, "gather_small")
    total = _sum8(gathered, "sum_small")
    (g_b_ada, g_conv, g_a_log, g_dt, g_wn, g_wq, g_wkv, g_wqn, g_wqr, g_wkn, g_wkr, g_won) = _unpack_rows(
        total, offs, [p.shape for p in small_parts])
    dmod_all = _unpack_rows(gathered.reshape(-1, LANES),
                            [(dd * packed.shape[0] + offs[0][0], offs[0][1]) for dd in range(8)],
                            [dmod.shape] * 8)
    dmod_all = jnp.concatenate(dmod_all, axis=0)
    dmod_mine = lax.dynamic_slice_in_dim(dmod_all, chip * shard, shard, axis=1)

    def ada_grad(r, p):
        acc = jnp.zeros((r[0].shape[0], shard), F32)
        for b in range(8):
            acc = acc + r[0][:, b:b + 1] * p[0][b:b + 1, :]
        return [acc], []

    (g_w_ada,) = _rowcall(ada_grad, [_pad_cols(sc_all.T, LANES)], [dmod_mine], [(shard, F32)], [], tile=256,
                          name="ada_grad")

    grads = dict(
        w_ada=g_w_ada[None], b_ada=g_b_ada,
        gdn_conv_w=lax.dynamic_slice_in_dim(g_conv, chip * gdn_conv_w.shape[2], gdn_conv_w.shape[2], axis=1)[None],
        gdn_a_log=g_a_log[:, :GDN_HEADS], gdn_dt_bias=g_dt[:, :GDN_HEADS], gdn_norm_w=g_wn, mla_q_norm_w=g_wq,
        mla_kv_norm_w=g_wkv, qkn_q_nope=g_wqn, qkn_q_rope=g_wqr[:, :MLA_ROPE], qkn_k_nope=g_wkn,
        qkn_k_rope=g_wkr[:, :MLA_ROPE], mla_out_norm_w=g_won)

    def rs_cols(dw, name, pad_to=None):
        g4 = _cols_to_chips(dw).astype(BF16)
        n = g4.shape[2]
        if pad_to:
            g4 = _pad_cols(g4, pad_to)
        return _reduce_scatter_chips(g4, name)[:, :n][None]

    def rs_rows(dw, name):
        r, cc = dw.shape
        return _reduce_scatter_chips(dw.astype(BF16).reshape(4, r // 4, cc), name)[None]

    grads["ffn2_w_in"] = rs_cols(dw_f2_in, "rs_f2_in")
    grads["ffn2_w_out"] = rs_rows(dw_f2_out, "rs_f2_out")
    grads["w_in"] = rs_cols(_unpack_w_in(dw_in_p), "rs_w_in", 768)
    grads["mla_w_uq"] = rs_cols(_unpack_w_uq(dw_uq_p), "rs_w_uq", 256)
    grads["mla_w_ukv"] = rs_cols(dw_ukv, "rs_w_ukv")
    grads["w_out"] = rs_rows(dw_out_m, "rs_w_out")
    grads["ffn1_w_in"] = rs_cols(dw_f1_in, "rs_f1_in")
    grads["ffn1_w_out"] = rs_rows(dw_f1_out, "rs_f1_out")

    big = ["w_ada", "ffn1_w_in", "ffn1_w_out", "w_in", "mla_w_uq", "mla_w_ukv", "w_out", "ffn2_w_in", "ffn2_w_out"]
    delta, new_m, new_v = {}, {}, {}
    for nme in big:
        shp = weights[nme].shape
        dl, nm, nv = _adamw(weights[nme][0], grads[nme][0], moms_m[nme][0], moms_v[nme][0], "adamw_" + nme)
        delta[nme], new_m[nme], new_v[nme] = dl.reshape(shp), nm.reshape(shp), nv.reshape(shp)
    tiny = [nme for nme in names if nme not in big]
    shapes = [weights[nme].shape for nme in tiny]
    pw, poffs = _pack_rows([weights[nme] for nme in tiny])
    pg, _ = _pack_rows([grads[nme] for nme in tiny])
    pm, _ = _pack_rows([moms_m[nme] for nme in tiny])
    pv, _ = _pack_rows([moms_v[nme] for nme in tiny])
    pd, pnm, pnv = _adamw(pw, pg, pm, pv, "adamw_small")
    for nme, dl, nm, nv in zip(tiny, _unpack_rows(pd, poffs, shapes), _unpack_rows(pnm, poffs, shapes),
                               _unpack_rows(pnv, poffs, shapes)):
        delta[nme], new_m[nme], new_v[nme] = dl, nm, nv

    return (loss, grad_x, *[grads[nme].reshape(weights[nme].shape) for nme in names],
            *[delta[nme] for nme in names], *[new_m[nme] for nme in names], *[new_v[nme] for nme in names])
```

```python
import functools

import jax
import jax.numpy as jnp
from jax import lax
from jax.experimental import pallas as pl
from jax.experimental.pallas import tpu as pltpu

F32 = jnp.float32
BF16 = jnp.bfloat16
HI = lax.Precision.HIGHEST
MESH = pl.DeviceIdType.MESH

EPS = 1e-6
CHUNK = 64
D_FF = 2816
GDN_HEADS = 4
HEAD = 128
MLA_HEADS = 4
MLA_ROPE = 64
MLA_Q_LORA = 384
MLA_KV_LORA = 256
QK_PAD = 256
ATT_SCALE = (HEAD + MLA_ROPE) ** -0.5
N_PROJ = 3072

ADAM_LR, ADAM_B1, ADAM_B2, ADAM_EPS, ADAM_WD, ADAM_STEP = 0.001, 0.9, 0.999, 1e-08, 0.01, 10

LANES = 128
SUBLANES = 8
VMEM_LIMIT = 56 * 2 ** 20


def _params(sem=None):
    return pltpu.CompilerParams(dimension_semantics=sem, vmem_limit_bytes=VMEM_LIMIT)


def _pick(n, cap, align):
    best = None
    d = align
    while d <= min(n, cap):
        if n % d == 0:
            best = d
        d += align
    return best if best is not None else n


def _iota(shape, dim):
    return lax.broadcasted_iota(jnp.int32, shape, dim)


def _rowcall(fn, rows, params, out_rows, out_accs, *, tile, name):
    rows = [r if isinstance(r, tuple) else (r, r.shape[1], 0) for r in rows]
    s = rows[0][0].shape[-2]
    t = min(tile, s)
    n = s // t
    n_in = len(rows) + len(params)
    n_row_out = len(out_rows)

    in_specs = []
    for r in rows:
        if len(r) == 3:
            in_specs.append(pl.BlockSpec((t, r[1]), functools.partial(lambda i, b: (i, b), b=r[2])))
        else:
            in_specs.append(pl.BlockSpec((None, t, r[1]), functools.partial(lambda i, b, h: (h, i, b), b=r[2], h=r[3])))
    in_specs += [pl.BlockSpec(p.shape, lambda i: (0, 0)) for p in params]
    out_shape, out_specs = [], []
    for o in out_rows:
        if len(o) == 2:
            out_shape.append(jax.ShapeDtypeStruct((s, o[0]), o[1]))
            out_specs.append(pl.BlockSpec((t, o[0]), lambda i: (i, 0)))
        else:
            out_shape.append(jax.ShapeDtypeStruct((o[2], s, o[0]), o[1]))
            out_specs.append(pl.BlockSpec((o[2], t, o[0]), lambda i: (0, i, 0)))
    out_shape += [jax.ShapeDtypeStruct(shape, F32) for shape in out_accs]
    out_specs += [pl.BlockSpec(shape, lambda i: (0, 0)) for shape in out_accs]

    def body(*refs):
        ins = refs[:n_in]
        outs = refs[n_in:]
        i = pl.program_id(0)
        vals = [r[...] for r in ins]
        row_outs, acc_outs = fn(vals[:len(rows)], vals[len(rows):])
        for r, v in zip(outs[:n_row_out], row_outs):
            if isinstance(v, (list, tuple)):
                for hh, piece in enumerate(v):
                    r[hh] = piece.astype(r.dtype)
            else:
                r[...] = v.astype(r.dtype)
        if out_accs:
            @pl.when(i == 0)
            def _():
                for r in outs[n_row_out:]:
                    r[...] = jnp.zeros(r.shape, F32)
            for r, v in zip(outs[n_row_out:], acc_outs):
                r[...] += v

    res = pl.pallas_call(
        body, name=name, grid=(n,), in_specs=in_specs, out_specs=out_specs, out_shape=out_shape,
        compiler_params=_params(("arbitrary",) if out_accs else ("parallel",)),
    )(*[r[0] for r in rows], *params)
    return list(res)


MM_TILE_MN = 1536


def _mm(a, b, mode, out_dtype, name, hi=False, gather=()):
    if mode == "nn":
        (m, k), (_, n) = a.shape, b.shape
        dims = (((1,), (0,)), ((), ()))
    elif mode == "nt":
        (m, k), (n, _) = a.shape, b.shape
        dims = (((1,), (1,)), ((), ()))
    else:
        (k, m), (_, n) = a.shape, b.shape
        dims = (((0,), (0,)), ((), ()))
    tm = _pick(m, MM_TILE_MN if mode == "tn" else 1024, LANES if mode == "tn" else 16)
    tn = _pick(n, MM_TILE_MN, LANES)
    tk = _pick(k, 1024 if mode == "tn" else MM_TILE_MN, LANES)
    nk = k // tk
    if mode == "nn":
        a_spec = pl.BlockSpec((tm, tk), lambda i, j, kk: (i, kk))
        b_spec = pl.BlockSpec((tk, tn), lambda i, j, kk: (kk, j))
    elif mode == "nt":
        a_spec = pl.BlockSpec((tm, tk), lambda i, j, kk: (i, kk))
        b_spec = pl.BlockSpec((tn, tk), lambda i, j, kk: (j, kk))
    else:
        a_spec = pl.BlockSpec((tk, tm), lambda i, j, kk: (kk, i))
        b_spec = pl.BlockSpec((tk, tn), lambda i, j, kk: (kk, j))

    ng = len(gather)
    grid = (m // tm, n // tn, nk)
    steps = grid[0] * grid[1] * grid[2]

    def body(*refs):
        a_ref, b_ref = refs[:2]
        x_refs = refs[2:2 + ng]
        o_ref = refs[2 + ng]
        got_refs = refs[3 + ng:3 + 2 * ng]
        acc_ref = refs[3 + 2 * ng]
        kk = pl.program_id(2)
        if ng:
            sems = refs[4 + 2 * ng:]
            step = (pl.program_id(0) * grid[1] + pl.program_id(1)) * nk + kk

            def phase(ph):
                for slot in range(ng):
                    _gather_phase(ph, x_refs[slot], got_refs[slot], *sems, slot)

            pl.when(step == 0)(lambda: phase(0))
            pl.when(step == steps // 2)(lambda: phase(1))

        @pl.when(kk == 0)
        def _():
            acc_ref[...] = jnp.zeros(acc_ref.shape, F32)

        av, bv = a_ref[...], b_ref[...]
        if hi:
            acc_ref[...] += lax.dot_general(av, bv, dims, precision=HI, preferred_element_type=F32)
        else:
            acc_ref[...] += lax.dot_general(av.astype(BF16), bv.astype(BF16), dims,
                                            preferred_element_type=F32)

        @pl.when(kk == nk - 1)
        def _():
            o_ref[...] = acc_ref[...].astype(o_ref.dtype)

        if ng:
            pl.when(step == steps - 1)(lambda: phase(2))

    hbm = pl.BlockSpec(memory_space=pl.ANY)
    res = pl.pallas_call(
        body, name=name, grid=grid,
        in_specs=[a_spec, b_spec] + [hbm] * ng,
        out_specs=[pl.BlockSpec((tm, tn), lambda i, j, kk: (i, j))] + [hbm] * ng,
        out_shape=[jax.ShapeDtypeStruct((m, n), out_dtype)]
        + [jax.ShapeDtypeStruct((4,) + x.shape, x.dtype) for x in gather],
        scratch_shapes=[pltpu.VMEM((tm, tn), F32)] + (_gather_sems(ng) if ng else []),
        compiler_params=_params(("arbitrary",) * 3 if ng else ("parallel", "parallel", "arbitrary")),
    )(a, b, *gather)
    return res if ng else res[0]


def _rms(x, w=None, n=None):
    n = x.shape[-1] if n is None else n
    y = x * lax.rsqrt(jnp.sum(x * x, axis=-1, keepdims=True) * (1.0 / n) + EPS)
    return y if w is None else y * w


def _silu(x):
    return x * jax.nn.sigmoid(x)


def _softplus(x):
    return jnp.maximum(x, 0.0) + jnp.log1p(jnp.exp(-jnp.abs(x)))


def _split(x, widths):
    out, o = [], 0
    for w in widths:
        out.append(x[:, o:o + w])
        o += w
    return out


def _modulate(x, s, sh):
    return _rms(x) * (1.0 + s) + sh


def _rope_rot(x):
    r, c = _iota((LANES, LANES), 0), _iota((LANES, LANES), 1)
    half = MLA_ROPE // 2
    perm = (((r < half) & (c == r + half)) | ((r >= half) & (r < MLA_ROPE) & (c == r - half))).astype(F32)
    return jnp.dot(x, perm, precision=HI, preferred_element_type=F32)


def _rope(x, cos2, sin2):
    return x * cos2 + _rope_rot(x) * sin2


def _gdn_prep_core(qkv_parts, gab, a_log, dt_bias):
    act = [_silu(p) for p in qkv_parts]
    qs = [p * lax.rsqrt(jnp.sum(p * p, -1, keepdims=True) + EPS) * (HEAD ** -0.5) for p in act[:4]]
    ks = [p * lax.rsqrt(jnp.sum(p * p, -1, keepdims=True) + EPS) for p in act[4:8]]
    lane = _iota(gab.shape, 1)
    g = -jnp.exp(a_log) * _softplus(gab + dt_bias)
    beta = jax.nn.sigmoid(gab)
    gb = jnp.where(lane < GDN_HEADS, g, jnp.where(lane < 2 * GDN_HEADS, beta, 0.0))
    return (jnp.concatenate(qs, 1), jnp.concatenate(ks, 1), jnp.concatenate(act[8:], 1), gb)


def _mla_prep_core(cq, ckv, kr, cos2, sin2, wq, wkv, wkr):
    cqn = _rms(cq, wq)
    ckvn = _rms(ckv, wkv)
    k_rope = _rope(_rms(kr, wkr, MLA_ROPE), cos2, sin2)
    return cqn, ckvn, k_rope


def _qk_prep_core(qn_parts, qr_parts, kn_parts, v_parts, k_rope, cos2, sin2, wqn, wqr, wkn):
    qs, ks = [], []
    for h in range(MLA_HEADS):
        qn = _rms(qn_parts[h], wqn) * ATT_SCALE
        qr = _rope(_rms(qr_parts[h], wqr, MLA_ROPE), cos2, sin2) * ATT_SCALE
        qs.append(jnp.concatenate([qn, qr], 1))
        ks.append(jnp.concatenate([_rms(kn_parts[h], wkn), k_rope], 1))
    return qs, ks, list(v_parts)


def _mix_post_core(o_parts, gz_parts, ob_parts, wn, won):
    oa = [_rms(o, wn) * _silu(z) for o, z in zip(o_parts, gz_parts)]
    ob = [_rms(o, won) for o in ob_parts]
    return jnp.concatenate(oa + ob, 1)


CONV_K = 4
HALO = SUBLANES


def _conv_fwd(proj, w8, name):
    s = proj.shape[0]
    c = w8.shape[1]
    t = min(256, s)
    n = s // t
    hb = t // HALO

    def body(x_ref, prev_ref, w_ref, o_ref, buf):
        i = pl.program_id(0)
        buf[pl.ds(0, HALO), :] = jnp.where(i > 0, prev_ref[...], 0.0)
        buf[pl.ds(HALO, t), :] = x_ref[...]
        acc = jnp.zeros((t, c), F32)
        for k in range(CONV_K):
            acc = acc + w_ref[k:k + 1, :] * buf[pl.ds(HALO - (CONV_K - 1) + k, t), :]
        o_ref[...] = acc

    return pl.pallas_call(
        body, name=name, grid=(n,),
        in_specs=[pl.BlockSpec((t, c), lambda i: (i, 0)),
                  pl.BlockSpec((HALO, c), lambda i: (jnp.maximum(i * hb - 1, 0), 0)),
                  pl.BlockSpec(w8.shape, lambda i: (0, 0))],
        out_specs=pl.BlockSpec((t, c), lambda i: (i, 0)),
        out_shape=jax.ShapeDtypeStruct((s, c), F32),
        scratch_shapes=[pltpu.VMEM((t + HALO, c), F32)],
        compiler_params=_params(("parallel",)),
    )(proj, proj, w8)


def _conv_bwd(proj, dy, w8, name):
    s = proj.shape[0]
    c = w8.shape[1]
    t = min(256, s)
    n = s // t
    hb = t // HALO

    def body(x_ref, prev_ref, dy_ref, next_ref, w_ref, dx_ref, dw_ref, bufx, bufd):
        i = pl.program_id(0)
        bufx[pl.ds(0, HALO), :] = jnp.where(i > 0, prev_ref[...], 0.0)
        bufx[pl.ds(HALO, t), :] = x_ref[...]
        bufd[pl.ds(0, t), :] = dy_ref[...]
        bufd[pl.ds(t, HALO), :] = jnp.where(i < n - 1, next_ref[...], 0.0)

        @pl.when(i == 0)
        def _():
            dw_ref[...] = jnp.zeros(dw_ref.shape, F32)

        dyv = dy_ref[...]
        acc = jnp.zeros((t, c), F32)
        for k in range(CONV_K):
            acc = acc + w_ref[k:k + 1, :] * bufd[pl.ds(CONV_K - 1 - k, t), :]
            dw_ref[k:k + 1, :] += jnp.sum(dyv * bufx[pl.ds(HALO - (CONV_K - 1) + k, t), :], axis=0, keepdims=True)
        dx_ref[...] = acc

    return pl.pallas_call(
        body, name=name, grid=(n,),
        in_specs=[pl.BlockSpec((t, c), lambda i: (i, 0)),
                  pl.BlockSpec((HALO, c), lambda i: (jnp.maximum(i * hb - 1, 0), 0)),
                  pl.BlockSpec((t, c), lambda i: (i, 0)),
                  pl.BlockSpec((HALO, c), lambda i: (jnp.minimum((i + 1) * hb, s // HALO - 1), 0)),
                  pl.BlockSpec(w8.shape, lambda i: (0, 0))],
        out_specs=[pl.BlockSpec((t, c), lambda i: (i, 0)), pl.BlockSpec(w8.shape, lambda i: (0, 0))],
        out_shape=[jax.ShapeDtypeStruct((s, c), F32), jax.ShapeDtypeStruct(w8.shape, F32)],
        scratch_shapes=[pltpu.VMEM((t + HALO, c), F32), pltpu.VMEM((t + HALO, c), F32)],
        compiler_params=_params(("arbitrary",)),
    )(proj, proj, dy, dy, w8)


def _dot(a, b):
    return jnp.dot(a, b, precision=HI, preferred_element_type=F32)


def _dot_nt(a, b):
    return lax.dot_general(a, b, (((1,), (1,)), ((), ())), precision=HI, preferred_element_type=F32)


def _dot_tn(a, b):
    return lax.dot_general(a, b, (((0,), (0,)), ((), ())), precision=HI, preferred_element_type=F32)


_B_NN = (((2,), (1,)), ((0,), (0,)))
_B_NT = (((2,), (2,)), ((0,), (0,)))
_B_TN = (((1,), (1,)), ((0,), (0,)))


def _bdot_hi(a, b):
    return lax.dot_general(a, b, _B_NN, precision=HI, preferred_element_type=F32)


def _dot3(a, b, dims):
    return lax.dot_general(a, b, dims, precision=lax.Precision.HIGH, preferred_element_type=F32)


class _Dots:
    def __init__(self, diff):
        nn = lambda a, b: _dot3(a, b, _B_NN)
        nt = lambda a, b: _dot3(a, b, _B_NT)
        tn = lambda a, b: _dot3(a, b, _B_TN)
        if diff:
            def with_rule(f, bwd):
                g = jax.custom_vjp(f)
                g.defvjp(lambda a, b: (f(a, b), (a, b)), bwd)
                return g
            self.nn = with_rule(nn, lambda r, ct: (nt(ct, r[1]), tn(r[0], ct)))
            self.nt = with_rule(nt, lambda r, ct: (nn(ct, r[1]), tn(ct, r[0])))
            self.tn = with_rule(tn, lambda r, ct: (nt(r[1], ct), nn(r[0], ct)))
        else:
            self.nn, self.nt, self.tn = nn, nt, tn


def _unit_lower_inverse(a, dots):
    c = a.shape[-1]
    ri, ci = _iota(a.shape, 1), _iota(a.shape, 2)
    inner = (ri // 2) == (ci // 2)
    t = (ri == ci).astype(F32) - jnp.where(inner, a, 0.0)
    blk = 4
    while blk <= c:
        outer = (ri // blk) == (ci // blk)
        low = jnp.where(outer & jnp.logical_not(inner), a, 0.0)
        t = t - dots.nn(dots.nn(t, low), t)
        inner = outer
        blk *= 2
    return t


def _stack(xs):
    return jnp.concatenate([x[None] for x in xs], axis=0)


def _gdn_local(dots, q, k, v, gbs):
    b, c, _ = q.shape
    gcols, bcols = [], []
    for gb in gbs:
        lane = _iota(gb.shape, 1)
        for h in range(GDN_HEADS):
            gcols.append(jnp.sum(jnp.where(lane == h, gb, 0.0), axis=1, keepdims=True))
            bcols.append(jnp.sum(jnp.where(lane == GDN_HEADS + h, gb, 0.0), axis=1, keepdims=True))
    gcol, bcol = _stack(gcols), _stack(bcols)
    ri, ci = _iota((b, c, c), 1), _iota((b, c, c), 2)
    incl = ri >= ci
    tril = incl.astype(F32)
    g_cc = _bdot_hi(tril, jnp.broadcast_to(gcol, (b, c, c)))
    g_row = _bdot_hi(jnp.ones((b, c, c), F32), jnp.where(ri == ci, g_cc, 0.0))
    g_cl = _bdot_hi(tril, jnp.broadcast_to(gcol, (b, c, HEAD)))
    g_last = jnp.sum(jnp.broadcast_to(gcol, (b, c, HEAD)), axis=1, keepdims=True)
    decay = jnp.where(incl, jnp.exp(jnp.where(incl, g_cc - g_row, 0.0)), 0.0)
    kk = dots.nt(k, k)
    minv = _unit_lower_inverse(jnp.where(ri > ci, bcol * kk * decay, 0.0), dots)
    e_g = jnp.exp(g_cl)
    u = dots.nn(minv, v * bcol)
    wk = dots.nn(minv, k * (bcol * e_g))
    qk = dots.nt(q, k) * decay
    return u, wk, q * e_g, k * jnp.exp(g_last - g_cl), qk, jnp.exp(g_last)


def _gdn_scan(dots, states, u, wk, qd, kd, qk, gl_tile):
    lane, row = _iota(gl_tile.shape, 1), _iota(gl_tile.shape, 0)
    gl = _stack([
        jnp.sum(jnp.sum(jnp.where((lane == h) & (row == 0), gl_tile, 0.0), axis=1, keepdims=True),
                axis=0, keepdims=True) for h in range(GDN_HEADS)])
    v_new = u - dots.nn(wk, states)
    o = dots.nn(qd, states) + dots.nn(qk, v_new)
    return states * gl + dots.tn(kd, v_new), o


def _heads(x):
    return jnp.stack(_split(x, HW4))


GDN_W = GDN_HEADS * HEAD
HW4 = [HEAD] * GDN_HEADS
LOCAL_CHUNKS = 4
_CHUNK_ROWS = [pl.ds(cc * CHUNK, CHUNK) for cc in range(LOCAL_CHUNKS)]


def _chunk_heads(ref):
    return jnp.concatenate([_heads(ref[rows, :]) for rows in _CHUNK_ROWS], 0)


def _gdn_local_fwd(q, k, v, gb, name):
    s = q.shape[0]
    t = LOCAL_CHUNKS * CHUNK

    def body(q_ref, k_ref, v_ref, gb_ref, u_ref, wk_ref, qd_ref, kd_ref, qk_ref, gl_ref):
        u, wk, qd, kd, qk, gl = _gdn_local(_Dots(False), _chunk_heads(q_ref), _chunk_heads(k_ref),
                                           _chunk_heads(v_ref), [gb_ref[rows, :] for rows in _CHUNK_ROWS])
        lane = _iota((CHUNK, LANES), 1)
        for cc, rows in enumerate(_CHUNK_ROWS):
            gl_tile = jnp.zeros((CHUNK, LANES), F32)
            for h in range(GDN_HEADS):
                b, cols = cc * GDN_HEADS + h, pl.ds(h * HEAD, HEAD)
                u_ref[rows, cols] = u[b]
                wk_ref[rows, cols] = wk[b]
                qd_ref[rows, cols] = qd[b]
                kd_ref[rows, cols] = kd[b]
                qk_ref[h, rows, :] = qk[b]
                gl_tile = gl_tile + jnp.where(lane == h, gl[b], 0.0)
            gl_ref[rows, :] = gl_tile

    row = pl.BlockSpec((t, GDN_W), lambda i: (i, 0))
    lane = pl.BlockSpec((t, LANES), lambda i: (i, 0))
    qks = pl.BlockSpec((GDN_HEADS, t, CHUNK), lambda i: (0, i, 0))
    return pl.pallas_call(
        body, name=name, grid=(s // t,),
        in_specs=[row, row, row, lane],
        out_specs=[row, row, row, row, qks, lane],
        out_shape=[jax.ShapeDtypeStruct((s, GDN_W), F32)] * 4
        + [jax.ShapeDtypeStruct((GDN_HEADS, s, CHUNK), F32), jax.ShapeDtypeStruct((s, LANES), F32)],
        compiler_params=_params(("parallel",)),
    )(q, k, v, gb)


def _gdn_local_bwd(q, k, v, gb, du, dwk, dqd, dkd, dqk, dgl, name):
    s = q.shape[0]
    t = LOCAL_CHUNKS * CHUNK

    def body(q_ref, k_ref, v_ref, gb_ref, du_ref, dwk_ref, dqd_ref, dkd_ref, dqk_ref, dgl_ref,
             dq_ref, dk_ref, dv_ref, dgb_ref):
        _, vjp = jax.vjp(functools.partial(_gdn_local, _Dots(False)), _chunk_heads(q_ref), _chunk_heads(k_ref),
                         _chunk_heads(v_ref), [gb_ref[rows, :] for rows in _CHUNK_ROWS])
        lane = _iota((CHUNK, LANES), 1)
        dqk = jnp.stack([dqk_ref[h, rows, :] for rows in _CHUNK_ROWS for h in range(GDN_HEADS)])
        dgl = jnp.stack([jnp.sum(jnp.where(lane == h, dgl_ref[rows, :], 0.0), axis=0, keepdims=True)
                         for rows in _CHUNK_ROWS for h in range(GDN_HEADS)])
        d_q, d_k, d_v, d_gbs = vjp((_chunk_heads(du_ref), _chunk_heads(dwk_ref), _chunk_heads(dqd_ref),
                                    _chunk_heads(dkd_ref), dqk, dgl))
        for cc, rows in enumerate(_CHUNK_ROWS):
            for h in range(GDN_HEADS):
                b, cols = cc * GDN_HEADS + h, pl.ds(h * HEAD, HEAD)
                dq_ref[rows, cols] = d_q[b]
                dk_ref[rows, cols] = d_k[b]
                dv_ref[rows, cols] = d_v[b]
            dgb_ref[rows, :] = d_gbs[cc]

    row = pl.BlockSpec((t, GDN_W), lambda i: (i, 0))
    lane = pl.BlockSpec((t, LANES), lambda i: (i, 0))
    qks = pl.BlockSpec((GDN_HEADS, t, CHUNK), lambda i: (0, i, 0))
    return pl.pallas_call(
        body, name=name, grid=(s // t,),
        in_specs=[row, row, row, lane, row, row, row, row, qks, lane],
        out_specs=[row, row, row, lane],
        out_shape=[jax.ShapeDtypeStruct((s, GDN_W), F32)] * 3 + [jax.ShapeDtypeStruct((s, LANES), F32)],
        compiler_params=_params(("parallel",)),
    )(q, k, v, gb, du, dwk, dqd, dkd, dqk, dgl)


def _gdn_scan_fwd(u, wk, qd, kd, qk, gl, name):
    s = u.shape[0]
    nc = s // CHUNK

    def body(u_ref, wk_ref, qd_ref, kd_ref, qk_ref, gl_ref, o_ref, st_ref, state):
        i = pl.program_id(0)

        @pl.when(i == 0)
        def _():
            state[...] = jnp.zeros(state.shape, F32)

        st_ref[...] = state[...]
        new_states, o = _gdn_scan(_Dots(False), state[...], _heads(u_ref[...]), _heads(wk_ref[...]),
                                  _heads(qd_ref[...]), _heads(kd_ref[...]), qk_ref[...], gl_ref[...])
        state[...] = new_states
        o_ref[...] = jnp.concatenate([o[h] for h in range(GDN_HEADS)], 1)

    row = pl.BlockSpec((CHUNK, GDN_W), lambda i: (i, 0))
    return pl.pallas_call(
        body, name=name, grid=(nc,),
        in_specs=[row, row, row, row, pl.BlockSpec((GDN_HEADS, CHUNK, CHUNK), lambda i: (0, i, 0)),
                  pl.BlockSpec((CHUNK, LANES), lambda i: (i, 0))],
        out_specs=[row, pl.BlockSpec((None, GDN_HEADS, HEAD, HEAD), lambda i: (i, 0, 0, 0))],
        out_shape=[jax.ShapeDtypeStruct((s, GDN_W), F32),
                   jax.ShapeDtypeStruct((nc, GDN_HEADS, HEAD, HEAD), F32)],
        scratch_shapes=[pltpu.VMEM((GDN_HEADS, HEAD, HEAD), F32)],
        compiler_params=_params(("arbitrary",)),
    )(u, wk, qd, kd, qk, gl)


def _gdn_scan_bwd(u, wk, qd, kd, qk, gl, st, do, name):
    s = u.shape[0]
    nc = s // CHUNK

    def body(u_ref, wk_ref, qd_ref, kd_ref, qk_ref, gl_ref, st_ref, do_ref,
             du_ref, dwk_ref, dqd_ref, dkd_ref, dqk_ref, dgl_ref, dstate):
        i = pl.program_id(0)

        @pl.when(i == 0)
        def _():
            dstate[...] = jnp.zeros(dstate.shape, F32)

        _, vjp = jax.vjp(functools.partial(_gdn_scan, _Dots(False)), st_ref[...], _heads(u_ref[...]),
                         _heads(wk_ref[...]), _heads(qd_ref[...]), _heads(kd_ref[...]), qk_ref[...], gl_ref[...])
        d_states, d_u, d_wk, d_qd, d_kd, d_qk, d_gl = vjp((dstate[...], _heads(do_ref[...])))
        dstate[...] = d_states
        dqk_ref[...] = d_qk
        unheads = lambda x: jnp.concatenate([x[h] for h in range(GDN_HEADS)], 1)
        du_ref[...] = unheads(d_u)
        dwk_ref[...] = unheads(d_wk)
        dqd_ref[...] = unheads(d_qd)
        dkd_ref[...] = unheads(d_kd)
        dgl_ref[...] = d_gl

    rev = lambda i: (nc - 1 - i, 0)
    row = pl.BlockSpec((CHUNK, GDN_W), rev)
    lane = pl.BlockSpec((CHUNK, LANES), rev)
    qks = pl.BlockSpec((GDN_HEADS, CHUNK, CHUNK), lambda i: (0, nc - 1 - i, 0))
    return pl.pallas_call(
        body, name=name, grid=(nc,),
        in_specs=[row, row, row, row, qks, lane,
                  pl.BlockSpec((None, GDN_HEADS, HEAD, HEAD), lambda i: (nc - 1 - i, 0, 0, 0)), row],
        out_specs=[row, row, row, row, qks, lane],
        out_shape=[jax.ShapeDtypeStruct((s, GDN_W), F32)] * 4
        + [jax.ShapeDtypeStruct((GDN_HEADS, s, CHUNK), F32), jax.ShapeDtypeStruct((s, LANES), F32)],
        scratch_shapes=[pltpu.VMEM((GDN_HEADS, HEAD, HEAD), F32)],
        compiler_params=_params(("arbitrary",)),
    )(u, wk, qd, kd, qk, gl, st, do)


def _chunk_mask(i, j, t):
    r = i * t + _iota((t, t), 0)
    c = j * t + _iota((t, t), 1)
    return (r // CHUNK) >= (c // CHUNK)


ATT_TILE = 1024
ATT_Q_TILES = 1
ATT_BWD_TILE = 1024


def _attn_fwd(q, k, v, name):
    nh, s = MLA_HEADS, q.shape[0]
    tk = min(ATT_TILE, s)
    tq = min(ATT_Q_TILES * tk, s)
    qk = tq // tk
    nq, n = s // tq, s // tk
    nt = (((1,), (1,)), ((), ()))

    def body(q_ref, k_ref, v_ref, o_ref, lse_ref, m_sc, l_sc, acc_sc):
        i, j = pl.program_id(1), pl.program_id(2)

        @pl.when(j == 0)
        def _():
            m_sc[...] = jnp.full(m_sc.shape, -jnp.inf, F32)
            l_sc[...] = jnp.zeros(l_sc.shape, F32)
            acc_sc[...] = jnp.zeros(acc_sc.shape, F32)

        def step(masked):
            sc = lax.dot_general(q_ref[...], k_ref[...], nt, preferred_element_type=F32)
            if masked:
                r = i * tq + _iota((tq, tk), 0)
                c = j * tk + _iota((tq, tk), 1)
                sc = jnp.where((r // CHUNK) >= (c // CHUNK), sc, -jnp.inf)
            m_prev = m_sc[:, :1]
            m_new = jnp.maximum(m_prev, jnp.max(sc, axis=1, keepdims=True))
            alpha = jnp.exp(m_prev - m_new)
            p = jnp.exp(sc - m_new)
            l_sc[...] = jnp.broadcast_to(alpha * l_sc[:, :1] + jnp.sum(p, axis=1, keepdims=True), l_sc.shape)
            acc_sc[...] = alpha * acc_sc[...] + jnp.dot(p.astype(BF16), v_ref[...], preferred_element_type=F32)
            m_sc[...] = jnp.broadcast_to(m_new, m_sc.shape)

        pl.when(j < i * qk)(lambda: step(False))
        pl.when(j // qk == i)(lambda: step(True))

        @pl.when(j == n - 1)
        def _():
            o_ref[...] = acc_sc[...] / l_sc[:, :1]
            lse_ref[...] = m_sc[...] + jnp.log(l_sc[...])

    qrow = lambda h, i, j: (i, h)
    krow = lambda h, i, j: (jnp.minimum(j, (i + 1) * qk - 1), h)
    return pl.pallas_call(
        body, name=name, grid=(nh, nq, n),
        in_specs=[pl.BlockSpec((tq, QK_PAD), qrow), pl.BlockSpec((tk, QK_PAD), krow),
                  pl.BlockSpec((tk, HEAD), krow)],
        out_specs=[pl.BlockSpec((tq, HEAD), qrow), pl.BlockSpec((None, tq, LANES), lambda h, i, j: (h, i, 0))],
        out_shape=[jax.ShapeDtypeStruct((s, nh * HEAD), F32), jax.ShapeDtypeStruct((nh, s, LANES), F32)],
        scratch_shapes=[pltpu.VMEM((tq, LANES), F32), pltpu.VMEM((tq, LANES), F32), pltpu.VMEM((tq, HEAD), F32)],
        compiler_params=_params(("parallel", "parallel", "arbitrary")),
    )(q, k, v)


def _attn_bwd(q, k, v, o, do, lse, name):
    nh, s = MLA_HEADS, q.shape[0]
    t = min(ATT_BWD_TILE, s)
    n = s // t
    tn = (((0,), (0,)), ((), ()))
    nt = (((1,), (1,)), ((), ()))

    def body(q_ref, k_ref, v_ref, o_ref, do_ref, lse_ref, dq_ref, dk_ref, dv_ref, dk_acc, dv_acc, dq_acc):
        j, i = pl.program_id(1), pl.program_id(2)

        @pl.when(i + j == 0)
        def _():
            dq_acc[...] = jnp.zeros(dq_acc.shape, F32)

        @pl.when(i == 0)
        def _():
            dk_acc[...] = jnp.zeros(dk_acc.shape, F32)
            dv_acc[...] = jnp.zeros(dv_acc.shape, F32)

        def step(masked):
            qv, kv, do = q_ref[...], k_ref[...], do_ref[...]
            sc = lax.dot_general(qv, kv, nt, preferred_element_type=F32)
            p = jnp.exp(sc - lse_ref[:, :1])
            if masked:
                p = jnp.where(_chunk_mask(i, j, t), p, 0.0)
            dob = do.astype(BF16)
            dp = lax.dot_general(dob, v_ref[...], nt, preferred_element_type=F32)
            ds = (p * (dp - jnp.sum(do * o_ref[...], axis=1, keepdims=True))).astype(BF16)
            dv_acc[...] += lax.dot_general(p.astype(BF16), dob, tn, preferred_element_type=F32)
            dk_acc[...] += lax.dot_general(ds, qv, tn, preferred_element_type=F32)
            rows = pl.ds(pl.multiple_of(i * t, t), t)
            dq_acc[rows, :] += jnp.dot(ds, kv, preferred_element_type=F32)

        pl.when(i > j)(lambda: step(False))
        pl.when(i == j)(lambda: step(True))

        @pl.when(i == n - 1)
        def _():
            dk_ref[...] = dk_acc[...]
            dv_ref[...] = dv_acc[...]

        @pl.when(i + j == 2 * (n - 1))
        def _():
            dq_ref[...] = dq_acc[...]

    qrow = lambda h, j, i: (jnp.maximum(i, j), h)
    krow = lambda h, j, i: (j, h)
    return pl.pallas_call(
        body, name=name, grid=(nh, n, n),
        in_specs=[pl.BlockSpec((t, QK_PAD), qrow), pl.BlockSpec((t, QK_PAD), krow), pl.BlockSpec((t, HEAD), krow),
                  pl.BlockSpec((t, HEAD), qrow), pl.BlockSpec((t, HEAD), qrow),
                  pl.BlockSpec((None, t, LANES), lambda h, j, i: (h, jnp.maximum(i, j), 0))],
        out_specs=[pl.BlockSpec((s, QK_PAD), lambda h, j, i: (0, h)),
                   pl.BlockSpec((t, QK_PAD), krow), pl.BlockSpec((t, HEAD), krow)],
        out_shape=[jax.ShapeDtypeStruct((s, nh * QK_PAD), F32), jax.ShapeDtypeStruct((s, nh * QK_PAD), F32),
                   jax.ShapeDtypeStruct((s, nh * HEAD), F32)],
        scratch_shapes=[pltpu.VMEM((t, QK_PAD), F32), pltpu.VMEM((t, HEAD), F32), pltpu.VMEM((s, QK_PAD), F32)],
        compiler_params=_params(("arbitrary", "arbitrary", "arbitrary")),
    )(q, k, v, o, do, lse)


def _place():
    return lax.axis_index("x"), lax.axis_index("y"), lax.axis_index("c")


def _allgather8(x, name):
    r, c = x.shape

    def body(x_ref, out_ref, send_sems, recv_sems, local_sem):
        mx, my, mc = _place()
        me = 4 * mx + 2 * my + mc
        mine = pltpu.make_async_copy(x_ref, out_ref.at[me], local_sem)
        mine.start()
        copies = []
        for d in range(1, 8):
            px = 1 - mx if d & 4 else mx
            py = 1 - my if d & 2 else my
            pc = 1 - mc if d & 1 else mc
            cp = pltpu.make_async_remote_copy(
                src_ref=x_ref, dst_ref=out_ref.at[me], send_sem=send_sems.at[d - 1], recv_sem=recv_sems.at[d - 1],
                device_id=(px, py, pc), device_id_type=MESH)
            cp.start()
            copies.append(cp)
        for cp in copies:
            cp.wait()
        mine.wait()

    return pl.pallas_call(
        body, name=name,
        out_shape=jax.ShapeDtypeStruct((8, r, c), x.dtype),
        in_specs=[pl.BlockSpec(memory_space=pltpu.VMEM)],
        out_specs=pl.BlockSpec(memory_space=pltpu.VMEM),
        scratch_shapes=[pltpu.SemaphoreType.DMA((7,)), pltpu.SemaphoreType.DMA((7,)), pltpu.SemaphoreType.DMA],
        compiler_params=pltpu.CompilerParams(vmem_limit_bytes=VMEM_LIMIT),
    )(x)


def _allgather_chips(x, name):
    r, c = x.shape

    def body(x_ref, out_ref, send_sems, recv_sems, local_sems):
        for phase in range(3):
            _gather_phase(phase, x_ref, out_ref, send_sems, recv_sems, local_sems, 0)

    return pl.pallas_call(
        body, name=name,
        out_shape=jax.ShapeDtypeStruct((4, r, c), x.dtype),
        in_specs=[pl.BlockSpec(memory_space=pltpu.VMEM)],
        out_specs=pl.BlockSpec(memory_space=pltpu.VMEM),
        scratch_shapes=_gather_sems(1),
        compiler_params=pltpu.CompilerParams(vmem_limit_bytes=VMEM_LIMIT),
    )(x)


GATHER_COPIES = 6


def _gather_sems(n):
    return [pltpu.SemaphoreType.DMA((GATHER_COPIES * n,)), pltpu.SemaphoreType.DMA((GATHER_COPIES * n,)),
            pltpu.SemaphoreType.DMA((n,))]


def _gather_phase(phase, x_ref, out_ref, send_sems, recv_sems, local_sems, slot):
    mx, my, mc = _place()
    j = 2 * mx + my
    rh = x_ref.shape[0] // 2
    base = GATHER_COPIES * slot
    chips = [(1 - mx, my), (mx, 1 - my), (1 - mx, 1 - my)]
    sibling = (mx, my, 1 - mc)

    def half(jj, hc):
        return out_ref.at[jj, pl.ds(hc * rh, rh), :]

    def over_ici(kk, block):
        px, py = chips[kk]
        return pltpu.make_async_remote_copy(
            src_ref=x_ref.at[pl.ds(mc * rh, rh), :], dst_ref=half(block, mc), send_sem=send_sems.at[base + kk],
            recv_sem=recv_sems.at[base + kk], device_id=(px, py, mc), device_id_type=MESH)

    def to_sibling(kk, hc):
        px, py = chips[kk]
        blk = half(2 * px + py, hc)
        return pltpu.make_async_remote_copy(
            src_ref=blk, dst_ref=blk, send_sem=send_sems.at[base + 3 + kk], recv_sem=recv_sems.at[base + 3 + kk],
            device_id=sibling, device_id_type=MESH)

    mine = pltpu.make_async_copy(x_ref, out_ref.at[j], local_sems.at[slot])
    if phase == 0:
        mine.start()
        for kk in range(3):
            over_ici(kk, j).start()
    elif phase == 1:
        for kk, (px, py) in enumerate(chips):
            over_ici(kk, 2 * px + py).wait_recv()
            to_sibling(kk, mc).start()
    else:
        for kk in range(3):
            to_sibling(kk, 1 - mc).wait_recv()
        for kk in range(3):
            over_ici(kk, j).wait_send()
            to_sibling(kk, mc).wait_send()
        mine.wait()


RS_ROWS = 32


def _reduce_scatter_chips(g, name):
    _, r, c = g.shape
    rh = r // 2
    steps = rh // RS_ROWS

    def body(g_ref, out_ref, sib_ref, part_ref, got_ref, send_sems, recv_sems):
        mx, my, mc = _place()
        j = 2 * mx + my
        sibling = (mx, my, 1 - mc)
        chips = [(1 - mx, my), (mx, 1 - my), (1 - mx, 1 - my)]

        to_sib = pltpu.make_async_remote_copy(
            src_ref=g_ref.at[:, pl.ds((1 - mc) * rh, rh), :], dst_ref=sib_ref,
            send_sem=send_sems.at[0], recv_sem=recv_sems.at[0], device_id=sibling, device_id_type=MESH)
        to_sib.start()
        to_sib.wait()

        def add_sibling(step, carry):
            rows = pl.ds(pl.multiple_of(step * RS_ROWS, RS_ROWS), RS_ROWS)
            mine = g_ref[:, pl.ds(pl.multiple_of(mc * rh + step * RS_ROWS, RS_ROWS), RS_ROWS), :]
            part_ref[:, rows, :] = mine.astype(F32) + sib_ref[:, rows, :].astype(F32)
            return carry

        lax.fori_loop(0, steps, add_sibling, 0)

        def to_bf16(step, carry):
            rows = pl.ds(pl.multiple_of(step * RS_ROWS, RS_ROWS), RS_ROWS)
            sib_ref[:, rows, :] = part_ref[:, rows, :].astype(BF16)
            return carry

        lax.fori_loop(0, steps, to_bf16, 0)

        sends = []
        for kk, (px, py) in enumerate(chips):
            cp = pltpu.make_async_remote_copy(
                src_ref=sib_ref.at[2 * px + py], dst_ref=got_ref.at[kk],
                send_sem=send_sems.at[1 + kk], recv_sem=recv_sems.at[1 + kk],
                device_id=(px, py, mc), device_id_type=MESH)
            cp.start()
            sends.append(cp)
        for cp in sends:
            cp.wait()

        def total(step, carry):
            rows = pl.ds(pl.multiple_of(step * RS_ROWS, RS_ROWS), RS_ROWS)
            acc = part_ref[j, rows, :]
            for kk in range(3):
                acc = acc + got_ref[kk, rows, :].astype(F32)
            out_ref[pl.ds(pl.multiple_of(mc * rh + step * RS_ROWS, RS_ROWS), RS_ROWS), :] = acc
            return carry

        lax.fori_loop(0, steps, total, 0)

        done = pltpu.make_async_remote_copy(
            src_ref=out_ref.at[pl.ds(mc * rh, rh), :], dst_ref=out_ref.at[pl.ds(mc * rh, rh), :],
            send_sem=send_sems.at[4], recv_sem=recv_sems.at[4], device_id=sibling, device_id_type=MESH)
        done.start()
        done.wait_send()
        pltpu.make_async_remote_copy(
            src_ref=out_ref.at[pl.ds((1 - mc) * rh, rh), :], dst_ref=out_ref.at[pl.ds((1 - mc) * rh, rh), :],
            send_sem=send_sems.at[4], recv_sem=recv_sems.at[4], device_id=sibling, device_id_type=MESH).wait_recv()

    return pl.pallas_call(
        body, name=name,
        out_shape=jax.ShapeDtypeStruct((r, c), F32),
        in_specs=[pl.BlockSpec(memory_space=pltpu.VMEM)],
        out_specs=pl.BlockSpec(memory_space=pltpu.VMEM),
        scratch_shapes=[pltpu.VMEM((4, rh, c), BF16), pltpu.VMEM((4, rh, c), F32), pltpu.VMEM((3, rh, c), BF16),
                        pltpu.SemaphoreType.DMA((5,)), pltpu.SemaphoreType.DMA((5,))],
        compiler_params=pltpu.CompilerParams(vmem_limit_bytes=VMEM_LIMIT),
    )(g)


def _sum8(x, name):
    _, r, c = x.shape

    def body(x_ref, o_ref):
        acc = x_ref[0]
        for d in range(1, 8):
            acc = acc + x_ref[d]
        o_ref[...] = acc

    return pl.pallas_call(
        body, name=name, out_shape=jax.ShapeDtypeStruct((r, c), F32),
        in_specs=[pl.BlockSpec(memory_space=pltpu.VMEM)], out_specs=pl.BlockSpec(memory_space=pltpu.VMEM),
    )(x)


def _adamw(w, g, m, v, name):
    r, c = w.shape
    t = _pick(r, 256, SUBLANES)
    spec = pl.BlockSpec((t, c), lambda i: (i, 0))

    def body(w_ref, g_ref, m_ref, v_ref, d_ref, nm_ref, nv_ref):
        gv = g_ref[...]
        m_new = ADAM_B1 * m_ref[...] + (1.0 - ADAM_B1) * gv
        v_new = ADAM_B2 * v_ref[...] + (1.0 - ADAM_B2) * (gv * gv)
        m_hat = m_new / (1.0 - ADAM_B1 ** ADAM_STEP)
        v_hat = v_new / (1.0 - ADAM_B2 ** ADAM_STEP)
        d_ref[...] = -ADAM_LR * (m_hat / (jnp.sqrt(v_hat) + ADAM_EPS) + ADAM_WD * w_ref[...])
        nm_ref[...] = m_new
        nv_ref[...] = v_new

    return pl.pallas_call(
        body, name=name, grid=(r // t,), in_specs=[spec] * 4, out_specs=[spec] * 3,
        out_shape=[jax.ShapeDtypeStruct((r, c), F32)] * 3, compiler_params=_params(("parallel",)),
    )(w, g, m, v)


def _pack_rows(parts):
    rows, offs, o = [], [], 0
    for p in parts:
        f = p.reshape(-1)
        n = -(-f.shape[0] // (LANES * SUBLANES)) * SUBLANES
        rows.append(jnp.pad(f, (0, n * LANES - f.shape[0])).reshape(n, LANES))
        offs.append((o, n))
        o += n
    return jnp.concatenate(rows, 0), offs


def _unpack_rows(packed, offs, shapes):
    out = []
    for (o, n), shp in zip(offs, shapes):
        size = 1
        for d in shp:
            size *= d
        out.append(packed[o:o + n].reshape(-1)[:size].reshape(shp))
    return out


def _mm_hosting(a, b, mode, out_dtype, name, gather):
    res = _mm(a, b, mode, out_dtype, name, gather=gather)
    return (res[0], list(res[1:])) if gather else (res, [])


def _ffn_fwd(x, s, sh, g, w_in, w_out, tag, gather_in=(), gather_out=()):
    (h,) = _rowcall(lambda r, p: ([_modulate(r[0], p[0], p[1])], []), [x], [s, sh], [(x.shape[1], BF16)], [],
                    tile=512, name=tag + "_mod")
    gu, got = _mm_hosting(h, w_in, "nn", BF16, tag + "_in", gather_in)
    (act,) = _rowcall(lambda r, p: ([_silu(r[0].astype(F32)) * r[1].astype(F32)], []),
                      [(gu, D_FF, 0), (gu, D_FF, 1)], [], [(D_FF, BF16)], [], tile=256, name=tag + "_act")
    f, got_out = _mm_hosting(act, w_out, "nn", F32, tag + "_out", gather_out)
    got = got + got_out
    (y,) = _rowcall(lambda r, p: ([r[0] + 0.5 * p[0] * r[1]], []), [x, f], [g], [(x.shape[1], F32)], [],
                    tile=512, name=tag + "_res")
    return y, (x, h, gu, act, f), got


def _ffn_bwd(dy, saved, s, sh, g, w_in, w_out, tag):
    x, h, gu, act, f = saved
    d = x.shape[1]
    df, dg = _rowcall(lambda r, p: ([0.5 * p[0] * r[0]], [0.5 * jnp.sum(r[0] * r[1], 0, keepdims=True)]),
                      [dy, f], [g], [(d, BF16)], [(1, d)], tile=512, name=tag + "_bres")
    da = _mm(df, w_out, "nt", BF16, tag + "_bout")
    dw_out = _mm(act, df, "tn", BF16, tag + "_bwout")

    def act_bwd(r, p):
        gate, up, dav = r[0].astype(F32), r[1].astype(F32), r[2].astype(F32)
        _, vjp = jax.vjp(lambda a, b: _silu(a) * b, gate, up)
        dgate, dup = vjp(dav)
        return [jnp.concatenate([dgate, dup], 1)], []

    (dgu,) = _rowcall(act_bwd, [(gu, D_FF, 0), (gu, D_FF, 1), da], [], [(2 * D_FF, BF16)], [], tile=256,
                      name=tag + "_bact")
    dh = _mm(dgu, w_in, "nt", F32, tag + "_bin")
    dw_in = _mm(h, dgu, "tn", BF16, tag + "_bwin")

    def mod_bwd(r, p):
        _, vjp = jax.vjp(_modulate, r[0], p[0], p[1])
        dx, ds, dsh = vjp(r[1])
        return [r[2] + dx], [ds, dsh]

    dx, ds, dsh = _rowcall(mod_bwd, [x, dh, dy], [s, sh], [(d, F32)], [(1, d), (1, d)], tile=512, name=tag + "_bmod")
    return dx, (dsh, ds, dg), dw_in, dw_out


def _mixer_fwd(x, s, sh, g, wts, rope, gather=()):
    w_in_p, conv8, a_log, dt_bias, wn, wq, w_uq_p, wkv, w_ukv, wqn, wqr, wkn, wkr, won, w_out = wts
    cos2, sin2 = rope
    d = x.shape[1]
    (h,) = _rowcall(lambda r, p: ([_modulate(r[0], p[0], p[1])], []), [x], [s, sh], [(d, BF16)], [],
                    tile=512, name="mix_mod")
    proj, got = _mm_hosting(h, w_in_p, "nn", F32, "mix_in", gather)
    qkv_c = _conv_fwd(proj, conv8, "mix_conv")
    gab = (proj, LANES, 23)

    q, k, v, gb = _rowcall(
        lambda r, p: (list(_gdn_prep_core(_split(r[0], [HEAD] * 12), r[1], p[0], p[1])), []),
        [qkv_c, gab], [a_log, dt_bias], [(512, F32)] * 3 + [(LANES, F32)], [], tile=256, name="mix_gdn_prep")
    gdn_local = _gdn_local_fwd(q, k, v, gb, "mix_gdn_local")
    o_gdn, gdn_states = _gdn_scan_fwd(*gdn_local, "mix_gdn_scan")
    states = (gdn_local, gdn_states)

    cq, ckv, kr = (proj, 512, 4), (proj, 256, 10), (proj, LANES, 22)
    cqn, ckvn, k_rope = _rowcall(
        lambda r, p: (list(_mla_prep_core(r[0][:, :MLA_Q_LORA], r[1], r[2], r[3], r[4], p[0], p[1], p[2])), []),
        [cq, ckv, kr, cos2, sin2], [wq, wkv, wkr], [(MLA_Q_LORA, BF16), (MLA_KV_LORA, BF16), (LANES, F32)], [],
        tile=512, name="mix_mla_prep")
    qf = _mm(cqn, w_uq_p, "nn", F32, "mix_uq")
    kvf = _mm(ckvn, w_ukv, "nn", F32, "mix_ukv")

    def qk_prep(r, p):
        qparts = _split(r[0], [HEAD] * 8)
        kvparts = _split(r[1], [HEAD] * 8)
        qs, ks, vs = _qk_prep_core(qparts[:4], qparts[4:], kvparts[0::2], kvparts[1::2], r[2], r[3], r[4],
                                   p[0], p[1], p[2])
        return [jnp.concatenate(qs, 1), jnp.concatenate(ks, 1), jnp.concatenate(vs, 1)], []

    qa, ka, va = _rowcall(qk_prep, [qf, kvf, k_rope, cos2, sin2], [wqn, wqr, wkn],
                          [(4 * QK_PAD, BF16), (4 * QK_PAD, BF16), (4 * HEAD, BF16)], [], tile=256,
                          name="mix_qk_prep")
    o_b, lse = _attn_fwd(qa, ka, va, "mix_attn")

    gz = (proj, 512, 3)
    (mixed,) = _rowcall(
        lambda r, p: ([_mix_post_core(_split(r[0], HW4), _split(r[1], HW4), _split(r[2], HW4), p[0], p[1])], []),
        [o_gdn, gz, o_b], [wn, won], [(2 * 512, BF16)], [], tile=512, name="mix_post")
    y = _mm(mixed, w_out, "nn", F32, "mix_out")
    (x_out,) = _rowcall(lambda r, p: ([r[0] + p[0] * r[1]], []), [x, y], [g], [(d, F32)], [], tile=512,
                        name="mix_res")
    saved = (x, h, proj, qkv_c, q, k, v, gb, states, o_gdn, cqn, ckvn, k_rope, qf, kvf, qa, ka, va, o_b, lse,
             mixed, y)
    return x_out, saved, got


def _mixer_bwd(dy, saved, s, sh, g, wts, rope):
    w_in_p, conv8, a_log, dt_bias, wn, wq, w_uq_p, wkv, w_ukv, wqn, wqr, wkn, wkr, won, w_out = wts
    cos2, sin2 = rope
    (x, h, proj, qkv_c, q, k, v, gb, states, o_gdn, cqn, ckvn, k_rope, qf, kvf, qa, ka, va, o_b, lse,
     mixed, y) = saved
    d = x.shape[1]
    dyb, dg = _rowcall(lambda r, p: ([p[0] * r[0]], [jnp.sum(r[0] * r[1], 0, keepdims=True)]),
                       [dy, y], [g], [(d, BF16)], [(1, d)], tile=512, name="mix_bres")
    dmixed = _mm(dyb, w_out, "nt", F32, "mix_bout")
    dw_out = _mm(mixed, dyb, "tn", BF16, "mix_bwout")

    gz = (proj, 512, 3)

    def post_bwd(r, p):
        _, vjp = jax.vjp(_mix_post_core, _split(r[0], HW4), _split(r[1], HW4), _split(r[2], HW4), p[0], p[1])
        do, dz, dob, dwn, dwon = vjp(r[3])
        return [jnp.concatenate(do, 1), jnp.concatenate(dz, 1), jnp.concatenate(dob, 1)], [dwn, dwon]

    do_gdn, dgz, do_b, dwn, dwon = _rowcall(post_bwd, [o_gdn, gz, o_b, dmixed], [wn, won], [(512, F32)] * 3,
                                            [(1, HEAD), (1, HEAD)], tile=256, name="mix_bpost")

    dqa, dka, dva = _attn_bwd(qa, ka, va, o_b, do_b, lse, "mix_battn")

    def qk_bwd(r, p):
        qparts = _split(r[0], [HEAD] * 8)
        kvparts = _split(r[1], [HEAD] * 8)
        _, vjp = jax.vjp(_qk_prep_core, qparts[:4], qparts[4:], kvparts[0::2], kvparts[1::2], r[2], r[3], r[4],
                         p[0], p[1], p[2])
        cot = (_split(r[5], [QK_PAD] * 4), _split(r[6], [QK_PAD] * 4), _split(r[7], HW4))
        dqn, dqr, dkn, dvp, dkrope, _, _, dwqn, dwqr, dwkn = vjp(cot)
        dkv = []
        for a, b in zip(dkn, dvp):
            dkv += [a, b]
        return [jnp.concatenate(list(dqn) + list(dqr), 1), jnp.concatenate(dkv, 1), dkrope], [dwqn, dwqr, dwkn]

    dqf, dkvf, dk_rope, dwqn, dwqr, dwkn = _rowcall(
        qk_bwd, [qf, kvf, k_rope, cos2, sin2, dqa, dka, dva], [wqn, wqr, wkn],
        [(8 * HEAD, BF16), (8 * HEAD, BF16), (LANES, F32)], [(1, HEAD)] * 3, tile=256, name="mix_bqk_prep")
    dcqn = _mm(dqf, w_uq_p, "nt", F32, "mix_buq")
    dw_uq_p = _mm(cqn, dqf, "tn", F32, "mix_bwuq")
    dckvn = _mm(dkvf, w_ukv, "nt", F32, "mix_bukv")
    dw_ukv = _mm(ckvn, dkvf, "tn", F32, "mix_bwukv")

    cq, ckv, kr = (proj, 512, 4), (proj, 256, 10), (proj, LANES, 22)

    def mla_bwd(r, p):
        _, vjp = jax.vjp(_mla_prep_core, r[0][:, :MLA_Q_LORA], r[1], r[2], r[3], r[4], p[0], p[1], p[2])
        dcq, dckv, dkr, _, _, dwq, dwkv, dwkr = vjp((r[5], r[6], r[7]))
        pad = jnp.zeros((dcq.shape[0], 512 - MLA_Q_LORA), F32)
        return [jnp.concatenate([dcq, pad], 1), dckv, dkr], [dwq, dwkv, dwkr]

    dcq, dckv, dkr, dwq, dwkv, dwkr = _rowcall(
        mla_bwd, [cq, ckv, kr, cos2, sin2, dcqn, dckvn, dk_rope], [wq, wkv, wkr],
        [(512, F32), (MLA_KV_LORA, F32), (LANES, F32)], [(1, MLA_Q_LORA), (1, MLA_KV_LORA), (1, LANES)],
        tile=512, name="mix_bmla_prep")

    gdn_local, gdn_states = states
    d_local = _gdn_scan_bwd(*gdn_local, gdn_states, do_gdn, "mix_bgdn_scan")
    dq, dk, dv, dgb = _gdn_local_bwd(q, k, v, gb, *d_local, "mix_bgdn_local")
    gab = (proj, LANES, 23)

    def gdn_prep_bwd(r, p):
        _, vjp = jax.vjp(_gdn_prep_core, _split(r[0], [HEAD] * 12), r[1], p[0], p[1])
        dparts, dgab, da_log, ddt = vjp((r[2], r[3], r[4], r[5]))
        return [jnp.concatenate(dparts, 1), dgab], [da_log, ddt]

    dqkv_c, dgab, da_log, ddt = _rowcall(gdn_prep_bwd, [qkv_c, gab, dq, dk, dv, dgb], [a_log, dt_bias],
                                         [(1536, F32), (LANES, F32)], [(1, LANES), (1, LANES)], tile=256,
                                         name="mix_bgdn_prep")
    dqkv_pre, dconv8 = _conv_bwd(proj, dqkv_c, conv8, "mix_bconv")

    dproj = jnp.concatenate([dqkv_pre.astype(BF16), dgz.astype(BF16), dcq.astype(BF16), dckv.astype(BF16),
                             dkr.astype(BF16), dgab.astype(BF16)], axis=1)
    dh = _mm(dproj, w_in_p, "nt", F32, "mix_bin")
    dw_in_p = _mm(h, dproj, "tn", F32, "mix_bwin")

    def mod_bwd(r, p):
        _, vjp = jax.vjp(_modulate, r[0], p[0], p[1])
        dx, ds, dsh = vjp(r[1])
        return [r[2] + dx], [ds, dsh]

    dx, ds, dsh = _rowcall(mod_bwd, [x, dh, dy], [s, sh], [(d, F32)], [(1, d), (1, d)], tile=512, name="mix_bmod")
    small = dict(conv=dconv8, a_log=da_log, dt=ddt, wn=dwn, wq=dwq, wkv=dwkv, wqn=dwqn, wqr=dwqr, wkn=dwkn,
                 wkr=dwkr, won=dwon)
    return dx, (dsh, ds, dg), dw_in_p, dw_uq_p, dw_ukv, dw_out, small


def _pad_cols(a, n):
    return jnp.pad(a, ((0, 0),) * (a.ndim - 1) + ((0, n - a.shape[-1]),))


def _pack_w_in(w):
    z = lambda n: jnp.zeros((w.shape[0], n), w.dtype)
    return jnp.concatenate([w[:, 0:2048], w[:, 2056:2440], z(128), w[:, 2440:2696], w[:, 2696:2760], z(64),
                            w[:, 2048:2056], z(120)], axis=1)


def _unpack_w_in(wp):
    return jnp.concatenate([wp[:, 0:2048], wp[:, 2944:2952], wp[:, 2048:2432], wp[:, 2560:2816], wp[:, 2816:2880]],
                           axis=1)


def _pack_w_uq(w):
    z = jnp.zeros((w.shape[0], LANES - MLA_ROPE), w.dtype)
    nope = [w[:, h * 192:h * 192 + HEAD] for h in range(MLA_HEADS)]
    rope = []
    for h in range(MLA_HEADS):
        rope += [w[:, h * 192 + HEAD:(h + 1) * 192], z]
    return jnp.concatenate(nope + rope, axis=1)


def _unpack_w_uq(wp):
    cols = []
    for h in range(MLA_HEADS):
        cols += [wp[:, h * HEAD:(h + 1) * HEAD], wp[:, 512 + h * LANES:512 + h * LANES + MLA_ROPE]]
    return jnp.concatenate(cols, axis=1)


def _cols_to_chips(a):
    r, c = a.shape
    return a.reshape(r, 4, c // 4).transpose(1, 0, 2)


def _chips_to_cols(a):
    _, r, n = a.shape
    return a.transpose(1, 0, 2).reshape(r, 4 * n)


def _pad128(v, n=LANES):
    return _pad_cols(v.reshape(1, -1), n)


def kernel(x, c, positions, w_ada, b_ada, ffn1_w_in, ffn1_w_out, w_in, gdn_conv_w, gdn_a_log, gdn_dt_bias, gdn_norm_w, mla_q_norm_w, mla_w_uq, mla_kv_norm_w, mla_w_ukv, qkn_q_nope, qkn_q_rope, qkn_k_nope, qkn_k_rope, mla_out_norm_w, w_out, ffn2_w_in, ffn2_w_out, loss_target, m_w_ada, m_b_ada, m_ffn1_w_in, m_ffn1_w_out, m_w_in, m_gdn_conv_w, m_gdn_a_log, m_gdn_dt_bias, m_gdn_norm_w, m_mla_q_norm_w, m_mla_w_uq, m_mla_kv_norm_w, m_mla_w_ukv, m_qkn_q_nope, m_qkn_q_rope, m_qkn_k_nope, m_qkn_k_rope, m_mla_out_norm_w, m_w_out, m_ffn2_w_in, m_ffn2_w_out, v_w_ada, v_b_ada, v_ffn1_w_in, v_ffn1_w_out, v_w_in, v_gdn_conv_w, v_gdn_a_log, v_gdn_dt_bias, v_gdn_norm_w, v_mla_q_norm_w, v_mla_w_uq, v_mla_kv_norm_w, v_mla_w_ukv, v_qkn_q_nope, v_qkn_q_rope, v_qkn_k_nope, v_qkn_k_rope, v_mla_out_norm_w, v_w_out, v_ffn2_w_in, v_ffn2_w_out):
    weights = dict(w_ada=w_ada, b_ada=b_ada, ffn1_w_in=ffn1_w_in, ffn1_w_out=ffn1_w_out, w_in=w_in,
                   gdn_conv_w=gdn_conv_w, gdn_a_log=gdn_a_log, gdn_dt_bias=gdn_dt_bias, gdn_norm_w=gdn_norm_w,
                   mla_q_norm_w=mla_q_norm_w, mla_w_uq=mla_w_uq, mla_kv_norm_w=mla_kv_norm_w, mla_w_ukv=mla_w_ukv,
                   qkn_q_nope=qkn_q_nope, qkn_q_rope=qkn_q_rope, qkn_k_nope=qkn_k_nope, qkn_k_rope=qkn_k_rope,
                   mla_out_norm_w=mla_out_norm_w, w_out=w_out, ffn2_w_in=ffn2_w_in, ffn2_w_out=ffn2_w_out)
    moms_m = dict(w_ada=m_w_ada, b_ada=m_b_ada, ffn1_w_in=m_ffn1_w_in, ffn1_w_out=m_ffn1_w_out, w_in=m_w_in,
                  gdn_conv_w=m_gdn_conv_w, gdn_a_log=m_gdn_a_log, gdn_dt_bias=m_gdn_dt_bias,
                  gdn_norm_w=m_gdn_norm_w, mla_q_norm_w=m_mla_q_norm_w, mla_w_uq=m_mla_w_uq,
                  mla_kv_norm_w=m_mla_kv_norm_w, mla_w_ukv=m_mla_w_ukv, qkn_q_nope=m_qkn_q_nope,
                  qkn_q_rope=m_qkn_q_rope, qkn_k_nope=m_qkn_k_nope, qkn_k_rope=m_qkn_k_rope,
                  mla_out_norm_w=m_mla_out_norm_w, w_out=m_w_out, ffn2_w_in=m_ffn2_w_in, ffn2_w_out=m_ffn2_w_out)
    moms_v = dict(w_ada=v_w_ada, b_ada=v_b_ada, ffn1_w_in=v_ffn1_w_in, ffn1_w_out=v_ffn1_w_out, w_in=v_w_in,
                  gdn_conv_w=v_gdn_conv_w, gdn_a_log=v_gdn_a_log, gdn_dt_bias=v_gdn_dt_bias,
                  gdn_norm_w=v_gdn_norm_w, mla_q_norm_w=v_mla_q_norm_w, mla_w_uq=v_mla_w_uq,
                  mla_kv_norm_w=v_mla_kv_norm_w, mla_w_ukv=v_mla_w_ukv, qkn_q_nope=v_qkn_q_nope,
                  qkn_q_rope=v_qkn_q_rope, qkn_k_nope=v_qkn_k_nope, qkn_k_rope=v_qkn_k_rope,
                  mla_out_norm_w=v_mla_out_norm_w, w_out=v_w_out, ffn2_w_in=v_ffn2_w_in, ffn2_w_out=v_ffn2_w_out)
    names = list(weights)

    seq, d = x.shape[1], x.shape[2]
    x2d = x.reshape(seq, d)
    tgt = loss_target.reshape(seq, d)
    mx, my, mc = _place()
    chip = 2 * mx + my
    me = 2 * chip + mc
    n_mod = b_ada.shape[1] // d
    shard = w_ada.shape[2]

    half = MLA_ROPE // 2
    inv_freq = 10000.0 ** (-jnp.arange(half, dtype=F32) / half)
    ang = positions.astype(F32).reshape(seq, 1) * inv_freq
    cosv, sinv = jnp.cos(ang), jnp.sin(ang)
    cos2 = _pad_cols(jnp.concatenate([cosv, cosv], 1), LANES)
    sin2 = _pad_cols(jnp.concatenate([-sinv, sinv], 1), LANES)
    rope = (cos2, sin2)

    c_all = _allgather8(jnp.pad(c, ((0, SUBLANES - 1), (0, 0))), "gather_c")[:, 0, :]
    (sc_all,) = _rowcall(lambda r, p: ([_silu(r[0])], []), [c_all], [], [(d, F32)], [], tile=8, name="ada_silu")
    mod_part = _mm(sc_all, w_ada[0], "nn", F32, "ada_mm", hi=True)
    mod_all = _allgather8(mod_part, "gather_mod")
    mod_rows = lax.dynamic_index_in_dim(mod_all, me, axis=1, keepdims=False)
    mod_raw = jnp.concatenate([mod_rows[2 * jj] for jj in range(4)], axis=0).reshape(1, 4 * shard)
    (mod,) = _rowcall(lambda r, p: ([r[0] + r[1]], []),
                      [jnp.pad(mod_raw, ((0, 7), (0, 0))), jnp.pad(b_ada, ((0, 7), (0, 0)))], [],
                      [(4 * shard, F32)], [], tile=8, name="ada_bias")
    mods = [mod[0:1, i * d:(i + 1) * d] for i in range(n_mod)]
    sh1, s1, g1, sh2, s2, g2, sh3, s3, g3 = mods

    def shard_bf16(w, pad_to=None):
        w2 = w[0].astype(BF16)
        return _pad_cols(w2, pad_to) if pad_to else w2

    def cols_of(got, w):
        return _chips_to_cols(got[:, :, :w.shape[2]])

    def rows_of(got):
        return got.reshape(4 * got.shape[1], got.shape[2])

    f1_in = cols_of(_allgather_chips(shard_bf16(ffn1_w_in), "gather_f1_in"), ffn1_w_in)
    f1_out = rows_of(_allgather_chips(shard_bf16(ffn1_w_out), "gather_f1_out"))
    mixer_shards = [shard_bf16(w_in, 768), shard_bf16(mla_w_uq, 256), shard_bf16(mla_w_ukv), shard_bf16(w_out)]
    conv_all = _allgather8(jnp.pad(gdn_conv_w[0], ((0, SUBLANES - CONV_K), (0, 0))), "gather_conv")
    conv8 = jnp.concatenate([conv_all[2 * jj] for jj in range(4)], axis=1)

    x1, sv1, got = _ffn_fwd(x2d, s1, sh1, g1, f1_in, f1_out, "ffn1", gather_in=mixer_shards,
                            gather_out=[shard_bf16(ffn2_w_out)])
    w_in_full, w_uq_full, w_ukv_full = cols_of(got[0], w_in), cols_of(got[1], mla_w_uq), cols_of(got[2], mla_w_ukv)
    w_out_full, f2_out = rows_of(got[3]), rows_of(got[4])
    wts = (_pack_w_in(w_in_full), conv8, _pad128(gdn_a_log), _pad128(gdn_dt_bias), gdn_norm_w,
           mla_q_norm_w, _pack_w_uq(w_uq_full), mla_kv_norm_w, w_ukv_full, qkn_q_nope, _pad128(qkn_q_rope),
           qkn_k_nope, _pad128(qkn_k_rope), mla_out_norm_w, w_out_full)
    xm, svm, got = _mixer_fwd(x1, s2, sh2, g2, wts, rope, gather=[shard_bf16(ffn2_w_in)])
    f2_in = cols_of(got[0], ffn2_w_in)
    x3, sv3, _ = _ffn_fwd(xm, s3, sh3, g3, f2_in, f2_out, "ffn2")

    def loss_fn(r, p):
        err = r[0] - r[1]
        part = 0.5 * jnp.sum(jnp.sum(err * err, axis=1, keepdims=True) * (1.0 / d), axis=0, keepdims=True)
        return [err * (1.0 / d)], [jnp.broadcast_to(part, (1, LANES))]

    dy, loss_part = _rowcall(loss_fn, [x3, tgt], [], [(d, F32)], [(1, LANES)], tile=512, name="loss")
    loss = lax.psum(loss_part[0, 0], ("x", "y", "c"))

    dxm, dmod3, dw_f2_in, dw_f2_out = _ffn_bwd(dy, sv3, s3, sh3, g3, f2_in, f2_out, "ffn2")
    dx1, dmod2, dw_in_p, dw_uq_p, dw_ukv, dw_out_m, small = _mixer_bwd(dxm, svm, s2, sh2, g2, wts, rope)
    dx0, dmod1, dw_f1_in, dw_f1_out = _ffn_bwd(dx1, sv1, s1, sh1, g1, f1_in, f1_out, "ffn1")
    grad_x = dx0.reshape(x.shape)

    dmod = jnp.concatenate(list(dmod1) + list(dmod2) + list(dmod3), axis=1)
    small_parts = [dmod, small["conv"][:CONV_K], small["a_log"], small["dt"], small["wn"], small["wq"],
                   small["wkv"], small["wqn"], small["wqr"], small["wkn"], small["wkr"], small["won"]]
    packed, offs = _pack_rows(small_parts)
    gathered = _allgather8(packed, "gather_small")
    total = _sum8(gathered, "sum_small")
    (g_b_ada, g_conv, g_a_log, g_dt, g_wn, g_wq, g_wkv, g_wqn, g_wqr, g_wkn, g_wkr, g_won) = _unpack_rows(
        total, offs, [p.shape for p in small_parts])
    dmod_all = _unpack_rows(gathered.reshape(-1, LANES),
                            [(dd * packed.shape[0] + offs[0][0], offs[0][1]) for dd in range(8)],
                            [dmod.shape] * 8)
    dmod_all = jnp.concatenate(dmod_all, axis=0)
    dmod_mine = lax.dynamic_slice_in_dim(dmod_all, chip * shard, shard, axis=1)

    def ada_grad(r, p):
        acc = jnp.zeros((r[0].shape[0], shard), F32)
        for b in range(8):
            acc = acc + r[0][:, b:b + 1] * p[0][b:b + 1, :]
        return [acc], []

    (g_w_ada,) = _rowcall(ada_grad, [_pad_cols(sc_all.T, LANES)], [dmod_mine], [(shard, F32)], [], tile=256,
                          name="ada_grad")

    grads = dict(
        w_ada=g_w_ada[None], b_ada=g_b_ada,
        gdn_conv_w=lax.dynamic_slice_in_dim(g_conv, chip * gdn_conv_w.shape[2], gdn_conv_w.shape[2], axis=1)[None],
        gdn_a_log=g_a_log[:, :GDN_HEADS], gdn_dt_bias=g_dt[:, :GDN_HEADS], gdn_norm_w=g_wn, mla_q_norm_w=g_wq,
        mla_kv_norm_w=g_wkv, qkn_q_nope=g_wqn, qkn_q_rope=g_wqr[:, :MLA_ROPE], qkn_k_nope=g_wkn,
        qkn_k_rope=g_wkr[:, :MLA_ROPE], mla_out_norm_w=g_won)

    def rs_cols(dw, name, pad_to=None):
        g4 = _cols_to_chips(dw).astype(BF16)
        n = g4.shape[2]
        if pad_to:
            g4 = _pad_cols(g4, pad_to)
        return _reduce_scatter_chips(g4, name)[:, :n][None]

    def rs_rows(dw, name):
        r, cc = dw.shape
        return _reduce_scatter_chips(dw.astype(BF16).reshape(4, r // 4, cc), name)[None]

    grads["ffn2_w_in"] = rs_cols(dw_f2_in, "rs_f2_in")
    grads["ffn2_w_out"] = rs_rows(dw_f2_out, "rs_f2_out")
    grads["w_in"] = rs_cols(_unpack_w_in(dw_in_p), "rs_w_in", 768)
    grads["mla_w_uq"] = rs_cols(_unpack_w_uq(dw_uq_p), "rs_w_uq", 256)
    grads["mla_w_ukv"] = rs_cols(dw_ukv, "rs_w_ukv")
    grads["w_out"] = rs_rows(dw_out_m, "rs_w_out")
    grads["ffn1_w_in"] = rs_cols(dw_f1_in, "rs_f1_in")
    grads["ffn1_w_out"] = rs_rows(dw_f1_out, "rs_f1_out")

    big = ["w_ada", "ffn1_w_in", "ffn1_w_out", "w_in", "mla_w_uq", "mla_w_ukv", "w_out", "ffn2_w_in", "ffn2_w_out"]
    delta, new_m, new_v = {}, {}, {}
    for nme in big:
        shp = weights[nme].shape
        dl, nm, nv = _adamw(weights[nme][0], grads[nme][0], moms_m[nme][0], moms_v[nme][0], "adamw_" + nme)
        delta[nme], new_m[nme], new_v[nme] = dl.reshape(shp), nm.reshape(shp), nv.reshape(shp)
    tiny = [nme for nme in names if nme not in big]
    shapes = [weights[nme].shape for nme in tiny]
    pw, poffs = _pack_rows([weights[nme] for nme in tiny])
    pg, _ = _pack_rows([grads[nme] for nme in tiny])
    pm, _ = _pack_rows([moms_m[nme] for nme in tiny])
    pv, _ = _pack_rows([moms_v[nme] for nme in tiny])
    pd, pnm, pnv = _adamw(pw, pg, pm, pv, "adamw_small")
    for nme, dl, nm, nv in zip(tiny, _unpack_rows(pd, poffs, shapes), _unpack_rows(pnm, poffs, shapes),
                               _unpack_rows(pnv, poffs, shapes)):
        delta[nme], new_m[nme], new_v[nme] = dl, nm, nv

    return (loss, grad_x, *[grads[nme].reshape(weights[nme].shape) for nme in names],
            *[delta[nme] for nme in names], *[new_m[nme] for nme in names], *[new_v[nme] for nme in names])
```

```python
import functools

import jax
import jax.numpy as jnp
from jax import lax
from jax.experimental import pallas as pl
from jax.experimental.pallas import tpu as pltpu

F32 = jnp.float32
BF16 = jnp.bfloat16
HI = lax.Precision.HIGHEST
MESH = pl.DeviceIdType.MESH

EPS = 1e-6
CHUNK = 64
D_FF = 2816
GDN_HEADS = 4
HEAD = 128
MLA_HEADS = 4
MLA_ROPE = 64
MLA_Q_LORA = 384
MLA_KV_LORA = 256
QK_PAD = 256
ATT_SCALE = (HEAD + MLA_ROPE) ** -0.5
N_PROJ = 3072

ADAM_LR, ADAM_B1, ADAM_B2, ADAM_EPS, ADAM_WD, ADAM_STEP = 0.001, 0.9, 0.999, 1e-08, 0.01, 10

LANES = 128
SUBLANES = 8
VMEM_LIMIT = 56 * 2 ** 20


def _params(sem=None):
    return pltpu.CompilerParams(dimension_semantics=sem, vmem_limit_bytes=VMEM_LIMIT)


def _pick(n, cap, align):
    best = None
    d = align
    while d <= min(n, cap):
        if n % d == 0:
            best = d
        d += align
    return best if best is not None else n


def _iota(shape, dim):
    return lax.broadcasted_iota(jnp.int32, shape, dim)


def _rowcall(fn, rows, params, out_rows, out_accs, *, tile, name):
    rows = [r if isinstance(r, tuple) else (r, r.shape[1], 0) for r in rows]
    s = rows[0][0].shape[-2]
    t = min(tile, s)
    n = s // t
    n_in = len(rows) + len(params)
    n_row_out = len(out_rows)

    in_specs = []
    for r in rows:
        if len(r) == 3:
            in_specs.append(pl.BlockSpec((t, r[1]), functools.partial(lambda i, b: (i, b), b=r[2])))
        else:
            in_specs.append(pl.BlockSpec((None, t, r[1]), functools.partial(lambda i, b, h: (h, i, b), b=r[2], h=r[3])))
    in_specs += [pl.BlockSpec(p.shape, lambda i: (0, 0)) for p in params]
    out_shape, out_specs = [], []
    for o in out_rows:
        if len(o) == 2:
            out_shape.append(jax.ShapeDtypeStruct((s, o[0]), o[1]))
            out_specs.append(pl.BlockSpec((t, o[0]), lambda i: (i, 0)))
        else:
            out_shape.append(jax.ShapeDtypeStruct((o[2], s, o[0]), o[1]))
            out_specs.append(pl.BlockSpec((o[2], t, o[0]), lambda i: (0, i, 0)))
    out_shape += [jax.ShapeDtypeStruct(shape, F32) for shape in out_accs]
    out_specs += [pl.BlockSpec(shape, lambda i: (0, 0)) for shape in out_accs]

    def body(*refs):
        ins = refs[:n_in]
        outs = refs[n_in:]
        i = pl.program_id(0)
        vals = [r[...] for r in ins]
        row_outs, acc_outs = fn(vals[:len(rows)], vals[len(rows):])
        for r, v in zip(outs[:n_row_out], row_outs):
            if isinstance(v, (list, tuple)):
                for hh, piece in enumerate(v):
                    r[hh] = piece.astype(r.dtype)
            else:
                r[...] = v.astype(r.dtype)
        if out_accs:
            @pl.when(i == 0)
            def _():
                for r in outs[n_row_out:]:
                    r[...] = jnp.zeros(r.shape, F32)
            for r, v in zip(outs[n_row_out:], acc_outs):
                r[...] += v

    res = pl.pallas_call(
        body, name=name, grid=(n,), in_specs=in_specs, out_specs=out_specs, out_shape=out_shape,
        compiler_params=_params(("arbitrary",) if out_accs else ("parallel",)),
    )(*[r[0] for r in rows], *params)
    return list(res)


MM_TILE_MN = 1536


def _mm(a, b, mode, out_dtype, name, hi=False, gather=()):
    if mode == "nn":
        (m, k), (_, n) = a.shape, b.shape
        dims = (((1,), (0,)), ((), ()))
    elif mode == "nt":
        (m, k), (n, _) = a.shape, b.shape
        dims = (((1,), (1,)), ((), ()))
    else:
        (k, m), (_, n) = a.shape, b.shape
        dims = (((0,), (0,)), ((), ()))
    tm = _pick(m, MM_TILE_MN if mode == "tn" else 1024, LANES if mode == "tn" else 16)
    tn = _pick(n, MM_TILE_MN, LANES)
    tk = _pick(k, 1024 if mode == "tn" else MM_TILE_MN, LANES)
    nk = k // tk
    if mode == "nn":
        a_spec = pl.BlockSpec((tm, tk), lambda i, j, kk: (i, kk))
        b_spec = pl.BlockSpec((tk, tn), lambda i, j, kk: (kk, j))
    elif mode == "nt":
        a_spec = pl.BlockSpec((tm, tk), lambda i, j, kk: (i, kk))
        b_spec = pl.BlockSpec((tn, tk), lambda i, j, kk: (j, kk))
    else:
        a_spec = pl.BlockSpec((tk, tm), lambda i, j, kk: (kk, i))
        b_spec = pl.BlockSpec((tk, tn), lambda i, j, kk: (kk, j))

    ng = len(gather)
    grid = (m // tm, n // tn, nk)
    steps = grid[0] * grid[1] * grid[2]

    def body(*refs):
        a_ref, b_ref = refs[:2]
        x_refs = refs[2:2 + ng]
        o_ref = refs[2 + ng]
        got_refs = refs[3 + ng:3 + 2 * ng]
        acc_ref = refs[3 + 2 * ng]
        kk = pl.program_id(2)
        if ng:
            sems = refs[4 + 2 * ng:]
            step = (pl.program_id(0) * grid[1] + pl.program_id(1)) * nk + kk

            def phase(ph):
                for slot in range(ng):
                    _gather_phase(ph, x_refs[slot], got_refs[slot], *sems, slot)

            pl.when(step == 0)(lambda: phase(0))
            pl.when(step == steps // 2)(lambda: phase(1))

        @pl.when(kk == 0)
        def _():
            acc_ref[...] = jnp.zeros(acc_ref.shape, F32)

        av, bv = a_ref[...], b_ref[...]
        if hi:
            acc_ref[...] += lax.dot_general(av, bv, dims, precision=HI, preferred_element_type=F32)
        else:
            acc_ref[...] += lax.dot_general(av.astype(BF16), bv.astype(BF16), dims,
                                            preferred_element_type=F32)

        @pl.when(kk == nk - 1)
        def _():
            o_ref[...] = acc_ref[...].astype(o_ref.dtype)

        if ng:
            pl.when(step == steps - 1)(lambda: phase(2))

    hbm = pl.BlockSpec(memory_space=pl.ANY)
    res = pl.pallas_call(
        body, name=name, grid=grid,
        in_specs=[a_spec, b_spec] + [hbm] * ng,
        out_specs=[pl.BlockSpec((tm, tn), lambda i, j, kk: (i, j))] + [hbm] * ng,
        out_shape=[jax.ShapeDtypeStruct((m, n), out_dtype)]
        + [jax.ShapeDtypeStruct((4,) + x.shape, x.dtype) for x in gather],
        scratch_shapes=[pltpu.VMEM((tm, tn), F32)] + (_gather_sems(ng) if ng else []),
        compiler_params=_params(("arbitrary",) * 3 if ng else ("parallel", "parallel", "arbitrary")),
    )(a, b, *gather)
    return res if ng else res[0]


def _rms(x, w=None, n=None):
    n = x.shape[-1] if n is None else n
    y = x * lax.rsqrt(jnp.sum(x * x, axis=-1, keepdims=True) * (1.0 / n) + EPS)
    return y if w is None else y * w


def _silu(x):
    return x * jax.nn.sigmoid(x)


def _softplus(x):
    return jnp.maximum(x, 0.0) + jnp.log1p(jnp.exp(-jnp.abs(x)))


def _split(x, widths):
    out, o = [], 0
    for w in widths:
        out.append(x[:, o:o + w])
        o += w
    return out


def _modulate(x, s, sh):
    return _rms(x) * (1.0 + s) + sh


def _rope_rot(x):
    r, c = _iota((LANES, LANES), 0), _iota((LANES, LANES), 1)
    half = MLA_ROPE // 2
    perm = (((r < half) & (c == r + half)) | ((r >= half) & (r < MLA_ROPE) & (c == r - half))).astype(F32)
    return jnp.dot(x, perm, precision=HI, preferred_element_type=F32)


def _rope(x, cos2, sin2):
    return x * cos2 + _rope_rot(x) * sin2


def _gdn_prep_core(qkv_parts, gab, a_log, dt_bias):
    act = [_silu(p) for p in qkv_parts]
    qs = [p * lax.rsqrt(jnp.sum(p * p, -1, keepdims=True) + EPS) * (HEAD ** -0.5) for p in act[:4]]
    ks = [p * lax.rsqrt(jnp.sum(p * p, -1, keepdims=True) + EPS) for p in act[4:8]]
    lane = _iota(gab.shape, 1)
    g = -jnp.exp(a_log) * _softplus(gab + dt_bias)
    beta = jax.nn.sigmoid(gab)
    gb = jnp.where(lane < GDN_HEADS, g, jnp.where(lane < 2 * GDN_HEADS, beta, 0.0))
    return (jnp.concatenate(qs, 1), jnp.concatenate(ks, 1), jnp.concatenate(act[8:], 1), gb)


def _mla_prep_core(cq, ckv, kr, cos2, sin2, wq, wkv, wkr):
    cqn = _rms(cq, wq)
    ckvn = _rms(ckv, wkv)
    k_rope = _rope(_rms(kr, wkr, MLA_ROPE), cos2, sin2)
    return cqn, ckvn, k_rope


def _qk_prep_core(qn_parts, qr_parts, kn_parts, v_parts, k_rope, cos2, sin2, wqn, wqr, wkn):
    qs, ks = [], []
    for h in range(MLA_HEADS):
        qn = _rms(qn_parts[h], wqn) * ATT_SCALE
        qr = _rope(_rms(qr_parts[h], wqr, MLA_ROPE), cos2, sin2) * ATT_SCALE
        qs.append(jnp.concatenate([qn, qr], 1))
        ks.append(jnp.concatenate([_rms(kn_parts[h], wkn), k_rope], 1))
    return qs, ks, list(v_parts)


def _mix_post_core(o_parts, gz_parts, ob_parts, wn, won):
    oa = [_rms(o, wn) * _silu(z) for o, z in zip(o_parts, gz_parts)]
    ob = [_rms(o, won) for o in ob_parts]
    return jnp.concatenate(oa + ob, 1)


CONV_K = 4
HALO = SUBLANES


def _conv_fwd(proj, w8, name):
    s = proj.shape[0]
    c = w8.shape[1]
    t = min(256, s)
    n = s // t
    hb = t // HALO

    def body(x_ref, prev_ref, w_ref, o_ref, buf):
        i = pl.program_id(0)
        buf[pl.ds(0, HALO), :] = jnp.where(i > 0, prev_ref[...], 0.0)
        buf[pl.ds(HALO, t), :] = x_ref[...]
        acc = jnp.zeros((t, c), F32)
        for k in range(CONV_K):
            acc = acc + w_ref[k:k + 1, :] * buf[pl.ds(HALO - (CONV_K - 1) + k, t), :]
        o_ref[...] = acc

    return pl.pallas_call(
        body, name=name, grid=(n,),
        in_specs=[pl.BlockSpec((t, c), lambda i: (i, 0)),
                  pl.BlockSpec((HALO, c), lambda i: (jnp.maximum(i * hb - 1, 0), 0)),
                  pl.BlockSpec(w8.shape, lambda i: (0, 0))],
        out_specs=pl.BlockSpec((t, c), lambda i: (i, 0)),
        out_shape=jax.ShapeDtypeStruct((s, c), F32),
        scratch_shapes=[pltpu.VMEM((t + HALO, c), F32)],
        compiler_params=_params(("parallel",)),
    )(proj, proj, w8)


def _conv_bwd(proj, dy, w8, name):
    s = proj.shape[0]
    c = w8.shape[1]
    t = min(256, s)
    n = s // t
    hb = t // HALO

    def body(x_ref, prev_ref, dy_ref, next_ref, w_ref, dx_ref, dw_ref, bufx, bufd):
        i = pl.program_id(0)
        bufx[pl.ds(0, HALO), :] = jnp.where(i > 0, prev_ref[...], 0.0)
        bufx[pl.ds(HALO, t), :] = x_ref[...]
        bufd[pl.ds(0, t), :] = dy_ref[...]
        bufd[pl.ds(t, HALO), :] = jnp.where(i < n - 1, next_ref[...], 0.0)

        @pl.when(i == 0)
        def _():
            dw_ref[...] = jnp.zeros(dw_ref.shape, F32)

        dyv = dy_ref[...]
        acc = jnp.zeros((t, c), F32)
        for k in range(CONV_K):
            acc = acc + w_ref[k:k + 1, :] * bufd[pl.ds(CONV_K - 1 - k, t), :]
            dw_ref[k:k + 1, :] += jnp.sum(dyv * bufx[pl.ds(HALO - (CONV_K - 1) + k, t), :], axis=0, keepdims=True)
        dx_ref[...] = acc

    return pl.pallas_call(
        body, name=name, grid=(n,),
        in_specs=[pl.BlockSpec((t, c), lambda i: (i, 0)),
                  pl.BlockSpec((HALO, c), lambda i: (jnp.maximum(i * hb - 1, 0), 0)),
                  pl.BlockSpec((t, c), lambda i: (i, 0)),
                  pl.BlockSpec((HALO, c), lambda i: (jnp.minimum((i + 1) * hb, s // HALO - 1), 0)),
                  pl.BlockSpec(w8.shape, lambda i: (0, 0))],
        out_specs=[pl.BlockSpec((t, c), lambda i: (i, 0)), pl.BlockSpec(w8.shape, lambda i: (0, 0))],
        out_shape=[jax.ShapeDtypeStruct((s, c), F32), jax.ShapeDtypeStruct(w8.shape, F32)],
        scratch_shapes=[pltpu.VMEM((t + HALO, c), F32), pltpu.VMEM((t + HALO, c), F32)],
        compiler_params=_params(("arbitrary",)),
    )(proj, proj, dy, dy, w8)


_B_NN = (((2,), (1,)), ((0,), (0,)))
_B_NT = (((2,), (2,)), ((0,), (0,)))
_B_TN = (((1,), (1,)), ((0,), (0,)))


def _dot3(a, b, dims):
    return lax.dot_general(a, b, dims, precision=lax.Precision.HIGH, preferred_element_type=F32)


def _bdot_hi(a, b):
    return _dot3(a, b, _B_NN)


class _Dots:
    nn = staticmethod(lambda a, b: _dot3(a, b, _B_NN))
    nt = staticmethod(lambda a, b: _dot3(a, b, _B_NT))
    tn = staticmethod(lambda a, b: _dot3(a, b, _B_TN))


def _unit_lower_inverse(a, dots):
    c = a.shape[-1]
    ri, ci = _iota(a.shape, 1), _iota(a.shape, 2)
    inner = (ri // 2) == (ci // 2)
    t = (ri == ci).astype(F32) - jnp.where(inner, a, 0.0)
    blk = 4
    while blk <= c:
        outer = (ri // blk) == (ci // blk)
        low = jnp.where(outer & jnp.logical_not(inner), a, 0.0)
        t = t - dots.nn(dots.nn(t, low), t)
        inner = outer
        blk *= 2
    return t


def _stack(xs):
    return jnp.concatenate([x[None] for x in xs], axis=0)


def _gdn_local(dots, q, k, v, gbs):
    b, c, _ = q.shape
    gcols, bcols = [], []
    for gb in gbs:
        lane = _iota(gb.shape, 1)
        for h in range(GDN_HEADS):
            gcols.append(jnp.sum(jnp.where(lane == h, gb, 0.0), axis=1, keepdims=True))
            bcols.append(jnp.sum(jnp.where(lane == GDN_HEADS + h, gb, 0.0), axis=1, keepdims=True))
    gcol, bcol = _stack(gcols), _stack(bcols)
    ri, ci = _iota((b, c, c), 1), _iota((b, c, c), 2)
    incl = ri >= ci
    tril = incl.astype(F32)
    g_cc = _bdot_hi(tril, jnp.broadcast_to(gcol, (b, c, c)))
    g_row = _bdot_hi(jnp.ones((b, c, c), F32), jnp.where(ri == ci, g_cc, 0.0))
    g_cl = _bdot_hi(tril, jnp.broadcast_to(gcol, (b, c, HEAD)))
    g_last = jnp.sum(jnp.broadcast_to(gcol, (b, c, HEAD)), axis=1, keepdims=True)
    decay = jnp.where(incl, jnp.exp(jnp.where(incl, g_cc - g_row, 0.0)), 0.0)
    kk = dots.nt(k, k)
    minv = _unit_lower_inverse(jnp.where(ri > ci, bcol * kk * decay, 0.0), dots)
    e_g = jnp.exp(g_cl)
    u = dots.nn(minv, v * bcol)
    wk = dots.nn(minv, k * (bcol * e_g))
    qk = dots.nt(q, k) * decay
    return u, wk, q * e_g, k * jnp.exp(g_last - g_cl), qk, jnp.exp(g_last)


def _gdn_scan(dots, states, u, wk, qd, kd, qk, gl_tile):
    lane, row = _iota(gl_tile.shape, 1), _iota(gl_tile.shape, 0)
    gl = _stack([
        jnp.sum(jnp.sum(jnp.where((lane == h) & (row == 0), gl_tile, 0.0), axis=1, keepdims=True),
                axis=0, keepdims=True) for h in range(GDN_HEADS)])
    v_new = u - dots.nn(wk, states)
    o = dots.nn(qd, states) + dots.nn(qk, v_new)
    return states * gl + dots.tn(kd, v_new), o


def _heads(x):
    return jnp.stack(_split(x, HW4))


GDN_W = GDN_HEADS * HEAD
HW4 = [HEAD] * GDN_HEADS
LOCAL_CHUNKS = 4
_CHUNK_ROWS = [pl.ds(cc * CHUNK, CHUNK) for cc in range(LOCAL_CHUNKS)]


def _chunk_heads(ref):
    return jnp.concatenate([_heads(ref[rows, :]) for rows in _CHUNK_ROWS], 0)


def _gdn_local_fwd(q, k, v, gb, name):
    s = q.shape[0]
    t = LOCAL_CHUNKS * CHUNK

    def body(q_ref, k_ref, v_ref, gb_ref, u_ref, wk_ref, qd_ref, kd_ref, qk_ref, gl_ref):
        u, wk, qd, kd, qk, gl = _gdn_local(_Dots, _chunk_heads(q_ref), _chunk_heads(k_ref),
                                           _chunk_heads(v_ref), [gb_ref[rows, :] for rows in _CHUNK_ROWS])
        lane = _iota((CHUNK, LANES), 1)
        for cc, rows in enumerate(_CHUNK_ROWS):
            gl_tile = jnp.zeros((CHUNK, LANES), F32)
            for h in range(GDN_HEADS):
                b, cols = cc * GDN_HEADS + h, pl.ds(h * HEAD, HEAD)
                u_ref[rows, cols] = u[b]
                wk_ref[rows, cols] = wk[b]
                qd_ref[rows, cols] = qd[b]
                kd_ref[rows, cols] = kd[b]
                qk_ref[h, rows, :] = qk[b]
                gl_tile = gl_tile + jnp.where(lane == h, gl[b], 0.0)
            gl_ref[rows, :] = gl_tile

    row = pl.BlockSpec((t, GDN_W), lambda i: (i, 0))
    lane = pl.BlockSpec((t, LANES), lambda i: (i, 0))
    qks = pl.BlockSpec((GDN_HEADS, t, CHUNK), lambda i: (0, i, 0))
    return pl.pallas_call(
        body, name=name, grid=(s // t,),
        in_specs=[row, row, row, lane],
        out_specs=[row, row, row, row, qks, lane],
        out_shape=[jax.ShapeDtypeStruct((s, GDN_W), F32)] * 4
        + [jax.ShapeDtypeStruct((GDN_HEADS, s, CHUNK), F32), jax.ShapeDtypeStruct((s, LANES), F32)],
        compiler_params=_params(("parallel",)),
    )(q, k, v, gb)


def _gdn_local_bwd(q, k, v, gb, du, dwk, dqd, dkd, dqk, dgl, name):
    s = q.shape[0]
    t = LOCAL_CHUNKS * CHUNK

    def body(q_ref, k_ref, v_ref, gb_ref, du_ref, dwk_ref, dqd_ref, dkd_ref, dqk_ref, dgl_ref,
             dq_ref, dk_ref, dv_ref, dgb_ref):
        _, vjp = jax.vjp(functools.partial(_gdn_local, _Dots), _chunk_heads(q_ref), _chunk_heads(k_ref),
                         _chunk_heads(v_ref), [gb_ref[rows, :] for rows in _CHUNK_ROWS])
        lane = _iota((CHUNK, LANES), 1)
        dqk = jnp.stack([dqk_ref[h, rows, :] for rows in _CHUNK_ROWS for h in range(GDN_HEADS)])
        dgl = jnp.stack([jnp.sum(jnp.where(lane == h, dgl_ref[rows, :], 0.0), axis=0, keepdims=True)
                         for rows in _CHUNK_ROWS for h in range(GDN_HEADS)])
        d_q, d_k, d_v, d_gbs = vjp((_chunk_heads(du_ref), _chunk_heads(dwk_ref), _chunk_heads(dqd_ref),
                                    _chunk_heads(dkd_ref), dqk, dgl))
        for cc, rows in enumerate(_CHUNK_ROWS):
            for h in range(GDN_HEADS):
                b, cols = cc * GDN_HEADS + h, pl.ds(h * HEAD, HEAD)
                dq_ref[rows, cols] = d_q[b]
                dk_ref[rows, cols] = d_k[b]
                dv_ref[rows, cols] = d_v[b]
            dgb_ref[rows, :] = d_gbs[cc]

    row = pl.BlockSpec((t, GDN_W), lambda i: (i, 0))
    lane = pl.BlockSpec((t, LANES), lambda i: (i, 0))
    qks = pl.BlockSpec((GDN_HEADS, t, CHUNK), lambda i: (0, i, 0))
    return pl.pallas_call(
        body, name=name, grid=(s // t,),
        in_specs=[row, row, row, lane, row, row, row, row, qks, lane],
        out_specs=[row, row, row, lane],
        out_shape=[jax.ShapeDtypeStruct((s, GDN_W), F32)] * 3 + [jax.ShapeDtypeStruct((s, LANES), F32)],
        compiler_params=_params(("parallel",)),
    )(q, k, v, gb, du, dwk, dqd, dkd, dqk, dgl)


def _gdn_scan_fwd(u, wk, qd, kd, qk, gl, name):
    s = u.shape[0]
    nc = s // CHUNK

    def body(u_ref, wk_ref, qd_ref, kd_ref, qk_ref, gl_ref, o_ref, st_ref, state):
        i = pl.program_id(0)

        @pl.when(i == 0)
        def _():
            state[...] = jnp.zeros(state.shape, F32)

        st_ref[...] = state[...]
        new_states, o = _gdn_scan(_Dots, state[...], _heads(u_ref[...]), _heads(wk_ref[...]),
                                  _heads(qd_ref[...]), _heads(kd_ref[...]), qk_ref[...], gl_ref[...])
        state[...] = new_states
        o_ref[...] = jnp.concatenate([o[h] for h in range(GDN_HEADS)], 1)

    row = pl.BlockSpec((CHUNK, GDN_W), lambda i: (i, 0))
    return pl.pallas_call(
        body, name=name, grid=(nc,),
        in_specs=[row, row, row, row, pl.BlockSpec((GDN_HEADS, CHUNK, CHUNK), lambda i: (0, i, 0)),
                  pl.BlockSpec((CHUNK, LANES), lambda i: (i, 0))],
        out_specs=[row, pl.BlockSpec((None, GDN_HEADS, HEAD, HEAD), lambda i: (i, 0, 0, 0))],
        out_shape=[jax.ShapeDtypeStruct((s, GDN_W), F32),
                   jax.ShapeDtypeStruct((nc, GDN_HEADS, HEAD, HEAD), F32)],
        scratch_shapes=[pltpu.VMEM((GDN_HEADS, HEAD, HEAD), F32)],
        compiler_params=_params(("arbitrary",)),
    )(u, wk, qd, kd, qk, gl)


def _gdn_scan_bwd(u, wk, qd, kd, qk, gl, st, do, name):
    s = u.shape[0]
    nc = s // CHUNK

    def body(u_ref, wk_ref, qd_ref, kd_ref, qk_ref, gl_ref, st_ref, do_ref,
             du_ref, dwk_ref, dqd_ref, dkd_ref, dqk_ref, dgl_ref, dstate):
        i = pl.program_id(0)

        @pl.when(i == 0)
        def _():
            dstate[...] = jnp.zeros(dstate.shape, F32)

        _, vjp = jax.vjp(functools.partial(_gdn_scan, _Dots), st_ref[...], _heads(u_ref[...]),
                         _heads(wk_ref[...]), _heads(qd_ref[...]), _heads(kd_ref[...]), qk_ref[...], gl_ref[...])
        d_states, d_u, d_wk, d_qd, d_kd, d_qk, d_gl = vjp((dstate[...], _heads(do_ref[...])))
        dstate[...] = d_states
        dqk_ref[...] = d_qk
        unheads = lambda x: jnp.concatenate([x[h] for h in range(GDN_HEADS)], 1)
        du_ref[...] = unheads(d_u)
        dwk_ref[...] = unheads(d_wk)
        dqd_ref[...] = unheads(d_qd)
        dkd_ref[...] = unheads(d_kd)
        dgl_ref[...] = d_gl

    rev = lambda i: (nc - 1 - i, 0)
    row = pl.BlockSpec((CHUNK, GDN_W), rev)
    lane = pl.BlockSpec((CHUNK, LANES), rev)
    qks = pl.BlockSpec((GDN_HEADS, CHUNK, CHUNK), lambda i: (0, nc - 1 - i, 0))
    return pl.pallas_call(
        body, name=name, grid=(nc,),
        in_specs=[row, row, row, row, qks, lane,
                  pl.BlockSpec((None, GDN_HEADS, HEAD, HEAD), lambda i: (nc - 1 - i, 0, 0, 0)), row],
        out_specs=[row, row, row, row, qks, lane],
        out_shape=[jax.ShapeDtypeStruct((s, GDN_W), F32)] * 4
        + [jax.ShapeDtypeStruct((GDN_HEADS, s, CHUNK), F32), jax.ShapeDtypeStruct((s, LANES), F32)],
        scratch_shapes=[pltpu.VMEM((GDN_HEADS, HEAD, HEAD), F32)],
        compiler_params=_params(("arbitrary",)),
    )(u, wk, qd, kd, qk, gl, st, do)


def _chunk_mask(i, j, t):
    r = i * t + _iota((t, t), 0)
    c = j * t + _iota((t, t), 1)
    return (r // CHUNK) >= (c // CHUNK)


ATT_TILE = 1024
ATT_Q_TILES = 1
ATT_BWD_TILE = 1024


def _attn_fwd(q, k, v, name):
    nh, s = MLA_HEADS, q.shape[0]
    tk = min(ATT_TILE, s)
    tq = min(ATT_Q_TILES * tk, s)
    qk = tq // tk
    nq, n = s // tq, s // tk
    nt = (((1,), (1,)), ((), ()))

    def body(q_ref, k_ref, v_ref, o_ref, lse_ref, m_sc, l_sc, acc_sc):
        i, j = pl.program_id(1), pl.program_id(2)

        @pl.when(j == 0)
        def _():
            m_sc[...] = jnp.full(m_sc.shape, -jnp.inf, F32)
            l_sc[...] = jnp.zeros(l_sc.shape, F32)
            acc_sc[...] = jnp.zeros(acc_sc.shape, F32)

        def step(masked):
            sc = lax.dot_general(q_ref[...], k_ref[...], nt, preferred_element_type=F32)
            if masked:
                r = i * tq + _iota((tq, tk), 0)
                c = j * tk + _iota((tq, tk), 1)
                sc = jnp.where((r // CHUNK) >= (c // CHUNK), sc, -jnp.inf)
            m_prev = m_sc[:, :1]
            m_new = jnp.maximum(m_prev, jnp.max(sc, axis=1, keepdims=True))
            alpha = jnp.exp(m_prev - m_new)
            p = jnp.exp(sc - m_new)
            l_sc[...] = jnp.broadcast_to(alpha * l_sc[:, :1] + jnp.sum(p, axis=1, keepdims=True), l_sc.shape)
            acc_sc[...] = alpha * acc_sc[...] + jnp.dot(p.astype(BF16), v_ref[...], preferred_element_type=F32)
            m_sc[...] = jnp.broadcast_to(m_new, m_sc.shape)

        pl.when(j < i * qk)(lambda: step(False))
        pl.when(j // qk == i)(lambda: step(True))

        @pl.when(j == n - 1)
        def _():
            o_ref[...] = acc_sc[...] / l_sc[:, :1]
            lse_ref[...] = m_sc[...] + jnp.log(l_sc[...])

    qrow = lambda h, i, j: (i, h)
    krow = lambda h, i, j: (jnp.minimum(j, (i + 1) * qk - 1), h)
    return pl.pallas_call(
        body, name=name, grid=(nh, nq, n),
        in_specs=[pl.BlockSpec((tq, QK_PAD), qrow), pl.BlockSpec((tk, QK_PAD), krow),
                  pl.BlockSpec((tk, HEAD), krow)],
        out_specs=[pl.BlockSpec((tq, HEAD), qrow), pl.BlockSpec((None, tq, LANES), lambda h, i, j: (h, i, 0))],
        out_shape=[jax.ShapeDtypeStruct((s, nh * HEAD), F32), jax.ShapeDtypeStruct((nh, s, LANES), F32)],
        scratch_shapes=[pltpu.VMEM((tq, LANES), F32), pltpu.VMEM((tq, LANES), F32), pltpu.VMEM((tq, HEAD), F32)],
        compiler_params=_params(("parallel", "parallel", "arbitrary")),
    )(q, k, v)


def _attn_bwd(q, k, v, o, do, lse, name):
    nh, s = MLA_HEADS, q.shape[0]
    t = min(ATT_BWD_TILE, s)
    n = s // t
    tn = (((0,), (0,)), ((), ()))
    nt = (((1,), (1,)), ((), ()))

    def body(q_ref, k_ref, v_ref, o_ref, do_ref, lse_ref, dq_ref, dk_ref, dv_ref, dk_acc, dv_acc, dq_acc):
        j, i = pl.program_id(1), pl.program_id(2)

        @pl.when(i + j == 0)
        def _():
            dq_acc[...] = jnp.zeros(dq_acc.shape, F32)

        @pl.when(i == 0)
        def _():
            dk_acc[...] = jnp.zeros(dk_acc.shape, F32)
            dv_acc[...] = jnp.zeros(dv_acc.shape, F32)

        def step(masked):
            qv, kv, do = q_ref[...], k_ref[...], do_ref[...]
            sc = lax.dot_general(qv, kv, nt, preferred_element_type=F32)
            p = jnp.exp(sc - lse_ref[:, :1])
            if masked:
                p = jnp.where(_chunk_mask(i, j, t), p, 0.0)
            dob = do.astype(BF16)
            dp = lax.dot_general(dob, v_ref[...], nt, preferred_element_type=F32)
            ds = (p * (dp - jnp.sum(do * o_ref[...], axis=1, keepdims=True))).astype(BF16)
            dv_acc[...] += lax.dot_general(p.astype(BF16), dob, tn, preferred_element_type=F32)
            dk_acc[...] += lax.dot_general(ds, qv, tn, preferred_element_type=F32)
            rows = pl.ds(pl.multiple_of(i * t, t), t)
            dq_acc[rows, :] += jnp.dot(ds, kv, preferred_element_type=F32)

        pl.when(i > j)(lambda: step(False))
        pl.when(i == j)(lambda: step(True))

        @pl.when(i == n - 1)
        def _():
            dk_ref[...] = dk_acc[...]
            dv_ref[...] = dv_acc[...]

        @pl.when(i + j == 2 * (n - 1))
        def _():
            dq_ref[...] = dq_acc[...]

    qrow = lambda h, j, i: (jnp.maximum(i, j), h)
    krow = lambda h, j, i: (j, h)
    return pl.pallas_call(
        body, name=name, grid=(nh, n, n),
        in_specs=[pl.BlockSpec((t, QK_PAD), qrow), pl.BlockSpec((t, QK_PAD), krow), pl.BlockSpec((t, HEAD), krow),
                  pl.BlockSpec((t, HEAD), qrow), pl.BlockSpec((t, HEAD), qrow),
                  pl.BlockSpec((None, t, LANES), lambda h, j, i: (h, jnp.maximum(i, j), 0))],
        out_specs=[pl.BlockSpec((s, QK_PAD), lambda h, j, i: (0, h)),
                   pl.BlockSpec((t, QK_PAD), krow), pl.BlockSpec((t, HEAD), krow)],
        out_shape=[jax.ShapeDtypeStruct((s, nh * QK_PAD), F32), jax.ShapeDtypeStruct((s, nh * QK_PAD), F32),
                   jax.ShapeDtypeStruct((s, nh * HEAD), F32)],
        scratch_shapes=[pltpu.VMEM((t, QK_PAD), F32), pltpu.VMEM((t, HEAD), F32), pltpu.VMEM((s, QK_PAD), F32)],
        compiler_params=_params(("arbitrary", "arbitrary", "arbitrary")),
    )(q, k, v, o, do, lse)


def _place():
    return lax.axis_index("x"), lax.axis_index("y"), lax.axis_index("c")


def _allgather8(x, name):
    r, c = x.shape

    def body(x_ref, out_ref, send_sems, recv_sems, local_sem):
        mx, my, mc = _place()
        me = 4 * mx + 2 * my + mc
        mine = pltpu.make_async_copy(x_ref, out_ref.at[me], local_sem)
        mine.start()
        copies = []
        for d in range(1, 8):
            px = 1 - mx if d & 4 else mx
            py = 1 - my if d & 2 else my
            pc = 1 - mc if d & 1 else mc
            cp = pltpu.make_async_remote_copy(
                src_ref=x_ref, dst_ref=out_ref.at[me], send_sem=send_sems.at[d - 1], recv_sem=recv_sems.at[d - 1],
                device_id=(px, py, pc), device_id_type=MESH)
            cp.start()
            copies.append(cp)
        for cp in copies:
            cp.wait()
        mine.wait()

    return pl.pallas_call(
        body, name=name,
        out_shape=jax.ShapeDtypeStruct((8, r, c), x.dtype),
        in_specs=[pl.BlockSpec(memory_space=pltpu.VMEM)],
        out_specs=pl.BlockSpec(memory_space=pltpu.VMEM),
        scratch_shapes=[pltpu.SemaphoreType.DMA((7,)), pltpu.SemaphoreType.DMA((7,)), pltpu.SemaphoreType.DMA],
        compiler_params=pltpu.CompilerParams(vmem_limit_bytes=VMEM_LIMIT),
    )(x)


def _allgather_chips(x, name):
    r, c = x.shape

    def body(x_ref, out_ref, send_sems, recv_sems, local_sems):
        for phase in range(3):
            _gather_phase(phase, x_ref, out_ref, send_sems, recv_sems, local_sems, 0)

    return pl.pallas_call(
        body, name=name,
        out_shape=jax.ShapeDtypeStruct((4, r, c), x.dtype),
        in_specs=[pl.BlockSpec(memory_space=pltpu.VMEM)],
        out_specs=pl.BlockSpec(memory_space=pltpu.VMEM),
        scratch_shapes=_gather_sems(1),
        compiler_params=pltpu.CompilerParams(vmem_limit_bytes=VMEM_LIMIT),
    )(x)


GATHER_COPIES = 6


def _gather_sems(n):
    return [pltpu.SemaphoreType.DMA((GATHER_COPIES * n,)), pltpu.SemaphoreType.DMA((GATHER_COPIES * n,)),
            pltpu.SemaphoreType.DMA((n,))]


def _gather_phase(phase, x_ref, out_ref, send_sems, recv_sems, local_sems, slot):
    mx, my, mc = _place()
    j = 2 * mx + my
    rh = x_ref.shape[0] // 2
    base = GATHER_COPIES * slot
    chips = [(1 - mx, my), (mx, 1 - my), (1 - mx, 1 - my)]
    sibling = (mx, my, 1 - mc)

    def half(jj, hc):
        return out_ref.at[jj, pl.ds(hc * rh, rh), :]

    def over_ici(kk, block):
        px, py = chips[kk]
        return pltpu.make_async_remote_copy(
            src_ref=x_ref.at[pl.ds(mc * rh, rh), :], dst_ref=half(block, mc), send_sem=send_sems.at[base + kk],
            recv_sem=recv_sems.at[base + kk], device_id=(px, py, mc), device_id_type=MESH)

    def to_sibling(kk, hc):
        px, py = chips[kk]
        blk = half(2 * px + py, hc)
        return pltpu.make_async_remote_copy(
            src_ref=blk, dst_ref=blk, send_sem=send_sems.at[base + 3 + kk], recv_sem=recv_sems.at[base + 3 + kk],
            device_id=sibling, device_id_type=MESH)

    mine = pltpu.make_async_copy(x_ref, out_ref.at[j], local_sems.at[slot])
    if phase == 0:
        mine.start()
        for kk in range(3):
            over_ici(kk, j).start()
    elif phase == 1:
        for kk, (px, py) in enumerate(chips):
            over_ici(kk, 2 * px + py).wait_recv()
            to_sibling(kk, mc).start()
    else:
        for kk in range(3):
            to_sibling(kk, 1 - mc).wait_recv()
        for kk in range(3):
            over_ici(kk, j).wait_send()
            to_sibling(kk, mc).wait_send()
        mine.wait()


RS_ROWS = 32


def _reduce_scatter_chips(g, name):
    _, r, c = g.shape
    rh = r // 2
    steps = rh // RS_ROWS

    def body(g_ref, out_ref, sib_ref, part_ref, got_ref, send_sems, recv_sems):
        mx, my, mc = _place()
        j = 2 * mx + my
        sibling = (mx, my, 1 - mc)
        chips = [(1 - mx, my), (mx, 1 - my), (1 - mx, 1 - my)]

        to_sib = pltpu.make_async_remote_copy(
            src_ref=g_ref.at[:, pl.ds((1 - mc) * rh, rh), :], dst_ref=sib_ref,
            send_sem=send_sems.at[0], recv_sem=recv_sems.at[0], device_id=sibling, device_id_type=MESH)
        to_sib.start()
        to_sib.wait()

        def add_sibling(step, carry):
            rows = pl.ds(pl.multiple_of(step * RS_ROWS, RS_ROWS), RS_ROWS)
            mine = g_ref[:, pl.ds(pl.multiple_of(mc * rh + step * RS_ROWS, RS_ROWS), RS_ROWS), :]
            part_ref[:, rows, :] = mine.astype(F32) + sib_ref[:, rows, :].astype(F32)
            return carry

        lax.fori_loop(0, steps, add_sibling, 0)

        def to_bf16(step, carry):
            rows = pl.ds(pl.multiple_of(step * RS_ROWS, RS_ROWS), RS_ROWS)
            sib_ref[:, rows, :] = part_ref[:, rows, :].astype(BF16)
            return carry

        lax.fori_loop(0, steps, to_bf16, 0)

        sends = []
        for kk, (px, py) in enumerate(chips):
            cp = pltpu.make_async_remote_copy(
                src_ref=sib_ref.at[2 * px + py], dst_ref=got_ref.at[kk],
                send_sem=send_sems.at[1 + kk], recv_sem=recv_sems.at[1 + kk],
                device_id=(px, py, mc), device_id_type=MESH)
            cp.start()
            sends.append(cp)
        for cp in sends:
            cp.wait()

        def total(step, carry):
            rows = pl.ds(pl.multiple_of(step * RS_ROWS, RS_ROWS), RS_ROWS)
            acc = part_ref[j, rows, :]
            for kk in range(3):
                acc = acc + got_ref[kk, rows, :].astype(F32)
            out_ref[pl.ds(pl.multiple_of(mc * rh + step * RS_ROWS, RS_ROWS), RS_ROWS), :] = acc
            return carry

        lax.fori_loop(0, steps, total, 0)

        done = pltpu.make_async_remote_copy(
            src_ref=out_ref.at[pl.ds(mc * rh, rh), :], dst_ref=out_ref.at[pl.ds(mc * rh, rh), :],
            send_sem=send_sems.at[4], recv_sem=recv_sems.at[4], device_id=sibling, device_id_type=MESH)
        done.start()
        done.wait_send()
        pltpu.make_async_remote_copy(
            src_ref=out_ref.at[pl.ds((1 - mc) * rh, rh), :], dst_ref=out_ref.at[pl.ds((1 - mc) * rh, rh), :],
            send_sem=send_sems.at[4], recv_sem=recv_sems.at[4], device_id=sibling, device_id_type=MESH).wait_recv()

    return pl.pallas_call(
        body, name=name,
        out_shape=jax.ShapeDtypeStruct((r, c), F32),
        in_specs=[pl.BlockSpec(memory_space=pltpu.VMEM)],
        out_specs=pl.BlockSpec(memory_space=pltpu.VMEM),
        scratch_shapes=[pltpu.VMEM((4, rh, c), BF16), pltpu.VMEM((4, rh, c), F32), pltpu.VMEM((3, rh, c), BF16),
                        pltpu.SemaphoreType.DMA((5,)), pltpu.SemaphoreType.DMA((5,))],
        compiler_params=pltpu.CompilerParams(vmem_limit_bytes=VMEM_LIMIT),
    )(g)


def _sum8(x, name):
    _, r, c = x.shape

    def body(x_ref, o_ref):
        acc = x_ref[0]
        for d in range(1, 8):
            acc = acc + x_ref[d]
        o_ref[...] = acc

    return pl.pallas_call(
        body, name=name, out_shape=jax.ShapeDtypeStruct((r, c), F32),
        in_specs=[pl.BlockSpec(memory_space=pltpu.VMEM)], out_specs=pl.BlockSpec(memory_space=pltpu.VMEM),
    )(x)


def _adamw(w, g, m, v, name):
    r, c = w.shape
    t = _pick(r, 256, SUBLANES)
    spec = pl.BlockSpec((t, c), lambda i: (i, 0))

    def body(w_ref, g_ref, m_ref, v_ref, d_ref, nm_ref, nv_ref):
        gv = g_ref[...]
        m_new = ADAM_B1 * m_ref[...] + (1.0 - ADAM_B1) * gv
        v_new = ADAM_B2 * v_ref[...] + (1.0 - ADAM_B2) * (gv * gv)
        m_hat = m_new / (1.0 - ADAM_B1 ** ADAM_STEP)
        v_hat = v_new / (1.0 - ADAM_B2 ** ADAM_STEP)
        d_ref[...] = -ADAM_LR * (m_hat / (jnp.sqrt(v_hat) + ADAM_EPS) + ADAM_WD * w_ref[...])
        nm_ref[...] = m_new
        nv_ref[...] = v_new

    return pl.pallas_call(
        body, name=name, grid=(r // t,), in_specs=[spec] * 4, out_specs=[spec] * 3,
        out_shape=[jax.ShapeDtypeStruct((r, c), F32)] * 3, compiler_params=_params(("parallel",)),
    )(w, g, m, v)


def _pack_rows(parts):
    rows, offs, o = [], [], 0
    for p in parts:
        f = p.reshape(-1)
        n = -(-f.shape[0] // (LANES * SUBLANES)) * SUBLANES
        rows.append(jnp.pad(f, (0, n * LANES - f.shape[0])).reshape(n, LANES))
        offs.append((o, n))
        o += n
    return jnp.concatenate(rows, 0), offs


def _unpack_rows(packed, offs, shapes):
    out = []
    for (o, n), shp in zip(offs, shapes):
        size = 1
        for d in shp:
            size *= d
        out.append(packed[o:o + n].reshape(-1)[:size].reshape(shp))
    return out


def _mm_hosting(a, b, mode, out_dtype, name, gather):
    res = _mm(a, b, mode, out_dtype, name, gather=gather)
    return (res[0], list(res[1:])) if gather else (res, [])


def _ffn_fwd(x, s, sh, g, w_in, w_out, tag, gather_in=(), gather_out=()):
    (h,) = _rowcall(lambda r, p: ([_modulate(r[0], p[0], p[1])], []), [x], [s, sh], [(x.shape[1], BF16)], [],
                    tile=512, name=tag + "_mod")
    gu, got = _mm_hosting(h, w_in, "nn", BF16, tag + "_in", gather_in)
    if w_out is None:
        first = got.pop(0)
        w_out = first.reshape(4 * first.shape[1], first.shape[2])
    (act,) = _rowcall(lambda r, p: ([_silu(r[0].astype(F32)) * r[1].astype(F32)], []),
                      [(gu, D_FF, 0), (gu, D_FF, 1)], [], [(D_FF, BF16)], [], tile=256, name=tag + "_act")
    f, got_out = _mm_hosting(act, w_out, "nn", F32, tag + "_out", gather_out)
    got = got + got_out
    (y,) = _rowcall(lambda r, p: ([r[0] + 0.5 * p[0] * r[1]], []), [x, f], [g], [(x.shape[1], F32)], [],
                    tile=512, name=tag + "_res")
    return y, (x, h, gu, act, f), got, w_out


def _ffn_bwd(dy, saved, s, sh, g, w_in, w_out, tag):
    x, h, gu, act, f = saved
    d = x.shape[1]
    df, dg = _rowcall(lambda r, p: ([0.5 * p[0] * r[0]], [0.5 * jnp.sum(r[0] * r[1], 0, keepdims=True)]),
                      [dy, f], [g], [(d, BF16)], [(1, d)], tile=512, name=tag + "_bres")
    da = _mm(df, w_out, "nt", BF16, tag + "_bout")
    dw_out = _mm(act, df, "tn", BF16, tag + "_bwout")

    def act_bwd(r, p):
        gate, up, dav = r[0].astype(F32), r[1].astype(F32), r[2].astype(F32)
        _, vjp = jax.vjp(lambda a, b: _silu(a) * b, gate, up)
        dgate, dup = vjp(dav)
        return [jnp.concatenate([dgate, dup], 1)], []

    (dgu,) = _rowcall(act_bwd, [(gu, D_FF, 0), (gu, D_FF, 1), da], [], [(2 * D_FF, BF16)], [], tile=256,
                      name=tag + "_bact")
    dh = _mm(dgu, w_in, "nt", F32, tag + "_bin")
    dw_in = _mm(h, dgu, "tn", BF16, tag + "_bwin")

    def mod_bwd(r, p):
        _, vjp = jax.vjp(_modulate, r[0], p[0], p[1])
        dx, ds, dsh = vjp(r[1])
        return [r[2] + dx], [ds, dsh]

    dx, ds, dsh = _rowcall(mod_bwd, [x, dh, dy], [s, sh], [(d, F32)], [(1, d), (1, d)], tile=512, name=tag + "_bmod")
    return dx, (dsh, ds, dg), dw_in, dw_out


def _mixer_fwd(x, s, sh, g, wts, rope, gather=()):
    w_in_p, conv8, a_log, dt_bias, wn, wq, w_uq_p, wkv, w_ukv, wqn, wqr, wkn, wkr, won, w_out = wts
    cos2, sin2 = rope
    d = x.shape[1]
    (h,) = _rowcall(lambda r, p: ([_modulate(r[0], p[0], p[1])], []), [x], [s, sh], [(d, BF16)], [],
                    tile=512, name="mix_mod")
    proj, got = _mm_hosting(h, w_in_p, "nn", F32, "mix_in", gather)
    qkv_c = _conv_fwd(proj, conv8, "mix_conv")
    gab = (proj, LANES, 23)

    q, k, v, gb = _rowcall(
        lambda r, p: (list(_gdn_prep_core(_split(r[0], [HEAD] * 12), r[1], p[0], p[1])), []),
        [qkv_c, gab], [a_log, dt_bias], [(512, F32)] * 3 + [(LANES, F32)], [], tile=256, name="mix_gdn_prep")
    gdn_local = _gdn_local_fwd(q, k, v, gb, "mix_gdn_local")
    o_gdn, gdn_states = _gdn_scan_fwd(*gdn_local, "mix_gdn_scan")
    states = (gdn_local, gdn_states)

    cq, ckv, kr = (proj, 512, 4), (proj, 256, 10), (proj, LANES, 22)
    cqn, ckvn, k_rope = _rowcall(
        lambda r, p: (list(_mla_prep_core(r[0][:, :MLA_Q_LORA], r[1], r[2], r[3], r[4], p[0], p[1], p[2])), []),
        [cq, ckv, kr, cos2, sin2], [wq, wkv, wkr], [(MLA_Q_LORA, BF16), (MLA_KV_LORA, BF16), (LANES, F32)], [],
        tile=512, name="mix_mla_prep")
    qf = _mm(cqn, w_uq_p, "nn", F32, "mix_uq")
    kvf = _mm(ckvn, w_ukv, "nn", F32, "mix_ukv")

    def qk_prep(r, p):
        qparts = _split(r[0], [HEAD] * 8)
        kvparts = _split(r[1], [HEAD] * 8)
        qs, ks, vs = _qk_prep_core(qparts[:4], qparts[4:], kvparts[0::2], kvparts[1::2], r[2], r[3], r[4],
                                   p[0], p[1], p[2])
        return [jnp.concatenate(qs, 1), jnp.concatenate(ks, 1), jnp.concatenate(vs, 1)], []

    qa, ka, va = _rowcall(qk_prep, [qf, kvf, k_rope, cos2, sin2], [wqn, wqr, wkn],
                          [(4 * QK_PAD, BF16), (4 * QK_PAD, BF16), (4 * HEAD, BF16)], [], tile=256,
                          name="mix_qk_prep")
    o_b, lse = _attn_fwd(qa, ka, va, "mix_attn")

    gz = (proj, 512, 3)
    (mixed,) = _rowcall(
        lambda r, p: ([_mix_post_core(_split(r[0], HW4), _split(r[1], HW4), _split(r[2], HW4), p[0], p[1])], []),
        [o_gdn, gz, o_b], [wn, won], [(2 * 512, BF16)], [], tile=512, name="mix_post")
    y = _mm(mixed, w_out, "nn", F32, "mix_out")
    (x_out,) = _rowcall(lambda r, p: ([r[0] + p[0] * r[1]], []), [x, y], [g], [(d, F32)], [], tile=512,
                        name="mix_res")
    saved = (x, h, proj, qkv_c, q, k, v, gb, states, o_gdn, cqn, ckvn, k_rope, qf, kvf, qa, ka, va, o_b, lse,
             mixed, y)
    return x_out, saved, got


def _mixer_bwd(dy, saved, s, sh, g, wts, rope):
    w_in_p, conv8, a_log, dt_bias, wn, wq, w_uq_p, wkv, w_ukv, wqn, wqr, wkn, wkr, won, w_out = wts
    cos2, sin2 = rope
    (x, h, proj, qkv_c, q, k, v, gb, states, o_gdn, cqn, ckvn, k_rope, qf, kvf, qa, ka, va, o_b, lse,
     mixed, y) = saved
    d = x.shape[1]
    dyb, dg = _rowcall(lambda r, p: ([p[0] * r[0]], [jnp.sum(r[0] * r[1], 0, keepdims=True)]),
                       [dy, y], [g], [(d, BF16)], [(1, d)], tile=512, name="mix_bres")
    dmixed = _mm(dyb, w_out, "nt", F32, "mix_bout")
    dw_out = _mm(mixed, dyb, "tn", BF16, "mix_bwout")

    gz = (proj, 512, 3)

    def post_bwd(r, p):
        _, vjp = jax.vjp(_mix_post_core, _split(r[0], HW4), _split(r[1], HW4), _split(r[2], HW4), p[0], p[1])
        do, dz, dob, dwn, dwon = vjp(r[3])
        return [jnp.concatenate(do, 1), jnp.concatenate(dz, 1), jnp.concatenate(dob, 1)], [dwn, dwon]

    do_gdn, dgz, do_b, dwn, dwon = _rowcall(post_bwd, [o_gdn, gz, o_b, dmixed], [wn, won], [(512, F32)] * 3,
                                            [(1, HEAD), (1, HEAD)], tile=256, name="mix_bpost")

    dqa, dka, dva = _attn_bwd(qa, ka, va, o_b, do_b, lse, "mix_battn")

    def qk_bwd(r, p):
        qparts = _split(r[0], [HEAD] * 8)
        kvparts = _split(r[1], [HEAD] * 8)
        _, vjp = jax.vjp(_qk_prep_core, qparts[:4], qparts[4:], kvparts[0::2], kvparts[1::2], r[2], r[3], r[4],
                         p[0], p[1], p[2])
        cot = (_split(r[5], [QK_PAD] * 4), _split(r[6], [QK_PAD] * 4), _split(r[7], HW4))
        dqn, dqr, dkn, dvp, dkrope, _, _, dwqn, dwqr, dwkn = vjp(cot)
        dkv = []
        for a, b in zip(dkn, dvp):
            dkv += [a, b]
        return [jnp.concatenate(list(dqn) + list(dqr), 1), jnp.concatenate(dkv, 1), dkrope], [dwqn, dwqr, dwkn]

    dqf, dkvf, dk_rope, dwqn, dwqr, dwkn = _rowcall(
        qk_bwd, [qf, kvf, k_rope, cos2, sin2, dqa, dka, dva], [wqn, wqr, wkn],
        [(8 * HEAD, BF16), (8 * HEAD, BF16), (LANES, F32)], [(1, HEAD)] * 3, tile=256, name="mix_bqk_prep")
    dcqn = _mm(dqf, w_uq_p, "nt", F32, "mix_buq")
    dw_uq_p = _mm(cqn, dqf, "tn", F32, "mix_bwuq")
    dckvn = _mm(dkvf, w_ukv, "nt", F32, "mix_bukv")
    dw_ukv = _mm(ckvn, dkvf, "tn", F32, "mix_bwukv")

    cq, ckv, kr = (proj, 512, 4), (proj, 256, 10), (proj, LANES, 22)

    def mla_bwd(r, p):
        _, vjp = jax.vjp(_mla_prep_core, r[0][:, :MLA_Q_LORA], r[1], r[2], r[3], r[4], p[0], p[1], p[2])
        dcq, dckv, dkr, _, _, dwq, dwkv, dwkr = vjp((r[5], r[6], r[7]))
        pad = jnp.zeros((dcq.shape[0], 512 - MLA_Q_LORA), F32)
        return [jnp.concatenate([dcq, pad], 1), dckv, dkr], [dwq, dwkv, dwkr]

    dcq, dckv, dkr, dwq, dwkv, dwkr = _rowcall(
        mla_bwd, [cq, ckv, kr, cos2, sin2, dcqn, dckvn, dk_rope], [wq, wkv, wkr],
        [(512, F32), (MLA_KV_LORA, F32), (LANES, F32)], [(1, MLA_Q_LORA), (1, MLA_KV_LORA), (1, LANES)],
        tile=512, name="mix_bmla_prep")

    gdn_local, gdn_states = states
    d_local = _gdn_scan_bwd(*gdn_local, gdn_states, do_gdn, "mix_bgdn_scan")
    dq, dk, dv, dgb = _gdn_local_bwd(q, k, v, gb, *d_local, "mix_bgdn_local")
    gab = (proj, LANES, 23)

    def gdn_prep_bwd(r, p):
        _, vjp = jax.vjp(_gdn_prep_core, _split(r[0], [HEAD] * 12), r[1], p[0], p[1])
        dparts, dgab, da_log, ddt = vjp((r[2], r[3], r[4], r[5]))
        return [jnp.concatenate(dparts, 1), dgab], [da_log, ddt]

    dqkv_c, dgab, da_log, ddt = _rowcall(gdn_prep_bwd, [qkv_c, gab, dq, dk, dv, dgb], [a_log, dt_bias],
                                         [(1536, F32), (LANES, F32)], [(1, LANES), (1, LANES)], tile=256,
                                         name="mix_bgdn_prep")
    dqkv_pre, dconv8 = _conv_bwd(proj, dqkv_c, conv8, "mix_bconv")

    dproj = jnp.concatenate([dqkv_pre.astype(BF16), dgz.astype(BF16), dcq.astype(BF16), dckv.astype(BF16),
                             dkr.astype(BF16), dgab.astype(BF16)], axis=1)
    dh = _mm(dproj, w_in_p, "nt", F32, "mix_bin")
    dw_in_p = _mm(h, dproj, "tn", F32, "mix_bwin")

    def mod_bwd(r, p):
        _, vjp = jax.vjp(_modulate, r[0], p[0], p[1])
        dx, ds, dsh = vjp(r[1])
        return [r[2] + dx], [ds, dsh]

    dx, ds, dsh = _rowcall(mod_bwd, [x, dh, dy], [s, sh], [(d, F32)], [(1, d), (1, d)], tile=512, name="mix_bmod")
    small = dict(conv=dconv8, a_log=da_log, dt=ddt, wn=dwn, wq=dwq, wkv=dwkv, wqn=dwqn, wqr=dwqr, wkn=dwkn,
                 wkr=dwkr, won=dwon)
    return dx, (dsh, ds, dg), dw_in_p, dw_uq_p, dw_ukv, dw_out, small


def _pad_cols(a, n):
    return jnp.pad(a, ((0, 0),) * (a.ndim - 1) + ((0, n - a.shape[-1]),))


def _pack_w_in(w):
    z = lambda n: jnp.zeros((w.shape[0], n), w.dtype)
    return jnp.concatenate([w[:, 0:2048], w[:, 2056:2440], z(128), w[:, 2440:2696], w[:, 2696:2760], z(64),
                            w[:, 2048:2056], z(120)], axis=1)


def _unpack_w_in(wp):
    return jnp.concatenate([wp[:, 0:2048], wp[:, 2944:2952], wp[:, 2048:2432], wp[:, 2560:2816], wp[:, 2816:2880]],
                           axis=1)


def _pack_w_uq(w):
    z = jnp.zeros((w.shape[0], LANES - MLA_ROPE), w.dtype)
    nope = [w[:, h * 192:h * 192 + HEAD] for h in range(MLA_HEADS)]
    rope = []
    for h in range(MLA_HEADS):
        rope += [w[:, h * 192 + HEAD:(h + 1) * 192], z]
    return jnp.concatenate(nope + rope, axis=1)


def _unpack_w_uq(wp):
    cols = []
    for h in range(MLA_HEADS):
        cols += [wp[:, h * HEAD:(h + 1) * HEAD], wp[:, 512 + h * LANES:512 + h * LANES + MLA_ROPE]]
    return jnp.concatenate(cols, axis=1)


def _cols_to_chips(a):
    r, c = a.shape
    return a.reshape(r, 4, c // 4).transpose(1, 0, 2)


def _chips_to_cols(a):
    _, r, n = a.shape
    return a.transpose(1, 0, 2).reshape(r, 4 * n)


def _pad128(v, n=LANES):
    return _pad_cols(v.reshape(1, -1), n)


def kernel(x, c, positions, w_ada, b_ada, ffn1_w_in, ffn1_w_out, w_in, gdn_conv_w, gdn_a_log, gdn_dt_bias, gdn_norm_w, mla_q_norm_w, mla_w_uq, mla_kv_norm_w, mla_w_ukv, qkn_q_nope, qkn_q_rope, qkn_k_nope, qkn_k_rope, mla_out_norm_w, w_out, ffn2_w_in, ffn2_w_out, loss_target, m_w_ada, m_b_ada, m_ffn1_w_in, m_ffn1_w_out, m_w_in, m_gdn_conv_w, m_gdn_a_log, m_gdn_dt_bias, m_gdn_norm_w, m_mla_q_norm_w, m_mla_w_uq, m_mla_kv_norm_w, m_mla_w_ukv, m_qkn_q_nope, m_qkn_q_rope, m_qkn_k_nope, m_qkn_k_rope, m_mla_out_norm_w, m_w_out, m_ffn2_w_in, m_ffn2_w_out, v_w_ada, v_b_ada, v_ffn1_w_in, v_ffn1_w_out, v_w_in, v_gdn_conv_w, v_gdn_a_log, v_gdn_dt_bias, v_gdn_norm_w, v_mla_q_norm_w, v_mla_w_uq, v_mla_kv_norm_w, v_mla_w_ukv, v_qkn_q_nope, v_qkn_q_rope, v_qkn_k_nope, v_qkn_k_rope, v_mla_out_norm_w, v_w_out, v_ffn2_w_in, v_ffn2_w_out):
    weights = dict(w_ada=w_ada, b_ada=b_ada, ffn1_w_in=ffn1_w_in, ffn1_w_out=ffn1_w_out, w_in=w_in,
                   gdn_conv_w=gdn_conv_w, gdn_a_log=gdn_a_log, gdn_dt_bias=gdn_dt_bias, gdn_norm_w=gdn_norm_w,
                   mla_q_norm_w=mla_q_norm_w, mla_w_uq=mla_w_uq, mla_kv_norm_w=mla_kv_norm_w, mla_w_ukv=mla_w_ukv,
                   qkn_q_nope=qkn_q_nope, qkn_q_rope=qkn_q_rope, qkn_k_nope=qkn_k_nope, qkn_k_rope=qkn_k_rope,
                   mla_out_norm_w=mla_out_norm_w, w_out=w_out, ffn2_w_in=ffn2_w_in, ffn2_w_out=ffn2_w_out)
    moms_m = dict(w_ada=m_w_ada, b_ada=m_b_ada, ffn1_w_in=m_ffn1_w_in, ffn1_w_out=m_ffn1_w_out, w_in=m_w_in,
                  gdn_conv_w=m_gdn_conv_w, gdn_a_log=m_gdn_a_log, gdn_dt_bias=m_gdn_dt_bias,
                  gdn_norm_w=m_gdn_norm_w, mla_q_norm_w=m_mla_q_norm_w, mla_w_uq=m_mla_w_uq,
                  mla_kv_norm_w=m_mla_kv_norm_w, mla_w_ukv=m_mla_w_ukv, qkn_q_nope=m_qkn_q_nope,
                  qkn_q_rope=m_qkn_q_rope, qkn_k_nope=m_qkn_k_nope, qkn_k_rope=m_qkn_k_rope,
                  mla_out_norm_w=m_mla_out_norm_w, w_out=m_w_out, ffn2_w_in=m_ffn2_w_in, ffn2_w_out=m_ffn2_w_out)
    moms_v = dict(w_ada=v_w_ada, b_ada=v_b_ada, ffn1_w_in=v_ffn1_w_in, ffn1_w_out=v_ffn1_w_out, w_in=v_w_in,
                  gdn_conv_w=v_gdn_conv_w, gdn_a_log=v_gdn_a_log, gdn_dt_bias=v_gdn_dt_bias,
                  gdn_norm_w=v_gdn_norm_w, mla_q_norm_w=v_mla_q_norm_w, mla_w_uq=v_mla_w_uq,
                  mla_kv_norm_w=v_mla_kv_norm_w, mla_w_ukv=v_mla_w_ukv, qkn_q_nope=v_qkn_q_nope,
                  qkn_q_rope=v_qkn_q_rope, qkn_k_nope=v_qkn_k_nope, qkn_k_rope=v_qkn_k_rope,
                  mla_out_norm_w=v_mla_out_norm_w, w_out=v_w_out, ffn2_w_in=v_ffn2_w_in, ffn2_w_out=v_ffn2_w_out)
    names = list(weights)

    seq, d = x.shape[1], x.shape[2]
    x2d = x.reshape(seq, d)
    tgt = loss_target.reshape(seq, d)
    mx, my, mc = _place()
    chip = 2 * mx + my
    me = 2 * chip + mc
    n_mod = b_ada.shape[1] // d
    shard = w_ada.shape[2]

    half = MLA_ROPE // 2
    inv_freq = 10000.0 ** (-jnp.arange(half, dtype=F32) / half)
    ang = positions.astype(F32).reshape(seq, 1) * inv_freq
    cosv, sinv = jnp.cos(ang), jnp.sin(ang)
    cos2 = _pad_cols(jnp.concatenate([cosv, cosv], 1), LANES)
    sin2 = _pad_cols(jnp.concatenate([-sinv, sinv], 1), LANES)
    rope = (cos2, sin2)

    c_all = _allgather8(jnp.pad(c, ((0, SUBLANES - 1), (0, 0))), "gather_c")[:, 0, :]
    (sc_all,) = _rowcall(lambda r, p: ([_silu(r[0])], []), [c_all], [], [(d, F32)], [], tile=8, name="ada_silu")
    mod_part = _mm(sc_all, w_ada[0], "nn", F32, "ada_mm", hi=True)
    mod_all = _allgather8(mod_part, "gather_mod")
    mod_rows = lax.dynamic_index_in_dim(mod_all, me, axis=1, keepdims=False)
    mod_raw = jnp.concatenate([mod_rows[2 * jj] for jj in range(4)], axis=0).reshape(1, 4 * shard)
    (mod,) = _rowcall(lambda r, p: ([r[0] + r[1]], []),
                      [jnp.pad(mod_raw, ((0, 7), (0, 0))), jnp.pad(b_ada, ((0, 7), (0, 0)))], [],
                      [(4 * shard, F32)], [], tile=8, name="ada_bias")
    mods = [mod[0:1, i * d:(i + 1) * d] for i in range(n_mod)]
    sh1, s1, g1, sh2, s2, g2, sh3, s3, g3 = mods

    def shard_bf16(w, pad_to=None):
        w2 = w[0].astype(BF16)
        return _pad_cols(w2, pad_to) if pad_to else w2

    def cols_of(got, w):
        return _chips_to_cols(got[:, :, :w.shape[2]])

    def rows_of(got):
        return got.reshape(4 * got.shape[1], got.shape[2])

    f1_in = cols_of(_allgather_chips(shard_bf16(ffn1_w_in), "gather_f1_in"), ffn1_w_in)
    mixer_shards = [shard_bf16(w_in, 768), shard_bf16(mla_w_uq, 256), shard_bf16(mla_w_ukv), shard_bf16(w_out)]
    conv_all = _allgather8(jnp.pad(gdn_conv_w[0], ((0, SUBLANES - CONV_K), (0, 0))), "gather_conv")
    conv8 = jnp.concatenate([conv_all[2 * jj] for jj in range(4)], axis=1)

    x1, sv1, got, f1_out = _ffn_fwd(x2d, s1, sh1, g1, f1_in, None, "ffn1",
                                    gather_in=[shard_bf16(ffn1_w_out)] + mixer_shards,
                                    gather_out=[shard_bf16(ffn2_w_in)])
    w_in_full, w_uq_full, w_ukv_full = cols_of(got[0], w_in), cols_of(got[1], mla_w_uq), cols_of(got[2], mla_w_ukv)
    w_out_full, f2_in = rows_of(got[3]), cols_of(got[4], ffn2_w_in)
    wts = (_pack_w_in(w_in_full), conv8, _pad128(gdn_a_log), _pad128(gdn_dt_bias), gdn_norm_w,
           mla_q_norm_w, _pack_w_uq(w_uq_full), mla_kv_norm_w, w_ukv_full, qkn_q_nope, _pad128(qkn_q_rope),
           qkn_k_nope, _pad128(qkn_k_rope), mla_out_norm_w, w_out_full)
    xm, svm, got = _mixer_fwd(x1, s2, sh2, g2, wts, rope, gather=[shard_bf16(ffn2_w_out)])
    f2_out = rows_of(got[0])
    x3, sv3, _, _ = _ffn_fwd(xm, s3, sh3, g3, f2_in, f2_out, "ffn2")

    def loss_fn(r, p):
        err = r[0] - r[1]
        part = 0.5 * jnp.sum(jnp.sum(err * err, axis=1, keepdims=True) * (1.0 / d), axis=0, keepdims=True)
        return [err * (1.0 / d)], [jnp.broadcast_to(part, (1, LANES))]

    dy, loss_part = _rowcall(loss_fn, [x3, tgt], [], [(d, F32)], [(1, LANES)], tile=512, name="loss")
    loss = lax.psum(loss_part[0, 0], ("x", "y", "c"))

    dxm, dmod3, dw_f2_in, dw_f2_out = _ffn_bwd(dy, sv3, s3, sh3, g3, f2_in, f2_out, "ffn2")
    dx1, dmod2, dw_in_p, dw_uq_p, dw_ukv, dw_out_m, small = _mixer_bwd(dxm, svm, s2, sh2, g2, wts, rope)
    dx0, dmod1, dw_f1_in, dw_f1_out = _ffn_bwd(dx1, sv1, s1, sh1, g1, f1_in, f1_out, "ffn1")
    grad_x = dx0.reshape(x.shape)

    dmod = jnp.concatenate(list(dmod1) + list(dmod2) + list(dmod3), axis=1)
    small_parts = [dmod, small["conv"][:CONV_K], small["a_log"], small["dt"], small["wn"], small["wq"],
                   small["wkv"], small["wqn"], small["wqr"], small["wkn"], small["wkr"], small["won"]]
    packed, offs = _pack_rows(small_parts)
    gathered = _allgather8(packed, "gather_small")
    total = _sum8(gathered, "sum_small")
    (g_b_ada, g_conv, g_a_log, g_dt, g_wn, g_wq, g_wkv, g_wqn, g_wqr, g_wkn, g_wkr, g_won) = _unpack_rows(
        total, offs, [p.shape for p in small_parts])
    dmod_all = _unpack_rows(gathered.reshape(-1, LANES),
                            [(dd * packed.shape[0] + offs[0][0], offs[0][1]) for dd in range(8)],
                            [dmod.shape] * 8)
    dmod_all = jnp.concatenate(dmod_all, axis=0)
    dmod_mine = lax.dynamic_slice_in_dim(dmod_all, chip * shard, shard, axis=1)

    def ada_grad(r, p):
        acc = jnp.zeros((r[0].shape[0], shard), F32)
        for b in range(8):
            acc = acc + r[0][:, b:b + 1] * p[0][b:b + 1, :]
        return [acc], []

    (g_w_ada,) = _rowcall(ada_grad, [_pad_cols(sc_all.T, LANES)], [dmod_mine], [(shard, F32)], [], tile=256,
                          name="ada_grad")

    grads = dict(
        w_ada=g_w_ada[None], b_ada=g_b_ada,
        gdn_conv_w=lax.dynamic_slice_in_dim(g_conv, chip * gdn_conv_w.shape[2], gdn_conv_w.shape[2], axis=1)[None],
        gdn_a_log=g_a_log[:, :GDN_HEADS], gdn_dt_bias=g_dt[:, :GDN_HEADS], gdn_norm_w=g_wn, mla_q_norm_w=g_wq,
        mla_kv_norm_w=g_wkv, qkn_q_nope=g_wqn, qkn_q_rope=g_wqr[:, :MLA_ROPE], qkn_k_nope=g_wkn,
        qkn_k_rope=g_wkr[:, :MLA_ROPE], mla_out_norm_w=g_won)

    def rs_cols(dw, name, pad_to=None):
        g4 = _cols_to_chips(dw).astype(BF16)
        n = g4.shape[2]
        if pad_to:
            g4 = _pad_cols(g4, pad_to)
        return _reduce_scatter_chips(g4, name)[:, :n][None]

    def rs_rows(dw, name):
        r, cc = dw.shape
        return _reduce_scatter_chips(dw.astype(BF16).reshape(4, r // 4, cc), name)[None]

    grads["ffn2_w_in"] = rs_cols(dw_f2_in, "rs_f2_in")
    grads["ffn2_w_out"] = rs_rows(dw_f2_out, "rs_f2_out")
    grads["w_in"] = rs_cols(_unpack_w_in(dw_in_p), "rs_w_in", 768)
    grads["mla_w_uq"] = rs_cols(_unpack_w_uq(dw_uq_p), "rs_w_uq", 256)
    grads["mla_w_ukv"] = rs_cols(dw_ukv, "rs_w_ukv")
    grads["w_out"] = rs_rows(dw_out_m, "rs_w_out")
    grads["ffn1_w_in"] = rs_cols(dw_f1_in, "rs_f1_in")
    grads["ffn1_w_out"] = rs_rows(dw_f1_out, "rs_f1_out")

    big = ["w_ada", "ffn1_w_in", "ffn1_w_out", "w_in", "mla_w_uq", "mla_w_ukv", "w_out", "ffn2_w_in", "ffn2_w_out"]
    delta, new_m, new_v = {}, {}, {}
    for nme in big:
        shp = weights[nme].shape
        dl, nm, nv = _adamw(weights[nme][0], grads[nme][0], moms_m[nme][0], moms_v[nme][0], "adamw_" + nme)
        delta[nme], new_m[nme], new_v[nme] = dl.reshape(shp), nm.reshape(shp), nv.reshape(shp)
    tiny = [nme for nme in names if nme not in big]
    shapes = [weights[nme].shape for nme in tiny]
    pw, poffs = _pack_rows([weights[nme] for nme in tiny])
    pg, _ = _pack_rows([grads[nme] for nme in tiny])
    pm, _ = _pack_rows([moms_m[nme] for nme in tiny])
    pv, _ = _pack_rows([moms_v[nme] for nme in tiny])
    pd, pnm, pnv = _adamw(pw, pg, pm, pv, "adamw_small")
    for nme, dl, nm, nv in zip(tiny, _unpack_rows(pd, poffs, shapes), _unpack_rows(pnm, poffs, shapes),
                               _unpack_rows(pnv, poffs, shapes)):
        delta[nme], new_m[nme], new_v[nme] = dl, nm, nv

    return (loss, grad_x, *[grads[nme].reshape(weights[nme].shape) for nme in names],
            *[delta[nme] for nme in names], *[new_m[nme] for nme in names], *[new_v[nme] for nme in names])
```

```python
import functools

import jax
import jax.numpy as jnp
from jax import lax
from jax.experimental import pallas as pl
from jax.experimental.pallas import tpu as pltpu

F32 = jnp.float32
BF16 = jnp.bfloat16
HI = lax.Precision.HIGHEST
MESH = pl.DeviceIdType.MESH

EPS = 1e-6
CHUNK = 64
D_FF = 2816
GDN_HEADS = 4
HEAD = 128
MLA_HEADS = 4
MLA_ROPE = 64
MLA_Q_LORA = 384
MLA_KV_LORA = 256
QK_PAD = 256
ATT_SCALE = (HEAD + MLA_ROPE) ** -0.5
N_PROJ = 3072

ADAM_LR, ADAM_B1, ADAM_B2, ADAM_EPS, ADAM_WD, ADAM_STEP = 0.001, 0.9, 0.999, 1e-08, 0.01, 10

LANES = 128
SUBLANES = 8
VMEM_LIMIT = 56 * 2 ** 20


def _params(sem=None):
    return pltpu.CompilerParams(dimension_semantics=sem, vmem_limit_bytes=VMEM_LIMIT)


def _pick(n, cap, align):
    best = None
    d = align
    while d <= min(n, cap):
        if n % d == 0:
            best = d
        d += align
    return best if best is not None else n


def _iota(shape, dim):
    return lax.broadcasted_iota(jnp.int32, shape, dim)


def _rowcall(fn, rows, params, out_rows, out_accs, *, tile, name):
    rows = [r if isinstance(r, tuple) else (r, r.shape[1], 0) for r in rows]
    s = rows[0][0].shape[-2]
    t = min(tile, s)
    n = s // t
    n_in = len(rows) + len(params)
    n_row_out = len(out_rows)

    in_specs = []
    for r in rows:
        if len(r) == 3:
            in_specs.append(pl.BlockSpec((t, r[1]), functools.partial(lambda i, b: (i, b), b=r[2])))
        else:
            in_specs.append(pl.BlockSpec((None, t, r[1]), functools.partial(lambda i, b, h: (h, i, b), b=r[2], h=r[3])))
    in_specs += [pl.BlockSpec(p.shape, lambda i: (0, 0)) for p in params]
    out_shape, out_specs = [], []
    for o in out_rows:
        if len(o) == 2:
            out_shape.append(jax.ShapeDtypeStruct((s, o[0]), o[1]))
            out_specs.append(pl.BlockSpec((t, o[0]), lambda i: (i, 0)))
        else:
            out_shape.append(jax.ShapeDtypeStruct((o[2], s, o[0]), o[1]))
            out_specs.append(pl.BlockSpec((o[2], t, o[0]), lambda i: (0, i, 0)))
    out_shape += [jax.ShapeDtypeStruct(shape, F32) for shape in out_accs]
    out_specs += [pl.BlockSpec(shape, lambda i: (0, 0)) for shape in out_accs]

    def body(*refs):
        ins = refs[:n_in]
        outs = refs[n_in:]
        i = pl.program_id(0)
        vals = [r[...] for r in ins]
        row_outs, acc_outs = fn(vals[:len(rows)], vals[len(rows):])
        for r, v in zip(outs[:n_row_out], row_outs):
            if isinstance(v, (list, tuple)):
                for hh, piece in enumerate(v):
                    r[hh] = piece.astype(r.dtype)
            else:
                r[...] = v.astype(r.dtype)
        if out_accs:
            @pl.when(i == 0)
            def _():
                for r in outs[n_row_out:]:
                    r[...] = jnp.zeros(r.shape, F32)
            for r, v in zip(outs[n_row_out:], acc_outs):
                r[...] += v

    res = pl.pallas_call(
        body, name=name, grid=(n,), in_specs=in_specs, out_specs=out_specs, out_shape=out_shape,
        compiler_params=_params(("arbitrary",) if out_accs else ("parallel",)),
    )(*[r[0] for r in rows], *params)
    return list(res)


MM_TILE_MN = 1536


def _mm(a, b, mode, out_dtype, name, hi=False, gather=(), chips=None):
    b_shape = b.shape
    if chips == "b":
        b_shape = (b.shape[1], 4 * b.shape[2])
    if mode == "nn":
        (m, k), (_, n) = a.shape, b_shape
        dims = (((1,), (0,)), ((), ()))
    elif mode == "nt":
        (m, k), (n, _) = a.shape, b_shape
        dims = (((1,), (1,)), ((), ()))
    else:
        (k, m), (_, n) = a.shape, b_shape
        dims = (((0,), (0,)), ((), ()))
    tm = _pick(m, MM_TILE_MN if mode == "tn" else 1024, LANES if mode == "tn" else 16)
    tn = _pick(n // 4 if chips and mode != "nt" else n, MM_TILE_MN, LANES)
    tk = _pick(k // 4 if chips and mode == "nt" else k, 1024 if mode == "tn" else MM_TILE_MN, LANES)
    nk = k // tk
    nb = (n // 4) // tn
    kb = (k // 4) // tk
    if mode == "nn":
        a_spec = pl.BlockSpec((tm, tk), lambda i, j, kk: (i, kk))
        b_spec = pl.BlockSpec((tk, tn), lambda i, j, kk: (kk, j))
        if chips == "b":
            b_spec = pl.BlockSpec((None, tk, tn), lambda i, j, kk: (j // nb, kk, j % nb))
    elif mode == "nt":
        a_spec = pl.BlockSpec((tm, tk), lambda i, j, kk: (i, kk))
        b_spec = pl.BlockSpec((tn, tk), lambda i, j, kk: (j, kk))
        if chips == "b":
            b_spec = pl.BlockSpec((None, tn, tk), lambda i, j, kk: (kk // kb, j, kk % kb))
    else:
        a_spec = pl.BlockSpec((tk, tm), lambda i, j, kk: (kk, i))
        b_spec = pl.BlockSpec((tk, tn), lambda i, j, kk: (kk, j))
    out_spec = pl.BlockSpec((tm, tn), lambda i, j, kk: (i, j))
    out_shape = jax.ShapeDtypeStruct((m, n), out_dtype)
    if chips == "out":
        out_spec = pl.BlockSpec((None, tm, tn), lambda i, j, kk: (j // nb, i, j % nb))
        out_shape = jax.ShapeDtypeStruct((4, m, n // 4), out_dtype)

    ng = len(gather)
    grid = (m // tm, n // tn, nk)
    steps = grid[0] * grid[1] * grid[2]

    def body(*refs):
        a_ref, b_ref = refs[:2]
        x_refs = refs[2:2 + ng]
        o_ref = refs[2 + ng]
        got_refs = refs[3 + ng:3 + 2 * ng]
        acc_ref = refs[3 + 2 * ng]
        kk = pl.program_id(2)
        if ng:
            sems = refs[4 + 2 * ng:]
            step = (pl.program_id(0) * grid[1] + pl.program_id(1)) * nk + kk

            def phase(ph):
                for slot in range(ng):
                    _gather_phase(ph, x_refs[slot], got_refs[slot], *sems, slot)

            pl.when(step == 0)(lambda: phase(0))
            pl.when(step == steps // 2)(lambda: phase(1))

        @pl.when(kk == 0)
        def _():
            acc_ref[...] = jnp.zeros(acc_ref.shape, F32)

        av, bv = a_ref[...], b_ref[...]
        if hi:
            acc_ref[...] += lax.dot_general(av, bv, dims, precision=HI, preferred_element_type=F32)
        else:
            acc_ref[...] += lax.dot_general(av.astype(BF16), bv.astype(BF16), dims,
                                            preferred_element_type=F32)

        @pl.when(kk == nk - 1)
        def _():
            o_ref[...] = acc_ref[...].astype(o_ref.dtype)

        if ng:
            pl.when(step == steps - 1)(lambda: phase(2))

    hbm = pl.BlockSpec(memory_space=pl.ANY)
    res = pl.pallas_call(
        body, name=name, grid=grid,
        in_specs=[a_spec, b_spec] + [hbm] * ng,
        out_specs=[out_spec] + [hbm] * ng,
        out_shape=[out_shape] + [jax.ShapeDtypeStruct((4,) + x.shape, x.dtype) for x in gather],
        scratch_shapes=[pltpu.VMEM((tm, tn), F32)] + (_gather_sems(ng) if ng else []),
        compiler_params=_params(("arbitrary",) * 3 if ng else ("parallel", "parallel", "arbitrary")),
    )(a, b, *gather)
    return res if ng else res[0]


def _rms(x, w=None, n=None):
    n = x.shape[-1] if n is None else n
    y = x * lax.rsqrt(jnp.sum(x * x, axis=-1, keepdims=True) * (1.0 / n) + EPS)
    return y if w is None else y * w


def _silu(x):
    return x * jax.nn.sigmoid(x)


def _softplus(x):
    return jnp.maximum(x, 0.0) + jnp.log1p(jnp.exp(-jnp.abs(x)))


def _split(x, widths):
    out, o = [], 0
    for w in widths:
        out.append(x[:, o:o + w])
        o += w
    return out


def _modulate(x, s, sh):
    return _rms(x) * (1.0 + s) + sh


def _rope_rot(x):
    r, c = _iota((LANES, LANES), 0), _iota((LANES, LANES), 1)
    half = MLA_ROPE // 2
    perm = (((r < half) & (c == r + half)) | ((r >= half) & (r < MLA_ROPE) & (c == r - half))).astype(F32)
    return jnp.dot(x, perm, precision=HI, preferred_element_type=F32)


def _rope(x, cos2, sin2):
    return x * cos2 + _rope_rot(x) * sin2


def _gdn_prep_core(qkv_parts, gab, a_log, dt_bias):
    act = [_silu(p) for p in qkv_parts]
    qs = [p * lax.rsqrt(jnp.sum(p * p, -1, keepdims=True) + EPS) * (HEAD ** -0.5) for p in act[:4]]
    ks = [p * lax.rsqrt(jnp.sum(p * p, -1, keepdims=True) + EPS) for p in act[4:8]]
    lane = _iota(gab.shape, 1)
    g = -jnp.exp(a_log) * _softplus(gab + dt_bias)
    beta = jax.nn.sigmoid(gab)
    gb = jnp.where(lane < GDN_HEADS, g, jnp.where(lane < 2 * GDN_HEADS, beta, 0.0))
    return (jnp.concatenate(qs, 1), jnp.concatenate(ks, 1), jnp.concatenate(act[8:], 1), gb)


def _mla_prep_core(cq, ckv, kr, cos2, sin2, wq, wkv, wkr):
    cqn = _rms(cq, wq)
    ckvn = _rms(ckv, wkv)
    k_rope = _rope(_rms(kr, wkr, MLA_ROPE), cos2, sin2)
    return cqn, ckvn, k_rope


def _qk_prep_core(qn_parts, qr_parts, kn_parts, v_parts, k_rope, cos2, sin2, wqn, wqr, wkn):
    qs, ks = [], []
    for h in range(MLA_HEADS):
        qn = _rms(qn_parts[h], wqn) * ATT_SCALE
        qr = _rope(_rms(qr_parts[h], wqr, MLA_ROPE), cos2, sin2) * ATT_SCALE
        qs.append(jnp.concatenate([qn, qr], 1))
        ks.append(jnp.concatenate([_rms(kn_parts[h], wkn), k_rope], 1))
    return qs, ks, list(v_parts)


def _mix_post_core(o_parts, gz_parts, ob_parts, wn, won):
    oa = [_rms(o, wn) * _silu(z) for o, z in zip(o_parts, gz_parts)]
    ob = [_rms(o, won) for o in ob_parts]
    return jnp.concatenate(oa + ob, 1)


CONV_K = 4
HALO = SUBLANES


def _conv_fwd(proj, w8, name):
    s = proj.shape[0]
    c = w8.shape[1]
    t = min(256, s)
    n = s // t
    hb = t // HALO

    def body(x_ref, prev_ref, w_ref, o_ref, buf):
        i = pl.program_id(0)
        buf[pl.ds(0, HALO), :] = jnp.where(i > 0, prev_ref[...], 0.0)
        buf[pl.ds(HALO, t), :] = x_ref[...]
        acc = jnp.zeros((t, c), F32)
        for k in range(CONV_K):
            acc = acc + w_ref[k:k + 1, :] * buf[pl.ds(HALO - (CONV_K - 1) + k, t), :]
        o_ref[...] = acc

    return pl.pallas_call(
        body, name=name, grid=(n,),
        in_specs=[pl.BlockSpec((t, c), lambda i: (i, 0)),
                  pl.BlockSpec((HALO, c), lambda i: (jnp.maximum(i * hb - 1, 0), 0)),
                  pl.BlockSpec(w8.shape, lambda i: (0, 0))],
        out_specs=pl.BlockSpec((t, c), lambda i: (i, 0)),
        out_shape=jax.ShapeDtypeStruct((s, c), F32),
        scratch_shapes=[pltpu.VMEM((t + HALO, c), F32)],
        compiler_params=_params(("parallel",)),
    )(proj, proj, w8)


def _conv_bwd(proj, dy, w8, name):
    s = proj.shape[0]
    c = w8.shape[1]
    t = min(256, s)
    n = s // t
    hb = t // HALO

    def body(x_ref, prev_ref, dy_ref, next_ref, w_ref, dx_ref, dw_ref, bufx, bufd):
        i = pl.program_id(0)
        bufx[pl.ds(0, HALO), :] = jnp.where(i > 0, prev_ref[...], 0.0)
        bufx[pl.ds(HALO, t), :] = x_ref[...]
        bufd[pl.ds(0, t), :] = dy_ref[...]
        bufd[pl.ds(t, HALO), :] = jnp.where(i < n - 1, next_ref[...], 0.0)

        @pl.when(i == 0)
        def _():
            dw_ref[...] = jnp.zeros(dw_ref.shape, F32)

        dyv = dy_ref[...]
        acc = jnp.zeros((t, c), F32)
        for k in range(CONV_K):
            acc = acc + w_ref[k:k + 1, :] * bufd[pl.ds(CONV_K - 1 - k, t), :]
            dw_ref[k:k + 1, :] += jnp.sum(dyv * bufx[pl.ds(HALO - (CONV_K - 1) + k, t), :], axis=0, keepdims=True)
        dx_ref[...] = acc

    return pl.pallas_call(
        body, name=name, grid=(n,),
        in_specs=[pl.BlockSpec((t, c), lambda i: (i, 0)),
                  pl.BlockSpec((HALO, c), lambda i: (jnp.maximum(i * hb - 1, 0), 0)),
                  pl.BlockSpec((t, c), lambda i: (i, 0)),
                  pl.BlockSpec((HALO, c), lambda i: (jnp.minimum((i + 1) * hb, s // HALO - 1), 0)),
                  pl.BlockSpec(w8.shape, lambda i: (0, 0))],
        out_specs=[pl.BlockSpec((t, c), lambda i: (i, 0)), pl.BlockSpec(w8.shape, lambda i: (0, 0))],
        out_shape=[jax.ShapeDtypeStruct((s, c), F32), jax.ShapeDtypeStruct(w8.shape, F32)],
        scratch_shapes=[pltpu.VMEM((t + HALO, c), F32), pltpu.VMEM((t + HALO, c), F32)],
        compiler_params=_params(("arbitrary",)),
    )(proj, proj, dy, dy, w8)


_B_NN = (((2,), (1,)), ((0,), (0,)))
_B_NT = (((2,), (2,)), ((0,), (0,)))
_B_TN = (((1,), (1,)), ((0,), (0,)))


def _dot3(a, b, dims):
    return lax.dot_general(a, b, dims, precision=lax.Precision.HIGH, preferred_element_type=F32)


def _bdot_hi(a, b):
    return _dot3(a, b, _B_NN)


class _Dots:
    nn = staticmethod(lambda a, b: _dot3(a, b, _B_NN))
    nt = staticmethod(lambda a, b: _dot3(a, b, _B_NT))
    tn = staticmethod(lambda a, b: _dot3(a, b, _B_TN))


def _unit_lower_inverse(a, dots):
    c = a.shape[-1]
    ri, ci = _iota(a.shape, 1), _iota(a.shape, 2)
    inner = (ri // 2) == (ci // 2)
    t = (ri == ci).astype(F32) - jnp.where(inner, a, 0.0)
    blk = 4
    while blk <= c:
        outer = (ri // blk) == (ci // blk)
        low = jnp.where(outer & jnp.logical_not(inner), a, 0.0)
        t = t - dots.nn(dots.nn(t, low), t)
        inner = outer
        blk *= 2
    return t


def _stack(xs):
    return jnp.concatenate([x[None] for x in xs], axis=0)


def _gdn_local(dots, q, k, v, gbs):
    b, c, _ = q.shape
    gcols, bcols = [], []
    for gb in gbs:
        lane = _iota(gb.shape, 1)
        for h in range(GDN_HEADS):
            gcols.append(jnp.sum(jnp.where(lane == h, gb, 0.0), axis=1, keepdims=True))
            bcols.append(jnp.sum(jnp.where(lane == GDN_HEADS + h, gb, 0.0), axis=1, keepdims=True))
    gcol, bcol = _stack(gcols), _stack(bcols)
    ri, ci = _iota((b, c, c), 1), _iota((b, c, c), 2)
    incl = ri >= ci
    tril = incl.astype(F32)
    g_cc = _bdot_hi(tril, jnp.broadcast_to(gcol, (b, c, c)))
    g_row = _bdot_hi(jnp.ones((b, c, c), F32), jnp.where(ri == ci, g_cc, 0.0))
    g_cl = _bdot_hi(tril, jnp.broadcast_to(gcol, (b, c, HEAD)))
    g_last = jnp.sum(jnp.broadcast_to(gcol, (b, c, HEAD)), axis=1, keepdims=True)
    decay = jnp.where(incl, jnp.exp(jnp.where(incl, g_cc - g_row, 0.0)), 0.0)
    kk = dots.nt(k, k)
    minv = _unit_lower_inverse(jnp.where(ri > ci, bcol * kk * decay, 0.0), dots)
    e_g = jnp.exp(g_cl)
    u = dots.nn(minv, v * bcol)
    wk = dots.nn(minv, k * (bcol * e_g))
    qk = dots.nt(q, k) * decay
    return u, wk, q * e_g, k * jnp.exp(g_last - g_cl), qk, jnp.exp(g_last)


def _gdn_scan(dots, states, u, wk, qd, kd, qk, gl_tile):
    lane, row = _iota(gl_tile.shape, 1), _iota(gl_tile.shape, 0)
    gl = _stack([
        jnp.sum(jnp.sum(jnp.where((lane == h) & (row == 0), gl_tile, 0.0), axis=1, keepdims=True),
                axis=0, keepdims=True) for h in range(GDN_HEADS)])
    v_new = u - dots.nn(wk, states)
    o = dots.nn(qd, states) + dots.nn(qk, v_new)
    return states * gl + dots.tn(kd, v_new), o


def _heads(x):
    return jnp.stack(_split(x, HW4))


GDN_W = GDN_HEADS * HEAD
HW4 = [HEAD] * GDN_HEADS
LOCAL_CHUNKS = 4
_CHUNK_ROWS = [pl.ds(cc * CHUNK, CHUNK) for cc in range(LOCAL_CHUNKS)]


def _chunk_heads(ref):
    return jnp.concatenate([_heads(ref[rows, :]) for rows in _CHUNK_ROWS], 0)


def _gdn_local_fwd(q, k, v, gb, name):
    s = q.shape[0]
    t = LOCAL_CHUNKS * CHUNK

    def body(q_ref, k_ref, v_ref, gb_ref, u_ref, wk_ref, qd_ref, kd_ref, qk_ref, gl_ref):
        u, wk, qd, kd, qk, gl = _gdn_local(_Dots, _chunk_heads(q_ref), _chunk_heads(k_ref),
                                           _chunk_heads(v_ref), [gb_ref[rows, :] for rows in _CHUNK_ROWS])
        lane = _iota((CHUNK, LANES), 1)
        for cc, rows in enumerate(_CHUNK_ROWS):
            gl_tile = jnp.zeros((CHUNK, LANES), F32)
            for h in range(GDN_HEADS):
                b, cols = cc * GDN_HEADS + h, pl.ds(h * HEAD, HEAD)
                u_ref[rows, cols] = u[b]
                wk_ref[rows, cols] = wk[b]
                qd_ref[rows, cols] = qd[b]
                kd_ref[rows, cols] = kd[b]
                qk_ref[h, rows, :] = qk[b]
                gl_tile = gl_tile + jnp.where(lane == h, gl[b], 0.0)
            gl_ref[rows, :] = gl_tile

    row = pl.BlockSpec((t, GDN_W), lambda i: (i, 0))
    lane = pl.BlockSpec((t, LANES), lambda i: (i, 0))
    qks = pl.BlockSpec((GDN_HEADS, t, CHUNK), lambda i: (0, i, 0))
    return pl.pallas_call(
        body, name=name, grid=(s // t,),
        in_specs=[row, row, row, lane],
        out_specs=[row, row, row, row, qks, lane],
        out_shape=[jax.ShapeDtypeStruct((s, GDN_W), F32)] * 4
        + [jax.ShapeDtypeStruct((GDN_HEADS, s, CHUNK), F32), jax.ShapeDtypeStruct((s, LANES), F32)],
        compiler_params=_params(("parallel",)),
    )(q, k, v, gb)


def _gdn_local_bwd(q, k, v, gb, du, dwk, dqd, dkd, dqk, dgl, name):
    s = q.shape[0]
    t = LOCAL_CHUNKS * CHUNK

    def body(q_ref, k_ref, v_ref, gb_ref, du_ref, dwk_ref, dqd_ref, dkd_ref, dqk_ref, dgl_ref,
             dq_ref, dk_ref, dv_ref, dgb_ref):
        _, vjp = jax.vjp(functools.partial(_gdn_local, _Dots), _chunk_heads(q_ref), _chunk_heads(k_ref),
                         _chunk_heads(v_ref), [gb_ref[rows, :] for rows in _CHUNK_ROWS])
        lane = _iota((CHUNK, LANES), 1)
        dqk = jnp.stack([dqk_ref[h, rows, :] for rows in _CHUNK_ROWS for h in range(GDN_HEADS)])
        dgl = jnp.stack([jnp.sum(jnp.where(lane == h, dgl_ref[rows, :], 0.0), axis=0, keepdims=True)
                         for rows in _CHUNK_ROWS for h in range(GDN_HEADS)])
        d_q, d_k, d_v, d_gbs = vjp((_chunk_heads(du_ref), _chunk_heads(dwk_ref), _chunk_heads(dqd_ref),
                                    _chunk_heads(dkd_ref), dqk, dgl))
        for cc, rows in enumerate(_CHUNK_ROWS):
            for h in range(GDN_HEADS):
                b, cols = cc * GDN_HEADS + h, pl.ds(h * HEAD, HEAD)
                dq_ref[rows, cols] = d_q[b]
                dk_ref[rows, cols] = d_k[b]
                dv_ref[rows, cols] = d_v[b]
            dgb_ref[rows, :] = d_gbs[cc]

    row = pl.BlockSpec((t, GDN_W), lambda i: (i, 0))
    lane = pl.BlockSpec((t, LANES), lambda i: (i, 0))
    qks = pl.BlockSpec((GDN_HEADS, t, CHUNK), lambda i: (0, i, 0))
    return pl.pallas_call(
        body, name=name, grid=(s // t,),
        in_specs=[row, row, row, lane, row, row, row, row, qks, lane],
        out_specs=[row, row, row, lane],
        out_shape=[jax.ShapeDtypeStruct((s, GDN_W), F32)] * 3 + [jax.ShapeDtypeStruct((s, LANES), F32)],
        compiler_params=_params(("parallel",)),
    )(q, k, v, gb, du, dwk, dqd, dkd, dqk, dgl)


def _gdn_scan_fwd(u, wk, qd, kd, qk, gl, name):
    s = u.shape[0]
    nc = s // CHUNK

    def body(u_ref, wk_ref, qd_ref, kd_ref, qk_ref, gl_ref, o_ref, st_ref, state):
        i = pl.program_id(0)

        @pl.when(i == 0)
        def _():
            state[...] = jnp.zeros(state.shape, F32)

        st_ref[...] = state[...]
        new_states, o = _gdn_scan(_Dots, state[...], _heads(u_ref[...]), _heads(wk_ref[...]),
                                  _heads(qd_ref[...]), _heads(kd_ref[...]), qk_ref[...], gl_ref[...])
        state[...] = new_states
        o_ref[...] = jnp.concatenate([o[h] for h in range(GDN_HEADS)], 1)

    row = pl.BlockSpec((CHUNK, GDN_W), lambda i: (i, 0))
    return pl.pallas_call(
        body, name=name, grid=(nc,),
        in_specs=[row, row, row, row, pl.BlockSpec((GDN_HEADS, CHUNK, CHUNK), lambda i: (0, i, 0)),
                  pl.BlockSpec((CHUNK, LANES), lambda i: (i, 0))],
        out_specs=[row, pl.BlockSpec((None, GDN_HEADS, HEAD, HEAD), lambda i: (i, 0, 0, 0))],
        out_shape=[jax.ShapeDtypeStruct((s, GDN_W), F32),
                   jax.ShapeDtypeStruct((nc, GDN_HEADS, HEAD, HEAD), F32)],
        scratch_shapes=[pltpu.VMEM((GDN_HEADS, HEAD, HEAD), F32)],
        compiler_params=_params(("arbitrary",)),
    )(u, wk, qd, kd, qk, gl)


def _gdn_scan_bwd(u, wk, qd, kd, qk, gl, st, do, name):
    s = u.shape[0]
    nc = s // CHUNK

    def body(u_ref, wk_ref, qd_ref, kd_ref, qk_ref, gl_ref, st_ref, do_ref,
             du_ref, dwk_ref, dqd_ref, dkd_ref, dqk_ref, dgl_ref, dstate):
        i = pl.program_id(0)

        @pl.when(i == 0)
        def _():
            dstate[...] = jnp.zeros(dstate.shape, F32)

        _, vjp = jax.vjp(functools.partial(_gdn_scan, _Dots), st_ref[...], _heads(u_ref[...]),
                         _heads(wk_ref[...]), _heads(qd_ref[...]), _heads(kd_ref[...]), qk_ref[...], gl_ref[...])
        d_states, d_u, d_wk, d_qd, d_kd, d_qk, d_gl = vjp((dstate[...], _heads(do_ref[...])))
        dstate[...] = d_states
        dqk_ref[...] = d_qk
        unheads = lambda x: jnp.concatenate([x[h] for h in range(GDN_HEADS)], 1)
        du_ref[...] = unheads(d_u)
        dwk_ref[...] = unheads(d_wk)
        dqd_ref[...] = unheads(d_qd)
        dkd_ref[...] = unheads(d_kd)
        dgl_ref[...] = d_gl

    rev = lambda i: (nc - 1 - i, 0)
    row = pl.BlockSpec((CHUNK, GDN_W), rev)
    lane = pl.BlockSpec((CHUNK, LANES), rev)
    qks = pl.BlockSpec((GDN_HEADS, CHUNK, CHUNK), lambda i: (0, nc - 1 - i, 0))
    return pl.pallas_call(
        body, name=name, grid=(nc,),
        in_specs=[row, row, row, row, qks, lane,
                  pl.BlockSpec((None, GDN_HEADS, HEAD, HEAD), lambda i: (nc - 1 - i, 0, 0, 0)), row],
        out_specs=[row, row, row, row, qks, lane],
        out_shape=[jax.ShapeDtypeStruct((s, GDN_W), F32)] * 4
        + [jax.ShapeDtypeStruct((GDN_HEADS, s, CHUNK), F32), jax.ShapeDtypeStruct((s, LANES), F32)],
        scratch_shapes=[pltpu.VMEM((GDN_HEADS, HEAD, HEAD), F32)],
        compiler_params=_params(("arbitrary",)),
    )(u, wk, qd, kd, qk, gl, st, do)


def _chunk_mask(i, j, t):
    r = i * t + _iota((t, t), 0)
    c = j * t + _iota((t, t), 1)
    return (r // CHUNK) >= (c // CHUNK)


ATT_TILE = 1024
ATT_Q_TILES = 1
ATT_BWD_TILE = 1024


def _attn_fwd(q, k, v, name, gather=()):
    nh, s = MLA_HEADS, q.shape[0]
    tk = min(ATT_TILE, s)
    tq = min(ATT_Q_TILES * tk, s)
    qk = tq // tk
    nq, n = s // tq, s // tk
    nt = (((1,), (1,)), ((), ()))
    ng = len(gather)
    steps = nh * nq * n

    def body(*refs):
        q_ref, k_ref, v_ref = refs[:3]
        x_refs = refs[3:3 + ng]
        o_ref, lse_ref = refs[3 + ng:5 + ng]
        got_refs = refs[5 + ng:5 + 2 * ng]
        m_sc, l_sc, acc_sc = refs[5 + 2 * ng:8 + 2 * ng]
        i, j = pl.program_id(1), pl.program_id(2)
        if ng:
            sems = refs[8 + 2 * ng:]
            step_no = (pl.program_id(0) * nq + i) * n + j

            def phase(ph):
                for slot in range(ng):
                    _gather_phase(ph, x_refs[slot], got_refs[slot], *sems, slot)

            pl.when(step_no == 0)(lambda: phase(0))
            pl.when(step_no == steps // 2)(lambda: phase(1))

        @pl.when(j == 0)
        def _():
            m_sc[...] = jnp.full(m_sc.shape, -jnp.inf, F32)
            l_sc[...] = jnp.zeros(l_sc.shape, F32)
            acc_sc[...] = jnp.zeros(acc_sc.shape, F32)

        def step(masked):
            sc = lax.dot_general(q_ref[...], k_ref[...], nt, preferred_element_type=F32)
            if masked:
                r = i * tq + _iota((tq, tk), 0)
                c = j * tk + _iota((tq, tk), 1)
                sc = jnp.where((r // CHUNK) >= (c // CHUNK), sc, -jnp.inf)
            m_prev = m_sc[:, :1]
            m_new = jnp.maximum(m_prev, jnp.max(sc, axis=1, keepdims=True))
            alpha = jnp.exp(m_prev - m_new)
            p = jnp.exp(sc - m_new)
            l_sc[...] = jnp.broadcast_to(alpha * l_sc[:, :1] + jnp.sum(p, axis=1, keepdims=True), l_sc.shape)
            acc_sc[...] = alpha * acc_sc[...] + jnp.dot(p.astype(BF16), v_ref[...], preferred_element_type=F32)
            m_sc[...] = jnp.broadcast_to(m_new, m_sc.shape)

        pl.when(j < i * qk)(lambda: step(False))
        pl.when(j // qk == i)(lambda: step(True))

        @pl.when(j == n - 1)
        def _():
            o_ref[...] = acc_sc[...] / l_sc[:, :1]
            lse_ref[...] = m_sc[...] + jnp.log(l_sc[...])

        if ng:
            pl.when(step_no == steps - 1)(lambda: phase(2))

    qrow = lambda h, i, j: (i, h)
    krow = lambda h, i, j: (jnp.minimum(j, (i + 1) * qk - 1), h)
    hbm = pl.BlockSpec(memory_space=pl.ANY)
    res = pl.pallas_call(
        body, name=name, grid=(nh, nq, n),
        in_specs=[pl.BlockSpec((tq, QK_PAD), qrow), pl.BlockSpec((tk, QK_PAD), krow),
                  pl.BlockSpec((tk, HEAD), krow)] + [hbm] * ng,
        out_specs=[pl.BlockSpec((tq, HEAD), qrow), pl.BlockSpec((None, tq, LANES), lambda h, i, j: (h, i, 0))]
        + [hbm] * ng,
        out_shape=[jax.ShapeDtypeStruct((s, nh * HEAD), F32), jax.ShapeDtypeStruct((nh, s, LANES), F32)]
        + [jax.ShapeDtypeStruct((4,) + x.shape, x.dtype) for x in gather],
        scratch_shapes=[pltpu.VMEM((tq, LANES), F32), pltpu.VMEM((tq, LANES), F32), pltpu.VMEM((tq, HEAD), F32)]
        + (_gather_sems(ng) if ng else []),
        compiler_params=_params(("arbitrary",) * 3 if ng else ("parallel", "parallel", "arbitrary")),
    )(q, k, v, *gather)
    return res[0], res[1], list(res[2:])


def _attn_bwd(q, k, v, o, do, lse, name):
    nh, s = MLA_HEADS, q.shape[0]
    t = min(ATT_BWD_TILE, s)
    n = s // t
    tn = (((0,), (0,)), ((), ()))
    nt = (((1,), (1,)), ((), ()))

    def body(q_ref, k_ref, v_ref, o_ref, do_ref, lse_ref, dq_ref, dk_ref, dv_ref, dk_acc, dv_acc, dq_acc):
        j, i = pl.program_id(1), pl.program_id(2)

        @pl.when(i + j == 0)
        def _():
            dq_acc[...] = jnp.zeros(dq_acc.shape, F32)

        @pl.when(i == 0)
        def _():
            dk_acc[...] = jnp.zeros(dk_acc.shape, F32)
            dv_acc[...] = jnp.zeros(dv_acc.shape, F32)

        def step(masked):
            qv, kv, do = q_ref[...], k_ref[...], do_ref[...]
            sc = lax.dot_general(qv, kv, nt, preferred_element_type=F32)
            p = jnp.exp(sc - lse_ref[:, :1])
            if masked:
                p = jnp.where(_chunk_mask(i, j, t), p, 0.0)
            dob = do.astype(BF16)
            dp = lax.dot_general(dob, v_ref[...], nt, preferred_element_type=F32)
            ds = (p * (dp - jnp.sum(do * o_ref[...], axis=1, keepdims=True))).astype(BF16)
            dv_acc[...] += lax.dot_general(p.astype(BF16), dob, tn, preferred_element_type=F32)
            dk_acc[...] += lax.dot_general(ds, qv, tn, preferred_element_type=F32)
            rows = pl.ds(pl.multiple_of(i * t, t), t)
            dq_acc[rows, :] += jnp.dot(ds, kv, preferred_element_type=F32)

        pl.when(i > j)(lambda: step(False))
        pl.when(i == j)(lambda: step(True))

        @pl.when(i == n - 1)
        def _():
            dk_ref[...] = dk_acc[...]
            dv_ref[...] = dv_acc[...]

        @pl.when(i + j == 2 * (n - 1))
        def _():
            dq_ref[...] = dq_acc[...]

    qrow = lambda h, j, i: (jnp.maximum(i, j), h)
    krow = lambda h, j, i: (j, h)
    return pl.pallas_call(
        body, name=name, grid=(nh, n, n),
        in_specs=[pl.BlockSpec((t, QK_PAD), qrow), pl.BlockSpec((t, QK_PAD), krow), pl.BlockSpec((t, HEAD), krow),
                  pl.BlockSpec((t, HEAD), qrow), pl.BlockSpec((t, HEAD), qrow),
                  pl.BlockSpec((None, t, LANES), lambda h, j, i: (h, jnp.maximum(i, j), 0))],
        out_specs=[pl.BlockSpec((s, QK_PAD), lambda h, j, i: (0, h)),
                   pl.BlockSpec((t, QK_PAD), krow), pl.BlockSpec((t, HEAD), krow)],
        out_shape=[jax.ShapeDtypeStruct((s, nh * QK_PAD), F32), jax.ShapeDtypeStruct((s, nh * QK_PAD), F32),
                   jax.ShapeDtypeStruct((s, nh * HEAD), F32)],
        scratch_shapes=[pltpu.VMEM((t, QK_PAD), F32), pltpu.VMEM((t, HEAD), F32), pltpu.VMEM((s, QK_PAD), F32)],
        compiler_params=_params(("arbitrary", "arbitrary", "arbitrary")),
    )(q, k, v, o, do, lse)


def _place():
    return lax.axis_index("x"), lax.axis_index("y"), lax.axis_index("c")


def _allgather8(x, name):
    r, c = x.shape

    def body(x_ref, out_ref, send_sems, recv_sems, local_sem):
        mx, my, mc = _place()
        me = 4 * mx + 2 * my + mc
        mine = pltpu.make_async_copy(x_ref, out_ref.at[me], local_sem)
        mine.start()
        copies = []
        for d in range(1, 8):
            px = 1 - mx if d & 4 else mx
            py = 1 - my if d & 2 else my
            pc = 1 - mc if d & 1 else mc
            cp = pltpu.make_async_remote_copy(
                src_ref=x_ref, dst_ref=out_ref.at[me], send_sem=send_sems.at[d - 1], recv_sem=recv_sems.at[d - 1],
                device_id=(px, py, pc), device_id_type=MESH)
            cp.start()
            copies.append(cp)
        for cp in copies:
            cp.wait()
        mine.wait()

    return pl.pallas_call(
        body, name=name,
        out_shape=jax.ShapeDtypeStruct((8, r, c), x.dtype),
        in_specs=[pl.BlockSpec(memory_space=pltpu.VMEM)],
        out_specs=pl.BlockSpec(memory_space=pltpu.VMEM),
        scratch_shapes=[pltpu.SemaphoreType.DMA((7,)), pltpu.SemaphoreType.DMA((7,)), pltpu.SemaphoreType.DMA],
        compiler_params=pltpu.CompilerParams(vmem_limit_bytes=VMEM_LIMIT),
    )(x)


def _allgather_chips(x, name):
    r, c = x.shape

    def body(x_ref, out_ref, send_sems, recv_sems, local_sems):
        for phase in range(3):
            _gather_phase(phase, x_ref, out_ref, send_sems, recv_sems, local_sems, 0)

    return pl.pallas_call(
        body, name=name,
        out_shape=jax.ShapeDtypeStruct((4, r, c), x.dtype),
        in_specs=[pl.BlockSpec(memory_space=pltpu.VMEM)],
        out_specs=pl.BlockSpec(memory_space=pltpu.VMEM),
        scratch_shapes=_gather_sems(1),
        compiler_params=pltpu.CompilerParams(vmem_limit_bytes=VMEM_LIMIT),
    )(x)


GATHER_COPIES = 6


def _gather_sems(n):
    return [pltpu.SemaphoreType.DMA((GATHER_COPIES * n,)), pltpu.SemaphoreType.DMA((GATHER_COPIES * n,)),
            pltpu.SemaphoreType.DMA((n,))]


def _gather_phase(phase, x_ref, out_ref, send_sems, recv_sems, local_sems, slot):
    mx, my, mc = _place()
    j = 2 * mx + my
    rh = x_ref.shape[0] // 2
    base = GATHER_COPIES * slot
    chips = [(1 - mx, my), (mx, 1 - my), (1 - mx, 1 - my)]
    sibling = (mx, my, 1 - mc)

    def half(jj, hc):
        return out_ref.at[jj, pl.ds(hc * rh, rh), :]

    def over_ici(kk, block):
        px, py = chips[kk]
        return pltpu.make_async_remote_copy(
            src_ref=x_ref.at[pl.ds(mc * rh, rh), :], dst_ref=half(block, mc), send_sem=send_sems.at[base + kk],
            recv_sem=recv_sems.at[base + kk], device_id=(px, py, mc), device_id_type=MESH)

    def to_sibling(kk, hc):
        px, py = chips[kk]
        blk = half(2 * px + py, hc)
        return pltpu.make_async_remote_copy(
            src_ref=blk, dst_ref=blk, send_sem=send_sems.at[base + 3 + kk], recv_sem=recv_sems.at[base + 3 + kk],
            device_id=sibling, device_id_type=MESH)

    mine = pltpu.make_async_copy(x_ref, out_ref.at[j], local_sems.at[slot])
    if phase == 0:
        mine.start()
        for kk in range(3):
            over_ici(kk, j).start()
    elif phase == 1:
        for kk, (px, py) in enumerate(chips):
            over_ici(kk, 2 * px + py).wait_recv()
            to_sibling(kk, mc).start()
    else:
        for kk in range(3):
            to_sibling(kk, 1 - mc).wait_recv()
        for kk in range(3):
            over_ici(kk, j).wait_send()
            to_sibling(kk, mc).wait_send()
        mine.wait()


RS_ROWS = 32


def _reduce_scatter_chips(g, name):
    _, r, c = g.shape
    rh = r // 2
    steps = rh // RS_ROWS

    def body(g_ref, out_ref, sib_ref, part_ref, got_ref, send_sems, recv_sems):
        mx, my, mc = _place()
        j = 2 * mx + my
        sibling = (mx, my, 1 - mc)
        chips = [(1 - mx, my), (mx, 1 - my), (1 - mx, 1 - my)]

        to_sib = pltpu.make_async_remote_copy(
            src_ref=g_ref.at[:, pl.ds((1 - mc) * rh, rh), :], dst_ref=sib_ref,
            send_sem=send_sems.at[0], recv_sem=recv_sems.at[0], device_id=sibling, device_id_type=MESH)
        to_sib.start()
        to_sib.wait()

        def add_sibling(step, carry):
            rows = pl.ds(pl.multiple_of(step * RS_ROWS, RS_ROWS), RS_ROWS)
            mine = g_ref[:, pl.ds(pl.multiple_of(mc * rh + step * RS_ROWS, RS_ROWS), RS_ROWS), :]
            part_ref[:, rows, :] = mine.astype(F32) + sib_ref[:, rows, :].astype(F32)
            return carry

        lax.fori_loop(0, steps, add_sibling, 0)

        def to_bf16(step, carry):
            rows = pl.ds(pl.multiple_of(step * RS_ROWS, RS_ROWS), RS_ROWS)
            sib_ref[:, rows, :] = part_ref[:, rows, :].astype(BF16)
            return carry

        lax.fori_loop(0, steps, to_bf16, 0)

        sends = []
        for kk, (px, py) in enumerate(chips):
            cp = pltpu.make_async_remote_copy(
                src_ref=sib_ref.at[2 * px + py], dst_ref=got_ref.at[kk],
                send_sem=send_sems.at[1 + kk], recv_sem=recv_sems.at[1 + kk],
                device_id=(px, py, mc), device_id_type=MESH)
            cp.start()
            sends.append(cp)
        for cp in sends:
            cp.wait()

        def total(step, carry):
            rows = pl.ds(pl.multiple_of(step * RS_ROWS, RS_ROWS), RS_ROWS)
            acc = part_ref[j, rows, :]
            for kk in range(3):
                acc = acc + got_ref[kk, rows, :].astype(F32)
            out_ref[pl.ds(pl.multiple_of(mc * rh + step * RS_ROWS, RS_ROWS), RS_ROWS), :] = acc
            return carry

        lax.fori_loop(0, steps, total, 0)

        done = pltpu.make_async_remote_copy(
            src_ref=out_ref.at[pl.ds(mc * rh, rh), :], dst_ref=out_ref.at[pl.ds(mc * rh, rh), :],
            send_sem=send_sems.at[4], recv_sem=recv_sems.at[4], device_id=sibling, device_id_type=MESH)
        done.start()
        done.wait_send()
        pltpu.make_async_remote_copy(
            src_ref=out_ref.at[pl.ds((1 - mc) * rh, rh), :], dst_ref=out_ref.at[pl.ds((1 - mc) * rh, rh), :],
            send_sem=send_sems.at[4], recv_sem=recv_sems.at[4], device_id=sibling, device_id_type=MESH).wait_recv()

    return pl.pallas_call(
        body, name=name,
        out_shape=jax.ShapeDtypeStruct((r, c), F32),
        in_specs=[pl.BlockSpec(memory_space=pltpu.VMEM)],
        out_specs=pl.BlockSpec(memory_space=pltpu.VMEM),
        scratch_shapes=[pltpu.VMEM((4, rh, c), BF16), pltpu.VMEM((4, rh, c), F32), pltpu.VMEM((3, rh, c), BF16),
                        pltpu.SemaphoreType.DMA((5,)), pltpu.SemaphoreType.DMA((5,))],
        compiler_params=pltpu.CompilerParams(vmem_limit_bytes=VMEM_LIMIT),
    )(g)


def _sum8(x, name):
    _, r, c = x.shape

    def body(x_ref, o_ref):
        acc = x_ref[0]
        for d in range(1, 8):
            acc = acc + x_ref[d]
        o_ref[...] = acc

    return pl.pallas_call(
        body, name=name, out_shape=jax.ShapeDtypeStruct((r, c), F32),
        in_specs=[pl.BlockSpec(memory_space=pltpu.VMEM)], out_specs=pl.BlockSpec(memory_space=pltpu.VMEM),
    )(x)


def _adamw(w, g, m, v, name):
    r, c = w.shape
    t = _pick(r, 256, SUBLANES)
    spec = pl.BlockSpec((t, c), lambda i: (i, 0))

    def body(w_ref, g_ref, m_ref, v_ref, d_ref, nm_ref, nv_ref):
        gv = g_ref[...]
        m_new = ADAM_B1 * m_ref[...] + (1.0 - ADAM_B1) * gv
        v_new = ADAM_B2 * v_ref[...] + (1.0 - ADAM_B2) * (gv * gv)
        m_hat = m_new / (1.0 - ADAM_B1 ** ADAM_STEP)
        v_hat = v_new / (1.0 - ADAM_B2 ** ADAM_STEP)
        d_ref[...] = -ADAM_LR * (m_hat / (jnp.sqrt(v_hat) + ADAM_EPS) + ADAM_WD * w_ref[...])
        nm_ref[...] = m_new
        nv_ref[...] = v_new

    return pl.pallas_call(
        body, name=name, grid=(r // t,), in_specs=[spec] * 4, out_specs=[spec] * 3,
        out_shape=[jax.ShapeDtypeStruct((r, c), F32)] * 3, compiler_params=_params(("parallel",)),
    )(w, g, m, v)


def _pack_rows(parts):
    rows, offs, o = [], [], 0
    for p in parts:
        f = p.reshape(-1)
        n = -(-f.shape[0] // (LANES * SUBLANES)) * SUBLANES
        rows.append(jnp.pad(f, (0, n * LANES - f.shape[0])).reshape(n, LANES))
        offs.append((o, n))
        o += n
    return jnp.concatenate(rows, 0), offs


def _unpack_rows(packed, offs, shapes):
    out = []
    for (o, n), shp in zip(offs, shapes):
        size = 1
        for d in shp:
            size *= d
        out.append(packed[o:o + n].reshape(-1)[:size].reshape(shp))
    return out


def _mm_hosting(a, b, mode, out_dtype, name, gather, chips=None):
    res = _mm(a, b, mode, out_dtype, name, gather=gather, chips=chips)
    return (res[0], list(res[1:])) if gather else (res, [])


def _ffn_fwd(x, s, sh, g, w_in, w_out, tag, gather_in=(), gather_out=()):
    (h,) = _rowcall(lambda r, p: ([_modulate(r[0], p[0], p[1])], []), [x], [s, sh], [(x.shape[1], BF16)], [],
                    tile=512, name=tag + "_mod")
    gu, got = _mm_hosting(h, w_in, "nn", BF16, tag + "_in", gather_in, chips="b")
    if w_out is None:
        first = got.pop(0)
        w_out = first.reshape(4 * first.shape[1], first.shape[2])
    (act,) = _rowcall(lambda r, p: ([_silu(r[0].astype(F32)) * r[1].astype(F32)], []),
                      [(gu, D_FF, 0), (gu, D_FF, 1)], [], [(D_FF, BF16)], [], tile=256, name=tag + "_act")
    f, got_out = _mm_hosting(act, w_out, "nn", F32, tag + "_out", gather_out)
    got = got + got_out
    (y,) = _rowcall(lambda r, p: ([r[0] + 0.5 * p[0] * r[1]], []), [x, f], [g], [(x.shape[1], F32)], [],
                    tile=512, name=tag + "_res")
    return y, (x, h, gu, act, f), got, w_out


def _ffn_bwd(dy, saved, s, sh, g, w_in, w_out, tag):
    x, h, gu, act, f = saved
    d = x.shape[1]
    df, dg = _rowcall(lambda r, p: ([0.5 * p[0] * r[0]], [0.5 * jnp.sum(r[0] * r[1], 0, keepdims=True)]),
                      [dy, f], [g], [(d, BF16)], [(1, d)], tile=512, name=tag + "_bres")
    da = _mm(df, w_out, "nt", BF16, tag + "_bout")
    dw_out = _mm(act, df, "tn", BF16, tag + "_bwout")

    def act_bwd(r, p):
        gate, up, dav = r[0].astype(F32), r[1].astype(F32), r[2].astype(F32)
        _, vjp = jax.vjp(lambda a, b: _silu(a) * b, gate, up)
        dgate, dup = vjp(dav)
        return [jnp.concatenate([dgate, dup], 1)], []

    (dgu,) = _rowcall(act_bwd, [(gu, D_FF, 0), (gu, D_FF, 1), da], [], [(2 * D_FF, BF16)], [], tile=256,
                      name=tag + "_bact")
    dh = _mm(dgu, w_in, "nt", F32, tag + "_bin", chips="b")
    dw_in = _mm(h, dgu, "tn", BF16, tag + "_bwin", chips="out")

    def mod_bwd(r, p):
        _, vjp = jax.vjp(_modulate, r[0], p[0], p[1])
        dx, ds, dsh = vjp(r[1])
        return [r[2] + dx], [ds, dsh]

    dx, ds, dsh = _rowcall(mod_bwd, [x, dh, dy], [s, sh], [(d, F32)], [(1, d), (1, d)], tile=512, name=tag + "_bmod")
    return dx, (dsh, ds, dg), dw_in, dw_out


def _mixer_fwd(x, s, sh, g, wts, rope, gather=()):
    w_in_p, conv8, a_log, dt_bias, wn, wq, w_uq_p, wkv, w_ukv, wqn, wqr, wkn, wkr, won, w_out = wts
    cos2, sin2 = rope
    d = x.shape[1]
    (h,) = _rowcall(lambda r, p: ([_modulate(r[0], p[0], p[1])], []), [x], [s, sh], [(d, BF16)], [],
                    tile=512, name="mix_mod")
    proj = _mm(h, w_in_p, "nn", F32, "mix_in")
    qkv_c = _conv_fwd(proj, conv8, "mix_conv")
    gab = (proj, LANES, 23)

    q, k, v, gb = _rowcall(
        lambda r, p: (list(_gdn_prep_core(_split(r[0], [HEAD] * 12), r[1], p[0], p[1])), []),
        [qkv_c, gab], [a_log, dt_bias], [(512, F32)] * 3 + [(LANES, F32)], [], tile=256, name="mix_gdn_prep")
    gdn_local = _gdn_local_fwd(q, k, v, gb, "mix_gdn_local")
    o_gdn, gdn_states = _gdn_scan_fwd(*gdn_local, "mix_gdn_scan")
    states = (gdn_local, gdn_states)

    cq, ckv, kr = (proj, 512, 4), (proj, 256, 10), (proj, LANES, 22)
    cqn, ckvn, k_rope = _rowcall(
        lambda r, p: (list(_mla_prep_core(r[0][:, :MLA_Q_LORA], r[1], r[2], r[3], r[4], p[0], p[1], p[2])), []),
        [cq, ckv, kr, cos2, sin2], [wq, wkv, wkr], [(MLA_Q_LORA, BF16), (MLA_KV_LORA, BF16), (LANES, F32)], [],
        tile=512, name="mix_mla_prep")
    qf = _mm(cqn, w_uq_p, "nn", F32, "mix_uq")
    kvf = _mm(ckvn, w_ukv, "nn", F32, "mix_ukv")

    def qk_prep(r, p):
        qparts = _split(r[0], [HEAD] * 8)
        kvparts = _split(r[1], [HEAD] * 8)
        qs, ks, vs = _qk_prep_core(qparts[:4], qparts[4:], kvparts[0::2], kvparts[1::2], r[2], r[3], r[4],
                                   p[0], p[1], p[2])
        return [jnp.concatenate(qs, 1), jnp.concatenate(ks, 1), jnp.concatenate(vs, 1)], []

    qa, ka, va = _rowcall(qk_prep, [qf, kvf, k_rope, cos2, sin2], [wqn, wqr, wkn],
                          [(4 * QK_PAD, BF16), (4 * QK_PAD, BF16), (4 * HEAD, BF16)], [], tile=256,
                          name="mix_qk_prep")
    o_b, lse, got = _attn_fwd(qa, ka, va, "mix_attn", gather=gather)
    if w_out is None:
        first = got.pop(0)
        w_out = first.reshape(4 * first.shape[1], first.shape[2])

    gz = (proj, 512, 3)
    (mixed,) = _rowcall(
        lambda r, p: ([_mix_post_core(_split(r[0], HW4), _split(r[1], HW4), _split(r[2], HW4), p[0], p[1])], []),
        [o_gdn, gz, o_b], [wn, won], [(2 * 512, BF16)], [], tile=512, name="mix_post")
    y = _mm(mixed, w_out, "nn", F32, "mix_out")
    (x_out,) = _rowcall(lambda r, p: ([r[0] + p[0] * r[1]], []), [x, y], [g], [(d, F32)], [], tile=512,
                        name="mix_res")
    saved = (x, h, proj, qkv_c, q, k, v, gb, states, o_gdn, cqn, ckvn, k_rope, qf, kvf, qa, ka, va, o_b, lse,
             mixed, y)
    return x_out, saved, got, w_out


def _mixer_bwd(dy, saved, s, sh, g, wts, rope):
    w_in_p, conv8, a_log, dt_bias, wn, wq, w_uq_p, wkv, w_ukv, wqn, wqr, wkn, wkr, won, w_out = wts
    cos2, sin2 = rope
    (x, h, proj, qkv_c, q, k, v, gb, states, o_gdn, cqn, ckvn, k_rope, qf, kvf, qa, ka, va, o_b, lse,
     mixed, y) = saved
    d = x.shape[1]
    dyb, dg = _rowcall(lambda r, p: ([p[0] * r[0]], [jnp.sum(r[0] * r[1], 0, keepdims=True)]),
                       [dy, y], [g], [(d, BF16)], [(1, d)], tile=512, name="mix_bres")
    dmixed = _mm(dyb, w_out, "nt", F32, "mix_bout")
    dw_out = _mm(mixed, dyb, "tn", BF16, "mix_bwout")

    gz = (proj, 512, 3)

    def post_bwd(r, p):
        _, vjp = jax.vjp(_mix_post_core, _split(r[0], HW4), _split(r[1], HW4), _split(r[2], HW4), p[0], p[1])
        do, dz, dob, dwn, dwon = vjp(r[3])
        return [jnp.concatenate(do, 1), jnp.concatenate(dz, 1), jnp.concatenate(dob, 1)], [dwn, dwon]

    do_gdn, dgz, do_b, dwn, dwon = _rowcall(post_bwd, [o_gdn, gz, o_b, dmixed], [wn, won], [(512, F32)] * 3,
                                            [(1, HEAD), (1, HEAD)], tile=256, name="mix_bpost")

    dqa, dka, dva = _attn_bwd(qa, ka, va, o_b, do_b, lse, "mix_battn")

    def qk_bwd(r, p):
        qparts = _split(r[0], [HEAD] * 8)
        kvparts = _split(r[1], [HEAD] * 8)
        _, vjp = jax.vjp(_qk_prep_core, qparts[:4], qparts[4:], kvparts[0::2], kvparts[1::2], r[2], r[3], r[4],
                         p[0], p[1], p[2])
        cot = (_split(r[5], [QK_PAD] * 4), _split(r[6], [QK_PAD] * 4), _split(r[7], HW4))
        dqn, dqr, dkn, dvp, dkrope, _, _, dwqn, dwqr, dwkn = vjp(cot)
        dkv = []
        for a, b in zip(dkn, dvp):
            dkv += [a, b]
        return [jnp.concatenate(list(dqn) + list(dqr), 1), jnp.concatenate(dkv, 1), dkrope], [dwqn, dwqr, dwkn]

    dqf, dkvf, dk_rope, dwqn, dwqr, dwkn = _rowcall(
        qk_bwd, [qf, kvf, k_rope, cos2, sin2, dqa, dka, dva], [wqn, wqr, wkn],
        [(8 * HEAD, BF16), (8 * HEAD, BF16), (LANES, F32)], [(1, HEAD)] * 3, tile=256, name="mix_bqk_prep")
    dcqn = _mm(dqf, w_uq_p, "nt", F32, "mix_buq")
    dw_uq_p = _mm(cqn, dqf, "tn", F32, "mix_bwuq")
    dckvn = _mm(dkvf, w_ukv, "nt", F32, "mix_bukv")
    dw_ukv = _mm(ckvn, dkvf, "tn", F32, "mix_bwukv")

    cq, ckv, kr = (proj, 512, 4), (proj, 256, 10), (proj, LANES, 22)

    def mla_bwd(r, p):
        _, vjp = jax.vjp(_mla_prep_core, r[0][:, :MLA_Q_LORA], r[1], r[2], r[3], r[4], p[0], p[1], p[2])
        dcq, dckv, dkr, _, _, dwq, dwkv, dwkr = vjp((r[5], r[6], r[7]))
        pad = jnp.zeros((dcq.shape[0], 512 - MLA_Q_LORA), F32)
        return [jnp.concatenate([dcq, pad], 1), dckv, dkr], [dwq, dwkv, dwkr]

    dcq, dckv, dkr, dwq, dwkv, dwkr = _rowcall(
        mla_bwd, [cq, ckv, kr, cos2, sin2, dcqn, dckvn, dk_rope], [wq, wkv, wkr],
        [(512, F32), (MLA_KV_LORA, F32), (LANES, F32)], [(1, MLA_Q_LORA), (1, MLA_KV_LORA), (1, LANES)],
        tile=512, name="mix_bmla_prep")

    gdn_local, gdn_states = states
    d_local = _gdn_scan_bwd(*gdn_local, gdn_states, do_gdn, "mix_bgdn_scan")
    dq, dk, dv, dgb = _gdn_local_bwd(q, k, v, gb, *d_local, "mix_bgdn_local")
    gab = (proj, LANES, 23)

    def gdn_prep_bwd(r, p):
        _, vjp = jax.vjp(_gdn_prep_core, _split(r[0], [HEAD] * 12), r[1], p[0], p[1])
        dparts, dgab, da_log, ddt = vjp((r[2], r[3], r[4], r[5]))
        return [jnp.concatenate(dparts, 1), dgab], [da_log, ddt]

    dqkv_c, dgab, da_log, ddt = _rowcall(gdn_prep_bwd, [qkv_c, gab, dq, dk, dv, dgb], [a_log, dt_bias],
                                         [(1536, F32), (LANES, F32)], [(1, LANES), (1, LANES)], tile=256,
                                         name="mix_bgdn_prep")
    dqkv_pre, dconv8 = _conv_bwd(proj, dqkv_c, conv8, "mix_bconv")

    dproj = jnp.concatenate([dqkv_pre.astype(BF16), dgz.astype(BF16), dcq.astype(BF16), dckv.astype(BF16),
                             dkr.astype(BF16), dgab.astype(BF16)], axis=1)
    dh = _mm(dproj, w_in_p, "nt", F32, "mix_bin")
    dw_in_p = _mm(h, dproj, "tn", F32, "mix_bwin")

    def mod_bwd(r, p):
        _, vjp = jax.vjp(_modulate, r[0], p[0], p[1])
        dx, ds, dsh = vjp(r[1])
        return [r[2] + dx], [ds, dsh]

    dx, ds, dsh = _rowcall(mod_bwd, [x, dh, dy], [s, sh], [(d, F32)], [(1, d), (1, d)], tile=512, name="mix_bmod")
    small = dict(conv=dconv8, a_log=da_log, dt=ddt, wn=dwn, wq=dwq, wkv=dwkv, wqn=dwqn, wqr=dwqr, wkn=dwkn,
                 wkr=dwkr, won=dwon)
    return dx, (dsh, ds, dg), dw_in_p, dw_uq_p, dw_ukv, dw_out, small


def _pad_cols(a, n):
    return jnp.pad(a, ((0, 0),) * (a.ndim - 1) + ((0, n - a.shape[-1]),))


def _pack_w_in(w):
    z = lambda n: jnp.zeros((w.shape[0], n), w.dtype)
    return jnp.concatenate([w[:, 0:2048], w[:, 2056:2440], z(128), w[:, 2440:2696], w[:, 2696:2760], z(64),
                            w[:, 2048:2056], z(120)], axis=1)


def _unpack_w_in(wp):
    return jnp.concatenate([wp[:, 0:2048], wp[:, 2944:2952], wp[:, 2048:2432], wp[:, 2560:2816], wp[:, 2816:2880]],
                           axis=1)


def _pack_w_uq(w):
    z = jnp.zeros((w.shape[0], LANES - MLA_ROPE), w.dtype)
    nope = [w[:, h * 192:h * 192 + HEAD] for h in range(MLA_HEADS)]
    rope = []
    for h in range(MLA_HEADS):
        rope += [w[:, h * 192 + HEAD:(h + 1) * 192], z]
    return jnp.concatenate(nope + rope, axis=1)


def _unpack_w_uq(wp):
    cols = []
    for h in range(MLA_HEADS):
        cols += [wp[:, h * HEAD:(h + 1) * HEAD], wp[:, 512 + h * LANES:512 + h * LANES + MLA_ROPE]]
    return jnp.concatenate(cols, axis=1)


def _cols_to_chips(a):
    r, c = a.shape
    return a.reshape(r, 4, c // 4).transpose(1, 0, 2)


def _chips_to_cols(a):
    _, r, n = a.shape
    return a.transpose(1, 0, 2).reshape(r, 4 * n)


def _pad128(v, n=LANES):
    return _pad_cols(v.reshape(1, -1), n)


def kernel(x, c, positions, w_ada, b_ada, ffn1_w_in, ffn1_w_out, w_in, gdn_conv_w, gdn_a_log, gdn_dt_bias, gdn_norm_w, mla_q_norm_w, mla_w_uq, mla_kv_norm_w, mla_w_ukv, qkn_q_nope, qkn_q_rope, qkn_k_nope, qkn_k_rope, mla_out_norm_w, w_out, ffn2_w_in, ffn2_w_out, loss_target, m_w_ada, m_b_ada, m_ffn1_w_in, m_ffn1_w_out, m_w_in, m_gdn_conv_w, m_gdn_a_log, m_gdn_dt_bias, m_gdn_norm_w, m_mla_q_norm_w, m_mla_w_uq, m_mla_kv_norm_w, m_mla_w_ukv, m_qkn_q_nope, m_qkn_q_rope, m_qkn_k_nope, m_qkn_k_rope, m_mla_out_norm_w, m_w_out, m_ffn2_w_in, m_ffn2_w_out, v_w_ada, v_b_ada, v_ffn1_w_in, v_ffn1_w_out, v_w_in, v_gdn_conv_w, v_gdn_a_log, v_gdn_dt_bias, v_gdn_norm_w, v_mla_q_norm_w, v_mla_w_uq, v_mla_kv_norm_w, v_mla_w_ukv, v_qkn_q_nope, v_qkn_q_rope, v_qkn_k_nope, v_qkn_k_rope, v_mla_out_norm_w, v_w_out, v_ffn2_w_in, v_ffn2_w_out):
    weights = dict(w_ada=w_ada, b_ada=b_ada, ffn1_w_in=ffn1_w_in, ffn1_w_out=ffn1_w_out, w_in=w_in,
                   gdn_conv_w=gdn_conv_w, gdn_a_log=gdn_a_log, gdn_dt_bias=gdn_dt_bias, gdn_norm_w=gdn_norm_w,
                   mla_q_norm_w=mla_q_norm_w, mla_w_uq=mla_w_uq, mla_kv_norm_w=mla_kv_norm_w, mla_w_ukv=mla_w_ukv,
                   qkn_q_nope=qkn_q_nope, qkn_q_rope=qkn_q_rope, qkn_k_nope=qkn_k_nope, qkn_k_rope=qkn_k_rope,
                   mla_out_norm_w=mla_out_norm_w, w_out=w_out, ffn2_w_in=ffn2_w_in, ffn2_w_out=ffn2_w_out)
    moms_m = dict(w_ada=m_w_ada, b_ada=m_b_ada, ffn1_w_in=m_ffn1_w_in, ffn1_w_out=m_ffn1_w_out, w_in=m_w_in,
                  gdn_conv_w=m_gdn_conv_w, gdn_a_log=m_gdn_a_log, gdn_dt_bias=m_gdn_dt_bias,
                  gdn_norm_w=m_gdn_norm_w, mla_q_norm_w=m_mla_q_norm_w, mla_w_uq=m_mla_w_uq,
                  mla_kv_norm_w=m_mla_kv_norm_w, mla_w_ukv=m_mla_w_ukv, qkn_q_nope=m_qkn_q_nope,
                  qkn_q_rope=m_qkn_q_rope, qkn_k_nope=m_qkn_k_nope, qkn_k_rope=m_qkn_k_rope,
                  mla_out_norm_w=m_mla_out_norm_w, w_out=m_w_out, ffn2_w_in=m_ffn2_w_in, ffn2_w_out=m_ffn2_w_out)
    moms_v = dict(w_ada=v_w_ada, b_ada=v_b_ada, ffn1_w_in=v_ffn1_w_in, ffn1_w_out=v_ffn1_w_out, w_in=v_w_in,
                  gdn_conv_w=v_gdn_conv_w, gdn_a_log=v_gdn_a_log, gdn_dt_bias=v_gdn_dt_bias,
                  gdn_norm_w=v_gdn_norm_w, mla_q_norm_w=v_mla_q_norm_w, mla_w_uq=v_mla_w_uq,
                  mla_kv_norm_w=v_mla_kv_norm_w, mla_w_ukv=v_mla_w_ukv, qkn_q_nope=v_qkn_q_nope,
                  qkn_q_rope=v_qkn_q_rope, qkn_k_nope=v_qkn_k_nope, qkn_k_rope=v_qkn_k_rope,
                  mla_out_norm_w=v_mla_out_norm_w, w_out=v_w_out, ffn2_w_in=v_ffn2_w_in, ffn2_w_out=v_ffn2_w_out)
    names = list(weights)

    seq, d = x.shape[1], x.shape[2]
    x2d = x.reshape(seq, d)
    tgt = loss_target.reshape(seq, d)
    mx, my, mc = _place()
    chip = 2 * mx + my
    me = 2 * chip + mc
    n_mod = b_ada.shape[1] // d
    shard = w_ada.shape[2]

    half = MLA_ROPE // 2
    inv_freq = 10000.0 ** (-jnp.arange(half, dtype=F32) / half)
    ang = positions.astype(F32).reshape(seq, 1) * inv_freq
    cosv, sinv = jnp.cos(ang), jnp.sin(ang)
    cos2 = _pad_cols(jnp.concatenate([cosv, cosv], 1), LANES)
    sin2 = _pad_cols(jnp.concatenate([-sinv, sinv], 1), LANES)
    rope = (cos2, sin2)

    c_all = _allgather8(jnp.pad(c, ((0, SUBLANES - 1), (0, 0))), "gather_c")[:, 0, :]
    (sc_all,) = _rowcall(lambda r, p: ([_silu(r[0])], []), [c_all], [], [(d, F32)], [], tile=8, name="ada_silu")
    mod_part = _mm(sc_all, w_ada[0], "nn", F32, "ada_mm", hi=True)
    mod_all = _allgather8(mod_part, "gather_mod")
    mod_rows = lax.dynamic_index_in_dim(mod_all, me, axis=1, keepdims=False)
    mod_raw = jnp.concatenate([mod_rows[2 * jj] for jj in range(4)], axis=0).reshape(1, 4 * shard)
    (mod,) = _rowcall(lambda r, p: ([r[0] + r[1]], []),
                      [jnp.pad(mod_raw, ((0, 7), (0, 0))), jnp.pad(b_ada, ((0, 7), (0, 0)))], [],
                      [(4 * shard, F32)], [], tile=8, name="ada_bias")
    mods = [mod[0:1, i * d:(i + 1) * d] for i in range(n_mod)]
    sh1, s1, g1, sh2, s2, g2, sh3, s3, g3 = mods

    def shard_bf16(w, pad_to=None):
        w2 = w[0].astype(BF16)
        return _pad_cols(w2, pad_to) if pad_to else w2

    def cols_of(got, w):
        return _chips_to_cols(got[:, :, :w.shape[2]])

    def rows_of(got):
        return got.reshape(4 * got.shape[1], got.shape[2])

    f1_in = _allgather_chips(shard_bf16(ffn1_w_in), "gather_f1_in")
    conv_all = _allgather8(jnp.pad(gdn_conv_w[0], ((0, SUBLANES - CONV_K), (0, 0))), "gather_conv")
    conv8 = jnp.concatenate([conv_all[2 * jj] for jj in range(4)], axis=1)

    x1, sv1, got, f1_out = _ffn_fwd(
        x2d, s1, sh1, g1, f1_in, None, "ffn1",
        gather_in=[shard_bf16(ffn1_w_out), shard_bf16(w_in, 768), shard_bf16(mla_w_uq, 256), shard_bf16(mla_w_ukv)])
    w_in_full, w_uq_full, w_ukv_full = cols_of(got[0], w_in), cols_of(got[1], mla_w_uq), cols_of(got[2], mla_w_ukv)
    wts = (_pack_w_in(w_in_full), conv8, _pad128(gdn_a_log), _pad128(gdn_dt_bias), gdn_norm_w,
           mla_q_norm_w, _pack_w_uq(w_uq_full), mla_kv_norm_w, w_ukv_full, qkn_q_nope, _pad128(qkn_q_rope),
           qkn_k_nope, _pad128(qkn_k_rope), mla_out_norm_w, None)
    xm, svm, got, w_out_full = _mixer_fwd(
        x1, s2, sh2, g2, wts, rope, gather=[shard_bf16(w_out), shard_bf16(ffn2_w_in), shard_bf16(ffn2_w_out)])
    wts = wts[:-1] + (w_out_full,)
    f2_in, f2_out = got[0], rows_of(got[1])
    x3, sv3, _, _ = _ffn_fwd(xm, s3, sh3, g3, f2_in, f2_out, "ffn2")

    def loss_fn(r, p):
        err = r[0] - r[1]
        part = 0.5 * jnp.sum(jnp.sum(err * err, axis=1, keepdims=True) * (1.0 / d), axis=0, keepdims=True)
        return [err * (1.0 / d)], [jnp.broadcast_to(part, (1, LANES))]

    dy, loss_part = _rowcall(loss_fn, [x3, tgt], [], [(d, F32)], [(1, LANES)], tile=512, name="loss")
    loss = lax.psum(loss_part[0, 0], ("x", "y", "c"))

    dxm, dmod3, dw_f2_in, dw_f2_out = _ffn_bwd(dy, sv3, s3, sh3, g3, f2_in, f2_out, "ffn2")
    dx1, dmod2, dw_in_p, dw_uq_p, dw_ukv, dw_out_m, small = _mixer_bwd(dxm, svm, s2, sh2, g2, wts, rope)
    dx0, dmod1, dw_f1_in, dw_f1_out = _ffn_bwd(dx1, sv1, s1, sh1, g1, f1_in, f1_out, "ffn1")
    grad_x = dx0.reshape(x.shape)

    dmod = jnp.concatenate(list(dmod1) + list(dmod2) + list(dmod3), axis=1)
    small_parts = [dmod, small["conv"][:CONV_K], small["a_log"], small["dt"], small["wn"], small["wq"],
                   small["wkv"], small["wqn"], small["wqr"], small["wkn"], small["wkr"], small["won"]]
    packed, offs = _pack_rows(small_parts)
    gathered = _allgather8(packed, "gather_small")
    total = _sum8(gathered, "sum_small")
    (g_b_ada, g_conv, g_a_log, g_dt, g_wn, g_wq, g_wkv, g_wqn, g_wqr, g_wkn, g_wkr, g_won) = _unpack_rows(
        total, offs, [p.shape for p in small_parts])
    dmod_all = _unpack_rows(gathered.reshape(-1, LANES),
                            [(dd * packed.shape[0] + offs[0][0], offs[0][1]) for dd in range(8)],
                            [dmod.shape] * 8)
    dmod_all = jnp.concatenate(dmod_all, axis=0)
    dmod_mine = lax.dynamic_slice_in_dim(dmod_all, chip * shard, shard, axis=1)

    def ada_grad(r, p):
        acc = jnp.zeros((r[0].shape[0], shard), F32)
        for b in range(8):
            acc = acc + r[0][:, b:b + 1] * p[0][b:b + 1, :]
        return [acc], []

    (g_w_ada,) = _rowcall(ada_grad, [_pad_cols(sc_all.T, LANES)], [dmod_mine], [(shard, F32)], [], tile=256,
                          name="ada_grad")

    grads = dict(
        w_ada=g_w_ada[None], b_ada=g_b_ada,
        gdn_conv_w=lax.dynamic_slice_in_dim(g_conv, chip * gdn_conv_w.shape[2], gdn_conv_w.shape[2], axis=1)[None],
        gdn_a_log=g_a_log[:, :GDN_HEADS], gdn_dt_bias=g_dt[:, :GDN_HEADS], gdn_norm_w=g_wn, mla_q_norm_w=g_wq,
        mla_kv_norm_w=g_wkv, qkn_q_nope=g_wqn, qkn_q_rope=g_wqr[:, :MLA_ROPE], qkn_k_nope=g_wkn,
        qkn_k_rope=g_wkr[:, :MLA_ROPE], mla_out_norm_w=g_won)

    def rs_cols(dw, name, pad_to=None):
        g4 = _cols_to_chips(dw).astype(BF16)
        n = g4.shape[2]
        if pad_to:
            g4 = _pad_cols(g4, pad_to)
        return _reduce_scatter_chips(g4, name)[:, :n][None]

    def rs_rows(dw, name):
        r, cc = dw.shape
        return _reduce_scatter_chips(dw.astype(BF16).reshape(4, r // 4, cc), name)[None]

    grads["ffn2_w_in"] = _reduce_scatter_chips(dw_f2_in, "rs_f2_in")[None]
    grads["ffn2_w_out"] = rs_rows(dw_f2_out, "rs_f2_out")
    grads["w_in"] = rs_cols(_unpack_w_in(dw_in_p), "rs_w_in", 768)
    grads["mla_w_uq"] = rs_cols(_unpack_w_uq(dw_uq_p), "rs_w_uq", 256)
    grads["mla_w_ukv"] = rs_cols(dw_ukv, "rs_w_ukv")
    grads["w_out"] = rs_rows(dw_out_m, "rs_w_out")
    grads["ffn1_w_in"] = _reduce_scatter_chips(dw_f1_in, "rs_f1_in")[None]
    grads["ffn1_w_out"] = rs_rows(dw_f1_out, "rs_f1_out")

    big = ["w_ada", "ffn1_w_in", "ffn1_w_out", "w_in", "mla_w_uq", "mla_w_ukv", "w_out", "ffn2_w_in", "ffn2_w_out"]
    delta, new_m, new_v = {}, {}, {}
    for nme in big:
        shp = weights[nme].shape
        dl, nm, nv = _adamw(weights[nme][0], grads[nme][0], moms_m[nme][0], moms_v[nme][0], "adamw_" + nme)
        delta[nme], new_m[nme], new_v[nme] = dl.reshape(shp), nm.reshape(shp), nv.reshape(shp)
    tiny = [nme for nme in names if nme not in big]
    shapes = [weights[nme].shape for nme in tiny]
    pw, poffs = _pack_rows([weights[nme] for nme in tiny])
    pg, _ = _pack_rows([grads[nme] for nme in tiny])
    pm, _ = _pack_rows([moms_m[nme] for nme in tiny])
    pv, _ = _pack_rows([moms_v[nme] for nme in tiny])
    pd, pnm, pnv = _adamw(pw, pg, pm, pv, "adamw_small")
    for nme, dl, nm, nv in zip(tiny, _unpack_rows(pd, poffs, shapes), _unpack_rows(pnm, poffs, shapes),
                               _unpack_rows(pnv, poffs, shapes)):
        delta[nme], new_m[nme], new_v[nme] = dl, nm, nv

    return (loss, grad_x, *[grads[nme].reshape(weights[nme].shape) for nme in names],
            *[delta[nme] for nme in names], *[new_m[nme] for nme in names], *[new_v[nme] for nme in names])
```

```python
import functools

import jax
import jax.numpy as jnp
from jax import lax
from jax.experimental import pallas as pl
from jax.experimental.pallas import tpu as pltpu

F32 = jnp.float32
BF16 = jnp.bfloat16
HI = lax.Precision.HIGHEST
MESH = pl.DeviceIdType.MESH

EPS = 1e-6
CHUNK = 64
D_FF = 2816
GDN_HEADS = 4
HEAD = 128
MLA_HEADS = 4
MLA_ROPE = 64
MLA_Q_LORA = 384
MLA_KV_LORA = 256
QK_PAD = 256
ATT_SCALE = (HEAD + MLA_ROPE) ** -0.5
N_PROJ = 3072

ADAM_LR, ADAM_B1, ADAM_B2, ADAM_EPS, ADAM_WD, ADAM_STEP = 0.001, 0.9, 0.999, 1e-08, 0.01, 10

LANES = 128
SUBLANES = 8
VMEM_LIMIT = 56 * 2 ** 20


def _params(sem=None):
    return pltpu.CompilerParams(dimension_semantics=sem, vmem_limit_bytes=VMEM_LIMIT)


def _pick(n, cap, align):
    best = None
    d = align
    while d <= min(n, cap):
        if n % d == 0:
            best = d
        d += align
    return best if best is not None else n


def _iota(shape, dim):
    return lax.broadcasted_iota(jnp.int32, shape, dim)


def _rowcall(fn, rows, params, out_rows, out_accs, *, tile, name):
    rows = [r if isinstance(r, tuple) else (r, r.shape[1], 0) for r in rows]
    s = rows[0][0].shape[-2]
    t = min(tile, s)
    n = s // t
    n_in = len(rows) + len(params)
    n_row_out = len(out_rows)

    in_specs = []
    for r in rows:
        if len(r) == 3:
            in_specs.append(pl.BlockSpec((t, r[1]), functools.partial(lambda i, b: (i, b), b=r[2])))
        else:
            in_specs.append(pl.BlockSpec((None, t, r[1]), functools.partial(lambda i, b, h: (h, i, b), b=r[2], h=r[3])))
    in_specs += [pl.BlockSpec(p.shape, lambda i: (0, 0)) for p in params]
    out_shape, out_specs = [], []
    for o in out_rows:
        if len(o) == 2:
            out_shape.append(jax.ShapeDtypeStruct((s, o[0]), o[1]))
            out_specs.append(pl.BlockSpec((t, o[0]), lambda i: (i, 0)))
        else:
            out_shape.append(jax.ShapeDtypeStruct((o[2], s, o[0]), o[1]))
            out_specs.append(pl.BlockSpec((o[2], t, o[0]), lambda i: (0, i, 0)))
    out_shape += [jax.ShapeDtypeStruct(shape, F32) for shape in out_accs]
    out_specs += [pl.BlockSpec(shape, lambda i: (0, 0)) for shape in out_accs]

    def body(*refs):
        ins = refs[:n_in]
        outs = refs[n_in:]
        i = pl.program_id(0)
        vals = [r[...] for r in ins]
        row_outs, acc_outs = fn(vals[:len(rows)], vals[len(rows):])
        for r, v in zip(outs[:n_row_out], row_outs):
            if isinstance(v, (list, tuple)):
                for hh, piece in enumerate(v):
                    r[hh] = piece.astype(r.dtype)
            else:
                r[...] = v.astype(r.dtype)
        if out_accs:
            @pl.when(i == 0)
            def _():
                for r in outs[n_row_out:]:
                    r[...] = jnp.zeros(r.shape, F32)
            for r, v in zip(outs[n_row_out:], acc_outs):
                r[...] += v

    res = pl.pallas_call(
        body, name=name, grid=(n,), in_specs=in_specs, out_specs=out_specs, out_shape=out_shape,
        compiler_params=_params(("arbitrary",) if out_accs else ("parallel",)),
    )(*[r[0] for r in rows], *params)
    return list(res)


MM_TILE_MN = 1536


class _Hosted:
    def __init__(self, gather=(), scatter=()):
        self.gather, self.scatter = list(gather), list(scatter)
        self.operands = self.gather + self.scatter
        self.n = len(self.operands)

    def specs(self):
        return [pl.BlockSpec(memory_space=pl.ANY)] * self.n

    def out_shapes(self):
        return ([jax.ShapeDtypeStruct((4,) + x.shape, x.dtype) for x in self.gather]
                + [jax.ShapeDtypeStruct((SCATTER_COPIES, g.shape[1] // 2, g.shape[2]), g.dtype) for g in self.scatter])

    def scratch(self):
        return ((_gather_sems(len(self.gather)) if self.gather else [])
                + (_scatter_sems(len(self.scatter)) if self.scatter else []))

    def _phase(self, ph, ins, outs, sems):
        ng = len(self.gather)
        g_sems, s_sems = (sems[:3], sems[3:]) if ng else ((), sems)
        for slot in range(ng):
            _gather_phase(ph, ins[slot], outs[slot], *g_sems, slot)
        if ph != 1:
            for slot in range(len(self.scatter)):
                _scatter_phase(0 if ph == 0 else 1, ins[ng + slot], outs[ng + slot], *s_sems, slot)

    def open(self, step, steps, ins, outs, sems):
        if self.n:
            pl.when(step == 0)(lambda: self._phase(0, ins, outs, sems))
            pl.when(step == steps // 2)(lambda: self._phase(1, ins, outs, sems))

    def close(self, step, steps, ins, outs, sems):
        if self.n:
            pl.when(step == steps - 1)(lambda: self._phase(2, ins, outs, sems))


def _mm(a, b, mode, out_dtype, name, hi=False, gather=(), chips=None, scatter=()):
    b_shape = b.shape
    if chips == "b":
        b_shape = (b.shape[1], 4 * b.shape[2])
    if mode == "nn":
        (m, k), (_, n) = a.shape, b_shape
        dims = (((1,), (0,)), ((), ()))
    elif mode == "nt":
        (m, k), (n, _) = a.shape, b_shape
        dims = (((1,), (1,)), ((), ()))
    else:
        (k, m), (_, n) = a.shape, b_shape
        dims = (((0,), (0,)), ((), ()))
    tm = _pick(m, MM_TILE_MN if mode == "tn" else 1024, LANES if mode == "tn" else 16)
    tn = _pick(n // 4 if chips and mode != "nt" else n, MM_TILE_MN, LANES)
    tk = _pick(k // 4 if chips and mode == "nt" else k, 1024 if mode == "tn" else MM_TILE_MN, LANES)
    nk = k // tk
    nb = (n // 4) // tn
    kb = (k // 4) // tk
    if mode == "nn":
        a_spec = pl.BlockSpec((tm, tk), lambda i, j, kk: (i, kk))
        b_spec = pl.BlockSpec((tk, tn), lambda i, j, kk: (kk, j))
        if chips == "b":
            b_spec = pl.BlockSpec((None, tk, tn), lambda i, j, kk: (j // nb, kk, j % nb))
    elif mode == "nt":
        a_spec = pl.BlockSpec((tm, tk), lambda i, j, kk: (i, kk))
        b_spec = pl.BlockSpec((tn, tk), lambda i, j, kk: (j, kk))
        if chips == "b":
            b_spec = pl.BlockSpec((None, tn, tk), lambda i, j, kk: (kk // kb, j, kk % kb))
    else:
        a_spec = pl.BlockSpec((tk, tm), lambda i, j, kk: (kk, i))
        b_spec = pl.BlockSpec((tk, tn), lambda i, j, kk: (kk, j))
    out_spec = pl.BlockSpec((tm, tn), lambda i, j, kk: (i, j))
    out_shape = jax.ShapeDtypeStruct((m, n), out_dtype)
    if chips == "out":
        out_spec = pl.BlockSpec((None, tm, tn), lambda i, j, kk: (j // nb, i, j % nb))
        out_shape = jax.ShapeDtypeStruct((4, m, n // 4), out_dtype)

    host = _Hosted(gather, scatter)
    ng = host.n
    grid = (m // tm, n // tn, nk)
    steps = grid[0] * grid[1] * grid[2]

    def body(*refs):
        a_ref, b_ref = refs[:2]
        x_refs = refs[2:2 + ng]
        o_ref = refs[2 + ng]
        got_refs = refs[3 + ng:3 + 2 * ng]
        acc_ref = refs[3 + 2 * ng]
        sems = refs[4 + 2 * ng:]
        kk = pl.program_id(2)
        step = (pl.program_id(0) * grid[1] + pl.program_id(1)) * nk + kk
        host.open(step, steps, x_refs, got_refs, sems)

        @pl.when(kk == 0)
        def _():
            acc_ref[...] = jnp.zeros(acc_ref.shape, F32)

        av, bv = a_ref[...], b_ref[...]
        if hi:
            acc_ref[...] += lax.dot_general(av, bv, dims, precision=HI, preferred_element_type=F32)
        else:
            acc_ref[...] += lax.dot_general(av.astype(BF16), bv.astype(BF16), dims,
                                            preferred_element_type=F32)

        @pl.when(kk == nk - 1)
        def _():
            o_ref[...] = acc_ref[...].astype(o_ref.dtype)

        host.close(step, steps, x_refs, got_refs, sems)

    res = pl.pallas_call(
        body, name=name, grid=grid,
        in_specs=[a_spec, b_spec] + host.specs(),
        out_specs=[out_spec] + host.specs(),
        out_shape=[out_shape] + host.out_shapes(),
        scratch_shapes=[pltpu.VMEM((tm, tn), F32)] + host.scratch(),
        compiler_params=_params(("arbitrary",) * 3 if ng else ("parallel", "parallel", "arbitrary")),
    )(a, b, *host.operands)
    return res if ng else res[0]


def _rms(x, w=None, n=None):
    n = x.shape[-1] if n is None else n
    y = x * lax.rsqrt(jnp.sum(x * x, axis=-1, keepdims=True) * (1.0 / n) + EPS)
    return y if w is None else y * w


def _silu(x):
    return x * jax.nn.sigmoid(x)


def _softplus(x):
    return jnp.maximum(x, 0.0) + jnp.log1p(jnp.exp(-jnp.abs(x)))


def _split(x, widths):
    out, o = [], 0
    for w in widths:
        out.append(x[:, o:o + w])
        o += w
    return out


def _modulate(x, s, sh):
    return _rms(x) * (1.0 + s) + sh


def _rope_rot(x):
    r, c = _iota((LANES, LANES), 0), _iota((LANES, LANES), 1)
    half = MLA_ROPE // 2
    perm = (((r < half) & (c == r + half)) | ((r >= half) & (r < MLA_ROPE) & (c == r - half))).astype(F32)
    return jnp.dot(x, perm, precision=HI, preferred_element_type=F32)


def _rope(x, cos2, sin2):
    return x * cos2 + _rope_rot(x) * sin2


def _gdn_prep_core(qkv_parts, gab, a_log, dt_bias):
    act = [_silu(p) for p in qkv_parts]
    qs = [p * lax.rsqrt(jnp.sum(p * p, -1, keepdims=True) + EPS) * (HEAD ** -0.5) for p in act[:4]]
    ks = [p * lax.rsqrt(jnp.sum(p * p, -1, keepdims=True) + EPS) for p in act[4:8]]
    lane = _iota(gab.shape, 1)
    g = -jnp.exp(a_log) * _softplus(gab + dt_bias)
    beta = jax.nn.sigmoid(gab)
    gb = jnp.where(lane < GDN_HEADS, g, jnp.where(lane < 2 * GDN_HEADS, beta, 0.0))
    return (jnp.concatenate(qs, 1), jnp.concatenate(ks, 1), jnp.concatenate(act[8:], 1), gb)


def _mla_prep_core(cq, ckv, kr, cos2, sin2, wq, wkv, wkr):
    cqn = _rms(cq, wq)
    ckvn = _rms(ckv, wkv)
    k_rope = _rope(_rms(kr, wkr, MLA_ROPE), cos2, sin2)
    return cqn, ckvn, k_rope


def _qk_prep_core(qn_parts, qr_parts, kn_parts, v_parts, k_rope, cos2, sin2, wqn, wqr, wkn):
    qs, ks = [], []
    for h in range(MLA_HEADS):
        qn = _rms(qn_parts[h], wqn) * ATT_SCALE
        qr = _rope(_rms(qr_parts[h], wqr, MLA_ROPE), cos2, sin2) * ATT_SCALE
        qs.append(jnp.concatenate([qn, qr], 1))
        ks.append(jnp.concatenate([_rms(kn_parts[h], wkn), k_rope], 1))
    return qs, ks, list(v_parts)


def _mix_post_core(o_parts, gz_parts, ob_parts, wn, won):
    oa = [_rms(o, wn) * _silu(z) for o, z in zip(o_parts, gz_parts)]
    ob = [_rms(o, won) for o in ob_parts]
    return jnp.concatenate(oa + ob, 1)


CONV_K = 4
HALO = SUBLANES


def _conv_fwd(proj, w8, name):
    s = proj.shape[0]
    c = w8.shape[1]
    t = min(256, s)
    n = s // t
    hb = t // HALO

    def body(x_ref, prev_ref, w_ref, o_ref, buf):
        i = pl.program_id(0)
        buf[pl.ds(0, HALO), :] = jnp.where(i > 0, prev_ref[...], 0.0)
        buf[pl.ds(HALO, t), :] = x_ref[...]
        acc = jnp.zeros((t, c), F32)
        for k in range(CONV_K):
            acc = acc + w_ref[k:k + 1, :] * buf[pl.ds(HALO - (CONV_K - 1) + k, t), :]
        o_ref[...] = acc

    return pl.pallas_call(
        body, name=name, grid=(n,),
        in_specs=[pl.BlockSpec((t, c), lambda i: (i, 0)),
                  pl.BlockSpec((HALO, c), lambda i: (jnp.maximum(i * hb - 1, 0), 0)),
                  pl.BlockSpec(w8.shape, lambda i: (0, 0))],
        out_specs=pl.BlockSpec((t, c), lambda i: (i, 0)),
        out_shape=jax.ShapeDtypeStruct((s, c), F32),
        scratch_shapes=[pltpu.VMEM((t + HALO, c), F32)],
        compiler_params=_params(("parallel",)),
    )(proj, proj, w8)


def _conv_bwd(proj, dy, w8, name):
    s = proj.shape[0]
    c = w8.shape[1]
    t = min(256, s)
    n = s // t
    hb = t // HALO

    def body(x_ref, prev_ref, dy_ref, next_ref, w_ref, dx_ref, dw_ref, bufx, bufd):
        i = pl.program_id(0)
        bufx[pl.ds(0, HALO), :] = jnp.where(i > 0, prev_ref[...], 0.0)
        bufx[pl.ds(HALO, t), :] = x_ref[...]
        bufd[pl.ds(0, t), :] = dy_ref[...]
        bufd[pl.ds(t, HALO), :] = jnp.where(i < n - 1, next_ref[...], 0.0)

        @pl.when(i == 0)
        def _():
            dw_ref[...] = jnp.zeros(dw_ref.shape, F32)

        dyv = dy_ref[...]
        acc = jnp.zeros((t, c), F32)
        for k in range(CONV_K):
            acc = acc + w_ref[k:k + 1, :] * bufd[pl.ds(CONV_K - 1 - k, t), :]
            dw_ref[k:k + 1, :] += jnp.sum(dyv * bufx[pl.ds(HALO - (CONV_K - 1) + k, t), :], axis=0, keepdims=True)
        dx_ref[...] = acc

    return pl.pallas_call(
        body, name=name, grid=(n,),
        in_specs=[pl.BlockSpec((t, c), lambda i: (i, 0)),
                  pl.BlockSpec((HALO, c), lambda i: (jnp.maximum(i * hb - 1, 0), 0)),
                  pl.BlockSpec((t, c), lambda i: (i, 0)),
                  pl.BlockSpec((HALO, c), lambda i: (jnp.minimum((i + 1) * hb, s // HALO - 1), 0)),
                  pl.BlockSpec(w8.shape, lambda i: (0, 0))],
        out_specs=[pl.BlockSpec((t, c), lambda i: (i, 0)), pl.BlockSpec(w8.shape, lambda i: (0, 0))],
        out_shape=[jax.ShapeDtypeStruct((s, c), F32), jax.ShapeDtypeStruct(w8.shape, F32)],
        scratch_shapes=[pltpu.VMEM((t + HALO, c), F32), pltpu.VMEM((t + HALO, c), F32)],
        compiler_params=_params(("arbitrary",)),
    )(proj, proj, dy, dy, w8)


_B_NN = (((2,), (1,)), ((0,), (0,)))
_B_NT = (((2,), (2,)), ((0,), (0,)))
_B_TN = (((1,), (1,)), ((0,), (0,)))


def _dot3(a, b, dims):
    return lax.dot_general(a, b, dims, precision=lax.Precision.HIGH, preferred_element_type=F32)


def _bdot_hi(a, b):
    return _dot3(a, b, _B_NN)


class _Dots:
    nn = staticmethod(lambda a, b: _dot3(a, b, _B_NN))
    nt = staticmethod(lambda a, b: _dot3(a, b, _B_NT))
    tn = staticmethod(lambda a, b: _dot3(a, b, _B_TN))


def _unit_lower_inverse(a, dots):
    c = a.shape[-1]
    ri, ci = _iota(a.shape, 1), _iota(a.shape, 2)
    inner = (ri // 2) == (ci // 2)
    t = (ri == ci).astype(F32) - jnp.where(inner, a, 0.0)
    blk = 4
    while blk <= c:
        outer = (ri // blk) == (ci // blk)
        low = jnp.where(outer & jnp.logical_not(inner), a, 0.0)
        t = t - dots.nn(dots.nn(t, low), t)
        inner = outer
        blk *= 2
    return t


def _stack(xs):
    return jnp.concatenate([x[None] for x in xs], axis=0)


def _gdn_local(dots, q, k, v, gbs):
    b, c, _ = q.shape
    gcols, bcols = [], []
    for gb in gbs:
        lane = _iota(gb.shape, 1)
        for h in range(GDN_HEADS):
            gcols.append(jnp.sum(jnp.where(lane == h, gb, 0.0), axis=1, keepdims=True))
            bcols.append(jnp.sum(jnp.where(lane == GDN_HEADS + h, gb, 0.0), axis=1, keepdims=True))
    gcol, bcol = _stack(gcols), _stack(bcols)
    ri, ci = _iota((b, c, c), 1), _iota((b, c, c), 2)
    incl = ri >= ci
    tril = incl.astype(F32)
    g_cc = _bdot_hi(tril, jnp.broadcast_to(gcol, (b, c, c)))
    g_row = _bdot_hi(jnp.ones((b, c, c), F32), jnp.where(ri == ci, g_cc, 0.0))
    g_cl = _bdot_hi(tril, jnp.broadcast_to(gcol, (b, c, HEAD)))
    g_last = jnp.sum(jnp.broadcast_to(gcol, (b, c, HEAD)), axis=1, keepdims=True)
    decay = jnp.where(incl, jnp.exp(jnp.where(incl, g_cc - g_row, 0.0)), 0.0)
    kk = dots.nt(k, k)
    minv = _unit_lower_inverse(jnp.where(ri > ci, bcol * kk * decay, 0.0), dots)
    e_g = jnp.exp(g_cl)
    u = dots.nn(minv, v * bcol)
    wk = dots.nn(minv, k * (bcol * e_g))
    qk = dots.nt(q, k) * decay
    return u, wk, q * e_g, k * jnp.exp(g_last - g_cl), qk, jnp.exp(g_last)


def _gdn_scan(dots, states, u, wk, qd, kd, qk, gl_tile):
    lane, row = _iota(gl_tile.shape, 1), _iota(gl_tile.shape, 0)
    gl = _stack([
        jnp.sum(jnp.sum(jnp.where((lane == h) & (row == 0), gl_tile, 0.0), axis=1, keepdims=True),
                axis=0, keepdims=True) for h in range(GDN_HEADS)])
    v_new = u - dots.nn(wk, states)
    o = dots.nn(qd, states) + dots.nn(qk, v_new)
    return states * gl + dots.tn(kd, v_new), o


def _heads(x):
    return jnp.stack(_split(x, HW4))


GDN_W = GDN_HEADS * HEAD
HW4 = [HEAD] * GDN_HEADS
LOCAL_CHUNKS = 4
_CHUNK_ROWS = [pl.ds(cc * CHUNK, CHUNK) for cc in range(LOCAL_CHUNKS)]


def _chunk_heads(ref):
    return jnp.concatenate([_heads(ref[rows, :]) for rows in _CHUNK_ROWS], 0)


def _gdn_local_fwd(q, k, v, gb, name):
    s = q.shape[0]
    t = LOCAL_CHUNKS * CHUNK

    def body(q_ref, k_ref, v_ref, gb_ref, u_ref, wk_ref, qd_ref, kd_ref, qk_ref, gl_ref):
        u, wk, qd, kd, qk, gl = _gdn_local(_Dots, _chunk_heads(q_ref), _chunk_heads(k_ref),
                                           _chunk_heads(v_ref), [gb_ref[rows, :] for rows in _CHUNK_ROWS])
        lane = _iota((CHUNK, LANES), 1)
        for cc, rows in enumerate(_CHUNK_ROWS):
            gl_tile = jnp.zeros((CHUNK, LANES), F32)
            for h in range(GDN_HEADS):
                b, cols = cc * GDN_HEADS + h, pl.ds(h * HEAD, HEAD)
                u_ref[rows, cols] = u[b]
                wk_ref[rows, cols] = wk[b]
                qd_ref[rows, cols] = qd[b]
                kd_ref[rows, cols] = kd[b]
                qk_ref[h, rows, :] = qk[b]
                gl_tile = gl_tile + jnp.where(lane == h, gl[b], 0.0)
            gl_ref[rows, :] = gl_tile

    row = pl.BlockSpec((t, GDN_W), lambda i: (i, 0))
    lane = pl.BlockSpec((t, LANES), lambda i: (i, 0))
    qks = pl.BlockSpec((GDN_HEADS, t, CHUNK), lambda i: (0, i, 0))
    return pl.pallas_call(
        body, name=name, grid=(s // t,),
        in_specs=[row, row, row, lane],
        out_specs=[row, row, row, row, qks, lane],
        out_shape=[jax.ShapeDtypeStruct((s, GDN_W), F32)] * 4
        + [jax.ShapeDtypeStruct((GDN_HEADS, s, CHUNK), F32), jax.ShapeDtypeStruct((s, LANES), F32)],
        compiler_params=_params(("parallel",)),
    )(q, k, v, gb)


def _gdn_local_bwd(q, k, v, gb, du, dwk, dqd, dkd, dqk, dgl, name):
    s = q.shape[0]
    t = LOCAL_CHUNKS * CHUNK

    def body(q_ref, k_ref, v_ref, gb_ref, du_ref, dwk_ref, dqd_ref, dkd_ref, dqk_ref, dgl_ref,
             dq_ref, dk_ref, dv_ref, dgb_ref):
        _, vjp = jax.vjp(functools.partial(_gdn_local, _Dots), _chunk_heads(q_ref), _chunk_heads(k_ref),
                         _chunk_heads(v_ref), [gb_ref[rows, :] for rows in _CHUNK_ROWS])
        lane = _iota((CHUNK, LANES), 1)
        dqk = jnp.stack([dqk_ref[h, rows, :] for rows in _CHUNK_ROWS for h in range(GDN_HEADS)])
        dgl = jnp.stack([jnp.sum(jnp.where(lane == h, dgl_ref[rows, :], 0.0), axis=0, keepdims=True)
                         for rows in _CHUNK_ROWS for h in range(GDN_HEADS)])
        d_q, d_k, d_v, d_gbs = vjp((_chunk_heads(du_ref), _chunk_heads(dwk_ref), _chunk_heads(dqd_ref),
                                    _chunk_heads(dkd_ref), dqk, dgl))
        for cc, rows in enumerate(_CHUNK_ROWS):
            for h in range(GDN_HEADS):
                b, cols = cc * GDN_HEADS + h, pl.ds(h * HEAD, HEAD)
                dq_ref[rows, cols] = d_q[b]
                dk_ref[rows, cols] = d_k[b]
                dv_ref[rows, cols] = d_v[b]
            dgb_ref[rows, :] = d_gbs[cc]

    row = pl.BlockSpec((t, GDN_W), lambda i: (i, 0))
    lane = pl.BlockSpec((t, LANES), lambda i: (i, 0))
    qks = pl.BlockSpec((GDN_HEADS, t, CHUNK), lambda i: (0, i, 0))
    return pl.pallas_call(
        body, name=name, grid=(s // t,),
        in_specs=[row, row, row, lane, row, row, row, row, qks, lane],
        out_specs=[row, row, row, lane],
        out_shape=[jax.ShapeDtypeStruct((s, GDN_W), F32)] * 3 + [jax.ShapeDtypeStruct((s, LANES), F32)],
        compiler_params=_params(("parallel",)),
    )(q, k, v, gb, du, dwk, dqd, dkd, dqk, dgl)


def _gdn_scan_fwd(u, wk, qd, kd, qk, gl, name):
    s = u.shape[0]
    nc = s // CHUNK

    def body(u_ref, wk_ref, qd_ref, kd_ref, qk_ref, gl_ref, o_ref, st_ref, state):
        i = pl.program_id(0)

        @pl.when(i == 0)
        def _():
            state[...] = jnp.zeros(state.shape, F32)

        st_ref[...] = state[...]
        new_states, o = _gdn_scan(_Dots, state[...], _heads(u_ref[...]), _heads(wk_ref[...]),
                                  _heads(qd_ref[...]), _heads(kd_ref[...]), qk_ref[...], gl_ref[...])
        state[...] = new_states
        o_ref[...] = jnp.concatenate([o[h] for h in range(GDN_HEADS)], 1)

    row = pl.BlockSpec((CHUNK, GDN_W), lambda i: (i, 0))
    return pl.pallas_call(
        body, name=name, grid=(nc,),
        in_specs=[row, row, row, row, pl.BlockSpec((GDN_HEADS, CHUNK, CHUNK), lambda i: (0, i, 0)),
                  pl.BlockSpec((CHUNK, LANES), lambda i: (i, 0))],
        out_specs=[row, pl.BlockSpec((None, GDN_HEADS, HEAD, HEAD), lambda i: (i, 0, 0, 0))],
        out_shape=[jax.ShapeDtypeStruct((s, GDN_W), F32),
                   jax.ShapeDtypeStruct((nc, GDN_HEADS, HEAD, HEAD), F32)],
        scratch_shapes=[pltpu.VMEM((GDN_HEADS, HEAD, HEAD), F32)],
        compiler_params=_params(("arbitrary",)),
    )(u, wk, qd, kd, qk, gl)


def _gdn_scan_bwd(u, wk, qd, kd, qk, gl, st, do, name):
    s = u.shape[0]
    nc = s // CHUNK

    def body(u_ref, wk_ref, qd_ref, kd_ref, qk_ref, gl_ref, st_ref, do_ref,
             du_ref, dwk_ref, dqd_ref, dkd_ref, dqk_ref, dgl_ref, dstate):
        i = pl.program_id(0)

        @pl.when(i == 0)
        def _():
            dstate[...] = jnp.zeros(dstate.shape, F32)

        _, vjp = jax.vjp(functools.partial(_gdn_scan, _Dots), st_ref[...], _heads(u_ref[...]),
                         _heads(wk_ref[...]), _heads(qd_ref[...]), _heads(kd_ref[...]), qk_ref[...], gl_ref[...])
        d_states, d_u, d_wk, d_qd, d_kd, d_qk, d_gl = vjp((dstate[...], _heads(do_ref[...])))
        dstate[...] = d_states
        dqk_ref[...] = d_qk
        unheads = lambda x: jnp.concatenate([x[h] for h in range(GDN_HEADS)], 1)
        du_ref[...] = unheads(d_u)
        dwk_ref[...] = unheads(d_wk)
        dqd_ref[...] = unheads(d_qd)
        dkd_ref[...] = unheads(d_kd)
        dgl_ref[...] = d_gl

    rev = lambda i: (nc - 1 - i, 0)
    row = pl.BlockSpec((CHUNK, GDN_W), rev)
    lane = pl.BlockSpec((CHUNK, LANES), rev)
    qks = pl.BlockSpec((GDN_HEADS, CHUNK, CHUNK), lambda i: (0, nc - 1 - i, 0))
    return pl.pallas_call(
        body, name=name, grid=(nc,),
        in_specs=[row, row, row, row, qks, lane,
                  pl.BlockSpec((None, GDN_HEADS, HEAD, HEAD), lambda i: (nc - 1 - i, 0, 0, 0)), row],
        out_specs=[row, row, row, row, qks, lane],
        out_shape=[jax.ShapeDtypeStruct((s, GDN_W), F32)] * 4
        + [jax.ShapeDtypeStruct((GDN_HEADS, s, CHUNK), F32), jax.ShapeDtypeStruct((s, LANES), F32)],
        scratch_shapes=[pltpu.VMEM((GDN_HEADS, HEAD, HEAD), F32)],
        compiler_params=_params(("arbitrary",)),
    )(u, wk, qd, kd, qk, gl, st, do)


def _chunk_mask(i, j, t):
    r = i * t + _iota((t, t), 0)
    c = j * t + _iota((t, t), 1)
    return (r // CHUNK) >= (c // CHUNK)


ATT_TILE = 1024
ATT_Q_TILES = 1
ATT_BWD_TILE = 1024


def _attn_fwd(q, k, v, name, gather=()):
    nh, s = MLA_HEADS, q.shape[0]
    tk = min(ATT_TILE, s)
    tq = min(ATT_Q_TILES * tk, s)
    qk = tq // tk
    nq, n = s // tq, s // tk
    nt = (((1,), (1,)), ((), ()))
    host = _Hosted(gather)
    ng = host.n
    steps = nh * nq * n

    def body(*refs):
        q_ref, k_ref, v_ref = refs[:3]
        x_refs = refs[3:3 + ng]
        o_ref, lse_ref = refs[3 + ng:5 + ng]
        got_refs = refs[5 + ng:5 + 2 * ng]
        m_sc, l_sc, acc_sc = refs[5 + 2 * ng:8 + 2 * ng]
        sems = refs[8 + 2 * ng:]
        i, j = pl.program_id(1), pl.program_id(2)
        step_no = (pl.program_id(0) * nq + i) * n + j
        host.open(step_no, steps, x_refs, got_refs, sems)

        @pl.when(j == 0)
        def _():
            m_sc[...] = jnp.full(m_sc.shape, -jnp.inf, F32)
            l_sc[...] = jnp.zeros(l_sc.shape, F32)
            acc_sc[...] = jnp.zeros(acc_sc.shape, F32)

        def step(masked):
            sc = lax.dot_general(q_ref[...], k_ref[...], nt, preferred_element_type=F32)
            if masked:
                r = i * tq + _iota((tq, tk), 0)
                c = j * tk + _iota((tq, tk), 1)
                sc = jnp.where((r // CHUNK) >= (c // CHUNK), sc, -jnp.inf)
            m_prev = m_sc[:, :1]
            m_new = jnp.maximum(m_prev, jnp.max(sc, axis=1, keepdims=True))
            alpha = jnp.exp(m_prev - m_new)
            p = jnp.exp(sc - m_new)
            l_sc[...] = jnp.broadcast_to(alpha * l_sc[:, :1] + jnp.sum(p, axis=1, keepdims=True), l_sc.shape)
            acc_sc[...] = alpha * acc_sc[...] + jnp.dot(p.astype(BF16), v_ref[...], preferred_element_type=F32)
            m_sc[...] = jnp.broadcast_to(m_new, m_sc.shape)

        pl.when(j < i * qk)(lambda: step(False))
        pl.when(j // qk == i)(lambda: step(True))

        @pl.when(j == n - 1)
        def _():
            o_ref[...] = acc_sc[...] / l_sc[:, :1]
            lse_ref[...] = m_sc[...] + jnp.log(l_sc[...])

        host.close(step_no, steps, x_refs, got_refs, sems)

    qrow = lambda h, i, j: (i, h)
    krow = lambda h, i, j: (jnp.minimum(j, (i + 1) * qk - 1), h)
    res = pl.pallas_call(
        body, name=name, grid=(nh, nq, n),
        in_specs=[pl.BlockSpec((tq, QK_PAD), qrow), pl.BlockSpec((tk, QK_PAD), krow),
                  pl.BlockSpec((tk, HEAD), krow)] + host.specs(),
        out_specs=[pl.BlockSpec((tq, HEAD), qrow), pl.BlockSpec((None, tq, LANES), lambda h, i, j: (h, i, 0))]
        + host.specs(),
        out_shape=[jax.ShapeDtypeStruct((s, nh * HEAD), F32), jax.ShapeDtypeStruct((nh, s, LANES), F32)]
        + host.out_shapes(),
        scratch_shapes=[pltpu.VMEM((tq, LANES), F32), pltpu.VMEM((tq, LANES), F32), pltpu.VMEM((tq, HEAD), F32)]
        + host.scratch(),
        compiler_params=_params(("arbitrary",) * 3 if ng else ("parallel", "parallel", "arbitrary")),
    )(q, k, v, *host.operands)
    return res[0], res[1], list(res[2:])


def _attn_bwd(q, k, v, o, do, lse, name, scatter=()):
    nh, s = MLA_HEADS, q.shape[0]
    t = min(ATT_BWD_TILE, s)
    n = s // t
    tn = (((0,), (0,)), ((), ()))
    nt = (((1,), (1,)), ((), ()))
    host = _Hosted(scatter=scatter)
    nx = host.n
    steps = nh * n * n

    def body(*refs):
        q_ref, k_ref, v_ref, o_ref, do_ref, lse_ref = refs[:6]
        x_refs = refs[6:6 + nx]
        dq_ref, dk_ref, dv_ref = refs[6 + nx:9 + nx]
        got_refs = refs[9 + nx:9 + 2 * nx]
        dk_acc, dv_acc, dq_acc = refs[9 + 2 * nx:12 + 2 * nx]
        sems = refs[12 + 2 * nx:]
        j, i = pl.program_id(1), pl.program_id(2)
        step_no = (pl.program_id(0) * n + j) * n + i
        host.open(step_no, steps, x_refs, got_refs, sems)

        @pl.when(i + j == 0)
        def _():
            dq_acc[...] = jnp.zeros(dq_acc.shape, F32)

        @pl.when(i == 0)
        def _():
            dk_acc[...] = jnp.zeros(dk_acc.shape, F32)
            dv_acc[...] = jnp.zeros(dv_acc.shape, F32)

        def step(masked):
            qv, kv, do = q_ref[...], k_ref[...], do_ref[...]
            sc = lax.dot_general(qv, kv, nt, preferred_element_type=F32)
            p = jnp.exp(sc - lse_ref[:, :1])
            if masked:
                p = jnp.where(_chunk_mask(i, j, t), p, 0.0)
            dob = do.astype(BF16)
            dp = lax.dot_general(dob, v_ref[...], nt, preferred_element_type=F32)
            ds = (p * (dp - jnp.sum(do * o_ref[...], axis=1, keepdims=True))).astype(BF16)
            dv_acc[...] += lax.dot_general(p.astype(BF16), dob, tn, preferred_element_type=F32)
            dk_acc[...] += lax.dot_general(ds, qv, tn, preferred_element_type=F32)
            rows = pl.ds(pl.multiple_of(i * t, t), t)
            dq_acc[rows, :] += jnp.dot(ds, kv, preferred_element_type=F32)

        pl.when(i > j)(lambda: step(False))
        pl.when(i == j)(lambda: step(True))

        @pl.when(i == n - 1)
        def _():
            dk_ref[...] = dk_acc[...]
            dv_ref[...] = dv_acc[...]

        @pl.when(i + j == 2 * (n - 1))
        def _():
            dq_ref[...] = dq_acc[...]

        host.close(step_no, steps, x_refs, got_refs, sems)

    qrow = lambda h, j, i: (jnp.maximum(i, j), h)
    krow = lambda h, j, i: (j, h)
    res = pl.pallas_call(
        body, name=name, grid=(nh, n, n),
        in_specs=[pl.BlockSpec((t, QK_PAD), qrow), pl.BlockSpec((t, QK_PAD), krow), pl.BlockSpec((t, HEAD), krow),
                  pl.BlockSpec((t, HEAD), qrow), pl.BlockSpec((t, HEAD), qrow),
                  pl.BlockSpec((None, t, LANES), lambda h, j, i: (h, jnp.maximum(i, j), 0))] + host.specs(),
        out_specs=[pl.BlockSpec((s, QK_PAD), lambda h, j, i: (0, h)),
                   pl.BlockSpec((t, QK_PAD), krow), pl.BlockSpec((t, HEAD), krow)] + host.specs(),
        out_shape=[jax.ShapeDtypeStruct((s, nh * QK_PAD), F32), jax.ShapeDtypeStruct((s, nh * QK_PAD), F32),
                   jax.ShapeDtypeStruct((s, nh * HEAD), F32)] + host.out_shapes(),
        scratch_shapes=[pltpu.VMEM((t, QK_PAD), F32), pltpu.VMEM((t, HEAD), F32), pltpu.VMEM((s, QK_PAD), F32)]
        + host.scratch(),
        compiler_params=_params(("arbitrary", "arbitrary", "arbitrary")),
    )(q, k, v, o, do, lse, *host.operands)
    return res[0], res[1], res[2], list(res[3:])


def _place():
    return lax.axis_index("x"), lax.axis_index("y"), lax.axis_index("c")


def _allgather8(x, name):
    r, c = x.shape

    def body(x_ref, out_ref, send_sems, recv_sems, local_sem):
        mx, my, mc = _place()
        me = 4 * mx + 2 * my + mc
        mine = pltpu.make_async_copy(x_ref, out_ref.at[me], local_sem)
        mine.start()
        copies = []
        for d in range(1, 8):
            px = 1 - mx if d & 4 else mx
            py = 1 - my if d & 2 else my
            pc = 1 - mc if d & 1 else mc
            cp = pltpu.make_async_remote_copy(
                src_ref=x_ref, dst_ref=out_ref.at[me], send_sem=send_sems.at[d - 1], recv_sem=recv_sems.at[d - 1],
                device_id=(px, py, pc), device_id_type=MESH)
            cp.start()
            copies.append(cp)
        for cp in copies:
            cp.wait()
        mine.wait()

    return pl.pallas_call(
        body, name=name,
        out_shape=jax.ShapeDtypeStruct((8, r, c), x.dtype),
        in_specs=[pl.BlockSpec(memory_space=pltpu.VMEM)],
        out_specs=pl.BlockSpec(memory_space=pltpu.VMEM),
        scratch_shapes=[pltpu.SemaphoreType.DMA((7,)), pltpu.SemaphoreType.DMA((7,)), pltpu.SemaphoreType.DMA],
        compiler_params=pltpu.CompilerParams(vmem_limit_bytes=VMEM_LIMIT),
    )(x)


def _allgather_chips(x, name):
    r, c = x.shape

    def body(x_ref, out_ref, send_sems, recv_sems, local_sems):
        for phase in range(3):
            _gather_phase(phase, x_ref, out_ref, send_sems, recv_sems, local_sems, 0)

    return pl.pallas_call(
        body, name=name,
        out_shape=jax.ShapeDtypeStruct((4, r, c), x.dtype),
        in_specs=[pl.BlockSpec(memory_space=pltpu.VMEM)],
        out_specs=pl.BlockSpec(memory_space=pltpu.VMEM),
        scratch_shapes=_gather_sems(1),
        compiler_params=pltpu.CompilerParams(vmem_limit_bytes=VMEM_LIMIT),
    )(x)


GATHER_COPIES = 6


def _gather_sems(n):
    return [pltpu.SemaphoreType.DMA((GATHER_COPIES * n,)), pltpu.SemaphoreType.DMA((GATHER_COPIES * n,)),
            pltpu.SemaphoreType.DMA((n,))]


def _gather_phase(phase, x_ref, out_ref, send_sems, recv_sems, local_sems, slot):
    mx, my, mc = _place()
    j = 2 * mx + my
    rh = x_ref.shape[0] // 2
    base = GATHER_COPIES * slot
    chips = [(1 - mx, my), (mx, 1 - my), (1 - mx, 1 - my)]
    sibling = (mx, my, 1 - mc)

    def half(jj, hc):
        return out_ref.at[jj, pl.ds(hc * rh, rh), :]

    def over_ici(kk, block):
        px, py = chips[kk]
        return pltpu.make_async_remote_copy(
            src_ref=x_ref.at[pl.ds(mc * rh, rh), :], dst_ref=half(block, mc), send_sem=send_sems.at[base + kk],
            recv_sem=recv_sems.at[base + kk], device_id=(px, py, mc), device_id_type=MESH)

    def to_sibling(kk, hc):
        px, py = chips[kk]
        blk = half(2 * px + py, hc)
        return pltpu.make_async_remote_copy(
            src_ref=blk, dst_ref=blk, send_sem=send_sems.at[base + 3 + kk], recv_sem=recv_sems.at[base + 3 + kk],
            device_id=sibling, device_id_type=MESH)

    mine = pltpu.make_async_copy(x_ref, out_ref.at[j], local_sems.at[slot])
    if phase == 0:
        mine.start()
        for kk in range(3):
            over_ici(kk, j).start()
    elif phase == 1:
        for kk, (px, py) in enumerate(chips):
            over_ici(kk, 2 * px + py).wait_recv()
            to_sibling(kk, mc).start()
    else:
        for kk in range(3):
            to_sibling(kk, 1 - mc).wait_recv()
        for kk in range(3):
            over_ici(kk, j).wait_send()
            to_sibling(kk, mc).wait_send()
        mine.wait()


RS_ROWS = 32


def _reduce_scatter_chips(g, name):
    _, r, c = g.shape
    rh = r // 2
    steps = rh // RS_ROWS

    def body(g_ref, out_ref, sib_ref, part_ref, got_ref, send_sems, recv_sems):
        mx, my, mc = _place()
        j = 2 * mx + my
        sibling = (mx, my, 1 - mc)
        chips = [(1 - mx, my), (mx, 1 - my), (1 - mx, 1 - my)]

        to_sib = pltpu.make_async_remote_copy(
            src_ref=g_ref.at[:, pl.ds((1 - mc) * rh, rh), :], dst_ref=sib_ref,
            send_sem=send_sems.at[0], recv_sem=recv_sems.at[0], device_id=sibling, device_id_type=MESH)
        to_sib.start()
        to_sib.wait()

        def add_sibling(step, carry):
            rows = pl.ds(pl.multiple_of(step * RS_ROWS, RS_ROWS), RS_ROWS)
            mine = g_ref[:, pl.ds(pl.multiple_of(mc * rh + step * RS_ROWS, RS_ROWS), RS_ROWS), :]
            part_ref[:, rows, :] = mine.astype(F32) + sib_ref[:, rows, :].astype(F32)
            return carry

        lax.fori_loop(0, steps, add_sibling, 0)

        def to_bf16(step, carry):
            rows = pl.ds(pl.multiple_of(step * RS_ROWS, RS_ROWS), RS_ROWS)
            sib_ref[:, rows, :] = part_ref[:, rows, :].astype(BF16)
            return carry

        lax.fori_loop(0, steps, to_bf16, 0)

        sends = []
        for kk, (px, py) in enumerate(chips):
            cp = pltpu.make_async_remote_copy(
                src_ref=sib_ref.at[2 * px + py], dst_ref=got_ref.at[kk],
                send_sem=send_sems.at[1 + kk], recv_sem=recv_sems.at[1 + kk],
                device_id=(px, py, mc), device_id_type=MESH)
            cp.start()
            sends.append(cp)
        for cp in sends:
            cp.wait()

        def total(step, carry):
            rows = pl.ds(pl.multiple_of(step * RS_ROWS, RS_ROWS), RS_ROWS)
            acc = part_ref[j, rows, :]
            for kk in range(3):
                acc = acc + got_ref[kk, rows, :].astype(F32)
            out_ref[pl.ds(pl.multiple_of(mc * rh + step * RS_ROWS, RS_ROWS), RS_ROWS), :] = acc
            return carry

        lax.fori_loop(0, steps, total, 0)

        done = pltpu.make_async_remote_copy(
            src_ref=out_ref.at[pl.ds(mc * rh, rh), :], dst_ref=out_ref.at[pl.ds(mc * rh, rh), :],
            send_sem=send_sems.at[4], recv_sem=recv_sems.at[4], device_id=sibling, device_id_type=MESH)
        done.start()
        done.wait_send()
        pltpu.make_async_remote_copy(
            src_ref=out_ref.at[pl.ds((1 - mc) * rh, rh), :], dst_ref=out_ref.at[pl.ds((1 - mc) * rh, rh), :],
            send_sem=send_sems.at[4], recv_sem=recv_sems.at[4], device_id=sibling, device_id_type=MESH).wait_recv()

    return pl.pallas_call(
        body, name=name,
        out_shape=jax.ShapeDtypeStruct((r, c), F32),
        in_specs=[pl.BlockSpec(memory_space=pltpu.VMEM)],
        out_specs=pl.BlockSpec(memory_space=pltpu.VMEM),
        scratch_shapes=[pltpu.VMEM((4, rh, c), BF16), pltpu.VMEM((4, rh, c), F32), pltpu.VMEM((3, rh, c), BF16),
                        pltpu.SemaphoreType.DMA((5,)), pltpu.SemaphoreType.DMA((5,))],
        compiler_params=pltpu.CompilerParams(vmem_limit_bytes=VMEM_LIMIT),
    )(g)


def _sum8(x, name):
    _, r, c = x.shape

    def body(x_ref, o_ref):
        acc = x_ref[0]
        for d in range(1, 8):
            acc = acc + x_ref[d]
        o_ref[...] = acc

    return pl.pallas_call(
        body, name=name, out_shape=jax.ShapeDtypeStruct((r, c), F32),
        in_specs=[pl.BlockSpec(memory_space=pltpu.VMEM)], out_specs=pl.BlockSpec(memory_space=pltpu.VMEM),
    )(x)


SCATTER_COPIES = 7


def _scatter_sems(n):
    return [pltpu.SemaphoreType.DMA((SCATTER_COPIES * n,)), pltpu.SemaphoreType.DMA((SCATTER_COPIES * n,))]


def _scatter_phase(phase, g_ref, got_ref, send_sems, recv_sems, slot):
    mx, my, mc = _place()
    rh = g_ref.shape[1] // 2
    base = SCATTER_COPIES * slot
    for d in range(1, 8):
        px = 1 - mx if d & 4 else mx
        py = 1 - my if d & 2 else my
        pc = 1 - mc if d & 1 else mc
        cp = pltpu.make_async_remote_copy(
            src_ref=g_ref.at[2 * px + py, pl.ds(pc * rh, rh), :], dst_ref=got_ref.at[d - 1],
            send_sem=send_sems.at[base + d - 1], recv_sem=recv_sems.at[base + d - 1],
            device_id=(px, py, pc), device_id_type=MESH)
        if phase == 0:
            cp.start()
        else:
            cp.wait()


def _scatter_sum(g, got, name):
    mx, my, mc = _place()
    rh, c = got.shape[1], got.shape[2]
    mine = lax.dynamic_slice(g, (2 * mx + my, mc * rh, 0), (1, rh, c))[0]
    t = _pick(rh, 256, 16)

    def body(m_ref, got_ref, o_ref):
        acc = m_ref[...].astype(F32)
        for d in range(SCATTER_COPIES):
            acc = acc + got_ref[d].astype(F32)
        o_ref[...] = acc

    return pl.pallas_call(
        body, name=name, grid=(rh // t,),
        in_specs=[pl.BlockSpec((t, c), lambda i: (i, 0)), pl.BlockSpec((SCATTER_COPIES, t, c), lambda i: (0, i, 0))],
        out_specs=pl.BlockSpec((t, c), lambda i: (i, 0)),
        out_shape=jax.ShapeDtypeStruct((rh, c), F32), compiler_params=_params(("parallel",)),
    )(mine, got)


def _scatter_finish(halves, name):
    n = len(halves)

    def body(*refs):
        h_refs, o_refs = refs[:n], refs[n:2 * n]
        send_sems, recv_sems, local_sems = refs[2 * n:]
        mx, my, mc = _place()
        copies = []
        for kk in range(n):
            rh = h_refs[kk].shape[0]
            rows = o_refs[kk].at[pl.ds(mc * rh, rh), :]
            loc = pltpu.make_async_copy(h_refs[kk], rows, local_sems.at[kk])
            rem = pltpu.make_async_remote_copy(
                src_ref=h_refs[kk], dst_ref=rows, send_sem=send_sems.at[kk], recv_sem=recv_sems.at[kk],
                device_id=(mx, my, 1 - mc), device_id_type=MESH)
            loc.start()
            rem.start()
            copies.append((loc, rem))
        for loc, rem in copies:
            rem.wait()
            loc.wait()

    hbm = pl.BlockSpec(memory_space=pl.ANY)
    return pl.pallas_call(
        body, name=name, in_specs=[hbm] * n, out_specs=[hbm] * n,
        out_shape=[jax.ShapeDtypeStruct((2 * h.shape[0], h.shape[1]), F32) for h in halves],
        scratch_shapes=[pltpu.SemaphoreType.DMA((n,)), pltpu.SemaphoreType.DMA((n,)), pltpu.SemaphoreType.DMA((n,))],
    )(*halves)


def _adamw(w, g, m, v, name):
    r, c = w.shape
    t = _pick(r, 256, SUBLANES)
    spec = pl.BlockSpec((t, c), lambda i: (i, 0))

    def body(w_ref, g_ref, m_ref, v_ref, d_ref, nm_ref, nv_ref):
        gv = g_ref[...]
        m_new = ADAM_B1 * m_ref[...] + (1.0 - ADAM_B1) * gv
        v_new = ADAM_B2 * v_ref[...] + (1.0 - ADAM_B2) * (gv * gv)
        m_hat = m_new / (1.0 - ADAM_B1 ** ADAM_STEP)
        v_hat = v_new / (1.0 - ADAM_B2 ** ADAM_STEP)
        d_ref[...] = -ADAM_LR * (m_hat / (jnp.sqrt(v_hat) + ADAM_EPS) + ADAM_WD * w_ref[...])
        nm_ref[...] = m_new
        nv_ref[...] = v_new

    return pl.pallas_call(
        body, name=name, grid=(r // t,), in_specs=[spec] * 4, out_specs=[spec] * 3,
        out_shape=[jax.ShapeDtypeStruct((r, c), F32)] * 3, compiler_params=_params(("parallel",)),
    )(w, g, m, v)


def _pack_rows(parts):
    rows, offs, o = [], [], 0
    for p in parts:
        f = p.reshape(-1)
        n = -(-f.shape[0] // (LANES * SUBLANES)) * SUBLANES
        rows.append(jnp.pad(f, (0, n * LANES - f.shape[0])).reshape(n, LANES))
        offs.append((o, n))
        o += n
    return jnp.concatenate(rows, 0), offs


def _unpack_rows(packed, offs, shapes):
    out = []
    for (o, n), shp in zip(offs, shapes):
        size = 1
        for d in shp:
            size *= d
        out.append(packed[o:o + n].reshape(-1)[:size].reshape(shp))
    return out


def _mm_hosting(a, b, mode, out_dtype, name, gather=(), chips=None, scatter=()):
    res = _mm(a, b, mode, out_dtype, name, gather=gather, chips=chips, scatter=scatter)
    return (res[0], list(res[1:])) if (gather or scatter) else (res, [])


def _ffn_fwd(x, s, sh, g, w_in, w_out, tag, gather_in=(), gather_out=()):
    (h,) = _rowcall(lambda r, p: ([_modulate(r[0], p[0], p[1])], []), [x], [s, sh], [(x.shape[1], BF16)], [],
                    tile=512, name=tag + "_mod")
    gu, got = _mm_hosting(h, w_in, "nn", BF16, tag + "_in", gather_in, chips="b")
    if w_out is None:
        first = got.pop(0)
        w_out = first.reshape(4 * first.shape[1], first.shape[2])
    (act,) = _rowcall(lambda r, p: ([_silu(r[0].astype(F32)) * r[1].astype(F32)], []),
                      [(gu, D_FF, 0), (gu, D_FF, 1)], [], [(D_FF, BF16)], [], tile=256, name=tag + "_act")
    f, got_out = _mm_hosting(act, w_out, "nn", F32, tag + "_out", gather_out)
    got = got + got_out
    (y,) = _rowcall(lambda r, p: ([r[0] + 0.5 * p[0] * r[1]], []), [x, f], [g], [(x.shape[1], F32)], [],
                    tile=512, name=tag + "_res")
    return y, (x, h, gu, act, f), got, w_out


def _ffn_bwd(dy, saved, s, sh, g, w_in, w_out, tag, scatter_bin=(), scatter_bwin=()):
    x, h, gu, act, f = saved
    d = x.shape[1]
    df, dg = _rowcall(lambda r, p: ([0.5 * p[0] * r[0]], [0.5 * jnp.sum(r[0] * r[1], 0, keepdims=True)]),
                      [dy, f], [g], [(d, BF16)], [(1, d)], tile=512, name=tag + "_bres")
    da = _mm(df, w_out, "nt", BF16, tag + "_bout")
    dw_out = _mm(act, df, "tn", BF16, tag + "_bwout")

    def act_bwd(r, p):
        gate, up, dav = r[0].astype(F32), r[1].astype(F32), r[2].astype(F32)
        _, vjp = jax.vjp(lambda a, b: _silu(a) * b, gate, up)
        dgate, dup = vjp(dav)
        return [jnp.concatenate([dgate, dup], 1)], []

    (dgu,) = _rowcall(act_bwd, [(gu, D_FF, 0), (gu, D_FF, 1), da], [], [(2 * D_FF, BF16)], [], tile=256,
                      name=tag + "_bact")
    dh, got_a = _mm_hosting(dgu, w_in, "nt", F32, tag + "_bin", chips="b", scatter=scatter_bin)
    dw_in, got_b = _mm_hosting(h, dgu, "tn", BF16, tag + "_bwin", chips="out", scatter=scatter_bwin)

    def mod_bwd(r, p):
        _, vjp = jax.vjp(_modulate, r[0], p[0], p[1])
        dx, ds, dsh = vjp(r[1])
        return [r[2] + dx], [ds, dsh]

    dx, ds, dsh = _rowcall(mod_bwd, [x, dh, dy], [s, sh], [(d, F32)], [(1, d), (1, d)], tile=512, name=tag + "_bmod")
    return dx, (dsh, ds, dg), dw_in, dw_out, list(got_a) + list(got_b)


def _mixer_fwd(x, s, sh, g, wts, rope, gather=()):
    w_in_p, conv8, a_log, dt_bias, wn, wq, w_uq_p, wkv, w_ukv, wqn, wqr, wkn, wkr, won, w_out = wts
    cos2, sin2 = rope
    d = x.shape[1]
    (h,) = _rowcall(lambda r, p: ([_modulate(r[0], p[0], p[1])], []), [x], [s, sh], [(d, BF16)], [],
                    tile=512, name="mix_mod")
    proj = _mm(h, w_in_p, "nn", F32, "mix_in")
    qkv_c = _conv_fwd(proj, conv8, "mix_conv")
    gab = (proj, LANES, 23)

    q, k, v, gb = _rowcall(
        lambda r, p: (list(_gdn_prep_core(_split(r[0], [HEAD] * 12), r[1], p[0], p[1])), []),
        [qkv_c, gab], [a_log, dt_bias], [(512, F32)] * 3 + [(LANES, F32)], [], tile=256, name="mix_gdn_prep")
    gdn_local = _gdn_local_fwd(q, k, v, gb, "mix_gdn_local")
    o_gdn, gdn_states = _gdn_scan_fwd(*gdn_local, "mix_gdn_scan")
    states = (gdn_local, gdn_states)

    cq, ckv, kr = (proj, 512, 4), (proj, 256, 10), (proj, LANES, 22)
    cqn, ckvn, k_rope = _rowcall(
        lambda r, p: (list(_mla_prep_core(r[0][:, :MLA_Q_LORA], r[1], r[2], r[3], r[4], p[0], p[1], p[2])), []),
        [cq, ckv, kr, cos2, sin2], [wq, wkv, wkr], [(MLA_Q_LORA, BF16), (MLA_KV_LORA, BF16), (LANES, F32)], [],
        tile=512, name="mix_mla_prep")
    qf = _mm(cqn, w_uq_p, "nn", F32, "mix_uq")
    kvf = _mm(ckvn, w_ukv, "nn", F32, "mix_ukv")

    def qk_prep(r, p):
        qparts = _split(r[0], [HEAD] * 8)
        kvparts = _split(r[1], [HEAD] * 8)
        qs, ks, vs = _qk_prep_core(qparts[:4], qparts[4:], kvparts[0::2], kvparts[1::2], r[2], r[3], r[4],
                                   p[0], p[1], p[2])
        return [jnp.concatenate(qs, 1), jnp.concatenate(ks, 1), jnp.concatenate(vs, 1)], []

    qa, ka, va = _rowcall(qk_prep, [qf, kvf, k_rope, cos2, sin2], [wqn, wqr, wkn],
                          [(4 * QK_PAD, BF16), (4 * QK_PAD, BF16), (4 * HEAD, BF16)], [], tile=256,
                          name="mix_qk_prep")
    o_b, lse, got = _attn_fwd(qa, ka, va, "mix_attn", gather=gather)
    if w_out is None:
        first = got.pop(0)
        w_out = first.reshape(4 * first.shape[1], first.shape[2])

    gz = (proj, 512, 3)
    (mixed,) = _rowcall(
        lambda r, p: ([_mix_post_core(_split(r[0], HW4), _split(r[1], HW4), _split(r[2], HW4), p[0], p[1])], []),
        [o_gdn, gz, o_b], [wn, won], [(2 * 512, BF16)], [], tile=512, name="mix_post")
    y = _mm(mixed, w_out, "nn", F32, "mix_out")
    (x_out,) = _rowcall(lambda r, p: ([r[0] + p[0] * r[1]], []), [x, y], [g], [(d, F32)], [], tile=512,
                        name="mix_res")
    saved = (x, h, proj, qkv_c, q, k, v, gb, states, o_gdn, cqn, ckvn, k_rope, qf, kvf, qa, ka, va, o_b, lse,
             mixed, y)
    return x_out, saved, got, w_out


def _mixer_bwd(dy, saved, s, sh, g, wts, rope, scatter=()):
    w_in_p, conv8, a_log, dt_bias, wn, wq, w_uq_p, wkv, w_ukv, wqn, wqr, wkn, wkr, won, w_out = wts
    cos2, sin2 = rope
    (x, h, proj, qkv_c, q, k, v, gb, states, o_gdn, cqn, ckvn, k_rope, qf, kvf, qa, ka, va, o_b, lse,
     mixed, y) = saved
    d = x.shape[1]
    dyb, dg = _rowcall(lambda r, p: ([p[0] * r[0]], [jnp.sum(r[0] * r[1], 0, keepdims=True)]),
                       [dy, y], [g], [(d, BF16)], [(1, d)], tile=512, name="mix_bres")
    dmixed = _mm(dyb, w_out, "nt", F32, "mix_bout")
    dw_out = _mm(mixed, dyb, "tn", BF16, "mix_bwout")

    gz = (proj, 512, 3)

    def post_bwd(r, p):
        _, vjp = jax.vjp(_mix_post_core, _split(r[0], HW4), _split(r[1], HW4), _split(r[2], HW4), p[0], p[1])
        do, dz, dob, dwn, dwon = vjp(r[3])
        return [jnp.concatenate(do, 1), jnp.concatenate(dz, 1), jnp.concatenate(dob, 1)], [dwn, dwon]

    do_gdn, dgz, do_b, dwn, dwon = _rowcall(post_bwd, [o_gdn, gz, o_b, dmixed], [wn, won], [(512, F32)] * 3,
                                            [(1, HEAD), (1, HEAD)], tile=256, name="mix_bpost")

    dqa, dka, dva, got = _attn_bwd(qa, ka, va, o_b, do_b, lse, "mix_battn", scatter=scatter)

    def qk_bwd(r, p):
        qparts = _split(r[0], [HEAD] * 8)
        kvparts = _split(r[1], [HEAD] * 8)
        _, vjp = jax.vjp(_qk_prep_core, qparts[:4], qparts[4:], kvparts[0::2], kvparts[1::2], r[2], r[3], r[4],
                         p[0], p[1], p[2])
        cot = (_split(r[5], [QK_PAD] * 4), _split(r[6], [QK_PAD] * 4), _split(r[7], HW4))
        dqn, dqr, dkn, dvp, dkrope, _, _, dwqn, dwqr, dwkn = vjp(cot)
        dkv = []
        for a, b in zip(dkn, dvp):
            dkv += [a, b]
        return [jnp.concatenate(list(dqn) + list(dqr), 1), jnp.concatenate(dkv, 1), dkrope], [dwqn, dwqr, dwkn]

    dqf, dkvf, dk_rope, dwqn, dwqr, dwkn = _rowcall(
        qk_bwd, [qf, kvf, k_rope, cos2, sin2, dqa, dka, dva], [wqn, wqr, wkn],
        [(8 * HEAD, BF16), (8 * HEAD, BF16), (LANES, F32)], [(1, HEAD)] * 3, tile=256, name="mix_bqk_prep")
    dcqn = _mm(dqf, w_uq_p, "nt", F32, "mix_buq")
    dw_uq_p = _mm(cqn, dqf, "tn", F32, "mix_bwuq")
    dckvn = _mm(dkvf, w_ukv, "nt", F32, "mix_bukv")
    dw_ukv = _mm(ckvn, dkvf, "tn", F32, "mix_bwukv")

    cq, ckv, kr = (proj, 512, 4), (proj, 256, 10), (proj, LANES, 22)

    def mla_bwd(r, p):
        _, vjp = jax.vjp(_mla_prep_core, r[0][:, :MLA_Q_LORA], r[1], r[2], r[3], r[4], p[0], p[1], p[2])
        dcq, dckv, dkr, _, _, dwq, dwkv, dwkr = vjp((r[5], r[6], r[7]))
        pad = jnp.zeros((dcq.shape[0], 512 - MLA_Q_LORA), F32)
        return [jnp.concatenate([dcq, pad], 1), dckv, dkr], [dwq, dwkv, dwkr]

    dcq, dckv, dkr, dwq, dwkv, dwkr = _rowcall(
        mla_bwd, [cq, ckv, kr, cos2, sin2, dcqn, dckvn, dk_rope], [wq, wkv, wkr],
        [(512, F32), (MLA_KV_LORA, F32), (LANES, F32)], [(1, MLA_Q_LORA), (1, MLA_KV_LORA), (1, LANES)],
        tile=512, name="mix_bmla_prep")

    gdn_local, gdn_states = states
    d_local = _gdn_scan_bwd(*gdn_local, gdn_states, do_gdn, "mix_bgdn_scan")
    dq, dk, dv, dgb = _gdn_local_bwd(q, k, v, gb, *d_local, "mix_bgdn_local")
    gab = (proj, LANES, 23)

    def gdn_prep_bwd(r, p):
        _, vjp = jax.vjp(_gdn_prep_core, _split(r[0], [HEAD] * 12), r[1], p[0], p[1])
        dparts, dgab, da_log, ddt = vjp((r[2], r[3], r[4], r[5]))
        return [jnp.concatenate(dparts, 1), dgab], [da_log, ddt]

    dqkv_c, dgab, da_log, ddt = _rowcall(gdn_prep_bwd, [qkv_c, gab, dq, dk, dv, dgb], [a_log, dt_bias],
                                         [(1536, F32), (LANES, F32)], [(1, LANES), (1, LANES)], tile=256,
                                         name="mix_bgdn_prep")
    dqkv_pre, dconv8 = _conv_bwd(proj, dqkv_c, conv8, "mix_bconv")

    dproj = jnp.concatenate([dqkv_pre.astype(BF16), dgz.astype(BF16), dcq.astype(BF16), dckv.astype(BF16),
                             dkr.astype(BF16), dgab.astype(BF16)], axis=1)
    dh = _mm(dproj, w_in_p, "nt", F32, "mix_bin")
    dw_in_p = _mm(h, dproj, "tn", F32, "mix_bwin")

    def mod_bwd(r, p):
        _, vjp = jax.vjp(_modulate, r[0], p[0], p[1])
        dx, ds, dsh = vjp(r[1])
        return [r[2] + dx], [ds, dsh]

    dx, ds, dsh = _rowcall(mod_bwd, [x, dh, dy], [s, sh], [(d, F32)], [(1, d), (1, d)], tile=512, name="mix_bmod")
    small = dict(conv=dconv8, a_log=da_log, dt=ddt, wn=dwn, wq=dwq, wkv=dwkv, wqn=dwqn, wqr=dwqr, wkn=dwkn,
                 wkr=dwkr, won=dwon)
    return dx, (dsh, ds, dg), dw_in_p, dw_uq_p, dw_ukv, dw_out, small, got


def _pad_cols(a, n):
    return jnp.pad(a, ((0, 0),) * (a.ndim - 1) + ((0, n - a.shape[-1]),))


def _pack_w_in(w):
    z = lambda n: jnp.zeros((w.shape[0], n), w.dtype)
    return jnp.concatenate([w[:, 0:2048], w[:, 2056:2440], z(128), w[:, 2440:2696], w[:, 2696:2760], z(64),
                            w[:, 2048:2056], z(120)], axis=1)


def _unpack_w_in(wp):
    return jnp.concatenate([wp[:, 0:2048], wp[:, 2944:2952], wp[:, 2048:2432], wp[:, 2560:2816], wp[:, 2816:2880]],
                           axis=1)


def _pack_w_uq(w):
    z = jnp.zeros((w.shape[0], LANES - MLA_ROPE), w.dtype)
    nope = [w[:, h * 192:h * 192 + HEAD] for h in range(MLA_HEADS)]
    rope = []
    for h in range(MLA_HEADS):
        rope += [w[:, h * 192 + HEAD:(h + 1) * 192], z]
    return jnp.concatenate(nope + rope, axis=1)


def _unpack_w_uq(wp):
    cols = []
    for h in range(MLA_HEADS):
        cols += [wp[:, h * HEAD:(h + 1) * HEAD], wp[:, 512 + h * LANES:512 + h * LANES + MLA_ROPE]]
    return jnp.concatenate(cols, axis=1)


def _cols_to_chips(a):
    r, c = a.shape
    return a.reshape(r, 4, c // 4).transpose(1, 0, 2)


def _chips_to_cols(a):
    _, r, n = a.shape
    return a.transpose(1, 0, 2).reshape(r, 4 * n)


def _pad128(v, n=LANES):
    return _pad_cols(v.reshape(1, -1), n)


def kernel(x, c, positions, w_ada, b_ada, ffn1_w_in, ffn1_w_out, w_in, gdn_conv_w, gdn_a_log, gdn_dt_bias, gdn_norm_w, mla_q_norm_w, mla_w_uq, mla_kv_norm_w, mla_w_ukv, qkn_q_nope, qkn_q_rope, qkn_k_nope, qkn_k_rope, mla_out_norm_w, w_out, ffn2_w_in, ffn2_w_out, loss_target, m_w_ada, m_b_ada, m_ffn1_w_in, m_ffn1_w_out, m_w_in, m_gdn_conv_w, m_gdn_a_log, m_gdn_dt_bias, m_gdn_norm_w, m_mla_q_norm_w, m_mla_w_uq, m_mla_kv_norm_w, m_mla_w_ukv, m_qkn_q_nope, m_qkn_q_rope, m_qkn_k_nope, m_qkn_k_rope, m_mla_out_norm_w, m_w_out, m_ffn2_w_in, m_ffn2_w_out, v_w_ada, v_b_ada, v_ffn1_w_in, v_ffn1_w_out, v_w_in, v_gdn_conv_w, v_gdn_a_log, v_gdn_dt_bias, v_gdn_norm_w, v_mla_q_norm_w, v_mla_w_uq, v_mla_kv_norm_w, v_mla_w_ukv, v_qkn_q_nope, v_qkn_q_rope, v_qkn_k_nope, v_qkn_k_rope, v_mla_out_norm_w, v_w_out, v_ffn2_w_in, v_ffn2_w_out):
    weights = dict(w_ada=w_ada, b_ada=b_ada, ffn1_w_in=ffn1_w_in, ffn1_w_out=ffn1_w_out, w_in=w_in,
                   gdn_conv_w=gdn_conv_w, gdn_a_log=gdn_a_log, gdn_dt_bias=gdn_dt_bias, gdn_norm_w=gdn_norm_w,
                   mla_q_norm_w=mla_q_norm_w, mla_w_uq=mla_w_uq, mla_kv_norm_w=mla_kv_norm_w, mla_w_ukv=mla_w_ukv,
                   qkn_q_nope=qkn_q_nope, qkn_q_rope=qkn_q_rope, qkn_k_nope=qkn_k_nope, qkn_k_rope=qkn_k_rope,
                   mla_out_norm_w=mla_out_norm_w, w_out=w_out, ffn2_w_in=ffn2_w_in, ffn2_w_out=ffn2_w_out)
    moms_m = dict(w_ada=m_w_ada, b_ada=m_b_ada, ffn1_w_in=m_ffn1_w_in, ffn1_w_out=m_ffn1_w_out, w_in=m_w_in,
                  gdn_conv_w=m_gdn_conv_w, gdn_a_log=m_gdn_a_log, gdn_dt_bias=m_gdn_dt_bias,
                  gdn_norm_w=m_gdn_norm_w, mla_q_norm_w=m_mla_q_norm_w, mla_w_uq=m_mla_w_uq,
                  mla_kv_norm_w=m_mla_kv_norm_w, mla_w_ukv=m_mla_w_ukv, qkn_q_nope=m_qkn_q_nope,
                  qkn_q_rope=m_qkn_q_rope, qkn_k_nope=m_qkn_k_nope, qkn_k_rope=m_qkn_k_rope,
                  mla_out_norm_w=m_mla_out_norm_w, w_out=m_w_out, ffn2_w_in=m_ffn2_w_in, ffn2_w_out=m_ffn2_w_out)
    moms_v = dict(w_ada=v_w_ada, b_ada=v_b_ada, ffn1_w_in=v_ffn1_w_in, ffn1_w_out=v_ffn1_w_out, w_in=v_w_in,
                  gdn_conv_w=v_gdn_conv_w, gdn_a_log=v_gdn_a_log, gdn_dt_bias=v_gdn_dt_bias,
                  gdn_norm_w=v_gdn_norm_w, mla_q_norm_w=v_mla_q_norm_w, mla_w_uq=v_mla_w_uq,
                  mla_kv_norm_w=v_mla_kv_norm_w, mla_w_ukv=v_mla_w_ukv, qkn_q_nope=v_qkn_q_nope,
                  qkn_q_rope=v_qkn_q_rope, qkn_k_nope=v_qkn_k_nope, qkn_k_rope=v_qkn_k_rope,
                  mla_out_norm_w=v_mla_out_norm_w, w_out=v_w_out, ffn2_w_in=v_ffn2_w_in, ffn2_w_out=v_ffn2_w_out)
    names = list(weights)

    seq, d = x.shape[1], x.shape[2]
    x2d = x.reshape(seq, d)
    tgt = loss_target.reshape(seq, d)
    mx, my, mc = _place()
    chip = 2 * mx + my
    me = 2 * chip + mc
    n_mod = b_ada.shape[1] // d
    shard = w_ada.shape[2]

    half = MLA_ROPE // 2
    inv_freq = 10000.0 ** (-jnp.arange(half, dtype=F32) / half)
    ang = positions.astype(F32).reshape(seq, 1) * inv_freq
    cosv, sinv = jnp.cos(ang), jnp.sin(ang)
    cos2 = _pad_cols(jnp.concatenate([cosv, cosv], 1), LANES)
    sin2 = _pad_cols(jnp.concatenate([-sinv, sinv], 1), LANES)
    rope = (cos2, sin2)

    c_all = _allgather8(jnp.pad(c, ((0, SUBLANES - 1), (0, 0))), "gather_c")[:, 0, :]
    (sc_all,) = _rowcall(lambda r, p: ([_silu(r[0])], []), [c_all], [], [(d, F32)], [], tile=8, name="ada_silu")
    mod_part = _mm(sc_all, w_ada[0], "nn", F32, "ada_mm", hi=True)
    mod_all = _allgather8(mod_part, "gather_mod")
    mod_rows = lax.dynamic_index_in_dim(mod_all, me, axis=1, keepdims=False)
    mod_raw = jnp.concatenate([mod_rows[2 * jj] for jj in range(4)], axis=0).reshape(1, 4 * shard)
    (mod,) = _rowcall(lambda r, p: ([r[0] + r[1]], []),
                      [jnp.pad(mod_raw, ((0, 7), (0, 0))), jnp.pad(b_ada, ((0, 7), (0, 0)))], [],
                      [(4 * shard, F32)], [], tile=8, name="ada_bias")
    mods = [mod[0:1, i * d:(i + 1) * d] for i in range(n_mod)]
    sh1, s1, g1, sh2, s2, g2, sh3, s3, g3 = mods

    def shard_bf16(w, pad_to=None):
        w2 = w[0].astype(BF16)
        return _pad_cols(w2, pad_to) if pad_to else w2

    def cols_of(got, w):
        return _chips_to_cols(got[:, :, :w.shape[2]])

    def rows_of(got):
        return got.reshape(4 * got.shape[1], got.shape[2])

    f1_in = _allgather_chips(shard_bf16(ffn1_w_in), "gather_f1_in")
    conv_all = _allgather8(jnp.pad(gdn_conv_w[0], ((0, SUBLANES - CONV_K), (0, 0))), "gather_conv")
    conv8 = jnp.concatenate([conv_all[2 * jj] for jj in range(4)], axis=1)

    x1, sv1, got, f1_out = _ffn_fwd(
        x2d, s1, sh1, g1, f1_in, None, "ffn1",
        gather_in=[shard_bf16(ffn1_w_out), shard_bf16(w_in, 768), shard_bf16(mla_w_uq, 256), shard_bf16(mla_w_ukv)])
    w_in_full, w_uq_full, w_ukv_full = cols_of(got[0], w_in), cols_of(got[1], mla_w_uq), cols_of(got[2], mla_w_ukv)
    wts = (_pack_w_in(w_in_full), conv8, _pad128(gdn_a_log), _pad128(gdn_dt_bias), gdn_norm_w,
           mla_q_norm_w, _pack_w_uq(w_uq_full), mla_kv_norm_w, w_ukv_full, qkn_q_nope, _pad128(qkn_q_rope),
           qkn_k_nope, _pad128(qkn_k_rope), mla_out_norm_w, None)
    xm, svm, got, w_out_full = _mixer_fwd(
        x1, s2, sh2, g2, wts, rope, gather=[shard_bf16(w_out), shard_bf16(ffn2_w_in), shard_bf16(ffn2_w_out)])
    wts = wts[:-1] + (w_out_full,)
    f2_in, f2_out = got[0], rows_of(got[1])
    x3, sv3, _, _ = _ffn_fwd(xm, s3, sh3, g3, f2_in, f2_out, "ffn2")

    def loss_fn(r, p):
        err = r[0] - r[1]
        part = 0.5 * jnp.sum(jnp.sum(err * err, axis=1, keepdims=True) * (1.0 / d), axis=0, keepdims=True)
        return [err * (1.0 / d)], [jnp.broadcast_to(part, (1, LANES))]

    dy, loss_part = _rowcall(loss_fn, [x3, tgt], [], [(d, F32)], [(1, LANES)], tile=512, name="loss")
    loss = lax.psum(loss_part[0, 0], ("x", "y", "c"))

    def chip_cols(dw, pad_to=None):
        g4 = _cols_to_chips(dw).astype(BF16)
        return _pad_cols(g4, pad_to) if pad_to else g4

    def chip_rows(dw):
        return dw.astype(BF16).reshape(4, dw.shape[0] // 4, dw.shape[1])

    dxm, dmod3, dw_f2_in, dw_f2_out, _ = _ffn_bwd(dy, sv3, s3, sh3, g3, f2_in, f2_out, "ffn2")
    parts2 = [dw_f2_in, chip_rows(dw_f2_out)]
    dx1, dmod2, dw_in_p, dw_uq_p, dw_ukv, dw_out_m, small, got2 = _mixer_bwd(dxm, svm, s2, sh2, g2, wts, rope,
                                                                             scatter=parts2)
    parts_m = [chip_cols(_unpack_w_in(dw_in_p), 768), chip_cols(_unpack_w_uq(dw_uq_p), 256), chip_cols(dw_ukv),
               chip_rows(dw_out_m)]
    dx0, dmod1, dw_f1_in, dw_f1_out, got_m = _ffn_bwd(dx1, sv1, s1, sh1, g1, f1_in, f1_out, "ffn1",
                                                      scatter_bin=parts_m[:1], scatter_bwin=parts_m[1:])
    grad_x = dx0.reshape(x.shape)

    dmod = jnp.concatenate(list(dmod1) + list(dmod2) + list(dmod3), axis=1)
    small_parts = [dmod, small["conv"][:CONV_K], small["a_log"], small["dt"], small["wn"], small["wq"],
                   small["wkv"], small["wqn"], small["wqr"], small["wkn"], small["wkr"], small["won"]]
    packed, offs = _pack_rows(small_parts)
    gathered = _allgather8(packed, "gather_small")
    total = _sum8(gathered, "sum_small")
    (g_b_ada, g_conv, g_a_log, g_dt, g_wn, g_wq, g_wkv, g_wqn, g_wqr, g_wkn, g_wkr, g_won) = _unpack_rows(
        total, offs, [p.shape for p in small_parts])
    dmod_all = _unpack_rows(gathered.reshape(-1, LANES),
                            [(dd * packed.shape[0] + offs[0][0], offs[0][1]) for dd in range(8)],
                            [dmod.shape] * 8)
    dmod_all = jnp.concatenate(dmod_all, axis=0)
    dmod_mine = lax.dynamic_slice_in_dim(dmod_all, chip * shard, shard, axis=1)

    def ada_grad(r, p):
        acc = jnp.zeros((r[0].shape[0], shard), F32)
        for b in range(8):
            acc = acc + r[0][:, b:b + 1] * p[0][b:b + 1, :]
        return [acc], []

    (g_w_ada,) = _rowcall(ada_grad, [_pad_cols(sc_all.T, LANES)], [dmod_mine], [(shard, F32)], [], tile=256,
                          name="ada_grad")

    grads = dict(
        w_ada=g_w_ada[None], b_ada=g_b_ada,
        gdn_conv_w=lax.dynamic_slice_in_dim(g_conv, chip * gdn_conv_w.shape[2], gdn_conv_w.shape[2], axis=1)[None],
        gdn_a_log=g_a_log[:, :GDN_HEADS], gdn_dt_bias=g_dt[:, :GDN_HEADS], gdn_norm_w=g_wn, mla_q_norm_w=g_wq,
        mla_kv_norm_w=g_wkv, qkn_q_nope=g_wqn, qkn_q_rope=g_wqr[:, :MLA_ROPE], qkn_k_nope=g_wkn,
        qkn_k_rope=g_wkr[:, :MLA_ROPE], mla_out_norm_w=g_won)

    hosted = ["ffn2_w_in", "ffn2_w_out", "w_in", "mla_w_uq", "mla_w_ukv", "w_out"]
    halves = [_scatter_sum(part, got, "rs_sum_" + nme)
              for nme, part, got in zip(hosted, parts2 + parts_m, got2 + got_m)]
    for nme, full in zip(hosted, _scatter_finish(halves, "rs_finish")):
        grads[nme] = full[:, :weights[nme].shape[2]][None]
    grads["ffn1_w_in"] = _reduce_scatter_chips(dw_f1_in, "rs_f1_in")[None]
    grads["ffn1_w_out"] = _reduce_scatter_chips(chip_rows(dw_f1_out), "rs_f1_out")[None]

    big = ["w_ada", "ffn1_w_in", "ffn1_w_out", "w_in", "mla_w_uq", "mla_w_ukv", "w_out", "ffn2_w_in", "ffn2_w_out"]
    delta, new_m, new_v = {}, {}, {}
    for nme in big:
        shp = weights[nme].shape
        dl, nm, nv = _adamw(weights[nme][0], grads[nme][0], moms_m[nme][0], moms_v[nme][0], "adamw_" + nme)
        delta[nme], new_m[nme], new_v[nme] = dl.reshape(shp), nm.reshape(shp), nv.reshape(shp)
    tiny = [nme for nme in names if nme not in big]
    shapes = [weights[nme].shape for nme in tiny]
    pw, poffs = _pack_rows([weights[nme] for nme in tiny])
    pg, _ = _pack_rows([grads[nme] for nme in tiny])
    pm, _ = _pack_rows([moms_m[nme] for nme in tiny])
    pv, _ = _pack_rows([moms_v[nme] for nme in tiny])
    pd, pnm, pnv = _adamw(pw, pg, pm, pv, "adamw_small")
    for nme, dl, nm, nv in zip(tiny, _unpack_rows(pd, poffs, shapes), _unpack_rows(pnm, poffs, shapes),
                               _unpack_rows(pnv, poffs, shapes)):
        delta[nme], new_m[nme], new_v[nme] = dl, nm, nv

    return (loss, grad_x, *[grads[nme].reshape(weights[nme].shape) for nme in names],
            *[delta[nme] for nme in names], *[new_m[nme] for nme in names], *[new_v[nme] for nme in names])
```

```python
import functools

import jax
import jax.numpy as jnp
from jax import lax
from jax.experimental import pallas as pl
from jax.experimental.pallas import tpu as pltpu

F32 = jnp.float32
BF16 = jnp.bfloat16
HI = lax.Precision.HIGHEST
MESH = pl.DeviceIdType.MESH

EPS = 1e-6
CHUNK = 64
D_FF = 2816
GDN_HEADS = 4
HEAD = 128
MLA_HEADS = 4
MLA_ROPE = 64
MLA_Q_LORA = 384
MLA_KV_LORA = 256
QK_PAD = 256
ATT_SCALE = (HEAD + MLA_ROPE) ** -0.5
N_PROJ = 3072

ADAM_LR, ADAM_B1, ADAM_B2, ADAM_EPS, ADAM_WD, ADAM_STEP = 0.001, 0.9, 0.999, 1e-08, 0.01, 10

LANES = 128
SUBLANES = 8
VMEM_LIMIT = 56 * 2 ** 20


def _params(sem=None):
    return pltpu.CompilerParams(dimension_semantics=sem, vmem_limit_bytes=VMEM_LIMIT)


def _pick(n, cap, align):
    best = None
    d = align
    while d <= min(n, cap):
        if n % d == 0:
            best = d
        d += align
    return best if best is not None else n


def _iota(shape, dim):
    return lax.broadcasted_iota(jnp.int32, shape, dim)


def _rowcall(fn, rows, params, out_rows, out_accs, *, tile, name):
    rows = [r if isinstance(r, tuple) else (r, r.shape[1], 0) for r in rows]
    s = rows[0][0].shape[-2]
    t = min(tile, s)
    n = s // t
    n_in = len(rows) + len(params)
    n_row_out = len(out_rows)

    in_specs = []
    for r in rows:
        if len(r) == 3:
            in_specs.append(pl.BlockSpec((t, r[1]), functools.partial(lambda i, b: (i, b), b=r[2])))
        else:
            in_specs.append(pl.BlockSpec((None, t, r[1]), functools.partial(lambda i, b, h: (h, i, b), b=r[2], h=r[3])))
    in_specs += [pl.BlockSpec(p.shape, lambda i: (0, 0)) for p in params]
    out_shape, out_specs = [], []
    for o in out_rows:
        if len(o) == 2:
            out_shape.append(jax.ShapeDtypeStruct((s, o[0]), o[1]))
            out_specs.append(pl.BlockSpec((t, o[0]), lambda i: (i, 0)))
        else:
            out_shape.append(jax.ShapeDtypeStruct((o[2], s, o[0]), o[1]))
            out_specs.append(pl.BlockSpec((o[2], t, o[0]), lambda i: (0, i, 0)))
    out_shape += [jax.ShapeDtypeStruct(shape, F32) for shape in out_accs]
    out_specs += [pl.BlockSpec(shape, lambda i: (0, 0)) for shape in out_accs]

    def body(*refs):
        ins = refs[:n_in]
        outs = refs[n_in:]
        i = pl.program_id(0)
        vals = [r[...] for r in ins]
        row_outs, acc_outs = fn(vals[:len(rows)], vals[len(rows):])
        for r, v in zip(outs[:n_row_out], row_outs):
            if isinstance(v, (list, tuple)):
                for hh, piece in enumerate(v):
                    r[hh] = piece.astype(r.dtype)
            else:
                r[...] = v.astype(r.dtype)
        if out_accs:
            @pl.when(i == 0)
            def _():
                for r in outs[n_row_out:]:
                    r[...] = jnp.zeros(r.shape, F32)
            for r, v in zip(outs[n_row_out:], acc_outs):
                r[...] += v

    res = pl.pallas_call(
        body, name=name, grid=(n,), in_specs=in_specs, out_specs=out_specs, out_shape=out_shape,
        compiler_params=_params(("arbitrary",) if out_accs else ("parallel",)),
    )(*[r[0] for r in rows], *params)
    return list(res)


MM_TILE_MN = 1536


class _Hosted:
    def __init__(self, gather=(), scatter=()):
        self.gather, self.scatter = list(gather), list(scatter)
        self.operands = self.gather + self.scatter
        self.n = len(self.operands)

    def specs(self):
        return [pl.BlockSpec(memory_space=pl.ANY)] * self.n

    def out_shapes(self):
        return ([jax.ShapeDtypeStruct((4,) + x.shape, x.dtype) for x in self.gather]
                + [jax.ShapeDtypeStruct((SCATTER_COPIES, g.shape[1] // 2, g.shape[2]), g.dtype) for g in self.scatter])

    def scratch(self):
        return ((_gather_sems(len(self.gather)) if self.gather else [])
                + (_scatter_sems(len(self.scatter)) if self.scatter else []))

    def _phase(self, ph, ins, outs, sems):
        ng = len(self.gather)
        g_sems, s_sems = (sems[:3], sems[3:]) if ng else ((), sems)
        for slot in range(ng):
            _gather_phase(ph, ins[slot], outs[slot], *g_sems, slot)
        if ph != 1:
            for slot in range(len(self.scatter)):
                _scatter_phase(0 if ph == 0 else 1, ins[ng + slot], outs[ng + slot], *s_sems, slot)

    def open(self, step, steps, ins, outs, sems):
        if self.n:
            pl.when(step == 0)(lambda: self._phase(0, ins, outs, sems))
            pl.when(step == steps // 2)(lambda: self._phase(1, ins, outs, sems))

    def close(self, step, steps, ins, outs, sems):
        if self.n:
            pl.when(step == steps - 1)(lambda: self._phase(2, ins, outs, sems))


def _mm(a, b, mode, out_dtype, name, hi=False, gather=(), chips=None, scatter=()):
    b_shape = b.shape
    if chips == "b":
        b_shape = (b.shape[1], 4 * b.shape[2])
    if mode == "nn":
        (m, k), (_, n) = a.shape, b_shape
        dims = (((1,), (0,)), ((), ()))
    elif mode == "nt":
        (m, k), (n, _) = a.shape, b_shape
        dims = (((1,), (1,)), ((), ()))
    else:
        (k, m), (_, n) = a.shape, b_shape
        dims = (((0,), (0,)), ((), ()))
    tm = _pick(m, MM_TILE_MN if mode == "tn" else 1024, LANES if mode == "tn" else 16)
    tn = _pick(n // 4 if chips and mode != "nt" else n, MM_TILE_MN, LANES)
    tk = _pick(k // 4 if chips and mode == "nt" else k, 1024 if mode == "tn" else MM_TILE_MN, LANES)
    nk = k // tk
    nb = (n // 4) // tn
    kb = (k // 4) // tk
    if mode == "nn":
        a_spec = pl.BlockSpec((tm, tk), lambda i, j, kk: (i, kk))
        b_spec = pl.BlockSpec((tk, tn), lambda i, j, kk: (kk, j))
        if chips == "b":
            b_spec = pl.BlockSpec((None, tk, tn), lambda i, j, kk: (j // nb, kk, j % nb))
    elif mode == "nt":
        a_spec = pl.BlockSpec((tm, tk), lambda i, j, kk: (i, kk))
        b_spec = pl.BlockSpec((tn, tk), lambda i, j, kk: (j, kk))
        if chips == "b":
            b_spec = pl.BlockSpec((None, tn, tk), lambda i, j, kk: (kk // kb, j, kk % kb))
    else:
        a_spec = pl.BlockSpec((tk, tm), lambda i, j, kk: (kk, i))
        b_spec = pl.BlockSpec((tk, tn), lambda i, j, kk: (kk, j))
    out_spec = pl.BlockSpec((tm, tn), lambda i, j, kk: (i, j))
    out_shape = jax.ShapeDtypeStruct((m, n), out_dtype)
    if chips == "out":
        out_spec = pl.BlockSpec((None, tm, tn), lambda i, j, kk: (j // nb, i, j % nb))
        out_shape = jax.ShapeDtypeStruct((4, m, n // 4), out_dtype)

    host = _Hosted(gather, scatter)
    ng = host.n
    grid = (m // tm, n // tn, nk)
    steps = grid[0] * grid[1] * grid[2]

    def body(*refs):
        a_ref, b_ref = refs[:2]
        x_refs = refs[2:2 + ng]
        o_ref = refs[2 + ng]
        got_refs = refs[3 + ng:3 + 2 * ng]
        acc_ref = refs[3 + 2 * ng]
        sems = refs[4 + 2 * ng:]
        kk = pl.program_id(2)
        step = (pl.program_id(0) * grid[1] + pl.program_id(1)) * nk + kk
        host.open(step, steps, x_refs, got_refs, sems)

        @pl.when(kk == 0)
        def _():
            acc_ref[...] = jnp.zeros(acc_ref.shape, F32)

        av, bv = a_ref[...], b_ref[...]
        if hi:
            acc_ref[...] += lax.dot_general(av, bv, dims, precision=HI, preferred_element_type=F32)
        else:
            acc_ref[...] += lax.dot_general(av.astype(BF16), bv.astype(BF16), dims,
                                            preferred_element_type=F32)

        @pl.when(kk == nk - 1)
        def _():
            o_ref[...] = acc_ref[...].astype(o_ref.dtype)

        host.close(step, steps, x_refs, got_refs, sems)

    res = pl.pallas_call(
        body, name=name, grid=grid,
        in_specs=[a_spec, b_spec] + host.specs(),
        out_specs=[out_spec] + host.specs(),
        out_shape=[out_shape] + host.out_shapes(),
        scratch_shapes=[pltpu.VMEM((tm, tn), F32)] + host.scratch(),
        compiler_params=_params(("arbitrary",) * 3 if ng else ("parallel", "parallel", "arbitrary")),
    )(a, b, *host.operands)
    return res if ng else res[0]


def _rms(x, w=None, n=None):
    n = x.shape[-1] if n is None else n
    y = x * lax.rsqrt(jnp.sum(x * x, axis=-1, keepdims=True) * (1.0 / n) + EPS)
    return y if w is None else y * w


def _silu(x):
    return x * jax.nn.sigmoid(x)


def _softplus(x):
    return jnp.maximum(x, 0.0) + jnp.log1p(jnp.exp(-jnp.abs(x)))


def _split(x, widths):
    out, o = [], 0
    for w in widths:
        out.append(x[:, o:o + w])
        o += w
    return out


def _modulate(x, s, sh):
    return _rms(x) * (1.0 + s) + sh


def _rope_rot(x):
    r, c = _iota((LANES, LANES), 0), _iota((LANES, LANES), 1)
    half = MLA_ROPE // 2
    perm = (((r < half) & (c == r + half)) | ((r >= half) & (r < MLA_ROPE) & (c == r - half))).astype(F32)
    return jnp.dot(x, perm, precision=HI, preferred_element_type=F32)


def _rope(x, cos2, sin2):
    return x * cos2 + _rope_rot(x) * sin2


def _gdn_prep_core(qkv_parts, gab, a_log, dt_bias):
    act = [_silu(p) for p in qkv_parts]
    qs = [p * lax.rsqrt(jnp.sum(p * p, -1, keepdims=True) + EPS) * (HEAD ** -0.5) for p in act[:4]]
    ks = [p * lax.rsqrt(jnp.sum(p * p, -1, keepdims=True) + EPS) for p in act[4:8]]
    lane = _iota(gab.shape, 1)
    g = -jnp.exp(a_log) * _softplus(gab + dt_bias)
    beta = jax.nn.sigmoid(gab)
    gb = jnp.where(lane < GDN_HEADS, g, jnp.where(lane < 2 * GDN_HEADS, beta, 0.0))
    return (jnp.concatenate(qs, 1), jnp.concatenate(ks, 1), jnp.concatenate(act[8:], 1), gb)


def _mla_prep_core(cq, ckv, kr, cos2, sin2, wq, wkv, wkr):
    cqn = _rms(cq, wq)
    ckvn = _rms(ckv, wkv)
    k_rope = _rope(_rms(kr, wkr, MLA_ROPE), cos2, sin2)
    return cqn, ckvn, k_rope


def _qk_prep_core(qn_parts, qr_parts, kn_parts, v_parts, k_rope, cos2, sin2, wqn, wqr, wkn):
    qs, ks = [], []
    for h in range(MLA_HEADS):
        qn = _rms(qn_parts[h], wqn) * ATT_SCALE
        qr = _rope(_rms(qr_parts[h], wqr, MLA_ROPE), cos2, sin2) * ATT_SCALE
        qs.append(jnp.concatenate([qn, qr], 1))
        ks.append(jnp.concatenate([_rms(kn_parts[h], wkn), k_rope], 1))
    return qs, ks, list(v_parts)


def _mix_post_core(o_parts, gz_parts, ob_parts, wn, won):
    oa = [_rms(o, wn) * _silu(z) for o, z in zip(o_parts, gz_parts)]
    ob = [_rms(o, won) for o in ob_parts]
    return jnp.concatenate(oa + ob, 1)


CONV_K = 4
HALO = SUBLANES


def _conv_fwd(proj, w8, name):
    s = proj.shape[0]
    c = w8.shape[1]
    t = min(256, s)
    n = s // t
    hb = t // HALO

    def body(x_ref, prev_ref, w_ref, o_ref, buf):
        i = pl.program_id(0)
        buf[pl.ds(0, HALO), :] = jnp.where(i > 0, prev_ref[...], 0.0)
        buf[pl.ds(HALO, t), :] = x_ref[...]
        acc = jnp.zeros((t, c), F32)
        for k in range(CONV_K):
            acc = acc + w_ref[k:k + 1, :] * buf[pl.ds(HALO - (CONV_K - 1) + k, t), :]
        o_ref[...] = acc

    return pl.pallas_call(
        body, name=name, grid=(n,),
        in_specs=[pl.BlockSpec((t, c), lambda i: (i, 0)),
                  pl.BlockSpec((HALO, c), lambda i: (jnp.maximum(i * hb - 1, 0), 0)),
                  pl.BlockSpec(w8.shape, lambda i: (0, 0))],
        out_specs=pl.BlockSpec((t, c), lambda i: (i, 0)),
        out_shape=jax.ShapeDtypeStruct((s, c), F32),
        scratch_shapes=[pltpu.VMEM((t + HALO, c), F32)],
        compiler_params=_params(("parallel",)),
    )(proj, proj, w8)


def _conv_bwd(proj, dy, w8, name):
    s = proj.shape[0]
    c = w8.shape[1]
    t = min(256, s)
    n = s // t
    hb = t // HALO

    def body(x_ref, prev_ref, dy_ref, next_ref, w_ref, dx_ref, dw_ref, bufx, bufd):
        i = pl.program_id(0)
        bufx[pl.ds(0, HALO), :] = jnp.where(i > 0, prev_ref[...], 0.0)
        bufx[pl.ds(HALO, t), :] = x_ref[...]
        bufd[pl.ds(0, t), :] = dy_ref[...]
        bufd[pl.ds(t, HALO), :] = jnp.where(i < n - 1, next_ref[...], 0.0)

        @pl.when(i == 0)
        def _():
            dw_ref[...] = jnp.zeros(dw_ref.shape, F32)

        dyv = dy_ref[...]
        acc = jnp.zeros((t, c), F32)
        for k in range(CONV_K):
            acc = acc + w_ref[k:k + 1, :] * bufd[pl.ds(CONV_K - 1 - k, t), :]
            dw_ref[k:k + 1, :] += jnp.sum(dyv * bufx[pl.ds(HALO - (CONV_K - 1) + k, t), :], axis=0, keepdims=True)
        dx_ref[...] = acc

    return pl.pallas_call(
        body, name=name, grid=(n,),
        in_specs=[pl.BlockSpec((t, c), lambda i: (i, 0)),
                  pl.BlockSpec((HALO, c), lambda i: (jnp.maximum(i * hb - 1, 0), 0)),
                  pl.BlockSpec((t, c), lambda i: (i, 0)),
                  pl.BlockSpec((HALO, c), lambda i: (jnp.minimum((i + 1) * hb, s // HALO - 1), 0)),
                  pl.BlockSpec(w8.shape, lambda i: (0, 0))],
        out_specs=[pl.BlockSpec((t, c), lambda i: (i, 0)), pl.BlockSpec(w8.shape, lambda i: (0, 0))],
        out_shape=[jax.ShapeDtypeStruct((s, c), F32), jax.ShapeDtypeStruct(w8.shape, F32)],
        scratch_shapes=[pltpu.VMEM((t + HALO, c), F32), pltpu.VMEM((t + HALO, c), F32)],
        compiler_params=_params(("arbitrary",)),
    )(proj, proj, dy, dy, w8)


_B_NN = (((2,), (1,)), ((0,), (0,)))
_B_NT = (((2,), (2,)), ((0,), (0,)))
_B_TN = (((1,), (1,)), ((0,), (0,)))


def _dot3(a, b, dims):
    return lax.dot_general(a, b, dims, precision=lax.Precision.HIGH, preferred_element_type=F32)


def _bdot_hi(a, b):
    return _dot3(a, b, _B_NN)


class _Dots:
    nn = staticmethod(lambda a, b: _dot3(a, b, _B_NN))
    nt = staticmethod(lambda a, b: _dot3(a, b, _B_NT))
    tn = staticmethod(lambda a, b: _dot3(a, b, _B_TN))


def _unit_lower_inverse(a, dots):
    c = a.shape[-1]
    ri, ci = _iota(a.shape, 1), _iota(a.shape, 2)
    inner = (ri // 2) == (ci // 2)
    t = (ri == ci).astype(F32) - jnp.where(inner, a, 0.0)
    blk = 4
    while blk <= c:
        outer = (ri // blk) == (ci // blk)
        low = jnp.where(outer & jnp.logical_not(inner), a, 0.0)
        t = t - dots.nn(dots.nn(t, low), t)
        inner = outer
        blk *= 2
    return t


def _stack(xs):
    return jnp.concatenate([x[None] for x in xs], axis=0)


def _gdn_local(dots, q, k, v, gbs):
    b, c, _ = q.shape
    gcols, bcols = [], []
    for gb in gbs:
        lane = _iota(gb.shape, 1)
        for h in range(GDN_HEADS):
            gcols.append(jnp.sum(jnp.where(lane == h, gb, 0.0), axis=1, keepdims=True))
            bcols.append(jnp.sum(jnp.where(lane == GDN_HEADS + h, gb, 0.0), axis=1, keepdims=True))
    gcol, bcol = _stack(gcols), _stack(bcols)
    ri, ci = _iota((b, c, c), 1), _iota((b, c, c), 2)
    incl = ri >= ci
    tril = incl.astype(F32)
    g_cc = _bdot_hi(tril, jnp.broadcast_to(gcol, (b, c, c)))
    g_row = _bdot_hi(jnp.ones((b, c, c), F32), jnp.where(ri == ci, g_cc, 0.0))
    g_cl = _bdot_hi(tril, jnp.broadcast_to(gcol, (b, c, HEAD)))
    g_last = jnp.sum(jnp.broadcast_to(gcol, (b, c, HEAD)), axis=1, keepdims=True)
    decay = jnp.where(incl, jnp.exp(jnp.where(incl, g_cc - g_row, 0.0)), 0.0)
    kk = dots.nt(k, k)
    minv = _unit_lower_inverse(jnp.where(ri > ci, bcol * kk * decay, 0.0), dots)
    e_g = jnp.exp(g_cl)
    u = dots.nn(minv, v * bcol)
    wk = dots.nn(minv, k * (bcol * e_g))
    qk = dots.nt(q, k) * decay
    return u, wk, q * e_g, k * jnp.exp(g_last - g_cl), qk, jnp.exp(g_last)


def _gdn_scan(dots, states, u, wk, qd, kd, qk, gl_tile):
    lane, row = _iota(gl_tile.shape, 1), _iota(gl_tile.shape, 0)
    gl = _stack([
        jnp.sum(jnp.sum(jnp.where((lane == h) & (row == 0), gl_tile, 0.0), axis=1, keepdims=True),
                axis=0, keepdims=True) for h in range(GDN_HEADS)])
    v_new = u - dots.nn(wk, states)
    o = dots.nn(qd, states) + dots.nn(qk, v_new)
    return states * gl + dots.tn(kd, v_new), o


def _heads(x):
    return jnp.stack(_split(x, HW4))


GDN_W = GDN_HEADS * HEAD
HW4 = [HEAD] * GDN_HEADS
LOCAL_CHUNKS = 4
_CHUNK_ROWS = [pl.ds(cc * CHUNK, CHUNK) for cc in range(LOCAL_CHUNKS)]


def _chunk_heads(ref):
    return jnp.concatenate([_heads(ref[rows, :]) for rows in _CHUNK_ROWS], 0)


def _gdn_local_fwd(q, k, v, gb, name):
    s = q.shape[0]
    t = LOCAL_CHUNKS * CHUNK

    def body(q_ref, k_ref, v_ref, gb_ref, u_ref, wk_ref, qd_ref, kd_ref, qk_ref, gl_ref):
        u, wk, qd, kd, qk, gl = _gdn_local(_Dots, _chunk_heads(q_ref), _chunk_heads(k_ref),
                                           _chunk_heads(v_ref), [gb_ref[rows, :] for rows in _CHUNK_ROWS])
        lane = _iota((CHUNK, LANES), 1)
        for cc, rows in enumerate(_CHUNK_ROWS):
            gl_tile = jnp.zeros((CHUNK, LANES), F32)
            for h in range(GDN_HEADS):
                b, cols = cc * GDN_HEADS + h, pl.ds(h * HEAD, HEAD)
                u_ref[rows, cols] = u[b]
                wk_ref[rows, cols] = wk[b]
                qd_ref[rows, cols] = qd[b]
                kd_ref[rows, cols] = kd[b]
                qk_ref[h, rows, :] = qk[b]
                gl_tile = gl_tile + jnp.where(lane == h, gl[b], 0.0)
            gl_ref[rows, :] = gl_tile

    row = pl.BlockSpec((t, GDN_W), lambda i: (i, 0))
    lane = pl.BlockSpec((t, LANES), lambda i: (i, 0))
    qks = pl.BlockSpec((GDN_HEADS, t, CHUNK), lambda i: (0, i, 0))
    return pl.pallas_call(
        body, name=name, grid=(s // t,),
        in_specs=[row, row, row, lane],
        out_specs=[row, row, row, row, qks, lane],
        out_shape=[jax.ShapeDtypeStruct((s, GDN_W), F32)] * 4
        + [jax.ShapeDtypeStruct((GDN_HEADS, s, CHUNK), F32), jax.ShapeDtypeStruct((s, LANES), F32)],
        compiler_params=_params(("parallel",)),
    )(q, k, v, gb)


def _gdn_local_bwd(q, k, v, gb, du, dwk, dqd, dkd, dqk, dgl, name):
    s = q.shape[0]
    t = LOCAL_CHUNKS * CHUNK

    def body(q_ref, k_ref, v_ref, gb_ref, du_ref, dwk_ref, dqd_ref, dkd_ref, dqk_ref, dgl_ref,
             dq_ref, dk_ref, dv_ref, dgb_ref):
        _, vjp = jax.vjp(functools.partial(_gdn_local, _Dots), _chunk_heads(q_ref), _chunk_heads(k_ref),
                         _chunk_heads(v_ref), [gb_ref[rows, :] for rows in _CHUNK_ROWS])
        lane = _iota((CHUNK, LANES), 1)
        dqk = jnp.stack([dqk_ref[h, rows, :] for rows in _CHUNK_ROWS for h in range(GDN_HEADS)])
        dgl = jnp.stack([jnp.sum(jnp.where(lane == h, dgl_ref[rows, :], 0.0), axis=0, keepdims=True)
                         for rows in _CHUNK_ROWS for h in range(GDN_HEADS)])
        d_q, d_k, d_v, d_gbs = vjp((_chunk_heads(du_ref), _chunk_heads(dwk_ref), _chunk_heads(dqd_ref),
                                    _chunk_heads(dkd_ref), dqk, dgl))
        for cc, rows in enumerate(_CHUNK_ROWS):
            for h in range(GDN_HEADS):
                b, cols = cc * GDN_HEADS + h, pl.ds(h * HEAD, HEAD)
                dq_ref[rows, cols] = d_q[b]
                dk_ref[rows, cols] = d_k[b]
                dv_ref[rows, cols] = d_v[b]
            dgb_ref[rows, :] = d_gbs[cc]

    row = pl.BlockSpec((t, GDN_W), lambda i: (i, 0))
    lane = pl.BlockSpec((t, LANES), lambda i: (i, 0))
    qks = pl.BlockSpec((GDN_HEADS, t, CHUNK), lambda i: (0, i, 0))
    return pl.pallas_call(
        body, name=name, grid=(s // t,),
        in_specs=[row, row, row, lane, row, row, row, row, qks, lane],
        out_specs=[row, row, row, lane],
        out_shape=[jax.ShapeDtypeStruct((s, GDN_W), F32)] * 3 + [jax.ShapeDtypeStruct((s, LANES), F32)],
        compiler_params=_params(("parallel",)),
    )(q, k, v, gb, du, dwk, dqd, dkd, dqk, dgl)


def _gdn_scan_fwd(u, wk, qd, kd, qk, gl, name):
    s = u.shape[0]
    nc = s // CHUNK

    def body(u_ref, wk_ref, qd_ref, kd_ref, qk_ref, gl_ref, o_ref, st_ref, state):
        i = pl.program_id(0)

        @pl.when(i == 0)
        def _():
            state[...] = jnp.zeros(state.shape, F32)

        st_ref[...] = state[...]
        new_states, o = _gdn_scan(_Dots, state[...], _heads(u_ref[...]), _heads(wk_ref[...]),
                                  _heads(qd_ref[...]), _heads(kd_ref[...]), qk_ref[...], gl_ref[...])
        state[...] = new_states
        o_ref[...] = jnp.concatenate([o[h] for h in range(GDN_HEADS)], 1)

    row = pl.BlockSpec((CHUNK, GDN_W), lambda i: (i, 0))
    return pl.pallas_call(
        body, name=name, grid=(nc,),
        in_specs=[row, row, row, row, pl.BlockSpec((GDN_HEADS, CHUNK, CHUNK), lambda i: (0, i, 0)),
                  pl.BlockSpec((CHUNK, LANES), lambda i: (i, 0))],
        out_specs=[row, pl.BlockSpec((None, GDN_HEADS, HEAD, HEAD), lambda i: (i, 0, 0, 0))],
        out_shape=[jax.ShapeDtypeStruct((s, GDN_W), F32),
                   jax.ShapeDtypeStruct((nc, GDN_HEADS, HEAD, HEAD), F32)],
        scratch_shapes=[pltpu.VMEM((GDN_HEADS, HEAD, HEAD), F32)],
        compiler_params=_params(("arbitrary",)),
    )(u, wk, qd, kd, qk, gl)


def _gdn_scan_bwd(u, wk, qd, kd, qk, gl, st, do, name):
    s = u.shape[0]
    nc = s // CHUNK

    def body(u_ref, wk_ref, qd_ref, kd_ref, qk_ref, gl_ref, st_ref, do_ref,
             du_ref, dwk_ref, dqd_ref, dkd_ref, dqk_ref, dgl_ref, dstate):
        i = pl.program_id(0)

        @pl.when(i == 0)
        def _():
            dstate[...] = jnp.zeros(dstate.shape, F32)

        _, vjp = jax.vjp(functools.partial(_gdn_scan, _Dots), st_ref[...], _heads(u_ref[...]),
                         _heads(wk_ref[...]), _heads(qd_ref[...]), _heads(kd_ref[...]), qk_ref[...], gl_ref[...])
        d_states, d_u, d_wk, d_qd, d_kd, d_qk, d_gl = vjp((dstate[...], _heads(do_ref[...])))
        dstate[...] = d_states
        dqk_ref[...] = d_qk
        unheads = lambda x: jnp.concatenate([x[h] for h in range(GDN_HEADS)], 1)
        du_ref[...] = unheads(d_u)
        dwk_ref[...] = unheads(d_wk)
        dqd_ref[...] = unheads(d_qd)
        dkd_ref[...] = unheads(d_kd)
        dgl_ref[...] = d_gl

    rev = lambda i: (nc - 1 - i, 0)
    row = pl.BlockSpec((CHUNK, GDN_W), rev)
    lane = pl.BlockSpec((CHUNK, LANES), rev)
    qks = pl.BlockSpec((GDN_HEADS, CHUNK, CHUNK), lambda i: (0, nc - 1 - i, 0))
    return pl.pallas_call(
        body, name=name, grid=(nc,),
        in_specs=[row, row, row, row, qks, lane,
                  pl.BlockSpec((None, GDN_HEADS, HEAD, HEAD), lambda i: (nc - 1 - i, 0, 0, 0)), row],
        out_specs=[row, row, row, row, qks, lane],
        out_shape=[jax.ShapeDtypeStruct((s, GDN_W), F32)] * 4
        + [jax.ShapeDtypeStruct((GDN_HEADS, s, CHUNK), F32), jax.ShapeDtypeStruct((s, LANES), F32)],
        scratch_shapes=[pltpu.VMEM((GDN_HEADS, HEAD, HEAD), F32)],
        compiler_params=_params(("arbitrary",)),
    )(u, wk, qd, kd, qk, gl, st, do)


def _chunk_mask(i, j, t):
    r = i * t + _iota((t, t), 0)
    c = j * t + _iota((t, t), 1)
    return (r // CHUNK) >= (c // CHUNK)


ATT_TILE = 1024
ATT_Q_TILES = 1
ATT_BWD_TILE = 1024


def _attn_fwd(q, k, v, name, gather=()):
    nh, s = MLA_HEADS, q.shape[0]
    tk = min(ATT_TILE, s)
    tq = min(ATT_Q_TILES * tk, s)
    qk = tq // tk
    nq, n = s // tq, s // tk
    nt = (((1,), (1,)), ((), ()))
    host = _Hosted(gather)
    ng = host.n
    steps = nh * nq * n

    def body(*refs):
        q_ref, k_ref, v_ref = refs[:3]
        x_refs = refs[3:3 + ng]
        o_ref, lse_ref = refs[3 + ng:5 + ng]
        got_refs = refs[5 + ng:5 + 2 * ng]
        m_sc, l_sc, acc_sc = refs[5 + 2 * ng:8 + 2 * ng]
        sems = refs[8 + 2 * ng:]
        i, j = pl.program_id(1), pl.program_id(2)
        step_no = (pl.program_id(0) * nq + i) * n + j
        host.open(step_no, steps, x_refs, got_refs, sems)

        @pl.when(j == 0)
        def _():
            m_sc[...] = jnp.full(m_sc.shape, -jnp.inf, F32)
            l_sc[...] = jnp.zeros(l_sc.shape, F32)
            acc_sc[...] = jnp.zeros(acc_sc.shape, F32)

        def step(masked):
            sc = lax.dot_general(q_ref[...], k_ref[...], nt, preferred_element_type=F32)
            if masked:
                r = i * tq + _iota((tq, tk), 0)
                c = j * tk + _iota((tq, tk), 1)
                sc = jnp.where((r // CHUNK) >= (c // CHUNK), sc, -jnp.inf)
            m_prev = m_sc[:, :1]
            m_new = jnp.maximum(m_prev, jnp.max(sc, axis=1, keepdims=True))
            alpha = jnp.exp(m_prev - m_new)
            p = jnp.exp(sc - m_new)
            l_sc[...] = jnp.broadcast_to(alpha * l_sc[:, :1] + jnp.sum(p, axis=1, keepdims=True), l_sc.shape)
            acc_sc[...] = alpha * acc_sc[...] + jnp.dot(p.astype(BF16), v_ref[...], preferred_element_type=F32)
            m_sc[...] = jnp.broadcast_to(m_new, m_sc.shape)

        pl.when(j < i * qk)(lambda: step(False))
        pl.when(j // qk == i)(lambda: step(True))

        @pl.when(j == n - 1)
        def _():
            o_ref[...] = acc_sc[...] / l_sc[:, :1]
            lse_ref[...] = m_sc[...] + jnp.log(l_sc[...])

        host.close(step_no, steps, x_refs, got_refs, sems)

    qrow = lambda h, i, j: (i, h)
    krow = lambda h, i, j: (jnp.minimum(j, (i + 1) * qk - 1), h)
    res = pl.pallas_call(
        body, name=name, grid=(nh, nq, n),
        in_specs=[pl.BlockSpec((tq, QK_PAD), qrow), pl.BlockSpec((tk, QK_PAD), krow),
                  pl.BlockSpec((tk, HEAD), krow)] + host.specs(),
        out_specs=[pl.BlockSpec((tq, HEAD), qrow), pl.BlockSpec((None, tq, LANES), lambda h, i, j: (h, i, 0))]
        + host.specs(),
        out_shape=[jax.ShapeDtypeStruct((s, nh * HEAD), F32), jax.ShapeDtypeStruct((nh, s, LANES), F32)]
        + host.out_shapes(),
        scratch_shapes=[pltpu.VMEM((tq, LANES), F32), pltpu.VMEM((tq, LANES), F32), pltpu.VMEM((tq, HEAD), F32)]
        + host.scratch(),
        compiler_params=_params(("arbitrary",) * 3 if ng else ("parallel", "parallel", "arbitrary")),
    )(q, k, v, *host.operands)
    return res[0], res[1], list(res[2:])


def _attn_bwd(q, k, v, o, do, lse, name, scatter=()):
    nh, s = MLA_HEADS, q.shape[0]
    t = min(ATT_BWD_TILE, s)
    n = s // t
    tn = (((0,), (0,)), ((), ()))
    nt = (((1,), (1,)), ((), ()))
    host = _Hosted(scatter=scatter)
    nx = host.n
    steps = nh * n * n

    def body(*refs):
        q_ref, k_ref, v_ref, o_ref, do_ref, lse_ref = refs[:6]
        x_refs = refs[6:6 + nx]
        dq_ref, dk_ref, dv_ref = refs[6 + nx:9 + nx]
        got_refs = refs[9 + nx:9 + 2 * nx]
        dk_acc, dv_acc, dq_acc = refs[9 + 2 * nx:12 + 2 * nx]
        sems = refs[12 + 2 * nx:]
        j, i = pl.program_id(1), pl.program_id(2)
        step_no = (pl.program_id(0) * n + j) * n + i
        host.open(step_no, steps, x_refs, got_refs, sems)

        @pl.when(i + j == 0)
        def _():
            dq_acc[...] = jnp.zeros(dq_acc.shape, F32)

        @pl.when(i == 0)
        def _():
            dk_acc[...] = jnp.zeros(dk_acc.shape, F32)
            dv_acc[...] = jnp.zeros(dv_acc.shape, F32)

        def step(masked):
            qv, kv, do = q_ref[...], k_ref[...], do_ref[...]
            sc = lax.dot_general(qv, kv, nt, preferred_element_type=F32)
            p = jnp.exp(sc - lse_ref[:, :1])
            if masked:
                p = jnp.where(_chunk_mask(i, j, t), p, 0.0)
            dob = do.astype(BF16)
            dp = lax.dot_general(dob, v_ref[...], nt, preferred_element_type=F32)
            ds = (p * (dp - jnp.sum(do * o_ref[...], axis=1, keepdims=True))).astype(BF16)
            dv_acc[...] += lax.dot_general(p.astype(BF16), dob, tn, preferred_element_type=F32)
            dk_acc[...] += lax.dot_general(ds, qv, tn, preferred_element_type=F32)
            rows = pl.ds(pl.multiple_of(i * t, t), t)
            dq_acc[rows, :] += jnp.dot(ds, kv, preferred_element_type=F32)

        pl.when(i > j)(lambda: step(False))
        pl.when(i == j)(lambda: step(True))

        @pl.when(i == n - 1)
        def _():
            dk_ref[...] = dk_acc[...]
            dv_ref[...] = dv_acc[...]

        @pl.when(i + j == 2 * (n - 1))
        def _():
            dq_ref[...] = dq_acc[...]

        host.close(step_no, steps, x_refs, got_refs, sems)

    qrow = lambda h, j, i: (jnp.maximum(i, j), h)
    krow = lambda h, j, i: (j, h)
    res = pl.pallas_call(
        body, name=name, grid=(nh, n, n),
        in_specs=[pl.BlockSpec((t, QK_PAD), qrow), pl.BlockSpec((t, QK_PAD), krow), pl.BlockSpec((t, HEAD), krow),
                  pl.BlockSpec((t, HEAD), qrow), pl.BlockSpec((t, HEAD), qrow),
                  pl.BlockSpec((None, t, LANES), lambda h, j, i: (h, jnp.maximum(i, j), 0))] + host.specs(),
        out_specs=[pl.BlockSpec((s, QK_PAD), lambda h, j, i: (0, h)),
                   pl.BlockSpec((t, QK_PAD), krow), pl.BlockSpec((t, HEAD), krow)] + host.specs(),
        out_shape=[jax.ShapeDtypeStruct((s, nh * QK_PAD), F32), jax.ShapeDtypeStruct((s, nh * QK_PAD), F32),
                   jax.ShapeDtypeStruct((s, nh * HEAD), F32)] + host.out_shapes(),
        scratch_shapes=[pltpu.VMEM((t, QK_PAD), F32), pltpu.VMEM((t, HEAD), F32), pltpu.VMEM((s, QK_PAD), F32)]
        + host.scratch(),
        compiler_params=_params(("arbitrary", "arbitrary", "arbitrary")),
    )(q, k, v, o, do, lse, *host.operands)
    return res[0], res[1], res[2], list(res[3:])


def _place():
    return lax.axis_index("x"), lax.axis_index("y"), lax.axis_index("c")


def _allgather8(x, name):
    r, c = x.shape

    def body(x_ref, out_ref, send_sems, recv_sems, local_sem):
        mx, my, mc = _place()
        me = 4 * mx + 2 * my + mc
        mine = pltpu.make_async_copy(x_ref, out_ref.at[me], local_sem)
        mine.start()
        copies = []
        for d in range(1, 8):
            px = 1 - mx if d & 4 else mx
            py = 1 - my if d & 2 else my
            pc = 1 - mc if d & 1 else mc
            cp = pltpu.make_async_remote_copy(
                src_ref=x_ref, dst_ref=out_ref.at[me], send_sem=send_sems.at[d - 1], recv_sem=recv_sems.at[d - 1],
                device_id=(px, py, pc), device_id_type=MESH)
            cp.start()
            copies.append(cp)
        for cp in copies:
            cp.wait()
        mine.wait()

    return pl.pallas_call(
        body, name=name,
        out_shape=jax.ShapeDtypeStruct((8, r, c), x.dtype),
        in_specs=[pl.BlockSpec(memory_space=pltpu.VMEM)],
        out_specs=pl.BlockSpec(memory_space=pltpu.VMEM),
        scratch_shapes=[pltpu.SemaphoreType.DMA((7,)), pltpu.SemaphoreType.DMA((7,)), pltpu.SemaphoreType.DMA],
        compiler_params=pltpu.CompilerParams(vmem_limit_bytes=VMEM_LIMIT),
    )(x)


def _allgather_chips(x, name):
    r, c = x.shape

    def body(x_ref, out_ref, send_sems, recv_sems, local_sems):
        for phase in range(3):
            _gather_phase(phase, x_ref, out_ref, send_sems, recv_sems, local_sems, 0)

    return pl.pallas_call(
        body, name=name,
        out_shape=jax.ShapeDtypeStruct((4, r, c), x.dtype),
        in_specs=[pl.BlockSpec(memory_space=pltpu.VMEM)],
        out_specs=pl.BlockSpec(memory_space=pltpu.VMEM),
        scratch_shapes=_gather_sems(1),
        compiler_params=pltpu.CompilerParams(vmem_limit_bytes=VMEM_LIMIT),
    )(x)


GATHER_COPIES = 6


def _gather_sems(n):
    return [pltpu.SemaphoreType.DMA((GATHER_COPIES * n,)), pltpu.SemaphoreType.DMA((GATHER_COPIES * n,)),
            pltpu.SemaphoreType.DMA((n,))]


def _gather_phase(phase, x_ref, out_ref, send_sems, recv_sems, local_sems, slot):
    mx, my, mc = _place()
    j = 2 * mx + my
    rh = x_ref.shape[0] // 2
    base = GATHER_COPIES * slot
    chips = [(1 - mx, my), (mx, 1 - my), (1 - mx, 1 - my)]
    sibling = (mx, my, 1 - mc)

    def half(jj, hc):
        return out_ref.at[jj, pl.ds(hc * rh, rh), :]

    def over_ici(kk, block):
        px, py = chips[kk]
        return pltpu.make_async_remote_copy(
            src_ref=x_ref.at[pl.ds(mc * rh, rh), :], dst_ref=half(block, mc), send_sem=send_sems.at[base + kk],
            recv_sem=recv_sems.at[base + kk], device_id=(px, py, mc), device_id_type=MESH)

    def to_sibling(kk, hc):
        px, py = chips[kk]
        blk = half(2 * px + py, hc)
        return pltpu.make_async_remote_copy(
            src_ref=blk, dst_ref=blk, send_sem=send_sems.at[base + 3 + kk], recv_sem=recv_sems.at[base + 3 + kk],
            device_id=sibling, device_id_type=MESH)

    mine = pltpu.make_async_copy(x_ref, out_ref.at[j], local_sems.at[slot])
    if phase == 0:
        mine.start()
        for kk in range(3):
            over_ici(kk, j).start()
    elif phase == 1:
        for kk, (px, py) in enumerate(chips):
            over_ici(kk, 2 * px + py).wait_recv()
            to_sibling(kk, mc).start()
    else:
        for kk in range(3):
            to_sibling(kk, 1 - mc).wait_recv()
        for kk in range(3):
            over_ici(kk, j).wait_send()
            to_sibling(kk, mc).wait_send()
        mine.wait()


RS_ROWS = 32


def _reduce_scatter_chips(g, name):
    _, r, c = g.shape
    rh = r // 2
    steps = rh // RS_ROWS

    def body(g_ref, out_ref, sib_ref, part_ref, got_ref, send_sems, recv_sems):
        mx, my, mc = _place()
        j = 2 * mx + my
        sibling = (mx, my, 1 - mc)
        chips = [(1 - mx, my), (mx, 1 - my), (1 - mx, 1 - my)]

        to_sib = pltpu.make_async_remote_copy(
            src_ref=g_ref.at[:, pl.ds((1 - mc) * rh, rh), :], dst_ref=sib_ref,
            send_sem=send_sems.at[0], recv_sem=recv_sems.at[0], device_id=sibling, device_id_type=MESH)
        to_sib.start()
        to_sib.wait()

        def add_sibling(step, carry):
            rows = pl.ds(pl.multiple_of(step * RS_ROWS, RS_ROWS), RS_ROWS)
            mine = g_ref[:, pl.ds(pl.multiple_of(mc * rh + step * RS_ROWS, RS_ROWS), RS_ROWS), :]
            part_ref[:, rows, :] = mine.astype(F32) + sib_ref[:, rows, :].astype(F32)
            return carry

        lax.fori_loop(0, steps, add_sibling, 0)

        def to_bf16(step, carry):
            rows = pl.ds(pl.multiple_of(step * RS_ROWS, RS_ROWS), RS_ROWS)
            sib_ref[:, rows, :] = part_ref[:, rows, :].astype(BF16)
            return carry

        lax.fori_loop(0, steps, to_bf16, 0)

        sends = []
        for kk, (px, py) in enumerate(chips):
            cp = pltpu.make_async_remote_copy(
                src_ref=sib_ref.at[2 * px + py], dst_ref=got_ref.at[kk],
                send_sem=send_sems.at[1 + kk], recv_sem=recv_sems.at[1 + kk],
                device_id=(px, py, mc), device_id_type=MESH)
            cp.start()
            sends.append(cp)
        for cp in sends:
            cp.wait()

        def total(step, carry):
            rows = pl.ds(pl.multiple_of(step * RS_ROWS, RS_ROWS), RS_ROWS)
            acc = part_ref[j, rows, :]
            for kk in range(3):
                acc = acc + got_ref[kk, rows, :].astype(F32)
            out_ref[pl.ds(pl.multiple_of(mc * rh + step * RS_ROWS, RS_ROWS), RS_ROWS), :] = acc
            return carry

        lax.fori_loop(0, steps, total, 0)

        done = pltpu.make_async_remote_copy(
            src_ref=out_ref.at[pl.ds(mc * rh, rh), :], dst_ref=out_ref.at[pl.ds(mc * rh, rh), :],
            send_sem=send_sems.at[4], recv_sem=recv_sems.at[4], device_id=sibling, device_id_type=MESH)
        done.start()
        done.wait_send()
        pltpu.make_async_remote_copy(
            src_ref=out_ref.at[pl.ds((1 - mc) * rh, rh), :], dst_ref=out_ref.at[pl.ds((1 - mc) * rh, rh), :],
            send_sem=send_sems.at[4], recv_sem=recv_sems.at[4], device_id=sibling, device_id_type=MESH).wait_recv()

    return pl.pallas_call(
        body, name=name,
        out_shape=jax.ShapeDtypeStruct((r, c), F32),
        in_specs=[pl.BlockSpec(memory_space=pltpu.VMEM)],
        out_specs=pl.BlockSpec(memory_space=pltpu.VMEM),
        scratch_shapes=[pltpu.VMEM((4, rh, c), BF16), pltpu.VMEM((4, rh, c), F32), pltpu.VMEM((3, rh, c), BF16),
                        pltpu.SemaphoreType.DMA((5,)), pltpu.SemaphoreType.DMA((5,))],
        compiler_params=pltpu.CompilerParams(vmem_limit_bytes=VMEM_LIMIT),
    )(g)


def _sum8(x, name):
    _, r, c = x.shape

    def body(x_ref, o_ref):
        acc = x_ref[0]
        for d in range(1, 8):
            acc = acc + x_ref[d]
        o_ref[...] = acc

    return pl.pallas_call(
        body, name=name, out_shape=jax.ShapeDtypeStruct((r, c), F32),
        in_specs=[pl.BlockSpec(memory_space=pltpu.VMEM)], out_specs=pl.BlockSpec(memory_space=pltpu.VMEM),
    )(x)


SCATTER_COPIES = 7


def _scatter_sems(n):
    return [pltpu.SemaphoreType.DMA((SCATTER_COPIES * n,)), pltpu.SemaphoreType.DMA((SCATTER_COPIES * n,))]


def _scatter_phase(phase, g_ref, got_ref, send_sems, recv_sems, slot):
    mx, my, mc = _place()
    rh = g_ref.shape[1] // 2
    base = SCATTER_COPIES * slot
    for d in range(1, 8):
        px = 1 - mx if d & 4 else mx
        py = 1 - my if d & 2 else my
        pc = 1 - mc if d & 1 else mc
        cp = pltpu.make_async_remote_copy(
            src_ref=g_ref.at[2 * px + py, pl.ds(pc * rh, rh), :], dst_ref=got_ref.at[d - 1],
            send_sem=send_sems.at[base + d - 1], recv_sem=recv_sems.at[base + d - 1],
            device_id=(px, py, pc), device_id_type=MESH)
        if phase == 0:
            cp.start()
        else:
            cp.wait()


def _scatter_sum(g, got, name):
    mx, my, mc = _place()
    rh, c = got.shape[1], got.shape[2]
    mine = lax.dynamic_slice(g, (2 * mx + my, mc * rh, 0), (1, rh, c))[0]
    t = _pick(rh, 256, 16)

    def body(m_ref, got_ref, o_ref):
        acc = m_ref[...].astype(F32)
        for d in range(SCATTER_COPIES):
            acc = acc + got_ref[d].astype(F32)
        o_ref[...] = acc

    return pl.pallas_call(
        body, name=name, grid=(rh // t,),
        in_specs=[pl.BlockSpec((t, c), lambda i: (i, 0)), pl.BlockSpec((SCATTER_COPIES, t, c), lambda i: (0, i, 0))],
        out_specs=pl.BlockSpec((t, c), lambda i: (i, 0)),
        out_shape=jax.ShapeDtypeStruct((rh, c), F32), compiler_params=_params(("parallel",)),
    )(mine, got)


def _scatter_finish(halves, name):
    n = len(halves)

    def body(*refs):
        h_refs, o_refs = refs[:n], refs[n:2 * n]
        send_sems, recv_sems = refs[2 * n:]
        mx, my, mc = _place()
        copies = [pltpu.make_async_remote_copy(
            src_ref=h_refs[kk], dst_ref=o_refs[kk], send_sem=send_sems.at[kk], recv_sem=recv_sems.at[kk],
            device_id=(mx, my, 1 - mc), device_id_type=MESH) for kk in range(n)]
        for cp in copies:
            cp.start()
        for cp in copies:
            cp.wait()

    hbm = pl.BlockSpec(memory_space=pl.ANY)
    theirs = pl.pallas_call(
        body, name=name, in_specs=[hbm] * n, out_specs=[hbm] * n,
        out_shape=[jax.ShapeDtypeStruct(h.shape, F32) for h in halves],
        scratch_shapes=[pltpu.SemaphoreType.DMA((n,)), pltpu.SemaphoreType.DMA((n,))],
    )(*halves)
    south = lax.axis_index("c") == 0
    return [jnp.concatenate([jnp.where(south, m, t), jnp.where(south, t, m)], axis=0) for m, t in zip(halves, theirs)]


def _adamw(w, g, m, v, name):
    r, c = w.shape
    t = _pick(r, 256, SUBLANES)
    spec = pl.BlockSpec((t, c), lambda i: (i, 0))

    def body(w_ref, g_ref, m_ref, v_ref, d_ref, nm_ref, nv_ref):
        gv = g_ref[...]
        m_new = ADAM_B1 * m_ref[...] + (1.0 - ADAM_B1) * gv
        v_new = ADAM_B2 * v_ref[...] + (1.0 - ADAM_B2) * (gv * gv)
        m_hat = m_new / (1.0 - ADAM_B1 ** ADAM_STEP)
        v_hat = v_new / (1.0 - ADAM_B2 ** ADAM_STEP)
        d_ref[...] = -ADAM_LR * (m_hat / (jnp.sqrt(v_hat) + ADAM_EPS) + ADAM_WD * w_ref[...])
        nm_ref[...] = m_new
        nv_ref[...] = v_new

    return pl.pallas_call(
        body, name=name, grid=(r // t,), in_specs=[spec] * 4, out_specs=[spec] * 3,
        out_shape=[jax.ShapeDtypeStruct((r, c), F32)] * 3, compiler_params=_params(("parallel",)),
    )(w, g, m, v)


def _pack_rows(parts):
    rows, offs, o = [], [], 0
    for p in parts:
        f = p.reshape(-1)
        n = -(-f.shape[0] // (LANES * SUBLANES)) * SUBLANES
        rows.append(jnp.pad(f, (0, n * LANES - f.shape[0])).reshape(n, LANES))
        offs.append((o, n))
        o += n
    return jnp.concatenate(rows, 0), offs


def _unpack_rows(packed, offs, shapes):
    out = []
    for (o, n), shp in zip(offs, shapes):
        size = 1
        for d in shp:
            size *= d
        out.append(packed[o:o + n].reshape(-1)[:size].reshape(shp))
    return out


def _mm_hosting(a, b, mode, out_dtype, name, gather=(), chips=None, scatter=()):
    res = _mm(a, b, mode, out_dtype, name, gather=gather, chips=chips, scatter=scatter)
    return (res[0], list(res[1:])) if (gather or scatter) else (res, [])


def _ffn_fwd(x, s, sh, g, w_in, w_out, tag, gather_in=(), gather_out=()):
    (h,) = _rowcall(lambda r, p: ([_modulate(r[0], p[0], p[1])], []), [x], [s, sh], [(x.shape[1], BF16)], [],
                    tile=512, name=tag + "_mod")
    gu, got = _mm_hosting(h, w_in, "nn", BF16, tag + "_in", gather_in, chips="b")
    if w_out is None:
        first = got.pop(0)
        w_out = first.reshape(4 * first.shape[1], first.shape[2])
    (act,) = _rowcall(lambda r, p: ([_silu(r[0].astype(F32)) * r[1].astype(F32)], []),
                      [(gu, D_FF, 0), (gu, D_FF, 1)], [], [(D_FF, BF16)], [], tile=256, name=tag + "_act")
    f, got_out = _mm_hosting(act, w_out, "nn", F32, tag + "_out", gather_out)
    got = got + got_out
    (y,) = _rowcall(lambda r, p: ([r[0] + 0.5 * p[0] * r[1]], []), [x, f], [g], [(x.shape[1], F32)], [],
                    tile=512, name=tag + "_res")
    return y, (x, h, gu, act, f), got, w_out


def _ffn_bwd(dy, saved, s, sh, g, w_in, w_out, tag, scatter_bin=(), scatter_bwin=()):
    x, h, gu, act, f = saved
    d = x.shape[1]
    df, dg = _rowcall(lambda r, p: ([0.5 * p[0] * r[0]], [0.5 * jnp.sum(r[0] * r[1], 0, keepdims=True)]),
                      [dy, f], [g], [(d, BF16)], [(1, d)], tile=512, name=tag + "_bres")
    da = _mm(df, w_out, "nt", BF16, tag + "_bout")
    dw_out = _mm(act, df, "tn", BF16, tag + "_bwout")

    def act_bwd(r, p):
        gate, up, dav = r[0].astype(F32), r[1].astype(F32), r[2].astype(F32)
        _, vjp = jax.vjp(lambda a, b: _silu(a) * b, gate, up)
        dgate, dup = vjp(dav)
        return [jnp.concatenate([dgate, dup], 1)], []

    (dgu,) = _rowcall(act_bwd, [(gu, D_FF, 0), (gu, D_FF, 1), da], [], [(2 * D_FF, BF16)], [], tile=256,
                      name=tag + "_bact")
    dh, got_a = _mm_hosting(dgu, w_in, "nt", F32, tag + "_bin", chips="b", scatter=scatter_bin)
    dw_in, got_b = _mm_hosting(h, dgu, "tn", BF16, tag + "_bwin", chips="out", scatter=scatter_bwin)

    def mod_bwd(r, p):
        _, vjp = jax.vjp(_modulate, r[0], p[0], p[1])
        dx, ds, dsh = vjp(r[1])
        return [r[2] + dx], [ds, dsh]

    dx, ds, dsh = _rowcall(mod_bwd, [x, dh, dy], [s, sh], [(d, F32)], [(1, d), (1, d)], tile=512, name=tag + "_bmod")
    return dx, (dsh, ds, dg), dw_in, dw_out, list(got_a) + list(got_b)


def _mixer_fwd(x, s, sh, g, wts, rope, gather=()):
    w_in_p, conv8, a_log, dt_bias, wn, wq, w_uq_p, wkv, w_ukv, wqn, wqr, wkn, wkr, won, w_out = wts
    cos2, sin2 = rope
    d = x.shape[1]
    (h,) = _rowcall(lambda r, p: ([_modulate(r[0], p[0], p[1])], []), [x], [s, sh], [(d, BF16)], [],
                    tile=512, name="mix_mod")
    proj = _mm(h, w_in_p, "nn", F32, "mix_in")
    qkv_c = _conv_fwd(proj, conv8, "mix_conv")
    gab = (proj, LANES, 23)

    q, k, v, gb = _rowcall(
        lambda r, p: (list(_gdn_prep_core(_split(r[0], [HEAD] * 12), r[1], p[0], p[1])), []),
        [qkv_c, gab], [a_log, dt_bias], [(512, F32)] * 3 + [(LANES, F32)], [], tile=256, name="mix_gdn_prep")
    gdn_local = _gdn_local_fwd(q, k, v, gb, "mix_gdn_local")
    o_gdn, gdn_states = _gdn_scan_fwd(*gdn_local, "mix_gdn_scan")
    states = (gdn_local, gdn_states)

    cq, ckv, kr = (proj, 512, 4), (proj, 256, 10), (proj, LANES, 22)
    cqn, ckvn, k_rope = _rowcall(
        lambda r, p: (list(_mla_prep_core(r[0][:, :MLA_Q_LORA], r[1], r[2], r[3], r[4], p[0], p[1], p[2])), []),
        [cq, ckv, kr, cos2, sin2], [wq, wkv, wkr], [(MLA_Q_LORA, BF16), (MLA_KV_LORA, BF16), (LANES, F32)], [],
        tile=512, name="mix_mla_prep")
    qf = _mm(cqn, w_uq_p, "nn", F32, "mix_uq")
    kvf = _mm(ckvn, w_ukv, "nn", F32, "mix_ukv")

    def qk_prep(r, p):
        qparts = _split(r[0], [HEAD] * 8)
        kvparts = _split(r[1], [HEAD] * 8)
        qs, ks, vs = _qk_prep_core(qparts[:4], qparts[4:], kvparts[0::2], kvparts[1::2], r[2], r[3], r[4],
                                   p[0], p[1], p[2])
        return [jnp.concatenate(qs, 1), jnp.concatenate(ks, 1), jnp.concatenate(vs, 1)], []

    qa, ka, va = _rowcall(qk_prep, [qf, kvf, k_rope, cos2, sin2], [wqn, wqr, wkn],
                          [(4 * QK_PAD, BF16), (4 * QK_PAD, BF16), (4 * HEAD, BF16)], [], tile=256,
                          name="mix_qk_prep")
    o_b, lse, got = _attn_fwd(qa, ka, va, "mix_attn", gather=gather)
    if w_out is None:
        first = got.pop(0)
        w_out = first.reshape(4 * first.shape[1], first.shape[2])

    gz = (proj, 512, 3)
    (mixed,) = _rowcall(
        lambda r, p: ([_mix_post_core(_split(r[0], HW4), _split(r[1], HW4), _split(r[2], HW4), p[0], p[1])], []),
        [o_gdn, gz, o_b], [wn, won], [(2 * 512, BF16)], [], tile=512, name="mix_post")
    y = _mm(mixed, w_out, "nn", F32, "mix_out")
    (x_out,) = _rowcall(lambda r, p: ([r[0] + p[0] * r[1]], []), [x, y], [g], [(d, F32)], [], tile=512,
                        name="mix_res")
    saved = (x, h, proj, qkv_c, q, k, v, gb, states, o_gdn, cqn, ckvn, k_rope, qf, kvf, qa, ka, va, o_b, lse,
             mixed, y)
    return x_out, saved, got, w_out


def _mixer_bwd(dy, saved, s, sh, g, wts, rope, scatter=()):
    w_in_p, conv8, a_log, dt_bias, wn, wq, w_uq_p, wkv, w_ukv, wqn, wqr, wkn, wkr, won, w_out = wts
    cos2, sin2 = rope
    (x, h, proj, qkv_c, q, k, v, gb, states, o_gdn, cqn, ckvn, k_rope, qf, kvf, qa, ka, va, o_b, lse,
     mixed, y) = saved
    d = x.shape[1]
    dyb, dg = _rowcall(lambda r, p: ([p[0] * r[0]], [jnp.sum(r[0] * r[1], 0, keepdims=True)]),
                       [dy, y], [g], [(d, BF16)], [(1, d)], tile=512, name="mix_bres")
    dmixed = _mm(dyb, w_out, "nt", F32, "mix_bout")
    dw_out = _mm(mixed, dyb, "tn", BF16, "mix_bwout")

    gz = (proj, 512, 3)

    def post_bwd(r, p):
        _, vjp = jax.vjp(_mix_post_core, _split(r[0], HW4), _split(r[1], HW4), _split(r[2], HW4), p[0], p[1])
        do, dz, dob, dwn, dwon = vjp(r[3])
        return [jnp.concatenate(do, 1), jnp.concatenate(dz, 1), jnp.concatenate(dob, 1)], [dwn, dwon]

    do_gdn, dgz, do_b, dwn, dwon = _rowcall(post_bwd, [o_gdn, gz, o_b, dmixed], [wn, won], [(512, F32)] * 3,
                                            [(1, HEAD), (1, HEAD)], tile=256, name="mix_bpost")

    dqa, dka, dva, got = _attn_bwd(qa, ka, va, o_b, do_b, lse, "mix_battn", scatter=scatter)

    def qk_bwd(r, p):
        qparts = _split(r[0], [HEAD] * 8)
        kvparts = _split(r[1], [HEAD] * 8)
        _, vjp = jax.vjp(_qk_prep_core, qparts[:4], qparts[4:], kvparts[0::2], kvparts[1::2], r[2], r[3], r[4],
                         p[0], p[1], p[2])
        cot = (_split(r[5], [QK_PAD] * 4), _split(r[6], [QK_PAD] * 4), _split(r[7], HW4))
        dqn, dqr, dkn, dvp, dkrope, _, _, dwqn, dwqr, dwkn = vjp(cot)
        dkv = []
        for a, b in zip(dkn, dvp):
            dkv += [a, b]
        return [jnp.concatenate(list(dqn) + list(dqr), 1), jnp.concatenate(dkv, 1), dkrope], [dwqn, dwqr, dwkn]

    dqf, dkvf, dk_rope, dwqn, dwqr, dwkn = _rowcall(
        qk_bwd, [qf, kvf, k_rope, cos2, sin2, dqa, dka, dva], [wqn, wqr, wkn],
        [(8 * HEAD, BF16), (8 * HEAD, BF16), (LANES, F32)], [(1, HEAD)] * 3, tile=256, name="mix_bqk_prep")
    dcqn = _mm(dqf, w_uq_p, "nt", F32, "mix_buq")
    dw_uq_p = _mm(cqn, dqf, "tn", F32, "mix_bwuq")
    dckvn = _mm(dkvf, w_ukv, "nt", F32, "mix_bukv")
    dw_ukv = _mm(ckvn, dkvf, "tn", F32, "mix_bwukv")

    cq, ckv, kr = (proj, 512, 4), (proj, 256, 10), (proj, LANES, 22)

    def mla_bwd(r, p):
        _, vjp = jax.vjp(_mla_prep_core, r[0][:, :MLA_Q_LORA], r[1], r[2], r[3], r[4], p[0], p[1], p[2])
        dcq, dckv, dkr, _, _, dwq, dwkv, dwkr = vjp((r[5], r[6], r[7]))
        pad = jnp.zeros((dcq.shape[0], 512 - MLA_Q_LORA), F32)
        return [jnp.concatenate([dcq, pad], 1), dckv, dkr], [dwq, dwkv, dwkr]

    dcq, dckv, dkr, dwq, dwkv, dwkr = _rowcall(
        mla_bwd, [cq, ckv, kr, cos2, sin2, dcqn, dckvn, dk_rope], [wq, wkv, wkr],
        [(512, F32), (MLA_KV_LORA, F32), (LANES, F32)], [(1, MLA_Q_LORA), (1, MLA_KV_LORA), (1, LANES)],
        tile=512, name="mix_bmla_prep")

    gdn_local, gdn_states = states
    d_local = _gdn_scan_bwd(*gdn_local, gdn_states, do_gdn, "mix_bgdn_scan")
    dq, dk, dv, dgb = _gdn_local_bwd(q, k, v, gb, *d_local, "mix_bgdn_local")
    gab = (proj, LANES, 23)

    def gdn_prep_bwd(r, p):
        _, vjp = jax.vjp(_gdn_prep_core, _split(r[0], [HEAD] * 12), r[1], p[0], p[1])
        dparts, dgab, da_log, ddt = vjp((r[2], r[3], r[4], r[5]))
        return [jnp.concatenate(dparts, 1), dgab], [da_log, ddt]

    dqkv_c, dgab, da_log, ddt = _rowcall(gdn_prep_bwd, [qkv_c, gab, dq, dk, dv, dgb], [a_log, dt_bias],
                                         [(1536, F32), (LANES, F32)], [(1, LANES), (1, LANES)], tile=256,
                                         name="mix_bgdn_prep")
    dqkv_pre, dconv8 = _conv_bwd(proj, dqkv_c, conv8, "mix_bconv")

    dproj = jnp.concatenate([dqkv_pre.astype(BF16), dgz.astype(BF16), dcq.astype(BF16), dckv.astype(BF16),
                             dkr.astype(BF16), dgab.astype(BF16)], axis=1)
    dh = _mm(dproj, w_in_p, "nt", F32, "mix_bin")
    dw_in_p = _mm(h, dproj, "tn", F32, "mix_bwin")

    def mod_bwd(r, p):
        _, vjp = jax.vjp(_modulate, r[0], p[0], p[1])
        dx, ds, dsh = vjp(r[1])
        return [r[2] + dx], [ds, dsh]

    dx, ds, dsh = _rowcall(mod_bwd, [x, dh, dy], [s, sh], [(d, F32)], [(1, d), (1, d)], tile=512, name="mix_bmod")
    small = dict(conv=dconv8, a_log=da_log, dt=ddt, wn=dwn, wq=dwq, wkv=dwkv, wqn=dwqn, wqr=dwqr, wkn=dwkn,
                 wkr=dwkr, won=dwon)
    return dx, (dsh, ds, dg), dw_in_p, dw_uq_p, dw_ukv, dw_out, small, got


def _pad_cols(a, n):
    return jnp.pad(a, ((0, 0),) * (a.ndim - 1) + ((0, n - a.shape[-1]),))


def _pack_w_in(w):
    z = lambda n: jnp.zeros((w.shape[0], n), w.dtype)
    return jnp.concatenate([w[:, 0:2048], w[:, 2056:2440], z(128), w[:, 2440:2696], w[:, 2696:2760], z(64),
                            w[:, 2048:2056], z(120)], axis=1)


def _unpack_w_in(wp):
    return jnp.concatenate([wp[:, 0:2048], wp[:, 2944:2952], wp[:, 2048:2432], wp[:, 2560:2816], wp[:, 2816:2880]],
                           axis=1)


def _pack_w_uq(w):
    z = jnp.zeros((w.shape[0], LANES - MLA_ROPE), w.dtype)
    nope = [w[:, h * 192:h * 192 + HEAD] for h in range(MLA_HEADS)]
    rope = []
    for h in range(MLA_HEADS):
        rope += [w[:, h * 192 + HEAD:(h + 1) * 192], z]
    return jnp.concatenate(nope + rope, axis=1)


def _unpack_w_uq(wp):
    cols = []
    for h in range(MLA_HEADS):
        cols += [wp[:, h * HEAD:(h + 1) * HEAD], wp[:, 512 + h * LANES:512 + h * LANES + MLA_ROPE]]
    return jnp.concatenate(cols, axis=1)


def _cols_to_chips(a):
    r, c = a.shape
    return a.reshape(r, 4, c // 4).transpose(1, 0, 2)


def _chips_to_cols(a):
    _, r, n = a.shape
    return a.transpose(1, 0, 2).reshape(r, 4 * n)


def _pad128(v, n=LANES):
    return _pad_cols(v.reshape(1, -1), n)


def kernel(x, c, positions, w_ada, b_ada, ffn1_w_in, ffn1_w_out, w_in, gdn_conv_w, gdn_a_log, gdn_dt_bias, gdn_norm_w, mla_q_norm_w, mla_w_uq, mla_kv_norm_w, mla_w_ukv, qkn_q_nope, qkn_q_rope, qkn_k_nope, qkn_k_rope, mla_out_norm_w, w_out, ffn2_w_in, ffn2_w_out, loss_target, m_w_ada, m_b_ada, m_ffn1_w_in, m_ffn1_w_out, m_w_in, m_gdn_conv_w, m_gdn_a_log, m_gdn_dt_bias, m_gdn_norm_w, m_mla_q_norm_w, m_mla_w_uq, m_mla_kv_norm_w, m_mla_w_ukv, m_qkn_q_nope, m_qkn_q_rope, m_qkn_k_nope, m_qkn_k_rope, m_mla_out_norm_w, m_w_out, m_ffn2_w_in, m_ffn2_w_out, v_w_ada, v_b_ada, v_ffn1_w_in, v_ffn1_w_out, v_w_in, v_gdn_conv_w, v_gdn_a_log, v_gdn_dt_bias, v_gdn_norm_w, v_mla_q_norm_w, v_mla_w_uq, v_mla_kv_norm_w, v_mla_w_ukv, v_qkn_q_nope, v_qkn_q_rope, v_qkn_k_nope, v_qkn_k_rope, v_mla_out_norm_w, v_w_out, v_ffn2_w_in, v_ffn2_w_out):
    weights = dict(w_ada=w_ada, b_ada=b_ada, ffn1_w_in=ffn1_w_in, ffn1_w_out=ffn1_w_out, w_in=w_in,
                   gdn_conv_w=gdn_conv_w, gdn_a_log=gdn_a_log, gdn_dt_bias=gdn_dt_bias, gdn_norm_w=gdn_norm_w,
                   mla_q_norm_w=mla_q_norm_w, mla_w_uq=mla_w_uq, mla_kv_norm_w=mla_kv_norm_w, mla_w_ukv=mla_w_ukv,
                   qkn_q_nope=qkn_q_nope, qkn_q_rope=qkn_q_rope, qkn_k_nope=qkn_k_nope, qkn_k_rope=qkn_k_rope,
                   mla_out_norm_w=mla_out_norm_w, w_out=w_out, ffn2_w_in=ffn2_w_in, ffn2_w_out=ffn2_w_out)
    moms_m = dict(w_ada=m_w_ada, b_ada=m_b_ada, ffn1_w_in=m_ffn1_w_in, ffn1_w_out=m_ffn1_w_out, w_in=m_w_in,
                  gdn_conv_w=m_gdn_conv_w, gdn_a_log=m_gdn_a_log, gdn_dt_bias=m_gdn_dt_bias,
                  gdn_norm_w=m_gdn_norm_w, mla_q_norm_w=m_mla_q_norm_w, mla_w_uq=m_mla_w_uq,
                  mla_kv_norm_w=m_mla_kv_norm_w, mla_w_ukv=m_mla_w_ukv, qkn_q_nope=m_qkn_q_nope,
                  qkn_q_rope=m_qkn_q_rope, qkn_k_nope=m_qkn_k_nope, qkn_k_rope=m_qkn_k_rope,
                  mla_out_norm_w=m_mla_out_norm_w, w_out=m_w_out, ffn2_w_in=m_ffn2_w_in, ffn2_w_out=m_ffn2_w_out)
    moms_v = dict(w_ada=v_w_ada, b_ada=v_b_ada, ffn1_w_in=v_ffn1_w_in, ffn1_w_out=v_ffn1_w_out, w_in=v_w_in,
                  gdn_conv_w=v_gdn_conv_w, gdn_a_log=v_gdn_a_log, gdn_dt_bias=v_gdn_dt_bias,
                  gdn_norm_w=v_gdn_norm_w, mla_q_norm_w=v_mla_q_norm_w, mla_w_uq=v_mla_w_uq,
                  mla_kv_norm_w=v_mla_kv_norm_w, mla_w_ukv=v_mla_w_ukv, qkn_q_nope=v_qkn_q_nope,
                  qkn_q_rope=v_qkn_q_rope, qkn_k_nope=v_qkn_k_nope, qkn_k_rope=v_qkn_k_rope,
                  mla_out_norm_w=v_mla_out_norm_w, w_out=v_w_out, ffn2_w_in=v_ffn2_w_in, ffn2_w_out=v_ffn2_w_out)
    names = list(weights)

    seq, d = x.shape[1], x.shape[2]
    x2d = x.reshape(seq, d)
    tgt = loss_target.reshape(seq, d)
    mx, my, mc = _place()
    chip = 2 * mx + my
    me = 2 * chip + mc
    n_mod = b_ada.shape[1] // d
    shard = w_ada.shape[2]

    half = MLA_ROPE // 2
    inv_freq = 10000.0 ** (-jnp.arange(half, dtype=F32) / half)
    ang = positions.astype(F32).reshape(seq, 1) * inv_freq
    cosv, sinv = jnp.cos(ang), jnp.sin(ang)
    cos2 = _pad_cols(jnp.concatenate([cosv, cosv], 1), LANES)
    sin2 = _pad_cols(jnp.concatenate([-sinv, sinv], 1), LANES)
    rope = (cos2, sin2)

    c_all = _allgather8(jnp.pad(c, ((0, SUBLANES - 1), (0, 0))), "gather_c")[:, 0, :]
    (sc_all,) = _rowcall(lambda r, p: ([_silu(r[0])], []), [c_all], [], [(d, F32)], [], tile=8, name="ada_silu")
    mod_part = _mm(sc_all, w_ada[0], "nn", F32, "ada_mm", hi=True)
    mod_all = _allgather8(mod_part, "gather_mod")
    mod_rows = lax.dynamic_index_in_dim(mod_all, me, axis=1, keepdims=False)
    mod_raw = jnp.concatenate([mod_rows[2 * jj] for jj in range(4)], axis=0).reshape(1, 4 * shard)
    (mod,) = _rowcall(lambda r, p: ([r[0] + r[1]], []),
                      [jnp.pad(mod_raw, ((0, 7), (0, 0))), jnp.pad(b_ada, ((0, 7), (0, 0)))], [],
                      [(4 * shard, F32)], [], tile=8, name="ada_bias")
    mods = [mod[0:1, i * d:(i + 1) * d] for i in range(n_mod)]
    sh1, s1, g1, sh2, s2, g2, sh3, s3, g3 = mods

    def shard_bf16(w, pad_to=None):
        w2 = w[0].astype(BF16)
        return _pad_cols(w2, pad_to) if pad_to else w2

    def cols_of(got, w):
        return _chips_to_cols(got[:, :, :w.shape[2]])

    def rows_of(got):
        return got.reshape(4 * got.shape[1], got.shape[2])

    f1_in = _allgather_chips(shard_bf16(ffn1_w_in), "gather_f1_in")
    conv_all = _allgather8(jnp.pad(gdn_conv_w[0], ((0, SUBLANES - CONV_K), (0, 0))), "gather_conv")
    conv8 = jnp.concatenate([conv_all[2 * jj] for jj in range(4)], axis=1)

    x1, sv1, got, f1_out = _ffn_fwd(
        x2d, s1, sh1, g1, f1_in, None, "ffn1",
        gather_in=[shard_bf16(ffn1_w_out), shard_bf16(w_in, 768), shard_bf16(mla_w_uq, 256), shard_bf16(mla_w_ukv)])
    w_in_full, w_uq_full, w_ukv_full = cols_of(got[0], w_in), cols_of(got[1], mla_w_uq), cols_of(got[2], mla_w_ukv)
    wts = (_pack_w_in(w_in_full), conv8, _pad128(gdn_a_log), _pad128(gdn_dt_bias), gdn_norm_w,
           mla_q_norm_w, _pack_w_uq(w_uq_full), mla_kv_norm_w, w_ukv_full, qkn_q_nope, _pad128(qkn_q_rope),
           qkn_k_nope, _pad128(qkn_k_rope), mla_out_norm_w, None)
    xm, svm, got, w_out_full = _mixer_fwd(
        x1, s2, sh2, g2, wts, rope, gather=[shard_bf16(w_out), shard_bf16(ffn2_w_in), shard_bf16(ffn2_w_out)])
    wts = wts[:-1] + (w_out_full,)
    f2_in, f2_out = got[0], rows_of(got[1])
    x3, sv3, _, _ = _ffn_fwd(xm, s3, sh3, g3, f2_in, f2_out, "ffn2")

    def loss_fn(r, p):
        err = r[0] - r[1]
        part = 0.5 * jnp.sum(jnp.sum(err * err, axis=1, keepdims=True) * (1.0 / d), axis=0, keepdims=True)
        return [err * (1.0 / d)], [jnp.broadcast_to(part, (1, LANES))]

    dy, loss_part = _rowcall(loss_fn, [x3, tgt], [], [(d, F32)], [(1, LANES)], tile=512, name="loss")
    loss = lax.psum(loss_part[0, 0], ("x", "y", "c"))

    def chip_cols(dw, pad_to=None):
        g4 = _cols_to_chips(dw).astype(BF16)
        return _pad_cols(g4, pad_to) if pad_to else g4

    def chip_rows(dw):
        return dw.astype(BF16).reshape(4, dw.shape[0] // 4, dw.shape[1])

    dxm, dmod3, dw_f2_in, dw_f2_out, _ = _ffn_bwd(dy, sv3, s3, sh3, g3, f2_in, f2_out, "ffn2")
    parts2 = [dw_f2_in, chip_rows(dw_f2_out)]
    dx1, dmod2, dw_in_p, dw_uq_p, dw_ukv, dw_out_m, small, got2 = _mixer_bwd(dxm, svm, s2, sh2, g2, wts, rope,
                                                                             scatter=parts2)
    parts_m = [chip_cols(_unpack_w_in(dw_in_p), 768), chip_cols(_unpack_w_uq(dw_uq_p), 256), chip_cols(dw_ukv),
               chip_rows(dw_out_m)]
    dx0, dmod1, dw_f1_in, dw_f1_out, got_m = _ffn_bwd(dx1, sv1, s1, sh1, g1, f1_in, f1_out, "ffn1",
                                                      scatter_bin=parts_m[:1], scatter_bwin=parts_m[1:])
    grad_x = dx0.reshape(x.shape)

    dmod = jnp.concatenate(list(dmod1) + list(dmod2) + list(dmod3), axis=1)
    small_parts = [dmod, small["conv"][:CONV_K], small["a_log"], small["dt"], small["wn"], small["wq"],
                   small["wkv"], small["wqn"], small["wqr"], small["wkn"], small["wkr"], small["won"]]
    packed, offs = _pack_rows(small_parts)
    gathered = _allgather8(packed, "gather_small")
    total = _sum8(gathered, "sum_small")
    (g_b_ada, g_conv, g_a_log, g_dt, g_wn, g_wq, g_wkv, g_wqn, g_wqr, g_wkn, g_wkr, g_won) = _unpack_rows(
        total, offs, [p.shape for p in small_parts])
    dmod_all = _unpack_rows(gathered.reshape(-1, LANES),
                            [(dd * packed.shape[0] + offs[0][0], offs[0][1]) for dd in range(8)],
                            [dmod.shape] * 8)
    dmod_all = jnp.concatenate(dmod_all, axis=0)
    dmod_mine = lax.dynamic_slice_in_dim(dmod_all, chip * shard, shard, axis=1)

    def ada_grad(r, p):
        acc = jnp.zeros((r[0].shape[0], shard), F32)
        for b in range(8):
            acc = acc + r[0][:, b:b + 1] * p[0][b:b + 1, :]
        return [acc], []

    (g_w_ada,) = _rowcall(ada_grad, [_pad_cols(sc_all.T, LANES)], [dmod_mine], [(shard, F32)], [], tile=256,
                          name="ada_grad")

    grads = dict(
        w_ada=g_w_ada[None], b_ada=g_b_ada,
        gdn_conv_w=lax.dynamic_slice_in_dim(g_conv, chip * gdn_conv_w.shape[2], gdn_conv_w.shape[2], axis=1)[None],
        gdn_a_log=g_a_log[:, :GDN_HEADS], gdn_dt_bias=g_dt[:, :GDN_HEADS], gdn_norm_w=g_wn, mla_q_norm_w=g_wq,
        mla_kv_norm_w=g_wkv, qkn_q_nope=g_wqn, qkn_q_rope=g_wqr[:, :MLA_ROPE], qkn_k_nope=g_wkn,
        qkn_k_rope=g_wkr[:, :MLA_ROPE], mla_out_norm_w=g_won)

    hosted = ["ffn2_w_in", "ffn2_w_out", "w_in", "mla_w_uq", "mla_w_ukv", "w_out"]
    halves = [_scatter_sum(part, got, "rs_sum_" + nme)
              for nme, part, got in zip(hosted, parts2 + parts_m, got2 + got_m)]
    for nme, full in zip(hosted, _scatter_finish(halves, "rs_finish")):
        grads[nme] = full[:, :weights[nme].shape[2]][None]
    grads["ffn1_w_in"] = _reduce_scatter_chips(dw_f1_in, "rs_f1_in")[None]
    grads["ffn1_w_out"] = _reduce_scatter_chips(chip_rows(dw_f1_out), "rs_f1_out")[None]

    big = ["w_ada", "ffn1_w_in", "ffn1_w_out", "w_in", "mla_w_uq", "mla_w_ukv", "w_out", "ffn2_w_in", "ffn2_w_out"]
    delta, new_m, new_v = {}, {}, {}
    for nme in big:
        shp = weights[nme].shape
        dl, nm, nv = _adamw(weights[nme][0], grads[nme][0], moms_m[nme][0], moms_v[nme][0], "adamw_" + nme)
        delta[nme], new_m[nme], new_v[nme] = dl.reshape(shp), nm.reshape(shp), nv.reshape(shp)
    tiny = [nme for nme in names if nme not in big]
    shapes = [weights[nme].shape for nme in tiny]
    pw, poffs = _pack_rows([weights[nme] for nme in tiny])
    pg, _ = _pack_rows([grads[nme] for nme in tiny])
    pm, _ = _pack_rows([moms_m[nme] for nme in tiny])
    pv, _ = _pack_rows([moms_v[nme] for nme in tiny])
    pd, pnm, pnv = _adamw(pw, pg, pm, pv, "adamw_small")
    for nme, dl, nm, nv in zip(tiny, _unpack_rows(pd, poffs, shapes), _unpack_rows(pnm, poffs, shapes),
                               _unpack_rows(pnv, poffs, shapes)):
        delta[nme], new_m[nme], new_v[nme] = dl, nm, nv

    return (loss, grad_x, *[grads[nme].reshape(weights[nme].shape) for nme in names],
            *[delta[nme] for nme in names], *[new_m[nme] for nme in names], *[new_v[nme] for nme in names])
```

```python
import functools

import jax
import jax.numpy as jnp
from jax import lax
from jax.experimental import pallas as pl
from jax.experimental.pallas import tpu as pltpu

F32 = jnp.float32
BF16 = jnp.bfloat16
HI = lax.Precision.HIGHEST
MESH = pl.DeviceIdType.MESH

EPS = 1e-6
CHUNK = 64
D_FF = 2816
GDN_HEADS = 4
HEAD = 128
MLA_HEADS = 4
MLA_ROPE = 64
MLA_Q_LORA = 384
MLA_KV_LORA = 256
QK_PAD = 256
ATT_SCALE = (HEAD + MLA_ROPE) ** -0.5
N_PROJ = 3072

ADAM_LR, ADAM_B1, ADAM_B2, ADAM_EPS, ADAM_WD, ADAM_STEP = 0.001, 0.9, 0.999, 1e-08, 0.01, 10

LANES = 128
SUBLANES = 8
VMEM_LIMIT = 56 * 2 ** 20


def _params(sem=None):
    return pltpu.CompilerParams(dimension_semantics=sem, vmem_limit_bytes=VMEM_LIMIT)


def _pick(n, cap, align):
    best = None
    d = align
    while d <= min(n, cap):
        if n % d == 0:
            best = d
        d += align
    return best if best is not None else n


def _iota(shape, dim):
    return lax.broadcasted_iota(jnp.int32, shape, dim)


def _rowcall(fn, rows, params, out_rows, out_accs, *, tile, name):
    rows = [r if isinstance(r, tuple) else (r, r.shape[1], 0) for r in rows]
    s = rows[0][0].shape[-2]
    t = min(tile, s)
    n = s // t
    n_in = len(rows) + len(params)
    n_row_out = len(out_rows)

    in_specs = []
    for r in rows:
        if len(r) == 3:
            in_specs.append(pl.BlockSpec((t, r[1]), functools.partial(lambda i, b: (i, b), b=r[2])))
        else:
            in_specs.append(pl.BlockSpec((None, t, r[1]), functools.partial(lambda i, b, h: (h, i, b), b=r[2], h=r[3])))
    in_specs += [pl.BlockSpec(p.shape, lambda i: (0, 0)) for p in params]
    out_shape, out_specs = [], []
    for o in out_rows:
        if len(o) == 2:
            out_shape.append(jax.ShapeDtypeStruct((s, o[0]), o[1]))
            out_specs.append(pl.BlockSpec((t, o[0]), lambda i: (i, 0)))
        else:
            out_shape.append(jax.ShapeDtypeStruct((o[2], s, o[0]), o[1]))
            out_specs.append(pl.BlockSpec((o[2], t, o[0]), lambda i: (0, i, 0)))
    out_shape += [jax.ShapeDtypeStruct(shape, F32) for shape in out_accs]
    out_specs += [pl.BlockSpec(shape, lambda i: (0, 0)) for shape in out_accs]

    def body(*refs):
        ins = refs[:n_in]
        outs = refs[n_in:]
        i = pl.program_id(0)
        vals = [r[...] for r in ins]
        row_outs, acc_outs = fn(vals[:len(rows)], vals[len(rows):])
        for r, v in zip(outs[:n_row_out], row_outs):
            if isinstance(v, (list, tuple)):
                for hh, piece in enumerate(v):
                    r[hh] = piece.astype(r.dtype)
            else:
                r[...] = v.astype(r.dtype)
        if out_accs:
            @pl.when(i == 0)
            def _():
                for r in outs[n_row_out:]:
                    r[...] = jnp.zeros(r.shape, F32)
            for r, v in zip(outs[n_row_out:], acc_outs):
                r[...] += v

    res = pl.pallas_call(
        body, name=name, grid=(n,), in_specs=in_specs, out_specs=out_specs, out_shape=out_shape,
        compiler_params=_params(("arbitrary",) if out_accs else ("parallel",)),
    )(*[r[0] for r in rows], *params)
    return list(res)


MM_TILE_MN = 1536


class _Hosted:
    def __init__(self, gather=(), scatter=()):
        self.gather, self.scatter = list(gather), list(scatter)
        self.operands = self.gather + self.scatter
        self.n = len(self.operands)

    def specs(self):
        return [pl.BlockSpec(memory_space=pl.ANY)] * self.n

    def out_shapes(self):
        return ([jax.ShapeDtypeStruct((4,) + x.shape, x.dtype) for x in self.gather]
                + [jax.ShapeDtypeStruct((SCATTER_COPIES, g.shape[1] // 2, g.shape[2]), g.dtype) for g in self.scatter])

    def scratch(self):
        return ((_gather_sems(len(self.gather)) if self.gather else [])
                + (_scatter_sems(len(self.scatter)) if self.scatter else []))

    def _phase(self, ph, ins, outs, sems):
        ng = len(self.gather)
        g_sems, s_sems = (sems[:3], sems[3:]) if ng else ((), sems)
        for slot in range(ng):
            _gather_phase(ph, ins[slot], outs[slot], *g_sems, slot)
        if ph != 1:
            for slot in range(len(self.scatter)):
                _scatter_phase(0 if ph == 0 else 1, ins[ng + slot], outs[ng + slot], *s_sems, slot)

    def open(self, step, steps, ins, outs, sems):
        if self.n:
            pl.when(step == 0)(lambda: self._phase(0, ins, outs, sems))
            pl.when(step == steps // 2)(lambda: self._phase(1, ins, outs, sems))

    def close(self, step, steps, ins, outs, sems):
        if self.n:
            pl.when(step == steps - 1)(lambda: self._phase(2, ins, outs, sems))


def _mm(a, b, mode, out_dtype, name, hi=False, gather=(), chips=None, scatter=()):
    b_shape = b.shape
    if chips == "b":
        b_shape = (b.shape[1], 4 * b.shape[2])
    if mode == "nn":
        (m, k), (_, n) = a.shape, b_shape
        dims = (((1,), (0,)), ((), ()))
    elif mode == "nt":
        (m, k), (n, _) = a.shape, b_shape
        dims = (((1,), (1,)), ((), ()))
    else:
        (k, m), (_, n) = a.shape, b_shape
        dims = (((0,), (0,)), ((), ()))
    tm = _pick(m, MM_TILE_MN if mode == "tn" else 1024, LANES if mode == "tn" else 16)
    tn = _pick(n // 4 if chips and mode != "nt" else n, MM_TILE_MN, LANES)
    tk = _pick(k // 4 if chips and mode == "nt" else k, 1024 if mode == "tn" else MM_TILE_MN, LANES)
    nk = k // tk
    nb = (n // 4) // tn
    kb = (k // 4) // tk
    if mode == "nn":
        a_spec = pl.BlockSpec((tm, tk), lambda i, j, kk: (i, kk))
        b_spec = pl.BlockSpec((tk, tn), lambda i, j, kk: (kk, j))
        if chips == "b":
            b_spec = pl.BlockSpec((None, tk, tn), lambda i, j, kk: (j // nb, kk, j % nb))
    elif mode == "nt":
        a_spec = pl.BlockSpec((tm, tk), lambda i, j, kk: (i, kk))
        b_spec = pl.BlockSpec((tn, tk), lambda i, j, kk: (j, kk))
        if chips == "b":
            b_spec = pl.BlockSpec((None, tn, tk), lambda i, j, kk: (kk // kb, j, kk % kb))
    else:
        a_spec = pl.BlockSpec((tk, tm), lambda i, j, kk: (kk, i))
        b_spec = pl.BlockSpec((tk, tn), lambda i, j, kk: (kk, j))
    out_spec = pl.BlockSpec((tm, tn), lambda i, j, kk: (i, j))
    out_shape = jax.ShapeDtypeStruct((m, n), out_dtype)
    if chips == "out":
        out_spec = pl.BlockSpec((None, tm, tn), lambda i, j, kk: (j // nb, i, j % nb))
        out_shape = jax.ShapeDtypeStruct((4, m, n // 4), out_dtype)

    host = _Hosted(gather, scatter)
    ng = host.n
    grid = (m // tm, n // tn, nk)
    steps = grid[0] * grid[1] * grid[2]

    def body(*refs):
        a_ref, b_ref = refs[:2]
        x_refs = refs[2:2 + ng]
        o_ref = refs[2 + ng]
        got_refs = refs[3 + ng:3 + 2 * ng]
        acc_ref = refs[3 + 2 * ng]
        sems = refs[4 + 2 * ng:]
        kk = pl.program_id(2)
        step = (pl.program_id(0) * grid[1] + pl.program_id(1)) * nk + kk
        host.open(step, steps, x_refs, got_refs, sems)

        @pl.when(kk == 0)
        def _():
            acc_ref[...] = jnp.zeros(acc_ref.shape, F32)

        av, bv = a_ref[...], b_ref[...]
        if hi:
            acc_ref[...] += lax.dot_general(av, bv, dims, precision=HI, preferred_element_type=F32)
        else:
            acc_ref[...] += lax.dot_general(av.astype(BF16), bv.astype(BF16), dims,
                                            preferred_element_type=F32)

        @pl.when(kk == nk - 1)
        def _():
            o_ref[...] = acc_ref[...].astype(o_ref.dtype)

        host.close(step, steps, x_refs, got_refs, sems)

    res = pl.pallas_call(
        body, name=name, grid=grid,
        in_specs=[a_spec, b_spec] + host.specs(),
        out_specs=[out_spec] + host.specs(),
        out_shape=[out_shape] + host.out_shapes(),
        scratch_shapes=[pltpu.VMEM((tm, tn), F32)] + host.scratch(),
        compiler_params=_params(("arbitrary",) * 3 if ng else ("parallel", "parallel", "arbitrary")),
    )(a, b, *host.operands)
    return res if ng else res[0]


def _rms(x, w=None, n=None):
    n = x.shape[-1] if n is None else n
    y = x * lax.rsqrt(jnp.sum(x * x, axis=-1, keepdims=True) * (1.0 / n) + EPS)
    return y if w is None else y * w


def _silu(x):
    return x * jax.nn.sigmoid(x)


def _softplus(x):
    return jnp.maximum(x, 0.0) + jnp.log1p(jnp.exp(-jnp.abs(x)))


def _split(x, widths):
    out, o = [], 0
    for w in widths:
        out.append(x[:, o:o + w])
        o += w
    return out


def _modulate(x, s, sh):
    return _rms(x) * (1.0 + s) + sh


def _rope_rot(x):
    r, c = _iota((LANES, LANES), 0), _iota((LANES, LANES), 1)
    half = MLA_ROPE // 2
    perm = (((r < half) & (c == r + half)) | ((r >= half) & (r < MLA_ROPE) & (c == r - half))).astype(F32)
    return jnp.dot(x, perm, precision=HI, preferred_element_type=F32)


def _rope(x, cos2, sin2):
    return x * cos2 + _rope_rot(x) * sin2


def _gdn_prep_core(qkv_parts, gab, a_log, dt_bias):
    act = [_silu(p) for p in qkv_parts]
    qs = [p * lax.rsqrt(jnp.sum(p * p, -1, keepdims=True) + EPS) * (HEAD ** -0.5) for p in act[:4]]
    ks = [p * lax.rsqrt(jnp.sum(p * p, -1, keepdims=True) + EPS) for p in act[4:8]]
    lane = _iota(gab.shape, 1)
    g = -jnp.exp(a_log) * _softplus(gab + dt_bias)
    beta = jax.nn.sigmoid(gab)
    gb = jnp.where(lane < GDN_HEADS, g, jnp.where(lane < 2 * GDN_HEADS, beta, 0.0))
    return (jnp.concatenate(qs, 1), jnp.concatenate(ks, 1), jnp.concatenate(act[8:], 1), gb)


def _mla_prep_core(cq, ckv, kr, cos2, sin2, wq, wkv, wkr):
    cqn = _rms(cq, wq)
    ckvn = _rms(ckv, wkv)
    k_rope = _rope(_rms(kr, wkr, MLA_ROPE), cos2, sin2)
    return cqn, ckvn, k_rope


def _qk_prep_core(qn_parts, qr_parts, kn_parts, v_parts, k_rope, cos2, sin2, wqn, wqr, wkn):
    qs, ks = [], []
    for h in range(MLA_HEADS):
        qn = _rms(qn_parts[h], wqn) * ATT_SCALE
        qr = _rope(_rms(qr_parts[h], wqr, MLA_ROPE), cos2, sin2) * ATT_SCALE
        qs.append(jnp.concatenate([qn, qr], 1))
        ks.append(jnp.concatenate([_rms(kn_parts[h], wkn), k_rope], 1))
    return qs, ks, list(v_parts)


def _mix_post_core(o_parts, gz_parts, ob_parts, wn, won):
    oa = [_rms(o, wn) * _silu(z) for o, z in zip(o_parts, gz_parts)]
    ob = [_rms(o, won) for o in ob_parts]
    return jnp.concatenate(oa + ob, 1)


CONV_K = 4
HALO = SUBLANES


def _conv_fwd(proj, w8, name):
    s = proj.shape[0]
    c = w8.shape[1]
    t = min(256, s)
    n = s // t
    hb = t // HALO

    def body(x_ref, prev_ref, w_ref, o_ref, buf):
        i = pl.program_id(0)
        buf[pl.ds(0, HALO), :] = jnp.where(i > 0, prev_ref[...], 0.0)
        buf[pl.ds(HALO, t), :] = x_ref[...]
        acc = jnp.zeros((t, c), F32)
        for k in range(CONV_K):
            acc = acc + w_ref[k:k + 1, :] * buf[pl.ds(HALO - (CONV_K - 1) + k, t), :]
        o_ref[...] = acc

    return pl.pallas_call(
        body, name=name, grid=(n,),
        in_specs=[pl.BlockSpec((t, c), lambda i: (i, 0)),
                  pl.BlockSpec((HALO, c), lambda i: (jnp.maximum(i * hb - 1, 0), 0)),
                  pl.BlockSpec(w8.shape, lambda i: (0, 0))],
        out_specs=pl.BlockSpec((t, c), lambda i: (i, 0)),
        out_shape=jax.ShapeDtypeStruct((s, c), F32),
        scratch_shapes=[pltpu.VMEM((t + HALO, c), F32)],
        compiler_params=_params(("parallel",)),
    )(proj, proj, w8)


def _conv_bwd(proj, dy, w8, name):
    s = proj.shape[0]
    c = w8.shape[1]
    t = min(256, s)
    n = s // t
    hb = t // HALO

    def body(x_ref, prev_ref, dy_ref, next_ref, w_ref, dx_ref, dw_ref, bufx, bufd):
        i = pl.program_id(0)
        bufx[pl.ds(0, HALO), :] = jnp.where(i > 0, prev_ref[...], 0.0)
        bufx[pl.ds(HALO, t), :] = x_ref[...]
        bufd[pl.ds(0, t), :] = dy_ref[...]
        bufd[pl.ds(t, HALO), :] = jnp.where(i < n - 1, next_ref[...], 0.0)

        @pl.when(i == 0)
        def _():
            dw_ref[...] = jnp.zeros(dw_ref.shape, F32)

        dyv = dy_ref[...]
        acc = jnp.zeros((t, c), F32)
        for k in range(CONV_K):
            acc = acc + w_ref[k:k + 1, :] * bufd[pl.ds(CONV_K - 1 - k, t), :]
            dw_ref[k:k + 1, :] += jnp.sum(dyv * bufx[pl.ds(HALO - (CONV_K - 1) + k, t), :], axis=0, keepdims=True)
        dx_ref[...] = acc

    return pl.pallas_call(
        body, name=name, grid=(n,),
        in_specs=[pl.BlockSpec((t, c), lambda i: (i, 0)),
                  pl.BlockSpec((HALO, c), lambda i: (jnp.maximum(i * hb - 1, 0), 0)),
                  pl.BlockSpec((t, c), lambda i: (i, 0)),
                  pl.BlockSpec((HALO, c), lambda i: (jnp.minimum((i + 1) * hb, s // HALO - 1), 0)),
                  pl.BlockSpec(w8.shape, lambda i: (0, 0))],
        out_specs=[pl.BlockSpec((t, c), lambda i: (i, 0)), pl.BlockSpec(w8.shape, lambda i: (0, 0))],
        out_shape=[jax.ShapeDtypeStruct((s, c), F32), jax.ShapeDtypeStruct(w8.shape, F32)],
        scratch_shapes=[pltpu.VMEM((t + HALO, c), F32), pltpu.VMEM((t + HALO, c), F32)],
        compiler_params=_params(("arbitrary",)),
    )(proj, proj, dy, dy, w8)


_B_NN = (((2,), (1,)), ((0,), (0,)))
_B_NT = (((2,), (2,)), ((0,), (0,)))
_B_TN = (((1,), (1,)), ((0,), (0,)))


def _dot3(a, b, dims):
    return lax.dot_general(a, b, dims, precision=lax.Precision.HIGH, preferred_element_type=F32)


def _bdot_hi(a, b):
    return _dot3(a, b, _B_NN)


class _Dots:
    nn = staticmethod(lambda a, b: _dot3(a, b, _B_NN))
    nt = staticmethod(lambda a, b: _dot3(a, b, _B_NT))
    tn = staticmethod(lambda a, b: _dot3(a, b, _B_TN))


def _unit_lower_inverse(a, dots):
    c = a.shape[-1]
    ri, ci = _iota(a.shape, 1), _iota(a.shape, 2)
    inner = (ri // 2) == (ci // 2)
    t = (ri == ci).astype(F32) - jnp.where(inner, a, 0.0)
    blk = 4
    while blk <= c:
        outer = (ri // blk) == (ci // blk)
        low = jnp.where(outer & jnp.logical_not(inner), a, 0.0)
        t = t - dots.nn(dots.nn(t, low), t)
        inner = outer
        blk *= 2
    return t


def _stack(xs):
    return jnp.concatenate([x[None] for x in xs], axis=0)


def _gdn_local(dots, q, k, v, gbs):
    b, c, _ = q.shape
    gcols, bcols = [], []
    for gb in gbs:
        lane = _iota(gb.shape, 1)
        for h in range(GDN_HEADS):
            gcols.append(jnp.sum(jnp.where(lane == h, gb, 0.0), axis=1, keepdims=True))
            bcols.append(jnp.sum(jnp.where(lane == GDN_HEADS + h, gb, 0.0), axis=1, keepdims=True))
    gcol, bcol = _stack(gcols), _stack(bcols)
    ri, ci = _iota((b, c, c), 1), _iota((b, c, c), 2)
    incl = ri >= ci
    tril = incl.astype(F32)
    g_cc = _bdot_hi(tril, jnp.broadcast_to(gcol, (b, c, c)))
    g_row = _bdot_hi(jnp.ones((b, c, c), F32), jnp.where(ri == ci, g_cc, 0.0))
    g_cl = _bdot_hi(tril, jnp.broadcast_to(gcol, (b, c, HEAD)))
    g_last = jnp.sum(jnp.broadcast_to(gcol, (b, c, HEAD)), axis=1, keepdims=True)
    decay = jnp.where(incl, jnp.exp(jnp.where(incl, g_cc - g_row, 0.0)), 0.0)
    kk = dots.nt(k, k)
    minv = _unit_lower_inverse(jnp.where(ri > ci, bcol * kk * decay, 0.0), dots)
    e_g = jnp.exp(g_cl)
    u = dots.nn(minv, v * bcol)
    wk = dots.nn(minv, k * (bcol * e_g))
    qk = dots.nt(q, k) * decay
    return u, wk, q * e_g, k * jnp.exp(g_last - g_cl), qk, jnp.exp(g_last)


def _gdn_scan(dots, states, u, wk, qd, kd, qk, gl_tile):
    lane, row = _iota(gl_tile.shape, 1), _iota(gl_tile.shape, 0)
    gl = _stack([
        jnp.sum(jnp.sum(jnp.where((lane == h) & (row == 0), gl_tile, 0.0), axis=1, keepdims=True),
                axis=0, keepdims=True) for h in range(GDN_HEADS)])
    v_new = u - dots.nn(wk, states)
    o = dots.nn(qd, states) + dots.nn(qk, v_new)
    return states * gl + dots.tn(kd, v_new), o


def _heads(x):
    return jnp.stack(_split(x, HW4))


GDN_W = GDN_HEADS * HEAD
HW4 = [HEAD] * GDN_HEADS
LOCAL_CHUNKS = 4
_CHUNK_ROWS = [pl.ds(cc * CHUNK, CHUNK) for cc in range(LOCAL_CHUNKS)]


def _chunk_heads(ref):
    return jnp.concatenate([_heads(ref[rows, :]) for rows in _CHUNK_ROWS], 0)


def _gdn_local_fwd(q, k, v, gb, name):
    s = q.shape[0]
    t = LOCAL_CHUNKS * CHUNK

    def body(q_ref, k_ref, v_ref, gb_ref, u_ref, wk_ref, qd_ref, kd_ref, qk_ref, gl_ref):
        u, wk, qd, kd, qk, gl = _gdn_local(_Dots, _chunk_heads(q_ref), _chunk_heads(k_ref),
                                           _chunk_heads(v_ref), [gb_ref[rows, :] for rows in _CHUNK_ROWS])
        lane = _iota((CHUNK, LANES), 1)
        for cc, rows in enumerate(_CHUNK_ROWS):
            gl_tile = jnp.zeros((CHUNK, LANES), F32)
            for h in range(GDN_HEADS):
                b, cols = cc * GDN_HEADS + h, pl.ds(h * HEAD, HEAD)
                u_ref[rows, cols] = u[b]
                wk_ref[rows, cols] = wk[b]
                qd_ref[rows, cols] = qd[b]
                kd_ref[rows, cols] = kd[b]
                qk_ref[h, rows, :] = qk[b]
                gl_tile = gl_tile + jnp.where(lane == h, gl[b], 0.0)
            gl_ref[rows, :] = gl_tile

    row = pl.BlockSpec((t, GDN_W), lambda i: (i, 0))
    lane = pl.BlockSpec((t, LANES), lambda i: (i, 0))
    qks = pl.BlockSpec((GDN_HEADS, t, CHUNK), lambda i: (0, i, 0))
    return pl.pallas_call(
        body, name=name, grid=(s // t,),
        in_specs=[row, row, row, lane],
        out_specs=[row, row, row, row, qks, lane],
        out_shape=[jax.ShapeDtypeStruct((s, GDN_W), F32)] * 4
        + [jax.ShapeDtypeStruct((GDN_HEADS, s, CHUNK), F32), jax.ShapeDtypeStruct((s, LANES), F32)],
        compiler_params=_params(("parallel",)),
    )(q, k, v, gb)


def _gdn_local_bwd(q, k, v, gb, du, dwk, dqd, dkd, dqk, dgl, name):
    s = q.shape[0]
    t = LOCAL_CHUNKS * CHUNK

    def body(q_ref, k_ref, v_ref, gb_ref, du_ref, dwk_ref, dqd_ref, dkd_ref, dqk_ref, dgl_ref,
             dq_ref, dk_ref, dv_ref, dgb_ref):
        _, vjp = jax.vjp(functools.partial(_gdn_local, _Dots), _chunk_heads(q_ref), _chunk_heads(k_ref),
                         _chunk_heads(v_ref), [gb_ref[rows, :] for rows in _CHUNK_ROWS])
        lane = _iota((CHUNK, LANES), 1)
        dqk = jnp.stack([dqk_ref[h, rows, :] for rows in _CHUNK_ROWS for h in range(GDN_HEADS)])
        dgl = jnp.stack([jnp.sum(jnp.where(lane == h, dgl_ref[rows, :], 0.0), axis=0, keepdims=True)
                         for rows in _CHUNK_ROWS for h in range(GDN_HEADS)])
        d_q, d_k, d_v, d_gbs = vjp((_chunk_heads(du_ref), _chunk_heads(dwk_ref), _chunk_heads(dqd_ref),
                                    _chunk_heads(dkd_ref), dqk, dgl))
        for cc, rows in enumerate(_CHUNK_ROWS):
            for h in range(GDN_HEADS):
                b, cols = cc * GDN_HEADS + h, pl.ds(h * HEAD, HEAD)
                dq_ref[rows, cols] = d_q[b]
                dk_ref[rows, cols] = d_k[b]
                dv_ref[rows, cols] = d_v[b]
            dgb_ref[rows, :] = d_gbs[cc]

    row = pl.BlockSpec((t, GDN_W), lambda i: (i, 0))
    lane = pl.BlockSpec((t, LANES), lambda i: (i, 0))
    qks = pl.BlockSpec((GDN_HEADS, t, CHUNK), lambda i: (0, i, 0))
    return pl.pallas_call(
        body, name=name, grid=(s // t,),
        in_specs=[row, row, row, lane, row, row, row, row, qks, lane],
        out_specs=[row, row, row, lane],
        out_shape=[jax.ShapeDtypeStruct((s, GDN_W), F32)] * 3 + [jax.ShapeDtypeStruct((s, LANES), F32)],
        compiler_params=_params(("parallel",)),
    )(q, k, v, gb, du, dwk, dqd, dkd, dqk, dgl)


SCAN_CHUNKS = 4


def _scan_rows(s):
    k = min(SCAN_CHUNKS, s // CHUNK)
    return k * CHUNK, [pl.ds(cc * CHUNK, CHUNK) for cc in range(k)]


def _gdn_scan_fwd(u, wk, qd, kd, qk, gl, name):
    s = u.shape[0]
    nc = s // CHUNK
    t, chunk_rows = _scan_rows(s)

    def body(u_ref, wk_ref, qd_ref, kd_ref, qk_ref, gl_ref, o_ref, st_ref, state):
        i = pl.program_id(0)

        @pl.when(i == 0)
        def _():
            state[...] = jnp.zeros(state.shape, F32)

        st = state[...]
        for cc, rows in enumerate(chunk_rows):
            st_ref[cc] = st
            st, o = _gdn_scan(_Dots, st, _heads(u_ref[rows, :]), _heads(wk_ref[rows, :]), _heads(qd_ref[rows, :]),
                              _heads(kd_ref[rows, :]), qk_ref[:, rows, :], gl_ref[rows, :])
            o_ref[rows, :] = jnp.concatenate([o[h] for h in range(GDN_HEADS)], 1)
        state[...] = st

    row = pl.BlockSpec((t, GDN_W), lambda i: (i, 0))
    return pl.pallas_call(
        body, name=name, grid=(s // t,),
        in_specs=[row, row, row, row, pl.BlockSpec((GDN_HEADS, t, CHUNK), lambda i: (0, i, 0)),
                  pl.BlockSpec((t, LANES), lambda i: (i, 0))],
        out_specs=[row, pl.BlockSpec((len(chunk_rows), GDN_HEADS, HEAD, HEAD), lambda i: (i, 0, 0, 0))],
        out_shape=[jax.ShapeDtypeStruct((s, GDN_W), F32),
                   jax.ShapeDtypeStruct((nc, GDN_HEADS, HEAD, HEAD), F32)],
        scratch_shapes=[pltpu.VMEM((GDN_HEADS, HEAD, HEAD), F32)],
        compiler_params=_params(("arbitrary",)),
    )(u, wk, qd, kd, qk, gl)


def _gdn_scan_bwd(u, wk, qd, kd, qk, gl, st, do, name):
    s = u.shape[0]
    t, chunk_rows = _scan_rows(s)
    n = s // t

    def body(u_ref, wk_ref, qd_ref, kd_ref, qk_ref, gl_ref, st_ref, do_ref,
             du_ref, dwk_ref, dqd_ref, dkd_ref, dqk_ref, dgl_ref, dstate):
        i = pl.program_id(0)

        @pl.when(i == 0)
        def _():
            dstate[...] = jnp.zeros(dstate.shape, F32)

        unheads = lambda x: jnp.concatenate([x[h] for h in range(GDN_HEADS)], 1)
        ds = dstate[...]
        for cc in reversed(range(len(chunk_rows))):
            rows = chunk_rows[cc]
            _, vjp = jax.vjp(functools.partial(_gdn_scan, _Dots), st_ref[cc], _heads(u_ref[rows, :]),
                             _heads(wk_ref[rows, :]), _heads(qd_ref[rows, :]), _heads(kd_ref[rows, :]),
                             qk_ref[:, rows, :], gl_ref[rows, :])
            ds, d_u, d_wk, d_qd, d_kd, d_qk, d_gl = vjp((ds, _heads(do_ref[rows, :])))
            dqk_ref[:, rows, :] = d_qk
            du_ref[rows, :] = unheads(d_u)
            dwk_ref[rows, :] = unheads(d_wk)
            dqd_ref[rows, :] = unheads(d_qd)
            dkd_ref[rows, :] = unheads(d_kd)
            dgl_ref[rows, :] = d_gl
        dstate[...] = ds

    rev = lambda i: (n - 1 - i, 0)
    row = pl.BlockSpec((t, GDN_W), rev)
    lane = pl.BlockSpec((t, LANES), rev)
    qks = pl.BlockSpec((GDN_HEADS, t, CHUNK), lambda i: (0, n - 1 - i, 0))
    return pl.pallas_call(
        body, name=name, grid=(n,),
        in_specs=[row, row, row, row, qks, lane,
                  pl.BlockSpec((len(chunk_rows), GDN_HEADS, HEAD, HEAD), lambda i: (n - 1 - i, 0, 0, 0)), row],
        out_specs=[row, row, row, row, qks, lane],
        out_shape=[jax.ShapeDtypeStruct((s, GDN_W), F32)] * 4
        + [jax.ShapeDtypeStruct((GDN_HEADS, s, CHUNK), F32), jax.ShapeDtypeStruct((s, LANES), F32)],
        scratch_shapes=[pltpu.VMEM((GDN_HEADS, HEAD, HEAD), F32)],
        compiler_params=_params(("arbitrary",)),
    )(u, wk, qd, kd, qk, gl, st, do)


def _chunk_mask(i, j, t):
    r = i * t + _iota((t, t), 0)
    c = j * t + _iota((t, t), 1)
    return (r // CHUNK) >= (c // CHUNK)


ATT_TILE = 1024
ATT_Q_TILES = 1
ATT_BWD_TILE = 1024


def _attn_fwd(q, k, v, name, gather=()):
    nh, s = MLA_HEADS, q.shape[0]
    tk = min(ATT_TILE, s)
    tq = min(ATT_Q_TILES * tk, s)
    qk = tq // tk
    nq, n = s // tq, s // tk
    nt = (((1,), (1,)), ((), ()))
    host = _Hosted(gather)
    ng = host.n
    steps = nh * nq * n

    def body(*refs):
        q_ref, k_ref, v_ref = refs[:3]
        x_refs = refs[3:3 + ng]
        o_ref, lse_ref = refs[3 + ng:5 + ng]
        got_refs = refs[5 + ng:5 + 2 * ng]
        m_sc, l_sc, acc_sc = refs[5 + 2 * ng:8 + 2 * ng]
        sems = refs[8 + 2 * ng:]
        i, j = pl.program_id(1), pl.program_id(2)
        step_no = (pl.program_id(0) * nq + i) * n + j
        host.open(step_no, steps, x_refs, got_refs, sems)

        @pl.when(j == 0)
        def _():
            m_sc[...] = jnp.full(m_sc.shape, -jnp.inf, F32)
            l_sc[...] = jnp.zeros(l_sc.shape, F32)
            acc_sc[...] = jnp.zeros(acc_sc.shape, F32)

        def step(masked):
            sc = lax.dot_general(q_ref[...], k_ref[...], nt, preferred_element_type=F32)
            if masked:
                r = i * tq + _iota((tq, tk), 0)
                c = j * tk + _iota((tq, tk), 1)
                sc = jnp.where((r // CHUNK) >= (c // CHUNK), sc, -jnp.inf)
            m_prev = m_sc[:, :1]
            m_new = jnp.maximum(m_prev, jnp.max(sc, axis=1, keepdims=True))
            alpha = jnp.exp(m_prev - m_new)
            p = jnp.exp(sc - m_new)
            l_sc[...] = jnp.broadcast_to(alpha * l_sc[:, :1] + jnp.sum(p, axis=1, keepdims=True), l_sc.shape)
            acc_sc[...] = alpha * acc_sc[...] + jnp.dot(p.astype(BF16), v_ref[...], preferred_element_type=F32)
            m_sc[...] = jnp.broadcast_to(m_new, m_sc.shape)

        pl.when(j < i * qk)(lambda: step(False))
        pl.when(j // qk == i)(lambda: step(True))

        @pl.when(j == n - 1)
        def _():
            o_ref[...] = acc_sc[...] / l_sc[:, :1]
            lse_ref[...] = m_sc[...] + jnp.log(l_sc[...])

        host.close(step_no, steps, x_refs, got_refs, sems)

    qrow = lambda h, i, j: (i, h)
    krow = lambda h, i, j: (jnp.minimum(j, (i + 1) * qk - 1), h)
    res = pl.pallas_call(
        body, name=name, grid=(nh, nq, n),
        in_specs=[pl.BlockSpec((tq, QK_PAD), qrow), pl.BlockSpec((tk, QK_PAD), krow),
                  pl.BlockSpec((tk, HEAD), krow)] + host.specs(),
        out_specs=[pl.BlockSpec((tq, HEAD), qrow), pl.BlockSpec((None, tq, LANES), lambda h, i, j: (h, i, 0))]
        + host.specs(),
        out_shape=[jax.ShapeDtypeStruct((s, nh * HEAD), F32), jax.ShapeDtypeStruct((nh, s, LANES), F32)]
        + host.out_shapes(),
        scratch_shapes=[pltpu.VMEM((tq, LANES), F32), pltpu.VMEM((tq, LANES), F32), pltpu.VMEM((tq, HEAD), F32)]
        + host.scratch(),
        compiler_params=_params(("arbitrary",) * 3 if ng else ("parallel", "parallel", "arbitrary")),
    )(q, k, v, *host.operands)
    return res[0], res[1], list(res[2:])


def _attn_bwd(q, k, v, o, do, lse, name, scatter=()):
    nh, s = MLA_HEADS, q.shape[0]
    t = min(ATT_BWD_TILE, s)
    n = s // t
    tn = (((0,), (0,)), ((), ()))
    nt = (((1,), (1,)), ((), ()))
    host = _Hosted(scatter=scatter)
    nx = host.n
    steps = nh * n * n

    def body(*refs):
        q_ref, k_ref, v_ref, o_ref, do_ref, lse_ref = refs[:6]
        x_refs = refs[6:6 + nx]
        dq_ref, dk_ref, dv_ref = refs[6 + nx:9 + nx]
        got_refs = refs[9 + nx:9 + 2 * nx]
        dk_acc, dv_acc, dq_acc = refs[9 + 2 * nx:12 + 2 * nx]
        sems = refs[12 + 2 * nx:]
        j, i = pl.program_id(1), pl.program_id(2)
        step_no = (pl.program_id(0) * n + j) * n + i
        host.open(step_no, steps, x_refs, got_refs, sems)

        @pl.when(i + j == 0)
        def _():
            dq_acc[...] = jnp.zeros(dq_acc.shape, F32)

        @pl.when(i == 0)
        def _():
            dk_acc[...] = jnp.zeros(dk_acc.shape, F32)
            dv_acc[...] = jnp.zeros(dv_acc.shape, F32)

        def step(masked):
            qv, kv, do = q_ref[...], k_ref[...], do_ref[...]
            sc = lax.dot_general(qv, kv, nt, preferred_element_type=F32)
            p = jnp.exp(sc - lse_ref[:, :1])
            if masked:
                p = jnp.where(_chunk_mask(i, j, t), p, 0.0)
            dob = do.astype(BF16)
            dp = lax.dot_general(dob, v_ref[...], nt, preferred_element_type=F32)
            ds = (p * (dp - jnp.sum(do * o_ref[...], axis=1, keepdims=True))).astype(BF16)
            dv_acc[...] += lax.dot_general(p.astype(BF16), dob, tn, preferred_element_type=F32)
            dk_acc[...] += lax.dot_general(ds, qv, tn, preferred_element_type=F32)
            rows = pl.ds(pl.multiple_of(i * t, t), t)
            dq_acc[rows, :] += jnp.dot(ds, kv, preferred_element_type=F32)

        pl.when(i > j)(lambda: step(False))
        pl.when(i == j)(lambda: step(True))

        @pl.when(i == n - 1)
        def _():
            dk_ref[...] = dk_acc[...]
            dv_ref[...] = dv_acc[...]

        @pl.when(i + j == 2 * (n - 1))
        def _():
            dq_ref[...] = dq_acc[...]

        host.close(step_no, steps, x_refs, got_refs, sems)

    qrow = lambda h, j, i: (jnp.maximum(i, j), h)
    krow = lambda h, j, i: (j, h)
    res = pl.pallas_call(
        body, name=name, grid=(nh, n, n),
        in_specs=[pl.BlockSpec((t, QK_PAD), qrow), pl.BlockSpec((t, QK_PAD), krow), pl.BlockSpec((t, HEAD), krow),
                  pl.BlockSpec((t, HEAD), qrow), pl.BlockSpec((t, HEAD), qrow),
                  pl.BlockSpec((None, t, LANES), lambda h, j, i: (h, jnp.maximum(i, j), 0))] + host.specs(),
        out_specs=[pl.BlockSpec((s, QK_PAD), lambda h, j, i: (0, h)),
                   pl.BlockSpec((t, QK_PAD), krow), pl.BlockSpec((t, HEAD), krow)] + host.specs(),
        out_shape=[jax.ShapeDtypeStruct((s, nh * QK_PAD), F32), jax.ShapeDtypeStruct((s, nh * QK_PAD), F32),
                   jax.ShapeDtypeStruct((s, nh * HEAD), F32)] + host.out_shapes(),
        scratch_shapes=[pltpu.VMEM((t, QK_PAD), F32), pltpu.VMEM((t, HEAD), F32), pltpu.VMEM((s, QK_PAD), F32)]
        + host.scratch(),
        compiler_params=_params(("arbitrary", "arbitrary", "arbitrary")),
    )(q, k, v, o, do, lse, *host.operands)
    return res[0], res[1], res[2], list(res[3:])


def _place():
    return lax.axis_index("x"), lax.axis_index("y"), lax.axis_index("c")


def _allgather8(x, name):
    r, c = x.shape

    def body(x_ref, out_ref, send_sems, recv_sems, local_sem):
        mx, my, mc = _place()
        me = 4 * mx + 2 * my + mc
        mine = pltpu.make_async_copy(x_ref, out_ref.at[me], local_sem)
        mine.start()
        copies = []
        for d in range(1, 8):
            px = 1 - mx if d & 4 else mx
            py = 1 - my if d & 2 else my
            pc = 1 - mc if d & 1 else mc
            cp = pltpu.make_async_remote_copy(
                src_ref=x_ref, dst_ref=out_ref.at[me], send_sem=send_sems.at[d - 1], recv_sem=recv_sems.at[d - 1],
                device_id=(px, py, pc), device_id_type=MESH)
            cp.start()
            copies.append(cp)
        for cp in copies:
            cp.wait()
        mine.wait()

    return pl.pallas_call(
        body, name=name,
        out_shape=jax.ShapeDtypeStruct((8, r, c), x.dtype),
        in_specs=[pl.BlockSpec(memory_space=pltpu.VMEM)],
        out_specs=pl.BlockSpec(memory_space=pltpu.VMEM),
        scratch_shapes=[pltpu.SemaphoreType.DMA((7,)), pltpu.SemaphoreType.DMA((7,)), pltpu.SemaphoreType.DMA],
        compiler_params=pltpu.CompilerParams(vmem_limit_bytes=VMEM_LIMIT),
    )(x)


def _allgather_chips(x, name):
    r, c = x.shape

    def body(x_ref, out_ref, send_sems, recv_sems, local_sems):
        for phase in range(3):
            _gather_phase(phase, x_ref, out_ref, send_sems, recv_sems, local_sems, 0)

    return pl.pallas_call(
        body, name=name,
        out_shape=jax.ShapeDtypeStruct((4, r, c), x.dtype),
        in_specs=[pl.BlockSpec(memory_space=pltpu.VMEM)],
        out_specs=pl.BlockSpec(memory_space=pltpu.VMEM),
        scratch_shapes=_gather_sems(1),
        compiler_params=pltpu.CompilerParams(vmem_limit_bytes=VMEM_LIMIT),
    )(x)


GATHER_COPIES = 6


def _gather_sems(n):
    return [pltpu.SemaphoreType.DMA((GATHER_COPIES * n,)), pltpu.SemaphoreType.DMA((GATHER_COPIES * n,)),
            pltpu.SemaphoreType.DMA((n,))]


def _gather_phase(phase, x_ref, out_ref, send_sems, recv_sems, local_sems, slot):
    mx, my, mc = _place()
    j = 2 * mx + my
    rh = x_ref.shape[0] // 2
    base = GATHER_COPIES * slot
    chips = [(1 - mx, my), (mx, 1 - my), (1 - mx, 1 - my)]
    sibling = (mx, my, 1 - mc)

    def half(jj, hc):
        return out_ref.at[jj, pl.ds(hc * rh, rh), :]

    def over_ici(kk, block):
        px, py = chips[kk]
        return pltpu.make_async_remote_copy(
            src_ref=x_ref.at[pl.ds(mc * rh, rh), :], dst_ref=half(block, mc), send_sem=send_sems.at[base + kk],
            recv_sem=recv_sems.at[base + kk], device_id=(px, py, mc), device_id_type=MESH)

    def to_sibling(kk, hc):
        px, py = chips[kk]
        blk = half(2 * px + py, hc)
        return pltpu.make_async_remote_copy(
            src_ref=blk, dst_ref=blk, send_sem=send_sems.at[base + 3 + kk], recv_sem=recv_sems.at[base + 3 + kk],
            device_id=sibling, device_id_type=MESH)

    mine = pltpu.make_async_copy(x_ref, out_ref.at[j], local_sems.at[slot])
    if phase == 0:
        mine.start()
        for kk in range(3):
            over_ici(kk, j).start()
    elif phase == 1:
        for kk, (px, py) in enumerate(chips):
            over_ici(kk, 2 * px + py).wait_recv()
            to_sibling(kk, mc).start()
    else:
        for kk in range(3):
            to_sibling(kk, 1 - mc).wait_recv()
        for kk in range(3):
            over_ici(kk, j).wait_send()
            to_sibling(kk, mc).wait_send()
        mine.wait()


RS_ROWS = 32


def _reduce_scatter_chips(g, name):
    _, r, c = g.shape
    rh = r // 2
    steps = rh // RS_ROWS

    def body(g_ref, out_ref, sib_ref, part_ref, got_ref, send_sems, recv_sems):
        mx, my, mc = _place()
        j = 2 * mx + my
        sibling = (mx, my, 1 - mc)
        chips = [(1 - mx, my), (mx, 1 - my), (1 - mx, 1 - my)]

        to_sib = pltpu.make_async_remote_copy(
            src_ref=g_ref.at[:, pl.ds((1 - mc) * rh, rh), :], dst_ref=sib_ref,
            send_sem=send_sems.at[0], recv_sem=recv_sems.at[0], device_id=sibling, device_id_type=MESH)
        to_sib.start()
        to_sib.wait()

        def add_sibling(step, carry):
            rows = pl.ds(pl.multiple_of(step * RS_ROWS, RS_ROWS), RS_ROWS)
            mine = g_ref[:, pl.ds(pl.multiple_of(mc * rh + step * RS_ROWS, RS_ROWS), RS_ROWS), :]
            part_ref[:, rows, :] = mine.astype(F32) + sib_ref[:, rows, :].astype(F32)
            return carry

        lax.fori_loop(0, steps, add_sibling, 0)

        def to_bf16(step, carry):
            rows = pl.ds(pl.multiple_of(step * RS_ROWS, RS_ROWS), RS_ROWS)
            sib_ref[:, rows, :] = part_ref[:, rows, :].astype(BF16)
            return carry

        lax.fori_loop(0, steps, to_bf16, 0)

        sends = []
        for kk, (px, py) in enumerate(chips):
            cp = pltpu.make_async_remote_copy(
                src_ref=sib_ref.at[2 * px + py], dst_ref=got_ref.at[kk],
                send_sem=send_sems.at[1 + kk], recv_sem=recv_sems.at[1 + kk],
                device_id=(px, py, mc), device_id_type=MESH)
            cp.start()
            sends.append(cp)
        for cp in sends:
            cp.wait()

        def total(step, carry):
            rows = pl.ds(pl.multiple_of(step * RS_ROWS, RS_ROWS), RS_ROWS)
            acc = part_ref[j, rows, :]
            for kk in range(3):
                acc = acc + got_ref[kk, rows, :].astype(F32)
            out_ref[pl.ds(pl.multiple_of(mc * rh + step * RS_ROWS, RS_ROWS), RS_ROWS), :] = acc
            return carry

        lax.fori_loop(0, steps, total, 0)

        done = pltpu.make_async_remote_copy(
            src_ref=out_ref.at[pl.ds(mc * rh, rh), :], dst_ref=out_ref.at[pl.ds(mc * rh, rh), :],
            send_sem=send_sems.at[4], recv_sem=recv_sems.at[4], device_id=sibling, device_id_type=MESH)
        done.start()
        done.wait_send()
        pltpu.make_async_remote_copy(
            src_ref=out_ref.at[pl.ds((1 - mc) * rh, rh), :], dst_ref=out_ref.at[pl.ds((1 - mc) * rh, rh), :],
            send_sem=send_sems.at[4], recv_sem=recv_sems.at[4], device_id=sibling, device_id_type=MESH).wait_recv()

    return pl.pallas_call(
        body, name=name,
        out_shape=jax.ShapeDtypeStruct((r, c), F32),
        in_specs=[pl.BlockSpec(memory_space=pltpu.VMEM)],
        out_specs=pl.BlockSpec(memory_space=pltpu.VMEM),
        scratch_shapes=[pltpu.VMEM((4, rh, c), BF16), pltpu.VMEM((4, rh, c), F32), pltpu.VMEM((3, rh, c), BF16),
                        pltpu.SemaphoreType.DMA((5,)), pltpu.SemaphoreType.DMA((5,))],
        compiler_params=pltpu.CompilerParams(vmem_limit_bytes=VMEM_LIMIT),
    )(g)


def _sum8(x, name):
    _, r, c = x.shape

    def body(x_ref, o_ref):
        acc = x_ref[0]
        for d in range(1, 8):
            acc = acc + x_ref[d]
        o_ref[...] = acc

    return pl.pallas_call(
        body, name=name, out_shape=jax.ShapeDtypeStruct((r, c), F32),
        in_specs=[pl.BlockSpec(memory_space=pltpu.VMEM)], out_specs=pl.BlockSpec(memory_space=pltpu.VMEM),
    )(x)


SCATTER_COPIES = 7


def _scatter_sems(n):
    return [pltpu.SemaphoreType.DMA((SCATTER_COPIES * n,)), pltpu.SemaphoreType.DMA((SCATTER_COPIES * n,))]


def _scatter_phase(phase, g_ref, got_ref, send_sems, recv_sems, slot):
    mx, my, mc = _place()
    rh = g_ref.shape[1] // 2
    base = SCATTER_COPIES * slot
    for d in range(1, 8):
        px = 1 - mx if d & 4 else mx
        py = 1 - my if d & 2 else my
        pc = 1 - mc if d & 1 else mc
        cp = pltpu.make_async_remote_copy(
            src_ref=g_ref.at[2 * px + py, pl.ds(pc * rh, rh), :], dst_ref=got_ref.at[d - 1],
            send_sem=send_sems.at[base + d - 1], recv_sem=recv_sems.at[base + d - 1],
            device_id=(px, py, pc), device_id_type=MESH)
        if phase == 0:
            cp.start()
        else:
            cp.wait()


def _scatter_sum(g, got, name):
    mx, my, mc = _place()
    rh, c = got.shape[1], got.shape[2]
    mine = lax.dynamic_slice(g, (2 * mx + my, mc * rh, 0), (1, rh, c))[0]
    t = _pick(rh, 256, 16)

    def body(m_ref, got_ref, o_ref):
        acc = m_ref[...].astype(F32)
        for d in range(SCATTER_COPIES):
            acc = acc + got_ref[d].astype(F32)
        o_ref[...] = acc

    return pl.pallas_call(
        body, name=name, grid=(rh // t,),
        in_specs=[pl.BlockSpec((t, c), lambda i: (i, 0)), pl.BlockSpec((SCATTER_COPIES, t, c), lambda i: (0, i, 0))],
        out_specs=pl.BlockSpec((t, c), lambda i: (i, 0)),
        out_shape=jax.ShapeDtypeStruct((rh, c), F32), compiler_params=_params(("parallel",)),
    )(mine, got)


def _scatter_finish(halves, name):
    n = len(halves)

    def body(*refs):
        h_refs, o_refs = refs[:n], refs[n:2 * n]
        send_sems, recv_sems = refs[2 * n:]
        mx, my, mc = _place()
        copies = [pltpu.make_async_remote_copy(
            src_ref=h_refs[kk], dst_ref=o_refs[kk], send_sem=send_sems.at[kk], recv_sem=recv_sems.at[kk],
            device_id=(mx, my, 1 - mc), device_id_type=MESH) for kk in range(n)]
        for cp in copies:
            cp.start()
        for cp in copies:
            cp.wait()

    hbm = pl.BlockSpec(memory_space=pl.ANY)
    theirs = pl.pallas_call(
        body, name=name, in_specs=[hbm] * n, out_specs=[hbm] * n,
        out_shape=[jax.ShapeDtypeStruct(h.shape, F32) for h in halves],
        scratch_shapes=[pltpu.SemaphoreType.DMA((n,)), pltpu.SemaphoreType.DMA((n,))],
    )(*halves)
    south = lax.axis_index("c") == 0
    return [jnp.concatenate([jnp.where(south, m, t), jnp.where(south, t, m)], axis=0) for m, t in zip(halves, theirs)]


def _adamw(w, g, m, v, name):
    r, c = w.shape
    t = _pick(r, 256, SUBLANES)
    spec = pl.BlockSpec((t, c), lambda i: (i, 0))

    def body(w_ref, g_ref, m_ref, v_ref, d_ref, nm_ref, nv_ref):
        gv = g_ref[...]
        m_new = ADAM_B1 * m_ref[...] + (1.0 - ADAM_B1) * gv
        v_new = ADAM_B2 * v_ref[...] + (1.0 - ADAM_B2) * (gv * gv)
        m_hat = m_new / (1.0 - ADAM_B1 ** ADAM_STEP)
        v_hat = v_new / (1.0 - ADAM_B2 ** ADAM_STEP)
        d_ref[...] = -ADAM_LR * (m_hat / (jnp.sqrt(v_hat) + ADAM_EPS) + ADAM_WD * w_ref[...])
        nm_ref[...] = m_new
        nv_ref[...] = v_new

    return pl.pallas_call(
        body, name=name, grid=(r // t,), in_specs=[spec] * 4, out_specs=[spec] * 3,
        out_shape=[jax.ShapeDtypeStruct((r, c), F32)] * 3, compiler_params=_params(("parallel",)),
    )(w, g, m, v)


def _pack_rows(parts):
    rows, offs, o = [], [], 0
    for p in parts:
        f = p.reshape(-1)
        n = -(-f.shape[0] // (LANES * SUBLANES)) * SUBLANES
        rows.append(jnp.pad(f, (0, n * LANES - f.shape[0])).reshape(n, LANES))
        offs.append((o, n))
        o += n
    return jnp.concatenate(rows, 0), offs


def _unpack_rows(packed, offs, shapes):
    out = []
    for (o, n), shp in zip(offs, shapes):
        size = 1
        for d in shp:
            size *= d
        out.append(packed[o:o + n].reshape(-1)[:size].reshape(shp))
    return out


def _mm_hosting(a, b, mode, out_dtype, name, gather=(), chips=None, scatter=()):
    res = _mm(a, b, mode, out_dtype, name, gather=gather, chips=chips, scatter=scatter)
    return (res[0], list(res[1:])) if (gather or scatter) else (res, [])


def _ffn_fwd(x, s, sh, g, w_in, w_out, tag, gather_in=(), gather_out=()):
    (h,) = _rowcall(lambda r, p: ([_modulate(r[0], p[0], p[1])], []), [x], [s, sh], [(x.shape[1], BF16)], [],
                    tile=512, name=tag + "_mod")
    gu, got = _mm_hosting(h, w_in, "nn", BF16, tag + "_in", gather_in, chips="b")
    if w_out is None:
        first = got.pop(0)
        w_out = first.reshape(4 * first.shape[1], first.shape[2])
    (act,) = _rowcall(lambda r, p: ([_silu(r[0].astype(F32)) * r[1].astype(F32)], []),
                      [(gu, D_FF, 0), (gu, D_FF, 1)], [], [(D_FF, BF16)], [], tile=256, name=tag + "_act")
    f, got_out = _mm_hosting(act, w_out, "nn", F32, tag + "_out", gather_out)
    got = got + got_out
    (y,) = _rowcall(lambda r, p: ([r[0] + 0.5 * p[0] * r[1]], []), [x, f], [g], [(x.shape[1], F32)], [],
                    tile=512, name=tag + "_res")
    return y, (x, h, gu, act, f), got, w_out


def _ffn_bwd(dy, saved, s, sh, g, w_in, w_out, tag, scatter_bin=(), scatter_bwin=()):
    x, h, gu, act, f = saved
    d = x.shape[1]
    df, dg = _rowcall(lambda r, p: ([0.5 * p[0] * r[0]], [0.5 * jnp.sum(r[0] * r[1], 0, keepdims=True)]),
                      [dy, f], [g], [(d, BF16)], [(1, d)], tile=512, name=tag + "_bres")
    da = _mm(df, w_out, "nt", BF16, tag + "_bout")
    dw_out = _mm(act, df, "tn", BF16, tag + "_bwout")

    def act_bwd(r, p):
        gate, up, dav = r[0].astype(F32), r[1].astype(F32), r[2].astype(F32)
        _, vjp = jax.vjp(lambda a, b: _silu(a) * b, gate, up)
        dgate, dup = vjp(dav)
        return [jnp.concatenate([dgate, dup], 1)], []

    (dgu,) = _rowcall(act_bwd, [(gu, D_FF, 0), (gu, D_FF, 1), da], [], [(2 * D_FF, BF16)], [], tile=256,
                      name=tag + "_bact")
    dh, got_a = _mm_hosting(dgu, w_in, "nt", F32, tag + "_bin", chips="b", scatter=scatter_bin)
    dw_in, got_b = _mm_hosting(h, dgu, "tn", BF16, tag + "_bwin", chips="out", scatter=scatter_bwin)

    def mod_bwd(r, p):
        _, vjp = jax.vjp(_modulate, r[0], p[0], p[1])
        dx, ds, dsh = vjp(r[1])
        return [r[2] + dx], [ds, dsh]

    dx, ds, dsh = _rowcall(mod_bwd, [x, dh, dy], [s, sh], [(d, F32)], [(1, d), (1, d)], tile=512, name=tag + "_bmod")
    return dx, (dsh, ds, dg), dw_in, dw_out, list(got_a) + list(got_b)


def _mixer_fwd(x, s, sh, g, wts, rope, gather=()):
    w_in_p, conv8, a_log, dt_bias, wn, wq, w_uq_p, wkv, w_ukv, wqn, wqr, wkn, wkr, won, w_out = wts
    cos2, sin2 = rope
    d = x.shape[1]
    (h,) = _rowcall(lambda r, p: ([_modulate(r[0], p[0], p[1])], []), [x], [s, sh], [(d, BF16)], [],
                    tile=512, name="mix_mod")
    proj = _mm(h, w_in_p, "nn", F32, "mix_in")
    qkv_c = _conv_fwd(proj, conv8, "mix_conv")
    gab = (proj, LANES, 23)

    q, k, v, gb = _rowcall(
        lambda r, p: (list(_gdn_prep_core(_split(r[0], [HEAD] * 12), r[1], p[0], p[1])), []),
        [qkv_c, gab], [a_log, dt_bias], [(512, F32)] * 3 + [(LANES, F32)], [], tile=256, name="mix_gdn_prep")
    gdn_local = _gdn_local_fwd(q, k, v, gb, "mix_gdn_local")
    o_gdn, gdn_states = _gdn_scan_fwd(*gdn_local, "mix_gdn_scan")
    states = (gdn_local, gdn_states)

    cq, ckv, kr = (proj, 512, 4), (proj, 256, 10), (proj, LANES, 22)
    cqn, ckvn, k_rope = _rowcall(
        lambda r, p: (list(_mla_prep_core(r[0][:, :MLA_Q_LORA], r[1], r[2], r[3], r[4], p[0], p[1], p[2])), []),
        [cq, ckv, kr, cos2, sin2], [wq, wkv, wkr], [(MLA_Q_LORA, BF16), (MLA_KV_LORA, BF16), (LANES, F32)], [],
        tile=512, name="mix_mla_prep")
    qf = _mm(cqn, w_uq_p, "nn", F32, "mix_uq")
    kvf = _mm(ckvn, w_ukv, "nn", F32, "mix_ukv")

    def qk_prep(r, p):
        qparts = _split(r[0], [HEAD] * 8)
        kvparts = _split(r[1], [HEAD] * 8)
        qs, ks, vs = _qk_prep_core(qparts[:4], qparts[4:], kvparts[0::2], kvparts[1::2], r[2], r[3], r[4],
                                   p[0], p[1], p[2])
        return [jnp.concatenate(qs, 1), jnp.concatenate(ks, 1), jnp.concatenate(vs, 1)], []

    qa, ka, va = _rowcall(qk_prep, [qf, kvf, k_rope, cos2, sin2], [wqn, wqr, wkn],
                          [(4 * QK_PAD, BF16), (4 * QK_PAD, BF16), (4 * HEAD, BF16)], [], tile=256,
                          name="mix_qk_prep")
    o_b, lse, got = _attn_fwd(qa, ka, va, "mix_attn", gather=gather)
    if w_out is None:
        first = got.pop(0)
        w_out = first.reshape(4 * first.shape[1], first.shape[2])

    gz = (proj, 512, 3)
    (mixed,) = _rowcall(
        lambda r, p: ([_mix_post_core(_split(r[0], HW4), _split(r[1], HW4), _split(r[2], HW4), p[0], p[1])], []),
        [o_gdn, gz, o_b], [wn, won], [(2 * 512, BF16)], [], tile=512, name="mix_post")
    y = _mm(mixed, w_out, "nn", F32, "mix_out")
    (x_out,) = _rowcall(lambda r, p: ([r[0] + p[0] * r[1]], []), [x, y], [g], [(d, F32)], [], tile=512,
                        name="mix_res")
    saved = (x, h, proj, qkv_c, q, k, v, gb, states, o_gdn, cqn, ckvn, k_rope, qf, kvf, qa, ka, va, o_b, lse,
             mixed, y)
    return x_out, saved, got, w_out


def _mixer_bwd(dy, saved, s, sh, g, wts, rope, scatter=()):
    w_in_p, conv8, a_log, dt_bias, wn, wq, w_uq_p, wkv, w_ukv, wqn, wqr, wkn, wkr, won, w_out = wts
    cos2, sin2 = rope
    (x, h, proj, qkv_c, q, k, v, gb, states, o_gdn, cqn, ckvn, k_rope, qf, kvf, qa, ka, va, o_b, lse,
     mixed, y) = saved
    d = x.shape[1]
    dyb, dg = _rowcall(lambda r, p: ([p[0] * r[0]], [jnp.sum(r[0] * r[1], 0, keepdims=True)]),
                       [dy, y], [g], [(d, BF16)], [(1, d)], tile=512, name="mix_bres")
    dmixed = _mm(dyb, w_out, "nt", F32, "mix_bout")
    dw_out = _mm(mixed, dyb, "tn", BF16, "mix_bwout")

    gz = (proj, 512, 3)

    def post_bwd(r, p):
        _, vjp = jax.vjp(_mix_post_core, _split(r[0], HW4), _split(r[1], HW4), _split(r[2], HW4), p[0], p[1])
        do, dz, dob, dwn, dwon = vjp(r[3])
        return [jnp.concatenate(do, 1), jnp.concatenate(dz, 1), jnp.concatenate(dob, 1)], [dwn, dwon]

    do_gdn, dgz, do_b, dwn, dwon = _rowcall(post_bwd, [o_gdn, gz, o_b, dmixed], [wn, won], [(512, F32)] * 3,
                                            [(1, HEAD), (1, HEAD)], tile=256, name="mix_bpost")

    dqa, dka, dva, got = _attn_bwd(qa, ka, va, o_b, do_b, lse, "mix_battn", scatter=scatter)

    def qk_bwd(r, p):
        qparts = _split(r[0], [HEAD] * 8)
        kvparts = _split(r[1], [HEAD] * 8)
        _, vjp = jax.vjp(_qk_prep_core, qparts[:4], qparts[4:], kvparts[0::2], kvparts[1::2], r[2], r[3], r[4],
                         p[0], p[1], p[2])
        cot = (_split(r[5], [QK_PAD] * 4), _split(r[6], [QK_PAD] * 4), _split(r[7], HW4))
        dqn, dqr, dkn, dvp, dkrope, _, _, dwqn, dwqr, dwkn = vjp(cot)
        dkv = []
        for a, b in zip(dkn, dvp):
            dkv += [a, b]
        return [jnp.concatenate(list(dqn) + list(dqr), 1), jnp.concatenate(dkv, 1), dkrope], [dwqn, dwqr, dwkn]

    dqf, dkvf, dk_rope, dwqn, dwqr, dwkn = _rowcall(
        qk_bwd, [qf, kvf, k_rope, cos2, sin2, dqa, dka, dva], [wqn, wqr, wkn],
        [(8 * HEAD, BF16), (8 * HEAD, BF16), (LANES, F32)], [(1, HEAD)] * 3, tile=256, name="mix_bqk_prep")
    dcqn = _mm(dqf, w_uq_p, "nt", F32, "mix_buq")
    dw_uq_p = _mm(cqn, dqf, "tn", F32, "mix_bwuq")
    dckvn = _mm(dkvf, w_ukv, "nt", F32, "mix_bukv")
    dw_ukv = _mm(ckvn, dkvf, "tn", F32, "mix_bwukv")

    cq, ckv, kr = (proj, 512, 4), (proj, 256, 10), (proj, LANES, 22)

    def mla_bwd(r, p):
        _, vjp = jax.vjp(_mla_prep_core, r[0][:, :MLA_Q_LORA], r[1], r[2], r[3], r[4], p[0], p[1], p[2])
        dcq, dckv, dkr, _, _, dwq, dwkv, dwkr = vjp((r[5], r[6], r[7]))
        pad = jnp.zeros((dcq.shape[0], 512 - MLA_Q_LORA), F32)
        return [jnp.concatenate([dcq, pad], 1), dckv, dkr], [dwq, dwkv, dwkr]

    dcq, dckv, dkr, dwq, dwkv, dwkr = _rowcall(
        mla_bwd, [cq, ckv, kr, cos2, sin2, dcqn, dckvn, dk_rope], [wq, wkv, wkr],
        [(512, F32), (MLA_KV_LORA, F32), (LANES, F32)], [(1, MLA_Q_LORA), (1, MLA_KV_LORA), (1, LANES)],
        tile=512, name="mix_bmla_prep")

    gdn_local, gdn_states = states
    d_local = _gdn_scan_bwd(*gdn_local, gdn_states, do_gdn, "mix_bgdn_scan")
    dq, dk, dv, dgb = _gdn_local_bwd(q, k, v, gb, *d_local, "mix_bgdn_local")
    gab = (proj, LANES, 23)

    def gdn_prep_bwd(r, p):
        _, vjp = jax.vjp(_gdn_prep_core, _split(r[0], [HEAD] * 12), r[1], p[0], p[1])
        dparts, dgab, da_log, ddt = vjp((r[2], r[3], r[4], r[5]))
        return [jnp.concatenate(dparts, 1), dgab], [da_log, ddt]

    dqkv_c, dgab, da_log, ddt = _rowcall(gdn_prep_bwd, [qkv_c, gab, dq, dk, dv, dgb], [a_log, dt_bias],
                                         [(1536, F32), (LANES, F32)], [(1, LANES), (1, LANES)], tile=256,
                                         name="mix_bgdn_prep")
    dqkv_pre, dconv8 = _conv_bwd(proj, dqkv_c, conv8, "mix_bconv")

    dproj = jnp.concatenate([dqkv_pre.astype(BF16), dgz.astype(BF16), dcq.astype(BF16), dckv.astype(BF16),
                             dkr.astype(BF16), dgab.astype(BF16)], axis=1)
    dh = _mm(dproj, w_in_p, "nt", F32, "mix_bin")
    dw_in_p = _mm(h, dproj, "tn", F32, "mix_bwin")

    def mod_bwd(r, p):
        _, vjp = jax.vjp(_modulate, r[0], p[0], p[1])
        dx, ds, dsh = vjp(r[1])
        return [r[2] + dx], [ds, dsh]

    dx, ds, dsh = _rowcall(mod_bwd, [x, dh, dy], [s, sh], [(d, F32)], [(1, d), (1, d)], tile=512, name="mix_bmod")
    small = dict(conv=dconv8, a_log=da_log, dt=ddt, wn=dwn, wq=dwq, wkv=dwkv, wqn=dwqn, wqr=dwqr, wkn=dwkn,
                 wkr=dwkr, won=dwon)
    return dx, (dsh, ds, dg), dw_in_p, dw_uq_p, dw_ukv, dw_out, small, got


def _pad_cols(a, n):
    return jnp.pad(a, ((0, 0),) * (a.ndim - 1) + ((0, n - a.shape[-1]),))


def _pack_w_in(w):
    z = lambda n: jnp.zeros((w.shape[0], n), w.dtype)
    return jnp.concatenate([w[:, 0:2048], w[:, 2056:2440], z(128), w[:, 2440:2696], w[:, 2696:2760], z(64),
                            w[:, 2048:2056], z(120)], axis=1)


def _unpack_w_in(wp):
    return jnp.concatenate([wp[:, 0:2048], wp[:, 2944:2952], wp[:, 2048:2432], wp[:, 2560:2816], wp[:, 2816:2880]],
                           axis=1)


def _pack_w_uq(w):
    z = jnp.zeros((w.shape[0], LANES - MLA_ROPE), w.dtype)
    nope = [w[:, h * 192:h * 192 + HEAD] for h in range(MLA_HEADS)]
    rope = []
    for h in range(MLA_HEADS):
        rope += [w[:, h * 192 + HEAD:(h + 1) * 192], z]
    return jnp.concatenate(nope + rope, axis=1)


def _unpack_w_uq(wp):
    cols = []
    for h in range(MLA_HEADS):
        cols += [wp[:, h * HEAD:(h + 1) * HEAD], wp[:, 512 + h * LANES:512 + h * LANES + MLA_ROPE]]
    return jnp.concatenate(cols, axis=1)


def _cols_to_chips(a):
    r, c = a.shape
    return a.reshape(r, 4, c // 4).transpose(1, 0, 2)


def _chips_to_cols(a):
    _, r, n = a.shape
    return a.transpose(1, 0, 2).reshape(r, 4 * n)


def _pad128(v, n=LANES):
    return _pad_cols(v.reshape(1, -1), n)


def kernel(x, c, positions, w_ada, b_ada, ffn1_w_in, ffn1_w_out, w_in, gdn_conv_w, gdn_a_log, gdn_dt_bias, gdn_norm_w, mla_q_norm_w, mla_w_uq, mla_kv_norm_w, mla_w_ukv, qkn_q_nope, qkn_q_rope, qkn_k_nope, qkn_k_rope, mla_out_norm_w, w_out, ffn2_w_in, ffn2_w_out, loss_target, m_w_ada, m_b_ada, m_ffn1_w_in, m_ffn1_w_out, m_w_in, m_gdn_conv_w, m_gdn_a_log, m_gdn_dt_bias, m_gdn_norm_w, m_mla_q_norm_w, m_mla_w_uq, m_mla_kv_norm_w, m_mla_w_ukv, m_qkn_q_nope, m_qkn_q_rope, m_qkn_k_nope, m_qkn_k_rope, m_mla_out_norm_w, m_w_out, m_ffn2_w_in, m_ffn2_w_out, v_w_ada, v_b_ada, v_ffn1_w_in, v_ffn1_w_out, v_w_in, v_gdn_conv_w, v_gdn_a_log, v_gdn_dt_bias, v_gdn_norm_w, v_mla_q_norm_w, v_mla_w_uq, v_mla_kv_norm_w, v_mla_w_ukv, v_qkn_q_nope, v_qkn_q_rope, v_qkn_k_nope, v_qkn_k_rope, v_mla_out_norm_w, v_w_out, v_ffn2_w_in, v_ffn2_w_out):
    weights = dict(w_ada=w_ada, b_ada=b_ada, ffn1_w_in=ffn1_w_in, ffn1_w_out=ffn1_w_out, w_in=w_in,
                   gdn_conv_w=gdn_conv_w, gdn_a_log=gdn_a_log, gdn_dt_bias=gdn_dt_bias, gdn_norm_w=gdn_norm_w,
                   mla_q_norm_w=mla_q_norm_w, mla_w_uq=mla_w_uq, mla_kv_norm_w=mla_kv_norm_w, mla_w_ukv=mla_w_ukv,
                   qkn_q_nope=qkn_q_nope, qkn_q_rope=qkn_q_rope, qkn_k_nope=qkn_k_nope, qkn_k_rope=qkn_k_rope,
                   mla_out_norm_w=mla_out_norm_w, w_out=w_out, ffn2_w_in=ffn2_w_in, ffn2_w_out=ffn2_w_out)
    moms_m = dict(w_ada=m_w_ada, b_ada=m_b_ada, ffn1_w_in=m_ffn1_w_in, ffn1_w_out=m_ffn1_w_out, w_in=m_w_in,
                  gdn_conv_w=m_gdn_conv_w, gdn_a_log=m_gdn_a_log, gdn_dt_bias=m_gdn_dt_bias,
                  gdn_norm_w=m_gdn_norm_w, mla_q_norm_w=m_mla_q_norm_w, mla_w_uq=m_mla_w_uq,
                  mla_kv_norm_w=m_mla_kv_norm_w, mla_w_ukv=m_mla_w_ukv, qkn_q_nope=m_qkn_q_nope,
                  qkn_q_rope=m_qkn_q_rope, qkn_k_nope=m_qkn_k_nope, qkn_k_rope=m_qkn_k_rope,
                  mla_out_norm_w=m_mla_out_norm_w, w_out=m_w_out, ffn2_w_in=m_ffn2_w_in, ffn2_w_out=m_ffn2_w_out)
    moms_v = dict(w_ada=v_w_ada, b_ada=v_b_ada, ffn1_w_in=v_ffn1_w_in, ffn1_w_out=v_ffn1_w_out, w_in=v_w_in,
                  gdn_conv_w=v_gdn_conv_w, gdn_a_log=v_gdn_a_log, gdn_dt_bias=v_gdn_dt_bias,
                  gdn_norm_w=v_gdn_norm_w, mla_q_norm_w=v_mla_q_norm_w, mla_w_uq=v_mla_w_uq,
                  mla_kv_norm_w=v_mla_kv_norm_w, mla_w_ukv=v_mla_w_ukv, qkn_q_nope=v_qkn_q_nope,
                  qkn_q_rope=v_qkn_q_rope, qkn_k_nope=v_qkn_k_nope, qkn_k_rope=v_qkn_k_rope,
                  mla_out_norm_w=v_mla_out_norm_w, w_out=v_w_out, ffn2_w_in=v_ffn2_w_in, ffn2_w_out=v_ffn2_w_out)
    names = list(weights)

    seq, d = x.shape[1], x.shape[2]
    x2d = x.reshape(seq, d)
    tgt = loss_target.reshape(seq, d)
    mx, my, mc = _place()
    chip = 2 * mx + my
    me = 2 * chip + mc
    n_mod = b_ada.shape[1] // d
    shard = w_ada.shape[2]

    half = MLA_ROPE // 2
    inv_freq = 10000.0 ** (-jnp.arange(half, dtype=F32) / half)
    ang = positions.astype(F32).reshape(seq, 1) * inv_freq
    cosv, sinv = jnp.cos(ang), jnp.sin(ang)
    cos2 = _pad_cols(jnp.concatenate([cosv, cosv], 1), LANES)
    sin2 = _pad_cols(jnp.concatenate([-sinv, sinv], 1), LANES)
    rope = (cos2, sin2)

    c_all = _allgather8(jnp.pad(c, ((0, SUBLANES - 1), (0, 0))), "gather_c")[:, 0, :]
    (sc_all,) = _rowcall(lambda r, p: ([_silu(r[0])], []), [c_all], [], [(d, F32)], [], tile=8, name="ada_silu")
    mod_part = _mm(sc_all, w_ada[0], "nn", F32, "ada_mm", hi=True)
    mod_all = _allgather8(mod_part, "gather_mod")
    mod_rows = lax.dynamic_index_in_dim(mod_all, me, axis=1, keepdims=False)
    mod_raw = jnp.concatenate([mod_rows[2 * jj] for jj in range(4)], axis=0).reshape(1, 4 * shard)
    (mod,) = _rowcall(lambda r, p: ([r[0] + r[1]], []),
                      [jnp.pad(mod_raw, ((0, 7), (0, 0))), jnp.pad(b_ada, ((0, 7), (0, 0)))], [],
                      [(4 * shard, F32)], [], tile=8, name="ada_bias")
    mods = [mod[0:1, i * d:(i + 1) * d] for i in range(n_mod)]
    sh1, s1, g1, sh2, s2, g2, sh3, s3, g3 = mods

    def shard_bf16(w, pad_to=None):
        w2 = w[0].astype(BF16)
        return _pad_cols(w2, pad_to) if pad_to else w2

    def cols_of(got, w):
        return _chips_to_cols(got[:, :, :w.shape[2]])

    def rows_of(got):
        return got.reshape(4 * got.shape[1], got.shape[2])

    f1_in = _allgather_chips(shard_bf16(ffn1_w_in), "gather_f1_in")
    conv_all = _allgather8(jnp.pad(gdn_conv_w[0], ((0, SUBLANES - CONV_K), (0, 0))), "gather_conv")
    conv8 = jnp.concatenate([conv_all[2 * jj] for jj in range(4)], axis=1)

    x1, sv1, got, f1_out = _ffn_fwd(
        x2d, s1, sh1, g1, f1_in, None, "ffn1",
        gather_in=[shard_bf16(ffn1_w_out), shard_bf16(w_in, 768), shard_bf16(mla_w_uq, 256), shard_bf16(mla_w_ukv)])
    w_in_full, w_uq_full, w_ukv_full = cols_of(got[0], w_in), cols_of(got[1], mla_w_uq), cols_of(got[2], mla_w_ukv)
    wts = (_pack_w_in(w_in_full), conv8, _pad128(gdn_a_log), _pad128(gdn_dt_bias), gdn_norm_w,
           mla_q_norm_w, _pack_w_uq(w_uq_full), mla_kv_norm_w, w_ukv_full, qkn_q_nope, _pad128(qkn_q_rope),
           qkn_k_nope, _pad128(qkn_k_rope), mla_out_norm_w, None)
    xm, svm, got, w_out_full = _mixer_fwd(
        x1, s2, sh2, g2, wts, rope, gather=[shard_bf16(w_out), shard_bf16(ffn2_w_in), shard_bf16(ffn2_w_out)])
    wts = wts[:-1] + (w_out_full,)
    f2_in, f2_out = got[0], rows_of(got[1])
    x3, sv3, _, _ = _ffn_fwd(xm, s3, sh3, g3, f2_in, f2_out, "ffn2")

    def loss_fn(r, p):
        err = r[0] - r[1]
        part = 0.5 * jnp.sum(jnp.sum(err * err, axis=1, keepdims=True) * (1.0 / d), axis=0, keepdims=True)
        return [err * (1.0 / d)], [jnp.broadcast_to(part, (1, LANES))]

    dy, loss_part = _rowcall(loss_fn, [x3, tgt], [], [(d, F32)], [(1, LANES)], tile=512, name="loss")
    loss = lax.psum(loss_part[0, 0], ("x", "y", "c"))

    def chip_cols(dw, pad_to=None):
        g4 = _cols_to_chips(dw).astype(BF16)
        return _pad_cols(g4, pad_to) if pad_to else g4

    def chip_rows(dw):
        return dw.astype(BF16).reshape(4, dw.shape[0] // 4, dw.shape[1])

    dxm, dmod3, dw_f2_in, dw_f2_out, _ = _ffn_bwd(dy, sv3, s3, sh3, g3, f2_in, f2_out, "ffn2")
    parts2 = [dw_f2_in, chip_rows(dw_f2_out)]
    dx1, dmod2, dw_in_p, dw_uq_p, dw_ukv, dw_out_m, small, got2 = _mixer_bwd(dxm, svm, s2, sh2, g2, wts, rope,
                                                                             scatter=parts2)
    parts_m = [chip_cols(_unpack_w_in(dw_in_p), 768), chip_cols(_unpack_w_uq(dw_uq_p), 256), chip_cols(dw_ukv),
               chip_rows(dw_out_m)]
    dx0, dmod1, dw_f1_in, dw_f1_out, got_m = _ffn_bwd(dx1, sv1, s1, sh1, g1, f1_in, f1_out, "ffn1",
                                                      scatter_bin=parts_m[:1], scatter_bwin=parts_m[1:])
    grad_x = dx0.reshape(x.shape)

    dmod = jnp.concatenate(list(dmod1) + list(dmod2) + list(dmod3), axis=1)
    small_parts = [dmod, small["conv"][:CONV_K], small["a_log"], small["dt"], small["wn"], small["wq"],
                   small["wkv"], small["wqn"], small["wqr"], small["wkn"], small["wkr"], small["won"]]
    packed, offs = _pack_rows(small_parts)
    gathered = _allgather8(packed, "gather_small")
    total = _sum8(gathered, "sum_small")
    (g_b_ada, g_conv, g_a_log, g_dt, g_wn, g_wq, g_wkv, g_wqn, g_wqr, g_wkn, g_wkr, g_won) = _unpack_rows(
        total, offs, [p.shape for p in small_parts])
    dmod_all = _unpack_rows(gathered.reshape(-1, LANES),
                            [(dd * packed.shape[0] + offs[0][0], offs[0][1]) for dd in range(8)],
                            [dmod.shape] * 8)
    dmod_all = jnp.concatenate(dmod_all, axis=0)
    dmod_mine = lax.dynamic_slice_in_dim(dmod_all, chip * shard, shard, axis=1)

    def ada_grad(r, p):
        acc = jnp.zeros((r[0].shape[0], shard), F32)
        for b in range(8):
            acc = acc + r[0][:, b:b + 1] * p[0][b:b + 1, :]
        return [acc], []

    (g_w_ada,) = _rowcall(ada_grad, [_pad_cols(sc_all.T, LANES)], [dmod_mine], [(shard, F32)], [], tile=256,
                          name="ada_grad")

    grads = dict(
        w_ada=g_w_ada[None], b_ada=g_b_ada,
        gdn_conv_w=lax.dynamic_slice_in_dim(g_conv, chip * gdn_conv_w.shape[2], gdn_conv_w.shape[2], axis=1)[None],
        gdn_a_log=g_a_log[:, :GDN_HEADS], gdn_dt_bias=g_dt[:, :GDN_HEADS], gdn_norm_w=g_wn, mla_q_norm_w=g_wq,
        mla_kv_norm_w=g_wkv, qkn_q_nope=g_wqn, qkn_q_rope=g_wqr[:, :MLA_ROPE], qkn_k_nope=g_wkn,
        qkn_k_rope=g_wkr[:, :MLA_ROPE], mla_out_norm_w=g_won)

    hosted = ["ffn2_w_in", "ffn2_w_out", "w_in", "mla_w_uq", "mla_w_ukv", "w_out"]
    halves = [_scatter_sum(part, got, "rs_sum_" + nme)
              for nme, part, got in zip(hosted, parts2 + parts_m, got2 + got_m)]
    for nme, full in zip(hosted, _scatter_finish(halves, "rs_finish")):
        grads[nme] = full[:, :weights[nme].shape[2]][None]
    grads["ffn1_w_in"] = _reduce_scatter_chips(dw_f1_in, "rs_f1_in")[None]
    grads["ffn1_w_out"] = _reduce_scatter_chips(chip_rows(dw_f1_out), "rs_f1_out")[None]

    big = ["w_ada", "ffn1_w_in", "ffn1_w_out", "w_in", "mla_w_uq", "mla_w_ukv", "w_out", "ffn2_w_in", "ffn2_w_out"]
    delta, new_m, new_v = {}, {}, {}
    for nme in big:
        shp = weights[nme].shape
        dl, nm, nv = _adamw(weights[nme][0], grads[nme][0], moms_m[nme][0], moms_v[nme][0], "adamw_" + nme)
        delta[nme], new_m[nme], new_v[nme] = dl.reshape(shp), nm.reshape(shp), nv.reshape(shp)
    tiny = [nme for nme in names if nme not in big]
    shapes = [weights[nme].shape for nme in tiny]
    pw, poffs = _pack_rows([weights[nme] for nme in tiny])
    pg, _ = _pack_rows([grads[nme] for nme in tiny])
    pm, _ = _pack_rows([moms_m[nme] for nme in tiny])
    pv, _ = _pack_rows([moms_v[nme] for nme in tiny])
    pd, pnm, pnv = _adamw(pw, pg, pm, pv, "adamw_small")
    for nme, dl, nm, nv in zip(tiny, _unpack_rows(pd, poffs, shapes), _unpack_rows(pnm, poffs, shapes),
                               _unpack_rows(pnv, poffs, shapes)):
        delta[nme], new_m[nme], new_v[nme] = dl, nm, nv

    return (loss, grad_x, *[grads[nme].reshape(weights[nme].shape) for nme in names],
            *[delta[nme] for nme in names], *[new_m[nme] for nme in names], *[new_v[nme] for nme in names])
```

```python
import functools

import jax
import jax.numpy as jnp
from jax import lax
from jax.experimental import pallas as pl
from jax.experimental.pallas import tpu as pltpu

F32 = jnp.float32
BF16 = jnp.bfloat16
HI = lax.Precision.HIGHEST
MESH = pl.DeviceIdType.MESH

EPS = 1e-6
CHUNK = 64
D_FF = 2816
GDN_HEADS = 4
HEAD = 128
MLA_HEADS = 4
MLA_ROPE = 64
MLA_Q_LORA = 384
MLA_KV_LORA = 256
QK_PAD = 256
ATT_SCALE = (HEAD + MLA_ROPE) ** -0.5

ADAM_LR, ADAM_B1, ADAM_B2, ADAM_EPS, ADAM_WD, ADAM_STEP = 0.001, 0.9, 0.999, 1e-08, 0.01, 10

LANES = 128
SUBLANES = 8
VMEM_LIMIT = 56 * 2 ** 20
ROW_TILE = 1024
ROW_TILE_WIDE = 512


def _params(sem=None):
    return pltpu.CompilerParams(dimension_semantics=sem, vmem_limit_bytes=VMEM_LIMIT)


def _pick(n, cap, align):
    best = None
    d = align
    while d <= min(n, cap):
        if n % d == 0:
            best = d
        d += align
    return best if best is not None else n


def _iota(shape, dim):
    return lax.broadcasted_iota(jnp.int32, shape, dim)


def _rowcall(fn, rows, params, out_rows, out_accs, *, tile, name):
    rows = [r if isinstance(r, tuple) else (r, r.shape[1], 0) for r in rows]
    s = rows[0][0].shape[-2]
    t = min(tile, s)
    n = s // t
    n_in = len(rows) + len(params)
    n_row_out = len(out_rows)

    in_specs = []
    for r in rows:
        if len(r) == 3:
            in_specs.append(pl.BlockSpec((t, r[1]), functools.partial(lambda i, b: (i, b), b=r[2])))
        else:
            in_specs.append(pl.BlockSpec((None, t, r[1]), functools.partial(lambda i, b, h: (h, i, b), b=r[2], h=r[3])))
    in_specs += [pl.BlockSpec(p.shape, lambda i: (0, 0)) for p in params]
    out_shape, out_specs = [], []
    for o in out_rows:
        if len(o) == 2:
            out_shape.append(jax.ShapeDtypeStruct((s, o[0]), o[1]))
            out_specs.append(pl.BlockSpec((t, o[0]), lambda i: (i, 0)))
        else:
            out_shape.append(jax.ShapeDtypeStruct((o[2], s, o[0]), o[1]))
            out_specs.append(pl.BlockSpec((o[2], t, o[0]), lambda i: (0, i, 0)))
    out_shape += [jax.ShapeDtypeStruct(shape, F32) for shape in out_accs]
    out_specs += [pl.BlockSpec(shape, lambda i: (0, 0)) for shape in out_accs]

    def body(*refs):
        ins = refs[:n_in]
        outs = refs[n_in:]
        i = pl.program_id(0)
        vals = [r[...] for r in ins]
        row_outs, acc_outs = fn(vals[:len(rows)], vals[len(rows):])
        for r, v in zip(outs[:n_row_out], row_outs):
            if isinstance(v, (list, tuple)):
                for hh, piece in enumerate(v):
                    r[hh] = piece.astype(r.dtype)
            else:
                r[...] = v.astype(r.dtype)
        if out_accs:
            @pl.when(i == 0)
            def _():
                for r in outs[n_row_out:]:
                    r[...] = jnp.zeros(r.shape, F32)
            for r, v in zip(outs[n_row_out:], acc_outs):
                r[...] += v

    res = pl.pallas_call(
        body, name=name, grid=(n,), in_specs=in_specs, out_specs=out_specs, out_shape=out_shape,
        compiler_params=_params(("arbitrary",) if out_accs else ("parallel",)),
    )(*[r[0] for r in rows], *params)
    return list(res)


MM_TILE_MN = 1536


class _Hosted:
    def __init__(self, gather=(), scatter=()):
        self.gather, self.scatter = list(gather), list(scatter)
        self.operands = self.gather + self.scatter
        self.n = len(self.operands)

    def specs(self):
        return [pl.BlockSpec(memory_space=pl.ANY)] * self.n

    def out_shapes(self):
        return ([jax.ShapeDtypeStruct((4,) + x.shape, x.dtype) for x in self.gather]
                + [jax.ShapeDtypeStruct((SCATTER_COPIES, g.shape[1] // 2, g.shape[2]), g.dtype) for g in self.scatter])

    def scratch(self):
        return ((_gather_sems(len(self.gather)) if self.gather else [])
                + (_scatter_sems(len(self.scatter)) if self.scatter else []))

    def _phase(self, ph, ins, outs, sems):
        ng = len(self.gather)
        g_sems, s_sems = (sems[:3], sems[3:]) if ng else ((), sems)
        for slot in range(ng):
            _gather_phase(ph, ins[slot], outs[slot], *g_sems, slot)
        if ph != 1:
            for slot in range(len(self.scatter)):
                _scatter_phase(0 if ph == 0 else 1, ins[ng + slot], outs[ng + slot], *s_sems, slot)

    def open(self, step, steps, ins, outs, sems):
        if self.n:
            pl.when(step == 0)(lambda: self._phase(0, ins, outs, sems))
            pl.when(step == steps // 2)(lambda: self._phase(1, ins, outs, sems))

    def close(self, step, steps, ins, outs, sems):
        if self.n:
            pl.when(step == steps - 1)(lambda: self._phase(2, ins, outs, sems))


def _mm(a, b, mode, out_dtype, name, hi=False, gather=(), chips=None, scatter=()):
    b_shape = b.shape
    if chips == "b":
        b_shape = (b.shape[1], 4 * b.shape[2])
    if mode == "nn":
        (m, k), (_, n) = a.shape, b_shape
        dims = (((1,), (0,)), ((), ()))
    elif mode == "nt":
        (m, k), (n, _) = a.shape, b_shape
        dims = (((1,), (1,)), ((), ()))
    else:
        (k, m), (_, n) = a.shape, b_shape
        dims = (((0,), (0,)), ((), ()))
    tm = _pick(m, MM_TILE_MN if mode == "tn" else 1024, LANES if mode == "tn" else 16)
    tn = _pick(n // 4 if chips and mode != "nt" else n, MM_TILE_MN, LANES)
    tk = _pick(k // 4 if chips and mode == "nt" else k, 1024 if mode == "tn" else MM_TILE_MN, LANES)
    nk = k // tk
    nb = (n // 4) // tn
    kb = (k // 4) // tk
    if mode == "nn":
        a_spec = pl.BlockSpec((tm, tk), lambda i, j, kk: (i, kk))
        b_spec = pl.BlockSpec((tk, tn), lambda i, j, kk: (kk, j))
        if chips == "b":
            b_spec = pl.BlockSpec((None, tk, tn), lambda i, j, kk: (j // nb, kk, j % nb))
    elif mode == "nt":
        a_spec = pl.BlockSpec((tm, tk), lambda i, j, kk: (i, kk))
        b_spec = pl.BlockSpec((tn, tk), lambda i, j, kk: (j, kk))
        if chips == "b":
            b_spec = pl.BlockSpec((None, tn, tk), lambda i, j, kk: (kk // kb, j, kk % kb))
    else:
        a_spec = pl.BlockSpec((tk, tm), lambda i, j, kk: (kk, i))
        b_spec = pl.BlockSpec((tk, tn), lambda i, j, kk: (kk, j))
    out_spec = pl.BlockSpec((tm, tn), lambda i, j, kk: (i, j))
    out_shape = jax.ShapeDtypeStruct((m, n), out_dtype)
    if chips == "out":
        out_spec = pl.BlockSpec((None, tm, tn), lambda i, j, kk: (j // nb, i, j % nb))
        out_shape = jax.ShapeDtypeStruct((4, m, n // 4), out_dtype)

    host = _Hosted(gather, scatter)
    ng = host.n
    grid = (m // tm, n // tn, nk)
    steps = grid[0] * grid[1] * grid[2]

    def body(*refs):
        a_ref, b_ref = refs[:2]
        x_refs = refs[2:2 + ng]
        o_ref = refs[2 + ng]
        got_refs = refs[3 + ng:3 + 2 * ng]
        acc_ref = refs[3 + 2 * ng]
        sems = refs[4 + 2 * ng:]
        kk = pl.program_id(2)
        step = (pl.program_id(0) * grid[1] + pl.program_id(1)) * nk + kk
        host.open(step, steps, x_refs, got_refs, sems)

        @pl.when(kk == 0)
        def _():
            acc_ref[...] = jnp.zeros(acc_ref.shape, F32)

        av, bv = a_ref[...], b_ref[...]
        if hi:
            acc_ref[...] += lax.dot_general(av, bv, dims, precision=HI, preferred_element_type=F32)
        else:
            acc_ref[...] += lax.dot_general(av.astype(BF16), bv.astype(BF16), dims,
                                            preferred_element_type=F32)

        @pl.when(kk == nk - 1)
        def _():
            o_ref[...] = acc_ref[...].astype(o_ref.dtype)

        host.close(step, steps, x_refs, got_refs, sems)

    res = pl.pallas_call(
        body, name=name, grid=grid,
        in_specs=[a_spec, b_spec] + host.specs(),
        out_specs=[out_spec] + host.specs(),
        out_shape=[out_shape] + host.out_shapes(),
        scratch_shapes=[pltpu.VMEM((tm, tn), F32)] + host.scratch(),
        compiler_params=_params(("arbitrary",) * 3 if ng else ("parallel", "parallel", "arbitrary")),
    )(a, b, *host.operands)
    return res if ng else res[0]


def _rms(x, w=None, n=None):
    n = x.shape[-1] if n is None else n
    y = x * lax.rsqrt(jnp.sum(x * x, axis=-1, keepdims=True) * (1.0 / n) + EPS)
    return y if w is None else y * w


def _silu(x):
    return x * jax.nn.sigmoid(x)


def _softplus(x):
    return jnp.maximum(x, 0.0) + jnp.log1p(jnp.exp(-jnp.abs(x)))


def _split(x, widths):
    out, o = [], 0
    for w in widths:
        out.append(x[:, o:o + w])
        o += w
    return out


def _modulate(x, s, sh):
    return _rms(x) * (1.0 + s) + sh


def _rope_rot(x):
    r, c = _iota((LANES, LANES), 0), _iota((LANES, LANES), 1)
    half = MLA_ROPE // 2
    perm = (((r < half) & (c == r + half)) | ((r >= half) & (r < MLA_ROPE) & (c == r - half))).astype(F32)
    return jnp.dot(x, perm, precision=HI, preferred_element_type=F32)


def _rope(x, cos2, sin2):
    return x * cos2 + _rope_rot(x) * sin2


def _gdn_prep_core(qkv_parts, gab, a_log, dt_bias):
    act = [_silu(p) for p in qkv_parts]
    qs = [p * lax.rsqrt(jnp.sum(p * p, -1, keepdims=True) + EPS) * (HEAD ** -0.5) for p in act[:4]]
    ks = [p * lax.rsqrt(jnp.sum(p * p, -1, keepdims=True) + EPS) for p in act[4:8]]
    lane = _iota(gab.shape, 1)
    g = -jnp.exp(a_log) * _softplus(gab + dt_bias)
    beta = jax.nn.sigmoid(gab)
    gb = jnp.where(lane < GDN_HEADS, g, jnp.where(lane < 2 * GDN_HEADS, beta, 0.0))
    return (jnp.concatenate(qs, 1), jnp.concatenate(ks, 1), jnp.concatenate(act[8:], 1), gb)


def _mla_prep_core(cq, ckv, kr, cos2, sin2, wq, wkv, wkr):
    cqn = _rms(cq, wq)
    ckvn = _rms(ckv, wkv)
    k_rope = _rope(_rms(kr, wkr, MLA_ROPE), cos2, sin2)
    return cqn, ckvn, k_rope


def _qk_prep_core(qn_parts, qr_parts, kn_parts, v_parts, k_rope, cos2, sin2, wqn, wqr, wkn):
    qs, ks = [], []
    for h in range(MLA_HEADS):
        qn = _rms(qn_parts[h], wqn) * ATT_SCALE
        qr = _rope(_rms(qr_parts[h], wqr, MLA_ROPE), cos2, sin2) * ATT_SCALE
        qs.append(jnp.concatenate([qn, qr], 1))
        ks.append(jnp.concatenate([_rms(kn_parts[h], wkn), k_rope], 1))
    return qs, ks, list(v_parts)


def _mix_post_core(o_parts, gz_parts, ob_parts, wn, won):
    oa = [_rms(o, wn) * _silu(z) for o, z in zip(o_parts, gz_parts)]
    ob = [_rms(o, won) for o in ob_parts]
    return jnp.concatenate(oa + ob, 1)


CONV_K = 4
HALO = SUBLANES


def _conv_fwd(proj, w8, name):
    s = proj.shape[0]
    c = w8.shape[1]
    t = min(ROW_TILE_WIDE, s)
    n = s // t
    hb = t // HALO

    def body(x_ref, prev_ref, w_ref, o_ref, buf):
        i = pl.program_id(0)
        buf[pl.ds(0, HALO), :] = jnp.where(i > 0, prev_ref[...], 0.0)
        buf[pl.ds(HALO, t), :] = x_ref[...]
        acc = jnp.zeros((t, c), F32)
        for k in range(CONV_K):
            acc = acc + w_ref[k:k + 1, :] * buf[pl.ds(HALO - (CONV_K - 1) + k, t), :]
        o_ref[...] = acc

    return pl.pallas_call(
        body, name=name, grid=(n,),
        in_specs=[pl.BlockSpec((t, c), lambda i: (i, 0)),
                  pl.BlockSpec((HALO, c), lambda i: (jnp.maximum(i * hb - 1, 0), 0)),
                  pl.BlockSpec(w8.shape, lambda i: (0, 0))],
        out_specs=pl.BlockSpec((t, c), lambda i: (i, 0)),
        out_shape=jax.ShapeDtypeStruct((s, c), F32),
        scratch_shapes=[pltpu.VMEM((t + HALO, c), F32)],
        compiler_params=_params(("parallel",)),
    )(proj, proj, w8)


def _conv_bwd(proj, dy, w8, name):
    s = proj.shape[0]
    c = w8.shape[1]
    t = min(ROW_TILE_WIDE, s)
    n = s // t
    hb = t // HALO

    def body(x_ref, prev_ref, dy_ref, next_ref, w_ref, dx_ref, dw_ref, bufx, bufd):
        i = pl.program_id(0)
        bufx[pl.ds(0, HALO), :] = jnp.where(i > 0, prev_ref[...], 0.0)
        bufx[pl.ds(HALO, t), :] = x_ref[...]
        bufd[pl.ds(0, t), :] = dy_ref[...]
        bufd[pl.ds(t, HALO), :] = jnp.where(i < n - 1, next_ref[...], 0.0)

        @pl.when(i == 0)
        def _():
            dw_ref[...] = jnp.zeros(dw_ref.shape, F32)

        dyv = dy_ref[...]
        acc = jnp.zeros((t, c), F32)
        for k in range(CONV_K):
            acc = acc + w_ref[k:k + 1, :] * bufd[pl.ds(CONV_K - 1 - k, t), :]
            dw_ref[k:k + 1, :] += jnp.sum(dyv * bufx[pl.ds(HALO - (CONV_K - 1) + k, t), :], axis=0, keepdims=True)
        dx_ref[...] = acc

    return pl.pallas_call(
        body, name=name, grid=(n,),
        in_specs=[pl.BlockSpec((t, c), lambda i: (i, 0)),
                  pl.BlockSpec((HALO, c), lambda i: (jnp.maximum(i * hb - 1, 0), 0)),
                  pl.BlockSpec((t, c), lambda i: (i, 0)),
                  pl.BlockSpec((HALO, c), lambda i: (jnp.minimum((i + 1) * hb, s // HALO - 1), 0)),
                  pl.BlockSpec(w8.shape, lambda i: (0, 0))],
        out_specs=[pl.BlockSpec((t, c), lambda i: (i, 0)), pl.BlockSpec(w8.shape, lambda i: (0, 0))],
        out_shape=[jax.ShapeDtypeStruct((s, c), F32), jax.ShapeDtypeStruct(w8.shape, F32)],
        scratch_shapes=[pltpu.VMEM((t + HALO, c), F32), pltpu.VMEM((t + HALO, c), F32)],
        compiler_params=_params(("arbitrary",)),
    )(proj, proj, dy, dy, w8)


_B_NN = (((2,), (1,)), ((0,), (0,)))
_B_NT = (((2,), (2,)), ((0,), (0,)))
_B_TN = (((1,), (1,)), ((0,), (0,)))


def _dot3(a, b, dims):
    return lax.dot_general(a, b, dims, precision=lax.Precision.HIGH, preferred_element_type=F32)


def _bdot_hi(a, b):
    return _dot3(a, b, _B_NN)


class _Dots:
    nn = staticmethod(lambda a, b: _dot3(a, b, _B_NN))
    nt = staticmethod(lambda a, b: _dot3(a, b, _B_NT))
    tn = staticmethod(lambda a, b: _dot3(a, b, _B_TN))


def _unit_lower_inverse(a, dots):
    c = a.shape[-1]
    ri, ci = _iota(a.shape, 1), _iota(a.shape, 2)
    inner = (ri // 2) == (ci // 2)
    t = (ri == ci).astype(F32) - jnp.where(inner, a, 0.0)
    blk = 4
    while blk <= c:
        outer = (ri // blk) == (ci // blk)
        low = jnp.where(outer & jnp.logical_not(inner), a, 0.0)
        t = t - dots.nn(dots.nn(t, low), t)
        inner = outer
        blk *= 2
    return t


def _stack(xs):
    return jnp.concatenate([x[None] for x in xs], axis=0)


def _gdn_local(dots, q, k, v, gbs):
    b, c, _ = q.shape
    gcols, bcols = [], []
    for gb in gbs:
        lane = _iota(gb.shape, 1)
        for h in range(GDN_HEADS):
            gcols.append(jnp.sum(jnp.where(lane == h, gb, 0.0), axis=1, keepdims=True))
            bcols.append(jnp.sum(jnp.where(lane == GDN_HEADS + h, gb, 0.0), axis=1, keepdims=True))
    gcol, bcol = _stack(gcols), _stack(bcols)
    ri, ci = _iota((b, c, c), 1), _iota((b, c, c), 2)
    incl = ri >= ci
    tril = incl.astype(F32)
    g_cc = _bdot_hi(tril, jnp.broadcast_to(gcol, (b, c, c)))
    g_row = _bdot_hi(jnp.ones((b, c, c), F32), jnp.where(ri == ci, g_cc, 0.0))
    g_cl = _bdot_hi(tril, jnp.broadcast_to(gcol, (b, c, HEAD)))
    g_last = jnp.sum(jnp.broadcast_to(gcol, (b, c, HEAD)), axis=1, keepdims=True)
    decay = jnp.where(incl, jnp.exp(jnp.where(incl, g_cc - g_row, 0.0)), 0.0)
    kk = dots.nt(k, k)
    minv = _unit_lower_inverse(jnp.where(ri > ci, bcol * kk * decay, 0.0), dots)
    e_g = jnp.exp(g_cl)
    u = dots.nn(minv, v * bcol)
    wk = dots.nn(minv, k * (bcol * e_g))
    qk = dots.nt(q, k) * decay
    return u, wk, q * e_g, k * jnp.exp(g_last - g_cl), qk, jnp.exp(g_last)


def _gdn_scan(dots, states, u, wk, qd, kd, qk, gl_tile):
    lane, row = _iota(gl_tile.shape, 1), _iota(gl_tile.shape, 0)
    gl = _stack([
        jnp.sum(jnp.sum(jnp.where((lane == h) & (row == 0), gl_tile, 0.0), axis=1, keepdims=True),
                axis=0, keepdims=True) for h in range(GDN_HEADS)])
    v_new = u - dots.nn(wk, states)
    o = dots.nn(qd, states) + dots.nn(qk, v_new)
    return states * gl + dots.tn(kd, v_new), o


def _heads(x):
    return jnp.stack(_split(x, HW4))


GDN_W = GDN_HEADS * HEAD
HW4 = [HEAD] * GDN_HEADS
LOCAL_CHUNKS = 4
_CHUNK_ROWS = [pl.ds(cc * CHUNK, CHUNK) for cc in range(LOCAL_CHUNKS)]


def _chunk_heads(ref):
    return jnp.concatenate([_heads(ref[rows, :]) for rows in _CHUNK_ROWS], 0)


def _gdn_local_fwd(q, k, v, gb, name):
    s = q.shape[0]
    t = LOCAL_CHUNKS * CHUNK

    def body(q_ref, k_ref, v_ref, gb_ref, u_ref, wk_ref, qd_ref, kd_ref, qk_ref, gl_ref):
        u, wk, qd, kd, qk, gl = _gdn_local(_Dots, _chunk_heads(q_ref), _chunk_heads(k_ref),
                                           _chunk_heads(v_ref), [gb_ref[rows, :] for rows in _CHUNK_ROWS])
        lane = _iota((CHUNK, LANES), 1)
        for cc, rows in enumerate(_CHUNK_ROWS):
            gl_tile = jnp.zeros((CHUNK, LANES), F32)
            for h in range(GDN_HEADS):
                b, cols = cc * GDN_HEADS + h, pl.ds(h * HEAD, HEAD)
                u_ref[rows, cols] = u[b]
                wk_ref[rows, cols] = wk[b]
                qd_ref[rows, cols] = qd[b]
                kd_ref[rows, cols] = kd[b]
                qk_ref[h, rows, :] = qk[b]
                gl_tile = gl_tile + jnp.where(lane == h, gl[b], 0.0)
            gl_ref[rows, :] = gl_tile

    row = pl.BlockSpec((t, GDN_W), lambda i: (i, 0))
    lane = pl.BlockSpec((t, LANES), lambda i: (i, 0))
    qks = pl.BlockSpec((GDN_HEADS, t, CHUNK), lambda i: (0, i, 0))
    return pl.pallas_call(
        body, name=name, grid=(s // t,),
        in_specs=[row, row, row, lane],
        out_specs=[row, row, row, row, qks, lane],
        out_shape=[jax.ShapeDtypeStruct((s, GDN_W), F32)] * 4
        + [jax.ShapeDtypeStruct((GDN_HEADS, s, CHUNK), F32), jax.ShapeDtypeStruct((s, LANES), F32)],
        compiler_params=_params(("parallel",)),
    )(q, k, v, gb)


def _gdn_local_bwd(q, k, v, gb, du, dwk, dqd, dkd, dqk, dgl, name):
    s = q.shape[0]
    t = LOCAL_CHUNKS * CHUNK

    def body(q_ref, k_ref, v_ref, gb_ref, du_ref, dwk_ref, dqd_ref, dkd_ref, dqk_ref, dgl_ref,
             dq_ref, dk_ref, dv_ref, dgb_ref):
        _, vjp = jax.vjp(functools.partial(_gdn_local, _Dots), _chunk_heads(q_ref), _chunk_heads(k_ref),
                         _chunk_heads(v_ref), [gb_ref[rows, :] for rows in _CHUNK_ROWS])
        lane = _iota((CHUNK, LANES), 1)
        dqk = jnp.stack([dqk_ref[h, rows, :] for rows in _CHUNK_ROWS for h in range(GDN_HEADS)])
        dgl = jnp.stack([jnp.sum(jnp.where(lane == h, dgl_ref[rows, :], 0.0), axis=0, keepdims=True)
                         for rows in _CHUNK_ROWS for h in range(GDN_HEADS)])
        d_q, d_k, d_v, d_gbs = vjp((_chunk_heads(du_ref), _chunk_heads(dwk_ref), _chunk_heads(dqd_ref),
                                    _chunk_heads(dkd_ref), dqk, dgl))
        for cc, rows in enumerate(_CHUNK_ROWS):
            for h in range(GDN_HEADS):
                b, cols = cc * GDN_HEADS + h, pl.ds(h * HEAD, HEAD)
                dq_ref[rows, cols] = d_q[b]
                dk_ref[rows, cols] = d_k[b]
                dv_ref[rows, cols] = d_v[b]
            dgb_ref[rows, :] = d_gbs[cc]

    row = pl.BlockSpec((t, GDN_W), lambda i: (i, 0))
    lane = pl.BlockSpec((t, LANES), lambda i: (i, 0))
    qks = pl.BlockSpec((GDN_HEADS, t, CHUNK), lambda i: (0, i, 0))
    return pl.pallas_call(
        body, name=name, grid=(s // t,),
        in_specs=[row, row, row, lane, row, row, row, row, qks, lane],
        out_specs=[row, row, row, lane],
        out_shape=[jax.ShapeDtypeStruct((s, GDN_W), F32)] * 3 + [jax.ShapeDtypeStruct((s, LANES), F32)],
        compiler_params=_params(("parallel",)),
    )(q, k, v, gb, du, dwk, dqd, dkd, dqk, dgl)


SCAN_CHUNKS = 4


def _scan_rows(s):
    k = min(SCAN_CHUNKS, s // CHUNK)
    return k * CHUNK, [pl.ds(cc * CHUNK, CHUNK) for cc in range(k)]


def _gdn_scan_fwd(u, wk, qd, kd, qk, gl, name):
    s = u.shape[0]
    nc = s // CHUNK
    t, chunk_rows = _scan_rows(s)

    def body(u_ref, wk_ref, qd_ref, kd_ref, qk_ref, gl_ref, o_ref, st_ref, state):
        i = pl.program_id(0)

        @pl.when(i == 0)
        def _():
            state[...] = jnp.zeros(state.shape, F32)

        st = state[...]
        for cc, rows in enumerate(chunk_rows):
            st_ref[cc] = st
            st, o = _gdn_scan(_Dots, st, _heads(u_ref[rows, :]), _heads(wk_ref[rows, :]), _heads(qd_ref[rows, :]),
                              _heads(kd_ref[rows, :]), qk_ref[:, rows, :], gl_ref[rows, :])
            o_ref[rows, :] = jnp.concatenate([o[h] for h in range(GDN_HEADS)], 1)
        state[...] = st

    row = pl.BlockSpec((t, GDN_W), lambda i: (i, 0))
    return pl.pallas_call(
        body, name=name, grid=(s // t,),
        in_specs=[row, row, row, row, pl.BlockSpec((GDN_HEADS, t, CHUNK), lambda i: (0, i, 0)),
                  pl.BlockSpec((t, LANES), lambda i: (i, 0))],
        out_specs=[row, pl.BlockSpec((len(chunk_rows), GDN_HEADS, HEAD, HEAD), lambda i: (i, 0, 0, 0))],
        out_shape=[jax.ShapeDtypeStruct((s, GDN_W), F32),
                   jax.ShapeDtypeStruct((nc, GDN_HEADS, HEAD, HEAD), F32)],
        scratch_shapes=[pltpu.VMEM((GDN_HEADS, HEAD, HEAD), F32)],
        compiler_params=_params(("arbitrary",)),
    )(u, wk, qd, kd, qk, gl)


def _gdn_scan_bwd(u, wk, qd, kd, qk, gl, st, do, name):
    s = u.shape[0]
    t, chunk_rows = _scan_rows(s)
    n = s // t

    def body(u_ref, wk_ref, qd_ref, kd_ref, qk_ref, gl_ref, st_ref, do_ref,
             du_ref, dwk_ref, dqd_ref, dkd_ref, dqk_ref, dgl_ref, dstate):
        i = pl.program_id(0)

        @pl.when(i == 0)
        def _():
            dstate[...] = jnp.zeros(dstate.shape, F32)

        unheads = lambda x: jnp.concatenate([x[h] for h in range(GDN_HEADS)], 1)
        ds = dstate[...]
        for cc in reversed(range(len(chunk_rows))):
            rows = chunk_rows[cc]
            _, vjp = jax.vjp(functools.partial(_gdn_scan, _Dots), st_ref[cc], _heads(u_ref[rows, :]),
                             _heads(wk_ref[rows, :]), _heads(qd_ref[rows, :]), _heads(kd_ref[rows, :]),
                             qk_ref[:, rows, :], gl_ref[rows, :])
            ds, d_u, d_wk, d_qd, d_kd, d_qk, d_gl = vjp((ds, _heads(do_ref[rows, :])))
            dqk_ref[:, rows, :] = d_qk
            du_ref[rows, :] = unheads(d_u)
            dwk_ref[rows, :] = unheads(d_wk)
            dqd_ref[rows, :] = unheads(d_qd)
            dkd_ref[rows, :] = unheads(d_kd)
            dgl_ref[rows, :] = d_gl
        dstate[...] = ds

    rev = lambda i: (n - 1 - i, 0)
    row = pl.BlockSpec((t, GDN_W), rev)
    lane = pl.BlockSpec((t, LANES), rev)
    qks = pl.BlockSpec((GDN_HEADS, t, CHUNK), lambda i: (0, n - 1 - i, 0))
    return pl.pallas_call(
        body, name=name, grid=(n,),
        in_specs=[row, row, row, row, qks, lane,
                  pl.BlockSpec((len(chunk_rows), GDN_HEADS, HEAD, HEAD), lambda i: (n - 1 - i, 0, 0, 0)), row],
        out_specs=[row, row, row, row, qks, lane],
        out_shape=[jax.ShapeDtypeStruct((s, GDN_W), F32)] * 4
        + [jax.ShapeDtypeStruct((GDN_HEADS, s, CHUNK), F32), jax.ShapeDtypeStruct((s, LANES), F32)],
        scratch_shapes=[pltpu.VMEM((GDN_HEADS, HEAD, HEAD), F32)],
        compiler_params=_params(("arbitrary",)),
    )(u, wk, qd, kd, qk, gl, st, do)


def _chunk_mask(i, j, t):
    r = i * t + _iota((t, t), 0)
    c = j * t + _iota((t, t), 1)
    return (r // CHUNK) >= (c // CHUNK)


ATT_TILE = 1024
ATT_Q_TILES = 1
ATT_BWD_TILE = 1024


def _attn_fwd(q, k, v, name, gather=()):
    nh, s = MLA_HEADS, q.shape[0]
    tk = min(ATT_TILE, s)
    tq = min(ATT_Q_TILES * tk, s)
    qk = tq // tk
    nq, n = s // tq, s // tk
    nt = (((1,), (1,)), ((), ()))
    host = _Hosted(gather)
    ng = host.n
    steps = nh * nq * n

    def body(*refs):
        q_ref, k_ref, v_ref = refs[:3]
        x_refs = refs[3:3 + ng]
        o_ref, lse_ref = refs[3 + ng:5 + ng]
        got_refs = refs[5 + ng:5 + 2 * ng]
        m_sc, l_sc, acc_sc = refs[5 + 2 * ng:8 + 2 * ng]
        sems = refs[8 + 2 * ng:]
        i, j = pl.program_id(1), pl.program_id(2)
        step_no = (pl.program_id(0) * nq + i) * n + j
        host.open(step_no, steps, x_refs, got_refs, sems)

        @pl.when(j == 0)
        def _():
            m_sc[...] = jnp.full(m_sc.shape, -jnp.inf, F32)
            l_sc[...] = jnp.zeros(l_sc.shape, F32)
            acc_sc[...] = jnp.zeros(acc_sc.shape, F32)

        def step(masked):
            sc = lax.dot_general(q_ref[...], k_ref[...], nt, preferred_element_type=F32)
            if masked:
                r = i * tq + _iota((tq, tk), 0)
                c = j * tk + _iota((tq, tk), 1)
                sc = jnp.where((r // CHUNK) >= (c // CHUNK), sc, -jnp.inf)
            m_prev = m_sc[:, :1]
            m_new = jnp.maximum(m_prev, jnp.max(sc, axis=1, keepdims=True))
            alpha = jnp.exp(m_prev - m_new)
            p = jnp.exp(sc - m_new)
            l_sc[...] = jnp.broadcast_to(alpha * l_sc[:, :1] + jnp.sum(p, axis=1, keepdims=True), l_sc.shape)
            acc_sc[...] = alpha * acc_sc[...] + jnp.dot(p.astype(BF16), v_ref[...], preferred_element_type=F32)
            m_sc[...] = jnp.broadcast_to(m_new, m_sc.shape)

        pl.when(j < i * qk)(lambda: step(False))
        pl.when(j // qk == i)(lambda: step(True))

        @pl.when(j == n - 1)
        def _():
            o_ref[...] = acc_sc[...] / l_sc[:, :1]
            lse_ref[...] = m_sc[...] + jnp.log(l_sc[...])

        host.close(step_no, steps, x_refs, got_refs, sems)

    qrow = lambda h, i, j: (i, h)
    krow = lambda h, i, j: (jnp.minimum(j, (i + 1) * qk - 1), h)
    res = pl.pallas_call(
        body, name=name, grid=(nh, nq, n),
        in_specs=[pl.BlockSpec((tq, QK_PAD), qrow), pl.BlockSpec((tk, QK_PAD), krow),
                  pl.BlockSpec((tk, HEAD), krow)] + host.specs(),
        out_specs=[pl.BlockSpec((tq, HEAD), qrow), pl.BlockSpec((None, tq, LANES), lambda h, i, j: (h, i, 0))]
        + host.specs(),
        out_shape=[jax.ShapeDtypeStruct((s, nh * HEAD), F32), jax.ShapeDtypeStruct((nh, s, LANES), F32)]
        + host.out_shapes(),
        scratch_shapes=[pltpu.VMEM((tq, LANES), F32), pltpu.VMEM((tq, LANES), F32), pltpu.VMEM((tq, HEAD), F32)]
        + host.scratch(),
        compiler_params=_params(("arbitrary",) * 3 if ng else ("parallel", "parallel", "arbitrary")),
    )(q, k, v, *host.operands)
    return res[0], res[1], list(res[2:])


def _attn_bwd(q, k, v, o, do, lse, name, scatter=()):
    nh, s = MLA_HEADS, q.shape[0]
    t = min(ATT_BWD_TILE, s)
    n = s // t
    tn = (((0,), (0,)), ((), ()))
    nt = (((1,), (1,)), ((), ()))
    host = _Hosted(scatter=scatter)
    nx = host.n
    steps = nh * n * n

    def body(*refs):
        q_ref, k_ref, v_ref, o_ref, do_ref, lse_ref = refs[:6]
        x_refs = refs[6:6 + nx]
        dq_ref, dk_ref, dv_ref = refs[6 + nx:9 + nx]
        got_refs = refs[9 + nx:9 + 2 * nx]
        dk_acc, dv_acc, dq_acc = refs[9 + 2 * nx:12 + 2 * nx]
        sems = refs[12 + 2 * nx:]
        j, i = pl.program_id(1), pl.program_id(2)
        step_no = (pl.program_id(0) * n + j) * n + i
        host.open(step_no, steps, x_refs, got_refs, sems)

        @pl.when(i + j == 0)
        def _():
            dq_acc[...] = jnp.zeros(dq_acc.shape, F32)

        @pl.when(i == 0)
        def _():
            dk_acc[...] = jnp.zeros(dk_acc.shape, F32)
            dv_acc[...] = jnp.zeros(dv_acc.shape, F32)

        def step(masked):
            qv, kv, do = q_ref[...], k_ref[...], do_ref[...]
            sc = lax.dot_general(qv, kv, nt, preferred_element_type=F32)
            p = jnp.exp(sc - lse_ref[:, :1])
            if masked:
                p = jnp.where(_chunk_mask(i, j, t), p, 0.0)
            dob = do.astype(BF16)
            dp = lax.dot_general(dob, v_ref[...], nt, preferred_element_type=F32)
            ds = (p * (dp - jnp.sum(do * o_ref[...], axis=1, keepdims=True))).astype(BF16)
            dv_acc[...] += lax.dot_general(p.astype(BF16), dob, tn, preferred_element_type=F32)
            dk_acc[...] += lax.dot_general(ds, qv, tn, preferred_element_type=F32)
            rows = pl.ds(pl.multiple_of(i * t, t), t)
            dq_acc[rows, :] += jnp.dot(ds, kv, preferred_element_type=F32)

        pl.when(i > j)(lambda: step(False))
        pl.when(i == j)(lambda: step(True))

        @pl.when(i == n - 1)
        def _():
            dk_ref[...] = dk_acc[...]
            dv_ref[...] = dv_acc[...]

        @pl.when(i + j == 2 * (n - 1))
        def _():
            dq_ref[...] = dq_acc[...]

        host.close(step_no, steps, x_refs, got_refs, sems)

    qrow = lambda h, j, i: (jnp.maximum(i, j), h)
    krow = lambda h, j, i: (j, h)
    res = pl.pallas_call(
        body, name=name, grid=(nh, n, n),
        in_specs=[pl.BlockSpec((t, QK_PAD), qrow), pl.BlockSpec((t, QK_PAD), krow), pl.BlockSpec((t, HEAD), krow),
                  pl.BlockSpec((t, HEAD), qrow), pl.BlockSpec((t, HEAD), qrow),
                  pl.BlockSpec((None, t, LANES), lambda h, j, i: (h, jnp.maximum(i, j), 0))] + host.specs(),
        out_specs=[pl.BlockSpec((s, QK_PAD), lambda h, j, i: (0, h)),
                   pl.BlockSpec((t, QK_PAD), krow), pl.BlockSpec((t, HEAD), krow)] + host.specs(),
        out_shape=[jax.ShapeDtypeStruct((s, nh * QK_PAD), F32), jax.ShapeDtypeStruct((s, nh * QK_PAD), F32),
                   jax.ShapeDtypeStruct((s, nh * HEAD), F32)] + host.out_shapes(),
        scratch_shapes=[pltpu.VMEM((t, QK_PAD), F32), pltpu.VMEM((t, HEAD), F32), pltpu.VMEM((s, QK_PAD), F32)]
        + host.scratch(),
        compiler_params=_params(("arbitrary", "arbitrary", "arbitrary")),
    )(q, k, v, o, do, lse, *host.operands)
    return res[0], res[1], res[2], list(res[3:])


def _place():
    return lax.axis_index("x"), lax.axis_index("y"), lax.axis_index("c")


def _allgather8(x, name):
    r, c = x.shape

    def body(x_ref, out_ref, send_sems, recv_sems, local_sem):
        mx, my, mc = _place()
        me = 4 * mx + 2 * my + mc
        mine = pltpu.make_async_copy(x_ref, out_ref.at[me], local_sem)
        mine.start()
        copies = []
        for d in range(1, 8):
            px = 1 - mx if d & 4 else mx
            py = 1 - my if d & 2 else my
            pc = 1 - mc if d & 1 else mc
            cp = pltpu.make_async_remote_copy(
                src_ref=x_ref, dst_ref=out_ref.at[me], send_sem=send_sems.at[d - 1], recv_sem=recv_sems.at[d - 1],
                device_id=(px, py, pc), device_id_type=MESH)
            cp.start()
            copies.append(cp)
        for cp in copies:
            cp.wait()
        mine.wait()

    return pl.pallas_call(
        body, name=name,
        out_shape=jax.ShapeDtypeStruct((8, r, c), x.dtype),
        in_specs=[pl.BlockSpec(memory_space=pltpu.VMEM)],
        out_specs=pl.BlockSpec(memory_space=pltpu.VMEM),
        scratch_shapes=[pltpu.SemaphoreType.DMA((7,)), pltpu.SemaphoreType.DMA((7,)), pltpu.SemaphoreType.DMA],
        compiler_params=pltpu.CompilerParams(vmem_limit_bytes=VMEM_LIMIT),
    )(x)


def _allgather_chips(x, name):
    r, c = x.shape

    def body(x_ref, out_ref, send_sems, recv_sems, local_sems):
        for phase in range(3):
            _gather_phase(phase, x_ref, out_ref, send_sems, recv_sems, local_sems, 0)

    return pl.pallas_call(
        body, name=name,
        out_shape=jax.ShapeDtypeStruct((4, r, c), x.dtype),
        in_specs=[pl.BlockSpec(memory_space=pltpu.VMEM)],
        out_specs=pl.BlockSpec(memory_space=pltpu.VMEM),
        scratch_shapes=_gather_sems(1),
        compiler_params=pltpu.CompilerParams(vmem_limit_bytes=VMEM_LIMIT),
    )(x)


GATHER_COPIES = 6


def _gather_sems(n):
    return [pltpu.SemaphoreType.DMA((GATHER_COPIES * n,)), pltpu.SemaphoreType.DMA((GATHER_COPIES * n,)),
            pltpu.SemaphoreType.DMA((n,))]


def _gather_phase(phase, x_ref, out_ref, send_sems, recv_sems, local_sems, slot):
    mx, my, mc = _place()
    j = 2 * mx + my
    rh = x_ref.shape[0] // 2
    base = GATHER_COPIES * slot
    chips = [(1 - mx, my), (mx, 1 - my), (1 - mx, 1 - my)]
    sibling = (mx, my, 1 - mc)

    def half(jj, hc):
        return out_ref.at[jj, pl.ds(hc * rh, rh), :]

    def over_ici(kk, block):
        px, py = chips[kk]
        return pltpu.make_async_remote_copy(
            src_ref=x_ref.at[pl.ds(mc * rh, rh), :], dst_ref=half(block, mc), send_sem=send_sems.at[base + kk],
            recv_sem=recv_sems.at[base + kk], device_id=(px, py, mc), device_id_type=MESH)

    def to_sibling(kk, hc):
        px, py = chips[kk]
        blk = half(2 * px + py, hc)
        return pltpu.make_async_remote_copy(
            src_ref=blk, dst_ref=blk, send_sem=send_sems.at[base + 3 + kk], recv_sem=recv_sems.at[base + 3 + kk],
            device_id=sibling, device_id_type=MESH)

    mine = pltpu.make_async_copy(x_ref, out_ref.at[j], local_sems.at[slot])
    if phase == 0:
        mine.start()
        for kk in range(3):
            over_ici(kk, j).start()
    elif phase == 1:
        for kk, (px, py) in enumerate(chips):
            over_ici(kk, 2 * px + py).wait_recv()
            to_sibling(kk, mc).start()
    else:
        for kk in range(3):
            to_sibling(kk, 1 - mc).wait_recv()
        for kk in range(3):
            over_ici(kk, j).wait_send()
            to_sibling(kk, mc).wait_send()
        mine.wait()


RS_ROWS = 32


def _reduce_scatter_chips(g, name):
    _, r, c = g.shape
    rh = r // 2
    steps = rh // RS_ROWS

    def body(g_ref, out_ref, sib_ref, part_ref, got_ref, send_sems, recv_sems):
        mx, my, mc = _place()
        j = 2 * mx + my
        sibling = (mx, my, 1 - mc)
        chips = [(1 - mx, my), (mx, 1 - my), (1 - mx, 1 - my)]

        to_sib = pltpu.make_async_remote_copy(
            src_ref=g_ref.at[:, pl.ds((1 - mc) * rh, rh), :], dst_ref=sib_ref,
            send_sem=send_sems.at[0], recv_sem=recv_sems.at[0], device_id=sibling, device_id_type=MESH)
        to_sib.start()
        to_sib.wait()

        def add_sibling(step, carry):
            rows = pl.ds(pl.multiple_of(step * RS_ROWS, RS_ROWS), RS_ROWS)
            mine = g_ref[:, pl.ds(pl.multiple_of(mc * rh + step * RS_ROWS, RS_ROWS), RS_ROWS), :]
            part_ref[:, rows, :] = mine.astype(F32) + sib_ref[:, rows, :].astype(F32)
            return carry

        lax.fori_loop(0, steps, add_sibling, 0)

        def to_bf16(step, carry):
            rows = pl.ds(pl.multiple_of(step * RS_ROWS, RS_ROWS), RS_ROWS)
            sib_ref[:, rows, :] = part_ref[:, rows, :].astype(BF16)
            return carry

        lax.fori_loop(0, steps, to_bf16, 0)

        sends = []
        for kk, (px, py) in enumerate(chips):
            cp = pltpu.make_async_remote_copy(
                src_ref=sib_ref.at[2 * px + py], dst_ref=got_ref.at[kk],
                send_sem=send_sems.at[1 + kk], recv_sem=recv_sems.at[1 + kk],
                device_id=(px, py, mc), device_id_type=MESH)
            cp.start()
            sends.append(cp)
        for cp in sends:
            cp.wait()

        def total(step, carry):
            rows = pl.ds(pl.multiple_of(step * RS_ROWS, RS_ROWS), RS_ROWS)
            acc = part_ref[j, rows, :]
            for kk in range(3):
                acc = acc + got_ref[kk, rows, :].astype(F32)
            out_ref[pl.ds(pl.multiple_of(mc * rh + step * RS_ROWS, RS_ROWS), RS_ROWS), :] = acc
            return carry

        lax.fori_loop(0, steps, total, 0)

        done = pltpu.make_async_remote_copy(
            src_ref=out_ref.at[pl.ds(mc * rh, rh), :], dst_ref=out_ref.at[pl.ds(mc * rh, rh), :],
            send_sem=send_sems.at[4], recv_sem=recv_sems.at[4], device_id=sibling, device_id_type=MESH)
        done.start()
        done.wait_send()
        pltpu.make_async_remote_copy(
            src_ref=out_ref.at[pl.ds((1 - mc) * rh, rh), :], dst_ref=out_ref.at[pl.ds((1 - mc) * rh, rh), :],
            send_sem=send_sems.at[4], recv_sem=recv_sems.at[4], device_id=sibling, device_id_type=MESH).wait_recv()

    return pl.pallas_call(
        body, name=name,
        out_shape=jax.ShapeDtypeStruct((r, c), F32),
        in_specs=[pl.BlockSpec(memory_space=pltpu.VMEM)],
        out_specs=pl.BlockSpec(memory_space=pltpu.VMEM),
        scratch_shapes=[pltpu.VMEM((4, rh, c), BF16), pltpu.VMEM((4, rh, c), F32), pltpu.VMEM((3, rh, c), BF16),
                        pltpu.SemaphoreType.DMA((5,)), pltpu.SemaphoreType.DMA((5,))],
        compiler_params=pltpu.CompilerParams(vmem_limit_bytes=VMEM_LIMIT),
    )(g)


def _sum8(x, name):
    _, r, c = x.shape

    def body(x_ref, o_ref):
        acc = x_ref[0]
        for d in range(1, 8):
            acc = acc + x_ref[d]
        o_ref[...] = acc

    return pl.pallas_call(
        body, name=name, out_shape=jax.ShapeDtypeStruct((r, c), F32),
        in_specs=[pl.BlockSpec(memory_space=pltpu.VMEM)], out_specs=pl.BlockSpec(memory_space=pltpu.VMEM),
    )(x)


SCATTER_COPIES = 7


def _scatter_sems(n):
    return [pltpu.SemaphoreType.DMA((SCATTER_COPIES * n,)), pltpu.SemaphoreType.DMA((SCATTER_COPIES * n,))]


def _scatter_phase(phase, g_ref, got_ref, send_sems, recv_sems, slot):
    mx, my, mc = _place()
    rh = g_ref.shape[1] // 2
    base = SCATTER_COPIES * slot
    for d in range(1, 8):
        px = 1 - mx if d & 4 else mx
        py = 1 - my if d & 2 else my
        pc = 1 - mc if d & 1 else mc
        cp = pltpu.make_async_remote_copy(
            src_ref=g_ref.at[2 * px + py, pl.ds(pc * rh, rh), :], dst_ref=got_ref.at[d - 1],
            send_sem=send_sems.at[base + d - 1], recv_sem=recv_sems.at[base + d - 1],
            device_id=(px, py, pc), device_id_type=MESH)
        if phase == 0:
            cp.start()
        else:
            cp.wait()


def _scatter_sum(g, got, name):
    mx, my, mc = _place()
    rh, c = got.shape[1], got.shape[2]
    mine = lax.dynamic_slice(g, (2 * mx + my, mc * rh, 0), (1, rh, c))[0]
    t = _pick(rh, 256, 16)

    def body(m_ref, got_ref, o_ref):
        acc = m_ref[...].astype(F32)
        for d in range(SCATTER_COPIES):
            acc = acc + got_ref[d].astype(F32)
        o_ref[...] = acc

    return pl.pallas_call(
        body, name=name, grid=(rh // t,),
        in_specs=[pl.BlockSpec((t, c), lambda i: (i, 0)), pl.BlockSpec((SCATTER_COPIES, t, c), lambda i: (0, i, 0))],
        out_specs=pl.BlockSpec((t, c), lambda i: (i, 0)),
        out_shape=jax.ShapeDtypeStruct((rh, c), F32), compiler_params=_params(("parallel",)),
    )(mine, got)


def _scatter_finish(halves, name):
    n = len(halves)

    def body(*refs):
        h_refs, o_refs = refs[:n], refs[n:2 * n]
        send_sems, recv_sems = refs[2 * n:]
        mx, my, mc = _place()
        copies = [pltpu.make_async_remote_copy(
            src_ref=h_refs[kk], dst_ref=o_refs[kk], send_sem=send_sems.at[kk], recv_sem=recv_sems.at[kk],
            device_id=(mx, my, 1 - mc), device_id_type=MESH) for kk in range(n)]
        for cp in copies:
            cp.start()
        for cp in copies:
            cp.wait()

    hbm = pl.BlockSpec(memory_space=pl.ANY)
    theirs = pl.pallas_call(
        body, name=name, in_specs=[hbm] * n, out_specs=[hbm] * n,
        out_shape=[jax.ShapeDtypeStruct(h.shape, F32) for h in halves],
        scratch_shapes=[pltpu.SemaphoreType.DMA((n,)), pltpu.SemaphoreType.DMA((n,))],
    )(*halves)
    south = lax.axis_index("c") == 0
    return [jnp.concatenate([jnp.where(south, m, t), jnp.where(south, t, m)], axis=0) for m, t in zip(halves, theirs)]


def _adamw(w, g, m, v, name):
    r, c = w.shape
    t = _pick(r, 256, SUBLANES)
    spec = pl.BlockSpec((t, c), lambda i: (i, 0))

    def body(w_ref, g_ref, m_ref, v_ref, d_ref, nm_ref, nv_ref):
        gv = g_ref[...]
        m_new = ADAM_B1 * m_ref[...] + (1.0 - ADAM_B1) * gv
        v_new = ADAM_B2 * v_ref[...] + (1.0 - ADAM_B2) * (gv * gv)
        m_hat = m_new / (1.0 - ADAM_B1 ** ADAM_STEP)
        v_hat = v_new / (1.0 - ADAM_B2 ** ADAM_STEP)
        d_ref[...] = -ADAM_LR * (m_hat / (jnp.sqrt(v_hat) + ADAM_EPS) + ADAM_WD * w_ref[...])
        nm_ref[...] = m_new
        nv_ref[...] = v_new

    return pl.pallas_call(
        body, name=name, grid=(r // t,), in_specs=[spec] * 4, out_specs=[spec] * 3,
        out_shape=[jax.ShapeDtypeStruct((r, c), F32)] * 3, compiler_params=_params(("parallel",)),
    )(w, g, m, v)


def _pack_rows(parts):
    rows, offs, o = [], [], 0
    for p in parts:
        f = p.reshape(-1)
        n = -(-f.shape[0] // (LANES * SUBLANES)) * SUBLANES
        rows.append(jnp.pad(f, (0, n * LANES - f.shape[0])).reshape(n, LANES))
        offs.append((o, n))
        o += n
    return jnp.concatenate(rows, 0), offs


def _unpack_rows(packed, offs, shapes):
    out = []
    for (o, n), shp in zip(offs, shapes):
        size = 1
        for d in shp:
            size *= d
        out.append(packed[o:o + n].reshape(-1)[:size].reshape(shp))
    return out


def _mm_hosting(a, b, mode, out_dtype, name, gather=(), chips=None, scatter=()):
    res = _mm(a, b, mode, out_dtype, name, gather=gather, chips=chips, scatter=scatter)
    return (res[0], list(res[1:])) if (gather or scatter) else (res, [])


def _ffn_fwd(x, s, sh, g, w_in, w_out, tag, gather_in=(), gather_out=()):
    (h,) = _rowcall(lambda r, p: ([_modulate(r[0], p[0], p[1])], []), [x], [s, sh], [(x.shape[1], BF16)], [],
                    tile=ROW_TILE, name=tag + "_mod")
    gu, got = _mm_hosting(h, w_in, "nn", BF16, tag + "_in", gather_in, chips="b")
    if w_out is None:
        first = got.pop(0)
        w_out = first.reshape(4 * first.shape[1], first.shape[2])
    (act,) = _rowcall(lambda r, p: ([_silu(r[0].astype(F32)) * r[1].astype(F32)], []),
                      [(gu, D_FF, 0), (gu, D_FF, 1)], [], [(D_FF, BF16)], [], tile=ROW_TILE_WIDE, name=tag + "_act")
    f, got_out = _mm_hosting(act, w_out, "nn", F32, tag + "_out", gather_out)
    got = got + got_out
    (y,) = _rowcall(lambda r, p: ([r[0] + 0.5 * p[0] * r[1]], []), [x, f], [g], [(x.shape[1], F32)], [],
                    tile=ROW_TILE, name=tag + "_res")
    return y, (x, h, gu, act, f), got, w_out


def _ffn_bwd(dy, saved, s, sh, g, w_in, w_out, tag, scatter_bin=(), scatter_bwin=()):
    x, h, gu, act, f = saved
    d = x.shape[1]
    df, dg = _rowcall(lambda r, p: ([0.5 * p[0] * r[0]], [0.5 * jnp.sum(r[0] * r[1], 0, keepdims=True)]),
                      [dy, f], [g], [(d, BF16)], [(1, d)], tile=ROW_TILE, name=tag + "_bres")
    da = _mm(df, w_out, "nt", BF16, tag + "_bout")
    dw_out = _mm(act, df, "tn", BF16, tag + "_bwout")

    def act_bwd(r, p):
        gate, up, dav = r[0].astype(F32), r[1].astype(F32), r[2].astype(F32)
        _, vjp = jax.vjp(lambda a, b: _silu(a) * b, gate, up)
        dgate, dup = vjp(dav)
        return [jnp.concatenate([dgate, dup], 1)], []

    (dgu,) = _rowcall(act_bwd, [(gu, D_FF, 0), (gu, D_FF, 1), da], [], [(2 * D_FF, BF16)], [], tile=ROW_TILE_WIDE,
                      name=tag + "_bact")
    dh, got_a = _mm_hosting(dgu, w_in, "nt", F32, tag + "_bin", chips="b", scatter=scatter_bin)
    dw_in, got_b = _mm_hosting(h, dgu, "tn", BF16, tag + "_bwin", chips="out", scatter=scatter_bwin)

    def mod_bwd(r, p):
        _, vjp = jax.vjp(_modulate, r[0], p[0], p[1])
        dx, ds, dsh = vjp(r[1])
        return [r[2] + dx], [ds, dsh]

    dx, ds, dsh = _rowcall(mod_bwd, [x, dh, dy], [s, sh], [(d, F32)], [(1, d), (1, d)], tile=ROW_TILE, name=tag + "_bmod")
    return dx, (dsh, ds, dg), dw_in, dw_out, list(got_a) + list(got_b)


def _mixer_fwd(x, s, sh, g, wts, rope, gather=()):
    w_in_p, conv8, a_log, dt_bias, wn, wq, w_uq_p, wkv, w_ukv, wqn, wqr, wkn, wkr, won, w_out = wts
    cos2, sin2 = rope
    d = x.shape[1]
    (h,) = _rowcall(lambda r, p: ([_modulate(r[0], p[0], p[1])], []), [x], [s, sh], [(d, BF16)], [],
                    tile=ROW_TILE, name="mix_mod")
    proj = _mm(h, w_in_p, "nn", F32, "mix_in")
    qkv_c = _conv_fwd(proj, conv8, "mix_conv")
    gab = (proj, LANES, 23)

    q, k, v, gb = _rowcall(
        lambda r, p: (list(_gdn_prep_core(_split(r[0], [HEAD] * 12), r[1], p[0], p[1])), []),
        [qkv_c, gab], [a_log, dt_bias], [(512, F32)] * 3 + [(LANES, F32)], [], tile=ROW_TILE_WIDE, name="mix_gdn_prep")
    gdn_local = _gdn_local_fwd(q, k, v, gb, "mix_gdn_local")
    o_gdn, gdn_states = _gdn_scan_fwd(*gdn_local, "mix_gdn_scan")
    states = (gdn_local, gdn_states)

    cq, ckv, kr = (proj, 512, 4), (proj, 256, 10), (proj, LANES, 22)
    cqn, ckvn, k_rope = _rowcall(
        lambda r, p: (list(_mla_prep_core(r[0][:, :MLA_Q_LORA], r[1], r[2], r[3], r[4], p[0], p[1], p[2])), []),
        [cq, ckv, kr, cos2, sin2], [wq, wkv, wkr], [(MLA_Q_LORA, BF16), (MLA_KV_LORA, BF16), (LANES, F32)], [],
        tile=ROW_TILE, name="mix_mla_prep")
    qf = _mm(cqn, w_uq_p, "nn", F32, "mix_uq")
    kvf = _mm(ckvn, w_ukv, "nn", F32, "mix_ukv")

    def qk_prep(r, p):
        qparts = _split(r[0], [HEAD] * 8)
        kvparts = _split(r[1], [HEAD] * 8)
        qs, ks, vs = _qk_prep_core(qparts[:4], qparts[4:], kvparts[0::2], kvparts[1::2], r[2], r[3], r[4],
                                   p[0], p[1], p[2])
        return [jnp.concatenate(qs, 1), jnp.concatenate(ks, 1), jnp.concatenate(vs, 1)], []

    qa, ka, va = _rowcall(qk_prep, [qf, kvf, k_rope, cos2, sin2], [wqn, wqr, wkn],
                          [(4 * QK_PAD, BF16), (4 * QK_PAD, BF16), (4 * HEAD, BF16)], [], tile=ROW_TILE_WIDE,
                          name="mix_qk_prep")
    o_b, lse, got = _attn_fwd(qa, ka, va, "mix_attn", gather=gather)
    if w_out is None:
        first = got.pop(0)
        w_out = first.reshape(4 * first.shape[1], first.shape[2])

    gz = (proj, 512, 3)
    (mixed,) = _rowcall(
        lambda r, p: ([_mix_post_core(_split(r[0], HW4), _split(r[1], HW4), _split(r[2], HW4), p[0], p[1])], []),
        [o_gdn, gz, o_b], [wn, won], [(2 * 512, BF16)], [], tile=ROW_TILE, name="mix_post")
    y = _mm(mixed, w_out, "nn", F32, "mix_out")
    (x_out,) = _rowcall(lambda r, p: ([r[0] + p[0] * r[1]], []), [x, y], [g], [(d, F32)], [], tile=ROW_TILE,
                        name="mix_res")
    saved = (x, h, proj, qkv_c, q, k, v, gb, states, o_gdn, cqn, ckvn, k_rope, qf, kvf, qa, ka, va, o_b, lse,
             mixed, y)
    return x_out, saved, got, w_out


def _mixer_bwd(dy, saved, s, sh, g, wts, rope, scatter=()):
    w_in_p, conv8, a_log, dt_bias, wn, wq, w_uq_p, wkv, w_ukv, wqn, wqr, wkn, wkr, won, w_out = wts
    cos2, sin2 = rope
    (x, h, proj, qkv_c, q, k, v, gb, states, o_gdn, cqn, ckvn, k_rope, qf, kvf, qa, ka, va, o_b, lse,
     mixed, y) = saved
    d = x.shape[1]
    dyb, dg = _rowcall(lambda r, p: ([p[0] * r[0]], [jnp.sum(r[0] * r[1], 0, keepdims=True)]),
                       [dy, y], [g], [(d, BF16)], [(1, d)], tile=ROW_TILE, name="mix_bres")
    dmixed = _mm(dyb, w_out, "nt", F32, "mix_bout")
    dw_out = _mm(mixed, dyb, "tn", BF16, "mix_bwout")

    gz = (proj, 512, 3)

    def post_bwd(r, p):
        _, vjp = jax.vjp(_mix_post_core, _split(r[0], HW4), _split(r[1], HW4), _split(r[2], HW4), p[0], p[1])
        do, dz, dob, dwn, dwon = vjp(r[3])
        return [jnp.concatenate(do, 1), jnp.concatenate(dz, 1), jnp.concatenate(dob, 1)], [dwn, dwon]

    do_gdn, dgz, do_b, dwn, dwon = _rowcall(post_bwd, [o_gdn, gz, o_b, dmixed], [wn, won], [(512, F32)] * 3,
                                            [(1, HEAD), (1, HEAD)], tile=ROW_TILE_WIDE, name="mix_bpost")

    dqa, dka, dva, got = _attn_bwd(qa, ka, va, o_b, do_b, lse, "mix_battn", scatter=scatter)

    def qk_bwd(r, p):
        qparts = _split(r[0], [HEAD] * 8)
        kvparts = _split(r[1], [HEAD] * 8)
        _, vjp = jax.vjp(_qk_prep_core, qparts[:4], qparts[4:], kvparts[0::2], kvparts[1::2], r[2], r[3], r[4],
                         p[0], p[1], p[2])
        cot = (_split(r[5], [QK_PAD] * 4), _split(r[6], [QK_PAD] * 4), _split(r[7], HW4))
        dqn, dqr, dkn, dvp, dkrope, _, _, dwqn, dwqr, dwkn = vjp(cot)
        dkv = []
        for a, b in zip(dkn, dvp):
            dkv += [a, b]
        return [jnp.concatenate(list(dqn) + list(dqr), 1), jnp.concatenate(dkv, 1), dkrope], [dwqn, dwqr, dwkn]

    dqf, dkvf, dk_rope, dwqn, dwqr, dwkn = _rowcall(
        qk_bwd, [qf, kvf, k_rope, cos2, sin2, dqa, dka, dva], [wqn, wqr, wkn],
        [(8 * HEAD, BF16), (8 * HEAD, BF16), (LANES, F32)], [(1, HEAD)] * 3, tile=ROW_TILE_WIDE, name="mix_bqk_prep")
    dcqn = _mm(dqf, w_uq_p, "nt", F32, "mix_buq")
    dw_uq_p = _mm(cqn, dqf, "tn", F32, "mix_bwuq")
    dckvn = _mm(dkvf, w_ukv, "nt", F32, "mix_bukv")
    dw_ukv = _mm(ckvn, dkvf, "tn", F32, "mix_bwukv")

    cq, ckv, kr = (proj, 512, 4), (proj, 256, 10), (proj, LANES, 22)

    def mla_bwd(r, p):
        _, vjp = jax.vjp(_mla_prep_core, r[0][:, :MLA_Q_LORA], r[1], r[2], r[3], r[4], p[0], p[1], p[2])
        dcq, dckv, dkr, _, _, dwq, dwkv, dwkr = vjp((r[5], r[6], r[7]))
        pad = jnp.zeros((dcq.shape[0], 512 - MLA_Q_LORA), F32)
        return [jnp.concatenate([dcq, pad], 1), dckv, dkr], [dwq, dwkv, dwkr]

    dcq, dckv, dkr, dwq, dwkv, dwkr = _rowcall(
        mla_bwd, [cq, ckv, kr, cos2, sin2, dcqn, dckvn, dk_rope], [wq, wkv, wkr],
        [(512, F32), (MLA_KV_LORA, F32), (LANES, F32)], [(1, MLA_Q_LORA), (1, MLA_KV_LORA), (1, LANES)],
        tile=ROW_TILE, name="mix_bmla_prep")

    gdn_local, gdn_states = states
    d_local = _gdn_scan_bwd(*gdn_local, gdn_states, do_gdn, "mix_bgdn_scan")
    dq, dk, dv, dgb = _gdn_local_bwd(q, k, v, gb, *d_local, "mix_bgdn_local")
    gab = (proj, LANES, 23)

    def gdn_prep_bwd(r, p):
        _, vjp = jax.vjp(_gdn_prep_core, _split(r[0], [HEAD] * 12), r[1], p[0], p[1])
        dparts, dgab, da_log, ddt = vjp((r[2], r[3], r[4], r[5]))
        return [jnp.concatenate(dparts, 1), dgab], [da_log, ddt]

    dqkv_c, dgab, da_log, ddt = _rowcall(gdn_prep_bwd, [qkv_c, gab, dq, dk, dv, dgb], [a_log, dt_bias],
                                         [(1536, F32), (LANES, F32)], [(1, LANES), (1, LANES)], tile=ROW_TILE_WIDE,
                                         name="mix_bgdn_prep")
    dqkv_pre, dconv8 = _conv_bwd(proj, dqkv_c, conv8, "mix_bconv")

    dproj = jnp.concatenate([dqkv_pre.astype(BF16), dgz.astype(BF16), dcq.astype(BF16), dckv.astype(BF16),
                             dkr.astype(BF16), dgab.astype(BF16)], axis=1)
    dh = _mm(dproj, w_in_p, "nt", F32, "mix_bin")
    dw_in_p = _mm(h, dproj, "tn", F32, "mix_bwin")

    def mod_bwd(r, p):
        _, vjp = jax.vjp(_modulate, r[0], p[0], p[1])
        dx, ds, dsh = vjp(r[1])
        return [r[2] + dx], [ds, dsh]

    dx, ds, dsh = _rowcall(mod_bwd, [x, dh, dy], [s, sh], [(d, F32)], [(1, d), (1, d)], tile=ROW_TILE, name="mix_bmod")
    small = dict(conv=dconv8, a_log=da_log, dt=ddt, wn=dwn, wq=dwq, wkv=dwkv, wqn=dwqn, wqr=dwqr, wkn=dwkn,
                 wkr=dwkr, won=dwon)
    return dx, (dsh, ds, dg), dw_in_p, dw_uq_p, dw_ukv, dw_out, small, got


def _pad_cols(a, n):
    return jnp.pad(a, ((0, 0),) * (a.ndim - 1) + ((0, n - a.shape[-1]),))


def _pack_w_in(w):
    z = lambda n: jnp.zeros((w.shape[0], n), w.dtype)
    return jnp.concatenate([w[:, 0:2048], w[:, 2056:2440], z(128), w[:, 2440:2696], w[:, 2696:2760], z(64),
                            w[:, 2048:2056], z(120)], axis=1)


def _unpack_w_in(wp):
    return jnp.concatenate([wp[:, 0:2048], wp[:, 2944:2952], wp[:, 2048:2432], wp[:, 2560:2816], wp[:, 2816:2880]],
                           axis=1)


def _pack_w_uq(w):
    z = jnp.zeros((w.shape[0], LANES - MLA_ROPE), w.dtype)
    nope = [w[:, h * 192:h * 192 + HEAD] for h in range(MLA_HEADS)]
    rope = []
    for h in range(MLA_HEADS):
        rope += [w[:, h * 192 + HEAD:(h + 1) * 192], z]
    return jnp.concatenate(nope + rope, axis=1)


def _unpack_w_uq(wp):
    cols = []
    for h in range(MLA_HEADS):
        cols += [wp[:, h * HEAD:(h + 1) * HEAD], wp[:, 512 + h * LANES:512 + h * LANES + MLA_ROPE]]
    return jnp.concatenate(cols, axis=1)


def _cols_to_chips(a):
    r, c = a.shape
    return a.reshape(r, 4, c // 4).transpose(1, 0, 2)


def _chips_to_cols(a):
    _, r, n = a.shape
    return a.transpose(1, 0, 2).reshape(r, 4 * n)


def _pad128(v, n=LANES):
    return _pad_cols(v.reshape(1, -1), n)


def kernel(x, c, positions, w_ada, b_ada, ffn1_w_in, ffn1_w_out, w_in, gdn_conv_w, gdn_a_log, gdn_dt_bias, gdn_norm_w, mla_q_norm_w, mla_w_uq, mla_kv_norm_w, mla_w_ukv, qkn_q_nope, qkn_q_rope, qkn_k_nope, qkn_k_rope, mla_out_norm_w, w_out, ffn2_w_in, ffn2_w_out, loss_target, m_w_ada, m_b_ada, m_ffn1_w_in, m_ffn1_w_out, m_w_in, m_gdn_conv_w, m_gdn_a_log, m_gdn_dt_bias, m_gdn_norm_w, m_mla_q_norm_w, m_mla_w_uq, m_mla_kv_norm_w, m_mla_w_ukv, m_qkn_q_nope, m_qkn_q_rope, m_qkn_k_nope, m_qkn_k_rope, m_mla_out_norm_w, m_w_out, m_ffn2_w_in, m_ffn2_w_out, v_w_ada, v_b_ada, v_ffn1_w_in, v_ffn1_w_out, v_w_in, v_gdn_conv_w, v_gdn_a_log, v_gdn_dt_bias, v_gdn_norm_w, v_mla_q_norm_w, v_mla_w_uq, v_mla_kv_norm_w, v_mla_w_ukv, v_qkn_q_nope, v_qkn_q_rope, v_qkn_k_nope, v_qkn_k_rope, v_mla_out_norm_w, v_w_out, v_ffn2_w_in, v_ffn2_w_out):
    weights = dict(w_ada=w_ada, b_ada=b_ada, ffn1_w_in=ffn1_w_in, ffn1_w_out=ffn1_w_out, w_in=w_in,
                   gdn_conv_w=gdn_conv_w, gdn_a_log=gdn_a_log, gdn_dt_bias=gdn_dt_bias, gdn_norm_w=gdn_norm_w,
                   mla_q_norm_w=mla_q_norm_w, mla_w_uq=mla_w_uq, mla_kv_norm_w=mla_kv_norm_w, mla_w_ukv=mla_w_ukv,
                   qkn_q_nope=qkn_q_nope, qkn_q_rope=qkn_q_rope, qkn_k_nope=qkn_k_nope, qkn_k_rope=qkn_k_rope,
                   mla_out_norm_w=mla_out_norm_w, w_out=w_out, ffn2_w_in=ffn2_w_in, ffn2_w_out=ffn2_w_out)
    moms_m = dict(w_ada=m_w_ada, b_ada=m_b_ada, ffn1_w_in=m_ffn1_w_in, ffn1_w_out=m_ffn1_w_out, w_in=m_w_in,
                  gdn_conv_w=m_gdn_conv_w, gdn_a_log=m_gdn_a_log, gdn_dt_bias=m_gdn_dt_bias,
                  gdn_norm_w=m_gdn_norm_w, mla_q_norm_w=m_mla_q_norm_w, mla_w_uq=m_mla_w_uq,
                  mla_kv_norm_w=m_mla_kv_norm_w, mla_w_ukv=m_mla_w_ukv, qkn_q_nope=m_qkn_q_nope,
                  qkn_q_rope=m_qkn_q_rope, qkn_k_nope=m_qkn_k_nope, qkn_k_rope=m_qkn_k_rope,
                  mla_out_norm_w=m_mla_out_norm_w, w_out=m_w_out, ffn2_w_in=m_ffn2_w_in, ffn2_w_out=m_ffn2_w_out)
    moms_v = dict(w_ada=v_w_ada, b_ada=v_b_ada, ffn1_w_in=v_ffn1_w_in, ffn1_w_out=v_ffn1_w_out, w_in=v_w_in,
                  gdn_conv_w=v_gdn_conv_w, gdn_a_log=v_gdn_a_log, gdn_dt_bias=v_gdn_dt_bias,
                  gdn_norm_w=v_gdn_norm_w, mla_q_norm_w=v_mla_q_norm_w, mla_w_uq=v_mla_w_uq,
                  mla_kv_norm_w=v_mla_kv_norm_w, mla_w_ukv=v_mla_w_ukv, qkn_q_nope=v_qkn_q_nope,
                  qkn_q_rope=v_qkn_q_rope, qkn_k_nope=v_qkn_k_nope, qkn_k_rope=v_qkn_k_rope,
                  mla_out_norm_w=v_mla_out_norm_w, w_out=v_w_out, ffn2_w_in=v_ffn2_w_in, ffn2_w_out=v_ffn2_w_out)
    names = list(weights)

    seq, d = x.shape[1], x.shape[2]
    x2d = x.reshape(seq, d)
    tgt = loss_target.reshape(seq, d)
    mx, my, mc = _place()
    chip = 2 * mx + my
    me = 2 * chip + mc
    n_mod = b_ada.shape[1] // d
    shard = w_ada.shape[2]

    half = MLA_ROPE // 2
    inv_freq = 10000.0 ** (-jnp.arange(half, dtype=F32) / half)
    ang = positions.astype(F32).reshape(seq, 1) * inv_freq
    cosv, sinv = jnp.cos(ang), jnp.sin(ang)
    cos2 = _pad_cols(jnp.concatenate([cosv, cosv], 1), LANES)
    sin2 = _pad_cols(jnp.concatenate([-sinv, sinv], 1), LANES)
    rope = (cos2, sin2)

    c_all = _allgather8(jnp.pad(c, ((0, SUBLANES - 1), (0, 0))), "gather_c")[:, 0, :]
    (sc_all,) = _rowcall(lambda r, p: ([_silu(r[0])], []), [c_all], [], [(d, F32)], [], tile=8, name="ada_silu")
    mod_part = _mm(sc_all, w_ada[0], "nn", F32, "ada_mm", hi=True)
    mod_all = _allgather8(mod_part, "gather_mod")
    mod_rows = lax.dynamic_index_in_dim(mod_all, me, axis=1, keepdims=False)
    mod_raw = jnp.concatenate([mod_rows[2 * jj] for jj in range(4)], axis=0).reshape(1, 4 * shard)
    (mod,) = _rowcall(lambda r, p: ([r[0] + r[1]], []),
                      [jnp.pad(mod_raw, ((0, 7), (0, 0))), jnp.pad(b_ada, ((0, 7), (0, 0)))], [],
                      [(4 * shard, F32)], [], tile=8, name="ada_bias")
    mods = [mod[0:1, i * d:(i + 1) * d] for i in range(n_mod)]
    sh1, s1, g1, sh2, s2, g2, sh3, s3, g3 = mods

    def shard_bf16(w, pad_to=None):
        w2 = w[0].astype(BF16)
        return _pad_cols(w2, pad_to) if pad_to else w2

    def cols_of(got, w):
        return _chips_to_cols(got[:, :, :w.shape[2]])

    def rows_of(got):
        return got.reshape(4 * got.shape[1], got.shape[2])

    f1_in = _allgather_chips(shard_bf16(ffn1_w_in), "gather_f1_in")
    conv_all = _allgather8(jnp.pad(gdn_conv_w[0], ((0, SUBLANES - CONV_K), (0, 0))), "gather_conv")
    conv8 = jnp.concatenate([conv_all[2 * jj] for jj in range(4)], axis=1)

    x1, sv1, got, f1_out = _ffn_fwd(
        x2d, s1, sh1, g1, f1_in, None, "ffn1",
        gather_in=[shard_bf16(ffn1_w_out), shard_bf16(w_in, 768), shard_bf16(mla_w_uq, 256), shard_bf16(mla_w_ukv)])
    w_in_full, w_uq_full, w_ukv_full = cols_of(got[0], w_in), cols_of(got[1], mla_w_uq), cols_of(got[2], mla_w_ukv)
    wts = (_pack_w_in(w_in_full), conv8, _pad128(gdn_a_log), _pad128(gdn_dt_bias), gdn_norm_w,
           mla_q_norm_w, _pack_w_uq(w_uq_full), mla_kv_norm_w, w_ukv_full, qkn_q_nope, _pad128(qkn_q_rope),
           qkn_k_nope, _pad128(qkn_k_rope), mla_out_norm_w, None)
    xm, svm, got, w_out_full = _mixer_fwd(
        x1, s2, sh2, g2, wts, rope, gather=[shard_bf16(w_out), shard_bf16(ffn2_w_in), shard_bf16(ffn2_w_out)])
    wts = wts[:-1] + (w_out_full,)
    f2_in, f2_out = got[0], rows_of(got[1])
    x3, sv3, _, _ = _ffn_fwd(xm, s3, sh3, g3, f2_in, f2_out, "ffn2")

    def loss_fn(r, p):
        err = r[0] - r[1]
        part = 0.5 * jnp.sum(jnp.sum(err * err, axis=1, keepdims=True) * (1.0 / d), axis=0, keepdims=True)
        return [err * (1.0 / d)], [jnp.broadcast_to(part, (1, LANES))]

    dy, loss_part = _rowcall(loss_fn, [x3, tgt], [], [(d, F32)], [(1, LANES)], tile=ROW_TILE, name="loss")
    loss = lax.psum(loss_part[0, 0], ("x", "y", "c"))

    def chip_cols(dw, pad_to=None):
        g4 = _cols_to_chips(dw).astype(BF16)
        return _pad_cols(g4, pad_to) if pad_to else g4

    def chip_rows(dw):
        return dw.astype(BF16).reshape(4, dw.shape[0] // 4, dw.shape[1])

    dxm, dmod3, dw_f2_in, dw_f2_out, _ = _ffn_bwd(dy, sv3, s3, sh3, g3, f2_in, f2_out, "ffn2")
    parts2 = [dw_f2_in, chip_rows(dw_f2_out)]
    dx1, dmod2, dw_in_p, dw_uq_p, dw_ukv, dw_out_m, small, got2 = _mixer_bwd(dxm, svm, s2, sh2, g2, wts, rope,
                                                                             scatter=parts2)
    parts_m = [chip_cols(_unpack_w_in(dw_in_p), 768), chip_cols(_unpack_w_uq(dw_uq_p), 256), chip_cols(dw_ukv),
               chip_rows(dw_out_m)]
    dx0, dmod1, dw_f1_in, dw_f1_out, got_m = _ffn_bwd(dx1, sv1, s1, sh1, g1, f1_in, f1_out, "ffn1",
                                                      scatter_bin=parts_m[:1], scatter_bwin=parts_m[1:])
    grad_x = dx0.reshape(x.shape)

    dmod = jnp.concatenate(list(dmod1) + list(dmod2) + list(dmod3), axis=1)
    small_parts = [dmod, small["conv"][:CONV_K], small["a_log"], small["dt"], small["wn"], small["wq"],
                   small["wkv"], small["wqn"], small["wqr"], small["wkn"], small["wkr"], small["won"]]
    packed, offs = _pack_rows(small_parts)
    gathered = _allgather8(packed, "gather_small")
    total = _sum8(gathered, "sum_small")
    (g_b_ada, g_conv, g_a_log, g_dt, g_wn, g_wq, g_wkv, g_wqn, g_wqr, g_wkn, g_wkr, g_won) = _unpack_rows(
        total, offs, [p.shape for p in small_parts])
    dmod_all = _unpack_rows(gathered.reshape(-1, LANES),
                            [(dd * packed.shape[0] + offs[0][0], offs[0][1]) for dd in range(8)],
                            [dmod.shape] * 8)
    dmod_all = jnp.concatenate(dmod_all, axis=0)
    dmod_mine = lax.dynamic_slice_in_dim(dmod_all, chip * shard, shard, axis=1)

    def ada_grad(r, p):
        acc = jnp.zeros((r[0].shape[0], shard), F32)
        for b in range(8):
            acc = acc + r[0][:, b:b + 1] * p[0][b:b + 1, :]
        return [acc], []

    (g_w_ada,) = _rowcall(ada_grad, [_pad_cols(sc_all.T, LANES)], [dmod_mine], [(shard, F32)], [], tile=ROW_TILE_WIDE,
                          name="ada_grad")

    grads = dict(
        w_ada=g_w_ada[None], b_ada=g_b_ada,
        gdn_conv_w=lax.dynamic_slice_in_dim(g_conv, chip * gdn_conv_w.shape[2], gdn_conv_w.shape[2], axis=1)[None],
        gdn_a_log=g_a_log[:, :GDN_HEADS], gdn_dt_bias=g_dt[:, :GDN_HEADS], gdn_norm_w=g_wn, mla_q_norm_w=g_wq,
        mla_kv_norm_w=g_wkv, qkn_q_nope=g_wqn, qkn_q_rope=g_wqr[:, :MLA_ROPE], qkn_k_nope=g_wkn,
        qkn_k_rope=g_wkr[:, :MLA_ROPE], mla_out_norm_w=g_won)

    hosted = ["ffn2_w_in", "ffn2_w_out", "w_in", "mla_w_uq", "mla_w_ukv", "w_out"]
    halves = [_scatter_sum(part, got, "rs_sum_" + nme)
              for nme, part, got in zip(hosted, parts2 + parts_m, got2 + got_m)]
    for nme, full in zip(hosted, _scatter_finish(halves, "rs_finish")):
        grads[nme] = full[:, :weights[nme].shape[2]][None]
    grads["ffn1_w_in"] = _reduce_scatter_chips(dw_f1_in, "rs_f1_in")[None]
    grads["ffn1_w_out"] = _reduce_scatter_chips(chip_rows(dw_f1_out), "rs_f1_out")[None]

    big = ["w_ada", "ffn1_w_in", "ffn1_w_out", "w_in", "mla_w_uq", "mla_w_ukv", "w_out", "ffn2_w_in", "ffn2_w_out"]
    delta, new_m, new_v = {}, {}, {}
    for nme in big:
        shp = weights[nme].shape
        dl, nm, nv = _adamw(weights[nme][0], grads[nme][0], moms_m[nme][0], moms_v[nme][0], "adamw_" + nme)
        delta[nme], new_m[nme], new_v[nme] = dl.reshape(shp), nm.reshape(shp), nv.reshape(shp)
    tiny = [nme for nme in names if nme not in big]
    shapes = [weights[nme].shape for nme in tiny]
    pw, poffs = _pack_rows([weights[nme] for nme in tiny])
    pg, _ = _pack_rows([grads[nme] for nme in tiny])
    pm, _ = _pack_rows([moms_m[nme] for nme in tiny])
    pv, _ = _pack_rows([moms_v[nme] for nme in tiny])
    pd, pnm, pnv = _adamw(pw, pg, pm, pv, "adamw_small")
    for nme, dl, nm, nv in zip(tiny, _unpack_rows(pd, poffs, shapes), _unpack_rows(pnm, poffs, shapes),
                               _unpack_rows(pnv, poffs, shapes)):
        delta[nme], new_m[nme], new_v[nme] = dl, nm, nv

    return (loss, grad_x, *[grads[nme].reshape(weights[nme].shape) for nme in names],
            *[delta[nme] for nme in names], *[new_m[nme] for nme in names], *[new_v[nme] for nme in names])
```

```python
import functools

import jax
import jax.numpy as jnp
from jax import lax
from jax.experimental import pallas as pl
from jax.experimental.pallas import tpu as pltpu

F32 = jnp.float32
BF16 = jnp.bfloat16
HI = lax.Precision.HIGHEST
MESH = pl.DeviceIdType.MESH

EPS = 1e-6
CHUNK = 64
D_FF = 2816
GDN_HEADS = 4
HEAD = 128
MLA_HEADS = 4
MLA_ROPE = 64
MLA_Q_LORA = 384
MLA_KV_LORA = 256
QK_PAD = 256
ATT_SCALE = (HEAD + MLA_ROPE) ** -0.5

ADAM_LR, ADAM_B1, ADAM_B2, ADAM_EPS, ADAM_WD, ADAM_STEP = 0.001, 0.9, 0.999, 1e-08, 0.01, 10

LANES = 128
SUBLANES = 8
VMEM_LIMIT = 56 * 2 ** 20
ROW_TILE = 1024
ROW_TILE_WIDE = 512


def _params(sem=None):
    return pltpu.CompilerParams(dimension_semantics=sem, vmem_limit_bytes=VMEM_LIMIT)


def _pick(n, cap, align):
    best = None
    d = align
    while d <= min(n, cap):
        if n % d == 0:
            best = d
        d += align
    return best if best is not None else n


def _iota(shape, dim):
    return lax.broadcasted_iota(jnp.int32, shape, dim)


def _rowcall(fn, rows, params, out_rows, out_accs, *, tile, name):
    rows = [r if isinstance(r, tuple) else (r, r.shape[1], 0) for r in rows]
    s = rows[0][0].shape[-2]
    t = min(tile, s)
    n = s // t
    n_in = len(rows) + len(params)
    n_row_out = len(out_rows)

    in_specs = []
    for r in rows:
        if len(r) == 3:
            in_specs.append(pl.BlockSpec((t, r[1]), functools.partial(lambda i, b: (i, b), b=r[2])))
        else:
            in_specs.append(pl.BlockSpec((None, t, r[1]), functools.partial(lambda i, b, h: (h, i, b), b=r[2], h=r[3])))
    in_specs += [pl.BlockSpec(p.shape, lambda i: (0, 0)) for p in params]
    out_shape, out_specs = [], []
    for o in out_rows:
        if len(o) == 2:
            out_shape.append(jax.ShapeDtypeStruct((s, o[0]), o[1]))
            out_specs.append(pl.BlockSpec((t, o[0]), lambda i: (i, 0)))
        else:
            out_shape.append(jax.ShapeDtypeStruct((o[2], s, o[0]), o[1]))
            out_specs.append(pl.BlockSpec((o[2], t, o[0]), lambda i: (0, i, 0)))
    out_shape += [jax.ShapeDtypeStruct(shape, F32) for shape in out_accs]
    out_specs += [pl.BlockSpec(shape, lambda i: (0, 0)) for shape in out_accs]

    def body(*refs):
        ins = refs[:n_in]
        outs = refs[n_in:]
        i = pl.program_id(0)
        vals = [r[...] for r in ins]
        row_outs, acc_outs = fn(vals[:len(rows)], vals[len(rows):])
        for r, v in zip(outs[:n_row_out], row_outs):
            if isinstance(v, (list, tuple)):
                for hh, piece in enumerate(v):
                    r[hh] = piece.astype(r.dtype)
            else:
                r[...] = v.astype(r.dtype)
        if out_accs:
            @pl.when(i == 0)
            def _():
                for r in outs[n_row_out:]:
                    r[...] = jnp.zeros(r.shape, F32)
            for r, v in zip(outs[n_row_out:], acc_outs):
                r[...] += v

    res = pl.pallas_call(
        body, name=name, grid=(n,), in_specs=in_specs, out_specs=out_specs, out_shape=out_shape,
        compiler_params=_params(("arbitrary",) if out_accs else ("parallel",)),
    )(*[r[0] for r in rows], *params)
    return list(res)


MM_TILE_MN = 1536
MM_VMEM_BUDGET = 44 * 2 ** 20


class _Hosted:
    def __init__(self, gather=(), scatter=()):
        self.gather, self.scatter = list(gather), list(scatter)
        self.operands = self.gather + self.scatter
        self.n = len(self.operands)

    def specs(self):
        return [pl.BlockSpec(memory_space=pl.ANY)] * self.n

    def out_shapes(self):
        return ([jax.ShapeDtypeStruct((4,) + x.shape, x.dtype) for x in self.gather]
                + [jax.ShapeDtypeStruct((SCATTER_COPIES, g.shape[1] // 2, g.shape[2]), g.dtype) for g in self.scatter])

    def scratch(self):
        return ((_gather_sems(len(self.gather)) if self.gather else [])
                + (_scatter_sems(len(self.scatter)) if self.scatter else []))

    def _phase(self, ph, ins, outs, sems):
        ng = len(self.gather)
        g_sems, s_sems = (sems[:3], sems[3:]) if ng else ((), sems)
        for slot in range(ng):
            _gather_phase(ph, ins[slot], outs[slot], *g_sems, slot)
        if ph != 1:
            for slot in range(len(self.scatter)):
                _scatter_phase(0 if ph == 0 else 1, ins[ng + slot], outs[ng + slot], *s_sems, slot)

    def open(self, step, steps, ins, outs, sems):
        if self.n:
            pl.when(step == 0)(lambda: self._phase(0, ins, outs, sems))
            pl.when(step == steps // 2)(lambda: self._phase(1, ins, outs, sems))

    def close(self, step, steps, ins, outs, sems):
        if self.n:
            pl.when(step == steps - 1)(lambda: self._phase(2, ins, outs, sems))


def _mm(a, b, mode, out_dtype, name, hi=False, gather=(), chips=None, scatter=()):
    b_shape = b.shape
    if chips == "b":
        b_shape = (b.shape[1], 4 * b.shape[2])
    if mode == "nn":
        (m, k), (_, n) = a.shape, b_shape
        dims = (((1,), (0,)), ((), ()))
    elif mode == "nt":
        (m, k), (n, _) = a.shape, b_shape
        dims = (((1,), (1,)), ((), ()))
    else:
        (k, m), (_, n) = a.shape, b_shape
        dims = (((0,), (0,)), ((), ()))
    tn = _pick(n // 4 if chips and mode != "nt" else n, MM_TILE_MN, LANES)
    tk = _pick(k // 4 if chips and mode == "nt" else k, 1024 if mode == "tn" else MM_TILE_MN, LANES)
    tm = _pick(m, MM_TILE_MN if mode == "tn" else 1024, LANES if mode == "tn" else 16)
    if mode != "tn" and m % (2 * tm) == 0:
        blocks = 2 * tm * (2 * tn * (2 + jnp.dtype(out_dtype).itemsize) + 2 * tk * a.dtype.itemsize)
        if blocks + 2 * tk * tn * b.dtype.itemsize <= MM_VMEM_BUDGET:
            tm = 2 * tm
    nk = k // tk
    nb = (n // 4) // tn
    kb = (k // 4) // tk
    if mode == "nn":
        a_spec = pl.BlockSpec((tm, tk), lambda i, j, kk: (i, kk))
        b_spec = pl.BlockSpec((tk, tn), lambda i, j, kk: (kk, j))
        if chips == "b":
            b_spec = pl.BlockSpec((None, tk, tn), lambda i, j, kk: (j // nb, kk, j % nb))
    elif mode == "nt":
        a_spec = pl.BlockSpec((tm, tk), lambda i, j, kk: (i, kk))
        b_spec = pl.BlockSpec((tn, tk), lambda i, j, kk: (j, kk))
        if chips == "b":
            b_spec = pl.BlockSpec((None, tn, tk), lambda i, j, kk: (kk // kb, j, kk % kb))
    else:
        a_spec = pl.BlockSpec((tk, tm), lambda i, j, kk: (kk, i))
        b_spec = pl.BlockSpec((tk, tn), lambda i, j, kk: (kk, j))
    out_spec = pl.BlockSpec((tm, tn), lambda i, j, kk: (i, j))
    out_shape = jax.ShapeDtypeStruct((m, n), out_dtype)
    if chips == "out":
        out_spec = pl.BlockSpec((None, tm, tn), lambda i, j, kk: (j // nb, i, j % nb))
        out_shape = jax.ShapeDtypeStruct((4, m, n // 4), out_dtype)

    host = _Hosted(gather, scatter)
    ng = host.n
    grid = (m // tm, n // tn, nk)
    steps = grid[0] * grid[1] * grid[2]

    def body(*refs):
        a_ref, b_ref = refs[:2]
        x_refs = refs[2:2 + ng]
        o_ref = refs[2 + ng]
        got_refs = refs[3 + ng:3 + 2 * ng]
        acc_ref = refs[3 + 2 * ng]
        sems = refs[4 + 2 * ng:]
        kk = pl.program_id(2)
        step = (pl.program_id(0) * grid[1] + pl.program_id(1)) * nk + kk
        host.open(step, steps, x_refs, got_refs, sems)

        @pl.when(kk == 0)
        def _():
            acc_ref[...] = jnp.zeros(acc_ref.shape, F32)

        av, bv = a_ref[...], b_ref[...]
        if hi:
            acc_ref[...] += lax.dot_general(av, bv, dims, precision=HI, preferred_element_type=F32)
        else:
            acc_ref[...] += lax.dot_general(av.astype(BF16), bv.astype(BF16), dims,
                                            preferred_element_type=F32)

        @pl.when(kk == nk - 1)
        def _():
            o_ref[...] = acc_ref[...].astype(o_ref.dtype)

        host.close(step, steps, x_refs, got_refs, sems)

    res = pl.pallas_call(
        body, name=name, grid=grid,
        in_specs=[a_spec, b_spec] + host.specs(),
        out_specs=[out_spec] + host.specs(),
        out_shape=[out_shape] + host.out_shapes(),
        scratch_shapes=[pltpu.VMEM((tm, tn), F32)] + host.scratch(),
        compiler_params=_params(("arbitrary",) * 3 if ng else ("parallel", "parallel", "arbitrary")),
    )(a, b, *host.operands)
    return res if ng else res[0]


def _rms(x, w=None, n=None):
    n = x.shape[-1] if n is None else n
    y = x * lax.rsqrt(jnp.sum(x * x, axis=-1, keepdims=True) * (1.0 / n) + EPS)
    return y if w is None else y * w


def _silu(x):
    return x * jax.nn.sigmoid(x)


def _softplus(x):
    return jnp.maximum(x, 0.0) + jnp.log1p(jnp.exp(-jnp.abs(x)))


def _split(x, widths):
    out, o = [], 0
    for w in widths:
        out.append(x[:, o:o + w])
        o += w
    return out


def _modulate(x, s, sh):
    return _rms(x) * (1.0 + s) + sh


def _rope_rot(x):
    r, c = _iota((LANES, LANES), 0), _iota((LANES, LANES), 1)
    half = MLA_ROPE // 2
    perm = (((r < half) & (c == r + half)) | ((r >= half) & (r < MLA_ROPE) & (c == r - half))).astype(F32)
    return jnp.dot(x, perm, precision=HI, preferred_element_type=F32)


def _rope(x, cos2, sin2):
    return x * cos2 + _rope_rot(x) * sin2


def _gdn_prep_core(qkv_parts, gab, a_log, dt_bias):
    act = [_silu(p) for p in qkv_parts]
    qs = [p * lax.rsqrt(jnp.sum(p * p, -1, keepdims=True) + EPS) * (HEAD ** -0.5) for p in act[:4]]
    ks = [p * lax.rsqrt(jnp.sum(p * p, -1, keepdims=True) + EPS) for p in act[4:8]]
    lane = _iota(gab.shape, 1)
    g = -jnp.exp(a_log) * _softplus(gab + dt_bias)
    beta = jax.nn.sigmoid(gab)
    gb = jnp.where(lane < GDN_HEADS, g, jnp.where(lane < 2 * GDN_HEADS, beta, 0.0))
    return (jnp.concatenate(qs, 1), jnp.concatenate(ks, 1), jnp.concatenate(act[8:], 1), gb)


def _mla_prep_core(cq, ckv, kr, cos2, sin2, wq, wkv, wkr):
    cqn = _rms(cq, wq)
    ckvn = _rms(ckv, wkv)
    k_rope = _rope(_rms(kr, wkr, MLA_ROPE), cos2, sin2)
    return cqn, ckvn, k_rope


def _qk_prep_core(qn_parts, qr_parts, kn_parts, v_parts, k_rope, cos2, sin2, wqn, wqr, wkn):
    qs, ks = [], []
    for h in range(MLA_HEADS):
        qn = _rms(qn_parts[h], wqn) * ATT_SCALE
        qr = _rope(_rms(qr_parts[h], wqr, MLA_ROPE), cos2, sin2) * ATT_SCALE
        qs.append(jnp.concatenate([qn, qr], 1))
        ks.append(jnp.concatenate([_rms(kn_parts[h], wkn), k_rope], 1))
    return qs, ks, list(v_parts)


def _mix_post_core(o_parts, gz_parts, ob_parts, wn, won):
    oa = [_rms(o, wn) * _silu(z) for o, z in zip(o_parts, gz_parts)]
    ob = [_rms(o, won) for o in ob_parts]
    return jnp.concatenate(oa + ob, 1)


CONV_K = 4
HALO = SUBLANES


def _conv_fwd(proj, w8, name):
    s = proj.shape[0]
    c = w8.shape[1]
    t = min(ROW_TILE_WIDE, s)
    n = s // t
    hb = t // HALO

    def body(x_ref, prev_ref, w_ref, o_ref, buf):
        i = pl.program_id(0)
        buf[pl.ds(0, HALO), :] = jnp.where(i > 0, prev_ref[...], 0.0)
        buf[pl.ds(HALO, t), :] = x_ref[...]
        acc = jnp.zeros((t, c), F32)
        for k in range(CONV_K):
            acc = acc + w_ref[k:k + 1, :] * buf[pl.ds(HALO - (CONV_K - 1) + k, t), :]
        o_ref[...] = acc

    return pl.pallas_call(
        body, name=name, grid=(n,),
        in_specs=[pl.BlockSpec((t, c), lambda i: (i, 0)),
                  pl.BlockSpec((HALO, c), lambda i: (jnp.maximum(i * hb - 1, 0), 0)),
                  pl.BlockSpec(w8.shape, lambda i: (0, 0))],
        out_specs=pl.BlockSpec((t, c), lambda i: (i, 0)),
        out_shape=jax.ShapeDtypeStruct((s, c), F32),
        scratch_shapes=[pltpu.VMEM((t + HALO, c), F32)],
        compiler_params=_params(("parallel",)),
    )(proj, proj, w8)


def _conv_bwd(proj, dy, w8, name):
    s = proj.shape[0]
    c = w8.shape[1]
    t = min(ROW_TILE_WIDE, s)
    n = s // t
    hb = t // HALO

    def body(x_ref, prev_ref, dy_ref, next_ref, w_ref, dx_ref, dw_ref, bufx, bufd):
        i = pl.program_id(0)
        bufx[pl.ds(0, HALO), :] = jnp.where(i > 0, prev_ref[...], 0.0)
        bufx[pl.ds(HALO, t), :] = x_ref[...]
        bufd[pl.ds(0, t), :] = dy_ref[...]
        bufd[pl.ds(t, HALO), :] = jnp.where(i < n - 1, next_ref[...], 0.0)

        @pl.when(i == 0)
        def _():
            dw_ref[...] = jnp.zeros(dw_ref.shape, F32)

        dyv = dy_ref[...]
        acc = jnp.zeros((t, c), F32)
        for k in range(CONV_K):
            acc = acc + w_ref[k:k + 1, :] * bufd[pl.ds(CONV_K - 1 - k, t), :]
            dw_ref[k:k + 1, :] += jnp.sum(dyv * bufx[pl.ds(HALO - (CONV_K - 1) + k, t), :], axis=0, keepdims=True)
        dx_ref[...] = acc

    return pl.pallas_call(
        body, name=name, grid=(n,),
        in_specs=[pl.BlockSpec((t, c), lambda i: (i, 0)),
                  pl.BlockSpec((HALO, c), lambda i: (jnp.maximum(i * hb - 1, 0), 0)),
                  pl.BlockSpec((t, c), lambda i: (i, 0)),
                  pl.BlockSpec((HALO, c), lambda i: (jnp.minimum((i + 1) * hb, s // HALO - 1), 0)),
                  pl.BlockSpec(w8.shape, lambda i: (0, 0))],
        out_specs=[pl.BlockSpec((t, c), lambda i: (i, 0)), pl.BlockSpec(w8.shape, lambda i: (0, 0))],
        out_shape=[jax.ShapeDtypeStruct((s, c), F32), jax.ShapeDtypeStruct(w8.shape, F32)],
        scratch_shapes=[pltpu.VMEM((t + HALO, c), F32), pltpu.VMEM((t + HALO, c), F32)],
        compiler_params=_params(("arbitrary",)),
    )(proj, proj, dy, dy, w8)


_B_NN = (((2,), (1,)), ((0,), (0,)))
_B_NT = (((2,), (2,)), ((0,), (0,)))
_B_TN = (((1,), (1,)), ((0,), (0,)))


def _dot3(a, b, dims):
    return lax.dot_general(a, b, dims, precision=lax.Precision.HIGH, preferred_element_type=F32)


def _bdot_hi(a, b):
    return _dot3(a, b, _B_NN)


class _Dots:
    nn = staticmethod(lambda a, b: _dot3(a, b, _B_NN))
    nt = staticmethod(lambda a, b: _dot3(a, b, _B_NT))
    tn = staticmethod(lambda a, b: _dot3(a, b, _B_TN))


def _unit_lower_inverse(a, dots):
    c = a.shape[-1]
    ri, ci = _iota(a.shape, 1), _iota(a.shape, 2)
    inner = (ri // 2) == (ci // 2)
    t = (ri == ci).astype(F32) - jnp.where(inner, a, 0.0)
    blk = 4
    while blk <= c:
        outer = (ri // blk) == (ci // blk)
        low = jnp.where(outer & jnp.logical_not(inner), a, 0.0)
        t = t - dots.nn(dots.nn(t, low), t)
        inner = outer
        blk *= 2
    return t


def _stack(xs):
    return jnp.concatenate([x[None] for x in xs], axis=0)


def _gdn_local(dots, q, k, v, gbs):
    b, c, _ = q.shape
    gcols, bcols = [], []
    for gb in gbs:
        lane = _iota(gb.shape, 1)
        for h in range(GDN_HEADS):
            gcols.append(jnp.sum(jnp.where(lane == h, gb, 0.0), axis=1, keepdims=True))
            bcols.append(jnp.sum(jnp.where(lane == GDN_HEADS + h, gb, 0.0), axis=1, keepdims=True))
    gcol, bcol = _stack(gcols), _stack(bcols)
    ri, ci = _iota((b, c, c), 1), _iota((b, c, c), 2)
    incl = ri >= ci
    tril = incl.astype(F32)
    g_cc = _bdot_hi(tril, jnp.broadcast_to(gcol, (b, c, c)))
    g_row = _bdot_hi(jnp.ones((b, c, c), F32), jnp.where(ri == ci, g_cc, 0.0))
    g_cl = _bdot_hi(tril, jnp.broadcast_to(gcol, (b, c, HEAD)))
    g_last = jnp.sum(jnp.broadcast_to(gcol, (b, c, HEAD)), axis=1, keepdims=True)
    decay = jnp.where(incl, jnp.exp(jnp.where(incl, g_cc - g_row, 0.0)), 0.0)
    kk = dots.nt(k, k)
    minv = _unit_lower_inverse(jnp.where(ri > ci, bcol * kk * decay, 0.0), dots)
    e_g = jnp.exp(g_cl)
    u = dots.nn(minv, v * bcol)
    wk = dots.nn(minv, k * (bcol * e_g))
    qk = dots.nt(q, k) * decay
    return u, wk, q * e_g, k * jnp.exp(g_last - g_cl), qk, jnp.exp(g_last)


def _gdn_scan(dots, states, u, wk, qd, kd, qk, gl_tile):
    lane, row = _iota(gl_tile.shape, 1), _iota(gl_tile.shape, 0)
    gl = _stack([
        jnp.sum(jnp.sum(jnp.where((lane == h) & (row == 0), gl_tile, 0.0), axis=1, keepdims=True),
                axis=0, keepdims=True) for h in range(GDN_HEADS)])
    v_new = u - dots.nn(wk, states)
    o = dots.nn(qd, states) + dots.nn(qk, v_new)
    return states * gl + dots.tn(kd, v_new), o


def _heads(x):
    return jnp.stack(_split(x, HW4))


GDN_W = GDN_HEADS * HEAD
HW4 = [HEAD] * GDN_HEADS
LOCAL_CHUNKS = 4
_CHUNK_ROWS = [pl.ds(cc * CHUNK, CHUNK) for cc in range(LOCAL_CHUNKS)]


def _chunk_heads(ref):
    return jnp.concatenate([_heads(ref[rows, :]) for rows in _CHUNK_ROWS], 0)


def _gdn_local_fwd(q, k, v, gb, name):
    s = q.shape[0]
    t = LOCAL_CHUNKS * CHUNK

    def body(q_ref, k_ref, v_ref, gb_ref, u_ref, wk_ref, qd_ref, kd_ref, qk_ref, gl_ref):
        u, wk, qd, kd, qk, gl = _gdn_local(_Dots, _chunk_heads(q_ref), _chunk_heads(k_ref),
                                           _chunk_heads(v_ref), [gb_ref[rows, :] for rows in _CHUNK_ROWS])
        lane = _iota((CHUNK, LANES), 1)
        for cc, rows in enumerate(_CHUNK_ROWS):
            gl_tile = jnp.zeros((CHUNK, LANES), F32)
            for h in range(GDN_HEADS):
                b, cols = cc * GDN_HEADS + h, pl.ds(h * HEAD, HEAD)
                u_ref[rows, cols] = u[b]
                wk_ref[rows, cols] = wk[b]
                qd_ref[rows, cols] = qd[b]
                kd_ref[rows, cols] = kd[b]
                qk_ref[h, rows, :] = qk[b]
                gl_tile = gl_tile + jnp.where(lane == h, gl[b], 0.0)
            gl_ref[rows, :] = gl_tile

    row = pl.BlockSpec((t, GDN_W), lambda i: (i, 0))
    lane = pl.BlockSpec((t, LANES), lambda i: (i, 0))
    qks = pl.BlockSpec((GDN_HEADS, t, CHUNK), lambda i: (0, i, 0))
    return pl.pallas_call(
        body, name=name, grid=(s // t,),
        in_specs=[row, row, row, lane],
        out_specs=[row, row, row, row, qks, lane],
        out_shape=[jax.ShapeDtypeStruct((s, GDN_W), F32)] * 4
        + [jax.ShapeDtypeStruct((GDN_HEADS, s, CHUNK), F32), jax.ShapeDtypeStruct((s, LANES), F32)],
        compiler_params=_params(("parallel",)),
    )(q, k, v, gb)


def _gdn_local_bwd(q, k, v, gb, du, dwk, dqd, dkd, dqk, dgl, name):
    s = q.shape[0]
    t = LOCAL_CHUNKS * CHUNK

    def body(q_ref, k_ref, v_ref, gb_ref, du_ref, dwk_ref, dqd_ref, dkd_ref, dqk_ref, dgl_ref,
             dq_ref, dk_ref, dv_ref, dgb_ref):
        _, vjp = jax.vjp(functools.partial(_gdn_local, _Dots), _chunk_heads(q_ref), _chunk_heads(k_ref),
                         _chunk_heads(v_ref), [gb_ref[rows, :] for rows in _CHUNK_ROWS])
        lane = _iota((CHUNK, LANES), 1)
        dqk = jnp.stack([dqk_ref[h, rows, :] for rows in _CHUNK_ROWS for h in range(GDN_HEADS)])
        dgl = jnp.stack([jnp.sum(jnp.where(lane == h, dgl_ref[rows, :], 0.0), axis=0, keepdims=True)
                         for rows in _CHUNK_ROWS for h in range(GDN_HEADS)])
        d_q, d_k, d_v, d_gbs = vjp((_chunk_heads(du_ref), _chunk_heads(dwk_ref), _chunk_heads(dqd_ref),
                                    _chunk_heads(dkd_ref), dqk, dgl))
        for cc, rows in enumerate(_CHUNK_ROWS):
            for h in range(GDN_HEADS):
                b, cols = cc * GDN_HEADS + h, pl.ds(h * HEAD, HEAD)
                dq_ref[rows, cols] = d_q[b]
                dk_ref[rows, cols] = d_k[b]
                dv_ref[rows, cols] = d_v[b]
            dgb_ref[rows, :] = d_gbs[cc]

    row = pl.BlockSpec((t, GDN_W), lambda i: (i, 0))
    lane = pl.BlockSpec((t, LANES), lambda i: (i, 0))
    qks = pl.BlockSpec((GDN_HEADS, t, CHUNK), lambda i: (0, i, 0))
    return pl.pallas_call(
        body, name=name, grid=(s // t,),
        in_specs=[row, row, row, lane, row, row, row, row, qks, lane],
        out_specs=[row, row, row, lane],
        out_shape=[jax.ShapeDtypeStruct((s, GDN_W), F32)] * 3 + [jax.ShapeDtypeStruct((s, LANES), F32)],
        compiler_params=_params(("parallel",)),
    )(q, k, v, gb, du, dwk, dqd, dkd, dqk, dgl)


SCAN_CHUNKS = 4


def _scan_rows(s):
    k = min(SCAN_CHUNKS, s // CHUNK)
    return k * CHUNK, [pl.ds(cc * CHUNK, CHUNK) for cc in range(k)]


def _gdn_scan_fwd(u, wk, qd, kd, qk, gl, name):
    s = u.shape[0]
    nc = s // CHUNK
    t, chunk_rows = _scan_rows(s)

    def body(u_ref, wk_ref, qd_ref, kd_ref, qk_ref, gl_ref, o_ref, st_ref, state):
        i = pl.program_id(0)

        @pl.when(i == 0)
        def _():
            state[...] = jnp.zeros(state.shape, F32)

        st = state[...]
        for cc, rows in enumerate(chunk_rows):
            st_ref[cc] = st
            st, o = _gdn_scan(_Dots, st, _heads(u_ref[rows, :]), _heads(wk_ref[rows, :]), _heads(qd_ref[rows, :]),
                              _heads(kd_ref[rows, :]), qk_ref[:, rows, :], gl_ref[rows, :])
            o_ref[rows, :] = jnp.concatenate([o[h] for h in range(GDN_HEADS)], 1)
        state[...] = st

    row = pl.BlockSpec((t, GDN_W), lambda i: (i, 0))
    return pl.pallas_call(
        body, name=name, grid=(s // t,),
        in_specs=[row, row, row, row, pl.BlockSpec((GDN_HEADS, t, CHUNK), lambda i: (0, i, 0)),
                  pl.BlockSpec((t, LANES), lambda i: (i, 0))],
        out_specs=[row, pl.BlockSpec((len(chunk_rows), GDN_HEADS, HEAD, HEAD), lambda i: (i, 0, 0, 0))],
        out_shape=[jax.ShapeDtypeStruct((s, GDN_W), F32),
                   jax.ShapeDtypeStruct((nc, GDN_HEADS, HEAD, HEAD), F32)],
        scratch_shapes=[pltpu.VMEM((GDN_HEADS, HEAD, HEAD), F32)],
        compiler_params=_params(("arbitrary",)),
    )(u, wk, qd, kd, qk, gl)


def _gdn_scan_bwd(u, wk, qd, kd, qk, gl, st, do, name):
    s = u.shape[0]
    t, chunk_rows = _scan_rows(s)
    n = s // t

    def body(u_ref, wk_ref, qd_ref, kd_ref, qk_ref, gl_ref, st_ref, do_ref,
             du_ref, dwk_ref, dqd_ref, dkd_ref, dqk_ref, dgl_ref, dstate):
        i = pl.program_id(0)

        @pl.when(i == 0)
        def _():
            dstate[...] = jnp.zeros(dstate.shape, F32)

        unheads = lambda x: jnp.concatenate([x[h] for h in range(GDN_HEADS)], 1)
        ds = dstate[...]
        for cc in reversed(range(len(chunk_rows))):
            rows = chunk_rows[cc]
            _, vjp = jax.vjp(functools.partial(_gdn_scan, _Dots), st_ref[cc], _heads(u_ref[rows, :]),
                             _heads(wk_ref[rows, :]), _heads(qd_ref[rows, :]), _heads(kd_ref[rows, :]),
                             qk_ref[:, rows, :], gl_ref[rows, :])
            ds, d_u, d_wk, d_qd, d_kd, d_qk, d_gl = vjp((ds, _heads(do_ref[rows, :])))
            dqk_ref[:, rows, :] = d_qk
            du_ref[rows, :] = unheads(d_u)
            dwk_ref[rows, :] = unheads(d_wk)
            dqd_ref[rows, :] = unheads(d_qd)
            dkd_ref[rows, :] = unheads(d_kd)
            dgl_ref[rows, :] = d_gl
        dstate[...] = ds

    rev = lambda i: (n - 1 - i, 0)
    row = pl.BlockSpec((t, GDN_W), rev)
    lane = pl.BlockSpec((t, LANES), rev)
    qks = pl.BlockSpec((GDN_HEADS, t, CHUNK), lambda i: (0, n - 1 - i, 0))
    return pl.pallas_call(
        body, name=name, grid=(n,),
        in_specs=[row, row, row, row, qks, lane,
                  pl.BlockSpec((len(chunk_rows), GDN_HEADS, HEAD, HEAD), lambda i: (n - 1 - i, 0, 0, 0)), row],
        out_specs=[row, row, row, row, qks, lane],
        out_shape=[jax.ShapeDtypeStruct((s, GDN_W), F32)] * 4
        + [jax.ShapeDtypeStruct((GDN_HEADS, s, CHUNK), F32), jax.ShapeDtypeStruct((s, LANES), F32)],
        scratch_shapes=[pltpu.VMEM((GDN_HEADS, HEAD, HEAD), F32)],
        compiler_params=_params(("arbitrary",)),
    )(u, wk, qd, kd, qk, gl, st, do)


def _chunk_mask(i, j, t):
    r = i * t + _iota((t, t), 0)
    c = j * t + _iota((t, t), 1)
    return (r // CHUNK) >= (c // CHUNK)


ATT_TILE = 1024
ATT_Q_TILES = 1
ATT_BWD_TILE = 1024


def _attn_fwd(q, k, v, name, gather=()):
    nh, s = MLA_HEADS, q.shape[0]
    tk = min(ATT_TILE, s)
    tq = min(ATT_Q_TILES * tk, s)
    qk = tq // tk
    nq, n = s // tq, s // tk
    nt = (((1,), (1,)), ((), ()))
    host = _Hosted(gather)
    ng = host.n
    steps = nh * nq * n

    def body(*refs):
        q_ref, k_ref, v_ref = refs[:3]
        x_refs = refs[3:3 + ng]
        o_ref, lse_ref = refs[3 + ng:5 + ng]
        got_refs = refs[5 + ng:5 + 2 * ng]
        m_sc, l_sc, acc_sc = refs[5 + 2 * ng:8 + 2 * ng]
        sems = refs[8 + 2 * ng:]
        i, j = pl.program_id(1), pl.program_id(2)
        step_no = (pl.program_id(0) * nq + i) * n + j
        host.open(step_no, steps, x_refs, got_refs, sems)

        @pl.when(j == 0)
        def _():
            m_sc[...] = jnp.full(m_sc.shape, -jnp.inf, F32)
            l_sc[...] = jnp.zeros(l_sc.shape, F32)
            acc_sc[...] = jnp.zeros(acc_sc.shape, F32)

        def step(masked):
            sc = lax.dot_general(q_ref[...], k_ref[...], nt, preferred_element_type=F32)
            if masked:
                r = i * tq + _iota((tq, tk), 0)
                c = j * tk + _iota((tq, tk), 1)
                sc = jnp.where((r // CHUNK) >= (c // CHUNK), sc, -jnp.inf)
            m_prev = m_sc[:, :1]
            m_new = jnp.maximum(m_prev, jnp.max(sc, axis=1, keepdims=True))
            alpha = jnp.exp(m_prev - m_new)
            p = jnp.exp(sc - m_new)
            l_sc[...] = jnp.broadcast_to(alpha * l_sc[:, :1] + jnp.sum(p, axis=1, keepdims=True), l_sc.shape)
            acc_sc[...] = alpha * acc_sc[...] + jnp.dot(p.astype(BF16), v_ref[...], preferred_element_type=F32)
            m_sc[...] = jnp.broadcast_to(m_new, m_sc.shape)

        pl.when(j < i * qk)(lambda: step(False))
        pl.when(j // qk == i)(lambda: step(True))

        @pl.when(j == n - 1)
        def _():
            o_ref[...] = acc_sc[...] / l_sc[:, :1]
            lse_ref[...] = m_sc[...] + jnp.log(l_sc[...])

        host.close(step_no, steps, x_refs, got_refs, sems)

    qrow = lambda h, i, j: (i, h)
    krow = lambda h, i, j: (jnp.minimum(j, (i + 1) * qk - 1), h)
    res = pl.pallas_call(
        body, name=name, grid=(nh, nq, n),
        in_specs=[pl.BlockSpec((tq, QK_PAD), qrow), pl.BlockSpec((tk, QK_PAD), krow),
                  pl.BlockSpec((tk, HEAD), krow)] + host.specs(),
        out_specs=[pl.BlockSpec((tq, HEAD), qrow), pl.BlockSpec((None, tq, LANES), lambda h, i, j: (h, i, 0))]
        + host.specs(),
        out_shape=[jax.ShapeDtypeStruct((s, nh * HEAD), F32), jax.ShapeDtypeStruct((nh, s, LANES), F32)]
        + host.out_shapes(),
        scratch_shapes=[pltpu.VMEM((tq, LANES), F32), pltpu.VMEM((tq, LANES), F32), pltpu.VMEM((tq, HEAD), F32)]
        + host.scratch(),
        compiler_params=_params(("arbitrary",) * 3 if ng else ("parallel", "parallel", "arbitrary")),
    )(q, k, v, *host.operands)
    return res[0], res[1], list(res[2:])


def _attn_bwd(q, k, v, o, do, lse, name, scatter=()):
    nh, s = MLA_HEADS, q.shape[0]
    t = min(ATT_BWD_TILE, s)
    n = s // t
    tn = (((0,), (0,)), ((), ()))
    nt = (((1,), (1,)), ((), ()))
    host = _Hosted(scatter=scatter)
    nx = host.n
    steps = nh * n * n

    def body(*refs):
        q_ref, k_ref, v_ref, o_ref, do_ref, lse_ref = refs[:6]
        x_refs = refs[6:6 + nx]
        dq_ref, dk_ref, dv_ref = refs[6 + nx:9 + nx]
        got_refs = refs[9 + nx:9 + 2 * nx]
        dk_acc, dv_acc, dq_acc = refs[9 + 2 * nx:12 + 2 * nx]
        sems = refs[12 + 2 * nx:]
        j, i = pl.program_id(1), pl.program_id(2)
        step_no = (pl.program_id(0) * n + j) * n + i
        host.open(step_no, steps, x_refs, got_refs, sems)

        @pl.when(i + j == 0)
        def _():
            dq_acc[...] = jnp.zeros(dq_acc.shape, F32)

        @pl.when(i == 0)
        def _():
            dk_acc[...] = jnp.zeros(dk_acc.shape, F32)
            dv_acc[...] = jnp.zeros(dv_acc.shape, F32)

        def step(masked):
            qv, kv, do = q_ref[...], k_ref[...], do_ref[...]
            sc = lax.dot_general(qv, kv, nt, preferred_element_type=F32)
            p = jnp.exp(sc - lse_ref[:, :1])
            if masked:
                p = jnp.where(_chunk_mask(i, j, t), p, 0.0)
            dob = do.astype(BF16)
            dp = lax.dot_general(dob, v_ref[...], nt, preferred_element_type=F32)
            ds = (p * (dp - jnp.sum(do * o_ref[...], axis=1, keepdims=True))).astype(BF16)
            dv_acc[...] += lax.dot_general(p.astype(BF16), dob, tn, preferred_element_type=F32)
            dk_acc[...] += lax.dot_general(ds, qv, tn, preferred_element_type=F32)
            rows = pl.ds(pl.multiple_of(i * t, t), t)
            dq_acc[rows, :] += jnp.dot(ds, kv, preferred_element_type=F32)

        pl.when(i > j)(lambda: step(False))
        pl.when(i == j)(lambda: step(True))

        @pl.when(i == n - 1)
        def _():
            dk_ref[...] = dk_acc[...]
            dv_ref[...] = dv_acc[...]

        @pl.when(i + j == 2 * (n - 1))
        def _():
            dq_ref[...] = dq_acc[...]

        host.close(step_no, steps, x_refs, got_refs, sems)

    qrow = lambda h, j, i: (jnp.maximum(i, j), h)
    krow = lambda h, j, i: (j, h)
    res = pl.pallas_call(
        body, name=name, grid=(nh, n, n),
        in_specs=[pl.BlockSpec((t, QK_PAD), qrow), pl.BlockSpec((t, QK_PAD), krow), pl.BlockSpec((t, HEAD), krow),
                  pl.BlockSpec((t, HEAD), qrow), pl.BlockSpec((t, HEAD), qrow),
                  pl.BlockSpec((None, t, LANES), lambda h, j, i: (h, jnp.maximum(i, j), 0))] + host.specs(),
        out_specs=[pl.BlockSpec((s, QK_PAD), lambda h, j, i: (0, h)),
                   pl.BlockSpec((t, QK_PAD), krow), pl.BlockSpec((t, HEAD), krow)] + host.specs(),
        out_shape=[jax.ShapeDtypeStruct((s, nh * QK_PAD), F32), jax.ShapeDtypeStruct((s, nh * QK_PAD), F32),
                   jax.ShapeDtypeStruct((s, nh * HEAD), F32)] + host.out_shapes(),
        scratch_shapes=[pltpu.VMEM((t, QK_PAD), F32), pltpu.VMEM((t, HEAD), F32), pltpu.VMEM((s, QK_PAD), F32)]
        + host.scratch(),
        compiler_params=_params(("arbitrary", "arbitrary", "arbitrary")),
    )(q, k, v, o, do, lse, *host.operands)
    return res[0], res[1], res[2], list(res[3:])


def _place():
    return lax.axis_index("x"), lax.axis_index("y"), lax.axis_index("c")


def _allgather8(x, name):
    r, c = x.shape

    def body(x_ref, out_ref, send_sems, recv_sems, local_sem):
        mx, my, mc = _place()
        me = 4 * mx + 2 * my + mc
        mine = pltpu.make_async_copy(x_ref, out_ref.at[me], local_sem)
        mine.start()
        copies = []
        for d in range(1, 8):
            px = 1 - mx if d & 4 else mx
            py = 1 - my if d & 2 else my
            pc = 1 - mc if d & 1 else mc
            cp = pltpu.make_async_remote_copy(
                src_ref=x_ref, dst_ref=out_ref.at[me], send_sem=send_sems.at[d - 1], recv_sem=recv_sems.at[d - 1],
                device_id=(px, py, pc), device_id_type=MESH)
            cp.start()
            copies.append(cp)
        for cp in copies:
            cp.wait()
        mine.wait()

    return pl.pallas_call(
        body, name=name,
        out_shape=jax.ShapeDtypeStruct((8, r, c), x.dtype),
        in_specs=[pl.BlockSpec(memory_space=pltpu.VMEM)],
        out_specs=pl.BlockSpec(memory_space=pltpu.VMEM),
        scratch_shapes=[pltpu.SemaphoreType.DMA((7,)), pltpu.SemaphoreType.DMA((7,)), pltpu.SemaphoreType.DMA],
        compiler_params=pltpu.CompilerParams(vmem_limit_bytes=VMEM_LIMIT),
    )(x)


def _allgather_chips(x, name):
    r, c = x.shape

    def body(x_ref, out_ref, send_sems, recv_sems, local_sems):
        for phase in range(3):
            _gather_phase(phase, x_ref, out_ref, send_sems, recv_sems, local_sems, 0)

    return pl.pallas_call(
        body, name=name,
        out_shape=jax.ShapeDtypeStruct((4, r, c), x.dtype),
        in_specs=[pl.BlockSpec(memory_space=pltpu.VMEM)],
        out_specs=pl.BlockSpec(memory_space=pltpu.VMEM),
        scratch_shapes=_gather_sems(1),
        compiler_params=pltpu.CompilerParams(vmem_limit_bytes=VMEM_LIMIT),
    )(x)


GATHER_COPIES = 6


def _gather_sems(n):
    return [pltpu.SemaphoreType.DMA((GATHER_COPIES * n,)), pltpu.SemaphoreType.DMA((GATHER_COPIES * n,)),
            pltpu.SemaphoreType.DMA((n,))]


def _gather_phase(phase, x_ref, out_ref, send_sems, recv_sems, local_sems, slot):
    mx, my, mc = _place()
    j = 2 * mx + my
    rh = x_ref.shape[0] // 2
    base = GATHER_COPIES * slot
    chips = [(1 - mx, my), (mx, 1 - my), (1 - mx, 1 - my)]
    sibling = (mx, my, 1 - mc)

    def half(jj, hc):
        return out_ref.at[jj, pl.ds(hc * rh, rh), :]

    def over_ici(kk, block):
        px, py = chips[kk]
        return pltpu.make_async_remote_copy(
            src_ref=x_ref.at[pl.ds(mc * rh, rh), :], dst_ref=half(block, mc), send_sem=send_sems.at[base + kk],
            recv_sem=recv_sems.at[base + kk], device_id=(px, py, mc), device_id_type=MESH)

    def to_sibling(kk, hc):
        px, py = chips[kk]
        blk = half(2 * px + py, hc)
        return pltpu.make_async_remote_copy(
            src_ref=blk, dst_ref=blk, send_sem=send_sems.at[base + 3 + kk], recv_sem=recv_sems.at[base + 3 + kk],
            device_id=sibling, device_id_type=MESH)

    mine = pltpu.make_async_copy(x_ref, out_ref.at[j], local_sems.at[slot])
    if phase == 0:
        mine.start()
        for kk in range(3):
            over_ici(kk, j).start()
    elif phase == 1:
        for kk, (px, py) in enumerate(chips):
            over_ici(kk, 2 * px + py).wait_recv()
            to_sibling(kk, mc).start()
    else:
        for kk in range(3):
            to_sibling(kk, 1 - mc).wait_recv()
        for kk in range(3):
            over_ici(kk, j).wait_send()
            to_sibling(kk, mc).wait_send()
        mine.wait()


RS_ROWS = 32


def _reduce_scatter_chips(g, name):
    _, r, c = g.shape
    rh = r // 2
    steps = rh // RS_ROWS

    def body(g_ref, out_ref, sib_ref, part_ref, got_ref, send_sems, recv_sems):
        mx, my, mc = _place()
        j = 2 * mx + my
        sibling = (mx, my, 1 - mc)
        chips = [(1 - mx, my), (mx, 1 - my), (1 - mx, 1 - my)]

        to_sib = pltpu.make_async_remote_copy(
            src_ref=g_ref.at[:, pl.ds((1 - mc) * rh, rh), :], dst_ref=sib_ref,
            send_sem=send_sems.at[0], recv_sem=recv_sems.at[0], device_id=sibling, device_id_type=MESH)
        to_sib.start()
        to_sib.wait()

        def add_sibling(step, carry):
            rows = pl.ds(pl.multiple_of(step * RS_ROWS, RS_ROWS), RS_ROWS)
            mine = g_ref[:, pl.ds(pl.multiple_of(mc * rh + step * RS_ROWS, RS_ROWS), RS_ROWS), :]
            part_ref[:, rows, :] = mine.astype(F32) + sib_ref[:, rows, :].astype(F32)
            return carry

        lax.fori_loop(0, steps, add_sibling, 0)

        def to_bf16(step, carry):
            rows = pl.ds(pl.multiple_of(step * RS_ROWS, RS_ROWS), RS_ROWS)
            sib_ref[:, rows, :] = part_ref[:, rows, :].astype(BF16)
            return carry

        lax.fori_loop(0, steps, to_bf16, 0)

        sends = []
        for kk, (px, py) in enumerate(chips):
            cp = pltpu.make_async_remote_copy(
                src_ref=sib_ref.at[2 * px + py], dst_ref=got_ref.at[kk],
                send_sem=send_sems.at[1 + kk], recv_sem=recv_sems.at[1 + kk],
                device_id=(px, py, mc), device_id_type=MESH)
            cp.start()
            sends.append(cp)
        for cp in sends:
            cp.wait()

        def total(step, carry):
            rows = pl.ds(pl.multiple_of(step * RS_ROWS, RS_ROWS), RS_ROWS)
            acc = part_ref[j, rows, :]
            for kk in range(3):
                acc = acc + got_ref[kk, rows, :].astype(F32)
            out_ref[pl.ds(pl.multiple_of(mc * rh + step * RS_ROWS, RS_ROWS), RS_ROWS), :] = acc
            return carry

        lax.fori_loop(0, steps, total, 0)

        done = pltpu.make_async_remote_copy(
            src_ref=out_ref.at[pl.ds(mc * rh, rh), :], dst_ref=out_ref.at[pl.ds(mc * rh, rh), :],
            send_sem=send_sems.at[4], recv_sem=recv_sems.at[4], device_id=sibling, device_id_type=MESH)
        done.start()
        done.wait_send()
        pltpu.make_async_remote_copy(
            src_ref=out_ref.at[pl.ds((1 - mc) * rh, rh), :], dst_ref=out_ref.at[pl.ds((1 - mc) * rh, rh), :],
            send_sem=send_sems.at[4], recv_sem=recv_sems.at[4], device_id=sibling, device_id_type=MESH).wait_recv()

    return pl.pallas_call(
        body, name=name,
        out_shape=jax.ShapeDtypeStruct((r, c), F32),
        in_specs=[pl.BlockSpec(memory_space=pltpu.VMEM)],
        out_specs=pl.BlockSpec(memory_space=pltpu.VMEM),
        scratch_shapes=[pltpu.VMEM((4, rh, c), BF16), pltpu.VMEM((4, rh, c), F32), pltpu.VMEM((3, rh, c), BF16),
                        pltpu.SemaphoreType.DMA((5,)), pltpu.SemaphoreType.DMA((5,))],
        compiler_params=pltpu.CompilerParams(vmem_limit_bytes=VMEM_LIMIT),
    )(g)


def _sum8(x, name):
    _, r, c = x.shape

    def body(x_ref, o_ref):
        acc = x_ref[0]
        for d in range(1, 8):
            acc = acc + x_ref[d]
        o_ref[...] = acc

    return pl.pallas_call(
        body, name=name, out_shape=jax.ShapeDtypeStruct((r, c), F32),
        in_specs=[pl.BlockSpec(memory_space=pltpu.VMEM)], out_specs=pl.BlockSpec(memory_space=pltpu.VMEM),
    )(x)


SCATTER_COPIES = 7


def _scatter_sems(n):
    return [pltpu.SemaphoreType.DMA((SCATTER_COPIES * n,)), pltpu.SemaphoreType.DMA((SCATTER_COPIES * n,))]


def _scatter_phase(phase, g_ref, got_ref, send_sems, recv_sems, slot):
    mx, my, mc = _place()
    rh = g_ref.shape[1] // 2
    base = SCATTER_COPIES * slot
    for d in range(1, 8):
        px = 1 - mx if d & 4 else mx
        py = 1 - my if d & 2 else my
        pc = 1 - mc if d & 1 else mc
        cp = pltpu.make_async_remote_copy(
            src_ref=g_ref.at[2 * px + py, pl.ds(pc * rh, rh), :], dst_ref=got_ref.at[d - 1],
            send_sem=send_sems.at[base + d - 1], recv_sem=recv_sems.at[base + d - 1],
            device_id=(px, py, pc), device_id_type=MESH)
        if phase == 0:
            cp.start()
        else:
            cp.wait()


def _scatter_sum(g, got, name):
    mx, my, mc = _place()
    rh, c = got.shape[1], got.shape[2]
    mine = lax.dynamic_slice(g, (2 * mx + my, mc * rh, 0), (1, rh, c))[0]
    t = _pick(rh, 256, 16)

    def body(m_ref, got_ref, o_ref):
        acc = m_ref[...].astype(F32)
        for d in range(SCATTER_COPIES):
            acc = acc + got_ref[d].astype(F32)
        o_ref[...] = acc

    return pl.pallas_call(
        body, name=name, grid=(rh // t,),
        in_specs=[pl.BlockSpec((t, c), lambda i: (i, 0)), pl.BlockSpec((SCATTER_COPIES, t, c), lambda i: (0, i, 0))],
        out_specs=pl.BlockSpec((t, c), lambda i: (i, 0)),
        out_shape=jax.ShapeDtypeStruct((rh, c), F32), compiler_params=_params(("parallel",)),
    )(mine, got)


def _scatter_finish(halves, name):
    n = len(halves)

    def body(*refs):
        h_refs, o_refs = refs[:n], refs[n:2 * n]
        send_sems, recv_sems = refs[2 * n:]
        mx, my, mc = _place()
        copies = [pltpu.make_async_remote_copy(
            src_ref=h_refs[kk], dst_ref=o_refs[kk], send_sem=send_sems.at[kk], recv_sem=recv_sems.at[kk],
            device_id=(mx, my, 1 - mc), device_id_type=MESH) for kk in range(n)]
        for cp in copies:
            cp.start()
        for cp in copies:
            cp.wait()

    hbm = pl.BlockSpec(memory_space=pl.ANY)
    theirs = pl.pallas_call(
        body, name=name, in_specs=[hbm] * n, out_specs=[hbm] * n,
        out_shape=[jax.ShapeDtypeStruct(h.shape, F32) for h in halves],
        scratch_shapes=[pltpu.SemaphoreType.DMA((n,)), pltpu.SemaphoreType.DMA((n,))],
    )(*halves)
    south = lax.axis_index("c") == 0
    return [jnp.concatenate([jnp.where(south, m, t), jnp.where(south, t, m)], axis=0) for m, t in zip(halves, theirs)]


def _adamw(w, g, m, v, name):
    r, c = w.shape
    t = _pick(r, 256, SUBLANES)
    spec = pl.BlockSpec((t, c), lambda i: (i, 0))

    def body(w_ref, g_ref, m_ref, v_ref, d_ref, nm_ref, nv_ref):
        gv = g_ref[...]
        m_new = ADAM_B1 * m_ref[...] + (1.0 - ADAM_B1) * gv
        v_new = ADAM_B2 * v_ref[...] + (1.0 - ADAM_B2) * (gv * gv)
        m_hat = m_new / (1.0 - ADAM_B1 ** ADAM_STEP)
        v_hat = v_new / (1.0 - ADAM_B2 ** ADAM_STEP)
        d_ref[...] = -ADAM_LR * (m_hat / (jnp.sqrt(v_hat) + ADAM_EPS) + ADAM_WD * w_ref[...])
        nm_ref[...] = m_new
        nv_ref[...] = v_new

    return pl.pallas_call(
        body, name=name, grid=(r // t,), in_specs=[spec] * 4, out_specs=[spec] * 3,
        out_shape=[jax.ShapeDtypeStruct((r, c), F32)] * 3, compiler_params=_params(("parallel",)),
    )(w, g, m, v)


def _pack_rows(parts):
    rows, offs, o = [], [], 0
    for p in parts:
        f = p.reshape(-1)
        n = -(-f.shape[0] // (LANES * SUBLANES)) * SUBLANES
        rows.append(jnp.pad(f, (0, n * LANES - f.shape[0])).reshape(n, LANES))
        offs.append((o, n))
        o += n
    return jnp.concatenate(rows, 0), offs


def _unpack_rows(packed, offs, shapes):
    out = []
    for (o, n), shp in zip(offs, shapes):
        size = 1
        for d in shp:
            size *= d
        out.append(packed[o:o + n].reshape(-1)[:size].reshape(shp))
    return out


def _mm_hosting(a, b, mode, out_dtype, name, gather=(), chips=None, scatter=()):
    res = _mm(a, b, mode, out_dtype, name, gather=gather, chips=chips, scatter=scatter)
    return (res[0], list(res[1:])) if (gather or scatter) else (res, [])


def _ffn_fwd(x, s, sh, g, w_in, w_out, tag, gather_in=(), gather_out=()):
    (h,) = _rowcall(lambda r, p: ([_modulate(r[0], p[0], p[1])], []), [x], [s, sh], [(x.shape[1], BF16)], [],
                    tile=ROW_TILE, name=tag + "_mod")
    gu, got = _mm_hosting(h, w_in, "nn", BF16, tag + "_in", gather_in, chips="b")
    if w_out is None:
        first = got.pop(0)
        w_out = first.reshape(4 * first.shape[1], first.shape[2])
    (act,) = _rowcall(lambda r, p: ([_silu(r[0].astype(F32)) * r[1].astype(F32)], []),
                      [(gu, D_FF, 0), (gu, D_FF, 1)], [], [(D_FF, BF16)], [], tile=ROW_TILE_WIDE, name=tag + "_act")
    f, got_out = _mm_hosting(act, w_out, "nn", F32, tag + "_out", gather_out)
    got = got + got_out
    (y,) = _rowcall(lambda r, p: ([r[0] + 0.5 * p[0] * r[1]], []), [x, f], [g], [(x.shape[1], F32)], [],
                    tile=ROW_TILE, name=tag + "_res")
    return y, (x, h, gu, act, f), got, w_out


def _ffn_bwd(dy, saved, s, sh, g, w_in, w_out, tag, scatter_bin=(), scatter_bwin=()):
    x, h, gu, act, f = saved
    d = x.shape[1]
    df, dg = _rowcall(lambda r, p: ([0.5 * p[0] * r[0]], [0.5 * jnp.sum(r[0] * r[1], 0, keepdims=True)]),
                      [dy, f], [g], [(d, BF16)], [(1, d)], tile=ROW_TILE, name=tag + "_bres")
    da = _mm(df, w_out, "nt", BF16, tag + "_bout")
    dw_out = _mm(act, df, "tn", BF16, tag + "_bwout")

    def act_bwd(r, p):
        gate, up, dav = r[0].astype(F32), r[1].astype(F32), r[2].astype(F32)
        _, vjp = jax.vjp(lambda a, b: _silu(a) * b, gate, up)
        dgate, dup = vjp(dav)
        return [jnp.concatenate([dgate, dup], 1)], []

    (dgu,) = _rowcall(act_bwd, [(gu, D_FF, 0), (gu, D_FF, 1), da], [], [(2 * D_FF, BF16)], [], tile=ROW_TILE_WIDE,
                      name=tag + "_bact")
    dh, got_a = _mm_hosting(dgu, w_in, "nt", F32, tag + "_bin", chips="b", scatter=scatter_bin)
    dw_in, got_b = _mm_hosting(h, dgu, "tn", BF16, tag + "_bwin", chips="out", scatter=scatter_bwin)

    def mod_bwd(r, p):
        _, vjp = jax.vjp(_modulate, r[0], p[0], p[1])
        dx, ds, dsh = vjp(r[1])
        return [r[2] + dx], [ds, dsh]

    dx, ds, dsh = _rowcall(mod_bwd, [x, dh, dy], [s, sh], [(d, F32)], [(1, d), (1, d)], tile=ROW_TILE, name=tag + "_bmod")
    return dx, (dsh, ds, dg), dw_in, dw_out, list(got_a) + list(got_b)


def _mixer_fwd(x, s, sh, g, wts, rope, gather=()):
    w_in_p, conv8, a_log, dt_bias, wn, wq, w_uq_p, wkv, w_ukv, wqn, wqr, wkn, wkr, won, w_out = wts
    cos2, sin2 = rope
    d = x.shape[1]
    (h,) = _rowcall(lambda r, p: ([_modulate(r[0], p[0], p[1])], []), [x], [s, sh], [(d, BF16)], [],
                    tile=ROW_TILE, name="mix_mod")
    proj = _mm(h, w_in_p, "nn", F32, "mix_in")
    qkv_c = _conv_fwd(proj, conv8, "mix_conv")
    gab = (proj, LANES, 23)

    q, k, v, gb = _rowcall(
        lambda r, p: (list(_gdn_prep_core(_split(r[0], [HEAD] * 12), r[1], p[0], p[1])), []),
        [qkv_c, gab], [a_log, dt_bias], [(512, F32)] * 3 + [(LANES, F32)], [], tile=ROW_TILE_WIDE, name="mix_gdn_prep")
    gdn_local = _gdn_local_fwd(q, k, v, gb, "mix_gdn_local")
    o_gdn, gdn_states = _gdn_scan_fwd(*gdn_local, "mix_gdn_scan")
    states = (gdn_local, gdn_states)

    cq, ckv, kr = (proj, 512, 4), (proj, 256, 10), (proj, LANES, 22)
    cqn, ckvn, k_rope = _rowcall(
        lambda r, p: (list(_mla_prep_core(r[0][:, :MLA_Q_LORA], r[1], r[2], r[3], r[4], p[0], p[1], p[2])), []),
        [cq, ckv, kr, cos2, sin2], [wq, wkv, wkr], [(MLA_Q_LORA, BF16), (MLA_KV_LORA, BF16), (LANES, F32)], [],
        tile=ROW_TILE, name="mix_mla_prep")
    qf = _mm(cqn, w_uq_p, "nn", F32, "mix_uq")
    kvf = _mm(ckvn, w_ukv, "nn", F32, "mix_ukv")

    def qk_prep(r, p):
        qparts = _split(r[0], [HEAD] * 8)
        kvparts = _split(r[1], [HEAD] * 8)
        qs, ks, vs = _qk_prep_core(qparts[:4], qparts[4:], kvparts[0::2], kvparts[1::2], r[2], r[3], r[4],
                                   p[0], p[1], p[2])
        return [jnp.concatenate(qs, 1), jnp.concatenate(ks, 1), jnp.concatenate(vs, 1)], []

    qa, ka, va = _rowcall(qk_prep, [qf, kvf, k_rope, cos2, sin2], [wqn, wqr, wkn],
                          [(4 * QK_PAD, BF16), (4 * QK_PAD, BF16), (4 * HEAD, BF16)], [], tile=ROW_TILE_WIDE,
                          name="mix_qk_prep")
    o_b, lse, got = _attn_fwd(qa, ka, va, "mix_attn", gather=gather)
    if w_out is None:
        first = got.pop(0)
        w_out = first.reshape(4 * first.shape[1], first.shape[2])

    gz = (proj, 512, 3)
    (mixed,) = _rowcall(
        lambda r, p: ([_mix_post_core(_split(r[0], HW4), _split(r[1], HW4), _split(r[2], HW4), p[0], p[1])], []),
        [o_gdn, gz, o_b], [wn, won], [(2 * 512, BF16)], [], tile=ROW_TILE, name="mix_post")
    y = _mm(mixed, w_out, "nn", F32, "mix_out")
    (x_out,) = _rowcall(lambda r, p: ([r[0] + p[0] * r[1]], []), [x, y], [g], [(d, F32)], [], tile=ROW_TILE,
                        name="mix_res")
    saved = (x, h, proj, qkv_c, q, k, v, gb, states, o_gdn, cqn, ckvn, k_rope, qf, kvf, qa, ka, va, o_b, lse,
             mixed, y)
    return x_out, saved, got, w_out


def _mixer_bwd(dy, saved, s, sh, g, wts, rope, scatter=()):
    w_in_p, conv8, a_log, dt_bias, wn, wq, w_uq_p, wkv, w_ukv, wqn, wqr, wkn, wkr, won, w_out = wts
    cos2, sin2 = rope
    (x, h, proj, qkv_c, q, k, v, gb, states, o_gdn, cqn, ckvn, k_rope, qf, kvf, qa, ka, va, o_b, lse,
     mixed, y) = saved
    d = x.shape[1]
    dyb, dg = _rowcall(lambda r, p: ([p[0] * r[0]], [jnp.sum(r[0] * r[1], 0, keepdims=True)]),
                       [dy, y], [g], [(d, BF16)], [(1, d)], tile=ROW_TILE, name="mix_bres")
    dmixed = _mm(dyb, w_out, "nt", F32, "mix_bout")
    dw_out = _mm(mixed, dyb, "tn", BF16, "mix_bwout")

    gz = (proj, 512, 3)

    def post_bwd(r, p):
        _, vjp = jax.vjp(_mix_post_core, _split(r[0], HW4), _split(r[1], HW4), _split(r[2], HW4), p[0], p[1])
        do, dz, dob, dwn, dwon = vjp(r[3])
        return [jnp.concatenate(do, 1), jnp.concatenate(dz, 1), jnp.concatenate(dob, 1)], [dwn, dwon]

    do_gdn, dgz, do_b, dwn, dwon = _rowcall(post_bwd, [o_gdn, gz, o_b, dmixed], [wn, won], [(512, F32)] * 3,
                                            [(1, HEAD), (1, HEAD)], tile=ROW_TILE_WIDE, name="mix_bpost")

    dqa, dka, dva, got = _attn_bwd(qa, ka, va, o_b, do_b, lse, "mix_battn", scatter=scatter)

    def qk_bwd(r, p):
        qparts = _split(r[0], [HEAD] * 8)
        kvparts = _split(r[1], [HEAD] * 8)
        _, vjp = jax.vjp(_qk_prep_core, qparts[:4], qparts[4:], kvparts[0::2], kvparts[1::2], r[2], r[3], r[4],
                         p[0], p[1], p[2])
        cot = (_split(r[5], [QK_PAD] * 4), _split(r[6], [QK_PAD] * 4), _split(r[7], HW4))
        dqn, dqr, dkn, dvp, dkrope, _, _, dwqn, dwqr, dwkn = vjp(cot)
        dkv = []
        for a, b in zip(dkn, dvp):
            dkv += [a, b]
        return [jnp.concatenate(list(dqn) + list(dqr), 1), jnp.concatenate(dkv, 1), dkrope], [dwqn, dwqr, dwkn]

    dqf, dkvf, dk_rope, dwqn, dwqr, dwkn = _rowcall(
        qk_bwd, [qf, kvf, k_rope, cos2, sin2, dqa, dka, dva], [wqn, wqr, wkn],
        [(8 * HEAD, BF16), (8 * HEAD, BF16), (LANES, F32)], [(1, HEAD)] * 3, tile=ROW_TILE_WIDE, name="mix_bqk_prep")
    dcqn = _mm(dqf, w_uq_p, "nt", F32, "mix_buq")
    dw_uq_p = _mm(cqn, dqf, "tn", F32, "mix_bwuq")
    dckvn = _mm(dkvf, w_ukv, "nt", F32, "mix_bukv")
    dw_ukv = _mm(ckvn, dkvf, "tn", F32, "mix_bwukv")

    cq, ckv, kr = (proj, 512, 4), (proj, 256, 10), (proj, LANES, 22)

    def mla_bwd(r, p):
        _, vjp = jax.vjp(_mla_prep_core, r[0][:, :MLA_Q_LORA], r[1], r[2], r[3], r[4], p[0], p[1], p[2])
        dcq, dckv, dkr, _, _, dwq, dwkv, dwkr = vjp((r[5], r[6], r[7]))
        pad = jnp.zeros((dcq.shape[0], 512 - MLA_Q_LORA), F32)
        return [jnp.concatenate([dcq, pad], 1), dckv, dkr], [dwq, dwkv, dwkr]

    dcq, dckv, dkr, dwq, dwkv, dwkr = _rowcall(
        mla_bwd, [cq, ckv, kr, cos2, sin2, dcqn, dckvn, dk_rope], [wq, wkv, wkr],
        [(512, F32), (MLA_KV_LORA, F32), (LANES, F32)], [(1, MLA_Q_LORA), (1, MLA_KV_LORA), (1, LANES)],
        tile=ROW_TILE, name="mix_bmla_prep")

    gdn_local, gdn_states = states
    d_local = _gdn_scan_bwd(*gdn_local, gdn_states, do_gdn, "mix_bgdn_scan")
    dq, dk, dv, dgb = _gdn_local_bwd(q, k, v, gb, *d_local, "mix_bgdn_local")
    gab = (proj, LANES, 23)

    def gdn_prep_bwd(r, p):
        _, vjp = jax.vjp(_gdn_prep_core, _split(r[0], [HEAD] * 12), r[1], p[0], p[1])
        dparts, dgab, da_log, ddt = vjp((r[2], r[3], r[4], r[5]))
        return [jnp.concatenate(dparts, 1), dgab], [da_log, ddt]

    dqkv_c, dgab, da_log, ddt = _rowcall(gdn_prep_bwd, [qkv_c, gab, dq, dk, dv, dgb], [a_log, dt_bias],
                                         [(1536, F32), (LANES, F32)], [(1, LANES), (1, LANES)], tile=ROW_TILE_WIDE,
                                         name="mix_bgdn_prep")
    dqkv_pre, dconv8 = _conv_bwd(proj, dqkv_c, conv8, "mix_bconv")

    dproj = jnp.concatenate([dqkv_pre.astype(BF16), dgz.astype(BF16), dcq.astype(BF16), dckv.astype(BF16),
                             dkr.astype(BF16), dgab.astype(BF16)], axis=1)
    dh = _mm(dproj, w_in_p, "nt", F32, "mix_bin")
    dw_in_p = _mm(h, dproj, "tn", F32, "mix_bwin")

    def mod_bwd(r, p):
        _, vjp = jax.vjp(_modulate, r[0], p[0], p[1])
        dx, ds, dsh = vjp(r[1])
        return [r[2] + dx], [ds, dsh]

    dx, ds, dsh = _rowcall(mod_bwd, [x, dh, dy], [s, sh], [(d, F32)], [(1, d), (1, d)], tile=ROW_TILE, name="mix_bmod")
    small = dict(conv=dconv8, a_log=da_log, dt=ddt, wn=dwn, wq=dwq, wkv=dwkv, wqn=dwqn, wqr=dwqr, wkn=dwkn,
                 wkr=dwkr, won=dwon)
    return dx, (dsh, ds, dg), dw_in_p, dw_uq_p, dw_ukv, dw_out, small, got


def _pad_cols(a, n):
    return jnp.pad(a, ((0, 0),) * (a.ndim - 1) + ((0, n - a.shape[-1]),))


def _pack_w_in(w):
    z = lambda n: jnp.zeros((w.shape[0], n), w.dtype)
    return jnp.concatenate([w[:, 0:2048], w[:, 2056:2440], z(128), w[:, 2440:2696], w[:, 2696:2760], z(64),
                            w[:, 2048:2056], z(120)], axis=1)


def _unpack_w_in(wp):
    return jnp.concatenate([wp[:, 0:2048], wp[:, 2944:2952], wp[:, 2048:2432], wp[:, 2560:2816], wp[:, 2816:2880]],
                           axis=1)


def _pack_w_uq(w):
    z = jnp.zeros((w.shape[0], LANES - MLA_ROPE), w.dtype)
    nope = [w[:, h * 192:h * 192 + HEAD] for h in range(MLA_HEADS)]
    rope = []
    for h in range(MLA_HEADS):
        rope += [w[:, h * 192 + HEAD:(h + 1) * 192], z]
    return jnp.concatenate(nope + rope, axis=1)


def _unpack_w_uq(wp):
    cols = []
    for h in range(MLA_HEADS):
        cols += [wp[:, h * HEAD:(h + 1) * HEAD], wp[:, 512 + h * LANES:512 + h * LANES + MLA_ROPE]]
    return jnp.concatenate(cols, axis=1)


def _cols_to_chips(a):
    r, c = a.shape
    return a.reshape(r, 4, c // 4).transpose(1, 0, 2)


def _chips_to_cols(a):
    _, r, n = a.shape
    return a.transpose(1, 0, 2).reshape(r, 4 * n)


def _pad128(v, n=LANES):
    return _pad_cols(v.reshape(1, -1), n)


def kernel(x, c, positions, w_ada, b_ada, ffn1_w_in, ffn1_w_out, w_in, gdn_conv_w, gdn_a_log, gdn_dt_bias, gdn_norm_w, mla_q_norm_w, mla_w_uq, mla_kv_norm_w, mla_w_ukv, qkn_q_nope, qkn_q_rope, qkn_k_nope, qkn_k_rope, mla_out_norm_w, w_out, ffn2_w_in, ffn2_w_out, loss_target, m_w_ada, m_b_ada, m_ffn1_w_in, m_ffn1_w_out, m_w_in, m_gdn_conv_w, m_gdn_a_log, m_gdn_dt_bias, m_gdn_norm_w, m_mla_q_norm_w, m_mla_w_uq, m_mla_kv_norm_w, m_mla_w_ukv, m_qkn_q_nope, m_qkn_q_rope, m_qkn_k_nope, m_qkn_k_rope, m_mla_out_norm_w, m_w_out, m_ffn2_w_in, m_ffn2_w_out, v_w_ada, v_b_ada, v_ffn1_w_in, v_ffn1_w_out, v_w_in, v_gdn_conv_w, v_gdn_a_log, v_gdn_dt_bias, v_gdn_norm_w, v_mla_q_norm_w, v_mla_w_uq, v_mla_kv_norm_w, v_mla_w_ukv, v_qkn_q_nope, v_qkn_q_rope, v_qkn_k_nope, v_qkn_k_rope, v_mla_out_norm_w, v_w_out, v_ffn2_w_in, v_ffn2_w_out):
    weights = dict(w_ada=w_ada, b_ada=b_ada, ffn1_w_in=ffn1_w_in, ffn1_w_out=ffn1_w_out, w_in=w_in,
                   gdn_conv_w=gdn_conv_w, gdn_a_log=gdn_a_log, gdn_dt_bias=gdn_dt_bias, gdn_norm_w=gdn_norm_w,
                   mla_q_norm_w=mla_q_norm_w, mla_w_uq=mla_w_uq, mla_kv_norm_w=mla_kv_norm_w, mla_w_ukv=mla_w_ukv,
                   qkn_q_nope=qkn_q_nope, qkn_q_rope=qkn_q_rope, qkn_k_nope=qkn_k_nope, qkn_k_rope=qkn_k_rope,
                   mla_out_norm_w=mla_out_norm_w, w_out=w_out, ffn2_w_in=ffn2_w_in, ffn2_w_out=ffn2_w_out)
    moms_m = dict(w_ada=m_w_ada, b_ada=m_b_ada, ffn1_w_in=m_ffn1_w_in, ffn1_w_out=m_ffn1_w_out, w_in=m_w_in,
                  gdn_conv_w=m_gdn_conv_w, gdn_a_log=m_gdn_a_log, gdn_dt_bias=m_gdn_dt_bias,
                  gdn_norm_w=m_gdn_norm_w, mla_q_norm_w=m_mla_q_norm_w, mla_w_uq=m_mla_w_uq,
                  mla_kv_norm_w=m_mla_kv_norm_w, mla_w_ukv=m_mla_w_ukv, qkn_q_nope=m_qkn_q_nope,
                  qkn_q_rope=m_qkn_q_rope, qkn_k_nope=m_qkn_k_nope, qkn_k_rope=m_qkn_k_rope,
                  mla_out_norm_w=m_mla_out_norm_w, w_out=m_w_out, ffn2_w_in=m_ffn2_w_in, ffn2_w_out=m_ffn2_w_out)
    moms_v = dict(w_ada=v_w_ada, b_ada=v_b_ada, ffn1_w_in=v_ffn1_w_in, ffn1_w_out=v_ffn1_w_out, w_in=v_w_in,
                  gdn_conv_w=v_gdn_conv_w, gdn_a_log=v_gdn_a_log, gdn_dt_bias=v_gdn_dt_bias,
                  gdn_norm_w=v_gdn_norm_w, mla_q_norm_w=v_mla_q_norm_w, mla_w_uq=v_mla_w_uq,
                  mla_kv_norm_w=v_mla_kv_norm_w, mla_w_ukv=v_mla_w_ukv, qkn_q_nope=v_qkn_q_nope,
                  qkn_q_rope=v_qkn_q_rope, qkn_k_nope=v_qkn_k_nope, qkn_k_rope=v_qkn_k_rope,
                  mla_out_norm_w=v_mla_out_norm_w, w_out=v_w_out, ffn2_w_in=v_ffn2_w_in, ffn2_w_out=v_ffn2_w_out)
    names = list(weights)

    seq, d = x.shape[1], x.shape[2]
    x2d = x.reshape(seq, d)
    tgt = loss_target.reshape(seq, d)
    mx, my, mc = _place()
    chip = 2 * mx + my
    me = 2 * chip + mc
    n_mod = b_ada.shape[1] // d
    shard = w_ada.shape[2]

    half = MLA_ROPE // 2
    inv_freq = 10000.0 ** (-jnp.arange(half, dtype=F32) / half)
    ang = positions.astype(F32).reshape(seq, 1) * inv_freq
    cosv, sinv = jnp.cos(ang), jnp.sin(ang)
    cos2 = _pad_cols(jnp.concatenate([cosv, cosv], 1), LANES)
    sin2 = _pad_cols(jnp.concatenate([-sinv, sinv], 1), LANES)
    rope = (cos2, sin2)

    c_all = _allgather8(jnp.pad(c, ((0, SUBLANES - 1), (0, 0))), "gather_c")[:, 0, :]
    (sc_all,) = _rowcall(lambda r, p: ([_silu(r[0])], []), [c_all], [], [(d, F32)], [], tile=8, name="ada_silu")
    mod_part = _mm(sc_all, w_ada[0], "nn", F32, "ada_mm", hi=True)
    mod_all = _allgather8(mod_part, "gather_mod")
    mod_rows = lax.dynamic_index_in_dim(mod_all, me, axis=1, keepdims=False)
    mod_raw = jnp.concatenate([mod_rows[2 * jj] for jj in range(4)], axis=0).reshape(1, 4 * shard)
    (mod,) = _rowcall(lambda r, p: ([r[0] + r[1]], []),
                      [jnp.pad(mod_raw, ((0, 7), (0, 0))), jnp.pad(b_ada, ((0, 7), (0, 0)))], [],
                      [(4 * shard, F32)], [], tile=8, name="ada_bias")
    mods = [mod[0:1, i * d:(i + 1) * d] for i in range(n_mod)]
    sh1, s1, g1, sh2, s2, g2, sh3, s3, g3 = mods

    def shard_bf16(w, pad_to=None):
        w2 = w[0].astype(BF16)
        return _pad_cols(w2, pad_to) if pad_to else w2

    def cols_of(got, w):
        return _chips_to_cols(got[:, :, :w.shape[2]])

    def rows_of(got):
        return got.reshape(4 * got.shape[1], got.shape[2])

    f1_in = _allgather_chips(shard_bf16(ffn1_w_in), "gather_f1_in")
    conv_all = _allgather8(jnp.pad(gdn_conv_w[0], ((0, SUBLANES - CONV_K), (0, 0))), "gather_conv")
    conv8 = jnp.concatenate([conv_all[2 * jj] for jj in range(4)], axis=1)

    x1, sv1, got, f1_out = _ffn_fwd(
        x2d, s1, sh1, g1, f1_in, None, "ffn1",
        gather_in=[shard_bf16(ffn1_w_out), shard_bf16(w_in, 768), shard_bf16(mla_w_uq, 256), shard_bf16(mla_w_ukv)])
    w_in_full, w_uq_full, w_ukv_full = cols_of(got[0], w_in), cols_of(got[1], mla_w_uq), cols_of(got[2], mla_w_ukv)
    wts = (_pack_w_in(w_in_full), conv8, _pad128(gdn_a_log), _pad128(gdn_dt_bias), gdn_norm_w,
           mla_q_norm_w, _pack_w_uq(w_uq_full), mla_kv_norm_w, w_ukv_full, qkn_q_nope, _pad128(qkn_q_rope),
           qkn_k_nope, _pad128(qkn_k_rope), mla_out_norm_w, None)
    xm, svm, got, w_out_full = _mixer_fwd(
        x1, s2, sh2, g2, wts, rope, gather=[shard_bf16(w_out), shard_bf16(ffn2_w_in), shard_bf16(ffn2_w_out)])
    wts = wts[:-1] + (w_out_full,)
    f2_in, f2_out = got[0], rows_of(got[1])
    x3, sv3, _, _ = _ffn_fwd(xm, s3, sh3, g3, f2_in, f2_out, "ffn2")

    def loss_fn(r, p):
        err = r[0] - r[1]
        part = 0.5 * jnp.sum(jnp.sum(err * err, axis=1, keepdims=True) * (1.0 / d), axis=0, keepdims=True)
        return [err * (1.0 / d)], [jnp.broadcast_to(part, (1, LANES))]

    dy, loss_part = _rowcall(loss_fn, [x3, tgt], [], [(d, F32)], [(1, LANES)], tile=ROW_TILE, name="loss")
    loss = lax.psum(loss_part[0, 0], ("x", "y", "c"))

    def chip_cols(dw, pad_to=None):
        g4 = _cols_to_chips(dw).astype(BF16)
        return _pad_cols(g4, pad_to) if pad_to else g4

    def chip_rows(dw):
        return dw.astype(BF16).reshape(4, dw.shape[0] // 4, dw.shape[1])

    dxm, dmod3, dw_f2_in, dw_f2_out, _ = _ffn_bwd(dy, sv3, s3, sh3, g3, f2_in, f2_out, "ffn2")
    parts2 = [dw_f2_in, chip_rows(dw_f2_out)]
    dx1, dmod2, dw_in_p, dw_uq_p, dw_ukv, dw_out_m, small, got2 = _mixer_bwd(dxm, svm, s2, sh2, g2, wts, rope,
                                                                             scatter=parts2)
    parts_m = [chip_cols(_unpack_w_in(dw_in_p), 768), chip_cols(_unpack_w_uq(dw_uq_p), 256), chip_cols(dw_ukv),
               chip_rows(dw_out_m)]
    dx0, dmod1, dw_f1_in, dw_f1_out, got_m = _ffn_bwd(dx1, sv1, s1, sh1, g1, f1_in, f1_out, "ffn1",
                                                      scatter_bin=parts_m[:1], scatter_bwin=parts_m[1:])
    grad_x = dx0.reshape(x.shape)

    dmod = jnp.concatenate(list(dmod1) + list(dmod2) + list(dmod3), axis=1)
    small_parts = [dmod, small["conv"][:CONV_K], small["a_log"], small["dt"], small["wn"], small["wq"],
                   small["wkv"], small["wqn"], small["wqr"], small["wkn"], small["wkr"], small["won"]]
    packed, offs = _pack_rows(small_parts)
    gathered = _allgather8(packed, "gather_small")
    total = _sum8(gathered, "sum_small")
    (g_b_ada, g_conv, g_a_log, g_dt, g_wn, g_wq, g_wkv, g_wqn, g_wqr, g_wkn, g_wkr, g_won) = _unpack_rows(
        total, offs, [p.shape for p in small_parts])
    dmod_all = _unpack_rows(gathered.reshape(-1, LANES),
                            [(dd * packed.shape[0] + offs[0][0], offs[0][1]) for dd in range(8)],
                            [dmod.shape] * 8)
    dmod_all = jnp.concatenate(dmod_all, axis=0)
    dmod_mine = lax.dynamic_slice_in_dim(dmod_all, chip * shard, shard, axis=1)

    def ada_grad(r, p):
        acc = jnp.zeros((r[0].shape[0], shard), F32)
        for b in range(8):
            acc = acc + r[0][:, b:b + 1] * p[0][b:b + 1, :]
        return [acc], []

    (g_w_ada,) = _rowcall(ada_grad, [_pad_cols(sc_all.T, LANES)], [dmod_mine], [(shard, F32)], [], tile=ROW_TILE_WIDE,
                          name="ada_grad")

    grads = dict(
        w_ada=g_w_ada[None], b_ada=g_b_ada,
        gdn_conv_w=lax.dynamic_slice_in_dim(g_conv, chip * gdn_conv_w.shape[2], gdn_conv_w.shape[2], axis=1)[None],
        gdn_a_log=g_a_log[:, :GDN_HEADS], gdn_dt_bias=g_dt[:, :GDN_HEADS], gdn_norm_w=g_wn, mla_q_norm_w=g_wq,
        mla_kv_norm_w=g_wkv, qkn_q_nope=g_wqn, qkn_q_rope=g_wqr[:, :MLA_ROPE], qkn_k_nope=g_wkn,
        qkn_k_rope=g_wkr[:, :MLA_ROPE], mla_out_norm_w=g_won)

    hosted = ["ffn2_w_in", "ffn2_w_out", "w_in", "mla_w_uq", "mla_w_ukv", "w_out"]
    halves = [_scatter_sum(part, got, "rs_sum_" + nme)
              for nme, part, got in zip(hosted, parts2 + parts_m, got2 + got_m)]
    for nme, full in zip(hosted, _scatter_finish(halves, "rs_finish")):
        grads[nme] = full[:, :weights[nme].shape[2]][None]
    grads["ffn1_w_in"] = _reduce_scatter_chips(dw_f1_in, "rs_f1_in")[None]
    grads["ffn1_w_out"] = _reduce_scatter_chips(chip_rows(dw_f1_out), "rs_f1_out")[None]

    big = ["w_ada", "ffn1_w_in", "ffn1_w_out", "w_in", "mla_w_uq", "mla_w_ukv", "w_out", "ffn2_w_in", "ffn2_w_out"]
    delta, new_m, new_v = {}, {}, {}
    for nme in big:
        shp = weights[nme].shape
        dl, nm, nv = _adamw(weights[nme][0], grads[nme][0], moms_m[nme][0], moms_v[nme][0], "adamw_" + nme)
        delta[nme], new_m[nme], new_v[nme] = dl.reshape(shp), nm.reshape(shp), nv.reshape(shp)
    tiny = [nme for nme in names if nme not in big]
    shapes = [weights[nme].shape for nme in tiny]
    pw, poffs = _pack_rows([weights[nme] for nme in tiny])
    pg, _ = _pack_rows([grads[nme] for nme in tiny])
    pm, _ = _pack_rows([moms_m[nme] for nme in tiny])
    pv, _ = _pack_rows([moms_v[nme] for nme in tiny])
    pd, pnm, pnv = _adamw(pw, pg, pm, pv, "adamw_small")
    for nme, dl, nm, nv in zip(tiny, _unpack_rows(pd, poffs, shapes), _unpack_rows(pnm, poffs, shapes),
                               _unpack_rows(pnv, poffs, shapes)):
        delta[nme], new_m[nme], new_v[nme] = dl, nm, nv

    return (loss, grad_x, *[grads[nme].reshape(weights[nme].shape) for nme in names],
            *[delta[nme] for nme in names], *[new_m[nme] for nme in names], *[new_v[nme] for nme in names])
```

```python
import functools

import jax
import jax.numpy as jnp
from jax import lax
from jax.experimental import pallas as pl
from jax.experimental.pallas import tpu as pltpu

F32 = jnp.float32
BF16 = jnp.bfloat16
HI = lax.Precision.HIGHEST
MESH = pl.DeviceIdType.MESH

EPS = 1e-6
CHUNK = 64
D_FF = 2816
GDN_HEADS = 4
HEAD = 128
MLA_HEADS = 4
MLA_ROPE = 64
MLA_Q_LORA = 384
MLA_KV_LORA = 256
QK_PAD = 256
ATT_SCALE = (HEAD + MLA_ROPE) ** -0.5

ADAM_LR, ADAM_B1, ADAM_B2, ADAM_EPS, ADAM_WD, ADAM_STEP = 0.001, 0.9, 0.999, 1e-08, 0.01, 10

LANES = 128
SUBLANES = 8
VMEM_LIMIT = 56 * 2 ** 20
ROW_TILE = 1024
ROW_TILE_WIDE = 512


def _params(sem=None):
    return pltpu.CompilerParams(dimension_semantics=sem, vmem_limit_bytes=VMEM_LIMIT)


def _pick(n, cap, align):
    best = None
    d = align
    while d <= min(n, cap):
        if n % d == 0:
            best = d
        d += align
    return best if best is not None else n


def _iota(shape, dim):
    return lax.broadcasted_iota(jnp.int32, shape, dim)


def _rowcall(fn, rows, params, out_rows, out_accs, *, tile, name):
    rows = [r if isinstance(r, tuple) else (r, r.shape[1], 0) for r in rows]
    s = rows[0][0].shape[-2]
    t = min(tile, s)
    n = s // t
    n_in = len(rows) + len(params)
    n_row_out = len(out_rows)

    in_specs = []
    for r in rows:
        if len(r) == 3:
            in_specs.append(pl.BlockSpec((t, r[1]), functools.partial(lambda i, b: (i, b), b=r[2])))
        else:
            in_specs.append(pl.BlockSpec((None, t, r[1]), functools.partial(lambda i, b, h: (h, i, b), b=r[2], h=r[3])))
    in_specs += [pl.BlockSpec(p.shape, lambda i: (0, 0)) for p in params]
    out_shape, out_specs = [], []
    for o in out_rows:
        if len(o) == 2:
            out_shape.append(jax.ShapeDtypeStruct((s, o[0]), o[1]))
            out_specs.append(pl.BlockSpec((t, o[0]), lambda i: (i, 0)))
        else:
            out_shape.append(jax.ShapeDtypeStruct((o[2], s, o[0]), o[1]))
            out_specs.append(pl.BlockSpec((o[2], t, o[0]), lambda i: (0, i, 0)))
    out_shape += [jax.ShapeDtypeStruct(shape, F32) for shape in out_accs]
    out_specs += [pl.BlockSpec(shape, lambda i: (0, 0)) for shape in out_accs]

    def body(*refs):
        ins = refs[:n_in]
        outs = refs[n_in:]
        i = pl.program_id(0)
        vals = [r[...] for r in ins]
        row_outs, acc_outs = fn(vals[:len(rows)], vals[len(rows):])
        for r, v in zip(outs[:n_row_out], row_outs):
            if isinstance(v, (list, tuple)):
                for hh, piece in enumerate(v):
                    r[hh] = piece.astype(r.dtype)
            else:
                r[...] = v.astype(r.dtype)
        if out_accs:
            @pl.when(i == 0)
            def _():
                for r in outs[n_row_out:]:
                    r[...] = jnp.zeros(r.shape, F32)
            for r, v in zip(outs[n_row_out:], acc_outs):
                r[...] += v

    res = pl.pallas_call(
        body, name=name, grid=(n,), in_specs=in_specs, out_specs=out_specs, out_shape=out_shape,
        compiler_params=_params(("arbitrary",) if out_accs else ("parallel",)),
    )(*[r[0] for r in rows], *params)
    return list(res)


MM_TILE_MN = 1536
MM_VMEM_BUDGET = 44 * 2 ** 20


class _Hosted:
    def __init__(self, gather=(), scatter=()):
        self.gather, self.scatter = list(gather), list(scatter)
        self.operands = self.gather + self.scatter
        self.n = len(self.operands)

    def specs(self):
        return [pl.BlockSpec(memory_space=pl.ANY)] * self.n

    def out_shapes(self):
        return ([jax.ShapeDtypeStruct((4,) + x.shape, x.dtype) for x in self.gather]
                + [jax.ShapeDtypeStruct((SCATTER_COPIES, g.shape[1] // 2, g.shape[2]), g.dtype) for g in self.scatter])

    def scratch(self):
        return ((_gather_sems(len(self.gather)) if self.gather else [])
                + (_scatter_sems(len(self.scatter)) if self.scatter else []))

    def _phase(self, ph, ins, outs, sems):
        ng = len(self.gather)
        g_sems, s_sems = (sems[:3], sems[3:]) if ng else ((), sems)
        for slot in range(ng):
            _gather_phase(ph, ins[slot], outs[slot], *g_sems, slot)
        if ph != 1:
            for slot in range(len(self.scatter)):
                _scatter_phase(0 if ph == 0 else 1, ins[ng + slot], outs[ng + slot], *s_sems, slot)

    def open(self, step, steps, ins, outs, sems):
        if self.n:
            pl.when(step == 0)(lambda: self._phase(0, ins, outs, sems))
            pl.when(step == steps // 2)(lambda: self._phase(1, ins, outs, sems))

    def close(self, step, steps, ins, outs, sems):
        if self.n:
            pl.when(step == steps - 1)(lambda: self._phase(2, ins, outs, sems))


def _mm(a, b, mode, out_dtype, name, hi=False, gather=(), chips=None, scatter=()):
    b_shape = b.shape
    if chips == "b":
        b_shape = (b.shape[1], 4 * b.shape[2])
    if mode == "nn":
        (m, k), (_, n) = a.shape, b_shape
        dims = (((1,), (0,)), ((), ()))
    elif mode == "nt":
        (m, k), (n, _) = a.shape, b_shape
        dims = (((1,), (1,)), ((), ()))
    else:
        (k, m), (_, n) = a.shape, b_shape
        dims = (((0,), (0,)), ((), ()))
    tn = _pick(n // 4 if chips and mode != "nt" else n, MM_TILE_MN, LANES)
    tk = _pick(k // 4 if chips and mode == "nt" else k, 2048 if mode == "tn" else MM_TILE_MN, LANES)
    tm = _pick(m, MM_TILE_MN if mode == "tn" else 1024, LANES if mode == "tn" else 16)
    if mode != "tn" and m % (2 * tm) == 0:
        blocks = 2 * tm * (2 * tn * (2 + jnp.dtype(out_dtype).itemsize) + 2 * tk * a.dtype.itemsize)
        if blocks + 2 * tk * tn * b.dtype.itemsize <= MM_VMEM_BUDGET:
            tm = 2 * tm
    nk = k // tk
    nb = (n // 4) // tn
    kb = (k // 4) // tk
    if mode == "nn":
        a_spec = pl.BlockSpec((tm, tk), lambda i, j, kk: (i, kk))
        b_spec = pl.BlockSpec((tk, tn), lambda i, j, kk: (kk, j))
        if chips == "b":
            b_spec = pl.BlockSpec((None, tk, tn), lambda i, j, kk: (j // nb, kk, j % nb))
    elif mode == "nt":
        a_spec = pl.BlockSpec((tm, tk), lambda i, j, kk: (i, kk))
        b_spec = pl.BlockSpec((tn, tk), lambda i, j, kk: (j, kk))
        if chips == "b":
            b_spec = pl.BlockSpec((None, tn, tk), lambda i, j, kk: (kk // kb, j, kk % kb))
    else:
        a_spec = pl.BlockSpec((tk, tm), lambda i, j, kk: (kk, i))
        b_spec = pl.BlockSpec((tk, tn), lambda i, j, kk: (kk, j))
    out_spec = pl.BlockSpec((tm, tn), lambda i, j, kk: (i, j))
    out_shape = jax.ShapeDtypeStruct((m, n), out_dtype)
    if chips == "out":
        out_spec = pl.BlockSpec((None, tm, tn), lambda i, j, kk: (j // nb, i, j % nb))
        out_shape = jax.ShapeDtypeStruct((4, m, n // 4), out_dtype)

    host = _Hosted(gather, scatter)
    ng = host.n
    grid = (m // tm, n // tn, nk)
    steps = grid[0] * grid[1] * grid[2]

    def body(*refs):
        a_ref, b_ref = refs[:2]
        x_refs = refs[2:2 + ng]
        o_ref = refs[2 + ng]
        got_refs = refs[3 + ng:3 + 2 * ng]
        acc_ref = refs[3 + 2 * ng]
        sems = refs[4 + 2 * ng:]
        kk = pl.program_id(2)
        step = (pl.program_id(0) * grid[1] + pl.program_id(1)) * nk + kk
        host.open(step, steps, x_refs, got_refs, sems)

        @pl.when(kk == 0)
        def _():
            acc_ref[...] = jnp.zeros(acc_ref.shape, F32)

        av, bv = a_ref[...], b_ref[...]
        if hi:
            acc_ref[...] += lax.dot_general(av, bv, dims, precision=HI, preferred_element_type=F32)
        else:
            acc_ref[...] += lax.dot_general(av.astype(BF16), bv.astype(BF16), dims,
                                            preferred_element_type=F32)

        @pl.when(kk == nk - 1)
        def _():
            o_ref[...] = acc_ref[...].astype(o_ref.dtype)

        host.close(step, steps, x_refs, got_refs, sems)

    res = pl.pallas_call(
        body, name=name, grid=grid,
        in_specs=[a_spec, b_spec] + host.specs(),
        out_specs=[out_spec] + host.specs(),
        out_shape=[out_shape] + host.out_shapes(),
        scratch_shapes=[pltpu.VMEM((tm, tn), F32)] + host.scratch(),
        compiler_params=_params(("arbitrary",) * 3 if ng else ("parallel", "parallel", "arbitrary")),
    )(a, b, *host.operands)
    return res if ng else res[0]


def _rms(x, w=None, n=None):
    n = x.shape[-1] if n is None else n
    y = x * lax.rsqrt(jnp.sum(x * x, axis=-1, keepdims=True) * (1.0 / n) + EPS)
    return y if w is None else y * w


def _silu(x):
    return x * jax.nn.sigmoid(x)


def _softplus(x):
    return jnp.maximum(x, 0.0) + jnp.log1p(jnp.exp(-jnp.abs(x)))


def _split(x, widths):
    out, o = [], 0
    for w in widths:
        out.append(x[:, o:o + w])
        o += w
    return out


def _modulate(x, s, sh):
    return _rms(x) * (1.0 + s) + sh


def _rope_rot(x):
    r, c = _iota((LANES, LANES), 0), _iota((LANES, LANES), 1)
    half = MLA_ROPE // 2
    perm = (((r < half) & (c == r + half)) | ((r >= half) & (r < MLA_ROPE) & (c == r - half))).astype(F32)
    return jnp.dot(x, perm, precision=HI, preferred_element_type=F32)


def _rope(x, cos2, sin2):
    return x * cos2 + _rope_rot(x) * sin2


def _gdn_prep_core(qkv_parts, gab, a_log, dt_bias):
    act = [_silu(p) for p in qkv_parts]
    qs = [p * lax.rsqrt(jnp.sum(p * p, -1, keepdims=True) + EPS) * (HEAD ** -0.5) for p in act[:4]]
    ks = [p * lax.rsqrt(jnp.sum(p * p, -1, keepdims=True) + EPS) for p in act[4:8]]
    lane = _iota(gab.shape, 1)
    g = -jnp.exp(a_log) * _softplus(gab + dt_bias)
    beta = jax.nn.sigmoid(gab)
    gb = jnp.where(lane < GDN_HEADS, g, jnp.where(lane < 2 * GDN_HEADS, beta, 0.0))
    return (jnp.concatenate(qs, 1), jnp.concatenate(ks, 1), jnp.concatenate(act[8:], 1), gb)


def _mla_prep_core(cq, ckv, kr, cos2, sin2, wq, wkv, wkr):
    cqn = _rms(cq, wq)
    ckvn = _rms(ckv, wkv)
    k_rope = _rope(_rms(kr, wkr, MLA_ROPE), cos2, sin2)
    return cqn, ckvn, k_rope


def _qk_prep_core(qn_parts, qr_parts, kn_parts, v_parts, k_rope, cos2, sin2, wqn, wqr, wkn):
    qs, ks = [], []
    for h in range(MLA_HEADS):
        qn = _rms(qn_parts[h], wqn) * ATT_SCALE
        qr = _rope(_rms(qr_parts[h], wqr, MLA_ROPE), cos2, sin2) * ATT_SCALE
        qs.append(jnp.concatenate([qn, qr], 1))
        ks.append(jnp.concatenate([_rms(kn_parts[h], wkn), k_rope], 1))
    return qs, ks, list(v_parts)


def _mix_post_core(o_parts, gz_parts, ob_parts, wn, won):
    oa = [_rms(o, wn) * _silu(z) for o, z in zip(o_parts, gz_parts)]
    ob = [_rms(o, won) for o in ob_parts]
    return jnp.concatenate(oa + ob, 1)


CONV_K = 4
HALO = SUBLANES


def _conv_fwd(proj, w8, name):
    s = proj.shape[0]
    c = w8.shape[1]
    t = min(ROW_TILE_WIDE, s)
    n = s // t
    hb = t // HALO

    def body(x_ref, prev_ref, w_ref, o_ref, buf):
        i = pl.program_id(0)
        buf[pl.ds(0, HALO), :] = jnp.where(i > 0, prev_ref[...], 0.0)
        buf[pl.ds(HALO, t), :] = x_ref[...]
        acc = jnp.zeros((t, c), F32)
        for k in range(CONV_K):
            acc = acc + w_ref[k:k + 1, :] * buf[pl.ds(HALO - (CONV_K - 1) + k, t), :]
        o_ref[...] = acc

    return pl.pallas_call(
        body, name=name, grid=(n,),
        in_specs=[pl.BlockSpec((t, c), lambda i: (i, 0)),
                  pl.BlockSpec((HALO, c), lambda i: (jnp.maximum(i * hb - 1, 0), 0)),
                  pl.BlockSpec(w8.shape, lambda i: (0, 0))],
        out_specs=pl.BlockSpec((t, c), lambda i: (i, 0)),
        out_shape=jax.ShapeDtypeStruct((s, c), F32),
        scratch_shapes=[pltpu.VMEM((t + HALO, c), F32)],
        compiler_params=_params(("parallel",)),
    )(proj, proj, w8)


def _conv_bwd(proj, dy, w8, name):
    s = proj.shape[0]
    c = w8.shape[1]
    t = min(ROW_TILE_WIDE, s)
    n = s // t
    hb = t // HALO

    def body(x_ref, prev_ref, dy_ref, next_ref, w_ref, dx_ref, dw_ref, bufx, bufd):
        i = pl.program_id(0)
        bufx[pl.ds(0, HALO), :] = jnp.where(i > 0, prev_ref[...], 0.0)
        bufx[pl.ds(HALO, t), :] = x_ref[...]
        bufd[pl.ds(0, t), :] = dy_ref[...]
        bufd[pl.ds(t, HALO), :] = jnp.where(i < n - 1, next_ref[...], 0.0)

        @pl.when(i == 0)
        def _():
            dw_ref[...] = jnp.zeros(dw_ref.shape, F32)

        dyv = dy_ref[...]
        acc = jnp.zeros((t, c), F32)
        for k in range(CONV_K):
            acc = acc + w_ref[k:k + 1, :] * bufd[pl.ds(CONV_K - 1 - k, t), :]
            dw_ref[k:k + 1, :] += jnp.sum(dyv * bufx[pl.ds(HALO - (CONV_K - 1) + k, t), :], axis=0, keepdims=True)
        dx_ref[...] = acc

    return pl.pallas_call(
        body, name=name, grid=(n,),
        in_specs=[pl.BlockSpec((t, c), lambda i: (i, 0)),
                  pl.BlockSpec((HALO, c), lambda i: (jnp.maximum(i * hb - 1, 0), 0)),
                  pl.BlockSpec((t, c), lambda i: (i, 0)),
                  pl.BlockSpec((HALO, c), lambda i: (jnp.minimum((i + 1) * hb, s // HALO - 1), 0)),
                  pl.BlockSpec(w8.shape, lambda i: (0, 0))],
        out_specs=[pl.BlockSpec((t, c), lambda i: (i, 0)), pl.BlockSpec(w8.shape, lambda i: (0, 0))],
        out_shape=[jax.ShapeDtypeStruct((s, c), F32), jax.ShapeDtypeStruct(w8.shape, F32)],
        scratch_shapes=[pltpu.VMEM((t + HALO, c), F32), pltpu.VMEM((t + HALO, c), F32)],
        compiler_params=_params(("arbitrary",)),
    )(proj, proj, dy, dy, w8)


_B_NN = (((2,), (1,)), ((0,), (0,)))
_B_NT = (((2,), (2,)), ((0,), (0,)))
_B_TN = (((1,), (1,)), ((0,), (0,)))


def _dot3(a, b, dims):
    return lax.dot_general(a, b, dims, precision=lax.Precision.HIGH, preferred_element_type=F32)


def _bdot_hi(a, b):
    return _dot3(a, b, _B_NN)


class _Dots:
    nn = staticmethod(lambda a, b: _dot3(a, b, _B_NN))
    nt = staticmethod(lambda a, b: _dot3(a, b, _B_NT))
    tn = staticmethod(lambda a, b: _dot3(a, b, _B_TN))


def _unit_lower_inverse(a, dots):
    c = a.shape[-1]
    ri, ci = _iota(a.shape, 1), _iota(a.shape, 2)
    inner = (ri // 2) == (ci // 2)
    t = (ri == ci).astype(F32) - jnp.where(inner, a, 0.0)
    blk = 4
    while blk <= c:
        outer = (ri // blk) == (ci // blk)
        low = jnp.where(outer & jnp.logical_not(inner), a, 0.0)
        t = t - dots.nn(dots.nn(t, low), t)
        inner = outer
        blk *= 2
    return t


def _stack(xs):
    return jnp.concatenate([x[None] for x in xs], axis=0)


def _gdn_local(dots, q, k, v, gbs):
    b, c, _ = q.shape
    gcols, bcols = [], []
    for gb in gbs:
        lane = _iota(gb.shape, 1)
        for h in range(GDN_HEADS):
            gcols.append(jnp.sum(jnp.where(lane == h, gb, 0.0), axis=1, keepdims=True))
            bcols.append(jnp.sum(jnp.where(lane == GDN_HEADS + h, gb, 0.0), axis=1, keepdims=True))
    gcol, bcol = _stack(gcols), _stack(bcols)
    ri, ci = _iota((b, c, c), 1), _iota((b, c, c), 2)
    incl = ri >= ci
    tril = incl.astype(F32)
    g_cc = _bdot_hi(tril, jnp.broadcast_to(gcol, (b, c, c)))
    g_row = _bdot_hi(jnp.ones((b, c, c), F32), jnp.where(ri == ci, g_cc, 0.0))
    g_cl = _bdot_hi(tril, jnp.broadcast_to(gcol, (b, c, HEAD)))
    g_last = jnp.sum(jnp.broadcast_to(gcol, (b, c, HEAD)), axis=1, keepdims=True)
    decay = jnp.where(incl, jnp.exp(jnp.where(incl, g_cc - g_row, 0.0)), 0.0)
    kk = dots.nt(k, k)
    minv = _unit_lower_inverse(jnp.where(ri > ci, bcol * kk * decay, 0.0), dots)
    e_g = jnp.exp(g_cl)
    u = dots.nn(minv, v * bcol)
    wk = dots.nn(minv, k * (bcol * e_g))
    qk = dots.nt(q, k) * decay
    return u, wk, q * e_g, k * jnp.exp(g_last - g_cl), qk, jnp.exp(g_last)


def _gdn_scan(dots, states, u, wk, qd, kd, qk, gl_tile):
    lane, row = _iota(gl_tile.shape, 1), _iota(gl_tile.shape, 0)
    gl = _stack([
        jnp.sum(jnp.sum(jnp.where((lane == h) & (row == 0), gl_tile, 0.0), axis=1, keepdims=True),
                axis=0, keepdims=True) for h in range(GDN_HEADS)])
    v_new = u - dots.nn(wk, states)
    o = dots.nn(qd, states) + dots.nn(qk, v_new)
    return states * gl + dots.tn(kd, v_new), o


def _heads(x):
    return jnp.stack(_split(x, HW4))


GDN_W = GDN_HEADS * HEAD
HW4 = [HEAD] * GDN_HEADS
LOCAL_CHUNKS = 4
_CHUNK_ROWS = [pl.ds(cc * CHUNK, CHUNK) for cc in range(LOCAL_CHUNKS)]


def _chunk_heads(ref):
    return jnp.concatenate([_heads(ref[rows, :]) for rows in _CHUNK_ROWS], 0)


def _gdn_local_fwd(q, k, v, gb, name):
    s = q.shape[0]
    t = LOCAL_CHUNKS * CHUNK

    def body(q_ref, k_ref, v_ref, gb_ref, u_ref, wk_ref, qd_ref, kd_ref, qk_ref, gl_ref):
        u, wk, qd, kd, qk, gl = _gdn_local(_Dots, _chunk_heads(q_ref), _chunk_heads(k_ref),
                                           _chunk_heads(v_ref), [gb_ref[rows, :] for rows in _CHUNK_ROWS])
        lane = _iota((CHUNK, LANES), 1)
        for cc, rows in enumerate(_CHUNK_ROWS):
            gl_tile = jnp.zeros((CHUNK, LANES), F32)
            for h in range(GDN_HEADS):
                b, cols = cc * GDN_HEADS + h, pl.ds(h * HEAD, HEAD)
                u_ref[rows, cols] = u[b]
                wk_ref[rows, cols] = wk[b]
                qd_ref[rows, cols] = qd[b]
                kd_ref[rows, cols] = kd[b]
                qk_ref[h, rows, :] = qk[b]
                gl_tile = gl_tile + jnp.where(lane == h, gl[b], 0.0)
            gl_ref[rows, :] = gl_tile

    row = pl.BlockSpec((t, GDN_W), lambda i: (i, 0))
    lane = pl.BlockSpec((t, LANES), lambda i: (i, 0))
    qks = pl.BlockSpec((GDN_HEADS, t, CHUNK), lambda i: (0, i, 0))
    return pl.pallas_call(
        body, name=name, grid=(s // t,),
        in_specs=[row, row, row, lane],
        out_specs=[row, row, row, row, qks, lane],
        out_shape=[jax.ShapeDtypeStruct((s, GDN_W), F32)] * 4
        + [jax.ShapeDtypeStruct((GDN_HEADS, s, CHUNK), F32), jax.ShapeDtypeStruct((s, LANES), F32)],
        compiler_params=_params(("parallel",)),
    )(q, k, v, gb)


def _gdn_local_bwd(q, k, v, gb, du, dwk, dqd, dkd, dqk, dgl, name):
    s = q.shape[0]
    t = LOCAL_CHUNKS * CHUNK

    def body(q_ref, k_ref, v_ref, gb_ref, du_ref, dwk_ref, dqd_ref, dkd_ref, dqk_ref, dgl_ref,
             dq_ref, dk_ref, dv_ref, dgb_ref):
        _, vjp = jax.vjp(functools.partial(_gdn_local, _Dots), _chunk_heads(q_ref), _chunk_heads(k_ref),
                         _chunk_heads(v_ref), [gb_ref[rows, :] for rows in _CHUNK_ROWS])
        lane = _iota((CHUNK, LANES), 1)
        dqk = jnp.stack([dqk_ref[h, rows, :] for rows in _CHUNK_ROWS for h in range(GDN_HEADS)])
        dgl = jnp.stack([jnp.sum(jnp.where(lane == h, dgl_ref[rows, :], 0.0), axis=0, keepdims=True)
                         for rows in _CHUNK_ROWS for h in range(GDN_HEADS)])
        d_q, d_k, d_v, d_gbs = vjp((_chunk_heads(du_ref), _chunk_heads(dwk_ref), _chunk_heads(dqd_ref),
                                    _chunk_heads(dkd_ref), dqk, dgl))
        for cc, rows in enumerate(_CHUNK_ROWS):
            for h in range(GDN_HEADS):
                b, cols = cc * GDN_HEADS + h, pl.ds(h * HEAD, HEAD)
                dq_ref[rows, cols] = d_q[b]
                dk_ref[rows, cols] = d_k[b]
                dv_ref[rows, cols] = d_v[b]
            dgb_ref[rows, :] = d_gbs[cc]

    row = pl.BlockSpec((t, GDN_W), lambda i: (i, 0))
    lane = pl.BlockSpec((t, LANES), lambda i: (i, 0))
    qks = pl.BlockSpec((GDN_HEADS, t, CHUNK), lambda i: (0, i, 0))
    return pl.pallas_call(
        body, name=name, grid=(s // t,),
        in_specs=[row, row, row, lane, row, row, row, row, qks, lane],
        out_specs=[row, row, row, lane],
        out_shape=[jax.ShapeDtypeStruct((s, GDN_W), F32)] * 3 + [jax.ShapeDtypeStruct((s, LANES), F32)],
        compiler_params=_params(("parallel",)),
    )(q, k, v, gb, du, dwk, dqd, dkd, dqk, dgl)


SCAN_CHUNKS = 4


def _scan_rows(s):
    k = min(SCAN_CHUNKS, s // CHUNK)
    return k * CHUNK, [pl.ds(cc * CHUNK, CHUNK) for cc in range(k)]


def _gdn_scan_fwd(u, wk, qd, kd, qk, gl, name):
    s = u.shape[0]
    nc = s // CHUNK
    t, chunk_rows = _scan_rows(s)

    def body(u_ref, wk_ref, qd_ref, kd_ref, qk_ref, gl_ref, o_ref, st_ref, state):
        i = pl.program_id(0)

        @pl.when(i == 0)
        def _():
            state[...] = jnp.zeros(state.shape, F32)

        st = state[...]
        for cc, rows in enumerate(chunk_rows):
            st_ref[cc] = st
            st, o = _gdn_scan(_Dots, st, _heads(u_ref[rows, :]), _heads(wk_ref[rows, :]), _heads(qd_ref[rows, :]),
                              _heads(kd_ref[rows, :]), qk_ref[:, rows, :], gl_ref[rows, :])
            o_ref[rows, :] = jnp.concatenate([o[h] for h in range(GDN_HEADS)], 1)
        state[...] = st

    row = pl.BlockSpec((t, GDN_W), lambda i: (i, 0))
    return pl.pallas_call(
        body, name=name, grid=(s // t,),
        in_specs=[row, row, row, row, pl.BlockSpec((GDN_HEADS, t, CHUNK), lambda i: (0, i, 0)),
                  pl.BlockSpec((t, LANES), lambda i: (i, 0))],
        out_specs=[row, pl.BlockSpec((len(chunk_rows), GDN_HEADS, HEAD, HEAD), lambda i: (i, 0, 0, 0))],
        out_shape=[jax.ShapeDtypeStruct((s, GDN_W), F32),
                   jax.ShapeDtypeStruct((nc, GDN_HEADS, HEAD, HEAD), F32)],
        scratch_shapes=[pltpu.VMEM((GDN_HEADS, HEAD, HEAD), F32)],
        compiler_params=_params(("arbitrary",)),
    )(u, wk, qd, kd, qk, gl)


def _gdn_scan_bwd(u, wk, qd, kd, qk, gl, st, do, name):
    s = u.shape[0]
    t, chunk_rows = _scan_rows(s)
    n = s // t

    def body(u_ref, wk_ref, qd_ref, kd_ref, qk_ref, gl_ref, st_ref, do_ref,
             du_ref, dwk_ref, dqd_ref, dkd_ref, dqk_ref, dgl_ref, dstate):
        i = pl.program_id(0)

        @pl.when(i == 0)
        def _():
            dstate[...] = jnp.zeros(dstate.shape, F32)

        unheads = lambda x: jnp.concatenate([x[h] for h in range(GDN_HEADS)], 1)
        ds = dstate[...]
        for cc in reversed(range(len(chunk_rows))):
            rows = chunk_rows[cc]
            _, vjp = jax.vjp(functools.partial(_gdn_scan, _Dots), st_ref[cc], _heads(u_ref[rows, :]),
                             _heads(wk_ref[rows, :]), _heads(qd_ref[rows, :]), _heads(kd_ref[rows, :]),
                             qk_ref[:, rows, :], gl_ref[rows, :])
            ds, d_u, d_wk, d_qd, d_kd, d_qk, d_gl = vjp((ds, _heads(do_ref[rows, :])))
            dqk_ref[:, rows, :] = d_qk
            du_ref[rows, :] = unheads(d_u)
            dwk_ref[rows, :] = unheads(d_wk)
            dqd_ref[rows, :] = unheads(d_qd)
            dkd_ref[rows, :] = unheads(d_kd)
            dgl_ref[rows, :] = d_gl
        dstate[...] = ds

    rev = lambda i: (n - 1 - i, 0)
    row = pl.BlockSpec((t, GDN_W), rev)
    lane = pl.BlockSpec((t, LANES), rev)
    qks = pl.BlockSpec((GDN_HEADS, t, CHUNK), lambda i: (0, n - 1 - i, 0))
    return pl.pallas_call(
        body, name=name, grid=(n,),
        in_specs=[row, row, row, row, qks, lane,
                  pl.BlockSpec((len(chunk_rows), GDN_HEADS, HEAD, HEAD), lambda i: (n - 1 - i, 0, 0, 0)), row],
        out_specs=[row, row, row, row, qks, lane],
        out_shape=[jax.ShapeDtypeStruct((s, GDN_W), F32)] * 4
        + [jax.ShapeDtypeStruct((GDN_HEADS, s, CHUNK), F32), jax.ShapeDtypeStruct((s, LANES), F32)],
        scratch_shapes=[pltpu.VMEM((GDN_HEADS, HEAD, HEAD), F32)],
        compiler_params=_params(("arbitrary",)),
    )(u, wk, qd, kd, qk, gl, st, do)


def _chunk_mask(i, j, t):
    r = i * t + _iota((t, t), 0)
    c = j * t + _iota((t, t), 1)
    return (r // CHUNK) >= (c // CHUNK)


ATT_TILE = 1024
ATT_Q_TILES = 1
ATT_BWD_TILE = 1024


def _attn_fwd(q, k, v, name, gather=()):
    nh, s = MLA_HEADS, q.shape[0]
    tk = min(ATT_TILE, s)
    tq = min(ATT_Q_TILES * tk, s)
    qk = tq // tk
    nq, n = s // tq, s // tk
    nt = (((1,), (1,)), ((), ()))
    host = _Hosted(gather)
    ng = host.n
    steps = nh * nq * n

    def body(*refs):
        q_ref, k_ref, v_ref = refs[:3]
        x_refs = refs[3:3 + ng]
        o_ref, lse_ref = refs[3 + ng:5 + ng]
        got_refs = refs[5 + ng:5 + 2 * ng]
        m_sc, l_sc, acc_sc = refs[5 + 2 * ng:8 + 2 * ng]
        sems = refs[8 + 2 * ng:]
        i, j = pl.program_id(1), pl.program_id(2)
        step_no = (pl.program_id(0) * nq + i) * n + j
        host.open(step_no, steps, x_refs, got_refs, sems)

        @pl.when(j == 0)
        def _():
            m_sc[...] = jnp.full(m_sc.shape, -jnp.inf, F32)
            l_sc[...] = jnp.zeros(l_sc.shape, F32)
            acc_sc[...] = jnp.zeros(acc_sc.shape, F32)

        def step(masked):
            sc = lax.dot_general(q_ref[...], k_ref[...], nt, preferred_element_type=F32)
            if masked:
                r = i * tq + _iota((tq, tk), 0)
                c = j * tk + _iota((tq, tk), 1)
                sc = jnp.where((r // CHUNK) >= (c // CHUNK), sc, -jnp.inf)
            m_prev = m_sc[:, :1]
            m_new = jnp.maximum(m_prev, jnp.max(sc, axis=1, keepdims=True))
            alpha = jnp.exp(m_prev - m_new)
            p = jnp.exp(sc - m_new)
            l_sc[...] = jnp.broadcast_to(alpha * l_sc[:, :1] + jnp.sum(p, axis=1, keepdims=True), l_sc.shape)
            acc_sc[...] = alpha * acc_sc[...] + jnp.dot(p.astype(BF16), v_ref[...], preferred_element_type=F32)
            m_sc[...] = jnp.broadcast_to(m_new, m_sc.shape)

        pl.when(j < i * qk)(lambda: step(False))
        pl.when(j // qk == i)(lambda: step(True))

        @pl.when(j == n - 1)
        def _():
            o_ref[...] = acc_sc[...] / l_sc[:, :1]
            lse_ref[...] = m_sc[...] + jnp.log(l_sc[...])

        host.close(step_no, steps, x_refs, got_refs, sems)

    qrow = lambda h, i, j: (i, h)
    krow = lambda h, i, j: (jnp.minimum(j, (i + 1) * qk - 1), h)
    res = pl.pallas_call(
        body, name=name, grid=(nh, nq, n),
        in_specs=[pl.BlockSpec((tq, QK_PAD), qrow), pl.BlockSpec((tk, QK_PAD), krow),
                  pl.BlockSpec((tk, HEAD), krow)] + host.specs(),
        out_specs=[pl.BlockSpec((tq, HEAD), qrow), pl.BlockSpec((None, tq, LANES), lambda h, i, j: (h, i, 0))]
        + host.specs(),
        out_shape=[jax.ShapeDtypeStruct((s, nh * HEAD), F32), jax.ShapeDtypeStruct((nh, s, LANES), F32)]
        + host.out_shapes(),
        scratch_shapes=[pltpu.VMEM((tq, LANES), F32), pltpu.VMEM((tq, LANES), F32), pltpu.VMEM((tq, HEAD), F32)]
        + host.scratch(),
        compiler_params=_params(("arbitrary",) * 3 if ng else ("parallel", "parallel", "arbitrary")),
    )(q, k, v, *host.operands)
    return res[0], res[1], list(res[2:])


def _attn_bwd(q, k, v, o, do, lse, name, scatter=()):
    nh, s = MLA_HEADS, q.shape[0]
    t = min(ATT_BWD_TILE, s)
    n = s // t
    tn = (((0,), (0,)), ((), ()))
    nt = (((1,), (1,)), ((), ()))
    host = _Hosted(scatter=scatter)
    nx = host.n
    steps = nh * n * n

    def body(*refs):
        q_ref, k_ref, v_ref, o_ref, do_ref, lse_ref = refs[:6]
        x_refs = refs[6:6 + nx]
        dq_ref, dk_ref, dv_ref = refs[6 + nx:9 + nx]
        got_refs = refs[9 + nx:9 + 2 * nx]
        dk_acc, dv_acc, dq_acc = refs[9 + 2 * nx:12 + 2 * nx]
        sems = refs[12 + 2 * nx:]
        j, i = pl.program_id(1), pl.program_id(2)
        step_no = (pl.program_id(0) * n + j) * n + i
        host.open(step_no, steps, x_refs, got_refs, sems)

        @pl.when(i + j == 0)
        def _():
            dq_acc[...] = jnp.zeros(dq_acc.shape, F32)

        @pl.when(i == 0)
        def _():
            dk_acc[...] = jnp.zeros(dk_acc.shape, F32)
            dv_acc[...] = jnp.zeros(dv_acc.shape, F32)

        def step(masked):
            qv, kv, do = q_ref[...], k_ref[...], do_ref[...]
            sc = lax.dot_general(qv, kv, nt, preferred_element_type=F32)
            p = jnp.exp(sc - lse_ref[:, :1])
            if masked:
                p = jnp.where(_chunk_mask(i, j, t), p, 0.0)
            dob = do.astype(BF16)
            dp = lax.dot_general(dob, v_ref[...], nt, preferred_element_type=F32)
            ds = (p * (dp - jnp.sum(do * o_ref[...], axis=1, keepdims=True))).astype(BF16)
            dv_acc[...] += lax.dot_general(p.astype(BF16), dob, tn, preferred_element_type=F32)
            dk_acc[...] += lax.dot_general(ds, qv, tn, preferred_element_type=F32)
            rows = pl.ds(pl.multiple_of(i * t, t), t)
            dq_acc[rows, :] += jnp.dot(ds, kv, preferred_element_type=F32)

        pl.when(i > j)(lambda: step(False))
        pl.when(i == j)(lambda: step(True))

        @pl.when(i == n - 1)
        def _():
            dk_ref[...] = dk_acc[...]
            dv_ref[...] = dv_acc[...]

        @pl.when(i + j == 2 * (n - 1))
        def _():
            dq_ref[...] = dq_acc[...]

        host.close(step_no, steps, x_refs, got_refs, sems)

    qrow = lambda h, j, i: (jnp.maximum(i, j), h)
    krow = lambda h, j, i: (j, h)
    res = pl.pallas_call(
        body, name=name, grid=(nh, n, n),
        in_specs=[pl.BlockSpec((t, QK_PAD), qrow), pl.BlockSpec((t, QK_PAD), krow), pl.BlockSpec((t, HEAD), krow),
                  pl.BlockSpec((t, HEAD), qrow), pl.BlockSpec((t, HEAD), qrow),
                  pl.BlockSpec((None, t, LANES), lambda h, j, i: (h, jnp.maximum(i, j), 0))] + host.specs(),
        out_specs=[pl.BlockSpec((s, QK_PAD), lambda h, j, i: (0, h)),
                   pl.BlockSpec((t, QK_PAD), krow), pl.BlockSpec((t, HEAD), krow)] + host.specs(),
        out_shape=[jax.ShapeDtypeStruct((s, nh * QK_PAD), F32), jax.ShapeDtypeStruct((s, nh * QK_PAD), F32),
                   jax.ShapeDtypeStruct((s, nh * HEAD), F32)] + host.out_shapes(),
        scratch_shapes=[pltpu.VMEM((t, QK_PAD), F32), pltpu.VMEM((t, HEAD), F32), pltpu.VMEM((s, QK_PAD), F32)]
        + host.scratch(),
        compiler_params=_params(("arbitrary", "arbitrary", "arbitrary")),
    )(q, k, v, o, do, lse, *host.operands)
    return res[0], res[1], res[2], list(res[3:])


def _place():
    return lax.axis_index("x"), lax.axis_index("y"), lax.axis_index("c")


def _allgather8(x, name):
    r, c = x.shape

    def body(x_ref, out_ref, send_sems, recv_sems, local_sem):
        mx, my, mc = _place()
        me = 4 * mx + 2 * my + mc
        mine = pltpu.make_async_copy(x_ref, out_ref.at[me], local_sem)
        mine.start()
        copies = []
        for d in range(1, 8):
            px = 1 - mx if d & 4 else mx
            py = 1 - my if d & 2 else my
            pc = 1 - mc if d & 1 else mc
            cp = pltpu.make_async_remote_copy(
                src_ref=x_ref, dst_ref=out_ref.at[me], send_sem=send_sems.at[d - 1], recv_sem=recv_sems.at[d - 1],
                device_id=(px, py, pc), device_id_type=MESH)
            cp.start()
            copies.append(cp)
        for cp in copies:
            cp.wait()
        mine.wait()

    return pl.pallas_call(
        body, name=name,
        out_shape=jax.ShapeDtypeStruct((8, r, c), x.dtype),
        in_specs=[pl.BlockSpec(memory_space=pltpu.VMEM)],
        out_specs=pl.BlockSpec(memory_space=pltpu.VMEM),
        scratch_shapes=[pltpu.SemaphoreType.DMA((7,)), pltpu.SemaphoreType.DMA((7,)), pltpu.SemaphoreType.DMA],
        compiler_params=pltpu.CompilerParams(vmem_limit_bytes=VMEM_LIMIT),
    )(x)


def _allgather_chips(x, name):
    r, c = x.shape

    def body(x_ref, out_ref, send_sems, recv_sems, local_sems):
        for phase in range(3):
            _gather_phase(phase, x_ref, out_ref, send_sems, recv_sems, local_sems, 0)

    return pl.pallas_call(
        body, name=name,
        out_shape=jax.ShapeDtypeStruct((4, r, c), x.dtype),
        in_specs=[pl.BlockSpec(memory_space=pltpu.VMEM)],
        out_specs=pl.BlockSpec(memory_space=pltpu.VMEM),
        scratch_shapes=_gather_sems(1),
        compiler_params=pltpu.CompilerParams(vmem_limit_bytes=VMEM_LIMIT),
    )(x)


GATHER_COPIES = 6


def _gather_sems(n):
    return [pltpu.SemaphoreType.DMA((GATHER_COPIES * n,)), pltpu.SemaphoreType.DMA((GATHER_COPIES * n,)),
            pltpu.SemaphoreType.DMA((n,))]


def _gather_phase(phase, x_ref, out_ref, send_sems, recv_sems, local_sems, slot):
    mx, my, mc = _place()
    j = 2 * mx + my
    rh = x_ref.shape[0] // 2
    base = GATHER_COPIES * slot
    chips = [(1 - mx, my), (mx, 1 - my), (1 - mx, 1 - my)]
    sibling = (mx, my, 1 - mc)

    def half(jj, hc):
        return out_ref.at[jj, pl.ds(hc * rh, rh), :]

    def over_ici(kk, block):
        px, py = chips[kk]
        return pltpu.make_async_remote_copy(
            src_ref=x_ref.at[pl.ds(mc * rh, rh), :], dst_ref=half(block, mc), send_sem=send_sems.at[base + kk],
            recv_sem=recv_sems.at[base + kk], device_id=(px, py, mc), device_id_type=MESH)

    def to_sibling(kk, hc):
        px, py = chips[kk]
        blk = half(2 * px + py, hc)
        return pltpu.make_async_remote_copy(
            src_ref=blk, dst_ref=blk, send_sem=send_sems.at[base + 3 + kk], recv_sem=recv_sems.at[base + 3 + kk],
            device_id=sibling, device_id_type=MESH)

    mine = pltpu.make_async_copy(x_ref, out_ref.at[j], local_sems.at[slot])
    if phase == 0:
        mine.start()
        for kk in range(3):
            over_ici(kk, j).start()
    elif phase == 1:
        for kk, (px, py) in enumerate(chips):
            over_ici(kk, 2 * px + py).wait_recv()
            to_sibling(kk, mc).start()
    else:
        for kk in range(3):
            to_sibling(kk, 1 - mc).wait_recv()
        for kk in range(3):
            over_ici(kk, j).wait_send()
            to_sibling(kk, mc).wait_send()
        mine.wait()


RS_ROWS = 32


def _reduce_scatter_chips(g, name):
    _, r, c = g.shape
    rh = r // 2
    steps = rh // RS_ROWS

    def body(g_ref, out_ref, sib_ref, part_ref, got_ref, send_sems, recv_sems):
        mx, my, mc = _place()
        j = 2 * mx + my
        sibling = (mx, my, 1 - mc)
        chips = [(1 - mx, my), (mx, 1 - my), (1 - mx, 1 - my)]

        to_sib = pltpu.make_async_remote_copy(
            src_ref=g_ref.at[:, pl.ds((1 - mc) * rh, rh), :], dst_ref=sib_ref,
            send_sem=send_sems.at[0], recv_sem=recv_sems.at[0], device_id=sibling, device_id_type=MESH)
        to_sib.start()
        to_sib.wait()

        def add_sibling(step, carry):
            rows = pl.ds(pl.multiple_of(step * RS_ROWS, RS_ROWS), RS_ROWS)
            mine = g_ref[:, pl.ds(pl.multiple_of(mc * rh + step * RS_ROWS, RS_ROWS), RS_ROWS), :]
            part_ref[:, rows, :] = mine.astype(F32) + sib_ref[:, rows, :].astype(F32)
            return carry

        lax.fori_loop(0, steps, add_sibling, 0)

        def to_bf16(step, carry):
            rows = pl.ds(pl.multiple_of(step * RS_ROWS, RS_ROWS), RS_ROWS)
            sib_ref[:, rows, :] = part_ref[:, rows, :].astype(BF16)
            return carry

        lax.fori_loop(0, steps, to_bf16, 0)

        sends = []
        for kk, (px, py) in enumerate(chips):
            cp = pltpu.make_async_remote_copy(
                src_ref=sib_ref.at[2 * px + py], dst_ref=got_ref.at[kk],
                send_sem=send_sems.at[1 + kk], recv_sem=recv_sems.at[1 + kk],
                device_id=(px, py, mc), device_id_type=MESH)
            cp.start()
            sends.append(cp)
        for cp in sends:
            cp.wait()

        def total(step, carry):
            rows = pl.ds(pl.multiple_of(step * RS_ROWS, RS_ROWS), RS_ROWS)
            acc = part_ref[j, rows, :]
            for kk in range(3):
                acc = acc + got_ref[kk, rows, :].astype(F32)
            out_ref[pl.ds(pl.multiple_of(mc * rh + step * RS_ROWS, RS_ROWS), RS_ROWS), :] = acc
            return carry

        lax.fori_loop(0, steps, total, 0)

        done = pltpu.make_async_remote_copy(
            src_ref=out_ref.at[pl.ds(mc * rh, rh), :], dst_ref=out_ref.at[pl.ds(mc * rh, rh), :],
            send_sem=send_sems.at[4], recv_sem=recv_sems.at[4], device_id=sibling, device_id_type=MESH)
        done.start()
        done.wait_send()
        pltpu.make_async_remote_copy(
            src_ref=out_ref.at[pl.ds((1 - mc) * rh, rh), :], dst_ref=out_ref.at[pl.ds((1 - mc) * rh, rh), :],
            send_sem=send_sems.at[4], recv_sem=recv_sems.at[4], device_id=sibling, device_id_type=MESH).wait_recv()

    return pl.pallas_call(
        body, name=name,
        out_shape=jax.ShapeDtypeStruct((r, c), F32),
        in_specs=[pl.BlockSpec(memory_space=pltpu.VMEM)],
        out_specs=pl.BlockSpec(memory_space=pltpu.VMEM),
        scratch_shapes=[pltpu.VMEM((4, rh, c), BF16), pltpu.VMEM((4, rh, c), F32), pltpu.VMEM((3, rh, c), BF16),
                        pltpu.SemaphoreType.DMA((5,)), pltpu.SemaphoreType.DMA((5,))],
        compiler_params=pltpu.CompilerParams(vmem_limit_bytes=VMEM_LIMIT),
    )(g)


def _sum8(x, name):
    _, r, c = x.shape

    def body(x_ref, o_ref):
        acc = x_ref[0]
        for d in range(1, 8):
            acc = acc + x_ref[d]
        o_ref[...] = acc

    return pl.pallas_call(
        body, name=name, out_shape=jax.ShapeDtypeStruct((r, c), F32),
        in_specs=[pl.BlockSpec(memory_space=pltpu.VMEM)], out_specs=pl.BlockSpec(memory_space=pltpu.VMEM),
    )(x)


SCATTER_COPIES = 7


def _scatter_sems(n):
    return [pltpu.SemaphoreType.DMA((SCATTER_COPIES * n,)), pltpu.SemaphoreType.DMA((SCATTER_COPIES * n,))]


def _scatter_phase(phase, g_ref, got_ref, send_sems, recv_sems, slot):
    mx, my, mc = _place()
    rh = g_ref.shape[1] // 2
    base = SCATTER_COPIES * slot
    for d in range(1, 8):
        px = 1 - mx if d & 4 else mx
        py = 1 - my if d & 2 else my
        pc = 1 - mc if d & 1 else mc
        cp = pltpu.make_async_remote_copy(
            src_ref=g_ref.at[2 * px + py, pl.ds(pc * rh, rh), :], dst_ref=got_ref.at[d - 1],
            send_sem=send_sems.at[base + d - 1], recv_sem=recv_sems.at[base + d - 1],
            device_id=(px, py, pc), device_id_type=MESH)
        if phase == 0:
            cp.start()
        else:
            cp.wait()


def _scatter_sum(g, got, name):
    mx, my, mc = _place()
    rh, c = got.shape[1], got.shape[2]
    mine = lax.dynamic_slice(g, (2 * mx + my, mc * rh, 0), (1, rh, c))[0]
    t = _pick(rh, 256, 16)

    def body(m_ref, got_ref, o_ref):
        acc = m_ref[...].astype(F32)
        for d in range(SCATTER_COPIES):
            acc = acc + got_ref[d].astype(F32)
        o_ref[...] = acc

    return pl.pallas_call(
        body, name=name, grid=(rh // t,),
        in_specs=[pl.BlockSpec((t, c), lambda i: (i, 0)), pl.BlockSpec((SCATTER_COPIES, t, c), lambda i: (0, i, 0))],
        out_specs=pl.BlockSpec((t, c), lambda i: (i, 0)),
        out_shape=jax.ShapeDtypeStruct((rh, c), F32), compiler_params=_params(("parallel",)),
    )(mine, got)


def _scatter_finish(halves, name):
    n = len(halves)

    def body(*refs):
        h_refs, o_refs = refs[:n], refs[n:2 * n]
        send_sems, recv_sems = refs[2 * n:]
        mx, my, mc = _place()
        copies = [pltpu.make_async_remote_copy(
            src_ref=h_refs[kk], dst_ref=o_refs[kk], send_sem=send_sems.at[kk], recv_sem=recv_sems.at[kk],
            device_id=(mx, my, 1 - mc), device_id_type=MESH) for kk in range(n)]
        for cp in copies:
            cp.start()
        for cp in copies:
            cp.wait()

    hbm = pl.BlockSpec(memory_space=pl.ANY)
    theirs = pl.pallas_call(
        body, name=name, in_specs=[hbm] * n, out_specs=[hbm] * n,
        out_shape=[jax.ShapeDtypeStruct(h.shape, F32) for h in halves],
        scratch_shapes=[pltpu.SemaphoreType.DMA((n,)), pltpu.SemaphoreType.DMA((n,))],
    )(*halves)
    south = lax.axis_index("c") == 0
    return [jnp.concatenate([jnp.where(south, m, t), jnp.where(south, t, m)], axis=0) for m, t in zip(halves, theirs)]


def _adamw(w, g, m, v, name):
    r, c = w.shape
    t = _pick(r, 256, SUBLANES)
    spec = pl.BlockSpec((t, c), lambda i: (i, 0))

    def body(w_ref, g_ref, m_ref, v_ref, d_ref, nm_ref, nv_ref):
        gv = g_ref[...]
        m_new = ADAM_B1 * m_ref[...] + (1.0 - ADAM_B1) * gv
        v_new = ADAM_B2 * v_ref[...] + (1.0 - ADAM_B2) * (gv * gv)
        m_hat = m_new / (1.0 - ADAM_B1 ** ADAM_STEP)
        v_hat = v_new / (1.0 - ADAM_B2 ** ADAM_STEP)
        d_ref[...] = -ADAM_LR * (m_hat / (jnp.sqrt(v_hat) + ADAM_EPS) + ADAM_WD * w_ref[...])
        nm_ref[...] = m_new
        nv_ref[...] = v_new

    return pl.pallas_call(
        body, name=name, grid=(r // t,), in_specs=[spec] * 4, out_specs=[spec] * 3,
        out_shape=[jax.ShapeDtypeStruct((r, c), F32)] * 3, compiler_params=_params(("parallel",)),
    )(w, g, m, v)


def _pack_rows(parts):
    rows, offs, o = [], [], 0
    for p in parts:
        f = p.reshape(-1)
        n = -(-f.shape[0] // (LANES * SUBLANES)) * SUBLANES
        rows.append(jnp.pad(f, (0, n * LANES - f.shape[0])).reshape(n, LANES))
        offs.append((o, n))
        o += n
    return jnp.concatenate(rows, 0), offs


def _unpack_rows(packed, offs, shapes):
    out = []
    for (o, n), shp in zip(offs, shapes):
        size = 1
        for d in shp:
            size *= d
        out.append(packed[o:o + n].reshape(-1)[:size].reshape(shp))
    return out


def _mm_hosting(a, b, mode, out_dtype, name, gather=(), chips=None, scatter=()):
    res = _mm(a, b, mode, out_dtype, name, gather=gather, chips=chips, scatter=scatter)
    return (res[0], list(res[1:])) if (gather or scatter) else (res, [])


def _ffn_fwd(x, s, sh, g, w_in, w_out, tag, gather_in=(), gather_out=()):
    (h,) = _rowcall(lambda r, p: ([_modulate(r[0], p[0], p[1])], []), [x], [s, sh], [(x.shape[1], BF16)], [],
                    tile=ROW_TILE, name=tag + "_mod")
    gu, got = _mm_hosting(h, w_in, "nn", BF16, tag + "_in", gather_in, chips="b")
    if w_out is None:
        first = got.pop(0)
        w_out = first.reshape(4 * first.shape[1], first.shape[2])
    (act,) = _rowcall(lambda r, p: ([_silu(r[0].astype(F32)) * r[1].astype(F32)], []),
                      [(gu, D_FF, 0), (gu, D_FF, 1)], [], [(D_FF, BF16)], [], tile=ROW_TILE_WIDE, name=tag + "_act")
    f, got_out = _mm_hosting(act, w_out, "nn", F32, tag + "_out", gather_out)
    got = got + got_out
    (y,) = _rowcall(lambda r, p: ([r[0] + 0.5 * p[0] * r[1]], []), [x, f], [g], [(x.shape[1], F32)], [],
                    tile=ROW_TILE, name=tag + "_res")
    return y, (x, h, gu, act, f), got, w_out


def _ffn_bwd(dy, saved, s, sh, g, w_in, w_out, tag, scatter_bin=(), scatter_bwin=()):
    x, h, gu, act, f = saved
    d = x.shape[1]
    df, dg = _rowcall(lambda r, p: ([0.5 * p[0] * r[0]], [0.5 * jnp.sum(r[0] * r[1], 0, keepdims=True)]),
                      [dy, f], [g], [(d, BF16)], [(1, d)], tile=ROW_TILE, name=tag + "_bres")
    da = _mm(df, w_out, "nt", BF16, tag + "_bout")
    dw_out = _mm(act, df, "tn", BF16, tag + "_bwout")

    def act_bwd(r, p):
        gate, up, dav = r[0].astype(F32), r[1].astype(F32), r[2].astype(F32)
        _, vjp = jax.vjp(lambda a, b: _silu(a) * b, gate, up)
        dgate, dup = vjp(dav)
        return [jnp.concatenate([dgate, dup], 1)], []

    (dgu,) = _rowcall(act_bwd, [(gu, D_FF, 0), (gu, D_FF, 1), da], [], [(2 * D_FF, BF16)], [], tile=ROW_TILE_WIDE,
                      name=tag + "_bact")
    dh, got_a = _mm_hosting(dgu, w_in, "nt", F32, tag + "_bin", chips="b", scatter=scatter_bin)
    dw_in, got_b = _mm_hosting(h, dgu, "tn", BF16, tag + "_bwin", chips="out", scatter=scatter_bwin)

    def mod_bwd(r, p):
        _, vjp = jax.vjp(_modulate, r[0], p[0], p[1])
        dx, ds, dsh = vjp(r[1])
        return [r[2] + dx], [ds, dsh]

    dx, ds, dsh = _rowcall(mod_bwd, [x, dh, dy], [s, sh], [(d, F32)], [(1, d), (1, d)], tile=ROW_TILE, name=tag + "_bmod")
    return dx, (dsh, ds, dg), dw_in, dw_out, list(got_a) + list(got_b)


def _mixer_fwd(x, s, sh, g, wts, rope, gather=()):
    w_in_p, conv8, a_log, dt_bias, wn, wq, w_uq_p, wkv, w_ukv, wqn, wqr, wkn, wkr, won, w_out = wts
    cos2, sin2 = rope
    d = x.shape[1]
    (h,) = _rowcall(lambda r, p: ([_modulate(r[0], p[0], p[1])], []), [x], [s, sh], [(d, BF16)], [],
                    tile=ROW_TILE, name="mix_mod")
    proj = _mm(h, w_in_p, "nn", F32, "mix_in")
    qkv_c = _conv_fwd(proj, conv8, "mix_conv")
    gab = (proj, LANES, 23)

    q, k, v, gb = _rowcall(
        lambda r, p: (list(_gdn_prep_core(_split(r[0], [HEAD] * 12), r[1], p[0], p[1])), []),
        [qkv_c, gab], [a_log, dt_bias], [(512, F32)] * 3 + [(LANES, F32)], [], tile=ROW_TILE_WIDE, name="mix_gdn_prep")
    gdn_local = _gdn_local_fwd(q, k, v, gb, "mix_gdn_local")
    o_gdn, gdn_states = _gdn_scan_fwd(*gdn_local, "mix_gdn_scan")
    states = (gdn_local, gdn_states)

    cq, ckv, kr = (proj, 512, 4), (proj, 256, 10), (proj, LANES, 22)
    cqn, ckvn, k_rope = _rowcall(
        lambda r, p: (list(_mla_prep_core(r[0][:, :MLA_Q_LORA], r[1], r[2], r[3], r[4], p[0], p[1], p[2])), []),
        [cq, ckv, kr, cos2, sin2], [wq, wkv, wkr], [(MLA_Q_LORA, BF16), (MLA_KV_LORA, BF16), (LANES, F32)], [],
        tile=ROW_TILE, name="mix_mla_prep")
    qf = _mm(cqn, w_uq_p, "nn", F32, "mix_uq")
    kvf = _mm(ckvn, w_ukv, "nn", F32, "mix_ukv")

    def qk_prep(r, p):
        qparts = _split(r[0], [HEAD] * 8)
        kvparts = _split(r[1], [HEAD] * 8)
        qs, ks, vs = _qk_prep_core(qparts[:4], qparts[4:], kvparts[0::2], kvparts[1::2], r[2], r[3], r[4],
                                   p[0], p[1], p[2])
        return [jnp.concatenate(qs, 1), jnp.concatenate(ks, 1), jnp.concatenate(vs, 1)], []

    qa, ka, va = _rowcall(qk_prep, [qf, kvf, k_rope, cos2, sin2], [wqn, wqr, wkn],
                          [(4 * QK_PAD, BF16), (4 * QK_PAD, BF16), (4 * HEAD, BF16)], [], tile=ROW_TILE_WIDE,
                          name="mix_qk_prep")
    o_b, lse, got = _attn_fwd(qa, ka, va, "mix_attn", gather=gather)
    if w_out is None:
        first = got.pop(0)
        w_out = first.reshape(4 * first.shape[1], first.shape[2])

    gz = (proj, 512, 3)
    (mixed,) = _rowcall(
        lambda r, p: ([_mix_post_core(_split(r[0], HW4), _split(r[1], HW4), _split(r[2], HW4), p[0], p[1])], []),
        [o_gdn, gz, o_b], [wn, won], [(2 * 512, BF16)], [], tile=ROW_TILE, name="mix_post")
    y = _mm(mixed, w_out, "nn", F32, "mix_out")
    (x_out,) = _rowcall(lambda r, p: ([r[0] + p[0] * r[1]], []), [x, y], [g], [(d, F32)], [], tile=ROW_TILE,
                        name="mix_res")
    saved = (x, h, proj, qkv_c, q, k, v, gb, states, o_gdn, cqn, ckvn, k_rope, qf, kvf, qa, ka, va, o_b, lse,
             mixed, y)
    return x_out, saved, got, w_out


def _mixer_bwd(dy, saved, s, sh, g, wts, rope, scatter=()):
    w_in_p, conv8, a_log, dt_bias, wn, wq, w_uq_p, wkv, w_ukv, wqn, wqr, wkn, wkr, won, w_out = wts
    cos2, sin2 = rope
    (x, h, proj, qkv_c, q, k, v, gb, states, o_gdn, cqn, ckvn, k_rope, qf, kvf, qa, ka, va, o_b, lse,
     mixed, y) = saved
    d = x.shape[1]
    dyb, dg = _rowcall(lambda r, p: ([p[0] * r[0]], [jnp.sum(r[0] * r[1], 0, keepdims=True)]),
                       [dy, y], [g], [(d, BF16)], [(1, d)], tile=ROW_TILE, name="mix_bres")
    dmixed = _mm(dyb, w_out, "nt", F32, "mix_bout")
    dw_out = _mm(mixed, dyb, "tn", BF16, "mix_bwout")

    gz = (proj, 512, 3)

    def post_bwd(r, p):
        _, vjp = jax.vjp(_mix_post_core, _split(r[0], HW4), _split(r[1], HW4), _split(r[2], HW4), p[0], p[1])
        do, dz, dob, dwn, dwon = vjp(r[3])
        return [jnp.concatenate(do, 1), jnp.concatenate(dz, 1), jnp.concatenate(dob, 1)], [dwn, dwon]

    do_gdn, dgz, do_b, dwn, dwon = _rowcall(post_bwd, [o_gdn, gz, o_b, dmixed], [wn, won], [(512, F32)] * 3,
                                            [(1, HEAD), (1, HEAD)], tile=ROW_TILE_WIDE, name="mix_bpost")

    dqa, dka, dva, got = _attn_bwd(qa, ka, va, o_b, do_b, lse, "mix_battn", scatter=scatter)

    def qk_bwd(r, p):
        qparts = _split(r[0], [HEAD] * 8)
        kvparts = _split(r[1], [HEAD] * 8)
        _, vjp = jax.vjp(_qk_prep_core, qparts[:4], qparts[4:], kvparts[0::2], kvparts[1::2], r[2], r[3], r[4],
                         p[0], p[1], p[2])
        cot = (_split(r[5], [QK_PAD] * 4), _split(r[6], [QK_PAD] * 4), _split(r[7], HW4))
        dqn, dqr, dkn, dvp, dkrope, _, _, dwqn, dwqr, dwkn = vjp(cot)
        dkv = []
        for a, b in zip(dkn, dvp):
            dkv += [a, b]
        return [jnp.concatenate(list(dqn) + list(dqr), 1), jnp.concatenate(dkv, 1), dkrope], [dwqn, dwqr, dwkn]

    dqf, dkvf, dk_rope, dwqn, dwqr, dwkn = _rowcall(
        qk_bwd, [qf, kvf, k_rope, cos2, sin2, dqa, dka, dva], [wqn, wqr, wkn],
        [(8 * HEAD, BF16), (8 * HEAD, BF16), (LANES, F32)], [(1, HEAD)] * 3, tile=ROW_TILE_WIDE, name="mix_bqk_prep")
    dcqn = _mm(dqf, w_uq_p, "nt", F32, "mix_buq")
    dw_uq_p = _mm(cqn, dqf, "tn", F32, "mix_bwuq")
    dckvn = _mm(dkvf, w_ukv, "nt", F32, "mix_bukv")
    dw_ukv = _mm(ckvn, dkvf, "tn", F32, "mix_bwukv")

    cq, ckv, kr = (proj, 512, 4), (proj, 256, 10), (proj, LANES, 22)

    def mla_bwd(r, p):
        _, vjp = jax.vjp(_mla_prep_core, r[0][:, :MLA_Q_LORA], r[1], r[2], r[3], r[4], p[0], p[1], p[2])
        dcq, dckv, dkr, _, _, dwq, dwkv, dwkr = vjp((r[5], r[6], r[7]))
        pad = jnp.zeros((dcq.shape[0], 512 - MLA_Q_LORA), F32)
        return [jnp.concatenate([dcq, pad], 1), dckv, dkr], [dwq, dwkv, dwkr]

    dcq, dckv, dkr, dwq, dwkv, dwkr = _rowcall(
        mla_bwd, [cq, ckv, kr, cos2, sin2, dcqn, dckvn, dk_rope], [wq, wkv, wkr],
        [(512, F32), (MLA_KV_LORA, F32), (LANES, F32)], [(1, MLA_Q_LORA), (1, MLA_KV_LORA), (1, LANES)],
        tile=ROW_TILE, name="mix_bmla_prep")

    gdn_local, gdn_states = states
    d_local = _gdn_scan_bwd(*gdn_local, gdn_states, do_gdn, "mix_bgdn_scan")
    dq, dk, dv, dgb = _gdn_local_bwd(q, k, v, gb, *d_local, "mix_bgdn_local")
    gab = (proj, LANES, 23)

    def gdn_prep_bwd(r, p):
        _, vjp = jax.vjp(_gdn_prep_core, _split(r[0], [HEAD] * 12), r[1], p[0], p[1])
        dparts, dgab, da_log, ddt = vjp((r[2], r[3], r[4], r[5]))
        return [jnp.concatenate(dparts, 1), dgab], [da_log, ddt]

    dqkv_c, dgab, da_log, ddt = _rowcall(gdn_prep_bwd, [qkv_c, gab, dq, dk, dv, dgb], [a_log, dt_bias],
                                         [(1536, F32), (LANES, F32)], [(1, LANES), (1, LANES)], tile=ROW_TILE_WIDE,
                                         name="mix_bgdn_prep")
    dqkv_pre, dconv8 = _conv_bwd(proj, dqkv_c, conv8, "mix_bconv")

    dproj = jnp.concatenate([dqkv_pre.astype(BF16), dgz.astype(BF16), dcq.astype(BF16), dckv.astype(BF16),
                             dkr.astype(BF16), dgab.astype(BF16)], axis=1)
    dh = _mm(dproj, w_in_p, "nt", F32, "mix_bin")
    dw_in_p = _mm(h, dproj, "tn", F32, "mix_bwin")

    def mod_bwd(r, p):
        _, vjp = jax.vjp(_modulate, r[0], p[0], p[1])
        dx, ds, dsh = vjp(r[1])
        return [r[2] + dx], [ds, dsh]

    dx, ds, dsh = _rowcall(mod_bwd, [x, dh, dy], [s, sh], [(d, F32)], [(1, d), (1, d)], tile=ROW_TILE, name="mix_bmod")
    small = dict(conv=dconv8, a_log=da_log, dt=ddt, wn=dwn, wq=dwq, wkv=dwkv, wqn=dwqn, wqr=dwqr, wkn=dwkn,
                 wkr=dwkr, won=dwon)
    return dx, (dsh, ds, dg), dw_in_p, dw_uq_p, dw_ukv, dw_out, small, got


def _pad_cols(a, n):
    return jnp.pad(a, ((0, 0),) * (a.ndim - 1) + ((0, n - a.shape[-1]),))


def _pack_w_in(w):
    z = lambda n: jnp.zeros((w.shape[0], n), w.dtype)
    return jnp.concatenate([w[:, 0:2048], w[:, 2056:2440], z(128), w[:, 2440:2696], w[:, 2696:2760], z(64),
                            w[:, 2048:2056], z(120)], axis=1)


def _unpack_w_in(wp):
    return jnp.concatenate([wp[:, 0:2048], wp[:, 2944:2952], wp[:, 2048:2432], wp[:, 2560:2816], wp[:, 2816:2880]],
                           axis=1)


def _pack_w_uq(w):
    z = jnp.zeros((w.shape[0], LANES - MLA_ROPE), w.dtype)
    nope = [w[:, h * 192:h * 192 + HEAD] for h in range(MLA_HEADS)]
    rope = []
    for h in range(MLA_HEADS):
        rope += [w[:, h * 192 + HEAD:(h + 1) * 192], z]
    return jnp.concatenate(nope + rope, axis=1)


def _unpack_w_uq(wp):
    cols = []
    for h in range(MLA_HEADS):
        cols += [wp[:, h * HEAD:(h + 1) * HEAD], wp[:, 512 + h * LANES:512 + h * LANES + MLA_ROPE]]
    return jnp.concatenate(cols, axis=1)


def _cols_to_chips(a):
    r, c = a.shape
    return a.reshape(r, 4, c // 4).transpose(1, 0, 2)


def _chips_to_cols(a):
    _, r, n = a.shape
    return a.transpose(1, 0, 2).reshape(r, 4 * n)


def _pad128(v, n=LANES):
    return _pad_cols(v.reshape(1, -1), n)


def kernel(x, c, positions, w_ada, b_ada, ffn1_w_in, ffn1_w_out, w_in, gdn_conv_w, gdn_a_log, gdn_dt_bias, gdn_norm_w, mla_q_norm_w, mla_w_uq, mla_kv_norm_w, mla_w_ukv, qkn_q_nope, qkn_q_rope, qkn_k_nope, qkn_k_rope, mla_out_norm_w, w_out, ffn2_w_in, ffn2_w_out, loss_target, m_w_ada, m_b_ada, m_ffn1_w_in, m_ffn1_w_out, m_w_in, m_gdn_conv_w, m_gdn_a_log, m_gdn_dt_bias, m_gdn_norm_w, m_mla_q_norm_w, m_mla_w_uq, m_mla_kv_norm_w, m_mla_w_ukv, m_qkn_q_nope, m_qkn_q_rope, m_qkn_k_nope, m_qkn_k_rope, m_mla_out_norm_w, m_w_out, m_ffn2_w_in, m_ffn2_w_out, v_w_ada, v_b_ada, v_ffn1_w_in, v_ffn1_w_out, v_w_in, v_gdn_conv_w, v_gdn_a_log, v_gdn_dt_bias, v_gdn_norm_w, v_mla_q_norm_w, v_mla_w_uq, v_mla_kv_norm_w, v_mla_w_ukv, v_qkn_q_nope, v_qkn_q_rope, v_qkn_k_nope, v_qkn_k_rope, v_mla_out_norm_w, v_w_out, v_ffn2_w_in, v_ffn2_w_out):
    weights = dict(w_ada=w_ada, b_ada=b_ada, ffn1_w_in=ffn1_w_in, ffn1_w_out=ffn1_w_out, w_in=w_in,
                   gdn_conv_w=gdn_conv_w, gdn_a_log=gdn_a_log, gdn_dt_bias=gdn_dt_bias, gdn_norm_w=gdn_norm_w,
                   mla_q_norm_w=mla_q_norm_w, mla_w_uq=mla_w_uq, mla_kv_norm_w=mla_kv_norm_w, mla_w_ukv=mla_w_ukv,
                   qkn_q_nope=qkn_q_nope, qkn_q_rope=qkn_q_rope, qkn_k_nope=qkn_k_nope, qkn_k_rope=qkn_k_rope,
                   mla_out_norm_w=mla_out_norm_w, w_out=w_out, ffn2_w_in=ffn2_w_in, ffn2_w_out=ffn2_w_out)
    moms_m = dict(w_ada=m_w_ada, b_ada=m_b_ada, ffn1_w_in=m_ffn1_w_in, ffn1_w_out=m_ffn1_w_out, w_in=m_w_in,
                  gdn_conv_w=m_gdn_conv_w, gdn_a_log=m_gdn_a_log, gdn_dt_bias=m_gdn_dt_bias,
                  gdn_norm_w=m_gdn_norm_w, mla_q_norm_w=m_mla_q_norm_w, mla_w_uq=m_mla_w_uq,
                  mla_kv_norm_w=m_mla_kv_norm_w, mla_w_ukv=m_mla_w_ukv, qkn_q_nope=m_qkn_q_nope,
                  qkn_q_rope=m_qkn_q_rope, qkn_k_nope=m_qkn_k_nope, qkn_k_rope=m_qkn_k_rope,
                  mla_out_norm_w=m_mla_out_norm_w, w_out=m_w_out, ffn2_w_in=m_ffn2_w_in, ffn2_w_out=m_ffn2_w_out)
    moms_v = dict(w_ada=v_w_ada, b_ada=v_b_ada, ffn1_w_in=v_ffn1_w_in, ffn1_w_out=v_ffn1_w_out, w_in=v_w_in,
                  gdn_conv_w=v_gdn_conv_w, gdn_a_log=v_gdn_a_log, gdn_dt_bias=v_gdn_dt_bias,
                  gdn_norm_w=v_gdn_norm_w, mla_q_norm_w=v_mla_q_norm_w, mla_w_uq=v_mla_w_uq,
                  mla_kv_norm_w=v_mla_kv_norm_w, mla_w_ukv=v_mla_w_ukv, qkn_q_nope=v_qkn_q_nope,
                  qkn_q_rope=v_qkn_q_rope, qkn_k_nope=v_qkn_k_nope, qkn_k_rope=v_qkn_k_rope,
                  mla_out_norm_w=v_mla_out_norm_w, w_out=v_w_out, ffn2_w_in=v_ffn2_w_in, ffn2_w_out=v_ffn2_w_out)
    names = list(weights)

    seq, d = x.shape[1], x.shape[2]
    x2d = x.reshape(seq, d)
    tgt = loss_target.reshape(seq, d)
    mx, my, mc = _place()
    chip = 2 * mx + my
    me = 2 * chip + mc
    n_mod = b_ada.shape[1] // d
    shard = w_ada.shape[2]

    half = MLA_ROPE // 2
    inv_freq = 10000.0 ** (-jnp.arange(half, dtype=F32) / half)
    ang = positions.astype(F32).reshape(seq, 1) * inv_freq
    cosv, sinv = jnp.cos(ang), jnp.sin(ang)
    cos2 = _pad_cols(jnp.concatenate([cosv, cosv], 1), LANES)
    sin2 = _pad_cols(jnp.concatenate([-sinv, sinv], 1), LANES)
    rope = (cos2, sin2)

    c_all = _allgather8(jnp.pad(c, ((0, SUBLANES - 1), (0, 0))), "gather_c")[:, 0, :]
    (sc_all,) = _rowcall(lambda r, p: ([_silu(r[0])], []), [c_all], [], [(d, F32)], [], tile=8, name="ada_silu")
    mod_part = _mm(sc_all, w_ada[0], "nn", F32, "ada_mm", hi=True)
    mod_all = _allgather8(mod_part, "gather_mod")
    mod_rows = lax.dynamic_index_in_dim(mod_all, me, axis=1, keepdims=False)
    mod_raw = jnp.concatenate([mod_rows[2 * jj] for jj in range(4)], axis=0).reshape(1, 4 * shard)
    (mod,) = _rowcall(lambda r, p: ([r[0] + r[1]], []),
                      [jnp.pad(mod_raw, ((0, 7), (0, 0))), jnp.pad(b_ada, ((0, 7), (0, 0)))], [],
                      [(4 * shard, F32)], [], tile=8, name="ada_bias")
    mods = [mod[0:1, i * d:(i + 1) * d] for i in range(n_mod)]
    sh1, s1, g1, sh2, s2, g2, sh3, s3, g3 = mods

    def shard_bf16(w, pad_to=None):
        w2 = w[0].astype(BF16)
        return _pad_cols(w2, pad_to) if pad_to else w2

    def cols_of(got, w):
        return _chips_to_cols(got[:, :, :w.shape[2]])

    def rows_of(got):
        return got.reshape(4 * got.shape[1], got.shape[2])

    f1_in = _allgather_chips(shard_bf16(ffn1_w_in), "gather_f1_in")
    conv_all = _allgather8(jnp.pad(gdn_conv_w[0], ((0, SUBLANES - CONV_K), (0, 0))), "gather_conv")
    conv8 = jnp.concatenate([conv_all[2 * jj] for jj in range(4)], axis=1)

    x1, sv1, got, f1_out = _ffn_fwd(
        x2d, s1, sh1, g1, f1_in, None, "ffn1",
        gather_in=[shard_bf16(ffn1_w_out), shard_bf16(w_in, 768), shard_bf16(mla_w_uq, 256), shard_bf16(mla_w_ukv)])
    w_in_full, w_uq_full, w_ukv_full = cols_of(got[0], w_in), cols_of(got[1], mla_w_uq), cols_of(got[2], mla_w_ukv)
    wts = (_pack_w_in(w_in_full), conv8, _pad128(gdn_a_log), _pad128(gdn_dt_bias), gdn_norm_w,
           mla_q_norm_w, _pack_w_uq(w_uq_full), mla_kv_norm_w, w_ukv_full, qkn_q_nope, _pad128(qkn_q_rope),
           qkn_k_nope, _pad128(qkn_k_rope), mla_out_norm_w, None)
    xm, svm, got, w_out_full = _mixer_fwd(
        x1, s2, sh2, g2, wts, rope, gather=[shard_bf16(w_out), shard_bf16(ffn2_w_in), shard_bf16(ffn2_w_out)])
    wts = wts[:-1] + (w_out_full,)
    f2_in, f2_out = got[0], rows_of(got[1])
    x3, sv3, _, _ = _ffn_fwd(xm, s3, sh3, g3, f2_in, f2_out, "ffn2")

    def loss_fn(r, p):
        err = r[0] - r[1]
        part = 0.5 * jnp.sum(jnp.sum(err * err, axis=1, keepdims=True) * (1.0 / d), axis=0, keepdims=True)
        return [err * (1.0 / d)], [jnp.broadcast_to(part, (1, LANES))]

    dy, loss_part = _rowcall(loss_fn, [x3, tgt], [], [(d, F32)], [(1, LANES)], tile=ROW_TILE, name="loss")
    loss = lax.psum(loss_part[0, 0], ("x", "y", "c"))

    def chip_cols(dw, pad_to=None):
        g4 = _cols_to_chips(dw).astype(BF16)
        return _pad_cols(g4, pad_to) if pad_to else g4

    def chip_rows(dw):
        return dw.astype(BF16).reshape(4, dw.shape[0] // 4, dw.shape[1])

    dxm, dmod3, dw_f2_in, dw_f2_out, _ = _ffn_bwd(dy, sv3, s3, sh3, g3, f2_in, f2_out, "ffn2")
    parts2 = [dw_f2_in, chip_rows(dw_f2_out)]
    dx1, dmod2, dw_in_p, dw_uq_p, dw_ukv, dw_out_m, small, got2 = _mixer_bwd(dxm, svm, s2, sh2, g2, wts, rope,
                                                                             scatter=parts2)
    parts_m = [chip_cols(_unpack_w_in(dw_in_p), 768), chip_cols(_unpack_w_uq(dw_uq_p), 256), chip_cols(dw_ukv),
               chip_rows(dw_out_m)]
    dx0, dmod1, dw_f1_in, dw_f1_out, got_m = _ffn_bwd(dx1, sv1, s1, sh1, g1, f1_in, f1_out, "ffn1",
                                                      scatter_bin=parts_m[:1], scatter_bwin=parts_m[1:])
    grad_x = dx0.reshape(x.shape)

    dmod = jnp.concatenate(list(dmod1) + list(dmod2) + list(dmod3), axis=1)
    small_parts = [dmod, small["conv"][:CONV_K], small["a_log"], small["dt"], small["wn"], small["wq"],
                   small["wkv"], small["wqn"], small["wqr"], small["wkn"], small["wkr"], small["won"]]
    packed, offs = _pack_rows(small_parts)
    gathered = _allgather8(packed, "gather_small")
    total = _sum8(gathered, "sum_small")
    (g_b_ada, g_conv, g_a_log, g_dt, g_wn, g_wq, g_wkv, g_wqn, g_wqr, g_wkn, g_wkr, g_won) = _unpack_rows(
        total, offs, [p.shape for p in small_parts])
    dmod_all = _unpack_rows(gathered.reshape(-1, LANES),
                            [(dd * packed.shape[0] + offs[0][0], offs[0][1]) for dd in range(8)],
                            [dmod.shape] * 8)
    dmod_all = jnp.concatenate(dmod_all, axis=0)
    dmod_mine = lax.dynamic_slice_in_dim(dmod_all, chip * shard, shard, axis=1)

    def ada_grad(r, p):
        acc = jnp.zeros((r[0].shape[0], shard), F32)
        for b in range(8):
            acc = acc + r[0][:, b:b + 1] * p[0][b:b + 1, :]
        return [acc], []

    (g_w_ada,) = _rowcall(ada_grad, [_pad_cols(sc_all.T, LANES)], [dmod_mine], [(shard, F32)], [], tile=ROW_TILE_WIDE,
                          name="ada_grad")

    grads = dict(
        w_ada=g_w_ada[None], b_ada=g_b_ada,
        gdn_conv_w=lax.dynamic_slice_in_dim(g_conv, chip * gdn_conv_w.shape[2], gdn_conv_w.shape[2], axis=1)[None],
        gdn_a_log=g_a_log[:, :GDN_HEADS], gdn_dt_bias=g_dt[:, :GDN_HEADS], gdn_norm_w=g_wn, mla_q_norm_w=g_wq,
        mla_kv_norm_w=g_wkv, qkn_q_nope=g_wqn, qkn_q_rope=g_wqr[:, :MLA_ROPE], qkn_k_nope=g_wkn,
        qkn_k_rope=g_wkr[:, :MLA_ROPE], mla_out_norm_w=g_won)

    hosted = ["ffn2_w_in", "ffn2_w_out", "w_in", "mla_w_uq", "mla_w_ukv", "w_out"]
    halves = [_scatter_sum(part, got, "rs_sum_" + nme)
              for nme, part, got in zip(hosted, parts2 + parts_m, got2 + got_m)]
    for nme, full in zip(hosted, _scatter_finish(halves, "rs_finish")):
        grads[nme] = full[:, :weights[nme].shape[2]][None]
    grads["ffn1_w_in"] = _reduce_scatter_chips(dw_f1_in, "rs_f1_in")[None]
    grads["ffn1_w_out"] = _reduce_scatter_chips(chip_rows(dw_f1_out), "rs_f1_out")[None]

    big = ["w_ada", "ffn1_w_in", "ffn1_w_out", "w_in", "mla_w_uq", "mla_w_ukv", "w_out", "ffn2_w_in", "ffn2_w_out"]
    delta, new_m, new_v = {}, {}, {}
    for nme in big:
        shp = weights[nme].shape
        dl, nm, nv = _adamw(weights[nme][0], grads[nme][0], moms_m[nme][0], moms_v[nme][0], "adamw_" + nme)
        delta[nme], new_m[nme], new_v[nme] = dl.reshape(shp), nm.reshape(shp), nv.reshape(shp)
    tiny = [nme for nme in names if nme not in big]
    shapes = [weights[nme].shape for nme in tiny]
    pw, poffs = _pack_rows([weights[nme] for nme in tiny])
    pg, _ = _pack_rows([grads[nme] for nme in tiny])
    pm, _ = _pack_rows([moms_m[nme] for nme in tiny])
    pv, _ = _pack_rows([moms_v[nme] for nme in tiny])
    pd, pnm, pnv = _adamw(pw, pg, pm, pv, "adamw_small")
    for nme, dl, nm, nv in zip(tiny, _unpack_rows(pd, poffs, shapes), _unpack_rows(pnm, poffs, shapes),
                               _unpack_rows(pnv, poffs, shapes)):
        delta[nme], new_m[nme], new_v[nme] = dl, nm, nv

    return (loss, grad_x, *[grads[nme].reshape(weights[nme].shape) for nme in names],
            *[delta[nme] for nme in names], *[new_m[nme] for nme in names], *[new_v[nme] for nme in names])
```

```python
import functools

import jax
import jax.numpy as jnp
from jax import lax
from jax.experimental import pallas as pl
from jax.experimental.pallas import tpu as pltpu

F32 = jnp.float32
BF16 = jnp.bfloat16
HI = lax.Precision.HIGHEST
MESH = pl.DeviceIdType.MESH

EPS = 1e-6
CHUNK = 64
D_FF = 2816
GDN_HEADS = 4
HEAD = 128
MLA_HEADS = 4
MLA_ROPE = 64
MLA_Q_LORA = 384
MLA_KV_LORA = 256
QK_PAD = 256
ATT_SCALE = (HEAD + MLA_ROPE) ** -0.5

ADAM_LR, ADAM_B1, ADAM_B2, ADAM_EPS, ADAM_WD, ADAM_STEP = 0.001, 0.9, 0.999, 1e-08, 0.01, 10

LANES = 128
SUBLANES = 8
VMEM_LIMIT = 56 * 2 ** 20
ROW_TILE = 1024
ROW_TILE_WIDE = 512


def _params(sem=None):
    return pltpu.CompilerParams(dimension_semantics=sem, vmem_limit_bytes=VMEM_LIMIT)


def _pick(n, cap, align):
    best = None
    d = align
    while d <= min(n, cap):
        if n % d == 0:
            best = d
        d += align
    return best if best is not None else n


def _iota(shape, dim):
    return lax.broadcasted_iota(jnp.int32, shape, dim)


def _rowcall(fn, rows, params, out_rows, out_accs, *, tile, name):
    rows = [r if isinstance(r, tuple) else (r, r.shape[1], 0) for r in rows]
    s = rows[0][0].shape[-2]
    t = min(tile, s)
    n = s // t
    n_in = len(rows) + len(params)
    n_row_out = len(out_rows)

    in_specs = []
    for r in rows:
        if len(r) == 3:
            in_specs.append(pl.BlockSpec((t, r[1]), functools.partial(lambda i, b: (i, b), b=r[2])))
        else:
            in_specs.append(pl.BlockSpec((None, t, r[1]), functools.partial(lambda i, b, h: (h, i, b), b=r[2], h=r[3])))
    in_specs += [pl.BlockSpec(p.shape, lambda i: (0, 0)) for p in params]
    out_shape, out_specs = [], []
    for o in out_rows:
        if len(o) == 2:
            out_shape.append(jax.ShapeDtypeStruct((s, o[0]), o[1]))
            out_specs.append(pl.BlockSpec((t, o[0]), lambda i: (i, 0)))
        else:
            out_shape.append(jax.ShapeDtypeStruct((o[2], s, o[0]), o[1]))
            out_specs.append(pl.BlockSpec((o[2], t, o[0]), lambda i: (0, i, 0)))
    out_shape += [jax.ShapeDtypeStruct(shape, F32) for shape in out_accs]
    out_specs += [pl.BlockSpec(shape, lambda i: (0, 0)) for shape in out_accs]

    def body(*refs):
        ins = refs[:n_in]
        outs = refs[n_in:]
        i = pl.program_id(0)
        vals = [r[...] for r in ins]
        row_outs, acc_outs = fn(vals[:len(rows)], vals[len(rows):])
        for r, v in zip(outs[:n_row_out], row_outs):
            if isinstance(v, (list, tuple)):
                for hh, piece in enumerate(v):
                    r[hh] = piece.astype(r.dtype)
            else:
                r[...] = v.astype(r.dtype)
        if out_accs:
            @pl.when(i == 0)
            def _():
                for r in outs[n_row_out:]:
                    r[...] = jnp.zeros(r.shape, F32)
            for r, v in zip(outs[n_row_out:], acc_outs):
                r[...] += v

    res = pl.pallas_call(
        body, name=name, grid=(n,), in_specs=in_specs, out_specs=out_specs, out_shape=out_shape,
        compiler_params=_params(("arbitrary",) if out_accs else ("parallel",)),
    )(*[r[0] for r in rows], *params)
    return list(res)


MM_TILE_MN = 1536
MM_VMEM_BUDGET = 44 * 2 ** 20


class _Hosted:
    def __init__(self, gather=(), scatter=()):
        self.gather, self.scatter = list(gather), list(scatter)
        self.operands = self.gather + self.scatter
        self.n = len(self.operands)

    def specs(self):
        return [pl.BlockSpec(memory_space=pl.ANY)] * self.n

    def out_shapes(self):
        return ([jax.ShapeDtypeStruct((4,) + x.shape, x.dtype) for x in self.gather]
                + [jax.ShapeDtypeStruct((SCATTER_COPIES, g.shape[1] // 2, g.shape[2]), g.dtype) for g in self.scatter])

    def scratch(self):
        return ((_gather_sems(len(self.gather)) if self.gather else [])
                + (_scatter_sems(len(self.scatter)) if self.scatter else []))

    def _phase(self, ph, ins, outs, sems):
        ng = len(self.gather)
        g_sems, s_sems = (sems[:3], sems[3:]) if ng else ((), sems)
        for slot in range(ng):
            _gather_phase(ph, ins[slot], outs[slot], *g_sems, slot)
        if ph != 1:
            for slot in range(len(self.scatter)):
                _scatter_phase(0 if ph == 0 else 1, ins[ng + slot], outs[ng + slot], *s_sems, slot)

    def open(self, step, steps, ins, outs, sems):
        if self.n:
            pl.when(step == 0)(lambda: self._phase(0, ins, outs, sems))
            pl.when(step == steps // 2)(lambda: self._phase(1, ins, outs, sems))

    def close(self, step, steps, ins, outs, sems):
        if self.n:
            pl.when(step == steps - 1)(lambda: self._phase(2, ins, outs, sems))


def _mm(a, b, mode, out_dtype, name, hi=False, gather=(), chips=None, scatter=()):
    b_shape = b.shape
    if chips == "b":
        b_shape = (b.shape[1], 4 * b.shape[2])
    if mode == "nn":
        (m, k), (_, n) = a.shape, b_shape
        dims = (((1,), (0,)), ((), ()))
    elif mode == "nt":
        (m, k), (n, _) = a.shape, b_shape
        dims = (((1,), (1,)), ((), ()))
    else:
        (k, m), (_, n) = a.shape, b_shape
        dims = (((0,), (0,)), ((), ()))
    tn = _pick(n // 4 if chips and mode != "nt" else n, MM_TILE_MN, LANES)
    tk = _pick(k // 4 if chips and mode == "nt" else k, 2048 if mode == "tn" else MM_TILE_MN, LANES)
    tm = _pick(m, MM_TILE_MN if mode == "tn" else 1024, LANES if mode == "tn" else 16)
    if mode != "tn" and m % (2 * tm) == 0:
        blocks = 2 * tm * (2 * tn * (2 + jnp.dtype(out_dtype).itemsize) + 2 * tk * a.dtype.itemsize)
        if blocks + 2 * tk * tn * b.dtype.itemsize <= MM_VMEM_BUDGET:
            tm = 2 * tm
    nk = k // tk
    nb = (n // 4) // tn
    kb = (k // 4) // tk
    if mode == "nn":
        a_spec = pl.BlockSpec((tm, tk), lambda i, j, kk: (i, kk))
        b_spec = pl.BlockSpec((tk, tn), lambda i, j, kk: (kk, j))
        if chips == "b":
            b_spec = pl.BlockSpec((None, tk, tn), lambda i, j, kk: (j // nb, kk, j % nb))
    elif mode == "nt":
        a_spec = pl.BlockSpec((tm, tk), lambda i, j, kk: (i, kk))
        b_spec = pl.BlockSpec((tn, tk), lambda i, j, kk: (j, kk))
        if chips == "b":
            b_spec = pl.BlockSpec((None, tn, tk), lambda i, j, kk: (kk // kb, j, kk % kb))
    else:
        a_spec = pl.BlockSpec((tk, tm), lambda i, j, kk: (kk, i))
        b_spec = pl.BlockSpec((tk, tn), lambda i, j, kk: (kk, j))
    out_spec = pl.BlockSpec((tm, tn), lambda i, j, kk: (i, j))
    out_shape = jax.ShapeDtypeStruct((m, n), out_dtype)
    if chips == "out":
        out_spec = pl.BlockSpec((None, tm, tn), lambda i, j, kk: (j // nb, i, j % nb))
        out_shape = jax.ShapeDtypeStruct((4, m, n // 4), out_dtype)

    host = _Hosted(gather, scatter)
    ng = host.n
    grid = (m // tm, n // tn, nk)
    steps = grid[0] * grid[1] * grid[2]

    def body(*refs):
        a_ref, b_ref = refs[:2]
        x_refs = refs[2:2 + ng]
        o_ref = refs[2 + ng]
        got_refs = refs[3 + ng:3 + 2 * ng]
        acc_ref = refs[3 + 2 * ng]
        sems = refs[4 + 2 * ng:]
        kk = pl.program_id(2)
        step = (pl.program_id(0) * grid[1] + pl.program_id(1)) * nk + kk
        host.open(step, steps, x_refs, got_refs, sems)

        @pl.when(kk == 0)
        def _():
            acc_ref[...] = jnp.zeros(acc_ref.shape, F32)

        av, bv = a_ref[...], b_ref[...]
        if hi:
            acc_ref[...] += lax.dot_general(av, bv, dims, precision=HI, preferred_element_type=F32)
        else:
            acc_ref[...] += lax.dot_general(av.astype(BF16), bv.astype(BF16), dims,
                                            preferred_element_type=F32)

        @pl.when(kk == nk - 1)
        def _():
            o_ref[...] = acc_ref[...].astype(o_ref.dtype)

        host.close(step, steps, x_refs, got_refs, sems)

    res = pl.pallas_call(
        body, name=name, grid=grid,
        in_specs=[a_spec, b_spec] + host.specs(),
        out_specs=[out_spec] + host.specs(),
        out_shape=[out_shape] + host.out_shapes(),
        scratch_shapes=[pltpu.VMEM((tm, tn), F32)] + host.scratch(),
        compiler_params=_params(("arbitrary",) * 3 if ng else ("parallel", "parallel", "arbitrary")),
    )(a, b, *host.operands)
    return res if ng else res[0]


def _rms(x, w=None, n=None):
    n = x.shape[-1] if n is None else n
    y = x * lax.rsqrt(jnp.sum(x * x, axis=-1, keepdims=True) * (1.0 / n) + EPS)
    return y if w is None else y * w


def _silu(x):
    return x * jax.nn.sigmoid(x)


def _softplus(x):
    return jnp.maximum(x, 0.0) + jnp.log1p(jnp.exp(-jnp.abs(x)))


def _split(x, widths):
    out, o = [], 0
    for w in widths:
        out.append(x[:, o:o + w])
        o += w
    return out


def _modulate(x, s, sh):
    return _rms(x) * (1.0 + s) + sh


def _rope_rot(x):
    r, c = _iota((LANES, LANES), 0), _iota((LANES, LANES), 1)
    half = MLA_ROPE // 2
    perm = (((r < half) & (c == r + half)) | ((r >= half) & (r < MLA_ROPE) & (c == r - half))).astype(F32)
    return jnp.dot(x, perm, precision=HI, preferred_element_type=F32)


def _rope(x, cos2, sin2):
    return x * cos2 + _rope_rot(x) * sin2


def _gdn_prep_core(qkv_parts, gab, a_log, dt_bias):
    act = [_silu(p) for p in qkv_parts]
    qs = [p * lax.rsqrt(jnp.sum(p * p, -1, keepdims=True) + EPS) * (HEAD ** -0.5) for p in act[:4]]
    ks = [p * lax.rsqrt(jnp.sum(p * p, -1, keepdims=True) + EPS) for p in act[4:8]]
    lane = _iota(gab.shape, 1)
    g = -jnp.exp(a_log) * _softplus(gab + dt_bias)
    beta = jax.nn.sigmoid(gab)
    gb = jnp.where(lane < GDN_HEADS, g, jnp.where(lane < 2 * GDN_HEADS, beta, 0.0))
    return (jnp.concatenate(qs, 1), jnp.concatenate(ks, 1), jnp.concatenate(act[8:], 1), gb)


def _mla_prep_core(cq, ckv, kr, cos2, sin2, wq, wkv, wkr):
    cqn = _rms(cq, wq)
    ckvn = _rms(ckv, wkv)
    k_rope = _rope(_rms(kr, wkr, MLA_ROPE), cos2, sin2)
    return cqn, ckvn, k_rope


def _qk_prep_core(qn_parts, qr_parts, kn_parts, v_parts, k_rope, cos2, sin2, wqn, wqr, wkn):
    qs, ks = [], []
    for h in range(MLA_HEADS):
        qn = _rms(qn_parts[h], wqn) * ATT_SCALE
        qr = _rope(_rms(qr_parts[h], wqr, MLA_ROPE), cos2, sin2) * ATT_SCALE
        qs.append(jnp.concatenate([qn, qr], 1))
        ks.append(jnp.concatenate([_rms(kn_parts[h], wkn), k_rope], 1))
    return qs, ks, list(v_parts)


def _mix_post_core(o_parts, gz_parts, ob_parts, wn, won):
    oa = [_rms(o, wn) * _silu(z) for o, z in zip(o_parts, gz_parts)]
    ob = [_rms(o, won) for o in ob_parts]
    return jnp.concatenate(oa + ob, 1)


CONV_K = 4
HALO = SUBLANES


def _conv_fwd(proj, w8, name):
    s = proj.shape[0]
    c = w8.shape[1]
    t = min(ROW_TILE_WIDE, s)
    n = s // t
    hb = t // HALO

    def body(x_ref, prev_ref, w_ref, o_ref, buf):
        i = pl.program_id(0)
        buf[pl.ds(0, HALO), :] = jnp.where(i > 0, prev_ref[...], 0.0)
        buf[pl.ds(HALO, t), :] = x_ref[...]
        acc = jnp.zeros((t, c), F32)
        for k in range(CONV_K):
            acc = acc + w_ref[k:k + 1, :] * buf[pl.ds(HALO - (CONV_K - 1) + k, t), :]
        o_ref[...] = acc

    return pl.pallas_call(
        body, name=name, grid=(n,),
        in_specs=[pl.BlockSpec((t, c), lambda i: (i, 0)),
                  pl.BlockSpec((HALO, c), lambda i: (jnp.maximum(i * hb - 1, 0), 0)),
                  pl.BlockSpec(w8.shape, lambda i: (0, 0))],
        out_specs=pl.BlockSpec((t, c), lambda i: (i, 0)),
        out_shape=jax.ShapeDtypeStruct((s, c), F32),
        scratch_shapes=[pltpu.VMEM((t + HALO, c), F32)],
        compiler_params=_params(("parallel",)),
    )(proj, proj, w8)


def _conv_bwd(proj, dy, w8, name):
    s = proj.shape[0]
    c = w8.shape[1]
    t = min(ROW_TILE_WIDE, s)
    n = s // t
    hb = t // HALO

    def body(x_ref, prev_ref, dy_ref, next_ref, w_ref, dx_ref, dw_ref, bufx, bufd):
        i = pl.program_id(0)
        bufx[pl.ds(0, HALO), :] = jnp.where(i > 0, prev_ref[...], 0.0)
        bufx[pl.ds(HALO, t), :] = x_ref[...]
        bufd[pl.ds(0, t), :] = dy_ref[...]
        bufd[pl.ds(t, HALO), :] = jnp.where(i < n - 1, next_ref[...], 0.0)

        @pl.when(i == 0)
        def _():
            dw_ref[...] = jnp.zeros(dw_ref.shape, F32)

        dyv = dy_ref[...]
        acc = jnp.zeros((t, c), F32)
        for k in range(CONV_K):
            acc = acc + w_ref[k:k + 1, :] * bufd[pl.ds(CONV_K - 1 - k, t), :]
            dw_ref[k:k + 1, :] += jnp.sum(dyv * bufx[pl.ds(HALO - (CONV_K - 1) + k, t), :], axis=0, keepdims=True)
        dx_ref[...] = acc

    return pl.pallas_call(
        body, name=name, grid=(n,),
        in_specs=[pl.BlockSpec((t, c), lambda i: (i, 0)),
                  pl.BlockSpec((HALO, c), lambda i: (jnp.maximum(i * hb - 1, 0), 0)),
                  pl.BlockSpec((t, c), lambda i: (i, 0)),
                  pl.BlockSpec((HALO, c), lambda i: (jnp.minimum((i + 1) * hb, s // HALO - 1), 0)),
                  pl.BlockSpec(w8.shape, lambda i: (0, 0))],
        out_specs=[pl.BlockSpec((t, c), lambda i: (i, 0)), pl.BlockSpec(w8.shape, lambda i: (0, 0))],
        out_shape=[jax.ShapeDtypeStruct((s, c), F32), jax.ShapeDtypeStruct(w8.shape, F32)],
        scratch_shapes=[pltpu.VMEM((t + HALO, c), F32), pltpu.VMEM((t + HALO, c), F32)],
        compiler_params=_params(("arbitrary",)),
    )(proj, proj, dy, dy, w8)


_B_NN = (((2,), (1,)), ((0,), (0,)))
_B_NT = (((2,), (2,)), ((0,), (0,)))
_B_TN = (((1,), (1,)), ((0,), (0,)))


def _dot3(a, b, dims):
    return lax.dot_general(a, b, dims, precision=lax.Precision.HIGH, preferred_element_type=F32)


def _bdot_hi(a, b):
    return _dot3(a, b, _B_NN)


class _Dots:
    nn = staticmethod(lambda a, b: _dot3(a, b, _B_NN))
    nt = staticmethod(lambda a, b: _dot3(a, b, _B_NT))
    tn = staticmethod(lambda a, b: _dot3(a, b, _B_TN))


def _unit_lower_inverse(a, dots):
    c = a.shape[-1]
    ri, ci = _iota(a.shape, 1), _iota(a.shape, 2)
    inner = (ri // 2) == (ci // 2)
    t = (ri == ci).astype(F32) - jnp.where(inner, a, 0.0)
    blk = 4
    while blk <= c:
        outer = (ri // blk) == (ci // blk)
        low = jnp.where(outer & jnp.logical_not(inner), a, 0.0)
        t = t - dots.nn(dots.nn(t, low), t)
        inner = outer
        blk *= 2
    return t


def _stack(xs):
    return jnp.concatenate([x[None] for x in xs], axis=0)


def _gdn_local(dots, q, k, v, gbs):
    b, c, _ = q.shape
    gcols, bcols = [], []
    for gb in gbs:
        lane = _iota(gb.shape, 1)
        for h in range(GDN_HEADS):
            gcols.append(jnp.sum(jnp.where(lane == h, gb, 0.0), axis=1, keepdims=True))
            bcols.append(jnp.sum(jnp.where(lane == GDN_HEADS + h, gb, 0.0), axis=1, keepdims=True))
    gcol, bcol = _stack(gcols), _stack(bcols)
    ri, ci = _iota((b, c, c), 1), _iota((b, c, c), 2)
    incl = ri >= ci
    tril = incl.astype(F32)
    g_cc = _bdot_hi(tril, jnp.broadcast_to(gcol, (b, c, c)))
    g_row = _bdot_hi(jnp.ones((b, c, c), F32), jnp.where(ri == ci, g_cc, 0.0))
    g_cl = _bdot_hi(tril, jnp.broadcast_to(gcol, (b, c, HEAD)))
    g_last = jnp.sum(jnp.broadcast_to(gcol, (b, c, HEAD)), axis=1, keepdims=True)
    decay = jnp.where(incl, jnp.exp(jnp.where(incl, g_cc - g_row, 0.0)), 0.0)
    kk = dots.nt(k, k)
    minv = _unit_lower_inverse(jnp.where(ri > ci, bcol * kk * decay, 0.0), dots)
    e_g = jnp.exp(g_cl)
    u = dots.nn(minv, v * bcol)
    wk = dots.nn(minv, k * (bcol * e_g))
    qk = dots.nt(q, k) * decay
    return u, wk, q * e_g, k * jnp.exp(g_last - g_cl), qk, jnp.exp(g_last)


def _gdn_scan(dots, states, u, wk, qd, kd, qk, gl_tile):
    lane, row = _iota(gl_tile.shape, 1), _iota(gl_tile.shape, 0)
    gl = _stack([
        jnp.sum(jnp.sum(jnp.where((lane == h) & (row == 0), gl_tile, 0.0), axis=1, keepdims=True),
                axis=0, keepdims=True) for h in range(GDN_HEADS)])
    v_new = u - dots.nn(wk, states)
    o = dots.nn(qd, states) + dots.nn(qk, v_new)
    return states * gl + dots.tn(kd, v_new), o


def _heads(x):
    return jnp.stack(_split(x, HW4))


GDN_W = GDN_HEADS * HEAD
HW4 = [HEAD] * GDN_HEADS
LOCAL_CHUNKS = 4
_CHUNK_ROWS = [pl.ds(cc * CHUNK, CHUNK) for cc in range(LOCAL_CHUNKS)]


def _chunk_heads(ref):
    return jnp.concatenate([_heads(ref[rows, :]) for rows in _CHUNK_ROWS], 0)


def _gdn_local_fwd(q, k, v, gb, name):
    s = q.shape[0]
    t = LOCAL_CHUNKS * CHUNK

    def body(q_ref, k_ref, v_ref, gb_ref, u_ref, wk_ref, qd_ref, kd_ref, qk_ref, gl_ref):
        u, wk, qd, kd, qk, gl = _gdn_local(_Dots, _chunk_heads(q_ref), _chunk_heads(k_ref),
                                           _chunk_heads(v_ref), [gb_ref[rows, :] for rows in _CHUNK_ROWS])
        lane = _iota((CHUNK, LANES), 1)
        for cc, rows in enumerate(_CHUNK_ROWS):
            gl_tile = jnp.zeros((CHUNK, LANES), F32)
            for h in range(GDN_HEADS):
                b, cols = cc * GDN_HEADS + h, pl.ds(h * HEAD, HEAD)
                u_ref[rows, cols] = u[b]
                wk_ref[rows, cols] = wk[b]
                qd_ref[rows, cols] = qd[b]
                kd_ref[rows, cols] = kd[b]
                qk_ref[h, rows, :] = qk[b]
                gl_tile = gl_tile + jnp.where(lane == h, gl[b], 0.0)
            gl_ref[rows, :] = gl_tile

    row = pl.BlockSpec((t, GDN_W), lambda i: (i, 0))
    lane = pl.BlockSpec((t, LANES), lambda i: (i, 0))
    qks = pl.BlockSpec((GDN_HEADS, t, CHUNK), lambda i: (0, i, 0))
    return pl.pallas_call(
        body, name=name, grid=(s // t,),
        in_specs=[row, row, row, lane],
        out_specs=[row, row, row, row, qks, lane],
        out_shape=[jax.ShapeDtypeStruct((s, GDN_W), F32)] * 4
        + [jax.ShapeDtypeStruct((GDN_HEADS, s, CHUNK), F32), jax.ShapeDtypeStruct((s, LANES), F32)],
        compiler_params=_params(("parallel",)),
    )(q, k, v, gb)


def _gdn_local_bwd(q, k, v, gb, du, dwk, dqd, dkd, dqk, dgl, name):
    s = q.shape[0]
    t = LOCAL_CHUNKS * CHUNK

    def body(q_ref, k_ref, v_ref, gb_ref, du_ref, dwk_ref, dqd_ref, dkd_ref, dqk_ref, dgl_ref,
             dq_ref, dk_ref, dv_ref, dgb_ref):
        _, vjp = jax.vjp(functools.partial(_gdn_local, _Dots), _chunk_heads(q_ref), _chunk_heads(k_ref),
                         _chunk_heads(v_ref), [gb_ref[rows, :] for rows in _CHUNK_ROWS])
        lane = _iota((CHUNK, LANES), 1)
        dqk = jnp.stack([dqk_ref[h, rows, :] for rows in _CHUNK_ROWS for h in range(GDN_HEADS)])
        dgl = jnp.stack([jnp.sum(jnp.where(lane == h, dgl_ref[rows, :], 0.0), axis=0, keepdims=True)
                         for rows in _CHUNK_ROWS for h in range(GDN_HEADS)])
        d_q, d_k, d_v, d_gbs = vjp((_chunk_heads(du_ref), _chunk_heads(dwk_ref), _chunk_heads(dqd_ref),
                                    _chunk_heads(dkd_ref), dqk, dgl))
        for cc, rows in enumerate(_CHUNK_ROWS):
            for h in range(GDN_HEADS):
                b, cols = cc * GDN_HEADS + h, pl.ds(h * HEAD, HEAD)
                dq_ref[rows, cols] = d_q[b]
                dk_ref[rows, cols] = d_k[b]
                dv_ref[rows, cols] = d_v[b]
            dgb_ref[rows, :] = d_gbs[cc]

    row = pl.BlockSpec((t, GDN_W), lambda i: (i, 0))
    lane = pl.BlockSpec((t, LANES), lambda i: (i, 0))
    qks = pl.BlockSpec((GDN_HEADS, t, CHUNK), lambda i: (0, i, 0))
    return pl.pallas_call(
        body, name=name, grid=(s // t,),
        in_specs=[row, row, row, lane, row, row, row, row, qks, lane],
        out_specs=[row, row, row, lane],
        out_shape=[jax.ShapeDtypeStruct((s, GDN_W), F32)] * 3 + [jax.ShapeDtypeStruct((s, LANES), F32)],
        compiler_params=_params(("parallel",)),
    )(q, k, v, gb, du, dwk, dqd, dkd, dqk, dgl)


SCAN_CHUNKS = 8


def _scan_rows(s):
    k = min(SCAN_CHUNKS, s // CHUNK)
    return k * CHUNK, [pl.ds(cc * CHUNK, CHUNK) for cc in range(k)]


def _gdn_scan_fwd(u, wk, qd, kd, qk, gl, name):
    s = u.shape[0]
    nc = s // CHUNK
    t, chunk_rows = _scan_rows(s)

    def body(u_ref, wk_ref, qd_ref, kd_ref, qk_ref, gl_ref, o_ref, st_ref, state):
        i = pl.program_id(0)

        @pl.when(i == 0)
        def _():
            state[...] = jnp.zeros(state.shape, F32)

        st = state[...]
        for cc, rows in enumerate(chunk_rows):
            st_ref[cc] = st
            st, o = _gdn_scan(_Dots, st, _heads(u_ref[rows, :]), _heads(wk_ref[rows, :]), _heads(qd_ref[rows, :]),
                              _heads(kd_ref[rows, :]), qk_ref[:, rows, :], gl_ref[rows, :])
            o_ref[rows, :] = jnp.concatenate([o[h] for h in range(GDN_HEADS)], 1)
        state[...] = st

    row = pl.BlockSpec((t, GDN_W), lambda i: (i, 0))
    return pl.pallas_call(
        body, name=name, grid=(s // t,),
        in_specs=[row, row, row, row, pl.BlockSpec((GDN_HEADS, t, CHUNK), lambda i: (0, i, 0)),
                  pl.BlockSpec((t, LANES), lambda i: (i, 0))],
        out_specs=[row, pl.BlockSpec((len(chunk_rows), GDN_HEADS, HEAD, HEAD), lambda i: (i, 0, 0, 0))],
        out_shape=[jax.ShapeDtypeStruct((s, GDN_W), F32),
                   jax.ShapeDtypeStruct((nc, GDN_HEADS, HEAD, HEAD), F32)],
        scratch_shapes=[pltpu.VMEM((GDN_HEADS, HEAD, HEAD), F32)],
        compiler_params=_params(("arbitrary",)),
    )(u, wk, qd, kd, qk, gl)


def _gdn_scan_bwd(u, wk, qd, kd, qk, gl, st, do, name):
    s = u.shape[0]
    t, chunk_rows = _scan_rows(s)
    n = s // t

    def body(u_ref, wk_ref, qd_ref, kd_ref, qk_ref, gl_ref, st_ref, do_ref,
             du_ref, dwk_ref, dqd_ref, dkd_ref, dqk_ref, dgl_ref, dstate):
        i = pl.program_id(0)

        @pl.when(i == 0)
        def _():
            dstate[...] = jnp.zeros(dstate.shape, F32)

        unheads = lambda x: jnp.concatenate([x[h] for h in range(GDN_HEADS)], 1)
        ds = dstate[...]
        for cc in reversed(range(len(chunk_rows))):
            rows = chunk_rows[cc]
            _, vjp = jax.vjp(functools.partial(_gdn_scan, _Dots), st_ref[cc], _heads(u_ref[rows, :]),
                             _heads(wk_ref[rows, :]), _heads(qd_ref[rows, :]), _heads(kd_ref[rows, :]),
                             qk_ref[:, rows, :], gl_ref[rows, :])
            ds, d_u, d_wk, d_qd, d_kd, d_qk, d_gl = vjp((ds, _heads(do_ref[rows, :])))
            dqk_ref[:, rows, :] = d_qk
            du_ref[rows, :] = unheads(d_u)
            dwk_ref[rows, :] = unheads(d_wk)
            dqd_ref[rows, :] = unheads(d_qd)
            dkd_ref[rows, :] = unheads(d_kd)
            dgl_ref[rows, :] = d_gl
        dstate[...] = ds

    rev = lambda i: (n - 1 - i, 0)
    row = pl.BlockSpec((t, GDN_W), rev)
    lane = pl.BlockSpec((t, LANES), rev)
    qks = pl.BlockSpec((GDN_HEADS, t, CHUNK), lambda i: (0, n - 1 - i, 0))
    return pl.pallas_call(
        body, name=name, grid=(n,),
        in_specs=[row, row, row, row, qks, lane,
                  pl.BlockSpec((len(chunk_rows), GDN_HEADS, HEAD, HEAD), lambda i: (n - 1 - i, 0, 0, 0)), row],
        out_specs=[row, row, row, row, qks, lane],
        out_shape=[jax.ShapeDtypeStruct((s, GDN_W), F32)] * 4
        + [jax.ShapeDtypeStruct((GDN_HEADS, s, CHUNK), F32), jax.ShapeDtypeStruct((s, LANES), F32)],
        scratch_shapes=[pltpu.VMEM((GDN_HEADS, HEAD, HEAD), F32)],
        compiler_params=_params(("arbitrary",)),
    )(u, wk, qd, kd, qk, gl, st, do)


def _chunk_mask(i, j, t):
    r = i * t + _iota((t, t), 0)
    c = j * t + _iota((t, t), 1)
    return (r // CHUNK) >= (c // CHUNK)


ATT_TILE = 1024
ATT_Q_TILES = 1
ATT_BWD_TILE = 1024


def _attn_fwd(q, k, v, name, gather=()):
    nh, s = MLA_HEADS, q.shape[0]
    tk = min(ATT_TILE, s)
    tq = min(ATT_Q_TILES * tk, s)
    qk = tq // tk
    nq, n = s // tq, s // tk
    nt = (((1,), (1,)), ((), ()))
    host = _Hosted(gather)
    ng = host.n
    steps = nh * nq * n

    def body(*refs):
        q_ref, k_ref, v_ref = refs[:3]
        x_refs = refs[3:3 + ng]
        o_ref, lse_ref = refs[3 + ng:5 + ng]
        got_refs = refs[5 + ng:5 + 2 * ng]
        m_sc, l_sc, acc_sc = refs[5 + 2 * ng:8 + 2 * ng]
        sems = refs[8 + 2 * ng:]
        i, j = pl.program_id(1), pl.program_id(2)
        step_no = (pl.program_id(0) * nq + i) * n + j
        host.open(step_no, steps, x_refs, got_refs, sems)

        @pl.when(j == 0)
        def _():
            m_sc[...] = jnp.full(m_sc.shape, -jnp.inf, F32)
            l_sc[...] = jnp.zeros(l_sc.shape, F32)
            acc_sc[...] = jnp.zeros(acc_sc.shape, F32)

        def step(masked):
            sc = lax.dot_general(q_ref[...], k_ref[...], nt, preferred_element_type=F32)
            if masked:
                r = i * tq + _iota((tq, tk), 0)
                c = j * tk + _iota((tq, tk), 1)
                sc = jnp.where((r // CHUNK) >= (c // CHUNK), sc, -jnp.inf)
            m_prev = m_sc[:, :1]
            m_new = jnp.maximum(m_prev, jnp.max(sc, axis=1, keepdims=True))
            alpha = jnp.exp(m_prev - m_new)
            p = jnp.exp(sc - m_new)
            l_sc[...] = jnp.broadcast_to(alpha * l_sc[:, :1] + jnp.sum(p, axis=1, keepdims=True), l_sc.shape)
            acc_sc[...] = alpha * acc_sc[...] + jnp.dot(p.astype(BF16), v_ref[...], preferred_element_type=F32)
            m_sc[...] = jnp.broadcast_to(m_new, m_sc.shape)

        pl.when(j < i * qk)(lambda: step(False))
        pl.when(j // qk == i)(lambda: step(True))

        @pl.when(j == n - 1)
        def _():
            o_ref[...] = acc_sc[...] / l_sc[:, :1]
            lse_ref[...] = m_sc[...] + jnp.log(l_sc[...])

        host.close(step_no, steps, x_refs, got_refs, sems)

    qrow = lambda h, i, j: (i, h)
    krow = lambda h, i, j: (jnp.minimum(j, (i + 1) * qk - 1), h)
    res = pl.pallas_call(
        body, name=name, grid=(nh, nq, n),
        in_specs=[pl.BlockSpec((tq, QK_PAD), qrow), pl.BlockSpec((tk, QK_PAD), krow),
                  pl.BlockSpec((tk, HEAD), krow)] + host.specs(),
        out_specs=[pl.BlockSpec((tq, HEAD), qrow), pl.BlockSpec((None, tq, LANES), lambda h, i, j: (h, i, 0))]
        + host.specs(),
        out_shape=[jax.ShapeDtypeStruct((s, nh * HEAD), F32), jax.ShapeDtypeStruct((nh, s, LANES), F32)]
        + host.out_shapes(),
        scratch_shapes=[pltpu.VMEM((tq, LANES), F32), pltpu.VMEM((tq, LANES), F32), pltpu.VMEM((tq, HEAD), F32)]
        + host.scratch(),
        compiler_params=_params(("arbitrary",) * 3 if ng else ("parallel", "parallel", "arbitrary")),
    )(q, k, v, *host.operands)
    return res[0], res[1], list(res[2:])


def _attn_bwd(q, k, v, o, do, lse, name, scatter=()):
    nh, s = MLA_HEADS, q.shape[0]
    t = min(ATT_BWD_TILE, s)
    n = s // t
    tn = (((0,), (0,)), ((), ()))
    nt = (((1,), (1,)), ((), ()))
    host = _Hosted(scatter=scatter)
    nx = host.n
    steps = nh * n * n

    def body(*refs):
        q_ref, k_ref, v_ref, o_ref, do_ref, lse_ref = refs[:6]
        x_refs = refs[6:6 + nx]
        dq_ref, dk_ref, dv_ref = refs[6 + nx:9 + nx]
        got_refs = refs[9 + nx:9 + 2 * nx]
        dk_acc, dv_acc, dq_acc = refs[9 + 2 * nx:12 + 2 * nx]
        sems = refs[12 + 2 * nx:]
        j, i = pl.program_id(1), pl.program_id(2)
        step_no = (pl.program_id(0) * n + j) * n + i
        host.open(step_no, steps, x_refs, got_refs, sems)

        @pl.when(i + j == 0)
        def _():
            dq_acc[...] = jnp.zeros(dq_acc.shape, F32)

        @pl.when(i == 0)
        def _():
            dk_acc[...] = jnp.zeros(dk_acc.shape, F32)
            dv_acc[...] = jnp.zeros(dv_acc.shape, F32)

        def step(masked):
            qv, kv, do = q_ref[...], k_ref[...], do_ref[...]
            sc = lax.dot_general(qv, kv, nt, preferred_element_type=F32)
            p = jnp.exp(sc - lse_ref[:, :1])
            if masked:
                p = jnp.where(_chunk_mask(i, j, t), p, 0.0)
            dob = do.astype(BF16)
            dp = lax.dot_general(dob, v_ref[...], nt, preferred_element_type=F32)
            ds = (p * (dp - jnp.sum(do * o_ref[...], axis=1, keepdims=True))).astype(BF16)
            dv_acc[...] += lax.dot_general(p.astype(BF16), dob, tn, preferred_element_type=F32)
            dk_acc[...] += lax.dot_general(ds, qv, tn, preferred_element_type=F32)
            rows = pl.ds(pl.multiple_of(i * t, t), t)
            dq_acc[rows, :] += jnp.dot(ds, kv, preferred_element_type=F32)

        pl.when(i > j)(lambda: step(False))
        pl.when(i == j)(lambda: step(True))

        @pl.when(i == n - 1)
        def _():
            dk_ref[...] = dk_acc[...]
            dv_ref[...] = dv_acc[...]

        @pl.when(i + j == 2 * (n - 1))
        def _():
            dq_ref[...] = dq_acc[...]

        host.close(step_no, steps, x_refs, got_refs, sems)

    qrow = lambda h, j, i: (jnp.maximum(i, j), h)
    krow = lambda h, j, i: (j, h)
    res = pl.pallas_call(
        body, name=name, grid=(nh, n, n),
        in_specs=[pl.BlockSpec((t, QK_PAD), qrow), pl.BlockSpec((t, QK_PAD), krow), pl.BlockSpec((t, HEAD), krow),
                  pl.BlockSpec((t, HEAD), qrow), pl.BlockSpec((t, HEAD), qrow),
                  pl.BlockSpec((None, t, LANES), lambda h, j, i: (h, jnp.maximum(i, j), 0))] + host.specs(),
        out_specs=[pl.BlockSpec((s, QK_PAD), lambda h, j, i: (0, h)),
                   pl.BlockSpec((t, QK_PAD), krow), pl.BlockSpec((t, HEAD), krow)] + host.specs(),
        out_shape=[jax.ShapeDtypeStruct((s, nh * QK_PAD), F32), jax.ShapeDtypeStruct((s, nh * QK_PAD), F32),
                   jax.ShapeDtypeStruct((s, nh * HEAD), F32)] + host.out_shapes(),
        scratch_shapes=[pltpu.VMEM((t, QK_PAD), F32), pltpu.VMEM((t, HEAD), F32), pltpu.VMEM((s, QK_PAD), F32)]
        + host.scratch(),
        compiler_params=_params(("arbitrary", "arbitrary", "arbitrary")),
    )(q, k, v, o, do, lse, *host.operands)
    return res[0], res[1], res[2], list(res[3:])


def _place():
    return lax.axis_index("x"), lax.axis_index("y"), lax.axis_index("c")


def _allgather8(x, name):
    r, c = x.shape

    def body(x_ref, out_ref, send_sems, recv_sems, local_sem):
        mx, my, mc = _place()
        me = 4 * mx + 2 * my + mc
        mine = pltpu.make_async_copy(x_ref, out_ref.at[me], local_sem)
        mine.start()
        copies = []
        for d in range(1, 8):
            px = 1 - mx if d & 4 else mx
            py = 1 - my if d & 2 else my
            pc = 1 - mc if d & 1 else mc
            cp = pltpu.make_async_remote_copy(
                src_ref=x_ref, dst_ref=out_ref.at[me], send_sem=send_sems.at[d - 1], recv_sem=recv_sems.at[d - 1],
                device_id=(px, py, pc), device_id_type=MESH)
            cp.start()
            copies.append(cp)
        for cp in copies:
            cp.wait()
        mine.wait()

    return pl.pallas_call(
        body, name=name,
        out_shape=jax.ShapeDtypeStruct((8, r, c), x.dtype),
        in_specs=[pl.BlockSpec(memory_space=pltpu.VMEM)],
        out_specs=pl.BlockSpec(memory_space=pltpu.VMEM),
        scratch_shapes=[pltpu.SemaphoreType.DMA((7,)), pltpu.SemaphoreType.DMA((7,)), pltpu.SemaphoreType.DMA],
        compiler_params=pltpu.CompilerParams(vmem_limit_bytes=VMEM_LIMIT),
    )(x)


def _allgather_chips(x, name):
    r, c = x.shape

    def body(x_ref, out_ref, send_sems, recv_sems, local_sems):
        for phase in range(3):
            _gather_phase(phase, x_ref, out_ref, send_sems, recv_sems, local_sems, 0)

    return pl.pallas_call(
        body, name=name,
        out_shape=jax.ShapeDtypeStruct((4, r, c), x.dtype),
        in_specs=[pl.BlockSpec(memory_space=pltpu.VMEM)],
        out_specs=pl.BlockSpec(memory_space=pltpu.VMEM),
        scratch_shapes=_gather_sems(1),
        compiler_params=pltpu.CompilerParams(vmem_limit_bytes=VMEM_LIMIT),
    )(x)


GATHER_COPIES = 6


def _gather_sems(n):
    return [pltpu.SemaphoreType.DMA((GATHER_COPIES * n,)), pltpu.SemaphoreType.DMA((GATHER_COPIES * n,)),
            pltpu.SemaphoreType.DMA((n,))]


def _gather_phase(phase, x_ref, out_ref, send_sems, recv_sems, local_sems, slot):
    mx, my, mc = _place()
    j = 2 * mx + my
    rh = x_ref.shape[0] // 2
    base = GATHER_COPIES * slot
    chips = [(1 - mx, my), (mx, 1 - my), (1 - mx, 1 - my)]
    sibling = (mx, my, 1 - mc)

    def half(jj, hc):
        return out_ref.at[jj, pl.ds(hc * rh, rh), :]

    def over_ici(kk, block):
        px, py = chips[kk]
        return pltpu.make_async_remote_copy(
            src_ref=x_ref.at[pl.ds(mc * rh, rh), :], dst_ref=half(block, mc), send_sem=send_sems.at[base + kk],
            recv_sem=recv_sems.at[base + kk], device_id=(px, py, mc), device_id_type=MESH)

    def to_sibling(kk, hc):
        px, py = chips[kk]
        blk = half(2 * px + py, hc)
        return pltpu.make_async_remote_copy(
            src_ref=blk, dst_ref=blk, send_sem=send_sems.at[base + 3 + kk], recv_sem=recv_sems.at[base + 3 + kk],
            device_id=sibling, device_id_type=MESH)

    mine = pltpu.make_async_copy(x_ref, out_ref.at[j], local_sems.at[slot])
    if phase == 0:
        mine.start()
        for kk in range(3):
            over_ici(kk, j).start()
    elif phase == 1:
        for kk, (px, py) in enumerate(chips):
            over_ici(kk, 2 * px + py).wait_recv()
            to_sibling(kk, mc).start()
    else:
        for kk in range(3):
            to_sibling(kk, 1 - mc).wait_recv()
        for kk in range(3):
            over_ici(kk, j).wait_send()
            to_sibling(kk, mc).wait_send()
        mine.wait()


RS_ROWS = 32


def _reduce_scatter_chips(g, name):
    _, r, c = g.shape
    rh = r // 2
    steps = rh // RS_ROWS

    def body(g_ref, out_ref, sib_ref, part_ref, got_ref, send_sems, recv_sems):
        mx, my, mc = _place()
        j = 2 * mx + my
        sibling = (mx, my, 1 - mc)
        chips = [(1 - mx, my), (mx, 1 - my), (1 - mx, 1 - my)]

        to_sib = pltpu.make_async_remote_copy(
            src_ref=g_ref.at[:, pl.ds((1 - mc) * rh, rh), :], dst_ref=sib_ref,
            send_sem=send_sems.at[0], recv_sem=recv_sems.at[0], device_id=sibling, device_id_type=MESH)
        to_sib.start()
        to_sib.wait()

        def add_sibling(step, carry):
            rows = pl.ds(pl.multiple_of(step * RS_ROWS, RS_ROWS), RS_ROWS)
            mine = g_ref[:, pl.ds(pl.multiple_of(mc * rh + step * RS_ROWS, RS_ROWS), RS_ROWS), :]
            part_ref[:, rows, :] = mine.astype(F32) + sib_ref[:, rows, :].astype(F32)
            return carry

        lax.fori_loop(0, steps, add_sibling, 0)

        def to_bf16(step, carry):
            rows = pl.ds(pl.multiple_of(step * RS_ROWS, RS_ROWS), RS_ROWS)
            sib_ref[:, rows, :] = part_ref[:, rows, :].astype(BF16)
            return carry

        lax.fori_loop(0, steps, to_bf16, 0)

        sends = []
        for kk, (px, py) in enumerate(chips):
            cp = pltpu.make_async_remote_copy(
                src_ref=sib_ref.at[2 * px + py], dst_ref=got_ref.at[kk],
                send_sem=send_sems.at[1 + kk], recv_sem=recv_sems.at[1 + kk],
                device_id=(px, py, mc), device_id_type=MESH)
            cp.start()
            sends.append(cp)
        for cp in sends:
            cp.wait()

        def total(step, carry):
            rows = pl.ds(pl.multiple_of(step * RS_ROWS, RS_ROWS), RS_ROWS)
            acc = part_ref[j, rows, :]
            for kk in range(3):
                acc = acc + got_ref[kk, rows, :].astype(F32)
            out_ref[pl.ds(pl.multiple_of(mc * rh + step * RS_ROWS, RS_ROWS), RS_ROWS), :] = acc
            return carry

        lax.fori_loop(0, steps, total, 0)

        done = pltpu.make_async_remote_copy(
            src_ref=out_ref.at[pl.ds(mc * rh, rh), :], dst_ref=out_ref.at[pl.ds(mc * rh, rh), :],
            send_sem=send_sems.at[4], recv_sem=recv_sems.at[4], device_id=sibling, device_id_type=MESH)
        done.start()
        done.wait_send()
        pltpu.make_async_remote_copy(
            src_ref=out_ref.at[pl.ds((1 - mc) * rh, rh), :], dst_ref=out_ref.at[pl.ds((1 - mc) * rh, rh), :],
            send_sem=send_sems.at[4], recv_sem=recv_sems.at[4], device_id=sibling, device_id_type=MESH).wait_recv()

    return pl.pallas_call(
        body, name=name,
        out_shape=jax.ShapeDtypeStruct((r, c), F32),
        in_specs=[pl.BlockSpec(memory_space=pltpu.VMEM)],
        out_specs=pl.BlockSpec(memory_space=pltpu.VMEM),
        scratch_shapes=[pltpu.VMEM((4, rh, c), BF16), pltpu.VMEM((4, rh, c), F32), pltpu.VMEM((3, rh, c), BF16),
                        pltpu.SemaphoreType.DMA((5,)), pltpu.SemaphoreType.DMA((5,))],
        compiler_params=pltpu.CompilerParams(vmem_limit_bytes=VMEM_LIMIT),
    )(g)


def _sum8(x, name):
    _, r, c = x.shape

    def body(x_ref, o_ref):
        acc = x_ref[0]
        for d in range(1, 8):
            acc = acc + x_ref[d]
        o_ref[...] = acc

    return pl.pallas_call(
        body, name=name, out_shape=jax.ShapeDtypeStruct((r, c), F32),
        in_specs=[pl.BlockSpec(memory_space=pltpu.VMEM)], out_specs=pl.BlockSpec(memory_space=pltpu.VMEM),
    )(x)


SCATTER_COPIES = 7


def _scatter_sems(n):
    return [pltpu.SemaphoreType.DMA((SCATTER_COPIES * n,)), pltpu.SemaphoreType.DMA((SCATTER_COPIES * n,))]


def _scatter_phase(phase, g_ref, got_ref, send_sems, recv_sems, slot):
    mx, my, mc = _place()
    rh = g_ref.shape[1] // 2
    base = SCATTER_COPIES * slot
    for d in range(1, 8):
        px = 1 - mx if d & 4 else mx
        py = 1 - my if d & 2 else my
        pc = 1 - mc if d & 1 else mc
        cp = pltpu.make_async_remote_copy(
            src_ref=g_ref.at[2 * px + py, pl.ds(pc * rh, rh), :], dst_ref=got_ref.at[d - 1],
            send_sem=send_sems.at[base + d - 1], recv_sem=recv_sems.at[base + d - 1],
            device_id=(px, py, pc), device_id_type=MESH)
        if phase == 0:
            cp.start()
        else:
            cp.wait()


def _scatter_sum(g, got, name):
    mx, my, mc = _place()
    rh, c = got.shape[1], got.shape[2]
    mine = lax.dynamic_slice(g, (2 * mx + my, mc * rh, 0), (1, rh, c))[0]
    t = _pick(rh, 256, 16)

    def body(m_ref, got_ref, o_ref):
        acc = m_ref[...].astype(F32)
        for d in range(SCATTER_COPIES):
            acc = acc + got_ref[d].astype(F32)
        o_ref[...] = acc

    return pl.pallas_call(
        body, name=name, grid=(rh // t,),
        in_specs=[pl.BlockSpec((t, c), lambda i: (i, 0)), pl.BlockSpec((SCATTER_COPIES, t, c), lambda i: (0, i, 0))],
        out_specs=pl.BlockSpec((t, c), lambda i: (i, 0)),
        out_shape=jax.ShapeDtypeStruct((rh, c), F32), compiler_params=_params(("parallel",)),
    )(mine, got)


def _scatter_finish(halves, name):
    n = len(halves)

    def body(*refs):
        h_refs, o_refs = refs[:n], refs[n:2 * n]
        send_sems, recv_sems = refs[2 * n:]
        mx, my, mc = _place()
        copies = [pltpu.make_async_remote_copy(
            src_ref=h_refs[kk], dst_ref=o_refs[kk], send_sem=send_sems.at[kk], recv_sem=recv_sems.at[kk],
            device_id=(mx, my, 1 - mc), device_id_type=MESH) for kk in range(n)]
        for cp in copies:
            cp.start()
        for cp in copies:
            cp.wait()

    hbm = pl.BlockSpec(memory_space=pl.ANY)
    theirs = pl.pallas_call(
        body, name=name, in_specs=[hbm] * n, out_specs=[hbm] * n,
        out_shape=[jax.ShapeDtypeStruct(h.shape, F32) for h in halves],
        scratch_shapes=[pltpu.SemaphoreType.DMA((n,)), pltpu.SemaphoreType.DMA((n,))],
    )(*halves)
    south = lax.axis_index("c") == 0
    return [jnp.concatenate([jnp.where(south, m, t), jnp.where(south, t, m)], axis=0) for m, t in zip(halves, theirs)]


def _adamw(w, g, m, v, name):
    r, c = w.shape
    t = _pick(r, 256, SUBLANES)
    spec = pl.BlockSpec((t, c), lambda i: (i, 0))

    def body(w_ref, g_ref, m_ref, v_ref, d_ref, nm_ref, nv_ref):
        gv = g_ref[...]
        m_new = ADAM_B1 * m_ref[...] + (1.0 - ADAM_B1) * gv
        v_new = ADAM_B2 * v_ref[...] + (1.0 - ADAM_B2) * (gv * gv)
        m_hat = m_new / (1.0 - ADAM_B1 ** ADAM_STEP)
        v_hat = v_new / (1.0 - ADAM_B2 ** ADAM_STEP)
        d_ref[...] = -ADAM_LR * (m_hat / (jnp.sqrt(v_hat) + ADAM_EPS) + ADAM_WD * w_ref[...])
        nm_ref[...] = m_new
        nv_ref[...] = v_new

    return pl.pallas_call(
        body, name=name, grid=(r // t,), in_specs=[spec] * 4, out_specs=[spec] * 3,
        out_shape=[jax.ShapeDtypeStruct((r, c), F32)] * 3, compiler_params=_params(("parallel",)),
    )(w, g, m, v)


def _pack_rows(parts):
    rows, offs, o = [], [], 0
    for p in parts:
        f = p.reshape(-1)
        n = -(-f.shape[0] // (LANES * SUBLANES)) * SUBLANES
        rows.append(jnp.pad(f, (0, n * LANES - f.shape[0])).reshape(n, LANES))
        offs.append((o, n))
        o += n
    return jnp.concatenate(rows, 0), offs


def _unpack_rows(packed, offs, shapes):
    out = []
    for (o, n), shp in zip(offs, shapes):
        size = 1
        for d in shp:
            size *= d
        out.append(packed[o:o + n].reshape(-1)[:size].reshape(shp))
    return out


def _mm_hosting(a, b, mode, out_dtype, name, gather=(), chips=None, scatter=()):
    res = _mm(a, b, mode, out_dtype, name, gather=gather, chips=chips, scatter=scatter)
    return (res[0], list(res[1:])) if (gather or scatter) else (res, [])


def _ffn_fwd(x, s, sh, g, w_in, w_out, tag, gather_in=(), gather_out=()):
    (h,) = _rowcall(lambda r, p: ([_modulate(r[0], p[0], p[1])], []), [x], [s, sh], [(x.shape[1], BF16)], [],
                    tile=ROW_TILE, name=tag + "_mod")
    gu, got = _mm_hosting(h, w_in, "nn", BF16, tag + "_in", gather_in, chips="b")
    if w_out is None:
        first = got.pop(0)
        w_out = first.reshape(4 * first.shape[1], first.shape[2])
    (act,) = _rowcall(lambda r, p: ([_silu(r[0].astype(F32)) * r[1].astype(F32)], []),
                      [(gu, D_FF, 0), (gu, D_FF, 1)], [], [(D_FF, BF16)], [], tile=ROW_TILE_WIDE, name=tag + "_act")
    f, got_out = _mm_hosting(act, w_out, "nn", F32, tag + "_out", gather_out)
    got = got + got_out
    (y,) = _rowcall(lambda r, p: ([r[0] + 0.5 * p[0] * r[1]], []), [x, f], [g], [(x.shape[1], F32)], [],
                    tile=ROW_TILE, name=tag + "_res")
    return y, (x, h, gu, act, f), got, w_out


def _ffn_bwd(dy, saved, s, sh, g, w_in, w_out, tag, scatter_bin=(), scatter_bwin=()):
    x, h, gu, act, f = saved
    d = x.shape[1]
    df, dg = _rowcall(lambda r, p: ([0.5 * p[0] * r[0]], [0.5 * jnp.sum(r[0] * r[1], 0, keepdims=True)]),
                      [dy, f], [g], [(d, BF16)], [(1, d)], tile=ROW_TILE, name=tag + "_bres")
    da = _mm(df, w_out, "nt", BF16, tag + "_bout")
    dw_out = _mm(act, df, "tn", BF16, tag + "_bwout")

    def act_bwd(r, p):
        gate, up, dav = r[0].astype(F32), r[1].astype(F32), r[2].astype(F32)
        _, vjp = jax.vjp(lambda a, b: _silu(a) * b, gate, up)
        dgate, dup = vjp(dav)
        return [jnp.concatenate([dgate, dup], 1)], []

    (dgu,) = _rowcall(act_bwd, [(gu, D_FF, 0), (gu, D_FF, 1), da], [], [(2 * D_FF, BF16)], [], tile=ROW_TILE_WIDE,
                      name=tag + "_bact")
    dh, got_a = _mm_hosting(dgu, w_in, "nt", F32, tag + "_bin", chips="b", scatter=scatter_bin)
    dw_in, got_b = _mm_hosting(h, dgu, "tn", BF16, tag + "_bwin", chips="out", scatter=scatter_bwin)

    def mod_bwd(r, p):
        _, vjp = jax.vjp(_modulate, r[0], p[0], p[1])
        dx, ds, dsh = vjp(r[1])
        return [r[2] + dx], [ds, dsh]

    dx, ds, dsh = _rowcall(mod_bwd, [x, dh, dy], [s, sh], [(d, F32)], [(1, d), (1, d)], tile=ROW_TILE, name=tag + "_bmod")
    return dx, (dsh, ds, dg), dw_in, dw_out, list(got_a) + list(got_b)


def _mixer_fwd(x, s, sh, g, wts, rope, gather=()):
    w_in_p, conv8, a_log, dt_bias, wn, wq, w_uq_p, wkv, w_ukv, wqn, wqr, wkn, wkr, won, w_out = wts
    cos2, sin2 = rope
    d = x.shape[1]
    (h,) = _rowcall(lambda r, p: ([_modulate(r[0], p[0], p[1])], []), [x], [s, sh], [(d, BF16)], [],
                    tile=ROW_TILE, name="mix_mod")
    proj = _mm(h, w_in_p, "nn", F32, "mix_in")
    qkv_c = _conv_fwd(proj, conv8, "mix_conv")
    gab = (proj, LANES, 23)

    q, k, v, gb = _rowcall(
        lambda r, p: (list(_gdn_prep_core(_split(r[0], [HEAD] * 12), r[1], p[0], p[1])), []),
        [qkv_c, gab], [a_log, dt_bias], [(512, F32)] * 3 + [(LANES, F32)], [], tile=ROW_TILE_WIDE, name="mix_gdn_prep")
    gdn_local = _gdn_local_fwd(q, k, v, gb, "mix_gdn_local")
    o_gdn, gdn_states = _gdn_scan_fwd(*gdn_local, "mix_gdn_scan")
    states = (gdn_local, gdn_states)

    cq, ckv, kr = (proj, 512, 4), (proj, 256, 10), (proj, LANES, 22)
    cqn, ckvn, k_rope = _rowcall(
        lambda r, p: (list(_mla_prep_core(r[0][:, :MLA_Q_LORA], r[1], r[2], r[3], r[4], p[0], p[1], p[2])), []),
        [cq, ckv, kr, cos2, sin2], [wq, wkv, wkr], [(MLA_Q_LORA, BF16), (MLA_KV_LORA, BF16), (LANES, F32)], [],
        tile=ROW_TILE, name="mix_mla_prep")
    qf = _mm(cqn, w_uq_p, "nn", F32, "mix_uq")
    kvf = _mm(ckvn, w_ukv, "nn", F32, "mix_ukv")

    def qk_prep(r, p):
        qparts = _split(r[0], [HEAD] * 8)
        kvparts = _split(r[1], [HEAD] * 8)
        qs, ks, vs = _qk_prep_core(qparts[:4], qparts[4:], kvparts[0::2], kvparts[1::2], r[2], r[3], r[4],
                                   p[0], p[1], p[2])
        return [jnp.concatenate(qs, 1), jnp.concatenate(ks, 1), jnp.concatenate(vs, 1)], []

    qa, ka, va = _rowcall(qk_prep, [qf, kvf, k_rope, cos2, sin2], [wqn, wqr, wkn],
                          [(4 * QK_PAD, BF16), (4 * QK_PAD, BF16), (4 * HEAD, BF16)], [], tile=ROW_TILE_WIDE,
                          name="mix_qk_prep")
    o_b, lse, got = _attn_fwd(qa, ka, va, "mix_attn", gather=gather)
    if w_out is None:
        first = got.pop(0)
        w_out = first.reshape(4 * first.shape[1], first.shape[2])

    gz = (proj, 512, 3)
    (mixed,) = _rowcall(
        lambda r, p: ([_mix_post_core(_split(r[0], HW4), _split(r[1], HW4), _split(r[2], HW4), p[0], p[1])], []),
        [o_gdn, gz, o_b], [wn, won], [(2 * 512, BF16)], [], tile=ROW_TILE, name="mix_post")
    y = _mm(mixed, w_out, "nn", F32, "mix_out")
    (x_out,) = _rowcall(lambda r, p: ([r[0] + p[0] * r[1]], []), [x, y], [g], [(d, F32)], [], tile=ROW_TILE,
                        name="mix_res")
    saved = (x, h, proj, qkv_c, q, k, v, gb, states, o_gdn, cqn, ckvn, k_rope, qf, kvf, qa, ka, va, o_b, lse,
             mixed, y)
    return x_out, saved, got, w_out


def _mixer_bwd(dy, saved, s, sh, g, wts, rope, scatter=()):
    w_in_p, conv8, a_log, dt_bias, wn, wq, w_uq_p, wkv, w_ukv, wqn, wqr, wkn, wkr, won, w_out = wts
    cos2, sin2 = rope
    (x, h, proj, qkv_c, q, k, v, gb, states, o_gdn, cqn, ckvn, k_rope, qf, kvf, qa, ka, va, o_b, lse,
     mixed, y) = saved
    d = x.shape[1]
    dyb, dg = _rowcall(lambda r, p: ([p[0] * r[0]], [jnp.sum(r[0] * r[1], 0, keepdims=True)]),
                       [dy, y], [g], [(d, BF16)], [(1, d)], tile=ROW_TILE, name="mix_bres")
    dmixed = _mm(dyb, w_out, "nt", F32, "mix_bout")
    dw_out = _mm(mixed, dyb, "tn", BF16, "mix_bwout")

    gz = (proj, 512, 3)

    def post_bwd(r, p):
        _, vjp = jax.vjp(_mix_post_core, _split(r[0], HW4), _split(r[1], HW4), _split(r[2], HW4), p[0], p[1])
        do, dz, dob, dwn, dwon = vjp(r[3])
        return [jnp.concatenate(do, 1), jnp.concatenate(dz, 1), jnp.concatenate(dob, 1)], [dwn, dwon]

    do_gdn, dgz, do_b, dwn, dwon = _rowcall(post_bwd, [o_gdn, gz, o_b, dmixed], [wn, won], [(512, F32)] * 3,
                                            [(1, HEAD), (1, HEAD)], tile=ROW_TILE_WIDE, name="mix_bpost")

    dqa, dka, dva, got = _attn_bwd(qa, ka, va, o_b, do_b, lse, "mix_battn", scatter=scatter)

    def qk_bwd(r, p):
        qparts = _split(r[0], [HEAD] * 8)
        kvparts = _split(r[1], [HEAD] * 8)
        _, vjp = jax.vjp(_qk_prep_core, qparts[:4], qparts[4:], kvparts[0::2], kvparts[1::2], r[2], r[3], r[4],
                         p[0], p[1], p[2])
        cot = (_split(r[5], [QK_PAD] * 4), _split(r[6], [QK_PAD] * 4), _split(r[7], HW4))
        dqn, dqr, dkn, dvp, dkrope, _, _, dwqn, dwqr, dwkn = vjp(cot)
        dkv = []
        for a, b in zip(dkn, dvp):
            dkv += [a, b]
        return [jnp.concatenate(list(dqn) + list(dqr), 1), jnp.concatenate(dkv, 1), dkrope], [dwqn, dwqr, dwkn]

    dqf, dkvf, dk_rope, dwqn, dwqr, dwkn = _rowcall(
        qk_bwd, [qf, kvf, k_rope, cos2, sin2, dqa, dka, dva], [wqn, wqr, wkn],
        [(8 * HEAD, BF16), (8 * HEAD, BF16), (LANES, F32)], [(1, HEAD)] * 3, tile=ROW_TILE_WIDE, name="mix_bqk_prep")
    dcqn = _mm(dqf, w_uq_p, "nt", F32, "mix_buq")
    dw_uq_p = _mm(cqn, dqf, "tn", F32, "mix_bwuq")
    dckvn = _mm(dkvf, w_ukv, "nt", F32, "mix_bukv")
    dw_ukv = _mm(ckvn, dkvf, "tn", F32, "mix_bwukv")

    cq, ckv, kr = (proj, 512, 4), (proj, 256, 10), (proj, LANES, 22)

    def mla_bwd(r, p):
        _, vjp = jax.vjp(_mla_prep_core, r[0][:, :MLA_Q_LORA], r[1], r[2], r[3], r[4], p[0], p[1], p[2])
        dcq, dckv, dkr, _, _, dwq, dwkv, dwkr = vjp((r[5], r[6], r[7]))
        pad = jnp.zeros((dcq.shape[0], 512 - MLA_Q_LORA), F32)
        return [jnp.concatenate([dcq, pad], 1), dckv, dkr], [dwq, dwkv, dwkr]

    dcq, dckv, dkr, dwq, dwkv, dwkr = _rowcall(
        mla_bwd, [cq, ckv, kr, cos2, sin2, dcqn, dckvn, dk_rope], [wq, wkv, wkr],
        [(512, F32), (MLA_KV_LORA, F32), (LANES, F32)], [(1, MLA_Q_LORA), (1, MLA_KV_LORA), (1, LANES)],
        tile=ROW_TILE, name="mix_bmla_prep")

    gdn_local, gdn_states = states
    d_local = _gdn_scan_bwd(*gdn_local, gdn_states, do_gdn, "mix_bgdn_scan")
    dq, dk, dv, dgb = _gdn_local_bwd(q, k, v, gb, *d_local, "mix_bgdn_local")
    gab = (proj, LANES, 23)

    def gdn_prep_bwd(r, p):
        _, vjp = jax.vjp(_gdn_prep_core, _split(r[0], [HEAD] * 12), r[1], p[0], p[1])
        dparts, dgab, da_log, ddt = vjp((r[2], r[3], r[4], r[5]))
        return [jnp.concatenate(dparts, 1), dgab], [da_log, ddt]

    dqkv_c, dgab, da_log, ddt = _rowcall(gdn_prep_bwd, [qkv_c, gab, dq, dk, dv, dgb], [a_log, dt_bias],
                                         [(1536, F32), (LANES, F32)], [(1, LANES), (1, LANES)], tile=ROW_TILE_WIDE,
                                         name="mix_bgdn_prep")
    dqkv_pre, dconv8 = _conv_bwd(proj, dqkv_c, conv8, "mix_bconv")

    dproj = jnp.concatenate([dqkv_pre.astype(BF16), dgz.astype(BF16), dcq.astype(BF16), dckv.astype(BF16),
                             dkr.astype(BF16), dgab.astype(BF16)], axis=1)
    dh = _mm(dproj, w_in_p, "nt", F32, "mix_bin")
    dw_in_p = _mm(h, dproj, "tn", F32, "mix_bwin")

    def mod_bwd(r, p):
        _, vjp = jax.vjp(_modulate, r[0], p[0], p[1])
        dx, ds, dsh = vjp(r[1])
        return [r[2] + dx], [ds, dsh]

    dx, ds, dsh = _rowcall(mod_bwd, [x, dh, dy], [s, sh], [(d, F32)], [(1, d), (1, d)], tile=ROW_TILE, name="mix_bmod")
    small = dict(conv=dconv8, a_log=da_log, dt=ddt, wn=dwn, wq=dwq, wkv=dwkv, wqn=dwqn, wqr=dwqr, wkn=dwkn,
                 wkr=dwkr, won=dwon)
    return dx, (dsh, ds, dg), dw_in_p, dw_uq_p, dw_ukv, dw_out, small, got


def _pad_cols(a, n):
    return jnp.pad(a, ((0, 0),) * (a.ndim - 1) + ((0, n - a.shape[-1]),))


def _pack_w_in(w):
    z = lambda n: jnp.zeros((w.shape[0], n), w.dtype)
    return jnp.concatenate([w[:, 0:2048], w[:, 2056:2440], z(128), w[:, 2440:2696], w[:, 2696:2760], z(64),
                            w[:, 2048:2056], z(120)], axis=1)


def _unpack_w_in(wp):
    return jnp.concatenate([wp[:, 0:2048], wp[:, 2944:2952], wp[:, 2048:2432], wp[:, 2560:2816], wp[:, 2816:2880]],
                           axis=1)


def _pack_w_uq(w):
    z = jnp.zeros((w.shape[0], LANES - MLA_ROPE), w.dtype)
    nope = [w[:, h * 192:h * 192 + HEAD] for h in range(MLA_HEADS)]
    rope = []
    for h in range(MLA_HEADS):
        rope += [w[:, h * 192 + HEAD:(h + 1) * 192], z]
    return jnp.concatenate(nope + rope, axis=1)


def _unpack_w_uq(wp):
    cols = []
    for h in range(MLA_HEADS):
        cols += [wp[:, h * HEAD:(h + 1) * HEAD], wp[:, 512 + h * LANES:512 + h * LANES + MLA_ROPE]]
    return jnp.concatenate(cols, axis=1)


def _cols_to_chips(a):
    r, c = a.shape
    return a.reshape(r, 4, c // 4).transpose(1, 0, 2)


def _chips_to_cols(a):
    _, r, n = a.shape
    return a.transpose(1, 0, 2).reshape(r, 4 * n)


def _pad128(v, n=LANES):
    return _pad_cols(v.reshape(1, -1), n)


def kernel(x, c, positions, w_ada, b_ada, ffn1_w_in, ffn1_w_out, w_in, gdn_conv_w, gdn_a_log, gdn_dt_bias, gdn_norm_w, mla_q_norm_w, mla_w_uq, mla_kv_norm_w, mla_w_ukv, qkn_q_nope, qkn_q_rope, qkn_k_nope, qkn_k_rope, mla_out_norm_w, w_out, ffn2_w_in, ffn2_w_out, loss_target, m_w_ada, m_b_ada, m_ffn1_w_in, m_ffn1_w_out, m_w_in, m_gdn_conv_w, m_gdn_a_log, m_gdn_dt_bias, m_gdn_norm_w, m_mla_q_norm_w, m_mla_w_uq, m_mla_kv_norm_w, m_mla_w_ukv, m_qkn_q_nope, m_qkn_q_rope, m_qkn_k_nope, m_qkn_k_rope, m_mla_out_norm_w, m_w_out, m_ffn2_w_in, m_ffn2_w_out, v_w_ada, v_b_ada, v_ffn1_w_in, v_ffn1_w_out, v_w_in, v_gdn_conv_w, v_gdn_a_log, v_gdn_dt_bias, v_gdn_norm_w, v_mla_q_norm_w, v_mla_w_uq, v_mla_kv_norm_w, v_mla_w_ukv, v_qkn_q_nope, v_qkn_q_rope, v_qkn_k_nope, v_qkn_k_rope, v_mla_out_norm_w, v_w_out, v_ffn2_w_in, v_ffn2_w_out):
    weights = dict(w_ada=w_ada, b_ada=b_ada, ffn1_w_in=ffn1_w_in, ffn1_w_out=ffn1_w_out, w_in=w_in,
                   gdn_conv_w=gdn_conv_w, gdn_a_log=gdn_a_log, gdn_dt_bias=gdn_dt_bias, gdn_norm_w=gdn_norm_w,
                   mla_q_norm_w=mla_q_norm_w, mla_w_uq=mla_w_uq, mla_kv_norm_w=mla_kv_norm_w, mla_w_ukv=mla_w_ukv,
                   qkn_q_nope=qkn_q_nope, qkn_q_rope=qkn_q_rope, qkn_k_nope=qkn_k_nope, qkn_k_rope=qkn_k_rope,
                   mla_out_norm_w=mla_out_norm_w, w_out=w_out, ffn2_w_in=ffn2_w_in, ffn2_w_out=ffn2_w_out)
    moms_m = dict(w_ada=m_w_ada, b_ada=m_b_ada, ffn1_w_in=m_ffn1_w_in, ffn1_w_out=m_ffn1_w_out, w_in=m_w_in,
                  gdn_conv_w=m_gdn_conv_w, gdn_a_log=m_gdn_a_log, gdn_dt_bias=m_gdn_dt_bias,
                  gdn_norm_w=m_gdn_norm_w, mla_q_norm_w=m_mla_q_norm_w, mla_w_uq=m_mla_w_uq,
                  mla_kv_norm_w=m_mla_kv_norm_w, mla_w_ukv=m_mla_w_ukv, qkn_q_nope=m_qkn_q_nope,
                  qkn_q_rope=m_qkn_q_rope, qkn_k_nope=m_qkn_k_nope, qkn_k_rope=m_qkn_k_rope,
                  mla_out_norm_w=m_mla_out_norm_w, w_out=m_w_out, ffn2_w_in=m_ffn2_w_in, ffn2_w_out=m_ffn2_w_out)
    moms_v = dict(w_ada=v_w_ada, b_ada=v_b_ada, ffn1_w_in=v_ffn1_w_in, ffn1_w_out=v_ffn1_w_out, w_in=v_w_in,
                  gdn_conv_w=v_gdn_conv_w, gdn_a_log=v_gdn_a_log, gdn_dt_bias=v_gdn_dt_bias,
                  gdn_norm_w=v_gdn_norm_w, mla_q_norm_w=v_mla_q_norm_w, mla_w_uq=v_mla_w_uq,
                  mla_kv_norm_w=v_mla_kv_norm_w, mla_w_ukv=v_mla_w_ukv, qkn_q_nope=v_qkn_q_nope,
                  qkn_q_rope=v_qkn_q_rope, qkn_k_nope=v_qkn_k_nope, qkn_k_rope=v_qkn_k_rope,
                  mla_out_norm_w=v_mla_out_norm_w, w_out=v_w_out, ffn2_w_in=v_ffn2_w_in, ffn2_w_out=v_ffn2_w_out)
    names = list(weights)

    seq, d = x.shape[1], x.shape[2]
    x2d = x.reshape(seq, d)
    tgt = loss_target.reshape(seq, d)
    mx, my, mc = _place()
    chip = 2 * mx + my
    me = 2 * chip + mc
    n_mod = b_ada.shape[1] // d
    shard = w_ada.shape[2]

    half = MLA_ROPE // 2
    inv_freq = 10000.0 ** (-jnp.arange(half, dtype=F32) / half)
    ang = positions.astype(F32).reshape(seq, 1) * inv_freq
    cosv, sinv = jnp.cos(ang), jnp.sin(ang)
    cos2 = _pad_cols(jnp.concatenate([cosv, cosv], 1), LANES)
    sin2 = _pad_cols(jnp.concatenate([-sinv, sinv], 1), LANES)
    rope = (cos2, sin2)

    c_all = _allgather8(jnp.pad(c, ((0, SUBLANES - 1), (0, 0))), "gather_c")[:, 0, :]
    (sc_all,) = _rowcall(lambda r, p: ([_silu(r[0])], []), [c_all], [], [(d, F32)], [], tile=8, name="ada_silu")
    mod_part = _mm(sc_all, w_ada[0], "nn", F32, "ada_mm", hi=True)
    mod_all = _allgather8(mod_part, "gather_mod")
    mod_rows = lax.dynamic_index_in_dim(mod_all, me, axis=1, keepdims=False)
    mod_raw = jnp.concatenate([mod_rows[2 * jj] for jj in range(4)], axis=0).reshape(1, 4 * shard)
    (mod,) = _rowcall(lambda r, p: ([r[0] + r[1]], []),
                      [jnp.pad(mod_raw, ((0, 7), (0, 0))), jnp.pad(b_ada, ((0, 7), (0, 0)))], [],
                      [(4 * shard, F32)], [], tile=8, name="ada_bias")
    mods = [mod[0:1, i * d:(i + 1) * d] for i in range(n_mod)]
    sh1, s1, g1, sh2, s2, g2, sh3, s3, g3 = mods

    def shard_bf16(w, pad_to=None):
        w2 = w[0].astype(BF16)
        return _pad_cols(w2, pad_to) if pad_to else w2

    def cols_of(got, w):
        return _chips_to_cols(got[:, :, :w.shape[2]])

    def rows_of(got):
        return got.reshape(4 * got.shape[1], got.shape[2])

    f1_in = _allgather_chips(shard_bf16(ffn1_w_in), "gather_f1_in")
    conv_all = _allgather8(jnp.pad(gdn_conv_w[0], ((0, SUBLANES - CONV_K), (0, 0))), "gather_conv")
    conv8 = jnp.concatenate([conv_all[2 * jj] for jj in range(4)], axis=1)

    x1, sv1, got, f1_out = _ffn_fwd(
        x2d, s1, sh1, g1, f1_in, None, "ffn1",
        gather_in=[shard_bf16(ffn1_w_out), shard_bf16(w_in, 768), shard_bf16(mla_w_uq, 256), shard_bf16(mla_w_ukv)])
    w_in_full, w_uq_full, w_ukv_full = cols_of(got[0], w_in), cols_of(got[1], mla_w_uq), cols_of(got[2], mla_w_ukv)
    wts = (_pack_w_in(w_in_full), conv8, _pad128(gdn_a_log), _pad128(gdn_dt_bias), gdn_norm_w,
           mla_q_norm_w, _pack_w_uq(w_uq_full), mla_kv_norm_w, w_ukv_full, qkn_q_nope, _pad128(qkn_q_rope),
           qkn_k_nope, _pad128(qkn_k_rope), mla_out_norm_w, None)
    xm, svm, got, w_out_full = _mixer_fwd(
        x1, s2, sh2, g2, wts, rope, gather=[shard_bf16(w_out), shard_bf16(ffn2_w_in), shard_bf16(ffn2_w_out)])
    wts = wts[:-1] + (w_out_full,)
    f2_in, f2_out = got[0], rows_of(got[1])
    x3, sv3, _, _ = _ffn_fwd(xm, s3, sh3, g3, f2_in, f2_out, "ffn2")

    def loss_fn(r, p):
        err = r[0] - r[1]
        part = 0.5 * jnp.sum(jnp.sum(err * err, axis=1, keepdims=True) * (1.0 / d), axis=0, keepdims=True)
        return [err * (1.0 / d)], [jnp.broadcast_to(part, (1, LANES))]

    dy, loss_part = _rowcall(loss_fn, [x3, tgt], [], [(d, F32)], [(1, LANES)], tile=ROW_TILE, name="loss")
    loss = lax.psum(loss_part[0, 0], ("x", "y", "c"))

    def chip_cols(dw, pad_to=None):
        g4 = _cols_to_chips(dw).astype(BF16)
        return _pad_cols(g4, pad_to) if pad_to else g4

    def chip_rows(dw):
        return dw.astype(BF16).reshape(4, dw.shape[0] // 4, dw.shape[1])

    dxm, dmod3, dw_f2_in, dw_f2_out, _ = _ffn_bwd(dy, sv3, s3, sh3, g3, f2_in, f2_out, "ffn2")
    parts2 = [dw_f2_in, chip_rows(dw_f2_out)]
    dx1, dmod2, dw_in_p, dw_uq_p, dw_ukv, dw_out_m, small, got2 = _mixer_bwd(dxm, svm, s2, sh2, g2, wts, rope,
                                                                             scatter=parts2)
    parts_m = [chip_cols(_unpack_w_in(dw_in_p), 768), chip_cols(_unpack_w_uq(dw_uq_p), 256), chip_cols(dw_ukv),
               chip_rows(dw_out_m)]
    dx0, dmod1, dw_f1_in, dw_f1_out, got_m = _ffn_bwd(dx1, sv1, s1, sh1, g1, f1_in, f1_out, "ffn1",
                                                      scatter_bin=parts_m[:1], scatter_bwin=parts_m[1:])
    grad_x = dx0.reshape(x.shape)

    dmod = jnp.concatenate(list(dmod1) + list(dmod2) + list(dmod3), axis=1)
    small_parts = [dmod, small["conv"][:CONV_K], small["a_log"], small["dt"], small["wn"], small["wq"],
                   small["wkv"], small["wqn"], small["wqr"], small["wkn"], small["wkr"], small["won"]]
    packed, offs = _pack_rows(small_parts)
    gathered = _allgather8(packed, "gather_small")
    total = _sum8(gathered, "sum_small")
    (g_b_ada, g_conv, g_a_log, g_dt, g_wn, g_wq, g_wkv, g_wqn, g_wqr, g_wkn, g_wkr, g_won) = _unpack_rows(
        total, offs, [p.shape for p in small_parts])
    dmod_all = _unpack_rows(gathered.reshape(-1, LANES),
                            [(dd * packed.shape[0] + offs[0][0], offs[0][1]) for dd in range(8)],
                            [dmod.shape] * 8)
    dmod_all = jnp.concatenate(dmod_all, axis=0)
    dmod_mine = lax.dynamic_slice_in_dim(dmod_all, chip * shard, shard, axis=1)

    def ada_grad(r, p):
        acc = jnp.zeros((r[0].shape[0], shard), F32)
        for b in range(8):
            acc = acc + r[0][:, b:b + 1] * p[0][b:b + 1, :]
        return [acc], []

    (g_w_ada,) = _rowcall(ada_grad, [_pad_cols(sc_all.T, LANES)], [dmod_mine], [(shard, F32)], [], tile=ROW_TILE_WIDE,
                          name="ada_grad")

    grads = dict(
        w_ada=g_w_ada[None], b_ada=g_b_ada,
        gdn_conv_w=lax.dynamic_slice_in_dim(g_conv, chip * gdn_conv_w.shape[2], gdn_conv_w.shape[2], axis=1)[None],
        gdn_a_log=g_a_log[:, :GDN_HEADS], gdn_dt_bias=g_dt[:, :GDN_HEADS], gdn_norm_w=g_wn, mla_q_norm_w=g_wq,
        mla_kv_norm_w=g_wkv, qkn_q_nope=g_wqn, qkn_q_rope=g_wqr[:, :MLA_ROPE], qkn_k_nope=g_wkn,
        qkn_k_rope=g_wkr[:, :MLA_ROPE], mla_out_norm_w=g_won)

    hosted = ["ffn2_w_in", "ffn2_w_out", "w_in", "mla_w_uq", "mla_w_ukv", "w_out"]
    halves = [_scatter_sum(part, got, "rs_sum_" + nme)
              for nme, part, got in zip(hosted, parts2 + parts_m, got2 + got_m)]
    for nme, full in zip(hosted, _scatter_finish(halves, "rs_finish")):
        grads[nme] = full[:, :weights[nme].shape[2]][None]
    grads["ffn1_w_in"] = _reduce_scatter_chips(dw_f1_in, "rs_f1_in")[None]
    grads["ffn1_w_out"] = _reduce_scatter_chips(chip_rows(dw_f1_out), "rs_f1_out")[None]

    big = ["w_ada", "ffn1_w_in", "ffn1_w_out", "w_in", "mla_w_uq", "mla_w_ukv", "w_out", "ffn2_w_in", "ffn2_w_out"]
    delta, new_m, new_v = {}, {}, {}
    for nme in big:
        shp = weights[nme].shape
        dl, nm, nv = _adamw(weights[nme][0], grads[nme][0], moms_m[nme][0], moms_v[nme][0], "adamw_" + nme)
        delta[nme], new_m[nme], new_v[nme] = dl.reshape(shp), nm.reshape(shp), nv.reshape(shp)
    tiny = [nme for nme in names if nme not in big]
    shapes = [weights[nme].shape for nme in tiny]
    pw, poffs = _pack_rows([weights[nme] for nme in tiny])
    pg, _ = _pack_rows([grads[nme] for nme in tiny])
    pm, _ = _pack_rows([moms_m[nme] for nme in tiny])
    pv, _ = _pack_rows([moms_v[nme] for nme in tiny])
    pd, pnm, pnv = _adamw(pw, pg, pm, pv, "adamw_small")
    for nme, dl, nm, nv in zip(tiny, _unpack_rows(pd, poffs, shapes), _unpack_rows(pnm, poffs, shapes),
                               _unpack_rows(pnv, poffs, shapes)):
        delta[nme], new_m[nme], new_v[nme] = dl, nm, nv

    return (loss, grad_x, *[grads[nme].reshape(weights[nme].shape) for nme in names],
            *[delta[nme] for nme in names], *[new_m[nme] for nme in names], *[new_v[nme] for nme in names])
```
